```python
import math
import jax, jax.numpy as jnp
from jax import lax
import numpy as np

D_MODEL = 1024
BATCH = 8
SEQ = 8192
DEPTH = 2

D_FF = ((8 * D_MODEL // 3 + 127) // 128) * 128
D_CONV = D_MODEL
CONV_WIDTH = 31
CONV_GROUPS = 8
D_RNN = ((4 * D_MODEL // 3 + 127) // 128) * 128
RNN_BLOCKS = 16
RNN_BLOCK = D_RNN // RNN_BLOCKS
RNN_CONV_WIDTH = 4
RG_LRU_C = 8.0
LN_EPS = 1e-5
D_IN = 2 * D_CONV + 2 * D_RNN + 2 * D_MODEL
SPLITS = [D_CONV, 2 * D_CONV, 2 * D_CONV + D_RNN, 2 * D_CONV + 2 * D_RNN, 2 * D_CONV + 2 * D_RNN + D_MODEL]
DEEPNORM_ALPHA = (2 * DEPTH) ** 0.25
DEEPNORM_BETA = (8 * DEPTH) ** -0.25

kernel_name = "hybrid_conformer_conv_rglru_deepnorm"


def layer_norm(x, g, b):
    xf = x.astype(jnp.float32)
    mu = jnp.mean(xf, axis=-1, keepdims=True)
    xc = xf - mu
    var = jnp.mean(xc * xc, axis=-1, keepdims=True)
    y = xc * lax.rsqrt(var + LN_EPS) * g.astype(jnp.float32) + b.astype(jnp.float32)
    return y.astype(x.dtype)


def group_norm(x, g, b, groups):
    B, S, C = x.shape
    xf = x.astype(jnp.float32).reshape(B, S, groups, C // groups)
    mu = jnp.mean(xf, axis=-1, keepdims=True)
    xc = xf - mu
    var = jnp.mean(xc * xc, axis=-1, keepdims=True)
    y = (xc * lax.rsqrt(var + LN_EPS)).reshape(B, S, C)
    y = y * g.astype(jnp.float32) + b.astype(jnp.float32)
    return y.astype(x.dtype)


def swiglu_ffn(x, w_gu, w_down):
    gate, up = jnp.split(x @ w_gu, 2, axis=-1)
    return (jax.nn.silu(gate) * up) @ w_down


def causal_depthwise_conv(x, w, b):
    K, C = w.shape
    y = lax.conv_general_dilated(
        x, w[:, None, :].astype(x.dtype),
        window_strides=(1,), padding=[(K - 1, 0)],
        dimension_numbers=("NWC", "WIO", "NWC"),
        feature_group_count=C)
    return y + b


def block_diag_linear(x, w, b):
    B, S, _ = x.shape
    xh = x.reshape(B, S, RNN_BLOCKS, RNN_BLOCK)
    y = jnp.einsum("bshi,hij->bshj", xh, w)
    return y.reshape(B, S, D_RNN) + b


def rg_lru(x, w_a, b_a, w_x, b_x, lam):
    r = jax.nn.sigmoid(block_diag_linear(x, w_a, b_a)).astype(jnp.float32)
    i = jax.nn.sigmoid(block_diag_linear(x, w_x, b_x)).astype(jnp.float32)
    log_a = -RG_LRU_C * r * jax.nn.softplus(-lam.astype(jnp.float32))
    a = jnp.exp(log_a)
    mult = jnp.sqrt(-jnp.expm1(2.0 * log_a))
    u = mult * (i * x.astype(jnp.float32))

    def combine(left, right):
        a1, b1 = left
        a2, b2 = right
        return a1 * a2, a2 * b1 + b2

    _, h = lax.associative_scan(combine, (a, u), axis=1)
    return h.astype(x.dtype)


def hybrid_mixer(x, w_in, b_in, dw_w, dw_b, gn_g, gn_b, conv_w_proj,
                 rc_w, rc_b, w_a, b_a, w_x, b_x, lam, rnn_w_proj, w_out):
    u = x @ w_in + b_in
    c_val, c_gate, r_x, r_gate, g_c, g_r = jnp.split(u, SPLITS, axis=-1)
    c = c_val * jax.nn.sigmoid(c_gate)
    c = causal_depthwise_conv(c, dw_w, dw_b)
    c = jax.nn.silu(group_norm(c, gn_g, gn_b, CONV_GROUPS))
    y_conv = c @ conv_w_proj
    r = causal_depthwise_conv(r_x, rc_w, rc_b)
    h = rg_lru(r, w_a, b_a, w_x, b_x, lam)
    y_rnn = (h * jax.nn.gelu(r_gate)) @ rnn_w_proj
    m = jax.nn.sigmoid(g_c) * y_conv + jax.nn.sigmoid(g_r) * y_rnn
    return m @ w_out


def _fwd_setup_inputs(seed: int = 0) -> dict:
    key = jax.random.key(seed)
    ks = jax.random.split(key, 32)
    L = DEPTH
    f32 = jnp.float32

    def nrm(k, shape, scale):
        return jax.random.normal(k, shape, f32) * scale

    def gain(k, shape):
        return 1.0 + nrm(k, shape, 0.02)

    u = jax.random.uniform(ks[18], (L, D_RNN), f32, minval=0.9, maxval=0.999)
    log_a = jnp.log(u) / RG_LRU_C
    lam = log_a - jnp.log(-jnp.expm1(log_a))

    return {
        "x": nrm(ks[0], (BATCH, SEQ, D_MODEL), 1.0),
        "ffn1_w_gu": nrm(ks[1], (L, D_MODEL, 2 * D_FF), D_MODEL ** -0.5),
        "ffn1_w_down": nrm(ks[2], (L, D_FF, D_MODEL), DEEPNORM_BETA * D_FF ** -0.5),
        "ln1_g": gain(ks[3], (L, D_MODEL)),
        "ln1_b": nrm(ks[4], (L, D_MODEL), 0.02),
        "mix_w_in": nrm(ks[5], (L, D_MODEL, D_IN), D_MODEL ** -0.5),
        "mix_b_in": nrm(ks[6], (L, D_IN), 0.02),
        "conv_dw_w": nrm(ks[7], (L, CONV_WIDTH, D_CONV), CONV_WIDTH ** -0.5),
        "conv_dw_b": nrm(ks[8], (L, D_CONV), 0.02),
        "conv_gn_g": gain(ks[9], (L, D_CONV)),
        "conv_gn_b": nrm(ks[10], (L, D_CONV), 0.02),
        "conv_w_proj": nrm(ks[11], (L, D_CONV, D_MODEL), DEEPNORM_BETA * D_CONV ** -0.5),
        "rnn_conv_w": nrm(ks[12], (L, RNN_CONV_WIDTH, D_RNN), RNN_CONV_WIDTH ** -0.5),
        "rnn_conv_b": nrm(ks[13], (L, D_RNN), 0.02),
        "rnn_w_a": nrm(ks[14], (L, RNN_BLOCKS, RNN_BLOCK, RNN_BLOCK), RNN_BLOCK ** -0.5),
        "rnn_b_a": nrm(ks[15], (L, D_RNN), 0.02),
        "rnn_w_x": nrm(ks[16], (L, RNN_BLOCKS, RNN_BLOCK, RNN_BLOCK), RNN_BLOCK ** -0.5),
        "rnn_b_x": nrm(ks[17], (L, D_RNN), 0.02),
        "rnn_lambda": lam,
        "rnn_w_proj": nrm(ks[19], (L, D_RNN, D_MODEL), DEEPNORM_BETA * D_RNN ** -0.5),
        "mix_w_out": nrm(ks[20], (L, D_MODEL, D_MODEL), DEEPNORM_BETA * D_MODEL ** -0.5),
        "ln2_g": gain(ks[21], (L, D_MODEL)),
        "ln2_b": nrm(ks[22], (L, D_MODEL), 0.02),
        "ffn2_w_gu": nrm(ks[23], (L, D_MODEL, 2 * D_FF), D_MODEL ** -0.5),
        "ffn2_w_down": nrm(ks[24], (L, D_FF, D_MODEL), DEEPNORM_BETA * D_FF ** -0.5),
        "ln3_g": gain(ks[25], (L, D_MODEL)),
        "ln3_b": nrm(ks[26], (L, D_MODEL), 0.02),
    }


def _fwd_reference(x, ffn1_w_gu, ffn1_w_down, ln1_g, ln1_b, mix_w_in, mix_b_in,
              conv_dw_w, conv_dw_b, conv_gn_g, conv_gn_b, conv_w_proj,
              rnn_conv_w, rnn_conv_b, rnn_w_a, rnn_b_a, rnn_w_x, rnn_b_x,
              rnn_lambda, rnn_w_proj, mix_w_out, ln2_g, ln2_b,
              ffn2_w_gu, ffn2_w_down, ln3_g, ln3_b):
    for l in range(DEPTH):
        x = layer_norm(DEEPNORM_ALPHA * x + 0.5 * swiglu_ffn(x, ffn1_w_gu[l], ffn1_w_down[l]),
                       ln1_g[l], ln1_b[l])
        mix = hybrid_mixer(x, mix_w_in[l], mix_b_in[l], conv_dw_w[l], conv_dw_b[l],
                           conv_gn_g[l], conv_gn_b[l], conv_w_proj[l],
                           rnn_conv_w[l], rnn_conv_b[l], rnn_w_a[l], rnn_b_a[l],
                           rnn_w_x[l], rnn_b_x[l], rnn_lambda[l], rnn_w_proj[l], mix_w_out[l])
        x = layer_norm(DEEPNORM_ALPHA * x + mix, ln2_g[l], ln2_b[l])
        x = layer_norm(DEEPNORM_ALPHA * x + 0.5 * swiglu_ffn(x, ffn2_w_gu[l], ffn2_w_down[l]),
                       ln3_g[l], ln3_b[l])
    return x


import jax as _jax
import jax.numpy as _jnp

TWIN_FORMAT = 'train_step'
FWD_PARAMS = ['x', 'ffn1_w_gu', 'ffn1_w_down', 'ln1_g', 'ln1_b', 'mix_w_in', 'mix_b_in', 'conv_dw_w', 'conv_dw_b', 'conv_gn_g', 'conv_gn_b', 'conv_w_proj', 'rnn_conv_w', 'rnn_conv_b', 'rnn_w_a', 'rnn_b_a', 'rnn_w_x', 'rnn_b_x', 'rnn_lambda', 'rnn_w_proj', 'mix_w_out', 'ln2_g', 'ln2_b', 'ffn2_w_gu', 'ffn2_w_down', 'ln3_g', 'ln3_b']
TWIN_WEIGHTS = ['ffn1_w_gu', 'ffn1_w_down', 'ln1_g', 'ln1_b', 'mix_w_in', 'mix_b_in', 'conv_dw_w', 'conv_dw_b', 'conv_gn_g', 'conv_gn_b', 'conv_w_proj', 'rnn_conv_w', 'rnn_conv_b', 'rnn_w_a', 'rnn_b_a', 'rnn_w_x', 'rnn_b_x', 'rnn_lambda', 'rnn_w_proj', 'mix_w_out', 'ln2_g', 'ln2_b', 'ffn2_w_gu', 'ffn2_w_down', 'ln3_g', 'ln3_b']
TWIN_DIFF_INPUT = 'x'
TWIN_INPUTS = ['x', 'ffn1_w_gu', 'ffn1_w_down', 'ln1_g', 'ln1_b', 'mix_w_in', 'mix_b_in', 'conv_dw_w', 'conv_dw_b', 'conv_gn_g', 'conv_gn_b', 'conv_w_proj', 'rnn_conv_w', 'rnn_conv_b', 'rnn_w_a', 'rnn_b_a', 'rnn_w_x', 'rnn_b_x', 'rnn_lambda', 'rnn_w_proj', 'mix_w_out', 'ln2_g', 'ln2_b', 'ffn2_w_gu', 'ffn2_w_down', 'ln3_g', 'ln3_b', 'loss_target', 'm_ffn1_w_gu', 'm_ffn1_w_down', 'm_ln1_g', 'm_ln1_b', 'm_mix_w_in', 'm_mix_b_in', 'm_conv_dw_w', 'm_conv_dw_b', 'm_conv_gn_g', 'm_conv_gn_b', 'm_conv_w_proj', 'm_rnn_conv_w', 'm_rnn_conv_b', 'm_rnn_w_a', 'm_rnn_b_a', 'm_rnn_w_x', 'm_rnn_b_x', 'm_rnn_lambda', 'm_rnn_w_proj', 'm_mix_w_out', 'm_ln2_g', 'm_ln2_b', 'm_ffn2_w_gu', 'm_ffn2_w_down', 'm_ln3_g', 'm_ln3_b', 'v_ffn1_w_gu', 'v_ffn1_w_down', 'v_ln1_g', 'v_ln1_b', 'v_mix_w_in', 'v_mix_b_in', 'v_conv_dw_w', 'v_conv_dw_b', 'v_conv_gn_g', 'v_conv_gn_b', 'v_conv_w_proj', 'v_rnn_conv_w', 'v_rnn_conv_b', 'v_rnn_w_a', 'v_rnn_b_a', 'v_rnn_w_x', 'v_rnn_b_x', 'v_rnn_lambda', 'v_rnn_w_proj', 'v_mix_w_out', 'v_ln2_g', 'v_ln2_b', 'v_ffn2_w_gu', 'v_ffn2_w_down', 'v_ln3_g', 'v_ln3_b']
TWIN_OUTPUTS = ['loss', 'grad_x', 'grad_ffn1_w_gu', 'grad_ffn1_w_down', 'grad_ln1_g', 'grad_ln1_b', 'grad_mix_w_in', 'grad_mix_b_in', 'grad_conv_dw_w', 'grad_conv_dw_b', 'grad_conv_gn_g', 'grad_conv_gn_b', 'grad_conv_w_proj', 'grad_rnn_conv_w', 'grad_rnn_conv_b', 'grad_rnn_w_a', 'grad_rnn_b_a', 'grad_rnn_w_x', 'grad_rnn_b_x', 'grad_rnn_lambda', 'grad_rnn_w_proj', 'grad_mix_w_out', 'grad_ln2_g', 'grad_ln2_b', 'grad_ffn2_w_gu', 'grad_ffn2_w_down', 'grad_ln3_g', 'grad_ln3_b', 'delta_ffn1_w_gu', 'delta_ffn1_w_down', 'delta_ln1_g', 'delta_ln1_b', 'delta_mix_w_in', 'delta_mix_b_in', 'delta_conv_dw_w', 'delta_conv_dw_b', 'delta_conv_gn_g', 'delta_conv_gn_b', 'delta_conv_w_proj', 'delta_rnn_conv_w', 'delta_rnn_conv_b', 'delta_rnn_w_a', 'delta_rnn_b_a', 'delta_rnn_w_x', 'delta_rnn_b_x', 'delta_rnn_lambda', 'delta_rnn_w_proj', 'delta_mix_w_out', 'delta_ln2_g', 'delta_ln2_b', 'delta_ffn2_w_gu', 'delta_ffn2_w_down', 'delta_ln3_g', 'delta_ln3_b', 'new_m_ffn1_w_gu', 'new_m_ffn1_w_down', 'new_m_ln1_g', 'new_m_ln1_b', 'new_m_mix_w_in', 'new_m_mix_b_in', 'new_m_conv_dw_w', 'new_m_conv_dw_b', 'new_m_conv_gn_g', 'new_m_conv_gn_b', 'new_m_conv_w_proj', 'new_m_rnn_conv_w', 'new_m_rnn_conv_b', 'new_m_rnn_w_a', 'new_m_rnn_b_a', 'new_m_rnn_w_x', 'new_m_rnn_b_x', 'new_m_rnn_lambda', 'new_m_rnn_w_proj', 'new_m_mix_w_out', 'new_m_ln2_g', 'new_m_ln2_b', 'new_m_ffn2_w_gu', 'new_m_ffn2_w_down', 'new_m_ln3_g', 'new_m_ln3_b', 'new_v_ffn1_w_gu', 'new_v_ffn1_w_down', 'new_v_ln1_g', 'new_v_ln1_b', 'new_v_mix_w_in', 'new_v_mix_b_in', 'new_v_conv_dw_w', 'new_v_conv_dw_b', 'new_v_conv_gn_g', 'new_v_conv_gn_b', 'new_v_conv_w_proj', 'new_v_rnn_conv_w', 'new_v_rnn_conv_b', 'new_v_rnn_w_a', 'new_v_rnn_b_a', 'new_v_rnn_w_x', 'new_v_rnn_b_x', 'new_v_rnn_lambda', 'new_v_rnn_w_proj', 'new_v_mix_w_out', 'new_v_ln2_g', 'new_v_ln2_b', 'new_v_ffn2_w_gu', 'new_v_ffn2_w_down', 'new_v_ln3_g', 'new_v_ln3_b']
TWIN_LEAF_KINDS = {'loss': 'loss', 'grad_x': 'grad_x', 'grad_ffn1_w_gu': 'grad_w', 'grad_ffn1_w_down': 'grad_w', 'grad_ln1_g': 'grad_w', 'grad_ln1_b': 'grad_w', 'grad_mix_w_in': 'grad_w', 'grad_mix_b_in': 'grad_w', 'grad_conv_dw_w': 'grad_w', 'grad_conv_dw_b': 'grad_w', 'grad_conv_gn_g': 'grad_w', 'grad_conv_gn_b': 'grad_w', 'grad_conv_w_proj': 'grad_w', 'grad_rnn_conv_w': 'grad_w', 'grad_rnn_conv_b': 'grad_w', 'grad_rnn_w_a': 'grad_w', 'grad_rnn_b_a': 'grad_w', 'grad_rnn_w_x': 'grad_w', 'grad_rnn_b_x': 'grad_w', 'grad_rnn_lambda': 'grad_w', 'grad_rnn_w_proj': 'grad_w', 'grad_mix_w_out': 'grad_w', 'grad_ln2_g': 'grad_w', 'grad_ln2_b': 'grad_w', 'grad_ffn2_w_gu': 'grad_w', 'grad_ffn2_w_down': 'grad_w', 'grad_ln3_g': 'grad_w', 'grad_ln3_b': 'grad_w', 'delta_ffn1_w_gu': 'delta_w', 'delta_ffn1_w_down': 'delta_w', 'delta_ln1_g': 'delta_w', 'delta_ln1_b': 'delta_w', 'delta_mix_w_in': 'delta_w', 'delta_mix_b_in': 'delta_w', 'delta_conv_dw_w': 'delta_w', 'delta_conv_dw_b': 'delta_w', 'delta_conv_gn_g': 'delta_w', 'delta_conv_gn_b': 'delta_w', 'delta_conv_w_proj': 'delta_w', 'delta_rnn_conv_w': 'delta_w', 'delta_rnn_conv_b': 'delta_w', 'delta_rnn_w_a': 'delta_w', 'delta_rnn_b_a': 'delta_w', 'delta_rnn_w_x': 'delta_w', 'delta_rnn_b_x': 'delta_w', 'delta_rnn_lambda': 'delta_w', 'delta_rnn_w_proj': 'delta_w', 'delta_mix_w_out': 'delta_w', 'delta_ln2_g': 'delta_w', 'delta_ln2_b': 'delta_w', 'delta_ffn2_w_gu': 'delta_w', 'delta_ffn2_w_down': 'delta_w', 'delta_ln3_g': 'delta_w', 'delta_ln3_b': 'delta_w', 'new_m_ffn1_w_gu': 'new_m', 'new_m_ffn1_w_down': 'new_m', 'new_m_ln1_g': 'new_m', 'new_m_ln1_b': 'new_m', 'new_m_mix_w_in': 'new_m', 'new_m_mix_b_in': 'new_m', 'new_m_conv_dw_w': 'new_m', 'new_m_conv_dw_b': 'new_m', 'new_m_conv_gn_g': 'new_m', 'new_m_conv_gn_b': 'new_m', 'new_m_conv_w_proj': 'new_m', 'new_m_rnn_conv_w': 'new_m', 'new_m_rnn_conv_b': 'new_m', 'new_m_rnn_w_a': 'new_m', 'new_m_rnn_b_a': 'new_m', 'new_m_rnn_w_x': 'new_m', 'new_m_rnn_b_x': 'new_m', 'new_m_rnn_lambda': 'new_m', 'new_m_rnn_w_proj': 'new_m', 'new_m_mix_w_out': 'new_m', 'new_m_ln2_g': 'new_m', 'new_m_ln2_b': 'new_m', 'new_m_ffn2_w_gu': 'new_m', 'new_m_ffn2_w_down': 'new_m', 'new_m_ln3_g': 'new_m', 'new_m_ln3_b': 'new_m', 'new_v_ffn1_w_gu': 'new_v', 'new_v_ffn1_w_down': 'new_v', 'new_v_ln1_g': 'new_v', 'new_v_ln1_b': 'new_v', 'new_v_mix_w_in': 'new_v', 'new_v_mix_b_in': 'new_v', 'new_v_conv_dw_w': 'new_v', 'new_v_conv_dw_b': 'new_v', 'new_v_conv_gn_g': 'new_v', 'new_v_conv_gn_b': 'new_v', 'new_v_conv_w_proj': 'new_v', 'new_v_rnn_conv_w': 'new_v', 'new_v_rnn_conv_b': 'new_v', 'new_v_rnn_w_a': 'new_v', 'new_v_rnn_b_a': 'new_v', 'new_v_rnn_w_x': 'new_v', 'new_v_rnn_b_x': 'new_v', 'new_v_rnn_lambda': 'new_v', 'new_v_rnn_w_proj': 'new_v', 'new_v_mix_w_out': 'new_v', 'new_v_ln2_g': 'new_v', 'new_v_ln2_b': 'new_v', 'new_v_ffn2_w_gu': 'new_v', 'new_v_ffn2_w_down': 'new_v', 'new_v_ln3_g': 'new_v', 'new_v_ln3_b': 'new_v'}


def _forward(args):
    return _fwd_reference(*[args[k] for k in FWD_PARAMS])


def _output_shape():
    def fwd():
        inp = _fwd_setup_inputs(0)
        return _fwd_reference(*[inp[k] for k in FWD_PARAMS])
    out = _jax.eval_shape(fwd)
    return out.shape, out.dtype

N_MICROBATCH = 1
ADAM_LR = 0.001
ADAM_B1 = 0.9
ADAM_B2 = 0.999
ADAM_EPS = 1e-08
ADAM_WD = 0.01
ADAM_STEP = 10
PER_EXAMPLE_BATCH_AXIS = {'x': 0, 'loss_target': 0}
SHARED_INPUTS = []
_WEIGHT_DTYPES = {'ffn1_w_gu': _jnp.float32, 'ffn1_w_down': _jnp.float32, 'ln1_g': _jnp.float32, 'ln1_b': _jnp.float32, 'mix_w_in': _jnp.float32, 'mix_b_in': _jnp.float32, 'conv_dw_w': _jnp.float32, 'conv_dw_b': _jnp.float32, 'conv_gn_g': _jnp.float32, 'conv_gn_b': _jnp.float32, 'conv_w_proj': _jnp.float32, 'rnn_conv_w': _jnp.float32, 'rnn_conv_b': _jnp.float32, 'rnn_w_a': _jnp.float32, 'rnn_b_a': _jnp.float32, 'rnn_w_x': _jnp.float32, 'rnn_b_x': _jnp.float32, 'rnn_lambda': _jnp.float32, 'rnn_w_proj': _jnp.float32, 'mix_w_out': _jnp.float32, 'ln2_g': _jnp.float32, 'ln2_b': _jnp.float32, 'ffn2_w_gu': _jnp.float32, 'ffn2_w_down': _jnp.float32, 'ln3_g': _jnp.float32, 'ln3_b': _jnp.float32}
MOMENT_SCALE = {'ffn1_w_gu': 1.628456e-02, 'ffn1_w_down': 5.316007e-02, 'ln1_g': 2.252432e+00, 'ln1_b': 9.694640e-01, 'mix_w_in': 9.401035e-03, 'mix_b_in': 7.916572e-02, 'conv_dw_w': 1.530520e-02, 'conv_dw_b': 6.983930e-02, 'conv_gn_g': 2.866952e-02, 'conv_gn_b': 4.208244e-02, 'conv_w_proj': 3.829711e-02, 'rnn_conv_w': 1.163269e-02, 'rnn_conv_b': 1.608526e-01, 'rnn_w_a': 4.547249e-03, 'rnn_b_a': 3.105387e-03, 'rnn_w_x': 8.310931e-03, 'rnn_b_x': 4.197083e-03, 'rnn_lambda': 5.486503e-03, 'rnn_w_proj': 2.939889e-02, 'mix_w_out': 4.927900e-02, 'ln2_g': 2.264107e+00, 'ln2_b': 9.511273e-01, 'ffn2_w_gu': 1.623157e-02, 'ffn2_w_down': 5.310554e-02, 'ln3_g': 4.535835e+01, 'ln3_b': 2.393397e+00}


def _to_microbatches(a, axis):
    t = _jnp.moveaxis(a, axis, 0)
    t = t.reshape((N_MICROBATCH, t.shape[0] // N_MICROBATCH) + t.shape[1:])
    return _jnp.moveaxis(t, 1, axis + 1)


def setup_inputs(seed: int = 0) -> dict:
    inp = _fwd_setup_inputs(seed)
    key = _jax.random.fold_in(_jax.random.key(seed), 7919)
    shape, _ = _output_shape()
    out = dict(inp)
    out["loss_target"] = _jax.random.normal(_jax.random.fold_in(key, 0), shape, _jnp.float32)
    for i, name in enumerate(TWIN_WEIGHTS):
        w = inp[name].astype(_jnp.float32)
        if MOMENT_SCALE is None:
            s = _jnp.sqrt(_jnp.mean(_jnp.square(w)) + 1e-30)
        else:
            s = MOMENT_SCALE[name]
        km, kv = _jax.random.split(_jax.random.fold_in(key, i + 1))
        out[name] = w
        out["m_" + name] = s * _jax.random.normal(km, w.shape, _jnp.float32)
        out["v_" + name] = (s * s) * _jax.random.uniform(kv, w.shape, _jnp.float32, 0.5, 1.5)
    if N_MICROBATCH > 1:
        for name, axis in PER_EXAMPLE_BATCH_AXIS.items():
            out[name] = _to_microbatches(out[name], axis)
    return {'x': out['x'], 'ffn1_w_gu': out['ffn1_w_gu'], 'ffn1_w_down': out['ffn1_w_down'], 'ln1_g': out['ln1_g'], 'ln1_b': out['ln1_b'], 'mix_w_in': out['mix_w_in'], 'mix_b_in': out['mix_b_in'], 'conv_dw_w': out['conv_dw_w'], 'conv_dw_b': out['conv_dw_b'], 'conv_gn_g': out['conv_gn_g'], 'conv_gn_b': out['conv_gn_b'], 'conv_w_proj': out['conv_w_proj'], 'rnn_conv_w': out['rnn_conv_w'], 'rnn_conv_b': out['rnn_conv_b'], 'rnn_w_a': out['rnn_w_a'], 'rnn_b_a': out['rnn_b_a'], 'rnn_w_x': out['rnn_w_x'], 'rnn_b_x': out['rnn_b_x'], 'rnn_lambda': out['rnn_lambda'], 'rnn_w_proj': out['rnn_w_proj'], 'mix_w_out': out['mix_w_out'], 'ln2_g': out['ln2_g'], 'ln2_b': out['ln2_b'], 'ffn2_w_gu': out['ffn2_w_gu'], 'ffn2_w_down': out['ffn2_w_down'], 'ln3_g': out['ln3_g'], 'ln3_b': out['ln3_b'], 'loss_target': out['loss_target'], 'm_ffn1_w_gu': out['m_ffn1_w_gu'], 'm_ffn1_w_down': out['m_ffn1_w_down'], 'm_ln1_g': out['m_ln1_g'], 'm_ln1_b': out['m_ln1_b'], 'm_mix_w_in': out['m_mix_w_in'], 'm_mix_b_in': out['m_mix_b_in'], 'm_conv_dw_w': out['m_conv_dw_w'], 'm_conv_dw_b': out['m_conv_dw_b'], 'm_conv_gn_g': out['m_conv_gn_g'], 'm_conv_gn_b': out['m_conv_gn_b'], 'm_conv_w_proj': out['m_conv_w_proj'], 'm_rnn_conv_w': out['m_rnn_conv_w'], 'm_rnn_conv_b': out['m_rnn_conv_b'], 'm_rnn_w_a': out['m_rnn_w_a'], 'm_rnn_b_a': out['m_rnn_b_a'], 'm_rnn_w_x': out['m_rnn_w_x'], 'm_rnn_b_x': out['m_rnn_b_x'], 'm_rnn_lambda': out['m_rnn_lambda'], 'm_rnn_w_proj': out['m_rnn_w_proj'], 'm_mix_w_out': out['m_mix_w_out'], 'm_ln2_g': out['m_ln2_g'], 'm_ln2_b': out['m_ln2_b'], 'm_ffn2_w_gu': out['m_ffn2_w_gu'], 'm_ffn2_w_down': out['m_ffn2_w_down'], 'm_ln3_g': out['m_ln3_g'], 'm_ln3_b': out['m_ln3_b'], 'v_ffn1_w_gu': out['v_ffn1_w_gu'], 'v_ffn1_w_down': out['v_ffn1_w_down'], 'v_ln1_g': out['v_ln1_g'], 'v_ln1_b': out['v_ln1_b'], 'v_mix_w_in': out['v_mix_w_in'], 'v_mix_b_in': out['v_mix_b_in'], 'v_conv_dw_w': out['v_conv_dw_w'], 'v_conv_dw_b': out['v_conv_dw_b'], 'v_conv_gn_g': out['v_conv_gn_g'], 'v_conv_gn_b': out['v_conv_gn_b'], 'v_conv_w_proj': out['v_conv_w_proj'], 'v_rnn_conv_w': out['v_rnn_conv_w'], 'v_rnn_conv_b': out['v_rnn_conv_b'], 'v_rnn_w_a': out['v_rnn_w_a'], 'v_rnn_b_a': out['v_rnn_b_a'], 'v_rnn_w_x': out['v_rnn_w_x'], 'v_rnn_b_x': out['v_rnn_b_x'], 'v_rnn_lambda': out['v_rnn_lambda'], 'v_rnn_w_proj': out['v_rnn_w_proj'], 'v_mix_w_out': out['v_mix_w_out'], 'v_ln2_g': out['v_ln2_g'], 'v_ln2_b': out['v_ln2_b'], 'v_ffn2_w_gu': out['v_ffn2_w_gu'], 'v_ffn2_w_down': out['v_ffn2_w_down'], 'v_ln3_g': out['v_ln3_g'], 'v_ln3_b': out['v_ln3_b']}


def _loss(weights, diff, rest, loss_target):
    with _jax.named_scope("forward"):
        args = {**rest, TWIN_DIFF_INPUT: diff, **{k: w.astype(_WEIGHT_DTYPES[k]) for k, w in weights.items()}}
        y = _forward(args)
    with _jax.named_scope("loss_head"):
        err = _jnp.square(y.astype(_jnp.float32) - loss_target)
        return 0.5 * _jnp.sum(_jnp.mean(err, axis=-1)) if err.ndim else 0.5 * err


def _adamw(w, g, m, v):
    m = ADAM_B1 * m + (1.0 - ADAM_B1) * g
    v = ADAM_B2 * v + (1.0 - ADAM_B2) * _jnp.square(g)
    m_hat = m / (1.0 - ADAM_B1 ** ADAM_STEP)
    v_hat = v / (1.0 - ADAM_B2 ** ADAM_STEP)
    delta = -ADAM_LR * (m_hat / (_jnp.sqrt(v_hat) + ADAM_EPS) + ADAM_WD * w)
    return delta, m, v


def reference(x, ffn1_w_gu, ffn1_w_down, ln1_g, ln1_b, mix_w_in, mix_b_in, conv_dw_w, conv_dw_b, conv_gn_g, conv_gn_b, conv_w_proj, rnn_conv_w, rnn_conv_b, rnn_w_a, rnn_b_a, rnn_w_x, rnn_b_x, rnn_lambda, rnn_w_proj, mix_w_out, ln2_g, ln2_b, ffn2_w_gu, ffn2_w_down, ln3_g, ln3_b, loss_target, m_ffn1_w_gu, m_ffn1_w_down, m_ln1_g, m_ln1_b, m_mix_w_in, m_mix_b_in, m_conv_dw_w, m_conv_dw_b, m_conv_gn_g, m_conv_gn_b, m_conv_w_proj, m_rnn_conv_w, m_rnn_conv_b, m_rnn_w_a, m_rnn_b_a, m_rnn_w_x, m_rnn_b_x, m_rnn_lambda, m_rnn_w_proj, m_mix_w_out, m_ln2_g, m_ln2_b, m_ffn2_w_gu, m_ffn2_w_down, m_ln3_g, m_ln3_b, v_ffn1_w_gu, v_ffn1_w_down, v_ln1_g, v_ln1_b, v_mix_w_in, v_mix_b_in, v_conv_dw_w, v_conv_dw_b, v_conv_gn_g, v_conv_gn_b, v_conv_w_proj, v_rnn_conv_w, v_rnn_conv_b, v_rnn_w_a, v_rnn_b_a, v_rnn_w_x, v_rnn_b_x, v_rnn_lambda, v_rnn_w_proj, v_mix_w_out, v_ln2_g, v_ln2_b, v_ffn2_w_gu, v_ffn2_w_down, v_ln3_g, v_ln3_b):
    given = dict(x=x, ffn1_w_gu=ffn1_w_gu, ffn1_w_down=ffn1_w_down, ln1_g=ln1_g, ln1_b=ln1_b, mix_w_in=mix_w_in, mix_b_in=mix_b_in, conv_dw_w=conv_dw_w, conv_dw_b=conv_dw_b, conv_gn_g=conv_gn_g, conv_gn_b=conv_gn_b, conv_w_proj=conv_w_proj, rnn_conv_w=rnn_conv_w, rnn_conv_b=rnn_conv_b, rnn_w_a=rnn_w_a, rnn_b_a=rnn_b_a, rnn_w_x=rnn_w_x, rnn_b_x=rnn_b_x, rnn_lambda=rnn_lambda, rnn_w_proj=rnn_w_proj, mix_w_out=mix_w_out, ln2_g=ln2_g, ln2_b=ln2_b, ffn2_w_gu=ffn2_w_gu, ffn2_w_down=ffn2_w_down, ln3_g=ln3_g, ln3_b=ln3_b, loss_target=loss_target, m_ffn1_w_gu=m_ffn1_w_gu, m_ffn1_w_down=m_ffn1_w_down, m_ln1_g=m_ln1_g, m_ln1_b=m_ln1_b, m_mix_w_in=m_mix_w_in, m_mix_b_in=m_mix_b_in, m_conv_dw_w=m_conv_dw_w, m_conv_dw_b=m_conv_dw_b, m_conv_gn_g=m_conv_gn_g, m_conv_gn_b=m_conv_gn_b, m_conv_w_proj=m_conv_w_proj, m_rnn_conv_w=m_rnn_conv_w, m_rnn_conv_b=m_rnn_conv_b, m_rnn_w_a=m_rnn_w_a, m_rnn_b_a=m_rnn_b_a, m_rnn_w_x=m_rnn_w_x, m_rnn_b_x=m_rnn_b_x, m_rnn_lambda=m_rnn_lambda, m_rnn_w_proj=m_rnn_w_proj, m_mix_w_out=m_mix_w_out, m_ln2_g=m_ln2_g, m_ln2_b=m_ln2_b, m_ffn2_w_gu=m_ffn2_w_gu, m_ffn2_w_down=m_ffn2_w_down, m_ln3_g=m_ln3_g, m_ln3_b=m_ln3_b, v_ffn1_w_gu=v_ffn1_w_gu, v_ffn1_w_down=v_ffn1_w_down, v_ln1_g=v_ln1_g, v_ln1_b=v_ln1_b, v_mix_w_in=v_mix_w_in, v_mix_b_in=v_mix_b_in, v_conv_dw_w=v_conv_dw_w, v_conv_dw_b=v_conv_dw_b, v_conv_gn_g=v_conv_gn_g, v_conv_gn_b=v_conv_gn_b, v_conv_w_proj=v_conv_w_proj, v_rnn_conv_w=v_rnn_conv_w, v_rnn_conv_b=v_rnn_conv_b, v_rnn_w_a=v_rnn_w_a, v_rnn_b_a=v_rnn_b_a, v_rnn_w_x=v_rnn_w_x, v_rnn_b_x=v_rnn_b_x, v_rnn_lambda=v_rnn_lambda, v_rnn_w_proj=v_rnn_w_proj, v_mix_w_out=v_mix_w_out, v_ln2_g=v_ln2_g, v_ln2_b=v_ln2_b, v_ffn2_w_gu=v_ffn2_w_gu, v_ffn2_w_down=v_ffn2_w_down, v_ln3_g=v_ln3_g, v_ln3_b=v_ln3_b)
    weights = {n: given[n] for n in TWIN_WEIGHTS}
    shared = {n: given[n] for n in SHARED_INPUTS}
    per_example = {n: given[n] for n in ['x']}
    grad_fn = _jax.value_and_grad(_loss, argnums=(0, 1))

    def one_microbatch(ex, loss_target):
        ex = dict(ex)
        diff = ex.pop(TWIN_DIFF_INPUT)
        return grad_fn(weights, diff, {**shared, **ex}, loss_target)

    if N_MICROBATCH == 1:
        loss, (grad_w, grad_x) = one_microbatch(per_example, given["loss_target"])
    else:
        def body(carry, xs):
            loss_sum, grad_sum = carry
            l_k, (gw_k, gx_k) = one_microbatch(xs[0], xs[1])
            with _jax.named_scope("update"):
                return (loss_sum + l_k, _jax.tree.map(_jnp.add, grad_sum, gw_k)), gx_k

        init = (_jnp.zeros((), _jnp.float32), _jax.tree.map(_jnp.zeros_like, weights))
        (loss, grad_w), grad_x = _jax.lax.scan(body, init, (per_example, given["loss_target"]))
    with _jax.named_scope("update"):
        delta_w, new_m, new_v = {}, {}, {}
        for n in TWIN_WEIGHTS:
            delta_w[n], new_m[n], new_v[n] = _adamw(weights[n], grad_w[n], given["m_" + n], given["v_" + n])
    return (loss, grad_x, *[grad_w[n] for n in TWIN_WEIGHTS], *[delta_w[n] for n in TWIN_WEIGHTS],
            *[new_m[n] for n in TWIN_WEIGHTS], *[new_v[n] for n in TWIN_WEIGHTS])
```

```python
import functools

import jax
import jax.numpy as jnp
from jax import lax
from jax.experimental import pallas as pl
from jax.experimental.pallas import tpu as pltpu

F32 = jnp.float32
BF16 = jnp.bfloat16
MESH = pl.DeviceIdType.MESH

LN_EPS = 1e-5
CONV_GROUPS = 8
RNN_BLOCKS = 16
RG_LRU_C = 8.0
ADAM_LR = 0.001
ADAM_B1 = 0.9
ADAM_B2 = 0.999
ADAM_EPS = 1e-08
ADAM_WD = 0.01
ADAM_STEP = 10

LANES = 128
SUBLANES = 8
V7X_VMEM_BYTES = 64 << 20
VMEM_LIMIT_CAP = V7X_VMEM_BYTES - (6 << 20)
N_CHIPS = 4
N_DEV = 8
CONV_ROWS = 64
EW_ROWS = 256
SCAN_SEGMENTS = 32


def _cparams(block_bytes):
    limit = min(VMEM_LIMIT_CAP, max(int(block_bytes) + (8 << 20), 24 << 20))
    return pltpu.CompilerParams(vmem_limit_bytes=limit)


def _nbytes(shape, dtype):
    n = 1
    for s in shape:
        n *= s
    return n * jnp.dtype(dtype).itemsize


def _divisor_tile(n, limit, quantum):
    if n <= limit:
        return n
    best = None
    for t in range(quantum, limit + 1, quantum):
        if n % t == 0:
            best = t
    assert best is not None, (n, limit, quantum)
    return best


def _bs(shape, imap, **kw):
    return pl.BlockSpec(shape, imap, **kw)


def _resident(shape):
    nd = len(shape)
    return pl.BlockSpec(shape, lambda *_: (0,) * nd, pipeline_mode=pl.Buffered(1))


def _sigmoid(x):
    return jax.nn.sigmoid(x)


def _dot(a, b):
    return jnp.dot(a, b, preferred_element_type=F32)


def _dot_nt(a, b):
    return lax.dot_general(a, b, (((1,), (1,)), ((), ())), preferred_element_type=F32)


def _dot_tn(a, b):
    return lax.dot_general(a, b, (((0,), (0,)), ((), ())), preferred_element_type=F32)


def _norm_fwd(z, g, b):
    mu = jnp.mean(z, axis=-1, keepdims=True)
    xc = z - mu
    var = jnp.mean(xc * xc, axis=-1, keepdims=True)
    return xc * lax.rsqrt(var + LN_EPS) * g + b


def _norm_bwd(z, g, dy):
    mu = jnp.mean(z, axis=-1, keepdims=True)
    xc = z - mu
    var = jnp.mean(xc * xc, axis=-1, keepdims=True)
    rstd = lax.rsqrt(var + LN_EPS)
    xhat = xc * rstd
    dxh = dy * g
    m1 = jnp.mean(dxh, axis=-1, keepdims=True)
    m2 = jnp.mean(dxh * xhat, axis=-1, keepdims=True)
    return rstd * (dxh - m1 - xhat * m2), xhat


GELU_K = 0.7978845608028654
GELU_C = 0.044715


def _gelu(x):
    return 0.5 * x * (1.0 + jnp.tanh(GELU_K * (x + GELU_C * x * x * x)))


def _gelu_grad(x):
    t = jnp.tanh(GELU_K * (x + GELU_C * x * x * x))
    return 0.5 * (1.0 + t) + 0.5 * x * (1.0 - t * t) * GELU_K * (1.0 + 3.0 * GELU_C * x * x)


def _softplus(y):
    return jnp.maximum(y, 0.0) + jnp.log1p(jnp.exp(-jnp.abs(y)))


def _neg_expm1(y):
    series = -y * (1.0 + y * (0.5 + y * (1.0 / 6.0 + y * (1.0 / 24.0 + y * (1.0 / 120.0 + y * (1.0 / 720.0))))))
    return jnp.where(y > -0.25, series, 1.0 - jnp.exp(y))


def _colsum(x):
    return jnp.sum(x, axis=0, keepdims=True)


def _shifted_taps(src_ref, base, rows, taps):
    max_off = max(o for o, _ in taps)
    win_rows = rows + SUBLANES * (max_off // SUBLANES) + SUBLANES
    win = src_ref[pl.ds(base, win_rows), :]
    acc = None
    for rr in range(SUBLANES):
        sel = [(o, c) for o, c in taps if o % SUBLANES == rr]
        if not sel:
            continue
        wr = win if rr == 0 else pltpu.roll(win, win_rows - rr, axis=0)
        for o, coef in sel:
            q8 = SUBLANES * (o // SUBLANES)
            term = coef() * wr[q8:q8 + rows]
            acc = term if acc is None else acc + term
    return acc


def _shifted_corr(src_ref, base, rows, d, acc_ref, offs):
    max_off = max(offs)
    win_rows = rows + SUBLANES * (max_off // SUBLANES) + SUBLANES
    win = src_ref[pl.ds(base, win_rows), :]
    for rr in range(SUBLANES):
        sel = [(k, o) for k, o in enumerate(offs) if o % SUBLANES == rr]
        if not sel:
            continue
        wr = win if rr == 0 else pltpu.roll(win, win_rows - rr, axis=0)
        for k, o in sel:
            q8 = SUBLANES * (o // SUBLANES)
            prod = d * wr[q8:q8 + rows]
            part = jnp.sum(prod.reshape(rows // SUBLANES, SUBLANES, prod.shape[-1]), axis=0)
            acc_ref[SUBLANES * k:SUBLANES * (k + 1), :] += part


def _front_pad(ktaps):
    return SUBLANES * ((ktaps - 1 + SUBLANES - 1) // SUBLANES)


def _pad_rows(ktaps):
    return _front_pad(ktaps) + SUBLANES


def _ffn_tiles(S, F):
    tm = _divisor_tile(S, 1024, 16)
    tf = _divisor_tile(F, 256, LANES)
    return tm, tf


def _ffn_fwd(x, wg, wu, wd, g, b, alpha, name):
    S, D = x.shape
    F = wd.shape[0]
    tm, tf = _ffn_tiles(S, F)
    nf = F // tf

    def body(x_ref, wg_ref, wu_ref, wd_ref, g_ref, b_ref, y_ref, z_ref, hg_ref, hu_ref, acc_ref, xb_ref):
        j = pl.program_id(1)

        @pl.when(j == 0)
        def _():
            xb_ref[...] = x_ref[...].astype(BF16)
            acc_ref[...] = jnp.zeros_like(acc_ref)

        xb = xb_ref[...]
        hg = _dot(xb, wg_ref[...])
        hu = _dot(xb, wu_ref[...])
        hg_ref[...] = hg
        hu_ref[...] = hu
        a = (hg * _sigmoid(hg) * hu).astype(BF16)
        acc_ref[...] += _dot(a, wd_ref[...])

        @pl.when(j == nf - 1)
        def _():
            z = alpha * x_ref[...] + 0.5 * acc_ref[...]
            z_ref[...] = z
            y_ref[...] = _norm_fwd(z, g_ref[...], b_ref[...])

    blk = 2 * (3 * tm * D * 4 + 2 * tm * tf * 4 + 3 * D * tf * 2) + tm * D * 6 + 3 * tm * tf * 4
    return pl.pallas_call(
        body, name=name, grid=(S // tm, nf),
        in_specs=[_bs((tm, D), lambda i, j: (i, 0)), _bs((D, tf), lambda i, j: (0, j)), _bs((D, tf), lambda i, j: (0, j)),
                  _bs((tf, D), lambda i, j: (j, 0)), _bs((1, D), lambda i, j: (0, 0)), _bs((1, D), lambda i, j: (0, 0))],
        out_specs=[_bs((tm, D), lambda i, j: (i, 0)), _bs((tm, D), lambda i, j: (i, 0)),
                   _bs((tm, tf), lambda i, j: (i, j)), _bs((tm, tf), lambda i, j: (i, j))],
        out_shape=[jax.ShapeDtypeStruct((S, D), F32), jax.ShapeDtypeStruct((S, D), F32),
                   jax.ShapeDtypeStruct((S, F), F32), jax.ShapeDtypeStruct((S, F), F32)],
        scratch_shapes=[pltpu.VMEM((tm, D), F32), pltpu.VMEM((tm, D), BF16)],
        compiler_params=_cparams(blk),
    )(x, wg, wu, wd, g, b)


def _ffn_bwd(dy, z, hg, hu, wg, wu, wd, g, alpha, name):
    S, D = dy.shape
    F = wd.shape[0]
    tm, tf = _ffn_tiles(S, F)
    nf = F // tf

    def body(dy_ref, z_ref, hg_ref, hu_ref, wg_ref, wu_ref, wd_ref, g_ref,
             dx_ref, df_ref, a_ref, dhg_ref, dhu_ref, dg_ref, db_ref, acc_ref):
        i = pl.program_id(0)
        j = pl.program_id(1)

        @pl.when((i == 0) & (j == 0))
        def _():
            dg_ref[...] = jnp.zeros_like(dg_ref)
            db_ref[...] = jnp.zeros_like(db_ref)

        @pl.when(j == 0)
        def _():
            dy_ = dy_ref[...]
            dz, xhat = _norm_bwd(z_ref[...], g_ref[...], dy_)
            dg_ref[...] += _colsum(dy_ * xhat)
            db_ref[...] += _colsum(dy_)
            acc_ref[...] = alpha * dz
            df_ref[...] = (0.5 * dz).astype(BF16)

        da = _dot_nt(df_ref[...], wd_ref[...])
        hg_ = hg_ref[...]
        hu_ = hu_ref[...]
        s = _sigmoid(hg_)
        sl = hg_ * s
        dgate = (da * hu_ * (s * (1.0 + hg_ * (1.0 - s)))).astype(BF16)
        dup = (da * sl).astype(BF16)
        a_ref[...] = (sl * hu_).astype(BF16)
        dhg_ref[...] = dgate
        dhu_ref[...] = dup
        acc_ref[...] += _dot_nt(dgate, wg_ref[...]) + _dot_nt(dup, wu_ref[...])

        @pl.when(j == nf - 1)
        def _():
            dx_ref[...] = acc_ref[...]

    blk = 2 * (2 * tm * D * 4 + tm * D * 2 + 2 * tm * tf * 4 + 3 * tm * tf * 2 + 3 * D * tf * 2) + 3 * tm * D * 4 + 8 * tm * tf * 4
    once = dict(pipeline_mode=pl.Buffered(1))
    return pl.pallas_call(
        body, name=name, grid=(S // tm, nf),
        in_specs=[_bs((tm, D), lambda i, j: (i, 0), **once), _bs((tm, D), lambda i, j: (i, 0), **once),
                  _bs((tm, tf), lambda i, j: (i, j)), _bs((tm, tf), lambda i, j: (i, j)),
                  _bs((D, tf), lambda i, j: (0, j)), _bs((D, tf), lambda i, j: (0, j)), _bs((tf, D), lambda i, j: (j, 0)),
                  _bs((1, D), lambda i, j: (0, 0))],
        out_specs=[_bs((tm, D), lambda i, j: (i, 0)), _bs((tm, D), lambda i, j: (i, 0)),
                   _bs((tm, tf), lambda i, j: (i, j)), _bs((tm, tf), lambda i, j: (i, j)), _bs((tm, tf), lambda i, j: (i, j)),
                   _bs((1, D), lambda i, j: (0, 0)), _bs((1, D), lambda i, j: (0, 0))],
        out_shape=[jax.ShapeDtypeStruct((S, D), F32), jax.ShapeDtypeStruct((S, D), BF16),
                   jax.ShapeDtypeStruct((S, F), BF16), jax.ShapeDtypeStruct((S, F), BF16), jax.ShapeDtypeStruct((S, F), BF16),
                   jax.ShapeDtypeStruct((1, D), F32), jax.ShapeDtypeStruct((1, D), F32)],
        scratch_shapes=[pltpu.VMEM((tm, D), F32)],
        compiler_params=_cparams(blk),
    )(dy, z, hg, hu, wg, wu, wd, g)


def _mm_bias(x, w, bias, name):
    S, K = x.shape
    N = w.shape[1]
    tm = _divisor_tile(S, 512, 16)

    def body(x_ref, w_ref, b_ref, o_ref):
        o_ref[...] = _dot(x_ref[...].astype(BF16), w_ref[...]) + b_ref[...]

    blk = 2 * (tm * K * 4 + tm * N * 4) + K * N * 2 + tm * K * 2
    return pl.pallas_call(
        body, name=name, grid=(S // tm,),
        in_specs=[_bs((tm, K), lambda i: (i, 0)), _resident((K, N)), _resident((1, N))],
        out_specs=_bs((tm, N), lambda i: (i, 0)),
        out_shape=jax.ShapeDtypeStruct((S, N), F32),
        compiler_params=_cparams(blk),
    )(x, w, bias)


def _mm_tn(a, b, name):
    S, M = a.shape
    N = b.shape[1]
    bm = _divisor_tile(M, 1408, LANES)
    bn = _divisor_tile(N, 1408, LANES)
    tk = _divisor_tile(S, 512, 16)

    def body(a_ref, b_ref, o_ref):
        @pl.when(pl.program_id(2) == 0)
        def _():
            o_ref[...] = jnp.zeros_like(o_ref)

        o_ref[...] += _dot_tn(a_ref[...].astype(BF16), b_ref[...].astype(BF16))

    blk = 2 * (tk * bm * a.dtype.itemsize + tk * bn * b.dtype.itemsize + bm * bn * 4) + tk * bm * 4 + bm * bn * 4
    return pl.pallas_call(
        body, name=name, grid=(M // bm, N // bn, S // tk),
        in_specs=[_bs((tk, bm), lambda i, j, k: (k, i)), _bs((tk, bn), lambda i, j, k: (k, j))],
        out_specs=_bs((bm, bn), lambda i, j, k: (i, j)),
        out_shape=jax.ShapeDtypeStruct((M, N), F32),
        compiler_params=_cparams(blk),
    )(a, b)


def _mix_dx(dz, parts, weights, alpha, name):
    S, D = dz.shape
    tm = _divisor_tile(S, 256, 16)
    n = len(parts)

    def body(*refs):
        dz_ref = refs[0]
        p_refs = refs[1:1 + n]
        w_refs = refs[1 + n:1 + 2 * n]
        o_ref = refs[1 + 2 * n]
        acc = alpha * dz_ref[...]
        for p_ref, w_ref in zip(p_refs, w_refs):
            acc = acc + _dot_nt(p_ref[...], w_ref[...])
        o_ref[...] = acc

    widths = [p.shape[1] for p in parts]
    blk = 2 * (2 * tm * D * 4 + sum(tm * w * 2 for w in widths)) + sum(D * w * 2 for w in widths) + 2 * tm * D * 4
    return pl.pallas_call(
        body, name=name, grid=(S // tm,),
        in_specs=[_bs((tm, D), lambda i: (i, 0))] + [_bs((tm, w), lambda i: (i, 0)) for w in widths]
                 + [_resident((D, w)) for w in widths],
        out_specs=_bs((tm, D), lambda i: (i, 0)),
        out_shape=jax.ShapeDtypeStruct((S, D), F32),
        compiler_params=_cparams(blk),
    )(dz, *parts, *weights)


def _conv_branch_fwd(cv, cg, w, b, gg, gb, name):
    S, C = cv.shape
    K = w.shape[0]
    assert C // CONV_GROUPS == LANES
    padf = _front_pad(K)
    R = min(CONV_ROWS, S)
    E = min(EW_ROWS, S)

    def body(cv_ref, cg_ref, w_ref, b_ref, gg_ref, gb_ref, c2_ref, c4_ref, pad_ref):
        pad_ref[0:padf, :] = jnp.zeros((padf, LANES), F32)
        pad_ref[S + padf:S + padf + SUBLANES, :] = jnp.zeros((SUBLANES, LANES), F32)

        def fill(i, carry):
            r = pl.multiple_of(i * E, E)
            pad_ref[pl.ds(r + padf, E), :] = cv_ref[pl.ds(r, E), :] * _sigmoid(cg_ref[pl.ds(r, E), :])
            return carry

        lax.fori_loop(0, S // E, fill, 0)
        taps = [(padf - (K - 1) + k, functools.partial(lambda k: w_ref[k:k + 1, :], k)) for k in range(K)]

        def conv(i, carry):
            r = pl.multiple_of(i * R, R)
            c2 = _shifted_taps(pad_ref, r, R, taps) + b_ref[...]
            c2_ref[pl.ds(r, R), :] = c2
            c3 = _norm_fwd(c2, gg_ref[...], gb_ref[...])
            c4_ref[pl.ds(r, R), :] = (c3 * _sigmoid(c3)).astype(BF16)
            return carry

        lax.fori_loop(0, S // R, conv, 0)

    col = lambda i: (0, i)
    blk = 2 * (3 * S * LANES * 4 + S * LANES * 2) + (S + _pad_rows(K)) * LANES * 4
    return pl.pallas_call(
        body, name=name, grid=(C // LANES,),
        in_specs=[_bs((S, LANES), col), _bs((S, LANES), col), _bs((K, LANES), col),
                  _bs((1, LANES), col), _bs((1, LANES), col), _bs((1, LANES), col)],
        out_specs=[_bs((S, LANES), col), _bs((S, LANES), col)],
        out_shape=[jax.ShapeDtypeStruct((S, C), F32), jax.ShapeDtypeStruct((S, C), BF16)],
        scratch_shapes=[pltpu.VMEM((S + _pad_rows(K), LANES), F32)],
        compiler_params=_cparams(blk),
    )(cv, cg, w, b, gg, gb)


def _conv_branch_bwd(dc4, c2, cv, cg, w, gg, gb, name):
    S, C = cv.shape
    K = w.shape[0]
    padf = _front_pad(K)
    R = min(CONV_ROWS, S)

    def body(dc4_ref, c2_ref, cv_ref, cg_ref, w_ref, gg_ref, gb_ref,
             dcv_ref, dcg_ref, dw_ref, dwb_ref, dgg_ref, dgb_ref, scv_ref, scg_ref,
             dpad_ref, cpad_ref, dwacc_ref):
        cpad_ref[0:padf, :] = jnp.zeros((padf, LANES), F32)
        cpad_ref[S + padf:S + padf + SUBLANES, :] = jnp.zeros((SUBLANES, LANES), F32)
        dpad_ref[S:S + padf + SUBLANES, :] = jnp.zeros((padf + SUBLANES, LANES), F32)
        dwacc_ref[...] = jnp.zeros_like(dwacc_ref)
        for ref in (dwb_ref, dgg_ref, dgb_ref, scv_ref, scg_ref):
            ref[...] = jnp.zeros_like(ref)

        def norm_pass(i, carry):
            r = pl.multiple_of(i * R, R)
            g_ = gg_ref[...]
            c2 = c2_ref[pl.ds(r, R), :]
            mu = jnp.mean(c2, axis=-1, keepdims=True)
            xc = c2 - mu
            rstd = lax.rsqrt(jnp.mean(xc * xc, axis=-1, keepdims=True) + LN_EPS)
            xhat = xc * rstd
            c3 = xhat * g_ + gb_ref[...]
            s = _sigmoid(c3)
            dc3 = dc4_ref[pl.ds(r, R), :] * (s * (1.0 + c3 * (1.0 - s)))
            dgg_ref[...] += _colsum(dc3 * xhat)
            dgb_ref[...] += _colsum(dc3)
            dxh = dc3 * g_
            m1 = jnp.mean(dxh, axis=-1, keepdims=True)
            m2 = jnp.mean(dxh * xhat, axis=-1, keepdims=True)
            dc2 = rstd * (dxh - m1 - xhat * m2)
            dpad_ref[pl.ds(r, R), :] = dc2
            dwb_ref[...] += _colsum(dc2)
            cpad_ref[pl.ds(r + padf, R), :] = cv_ref[pl.ds(r, R), :] * _sigmoid(cg_ref[pl.ds(r, R), :])
            return carry

        lax.fori_loop(0, S // R, norm_pass, 0)
        taps = [(K - 1 - k, functools.partial(lambda k: w_ref[k:k + 1, :], k)) for k in range(K)]
        offs = [padf - (K - 1) + k for k in range(K)]

        def conv_pass(i, carry):
            r = pl.multiple_of(i * R, R)
            dc1 = _shifted_taps(dpad_ref, r, R, taps)
            sg = _sigmoid(cg_ref[pl.ds(r, R), :])
            cv_ = cv_ref[pl.ds(r, R), :]
            dcv = dc1 * sg
            dcg = dc1 * cv_ * sg * (1.0 - sg)
            dcv_ref[pl.ds(r, R), :] = dcv.astype(BF16)
            dcg_ref[pl.ds(r, R), :] = dcg.astype(BF16)
            scv_ref[...] += _colsum(dcv)
            scg_ref[...] += _colsum(dcg)
            _shifted_corr(cpad_ref, r, R, dpad_ref[pl.ds(r, R), :], dwacc_ref, offs)
            return carry

        lax.fori_loop(0, S // R, conv_pass, 0)
        for k in range(K):
            dw_ref[k:k + 1, :] = _colsum(dwacc_ref[SUBLANES * k:SUBLANES * (k + 1), :])

    col = lambda i: (0, i)
    row = jax.ShapeDtypeStruct((1, C), F32)
    blk = 2 * (4 * S * LANES * 4 + 2 * S * LANES * 2) + 2 * (S + _pad_rows(K)) * LANES * 4
    return pl.pallas_call(
        body, name=name, grid=(C // LANES,),
        in_specs=[_bs((S, LANES), col)] * 4 + [_bs((K, LANES), col), _bs((1, LANES), col), _bs((1, LANES), col)],
        out_specs=[_bs((S, LANES), col), _bs((S, LANES), col), _bs((K, LANES), col)] + [_bs((1, LANES), col)] * 5,
        out_shape=[jax.ShapeDtypeStruct((S, C), BF16), jax.ShapeDtypeStruct((S, C), BF16),
                   jax.ShapeDtypeStruct((K, C), F32), row, row, row, row, row],
        scratch_shapes=[pltpu.VMEM((S + _pad_rows(K), LANES), F32), pltpu.VMEM((S + _pad_rows(K), LANES), F32),
                        pltpu.VMEM((SUBLANES * K, LANES), F32)],
        compiler_params=_cparams(blk),
    )(dc4, c2, cv, cg, w, gg, gb)


def _short_conv_fwd(xin, w, b, name):
    S, C = xin.shape
    K = w.shape[0]
    padf = _front_pad(K)
    R = min(CONV_ROWS, S)
    E = min(EW_ROWS, S)

    def body(x_ref, w_ref, b_ref, o_ref, pad_ref):
        pad_ref[0:padf, :] = jnp.zeros((padf, LANES), F32)
        pad_ref[S + padf:S + padf + SUBLANES, :] = jnp.zeros((SUBLANES, LANES), F32)

        def fill(i, carry):
            r = pl.multiple_of(i * E, E)
            pad_ref[pl.ds(r + padf, E), :] = x_ref[pl.ds(r, E), :]
            return carry

        lax.fori_loop(0, S // E, fill, 0)
        taps = [(padf - (K - 1) + k, functools.partial(lambda k: w_ref[k:k + 1, :], k)) for k in range(K)]

        def conv(i, carry):
            r = pl.multiple_of(i * R, R)
            o_ref[pl.ds(r, R), :] = _shifted_taps(pad_ref, r, R, taps) + b_ref[...]
            return carry

        lax.fori_loop(0, S // R, conv, 0)

    col = lambda i: (0, i)
    blk = 2 * (2 * S * LANES * 4) + (S + _pad_rows(K)) * LANES * 4
    return pl.pallas_call(
        body, name=name, grid=(C // LANES,),
        in_specs=[_bs((S, LANES), col), _bs((K, LANES), col), _bs((1, LANES), col)],
        out_specs=_bs((S, LANES), col),
        out_shape=jax.ShapeDtypeStruct((S, C), F32),
        scratch_shapes=[pltpu.VMEM((S + _pad_rows(K), LANES), F32)],
        compiler_params=_cparams(blk),
    )(xin, w, b)


def _short_conv_bwd(dy, xin, w, name):
    S, C = xin.shape
    K = w.shape[0]
    padf = _front_pad(K)
    R = min(CONV_ROWS, S)
    E = min(EW_ROWS, S)

    def body(dy_ref, x_ref, w_ref, dx_ref, dw_ref, db_ref, sx_ref, dpad_ref, xpad_ref, dwacc_ref):
        xpad_ref[0:padf, :] = jnp.zeros((padf, LANES), F32)
        xpad_ref[S + padf:S + padf + SUBLANES, :] = jnp.zeros((SUBLANES, LANES), F32)
        dpad_ref[S:S + padf + SUBLANES, :] = jnp.zeros((padf + SUBLANES, LANES), F32)
        dwacc_ref[...] = jnp.zeros_like(dwacc_ref)
        db_ref[...] = jnp.zeros_like(db_ref)
        sx_ref[...] = jnp.zeros_like(sx_ref)

        def fill(i, carry):
            r = pl.multiple_of(i * E, E)
            d = dy_ref[pl.ds(r, E), :]
            dpad_ref[pl.ds(r, E), :] = d
            db_ref[...] += _colsum(d)
            xpad_ref[pl.ds(r + padf, E), :] = x_ref[pl.ds(r, E), :]
            return carry

        lax.fori_loop(0, S // E, fill, 0)
        taps = [(K - 1 - k, functools.partial(lambda k: w_ref[k:k + 1, :], k)) for k in range(K)]
        offs = [padf - (K - 1) + k for k in range(K)]

        def conv_pass(i, carry):
            r = pl.multiple_of(i * R, R)
            dx = _shifted_taps(dpad_ref, r, R, taps)
            dx_ref[pl.ds(r, R), :] = dx.astype(BF16)
            sx_ref[...] += _colsum(dx)
            _shifted_corr(xpad_ref, r, R, dpad_ref[pl.ds(r, R), :], dwacc_ref, offs)
            return carry

        lax.fori_loop(0, S // R, conv_pass, 0)
        for k in range(K):
            dw_ref[k:k + 1, :] = _colsum(dwacc_ref[SUBLANES * k:SUBLANES * (k + 1), :])

    col = lambda i: (0, i)
    row = jax.ShapeDtypeStruct((1, C), F32)
    blk = 2 * (2 * S * LANES * 4 + S * LANES * 2) + 2 * (S + _pad_rows(K)) * LANES * 4
    return pl.pallas_call(
        body, name=name, grid=(C // LANES,),
        in_specs=[_bs((S, LANES), col), _bs((S, LANES), col), _bs((K, LANES), col)],
        out_specs=[_bs((S, LANES), col), _bs((K, LANES), col), _bs((1, LANES), col), _bs((1, LANES), col)],
        out_shape=[jax.ShapeDtypeStruct((S, C), BF16), jax.ShapeDtypeStruct((K, C), F32), row, row],
        scratch_shapes=[pltpu.VMEM((S + _pad_rows(K), LANES), F32), pltpu.VMEM((S + _pad_rows(K), LANES), F32),
                        pltpu.VMEM((SUBLANES * K, LANES), F32)],
        compiler_params=_cparams(blk),
    )(dy, xin, w)


def _gates_fwd(r1, wa, wx, ba, bx, lam, name):
    S, R = r1.shape
    tm = _divisor_tile(S, 256, 16)

    def body(r1_ref, wa_ref, wx_ref, ba_ref, bx_ref, lam_ref, ra_ref, ri_ref, a_ref, uu_ref):
        r1_ = r1_ref[...]
        rb = r1_.astype(BF16)
        ra = _sigmoid(_dot(rb, wa_ref[...]) + ba_ref[...])
        ri = _sigmoid(_dot(rb, wx_ref[...]) + bx_ref[...])
        log_a = -RG_LRU_C * ra * _softplus(-lam_ref[...])
        ra_ref[...] = ra
        ri_ref[...] = ri
        a_ref[...] = jnp.exp(log_a)
        uu_ref[...] = jnp.sqrt(_neg_expm1(2.0 * log_a)) * (ri * r1_)

    blk = 2 * (5 * tm * R * 4) + 2 * R * R * 2 + 6 * tm * R * 4
    tile = _bs((tm, R), lambda i: (i, 0))
    return pl.pallas_call(
        body, name=name, grid=(S // tm,),
        in_specs=[tile, _resident((R, R)), _resident((R, R)), _resident((1, R)), _resident((1, R)), _resident((1, R))],
        out_specs=[tile] * 4,
        out_shape=[jax.ShapeDtypeStruct((S, R), F32)] * 4,
        compiler_params=_cparams(blk),
    )(r1, wa, wx, ba, bx, lam)


def _gates_bwd(guu, da, ra, ri, r1, wa, wx, lam, name):
    S, R = r1.shape
    tm = _divisor_tile(S, 256, 16)
    nsteps = S // tm

    def body(g_ref, da_ref, ra_ref, ri_ref, r1_ref, wa_ref, wx_ref, lam_ref,
             dr1_ref, dpa_ref, dpx_ref, dba_ref, dbx_ref, dlam_ref):
        i = pl.program_id(0)

        @pl.when(i == 0)
        def _():
            dba_ref[...] = jnp.zeros_like(dba_ref)
            dbx_ref[...] = jnp.zeros_like(dbx_ref)
            dlam_ref[...] = jnp.zeros_like(dlam_ref)

        g = g_ref[...]
        ra = ra_ref[...]
        ri = ri_ref[...]
        r1_ = r1_ref[...]
        sp = _softplus(-lam_ref[...])
        log_a = -RG_LRU_C * ra * sp
        a = jnp.exp(log_a)
        mult = jnp.sqrt(_neg_expm1(2.0 * log_a))
        d_ri = g * mult * r1_
        dr1 = g * mult * ri
        dmult = g * ri * r1_
        dlog_a = da_ref[...] * a - dmult * (a * a) / mult
        dra = dlog_a * (-RG_LRU_C * sp)
        dlam_ref[...] += _colsum(dlog_a * (-RG_LRU_C * ra))
        dpa = dra * ra * (1.0 - ra)
        dpx = d_ri * ri * (1.0 - ri)
        dba_ref[...] += _colsum(dpa)
        dbx_ref[...] += _colsum(dpx)
        dpa_b = dpa.astype(BF16)
        dpx_b = dpx.astype(BF16)
        dpa_ref[...] = dpa_b
        dpx_ref[...] = dpx_b
        dr1_ref[...] = dr1 + _dot_nt(dpa_b, wa_ref[...]) + _dot_nt(dpx_b, wx_ref[...])

        @pl.when(i == nsteps - 1)
        def _():
            dlam_ref[...] = dlam_ref[...] * (-_sigmoid(-lam_ref[...]))

    blk = 2 * (6 * tm * R * 4 + 2 * tm * R * 2) + 2 * R * R * 2 + 10 * tm * R * 4
    tile = _bs((tm, R), lambda i: (i, 0))
    rowspec = _bs((1, R), lambda i: (0, 0))
    row = jax.ShapeDtypeStruct((1, R), F32)
    return pl.pallas_call(
        body, name=name, grid=(nsteps,),
        in_specs=[tile] * 5 + [_resident((R, R)), _resident((R, R)), _resident((1, R))],
        out_specs=[tile, tile, tile, rowspec, rowspec, rowspec],
        out_shape=[jax.ShapeDtypeStruct((S, R), F32), jax.ShapeDtypeStruct((S, R), BF16), jax.ShapeDtypeStruct((S, R), BF16),
                   row, row, row],
        compiler_params=_cparams(blk),
    )(guu, da, ra, ri, r1, wa, wx, lam)


def _scan_geometry(S):
    nseg = SCAN_SEGMENTS if S % (SCAN_SEGMENTS * SUBLANES) == 0 else SUBLANES
    return nseg, S // nseg, nseg // SUBLANES


def _scan_fwd(a, u, name):
    S, C = a.shape
    nseg, L, V = _scan_geometry(S)

    def body(a_ref, u_ref, h_ref, e_ref, p_ref, init_ref):
        def rows(ref, v, j):
            return ref[pl.ds(v * SUBLANES * L + j, SUBLANES, stride=L), :]

        def run1(j, carry):
            hs, ps = carry
            nh, npd = [], []
            for v in range(V):
                aj = rows(a_ref, v, j)
                nh.append(aj * hs[v] + rows(u_ref, v, j))
                npd.append(aj * ps[v])
            return tuple(nh), tuple(npd)

        zero = jnp.zeros((SUBLANES, LANES), F32)
        one = jnp.ones((SUBLANES, LANES), F32)
        hs, ps = lax.fori_loop(0, L, run1, ((zero,) * V, (one,) * V))
        for v in range(V):
            e_ref[SUBLANES * v:SUBLANES * (v + 1), :] = hs[v]
            p_ref[SUBLANES * v:SUBLANES * (v + 1), :] = ps[v]
        init_ref[0:1, :] = jnp.zeros((1, LANES), F32)
        for s in range(1, nseg):
            init_ref[s:s + 1, :] = e_ref[s - 1:s, :] + p_ref[s - 1:s, :] * init_ref[s - 1:s, :]

        def run2(j, hs):
            nh = []
            for v in range(V):
                h = rows(a_ref, v, j) * hs[v] + rows(u_ref, v, j)
                h_ref[pl.ds(v * SUBLANES * L + j, SUBLANES, stride=L), :] = h
                nh.append(h)
            return tuple(nh)

        lax.fori_loop(0, L, run2, tuple(init_ref[SUBLANES * v:SUBLANES * (v + 1), :] for v in range(V)))

    col = lambda i: (0, i)
    blk = 2 * (3 * S * LANES * 4)
    return pl.pallas_call(
        body, name=name, grid=(C // LANES,),
        in_specs=[_bs((S, LANES), col), _bs((S, LANES), col)],
        out_specs=_bs((S, LANES), col),
        out_shape=jax.ShapeDtypeStruct((S, C), F32),
        scratch_shapes=[pltpu.VMEM((nseg, LANES), F32)] * 3,
        compiler_params=_cparams(blk),
    )(a, u)


def _scan_bwd(a, dh, h, name):
    S, C = a.shape
    nseg, L, V = _scan_geometry(S)
    E = min(EW_ROWS, S)

    def body(a_ref, d_ref, h_ref, g_ref, da_ref, apad_ref, hpad_ref, e_ref, p_ref, init_ref):
        apad_ref[S:S + SUBLANES, :] = jnp.zeros((SUBLANES, LANES), F32)
        hpad_ref[0:SUBLANES, :] = jnp.zeros((SUBLANES, LANES), F32)

        def fill(i, carry):
            r = pl.multiple_of(i * E, E)
            apad_ref[pl.ds(r, E), :] = a_ref[pl.ds(r, E), :]
            hpad_ref[pl.ds(r + SUBLANES, E), :] = h_ref[pl.ds(r, E), :]
            return carry

        lax.fori_loop(0, S // E, fill, 0)

        def rows(ref, v, j, shift=0):
            return ref[pl.ds(v * SUBLANES * L + j + shift, SUBLANES, stride=L), :]

        def run1(jj, carry):
            j = L - 1 - jj
            gs, ps = carry
            ng, npd = [], []
            for v in range(V):
                bj = rows(apad_ref, v, j, 1)
                ng.append(bj * gs[v] + rows(d_ref, v, j))
                npd.append(bj * ps[v])
            return tuple(ng), tuple(npd)

        zero = jnp.zeros((SUBLANES, LANES), F32)
        one = jnp.ones((SUBLANES, LANES), F32)
        gs, ps = lax.fori_loop(0, L, run1, ((zero,) * V, (one,) * V))
        for v in range(V):
            e_ref[SUBLANES * v:SUBLANES * (v + 1), :] = gs[v]
            p_ref[SUBLANES * v:SUBLANES * (v + 1), :] = ps[v]
        init_ref[nseg - 1:nseg, :] = jnp.zeros((1, LANES), F32)
        for s in range(nseg - 2, -1, -1):
            init_ref[s:s + 1, :] = e_ref[s + 1:s + 2, :] + p_ref[s + 1:s + 2, :] * init_ref[s + 1:s + 2, :]

        def run2(jj, gs):
            j = L - 1 - jj
            ng = []
            for v in range(V):
                g = rows(apad_ref, v, j, 1) * gs[v] + rows(d_ref, v, j)
                g_ref[pl.ds(v * SUBLANES * L + j, SUBLANES, stride=L), :] = g
                da_ref[pl.ds(v * SUBLANES * L + j, SUBLANES, stride=L), :] = g * rows(hpad_ref, v, j, SUBLANES - 1)
                ng.append(g)
            return tuple(ng)

        lax.fori_loop(0, L, run2, tuple(init_ref[SUBLANES * v:SUBLANES * (v + 1), :] for v in range(V)))

    col = lambda i: (0, i)
    blk = 2 * (5 * S * LANES * 4) + 2 * (S + SUBLANES) * LANES * 4
    return pl.pallas_call(
        body, name=name, grid=(C // LANES,),
        in_specs=[_bs((S, LANES), col)] * 3,
        out_specs=[_bs((S, LANES), col)] * 2,
        out_shape=[jax.ShapeDtypeStruct((S, C), F32)] * 2,
        scratch_shapes=[pltpu.VMEM((S + SUBLANES, LANES), F32), pltpu.VMEM((S + SUBLANES, LANES), F32)]
                       + [pltpu.VMEM((nseg, LANES), F32)] * 3,
        compiler_params=_cparams(blk),
    )(a, dh, h)


def _mixer_out_fwd(c4, h, rg, gc, gr, x1, wc, wr, wo, g, b, alpha, name):
    S, D = x1.shape
    R = h.shape[1]
    tm = _divisor_tile(S, 256, 16)

    def body(c4_ref, h_ref, rg_ref, gc_ref, gr_ref, x_ref, wc_ref, wr_ref, wo_ref, g_ref, b_ref,
             yc_ref, yr_ref, z_ref, y_ref):
        yc = _dot(c4_ref[...], wc_ref[...])
        q = (h_ref[...] * _gelu(rg_ref[...])).astype(BF16)
        yr = _dot(q, wr_ref[...])
        yc_ref[...] = yc
        yr_ref[...] = yr
        m = (_sigmoid(gc_ref[...]) * yc + _sigmoid(gr_ref[...]) * yr).astype(BF16)
        z = alpha * x_ref[...] + _dot(m, wo_ref[...])
        z_ref[...] = z
        y_ref[...] = _norm_fwd(z, g_ref[...], b_ref[...])

    blk = 2 * (tm * D * 2 + 2 * tm * R * 4 + 7 * tm * D * 4) + (2 * D * D + R * D) * 2 + 6 * tm * D * 4
    td = _bs((tm, D), lambda i: (i, 0))
    tr = _bs((tm, R), lambda i: (i, 0))
    return pl.pallas_call(
        body, name=name, grid=(S // tm,),
        in_specs=[td, tr, tr, td, td, td, _resident((D, D)), _resident((R, D)), _resident((D, D)),
                  _resident((1, D)), _resident((1, D))],
        out_specs=[td] * 4,
        out_shape=[jax.ShapeDtypeStruct((S, D), F32)] * 4,
        compiler_params=_cparams(blk),
    )(c4, h, rg, gc, gr, x1, wc, wr, wo, g, b)


def _mixer_out_bwd(dy, z, g, wo, yc, yr, gc, gr, name):
    S, D = dy.shape
    tm = _divisor_tile(S, 256, 16)

    def body(dy_ref, z_ref, g_ref, wo_ref, yc_ref, yr_ref, gc_ref, gr_ref,
             dz_ref, dzb_ref, m_ref, dyc_ref, dyr_ref, dgc_ref, dgr_ref, sgc_ref, sgr_ref, dg_ref, db_ref):
        @pl.when(pl.program_id(0) == 0)
        def _():
            for ref in (sgc_ref, sgr_ref, dg_ref, db_ref):
                ref[...] = jnp.zeros_like(ref)

        dy_ = dy_ref[...]
        dz, xhat = _norm_bwd(z_ref[...], g_ref[...], dy_)
        dg_ref[...] += _colsum(dy_ * xhat)
        db_ref[...] += _colsum(dy_)
        dz_ref[...] = dz
        dzb = dz.astype(BF16)
        dzb_ref[...] = dzb
        dm = _dot_nt(dzb, wo_ref[...])
        yc = yc_ref[...]
        yr = yr_ref[...]
        sc = _sigmoid(gc_ref[...])
        sr = _sigmoid(gr_ref[...])
        m_ref[...] = (sc * yc + sr * yr).astype(BF16)
        dyc_ref[...] = (dm * sc).astype(BF16)
        dyr_ref[...] = (dm * sr).astype(BF16)
        dgc = dm * yc * sc * (1.0 - sc)
        dgr = dm * yr * sr * (1.0 - sr)
        dgc_ref[...] = dgc.astype(BF16)
        dgr_ref[...] = dgr.astype(BF16)
        sgc_ref[...] += _colsum(dgc)
        sgr_ref[...] += _colsum(dgr)

    blk = 2 * (7 * tm * D * 4 + 6 * tm * D * 2) + D * D * 2 + 8 * tm * D * 4
    td = _bs((tm, D), lambda i: (i, 0))
    rowspec = _bs((1, D), lambda i: (0, 0))
    row = jax.ShapeDtypeStruct((1, D), F32)
    bfd = jax.ShapeDtypeStruct((S, D), BF16)
    return pl.pallas_call(
        body, name=name, grid=(S // tm,),
        in_specs=[td, td, _resident((1, D)), _resident((D, D)), td, td, td, td],
        out_specs=[td] * 7 + [rowspec] * 4,
        out_shape=[jax.ShapeDtypeStruct((S, D), F32), bfd, bfd, bfd, bfd, bfd, bfd, row, row, row, row],
        compiler_params=_cparams(blk),
    )(dy, z, g, wo, yc, yr, gc, gr)


def _branch_bwd(dyc, dyr, wc, wr, h, rg, name):
    S, D = dyc.shape
    R = h.shape[1]
    tm = _divisor_tile(S, 256, 16)

    def body(dyc_ref, dyr_ref, wc_ref, wr_ref, h_ref, rg_ref, dc4_ref, dh_ref, drg_ref, q_ref, srg_ref):
        @pl.when(pl.program_id(0) == 0)
        def _():
            srg_ref[...] = jnp.zeros_like(srg_ref)

        dc4_ref[...] = _dot_nt(dyc_ref[...], wc_ref[...])
        dq = _dot_nt(dyr_ref[...], wr_ref[...])
        h_ = h_ref[...]
        rg_ = rg_ref[...]
        ge = _gelu(rg_)
        dh_ref[...] = dq * ge
        drg = dq * h_ * _gelu_grad(rg_)
        drg_ref[...] = drg.astype(BF16)
        srg_ref[...] += _colsum(drg)
        q_ref[...] = (h_ * ge).astype(BF16)

    blk = 2 * (2 * tm * D * 2 + tm * D * 4 + 3 * tm * R * 4 + 2 * tm * R * 2) + (D * D + R * D) * 2 + 6 * tm * R * 4
    td = _bs((tm, D), lambda i: (i, 0))
    tr = _bs((tm, R), lambda i: (i, 0))
    return pl.pallas_call(
        body, name=name, grid=(S // tm,),
        in_specs=[td, td, _resident((D, D)), _resident((R, D)), tr, tr],
        out_specs=[td, tr, tr, tr, _bs((1, R), lambda i: (0, 0))],
        out_shape=[jax.ShapeDtypeStruct((S, D), F32), jax.ShapeDtypeStruct((S, R), F32), jax.ShapeDtypeStruct((S, R), BF16),
                   jax.ShapeDtypeStruct((S, R), BF16), jax.ShapeDtypeStruct((1, R), F32)],
        compiler_params=_cparams(blk),
    )(dyc, dyr, wc, wr, h, rg)


def _loss_head(y, target, name):
    S, D = y.shape
    tm = _divisor_tile(S, 512, 16)
    nsteps = S // tm

    def body(y_ref, t_ref, loss_ref, dy_ref, acc_ref):
        i = pl.program_id(0)

        @pl.when(i == 0)
        def _():
            acc_ref[...] = jnp.zeros_like(acc_ref)

        err = y_ref[...] - t_ref[...]
        dy_ref[...] = err * (1.0 / D)
        acc_ref[...] += _colsum(err * err)

        @pl.when(i == nsteps - 1)
        def _():
            loss_ref[...] = jnp.sum(acc_ref[...], axis=-1, keepdims=True) * (0.5 / D)

    td = _bs((tm, D), lambda i: (i, 0))
    return pl.pallas_call(
        body, name=name, grid=(nsteps,),
        in_specs=[td, td],
        out_specs=[_bs((1, 1), lambda i: (0, 0)), td],
        out_shape=[jax.ShapeDtypeStruct((1, 1), F32), jax.ShapeDtypeStruct((S, D), F32)],
        scratch_shapes=[pltpu.VMEM((1, D), F32)],
        compiler_params=_cparams(2 * 3 * tm * D * 4),
    )(y, target)


def _adamw_math(w, g, m, v):
    m = ADAM_B1 * m + (1.0 - ADAM_B1) * g
    v = ADAM_B2 * v + (1.0 - ADAM_B2) * (g * g)
    m_hat = m / (1.0 - ADAM_B1 ** ADAM_STEP)
    v_hat = v / (1.0 - ADAM_B2 ** ADAM_STEP)
    delta = -ADAM_LR * (m_hat / (jnp.sqrt(v_hat) + ADAM_EPS) + ADAM_WD * w)
    return delta, m, v


def _adamw_sharded(w, m, v, parts, name):
    _, r, c = w.shape
    tr = _divisor_tile(r, max(16, (1 << 20) // (4 * c) // 16 * 16), 16)

    def body(w_ref, m_ref, v_ref, p_ref, g_ref, d_ref, nm_ref, nv_ref):
        g = p_ref[0].astype(F32)
        for k in range(1, N_CHIPS):
            g = g + p_ref[k].astype(F32)
        delta, nm, nv = _adamw_math(w_ref[...], g, m_ref[...], v_ref[...])
        g_ref[...] = g
        d_ref[...] = delta
        nm_ref[...] = nm
        nv_ref[...] = nv

    tile = _bs((None, tr, c), lambda l, i: (l, i, 0))
    sds = jax.ShapeDtypeStruct(w.shape, F32)
    return pl.pallas_call(
        body, name=name, grid=(2, r // tr),
        in_specs=[tile, tile, tile, _bs((N_CHIPS, None, tr, c), lambda l, i: (0, l, i, 0))],
        out_specs=[tile] * 4,
        out_shape=[sds] * 4,
        compiler_params=_cparams(2 * (7 * tr * c * 4 + N_CHIPS * tr * c * 2) + 6 * tr * c * 4),
    )(w, m, v, parts)


def _adamw_flat(w, m, v, g, name):
    rows = w.shape[0]
    tr = _divisor_tile(rows, 1024, SUBLANES)

    def body(w_ref, m_ref, v_ref, g_ref, d_ref, nm_ref, nv_ref):
        delta, nm, nv = _adamw_math(w_ref[...], g_ref[...], m_ref[...], v_ref[...])
        d_ref[...] = delta
        nm_ref[...] = nm
        nv_ref[...] = nv

    tile = _bs((tr, LANES), lambda i: (i, 0))
    sds = jax.ShapeDtypeStruct(w.shape, F32)
    return pl.pallas_call(
        body, name=name, grid=(rows // tr,),
        in_specs=[tile] * 4, out_specs=[tile] * 3, out_shape=[sds] * 3,
        compiler_params=_cparams(2 * 7 * tr * LANES * 4),
    )(w, m, v, g)


def _pair_sum_bf16(a, b, name):
    rows, c = a.shape
    tr = _divisor_tile(rows, max(16, (1 << 20) // (4 * c) // 16 * 16), 16)

    def body(a_ref, b_ref, o_ref):
        o_ref[...] = (a_ref[...] + b_ref[...]).astype(BF16)

    tile = _bs((tr, c), lambda i: (i, 0))
    return pl.pallas_call(
        body, name=name, grid=(rows // tr,),
        in_specs=[tile, tile], out_specs=tile, out_shape=jax.ShapeDtypeStruct((rows, c), BF16),
        compiler_params=_cparams(2 * 3 * tr * c * 4),
    )(a, b)


ANY = pl.BlockSpec(memory_space=pl.ANY)


def _mesh_position():
    return lax.axis_index("x"), lax.axis_index("y"), lax.axis_index("c")


def _chip_exchange(srcs, gather, name):
    n = len(srcs)
    n_copies = 7

    def body(*refs):
        src = refs[:n]
        out = refs[n:2 * n]
        send_sems, recv_sems, local_sems = refs[2 * n:]
        x, y, c = _mesh_position()
        me = 2 * x + y
        sibling = (x, y, 1 - c)
        chips = [(1 - x, y), (x, 1 - y), (1 - x, 1 - y)]
        chip_ids = [2 * cx + cy for cx, cy in chips]

        def remote(i, k, src_ref, dst_ref, to):
            return pltpu.make_async_remote_copy(src_ref=src_ref, dst_ref=dst_ref, send_sem=send_sems.at[i, k],
                                                recv_sem=recv_sems.at[i, k], device_id=to, device_id_type=MESH)

        started, local = [], []
        for i in range(n):
            if gather:
                own = pltpu.make_async_copy(src[i], out[i].at[me], local_sems.at[i])
            else:
                own = pltpu.make_async_copy(src[i].at[me], out[i].at[me, c], local_sems.at[i])
                started.append(remote(i, 6, src[i].at[me], out[i].at[me, c], sibling))
                started[-1].start()
            own.start()
            local.append(own)
            for j in range(3):
                payload = src[i].at[c] if gather else src[i].at[chip_ids[j]]
                cp = remote(i, j, payload, out[i].at[me, c], (*chips[j], c))
                cp.start()
                started.append(cp)
        for i in range(n):
            for j in range(3):
                slot = out[i].at[chip_ids[j], c]
                remote(i, j, slot, slot, sibling).wait_recv()
                fwd = remote(i, 3 + j, slot, slot, sibling)
                fwd.start()
                started.append(fwd)
        for i in range(n):
            for j in range(3):
                slot = out[i].at[chip_ids[j], 1 - c]
                remote(i, 3 + j, slot, slot, sibling).wait_recv()
            if not gather:
                slot = out[i].at[me, 1 - c]
                remote(i, 6, slot, slot, sibling).wait_recv()
        for cp in started:
            cp.wait_send()
        for cp in local:
            cp.wait()

    out_shape = [jax.ShapeDtypeStruct((N_CHIPS, 2) + (s.shape[1:]), s.dtype) for s in srcs]
    return pl.pallas_call(
        body, name=name,
        in_specs=[ANY] * n, out_specs=[ANY] * n, out_shape=out_shape,
        scratch_shapes=[pltpu.SemaphoreType.DMA((n, n_copies)), pltpu.SemaphoreType.DMA((n, n_copies)),
                        pltpu.SemaphoreType.DMA((n,))],
    )(*srcs)


def _sibling_exchange(grads, name):
    n = len(grads)

    def body(*refs):
        src = refs[:n]
        mine = refs[n:2 * n]
        theirs = refs[2 * n:3 * n]
        send_sems, recv_sems, local_sems = refs[3 * n:]
        x, y, c = _mesh_position()
        sibling = (x, y, 1 - c)
        started = []
        for i in range(n):
            own = pltpu.make_async_copy(src[i].at[:, c], mine[i], local_sems.at[i])
            own.start()
            cp = pltpu.make_async_remote_copy(src_ref=src[i].at[:, 1 - c], dst_ref=theirs[i], send_sem=send_sems.at[i],
                                              recv_sem=recv_sems.at[i], device_id=sibling, device_id_type=MESH)
            cp.start()
            started.append((own, cp))
        for own, cp in started:
            cp.wait()
            own.wait()

    out_shape = [jax.ShapeDtypeStruct((N_CHIPS,) + g.shape[2:], g.dtype) for g in grads] * 2
    outs = pl.pallas_call(
        body, name=name,
        in_specs=[ANY] * n, out_specs=[ANY] * (2 * n), out_shape=out_shape,
        scratch_shapes=[pltpu.SemaphoreType.DMA((n,)), pltpu.SemaphoreType.DMA((n,)), pltpu.SemaphoreType.DMA((n,))],
    )(*grads)
    return outs[:n], outs[n:]


def _all_reduce_small(v, name):
    _, rows, _ = v.shape

    def body(v_ref, o_ref, recv_ref, send_sems, recv_sems):
        x, y, c = _mesh_position()
        me = 4 * x + 2 * y + c
        peers = []
        for d in range(1, N_DEV):
            px, py, pc = x ^ ((d >> 2) & 1), y ^ ((d >> 1) & 1), c ^ (d & 1)
            peers.append(((px, py, pc), 4 * px + 2 * py + pc))

        def remote(k, src_ref, dst_ref, to):
            return pltpu.make_async_remote_copy(src_ref=src_ref, dst_ref=dst_ref, send_sem=send_sems.at[k],
                                                recv_sem=recv_sems.at[k], device_id=to, device_id_type=MESH)

        scatter = [remote(d, v_ref.at[pid], recv_ref.at[me], to) for d, (to, pid) in enumerate(peers)]
        for cp in scatter:
            cp.start()
        recv_ref[pl.ds(me, 1)] = v_ref[pl.ds(me, 1)]
        for d, (to, pid) in enumerate(peers):
            remote(d, v_ref.at[pid], recv_ref.at[pid], to).wait_recv()
        total = recv_ref[0]
        for s in range(1, N_DEV):
            total = total + recv_ref[s]
        o_ref[pl.ds(me, 1)] = total[None]
        gather = [remote(N_DEV - 1 + d, o_ref.at[me], o_ref.at[me], to) for d, (to, pid) in enumerate(peers)]
        for cp in gather:
            cp.start()
        for d, (to, pid) in enumerate(peers):
            remote(N_DEV - 1 + d, o_ref.at[pid], o_ref.at[pid], to).wait_recv()
        for cp in scatter + gather:
            cp.wait_send()

    vm = pl.BlockSpec(memory_space=pltpu.VMEM)
    return pl.pallas_call(
        body, name=name,
        in_specs=[vm], out_specs=vm, out_shape=jax.ShapeDtypeStruct(v.shape, F32),
        scratch_shapes=[pltpu.VMEM(v.shape, F32), pltpu.SemaphoreType.DMA((2 * (N_DEV - 1),)),
                        pltpu.SemaphoreType.DMA((2 * (N_DEV - 1),))],
        compiler_params=_cparams(4 * _nbytes(v.shape, F32)),
    )(v)


SHARDED_MATS = ("ffn1_w_gu", "ffn1_w_down", "mix_w_in", "conv_w_proj", "rnn_w_proj", "mix_w_out", "ffn2_w_gu", "ffn2_w_down")
COL_SHARDED = ("ffn1_w_gu", "mix_w_in", "ffn2_w_gu")
SHARDED_VECS = ("conv_dw_w", "rnn_conv_w")
WEIGHT_NAMES = ("ffn1_w_gu", "ffn1_w_down", "ln1_g", "ln1_b", "mix_w_in", "mix_b_in", "conv_dw_w", "conv_dw_b", "conv_gn_g",
                "conv_gn_b", "conv_w_proj", "rnn_conv_w", "rnn_conv_b", "rnn_w_a", "rnn_b_a", "rnn_w_x", "rnn_b_x",
                "rnn_lambda", "rnn_w_proj", "mix_w_out", "ln2_g", "ln2_b", "ffn2_w_gu", "ffn2_w_down", "ln3_g", "ln3_b")
SMALL_NAMES = tuple(n for n in WEIGHT_NAMES if n not in SHARDED_MATS)


def _unshard_cols(gathered):
    k4, K, n = gathered.shape
    return jnp.transpose(gathered, (1, 0, 2)).reshape(K, k4 * n)


def _shard_cols(full):
    K, N = full.shape
    return jnp.transpose(full.reshape(K, N_CHIPS, N // N_CHIPS), (1, 0, 2))


def _block_diag(w):
    H, bk, _ = w.shape
    eye = jnp.eye(H, dtype=w.dtype)
    return jnp.einsum("hij,hk->hikj", w, eye).reshape(H * bk, H * bk)


def _diag_blocks(dense, H):
    bk = dense.shape[0] // H
    d4 = dense.reshape(H, bk, H, bk)
    return jnp.transpose(jnp.diagonal(d4, axis1=0, axis2=2), (2, 0, 1))


def _row(v):
    return v.reshape(1, -1)


def _layer_forward(x0, p, alpha, l):
    t = f"l{l}_"
    sv = {"x0": x0}
    x1, sv["z1"], sv["hg1"], sv["hu1"] = _ffn_fwd(x0, p["wg1"], p["wu1"], p["wd1"], p["ln1_g"], p["ln1_b"], alpha, t + "ffn1_fwd")
    sv["x1"] = x1
    sec = {}
    for s in ("cv", "cg", "rx", "rg", "gc", "gr"):
        sec[s] = _mm_bias(x1, p["win_" + s], p["bin_" + s], t + "win_" + s)
    sv.update(sec)
    sv["c2"], c4 = _conv_branch_fwd(sec["cv"], sec["cg"], p["conv_dw_w"], p["conv_dw_b"], p["conv_gn_g"], p["conv_gn_b"], t + "conv_fwd")
    sv["c4"] = c4
    r1 = _short_conv_fwd(sec["rx"], p["rnn_conv_w"], p["rnn_conv_b"], t + "rconv_fwd")
    sv["r1"] = r1
    sv["ra"], sv["ri"], a, uu = _gates_fwd(r1, p["wa"], p["wx"], p["rnn_b_a"], p["rnn_b_x"], p["rnn_lambda"], t + "gates_fwd")
    sv["a"] = a
    h = _scan_fwd(a, uu, t + "scan_fwd")
    sv["h"] = h
    sv["yc"], sv["yr"], sv["z2"], x2 = _mixer_out_fwd(c4, h, sec["rg"], sec["gc"], sec["gr"], x1, p["wc"], p["wr"], p["wo"],
                                                      p["ln2_g"], p["ln2_b"], alpha, t + "mixout_fwd")
    sv["x2"] = x2
    x3, sv["z3"], sv["hg2"], sv["hu2"] = _ffn_fwd(x2, p["wg2"], p["wu2"], p["wd2"], p["ln3_g"], p["ln3_b"], alpha, t + "ffn2_fwd")
    return x3, sv


def _layer_backward(dy, p, sv, alpha, l):
    t = f"l{l}_"
    g = {}
    dx2, df, a_act, dhg, dhu, g["ln3_g"], g["ln3_b"] = _ffn_bwd(dy, sv["z3"], sv["hg2"], sv["hu2"], p["wg2"], p["wu2"], p["wd2"],
                                                                 p["ln3_g"], alpha, t + "ffn2_bwd")
    g["ffn2_w_down"] = _mm_tn(a_act, df, t + "dwd2")
    g["ffn2_w_gu"] = jnp.concatenate([_mm_tn(sv["x2"], dhg, t + "dwg2"), _mm_tn(sv["x2"], dhu, t + "dwu2")], axis=1)
    (dz2, dz2b, m_b, dyc, dyr, dgc, dgr, s_gc, s_gr, g["ln2_g"], g["ln2_b"]) = _mixer_out_bwd(
        dx2, sv["z2"], p["ln2_g"], p["wo"], sv["yc"], sv["yr"], sv["gc"], sv["gr"], t + "mixout_bwd")
    g["mix_w_out"] = _mm_tn(m_b, dz2b, t + "dwo")
    dc4, dh, drg, q_b, s_rg = _branch_bwd(dyc, dyr, p["wc"], p["wr"], sv["h"], sv["rg"], t + "branch_bwd")
    g["conv_w_proj"] = _mm_tn(sv["c4"], dyc, t + "dwc")
    g["rnn_w_proj"] = _mm_tn(q_b, dyr, t + "dwr")
    (dcv, dcg, g["conv_dw_w"], g["conv_dw_b"], g["conv_gn_g"], g["conv_gn_b"], s_cv, s_cg) = _conv_branch_bwd(
        dc4, sv["c2"], sv["cv"], sv["cg"], p["conv_dw_w"], p["conv_gn_g"], p["conv_gn_b"], t + "conv_bwd")
    guu, da = _scan_bwd(sv["a"], dh, sv["h"], t + "scan_bwd")
    dr1, dpa, dpx, g["rnn_b_a"], g["rnn_b_x"], g["rnn_lambda"] = _gates_bwd(
        guu, da, sv["ra"], sv["ri"], sv["r1"], p["wa"], p["wx"], p["rnn_lambda"], t + "gates_bwd")
    g["rnn_w_a"] = _diag_blocks(_mm_tn(sv["r1"], dpa, t + "dwa"), RNN_BLOCKS)
    g["rnn_w_x"] = _diag_blocks(_mm_tn(sv["r1"], dpx, t + "dwx"), RNN_BLOCKS)
    drx, g["rnn_conv_w"], g["rnn_conv_b"], s_rx = _short_conv_bwd(dr1, sv["rx"], p["rnn_conv_w"], t + "rconv_bwd")
    du = {"cv": dcv, "cg": dcg, "rx": drx, "rg": drg, "gc": dgc, "gr": dgr}
    order = ("cv", "cg", "rx", "rg", "gc", "gr")
    g["mix_w_in"] = jnp.concatenate([_mm_tn(sv["x1"], du[s], t + "dwin_" + s) for s in order], axis=1)
    g["mix_b_in"] = jnp.concatenate([s_cv, s_cg, s_rx, s_rg, s_gc, s_gr], axis=1)
    dx1 = _mix_dx(dz2, [du[s] for s in order], [p["win_" + s] for s in order], alpha, t + "mix_dx")
    dx0, df, a_act, dhg, dhu, g["ln1_g"], g["ln1_b"] = _ffn_bwd(dx1, sv["z1"], sv["hg1"], sv["hu1"], p["wg1"], p["wu1"], p["wd1"],
                                                                 p["ln1_g"], alpha, t + "ffn1_bwd")
    g["ffn1_w_down"] = _mm_tn(a_act, df, t + "dwd1")
    g["ffn1_w_gu"] = jnp.concatenate([_mm_tn(sv["x0"], dhg, t + "dwg1"), _mm_tn(sv["x0"], dhu, t + "dwu1")], axis=1)
    return dx0, g


def _pack_small(arrays, piece_rows):
    flat = jnp.concatenate([a.reshape(-1) for a in arrays])
    total = N_DEV * piece_rows * LANES
    return jnp.pad(flat, (0, total - flat.shape[0])).reshape(N_DEV, piece_rows, LANES)


def _unpack_small(packed, shapes):
    flat = packed.reshape(-1)
    out, off = [], 0
    for shp in shapes:
        n = 1
        for s in shp:
            n *= s
        out.append(flat[off:off + n].reshape(shp))
        off += n
    return out


def kernel(x, ffn1_w_gu, ffn1_w_down, ln1_g, ln1_b, mix_w_in, mix_b_in, conv_dw_w, conv_dw_b, conv_gn_g, conv_gn_b, conv_w_proj, rnn_conv_w, rnn_conv_b, rnn_w_a, rnn_b_a, rnn_w_x, rnn_b_x, rnn_lambda, rnn_w_proj, mix_w_out, ln2_g, ln2_b, ffn2_w_gu, ffn2_w_down, ln3_g, ln3_b, loss_target, m_ffn1_w_gu, m_ffn1_w_down, m_ln1_g, m_ln1_b, m_mix_w_in, m_mix_b_in, m_conv_dw_w, m_conv_dw_b, m_conv_gn_g, m_conv_gn_b, m_conv_w_proj, m_rnn_conv_w, m_rnn_conv_b, m_rnn_w_a, m_rnn_b_a, m_rnn_w_x, m_rnn_b_x, m_rnn_lambda, m_rnn_w_proj, m_mix_w_out, m_ln2_g, m_ln2_b, m_ffn2_w_gu, m_ffn2_w_down, m_ln3_g, m_ln3_b, v_ffn1_w_gu, v_ffn1_w_down, v_ln1_g, v_ln1_b, v_mix_w_in, v_mix_b_in, v_conv_dw_w, v_conv_dw_b, v_conv_gn_g, v_conv_gn_b, v_conv_w_proj, v_rnn_conv_w, v_rnn_conv_b, v_rnn_w_a, v_rnn_b_a, v_rnn_w_x, v_rnn_b_x, v_rnn_lambda, v_rnn_w_proj, v_mix_w_out, v_ln2_g, v_ln2_b, v_ffn2_w_gu, v_ffn2_w_down, v_ln3_g, v_ln3_b):
    args = locals()
    W = {n: args[n] for n in WEIGHT_NAMES}
    M = {n: args["m_" + n] for n in WEIGHT_NAMES}
    V = {n: args["v_" + n] for n in WEIGHT_NAMES}
    depth = ln1_g.shape[0]
    assert depth == 2, "each core of a chip moves one layer's weights and gradients"
    alpha = float((2 * depth) ** 0.25)
    S, D = x.shape[1], x.shape[2]
    F = ffn1_w_down.shape[1] * N_CHIPS
    R = rnn_w_proj.shape[1] * N_CHIPS
    chip = 2 * lax.axis_index("x") + lax.axis_index("y")

    srcs = [W[n].astype(BF16) for n in SHARDED_MATS] + [W[n] for n in SHARDED_VECS]
    gathered = dict(zip(SHARDED_MATS + SHARDED_VECS, _chip_exchange(srcs, True, "gather_weights")))

    sections = (("cv", 0, D), ("cg", D, D), ("rx", 2 * D, R), ("rg", 2 * D + R, R), ("gc", 2 * D + 2 * R, D),
                ("gr", 3 * D + 2 * R, D))
    params = []
    for l in range(depth):
        p = {}
        for tag, name in (("1", "ffn1_w_gu"), ("2", "ffn2_w_gu")):
            full = _unshard_cols(gathered[name][:, l])
            p["wg" + tag], p["wu" + tag] = full[:, :F], full[:, F:]
        p["wd1"] = gathered["ffn1_w_down"][:, l].reshape(F, D)
        p["wd2"] = gathered["ffn2_w_down"][:, l].reshape(F, D)
        w_in = _unshard_cols(gathered["mix_w_in"][:, l])
        for s, off, width in sections:
            p["win_" + s] = w_in[:, off:off + width]
            p["bin_" + s] = _row(mix_b_in[l, off:off + width])
        p["wc"] = gathered["conv_w_proj"][:, l].reshape(D, D)
        p["wr"] = gathered["rnn_w_proj"][:, l].reshape(R, D)
        p["wo"] = gathered["mix_w_out"][:, l].reshape(D, D)
        p["conv_dw_w"] = _unshard_cols(gathered["conv_dw_w"][:, l])
        p["rnn_conv_w"] = _unshard_cols(gathered["rnn_conv_w"][:, l])
        p["wa"] = _block_diag(rnn_w_a[l]).astype(BF16)
        p["wx"] = _block_diag(rnn_w_x[l]).astype(BF16)
        for n in ("ln1_g", "ln1_b", "ln2_g", "ln2_b", "ln3_g", "ln3_b", "conv_dw_b", "conv_gn_g", "conv_gn_b", "rnn_conv_b",
                  "rnn_b_a", "rnn_b_x", "rnn_lambda"):
            p[n] = _row(W[n][l])
        params.append(p)

    h = x[0]
    saved = []
    for l in range(depth):
        h, sv = _layer_forward(h, params[l], alpha, l)
        saved.append(sv)
    loss_part, dy = _loss_head(h, loss_target[0], "loss_head")
    loss = lax.psum(loss_part[0, 0], ("x", "y", "c"))
    grads = [None] * depth
    for l in reversed(range(depth)):
        dy, grads[l] = _layer_backward(dy, params[l], saved[l], alpha, l)
    grad_x = dy[None]

    def chip_major(name, l):
        gl = grads[l][name]
        if name in COL_SHARDED:
            return _shard_cols(gl)
        return gl.reshape((N_CHIPS, gl.shape[0] // N_CHIPS) + gl.shape[1:])

    stacked = [jnp.stack([chip_major(n, l) for l in range(depth)], axis=1) for n in SHARDED_MATS]
    mine, theirs = _sibling_exchange(stacked, "pair_exchange")
    chip_sums = []
    for n, a, b in zip(SHARDED_MATS, mine, theirs):
        k4, r, c = a.shape
        chip_sums.append(_pair_sum_bf16(a.reshape(k4 * r, c), b.reshape(k4 * r, c), "pair_sum_" + n).reshape(k4, r, c))
    partials = dict(zip(SHARDED_MATS, _chip_exchange(chip_sums, False, "scatter_grads")))
    out_g, out_d, out_m, out_v = {}, {}, {}, {}
    for n in SHARDED_MATS:
        out_g[n], out_d[n], out_m[n], out_v[n] = _adamw_sharded(W[n], M[n], V[n], partials[n], "adamw_" + n)

    small_grads = [jnp.stack([grads[l][n].reshape(W[n].shape[1:] if n not in SHARDED_VECS else
                                                   (W[n].shape[1], W[n].shape[2] * N_CHIPS)) for l in range(depth)])
                   for n in SMALL_NAMES]
    n_small = sum(int(a.size) for a in small_grads)
    piece_rows = -(-n_small // (N_DEV * LANES * SUBLANES)) * SUBLANES
    reduced = _unpack_small(_all_reduce_small(_pack_small(small_grads, piece_rows), "all_reduce_small"),
                            [a.shape for a in small_grads])
    local_g = []
    for n, gr in zip(SMALL_NAMES, reduced):
        if n in SHARDED_VECS:
            width = W[n].shape[2]
            gr = lax.dynamic_slice_in_dim(gr, chip * width, width, axis=2)
        local_g.append(gr)
    n_local = sum(int(a.size) for a in local_g)
    flat_rows = -(-n_local // (N_DEV * LANES * SUBLANES)) * SUBLANES * N_DEV
    pack = lambda arrs: _pack_small(arrs, flat_rows // N_DEV).reshape(flat_rows, LANES)
    shapes = [a.shape for a in local_g]
    deltas, new_m, new_v = _adamw_flat(pack([W[n] for n in SMALL_NAMES]), pack([M[n] for n in SMALL_NAMES]),
                                       pack([V[n] for n in SMALL_NAMES]), pack(local_g), "adamw_small")
    for n, gr, d_, m_, v_ in zip(SMALL_NAMES, local_g, _unpack_small(deltas, shapes), _unpack_small(new_m, shapes),
                                 _unpack_small(new_v, shapes)):
        out_g[n], out_d[n], out_m[n], out_v[n] = gr, d_, m_, v_

    return (loss, grad_x, *[out_g[n] for n in WEIGHT_NAMES], *[out_d[n] for n in WEIGHT_NAMES],
            *[out_m[n] for n in WEIGHT_NAMES], *[out_v[n] for n in WEIGHT_NAMES])
```

```python
import functools

import jax
import jax.numpy as jnp
from jax import lax
from jax.experimental import pallas as pl
from jax.experimental.pallas import tpu as pltpu

F32 = jnp.float32
BF16 = jnp.bfloat16
MESH = pl.DeviceIdType.MESH

LN_EPS = 1e-5
CONV_GROUPS = 8
RNN_BLOCKS = 16
RG_LRU_C = 8.0
ADAM_LR = 0.001
ADAM_B1 = 0.9
ADAM_B2 = 0.999
ADAM_EPS = 1e-08
ADAM_WD = 0.01
ADAM_STEP = 10

LANES = 128
SUBLANES = 8
V7X_VMEM_BYTES = 64 << 20
VMEM_LIMIT_CAP = V7X_VMEM_BYTES - (6 << 20)
N_CHIPS = 4
N_DEV = 8
CONV_ROWS = 64
EW_ROWS = 1024
SCAN_SEGMENTS = 32
SCAN_UNROLL = 4


def _cparams(block_bytes):
    limit = min(VMEM_LIMIT_CAP, max(int(block_bytes) + (8 << 20), 24 << 20))
    return pltpu.CompilerParams(vmem_limit_bytes=limit)


def _nbytes(shape, dtype):
    n = 1
    for s in shape:
        n *= s
    return n * jnp.dtype(dtype).itemsize


def _divisor_tile(n, limit, quantum):
    if n <= limit:
        return n
    best = None
    for t in range(quantum, limit + 1, quantum):
        if n % t == 0:
            best = t
    assert best is not None, (n, limit, quantum)
    return best


def _bs(shape, imap, **kw):
    return pl.BlockSpec(shape, imap, **kw)


def _resident(shape):
    nd = len(shape)
    return pl.BlockSpec(shape, lambda *_: (0,) * nd, pipeline_mode=pl.Buffered(1))


def _sigmoid(x):
    return jax.nn.sigmoid(x)


def _dot(a, b):
    return jnp.dot(a, b, preferred_element_type=F32)


def _dot_nt(a, b):
    return lax.dot_general(a, b, (((1,), (1,)), ((), ())), preferred_element_type=F32)


def _dot_tn(a, b):
    return lax.dot_general(a, b, (((0,), (0,)), ((), ())), preferred_element_type=F32)


def _row_mean(z):
    return jnp.mean(z, axis=-1, keepdims=True)


def _lane_mean(z):
    hi = z.astype(BF16)
    lo = (z - hi.astype(F32)).astype(BF16)
    ones = jnp.full((2 * LANES, LANES), 1.0 / LANES, BF16)
    return jnp.dot(jnp.concatenate([hi, lo], axis=-1), ones, preferred_element_type=F32)


def _norm_fwd(z, g, b, mean=_row_mean):
    mu = mean(z)
    xc = z - mu
    var = mean(xc * xc)
    return xc * lax.rsqrt(var + LN_EPS) * g + b


def _norm_bwd(z, g, dy, mean=_row_mean):
    mu = mean(z)
    xc = z - mu
    var = mean(xc * xc)
    rstd = lax.rsqrt(var + LN_EPS)
    xhat = xc * rstd
    dxh = dy * g
    m1 = mean(dxh)
    m2 = mean(dxh * xhat)
    return rstd * (dxh - m1 - xhat * m2), xhat


GELU_K = 0.7978845608028654
GELU_C = 0.044715


def _gelu(x):
    return 0.5 * x * (1.0 + jnp.tanh(GELU_K * (x + GELU_C * x * x * x)))


def _gelu_grad(x):
    t = jnp.tanh(GELU_K * (x + GELU_C * x * x * x))
    return 0.5 * (1.0 + t) + 0.5 * x * (1.0 - t * t) * GELU_K * (1.0 + 3.0 * GELU_C * x * x)


def _softplus(y):
    return jnp.maximum(y, 0.0) + jnp.log1p(jnp.exp(-jnp.abs(y)))


def _neg_expm1(y):
    series = -y * (1.0 + y * (0.5 + y * (1.0 / 6.0 + y * (1.0 / 24.0 + y * (1.0 / 120.0 + y * (1.0 / 720.0))))))
    return jnp.where(y > -0.25, series, 1.0 - jnp.exp(y))


def _colsum(x):
    return jnp.sum(x, axis=0, keepdims=True)


def _shifted_taps(src_ref, base, rows, taps):
    acc = None
    for o, coef in taps:
        term = coef() * src_ref[pl.ds(base + o, rows), :]
        acc = term if acc is None else acc + term
    return acc


def _shifted_corr(src_ref, base, rows, d, acc_ref, offs):
    for k, o in enumerate(offs):
        prod = d * src_ref[pl.ds(base + o, rows), :]
        part = jnp.sum(prod.reshape(rows // SUBLANES, SUBLANES, prod.shape[-1]), axis=0)
        acc_ref[SUBLANES * k:SUBLANES * (k + 1), :] += part


def _front_pad(ktaps):
    return SUBLANES * ((ktaps - 1 + SUBLANES - 1) // SUBLANES)


def _pad_rows(ktaps):
    return _front_pad(ktaps) + SUBLANES


def _ffn_tiles(S, F):
    tm = _divisor_tile(S, 1024, 16)
    tf = _divisor_tile(F, 256, LANES)
    return tm, tf


def _ffn_fwd(x, wg, wu, wd, g, b, alpha, name):
    S, D = x.shape
    F = wd.shape[0]
    tm, tf = _ffn_tiles(S, F)
    nf = F // tf

    def body(x_ref, wg_ref, wu_ref, wd_ref, g_ref, b_ref, y_ref, z_ref, hg_ref, hu_ref, acc_ref, xb_ref):
        j = pl.program_id(1)

        @pl.when(j == 0)
        def _():
            xb_ref[...] = x_ref[...].astype(BF16)
            acc_ref[...] = jnp.zeros_like(acc_ref)

        xb = xb_ref[...]
        hg = _dot(xb, wg_ref[...])
        hu = _dot(xb, wu_ref[...])
        hg_ref[...] = hg
        hu_ref[...] = hu
        a = (hg * _sigmoid(hg) * hu).astype(BF16)
        acc_ref[...] += _dot(a, wd_ref[...])

        @pl.when(j == nf - 1)
        def _():
            z = alpha * x_ref[...] + 0.5 * acc_ref[...]
            z_ref[...] = z
            y_ref[...] = _norm_fwd(z, g_ref[...], b_ref[...])

    blk = 2 * (3 * tm * D * 4 + 2 * tm * tf * 4 + 3 * D * tf * 2) + tm * D * 6 + 3 * tm * tf * 4
    return pl.pallas_call(
        body, name=name, grid=(S // tm, nf),
        in_specs=[_bs((tm, D), lambda i, j: (i, 0)), _bs((D, tf), lambda i, j: (0, j)), _bs((D, tf), lambda i, j: (0, j)),
                  _bs((tf, D), lambda i, j: (j, 0)), _bs((1, D), lambda i, j: (0, 0)), _bs((1, D), lambda i, j: (0, 0))],
        out_specs=[_bs((tm, D), lambda i, j: (i, 0)), _bs((tm, D), lambda i, j: (i, 0)),
                   _bs((tm, tf), lambda i, j: (i, j)), _bs((tm, tf), lambda i, j: (i, j))],
        out_shape=[jax.ShapeDtypeStruct((S, D), F32), jax.ShapeDtypeStruct((S, D), F32),
                   jax.ShapeDtypeStruct((S, F), F32), jax.ShapeDtypeStruct((S, F), F32)],
        scratch_shapes=[pltpu.VMEM((tm, D), F32), pltpu.VMEM((tm, D), BF16)],
        compiler_params=_cparams(blk),
    )(x, wg, wu, wd, g, b)


def _ffn_bwd(dy, z, hg, hu, wg, wu, wd, g, alpha, name):
    S, D = dy.shape
    F = wd.shape[0]
    tm, tf = _ffn_tiles(S, F)
    nf = F // tf

    def body(dy_ref, z_ref, hg_ref, hu_ref, wg_ref, wu_ref, wd_ref, g_ref,
             dx_ref, df_ref, a_ref, dhg_ref, dhu_ref, dg_ref, db_ref, acc_ref):
        i = pl.program_id(0)
        j = pl.program_id(1)

        @pl.when((i == 0) & (j == 0))
        def _():
            dg_ref[...] = jnp.zeros_like(dg_ref)
            db_ref[...] = jnp.zeros_like(db_ref)

        @pl.when(j == 0)
        def _():
            dy_ = dy_ref[...]
            dz, xhat = _norm_bwd(z_ref[...], g_ref[...], dy_)
            dg_ref[...] += _colsum(dy_ * xhat)
            db_ref[...] += _colsum(dy_)
            acc_ref[...] = alpha * dz
            df_ref[...] = (0.5 * dz).astype(BF16)

        da = _dot_nt(df_ref[...], wd_ref[...])
        hg_ = hg_ref[...]
        hu_ = hu_ref[...]
        s = _sigmoid(hg_)
        sl = hg_ * s
        dgate = (da * hu_ * (s * (1.0 + hg_ * (1.0 - s)))).astype(BF16)
        dup = (da * sl).astype(BF16)
        a_ref[...] = (sl * hu_).astype(BF16)
        dhg_ref[...] = dgate
        dhu_ref[...] = dup
        acc_ref[...] += _dot_nt(dgate, wg_ref[...]) + _dot_nt(dup, wu_ref[...])

        @pl.when(j == nf - 1)
        def _():
            dx_ref[...] = acc_ref[...]

    blk = 2 * (2 * tm * D * 4 + tm * D * 2 + 2 * tm * tf * 4 + 3 * tm * tf * 2 + 3 * D * tf * 2) + 3 * tm * D * 4 + 8 * tm * tf * 4
    once = dict(pipeline_mode=pl.Buffered(1))
    return pl.pallas_call(
        body, name=name, grid=(S // tm, nf),
        in_specs=[_bs((tm, D), lambda i, j: (i, 0), **once), _bs((tm, D), lambda i, j: (i, 0), **once),
                  _bs((tm, tf), lambda i, j: (i, j)), _bs((tm, tf), lambda i, j: (i, j)),
                  _bs((D, tf), lambda i, j: (0, j)), _bs((D, tf), lambda i, j: (0, j)), _bs((tf, D), lambda i, j: (j, 0)),
                  _bs((1, D), lambda i, j: (0, 0))],
        out_specs=[_bs((tm, D), lambda i, j: (i, 0)), _bs((tm, D), lambda i, j: (i, 0)),
                   _bs((tm, tf), lambda i, j: (i, j)), _bs((tm, tf), lambda i, j: (i, j)), _bs((tm, tf), lambda i, j: (i, j)),
                   _bs((1, D), lambda i, j: (0, 0)), _bs((1, D), lambda i, j: (0, 0))],
        out_shape=[jax.ShapeDtypeStruct((S, D), F32), jax.ShapeDtypeStruct((S, D), BF16),
                   jax.ShapeDtypeStruct((S, F), BF16), jax.ShapeDtypeStruct((S, F), BF16), jax.ShapeDtypeStruct((S, F), BF16),
                   jax.ShapeDtypeStruct((1, D), F32), jax.ShapeDtypeStruct((1, D), F32)],
        scratch_shapes=[pltpu.VMEM((tm, D), F32)],
        compiler_params=_cparams(blk),
    )(dy, z, hg, hu, wg, wu, wd, g)


def _mm_bias(x, w, bias, name):
    S, K = x.shape
    N = w.shape[1]
    tm = _divisor_tile(S, 512, 16)

    def body(x_ref, w_ref, b_ref, o_ref):
        o_ref[...] = _dot(x_ref[...].astype(BF16), w_ref[...]) + b_ref[...]

    blk = 2 * (tm * K * 4 + tm * N * 4) + K * N * 2 + tm * K * 2
    return pl.pallas_call(
        body, name=name, grid=(S // tm,),
        in_specs=[_bs((tm, K), lambda i: (i, 0)), _resident((K, N)), _resident((1, N))],
        out_specs=_bs((tm, N), lambda i: (i, 0)),
        out_shape=jax.ShapeDtypeStruct((S, N), F32),
        compiler_params=_cparams(blk),
    )(x, w, bias)


def _mm_tn(a, b, name):
    S, M = a.shape
    N = b.shape[1]
    bm = _divisor_tile(M, 1408, LANES)
    bn = _divisor_tile(N, 1408, LANES)
    tk = _divisor_tile(S, 512, 16)

    def body(a_ref, b_ref, o_ref):
        @pl.when(pl.program_id(2) == 0)
        def _():
            o_ref[...] = jnp.zeros_like(o_ref)

        o_ref[...] += _dot_tn(a_ref[...].astype(BF16), b_ref[...].astype(BF16))

    blk = 2 * (tk * bm * a.dtype.itemsize + tk * bn * b.dtype.itemsize + bm * bn * 4) + tk * bm * 4 + bm * bn * 4
    return pl.pallas_call(
        body, name=name, grid=(M // bm, N // bn, S // tk),
        in_specs=[_bs((tk, bm), lambda i, j, k: (k, i)), _bs((tk, bn), lambda i, j, k: (k, j))],
        out_specs=_bs((bm, bn), lambda i, j, k: (i, j)),
        out_shape=jax.ShapeDtypeStruct((M, N), F32),
        compiler_params=_cparams(blk),
    )(a, b)


def _mix_dx(dz, parts, weights, alpha, name):
    S, D = dz.shape
    tm = _divisor_tile(S, 256, 16)
    n = len(parts)

    def body(*refs):
        dz_ref = refs[0]
        p_refs = refs[1:1 + n]
        w_refs = refs[1 + n:1 + 2 * n]
        o_ref = refs[1 + 2 * n]
        acc = alpha * dz_ref[...]
        for p_ref, w_ref in zip(p_refs, w_refs):
            acc = acc + _dot_nt(p_ref[...], w_ref[...])
        o_ref[...] = acc

    widths = [p.shape[1] for p in parts]
    blk = 2 * (2 * tm * D * 4 + sum(tm * w * 2 for w in widths)) + sum(D * w * 2 for w in widths) + 2 * tm * D * 4
    return pl.pallas_call(
        body, name=name, grid=(S // tm,),
        in_specs=[_bs((tm, D), lambda i: (i, 0))] + [_bs((tm, w), lambda i: (i, 0)) for w in widths]
                 + [_resident((D, w)) for w in widths],
        out_specs=_bs((tm, D), lambda i: (i, 0)),
        out_shape=jax.ShapeDtypeStruct((S, D), F32),
        compiler_params=_cparams(blk),
    )(dz, *parts, *weights)


def _conv_branch_fwd(cv, cg, w, b, gg, gb, name):
    S, C = cv.shape
    K = w.shape[0]
    assert C // CONV_GROUPS == LANES
    padf = _front_pad(K)
    R = min(CONV_ROWS, S)
    E = min(EW_ROWS, S)

    def body(cv_ref, cg_ref, w_ref, b_ref, gg_ref, gb_ref, c2_ref, c4_ref, pad_ref):
        pad_ref[0:padf, :] = jnp.zeros((padf, LANES), F32)
        pad_ref[S + padf:S + padf + SUBLANES, :] = jnp.zeros((SUBLANES, LANES), F32)

        def fill(i, carry):
            r = pl.multiple_of(i * E, E)
            pad_ref[pl.ds(r + padf, E), :] = cv_ref[pl.ds(r, E), :] * _sigmoid(cg_ref[pl.ds(r, E), :])
            return carry

        lax.fori_loop(0, S // E, fill, 0)
        taps = [(padf - (K - 1) + k, functools.partial(lambda k: w_ref[k:k + 1, :], k)) for k in range(K)]

        def conv(i, carry):
            r = pl.multiple_of(i * R, R)
            c2_ref[pl.ds(r, R), :] = _shifted_taps(pad_ref, r, R, taps) + b_ref[...]
            return carry

        lax.fori_loop(0, S // R, conv, 0)

        def norm(i, carry):
            r = pl.multiple_of(i * E, E)
            c3 = _norm_fwd(c2_ref[pl.ds(r, E), :], gg_ref[...], gb_ref[...], _lane_mean)
            c4_ref[pl.ds(r, E), :] = (c3 * _sigmoid(c3)).astype(BF16)
            return carry

        lax.fori_loop(0, S // E, norm, 0)

    col = lambda i: (0, i)
    blk = 2 * (3 * S * LANES * 4 + S * LANES * 2) + (S + _pad_rows(K)) * LANES * 4
    return pl.pallas_call(
        body, name=name, grid=(C // LANES,),
        in_specs=[_bs((S, LANES), col), _bs((S, LANES), col), _bs((K, LANES), col),
                  _bs((1, LANES), col), _bs((1, LANES), col), _bs((1, LANES), col)],
        out_specs=[_bs((S, LANES), col), _bs((S, LANES), col)],
        out_shape=[jax.ShapeDtypeStruct((S, C), F32), jax.ShapeDtypeStruct((S, C), BF16)],
        scratch_shapes=[pltpu.VMEM((S + _pad_rows(K), LANES), F32)],
        compiler_params=_cparams(blk),
    )(cv, cg, w, b, gg, gb)


def _conv_branch_bwd(dc4, c2, cv, cg, w, gg, gb, name):
    S, C = cv.shape
    K = w.shape[0]
    padf = _front_pad(K)
    R = min(CONV_ROWS, S)
    E = min(EW_ROWS, S)

    def body(dc4_ref, c2_ref, cv_ref, cg_ref, w_ref, gg_ref, gb_ref,
             dcv_ref, dcg_ref, dw_ref, dwb_ref, dgg_ref, dgb_ref, scv_ref, scg_ref,
             dpad_ref, cpad_ref, dwacc_ref):
        cpad_ref[0:padf, :] = jnp.zeros((padf, LANES), F32)
        cpad_ref[S + padf:S + padf + SUBLANES, :] = jnp.zeros((SUBLANES, LANES), F32)
        dpad_ref[S:S + padf + SUBLANES, :] = jnp.zeros((padf + SUBLANES, LANES), F32)
        dwacc_ref[...] = jnp.zeros_like(dwacc_ref)
        for ref in (dwb_ref, dgg_ref, dgb_ref, scv_ref, scg_ref):
            ref[...] = jnp.zeros_like(ref)

        def norm_pass(i, carry):
            r = pl.multiple_of(i * E, E)
            g_ = gg_ref[...]
            c2 = c2_ref[pl.ds(r, E), :]
            xc = c2 - _lane_mean(c2)
            rstd = lax.rsqrt(_lane_mean(xc * xc) + LN_EPS)
            xhat = xc * rstd
            c3 = xhat * g_ + gb_ref[...]
            s = _sigmoid(c3)
            dc3 = dc4_ref[pl.ds(r, E), :] * (s * (1.0 + c3 * (1.0 - s)))
            dgg_ref[...] += _colsum(dc3 * xhat)
            dgb_ref[...] += _colsum(dc3)
            dxh = dc3 * g_
            dc2 = rstd * (dxh - _lane_mean(dxh) - xhat * _lane_mean(dxh * xhat))
            dpad_ref[pl.ds(r, E), :] = dc2
            dwb_ref[...] += _colsum(dc2)
            cpad_ref[pl.ds(r + padf, E), :] = cv_ref[pl.ds(r, E), :] * _sigmoid(cg_ref[pl.ds(r, E), :])
            return carry

        lax.fori_loop(0, S // E, norm_pass, 0)
        taps = [(K - 1 - k, functools.partial(lambda k: w_ref[k:k + 1, :], k)) for k in range(K)]
        offs = [padf - (K - 1) + k for k in range(K)]

        def conv_pass(i, carry):
            r = pl.multiple_of(i * R, R)
            dc1 = _shifted_taps(dpad_ref, r, R, taps)
            sg = _sigmoid(cg_ref[pl.ds(r, R), :])
            cv_ = cv_ref[pl.ds(r, R), :]
            dcv = dc1 * sg
            dcg = dc1 * cv_ * sg * (1.0 - sg)
            dcv_ref[pl.ds(r, R), :] = dcv.astype(BF16)
            dcg_ref[pl.ds(r, R), :] = dcg.astype(BF16)
            scv_ref[...] += _colsum(dcv)
            scg_ref[...] += _colsum(dcg)
            _shifted_corr(cpad_ref, r, R, dpad_ref[pl.ds(r, R), :], dwacc_ref, offs)
            return carry

        lax.fori_loop(0, S // R, conv_pass, 0)
        for k in range(K):
            dw_ref[k:k + 1, :] = _colsum(dwacc_ref[SUBLANES * k:SUBLANES * (k + 1), :])

    col = lambda i: (0, i)
    row = jax.ShapeDtypeStruct((1, C), F32)
    blk = 2 * (4 * S * LANES * 4 + 2 * S * LANES * 2) + 2 * (S + _pad_rows(K)) * LANES * 4
    return pl.pallas_call(
        body, name=name, grid=(C // LANES,),
        in_specs=[_bs((S, LANES), col)] * 4 + [_bs((K, LANES), col), _bs((1, LANES), col), _bs((1, LANES), col)],
        out_specs=[_bs((S, LANES), col), _bs((S, LANES), col), _bs((K, LANES), col)] + [_bs((1, LANES), col)] * 5,
        out_shape=[jax.ShapeDtypeStruct((S, C), BF16), jax.ShapeDtypeStruct((S, C), BF16),
                   jax.ShapeDtypeStruct((K, C), F32), row, row, row, row, row],
        scratch_shapes=[pltpu.VMEM((S + _pad_rows(K), LANES), F32), pltpu.VMEM((S + _pad_rows(K), LANES), F32),
                        pltpu.VMEM((SUBLANES * K, LANES), F32)],
        compiler_params=_cparams(blk),
    )(dc4, c2, cv, cg, w, gg, gb)


def _short_conv_fwd(xin, w, b, name):
    S, C = xin.shape
    K = w.shape[0]
    padf = _front_pad(K)
    R = min(CONV_ROWS, S)
    E = min(EW_ROWS, S)

    def body(x_ref, w_ref, b_ref, o_ref, pad_ref):
        pad_ref[0:padf, :] = jnp.zeros((padf, LANES), F32)
        pad_ref[S + padf:S + padf + SUBLANES, :] = jnp.zeros((SUBLANES, LANES), F32)

        def fill(i, carry):
            r = pl.multiple_of(i * E, E)
            pad_ref[pl.ds(r + padf, E), :] = x_ref[pl.ds(r, E), :]
            return carry

        lax.fori_loop(0, S // E, fill, 0)
        taps = [(padf - (K - 1) + k, functools.partial(lambda k: w_ref[k:k + 1, :], k)) for k in range(K)]

        def conv(i, carry):
            r = pl.multiple_of(i * R, R)
            o_ref[pl.ds(r, R), :] = _shifted_taps(pad_ref, r, R, taps) + b_ref[...]
            return carry

        lax.fori_loop(0, S // R, conv, 0)

    col = lambda i: (0, i)
    blk = 2 * (2 * S * LANES * 4) + (S + _pad_rows(K)) * LANES * 4
    return pl.pallas_call(
        body, name=name, grid=(C // LANES,),
        in_specs=[_bs((S, LANES), col), _bs((K, LANES), col), _bs((1, LANES), col)],
        out_specs=_bs((S, LANES), col),
        out_shape=jax.ShapeDtypeStruct((S, C), F32),
        scratch_shapes=[pltpu.VMEM((S + _pad_rows(K), LANES), F32)],
        compiler_params=_cparams(blk),
    )(xin, w, b)


def _short_conv_bwd(dy, xin, w, name):
    S, C = xin.shape
    K = w.shape[0]
    padf = _front_pad(K)
    R = min(CONV_ROWS, S)
    E = min(EW_ROWS, S)

    def body(dy_ref, x_ref, w_ref, dx_ref, dw_ref, db_ref, sx_ref, dpad_ref, xpad_ref, dwacc_ref):
        xpad_ref[0:padf, :] = jnp.zeros((padf, LANES), F32)
        xpad_ref[S + padf:S + padf + SUBLANES, :] = jnp.zeros((SUBLANES, LANES), F32)
        dpad_ref[S:S + padf + SUBLANES, :] = jnp.zeros((padf + SUBLANES, LANES), F32)
        dwacc_ref[...] = jnp.zeros_like(dwacc_ref)
        db_ref[...] = jnp.zeros_like(db_ref)
        sx_ref[...] = jnp.zeros_like(sx_ref)

        def fill(i, carry):
            r = pl.multiple_of(i * E, E)
            d = dy_ref[pl.ds(r, E), :]
            dpad_ref[pl.ds(r, E), :] = d
            db_ref[...] += _colsum(d)
            xpad_ref[pl.ds(r + padf, E), :] = x_ref[pl.ds(r, E), :]
            return carry

        lax.fori_loop(0, S // E, fill, 0)
        taps = [(K - 1 - k, functools.partial(lambda k: w_ref[k:k + 1, :], k)) for k in range(K)]
        offs = [padf - (K - 1) + k for k in range(K)]

        def conv_pass(i, carry):
            r = pl.multiple_of(i * R, R)
            dx = _shifted_taps(dpad_ref, r, R, taps)
            dx_ref[pl.ds(r, R), :] = dx.astype(BF16)
            sx_ref[...] += _colsum(dx)
            _shifted_corr(xpad_ref, r, R, dpad_ref[pl.ds(r, R), :], dwacc_ref, offs)
            return carry

        lax.fori_loop(0, S // R, conv_pass, 0)
        for k in range(K):
            dw_ref[k:k + 1, :] = _colsum(dwacc_ref[SUBLANES * k:SUBLANES * (k + 1), :])

    col = lambda i: (0, i)
    row = jax.ShapeDtypeStruct((1, C), F32)
    blk = 2 * (2 * S * LANES * 4 + S * LANES * 2) + 2 * (S + _pad_rows(K)) * LANES * 4
    return pl.pallas_call(
        body, name=name, grid=(C // LANES,),
        in_specs=[_bs((S, LANES), col), _bs((S, LANES), col), _bs((K, LANES), col)],
        out_specs=[_bs((S, LANES), col), _bs((K, LANES), col), _bs((1, LANES), col), _bs((1, LANES), col)],
        out_shape=[jax.ShapeDtypeStruct((S, C), BF16), jax.ShapeDtypeStruct((K, C), F32), row, row],
        scratch_shapes=[pltpu.VMEM((S + _pad_rows(K), LANES), F32), pltpu.VMEM((S + _pad_rows(K), LANES), F32),
                        pltpu.VMEM((SUBLANES * K, LANES), F32)],
        compiler_params=_cparams(blk),
    )(dy, xin, w)


def _gates_fwd(r1, wa, wx, ba, bx, lam, name):
    S, R = r1.shape
    tm = _divisor_tile(S, 256, 16)

    def body(r1_ref, wa_ref, wx_ref, ba_ref, bx_ref, lam_ref, ra_ref, ri_ref, a_ref, uu_ref):
        r1_ = r1_ref[...]
        rb = r1_.astype(BF16)
        ra = _sigmoid(_dot(rb, wa_ref[...]) + ba_ref[...])
        ri = _sigmoid(_dot(rb, wx_ref[...]) + bx_ref[...])
        log_a = -RG_LRU_C * ra * _softplus(-lam_ref[...])
        ra_ref[...] = ra
        ri_ref[...] = ri
        a_ref[...] = jnp.exp(log_a)
        uu_ref[...] = jnp.sqrt(_neg_expm1(2.0 * log_a)) * (ri * r1_)

    blk = 2 * (5 * tm * R * 4) + 2 * R * R * 2 + 6 * tm * R * 4
    tile = _bs((tm, R), lambda i: (i, 0))
    return pl.pallas_call(
        body, name=name, grid=(S // tm,),
        in_specs=[tile, _resident((R, R)), _resident((R, R)), _resident((1, R)), _resident((1, R)), _resident((1, R))],
        out_specs=[tile] * 4,
        out_shape=[jax.ShapeDtypeStruct((S, R), F32)] * 4,
        compiler_params=_cparams(blk),
    )(r1, wa, wx, ba, bx, lam)


def _gates_bwd(guu, da, ra, ri, r1, wa, wx, lam, name):
    S, R = r1.shape
    tm = _divisor_tile(S, 256, 16)
    nsteps = S // tm

    def body(g_ref, da_ref, ra_ref, ri_ref, r1_ref, wa_ref, wx_ref, lam_ref,
             dr1_ref, dpa_ref, dpx_ref, dba_ref, dbx_ref, dlam_ref):
        i = pl.program_id(0)

        @pl.when(i == 0)
        def _():
            dba_ref[...] = jnp.zeros_like(dba_ref)
            dbx_ref[...] = jnp.zeros_like(dbx_ref)
            dlam_ref[...] = jnp.zeros_like(dlam_ref)

        g = g_ref[...]
        ra = ra_ref[...]
        ri = ri_ref[...]
        r1_ = r1_ref[...]
        sp = _softplus(-lam_ref[...])
        log_a = -RG_LRU_C * ra * sp
        a = jnp.exp(log_a)
        mult = jnp.sqrt(_neg_expm1(2.0 * log_a))
        d_ri = g * mult * r1_
        dr1 = g * mult * ri
        dmult = g * ri * r1_
        dlog_a = da_ref[...] * a - dmult * (a * a) / mult
        dra = dlog_a * (-RG_LRU_C * sp)
        dlam_ref[...] += _colsum(dlog_a * (-RG_LRU_C * ra))
        dpa = dra * ra * (1.0 - ra)
        dpx = d_ri * ri * (1.0 - ri)
        dba_ref[...] += _colsum(dpa)
        dbx_ref[...] += _colsum(dpx)
        dpa_b = dpa.astype(BF16)
        dpx_b = dpx.astype(BF16)
        dpa_ref[...] = dpa_b
        dpx_ref[...] = dpx_b
        dr1_ref[...] = dr1 + _dot_nt(dpa_b, wa_ref[...]) + _dot_nt(dpx_b, wx_ref[...])

        @pl.when(i == nsteps - 1)
        def _():
            dlam_ref[...] = dlam_ref[...] * (-_sigmoid(-lam_ref[...]))

    blk = 2 * (6 * tm * R * 4 + 2 * tm * R * 2) + 2 * R * R * 2 + 10 * tm * R * 4
    tile = _bs((tm, R), lambda i: (i, 0))
    rowspec = _bs((1, R), lambda i: (0, 0))
    row = jax.ShapeDtypeStruct((1, R), F32)
    return pl.pallas_call(
        body, name=name, grid=(nsteps,),
        in_specs=[tile] * 5 + [_resident((R, R)), _resident((R, R)), _resident((1, R))],
        out_specs=[tile, tile, tile, rowspec, rowspec, rowspec],
        out_shape=[jax.ShapeDtypeStruct((S, R), F32), jax.ShapeDtypeStruct((S, R), BF16), jax.ShapeDtypeStruct((S, R), BF16),
                   row, row, row],
        compiler_params=_cparams(blk),
    )(guu, da, ra, ri, r1, wa, wx, lam)


def _scan_geometry(S):
    nseg = SCAN_SEGMENTS if S % (SCAN_SEGMENTS * SUBLANES) == 0 else SUBLANES
    return nseg, S // nseg, nseg // SUBLANES


def _steps(n, step, init):
    u = SCAN_UNROLL if n % SCAN_UNROLL == 0 else 1

    def trip(t, carry):
        for k in range(u):
            carry = step(t * u + k, carry)
        return carry

    return lax.fori_loop(0, n // u, trip, init)


def _scan_fwd(a, u, name):
    S, C = a.shape
    nseg, L, V = _scan_geometry(S)

    def body(a_ref, u_ref, h_ref, e_ref, p_ref, init_ref):
        def rows(ref, v, j):
            return ref[pl.ds(v * SUBLANES * L + j, SUBLANES, stride=L), :]

        def run1(j, carry):
            hs, ps = carry
            nh, npd = [], []
            for v in range(V):
                aj = rows(a_ref, v, j)
                nh.append(aj * hs[v] + rows(u_ref, v, j))
                npd.append(aj * ps[v])
            return tuple(nh), tuple(npd)

        zero = jnp.zeros((SUBLANES, LANES), F32)
        one = jnp.ones((SUBLANES, LANES), F32)
        hs, ps = _steps(L, run1, ((zero,) * V, (one,) * V))
        for v in range(V):
            e_ref[SUBLANES * v:SUBLANES * (v + 1), :] = hs[v]
            p_ref[SUBLANES * v:SUBLANES * (v + 1), :] = ps[v]
        init_ref[0:1, :] = jnp.zeros((1, LANES), F32)
        for s in range(1, nseg):
            init_ref[s:s + 1, :] = e_ref[s - 1:s, :] + p_ref[s - 1:s, :] * init_ref[s - 1:s, :]

        def run2(j, hs):
            nh = []
            for v in range(V):
                h = rows(a_ref, v, j) * hs[v] + rows(u_ref, v, j)
                h_ref[pl.ds(v * SUBLANES * L + j, SUBLANES, stride=L), :] = h
                nh.append(h)
            return tuple(nh)

        _steps(L, run2, tuple(init_ref[SUBLANES * v:SUBLANES * (v + 1), :] for v in range(V)))

    col = lambda i: (0, i)
    blk = 2 * (3 * S * LANES * 4)
    return pl.pallas_call(
        body, name=name, grid=(C // LANES,),
        in_specs=[_bs((S, LANES), col), _bs((S, LANES), col)],
        out_specs=_bs((S, LANES), col),
        out_shape=jax.ShapeDtypeStruct((S, C), F32),
        scratch_shapes=[pltpu.VMEM((nseg, LANES), F32)] * 3,
        compiler_params=_cparams(blk),
    )(a, u)


def _scan_bwd(a, dh, h, name):
    S, C = a.shape
    nseg, L, V = _scan_geometry(S)
    E = min(EW_ROWS, S)

    def body(a_ref, d_ref, h_ref, g_ref, da_ref, apad_ref, hpad_ref, e_ref, p_ref, init_ref):
        apad_ref[S:S + SUBLANES, :] = jnp.zeros((SUBLANES, LANES), F32)
        hpad_ref[0:SUBLANES, :] = jnp.zeros((SUBLANES, LANES), F32)

        def fill(i, carry):
            r = pl.multiple_of(i * E, E)
            apad_ref[pl.ds(r, E), :] = a_ref[pl.ds(r, E), :]
            hpad_ref[pl.ds(r + SUBLANES, E), :] = h_ref[pl.ds(r, E), :]
            return carry

        lax.fori_loop(0, S // E, fill, 0)

        def rows(ref, v, j, shift=0):
            return ref[pl.ds(v * SUBLANES * L + j + shift, SUBLANES, stride=L), :]

        def run1(jj, carry):
            j = L - 1 - jj
            gs, ps = carry
            ng, npd = [], []
            for v in range(V):
                bj = rows(apad_ref, v, j, 1)
                ng.append(bj * gs[v] + rows(d_ref, v, j))
                npd.append(bj * ps[v])
            return tuple(ng), tuple(npd)

        zero = jnp.zeros((SUBLANES, LANES), F32)
        one = jnp.ones((SUBLANES, LANES), F32)
        gs, ps = _steps(L, run1, ((zero,) * V, (one,) * V))
        for v in range(V):
            e_ref[SUBLANES * v:SUBLANES * (v + 1), :] = gs[v]
            p_ref[SUBLANES * v:SUBLANES * (v + 1), :] = ps[v]
        init_ref[nseg - 1:nseg, :] = jnp.zeros((1, LANES), F32)
        for s in range(nseg - 2, -1, -1):
            init_ref[s:s + 1, :] = e_ref[s + 1:s + 2, :] + p_ref[s + 1:s + 2, :] * init_ref[s + 1:s + 2, :]

        def run2(jj, gs):
            j = L - 1 - jj
            ng = []
            for v in range(V):
                g = rows(apad_ref, v, j, 1) * gs[v] + rows(d_ref, v, j)
                g_ref[pl.ds(v * SUBLANES * L + j, SUBLANES, stride=L), :] = g
                da_ref[pl.ds(v * SUBLANES * L + j, SUBLANES, stride=L), :] = g * rows(hpad_ref, v, j, SUBLANES - 1)
                ng.append(g)
            return tuple(ng)

        _steps(L, run2, tuple(init_ref[SUBLANES * v:SUBLANES * (v + 1), :] for v in range(V)))

    col = lambda i: (0, i)
    blk = 2 * (5 * S * LANES * 4) + 2 * (S + SUBLANES) * LANES * 4
    return pl.pallas_call(
        body, name=name, grid=(C // LANES,),
        in_specs=[_bs((S, LANES), col)] * 3,
        out_specs=[_bs((S, LANES), col)] * 2,
        out_shape=[jax.ShapeDtypeStruct((S, C), F32)] * 2,
        scratch_shapes=[pltpu.VMEM((S + SUBLANES, LANES), F32), pltpu.VMEM((S + SUBLANES, LANES), F32)]
                       + [pltpu.VMEM((nseg, LANES), F32)] * 3,
        compiler_params=_cparams(blk),
    )(a, dh, h)


def _mixer_out_fwd(c4, h, rg, gc, gr, x1, wc, wr, wo, g, b, alpha, name):
    S, D = x1.shape
    R = h.shape[1]
    tm = _divisor_tile(S, 256, 16)

    def body(c4_ref, h_ref, rg_ref, gc_ref, gr_ref, x_ref, wc_ref, wr_ref, wo_ref, g_ref, b_ref,
             yc_ref, yr_ref, z_ref, y_ref):
        yc = _dot(c4_ref[...], wc_ref[...])
        q = (h_ref[...] * _gelu(rg_ref[...])).astype(BF16)
        yr = _dot(q, wr_ref[...])
        yc_ref[...] = yc
        yr_ref[...] = yr
        m = (_sigmoid(gc_ref[...]) * yc + _sigmoid(gr_ref[...]) * yr).astype(BF16)
        z = alpha * x_ref[...] + _dot(m, wo_ref[...])
        z_ref[...] = z
        y_ref[...] = _norm_fwd(z, g_ref[...], b_ref[...])

    blk = 2 * (tm * D * 2 + 2 * tm * R * 4 + 7 * tm * D * 4) + (2 * D * D + R * D) * 2 + 6 * tm * D * 4
    td = _bs((tm, D), lambda i: (i, 0))
    tr = _bs((tm, R), lambda i: (i, 0))
    return pl.pallas_call(
        body, name=name, grid=(S // tm,),
        in_specs=[td, tr, tr, td, td, td, _resident((D, D)), _resident((R, D)), _resident((D, D)),
                  _resident((1, D)), _resident((1, D))],
        out_specs=[td] * 4,
        out_shape=[jax.ShapeDtypeStruct((S, D), F32)] * 4,
        compiler_params=_cparams(blk),
    )(c4, h, rg, gc, gr, x1, wc, wr, wo, g, b)


def _mixer_out_bwd(dy, z, g, wo, yc, yr, gc, gr, name):
    S, D = dy.shape
    tm = _divisor_tile(S, 256, 16)

    def body(dy_ref, z_ref, g_ref, wo_ref, yc_ref, yr_ref, gc_ref, gr_ref,
             dz_ref, dzb_ref, m_ref, dyc_ref, dyr_ref, dgc_ref, dgr_ref, sgc_ref, sgr_ref, dg_ref, db_ref):
        @pl.when(pl.program_id(0) == 0)
        def _():
            for ref in (sgc_ref, sgr_ref, dg_ref, db_ref):
                ref[...] = jnp.zeros_like(ref)

        dy_ = dy_ref[...]
        dz, xhat = _norm_bwd(z_ref[...], g_ref[...], dy_)
        dg_ref[...] += _colsum(dy_ * xhat)
        db_ref[...] += _colsum(dy_)
        dz_ref[...] = dz
        dzb = dz.astype(BF16)
        dzb_ref[...] = dzb
        dm = _dot_nt(dzb, wo_ref[...])
        yc = yc_ref[...]
        yr = yr_ref[...]
        sc = _sigmoid(gc_ref[...])
        sr = _sigmoid(gr_ref[...])
        m_ref[...] = (sc * yc + sr * yr).astype(BF16)
        dyc_ref[...] = (dm * sc).astype(BF16)
        dyr_ref[...] = (dm * sr).astype(BF16)
        dgc = dm * yc * sc * (1.0 - sc)
        dgr = dm * yr * sr * (1.0 - sr)
        dgc_ref[...] = dgc.astype(BF16)
        dgr_ref[...] = dgr.astype(BF16)
        sgc_ref[...] += _colsum(dgc)
        sgr_ref[...] += _colsum(dgr)

    blk = 2 * (7 * tm * D * 4 + 6 * tm * D * 2) + D * D * 2 + 8 * tm * D * 4
    td = _bs((tm, D), lambda i: (i, 0))
    rowspec = _bs((1, D), lambda i: (0, 0))
    row = jax.ShapeDtypeStruct((1, D), F32)
    bfd = jax.ShapeDtypeStruct((S, D), BF16)
    return pl.pallas_call(
        body, name=name, grid=(S // tm,),
        in_specs=[td, td, _resident((1, D)), _resident((D, D)), td, td, td, td],
        out_specs=[td] * 7 + [rowspec] * 4,
        out_shape=[jax.ShapeDtypeStruct((S, D), F32), bfd, bfd, bfd, bfd, bfd, bfd, row, row, row, row],
        compiler_params=_cparams(blk),
    )(dy, z, g, wo, yc, yr, gc, gr)


def _branch_bwd(dyc, dyr, wc, wr, h, rg, name):
    S, D = dyc.shape
    R = h.shape[1]
    tm = _divisor_tile(S, 256, 16)

    def body(dyc_ref, dyr_ref, wc_ref, wr_ref, h_ref, rg_ref, dc4_ref, dh_ref, drg_ref, q_ref, srg_ref):
        @pl.when(pl.program_id(0) == 0)
        def _():
            srg_ref[...] = jnp.zeros_like(srg_ref)

        dc4_ref[...] = _dot_nt(dyc_ref[...], wc_ref[...])
        dq = _dot_nt(dyr_ref[...], wr_ref[...])
        h_ = h_ref[...]
        rg_ = rg_ref[...]
        ge = _gelu(rg_)
        dh_ref[...] = dq * ge
        drg = dq * h_ * _gelu_grad(rg_)
        drg_ref[...] = drg.astype(BF16)
        srg_ref[...] += _colsum(drg)
        q_ref[...] = (h_ * ge).astype(BF16)

    blk = 2 * (2 * tm * D * 2 + tm * D * 4 + 3 * tm * R * 4 + 2 * tm * R * 2) + (D * D + R * D) * 2 + 6 * tm * R * 4
    td = _bs((tm, D), lambda i: (i, 0))
    tr = _bs((tm, R), lambda i: (i, 0))
    return pl.pallas_call(
        body, name=name, grid=(S // tm,),
        in_specs=[td, td, _resident((D, D)), _resident((R, D)), tr, tr],
        out_specs=[td, tr, tr, tr, _bs((1, R), lambda i: (0, 0))],
        out_shape=[jax.ShapeDtypeStruct((S, D), F32), jax.ShapeDtypeStruct((S, R), F32), jax.ShapeDtypeStruct((S, R), BF16),
                   jax.ShapeDtypeStruct((S, R), BF16), jax.ShapeDtypeStruct((1, R), F32)],
        compiler_params=_cparams(blk),
    )(dyc, dyr, wc, wr, h, rg)


def _loss_head(y, target, name):
    S, D = y.shape
    tm = _divisor_tile(S, 512, 16)
    nsteps = S // tm

    def body(y_ref, t_ref, loss_ref, dy_ref, acc_ref):
        i = pl.program_id(0)

        @pl.when(i == 0)
        def _():
            acc_ref[...] = jnp.zeros_like(acc_ref)

        err = y_ref[...] - t_ref[...]
        dy_ref[...] = err * (1.0 / D)
        acc_ref[...] += _colsum(err * err)

        @pl.when(i == nsteps - 1)
        def _():
            loss_ref[...] = jnp.sum(acc_ref[...], axis=-1, keepdims=True) * (0.5 / D)

    td = _bs((tm, D), lambda i: (i, 0))
    return pl.pallas_call(
        body, name=name, grid=(nsteps,),
        in_specs=[td, td],
        out_specs=[_bs((1, 1), lambda i: (0, 0)), td],
        out_shape=[jax.ShapeDtypeStruct((1, 1), F32), jax.ShapeDtypeStruct((S, D), F32)],
        scratch_shapes=[pltpu.VMEM((1, D), F32)],
        compiler_params=_cparams(2 * 3 * tm * D * 4),
    )(y, target)


def _adamw_math(w, g, m, v):
    m = ADAM_B1 * m + (1.0 - ADAM_B1) * g
    v = ADAM_B2 * v + (1.0 - ADAM_B2) * (g * g)
    m_hat = m / (1.0 - ADAM_B1 ** ADAM_STEP)
    v_hat = v / (1.0 - ADAM_B2 ** ADAM_STEP)
    delta = -ADAM_LR * (m_hat / (jnp.sqrt(v_hat) + ADAM_EPS) + ADAM_WD * w)
    return delta, m, v


def _adamw_sharded(w, m, v, own, sib, rem, name):
    _, r, c = w.shape
    tr = _divisor_tile(r, max(16, (1 << 20) // (4 * c) // 16 * 16), 16)

    def body(w_ref, m_ref, v_ref, own_ref, sib_ref, rem_ref, g_ref, d_ref, nm_ref, nv_ref):
        mine = pl.program_id(0) == lax.axis_index("c")
        g = jnp.where(mine, own_ref[...], sib_ref[...]).astype(F32)
        for j in range(N_CHIPS - 1):
            g = g + rem_ref[j].astype(F32)
        delta, nm, nv = _adamw_math(w_ref[...], g, m_ref[...], v_ref[...])
        g_ref[...] = g
        d_ref[...] = delta
        nm_ref[...] = nm
        nv_ref[...] = nv

    tile = _bs((None, tr, c), lambda l, i: (l, i, 0))
    flat = _bs((tr, c), lambda l, i: (i, 0))
    sds = jax.ShapeDtypeStruct(w.shape, F32)
    return pl.pallas_call(
        body, name=name, grid=(2, r // tr),
        in_specs=[tile, tile, tile, flat, flat, _bs((N_CHIPS - 1, None, tr, c), lambda l, i: (0, l, i, 0))],
        out_specs=[tile] * 4,
        out_shape=[sds] * 4,
        compiler_params=_cparams(2 * (7 * tr * c * 4 + (N_CHIPS + 1) * tr * c * 2) + 6 * tr * c * 4),
    )(w, m, v, own, sib, rem)


def _adamw_flat(w, m, v, g, name):
    rows = w.shape[0]
    tr = _divisor_tile(rows, 1024, SUBLANES)

    def body(w_ref, m_ref, v_ref, g_ref, d_ref, nm_ref, nv_ref):
        delta, nm, nv = _adamw_math(w_ref[...], g_ref[...], m_ref[...], v_ref[...])
        d_ref[...] = delta
        nm_ref[...] = nm
        nv_ref[...] = nv

    tile = _bs((tr, LANES), lambda i: (i, 0))
    sds = jax.ShapeDtypeStruct(w.shape, F32)
    return pl.pallas_call(
        body, name=name, grid=(rows // tr,),
        in_specs=[tile] * 4, out_specs=[tile] * 3, out_shape=[sds] * 3,
        compiler_params=_cparams(2 * 7 * tr * LANES * 4),
    )(w, m, v, g)


def _pair_sum_bf16(g0, g1, theirs, name):
    rows, c = g0.shape
    tr = _divisor_tile(rows, max(16, (1 << 20) // (4 * c) // 16 * 16), 16)

    def body(g0_ref, g1_ref, t_ref, o_ref):
        mine = jnp.where(lax.axis_index("c") == 0, g0_ref[...], g1_ref[...])
        o_ref[...] = (mine + t_ref[...]).astype(BF16)

    tile = _bs((tr, c), lambda i: (i, 0))
    return pl.pallas_call(
        body, name=name, grid=(rows // tr,),
        in_specs=[tile, tile, tile], out_specs=tile, out_shape=jax.ShapeDtypeStruct((rows, c), BF16),
        compiler_params=_cparams(2 * 4 * tr * c * 4),
    )(g0, g1, theirs)


ANY = pl.BlockSpec(memory_space=pl.ANY)


def _mesh_position():
    return lax.axis_index("x"), lax.axis_index("y"), lax.axis_index("c")


def _other_chips():
    x, y, c = _mesh_position()
    chips = [(1 - x, y), (x, 1 - y), (1 - x, 1 - y)]
    return 2 * x + y, (x, y, 1 - c), chips, [2 * cx + cy for cx, cy in chips]


def _gather_weights(srcs, name):
    n = len(srcs)

    def body(*refs):
        src = refs[:n]
        out = refs[n:2 * n]
        send_sems, recv_sems = refs[2 * n:]
        c = lax.axis_index("c")
        me, sibling, chips, chip_ids = _other_chips()

        def remote(i, k, src_ref, dst_ref, to):
            return pltpu.make_async_remote_copy(src_ref=src_ref, dst_ref=dst_ref, send_sem=send_sems.at[i, k],
                                                recv_sem=recv_sems.at[i, k], device_id=to, device_id_type=MESH)

        started = []
        for i in range(n):
            for j in range(3):
                started.append(remote(i, j, src[i].at[c], out[i].at[c, me], (*chips[j], c)))
            for l in range(2):
                started.append(remote(i, 6 + l, src[i].at[l], out[i].at[l, me], sibling))
        for cp in started:
            cp.start()
        for i in range(n):
            for j in range(3):
                slot = out[i].at[c, chip_ids[j]]
                remote(i, j, slot, slot, sibling).wait_recv()
                fwd = remote(i, 3 + j, slot, slot, sibling)
                fwd.start()
                started.append(fwd)
        for i in range(n):
            for j in range(3):
                slot = out[i].at[1 - c, chip_ids[j]]
                remote(i, 3 + j, slot, slot, sibling).wait_recv()
            for l in range(2):
                slot = out[i].at[l, me]
                remote(i, 6 + l, slot, slot, sibling).wait_recv()
        for cp in started:
            cp.wait_send()

    out_shape = [jax.ShapeDtypeStruct((2, N_CHIPS) + s.shape[1:], s.dtype) for s in srcs]
    return pl.pallas_call(
        body, name=name,
        in_specs=[ANY] * n, out_specs=[ANY] * n, out_shape=out_shape,
        scratch_shapes=[pltpu.SemaphoreType.DMA((n, 8)), pltpu.SemaphoreType.DMA((n, 8))],
    )(*srcs)


def _scatter_grads(csums, name):
    n = len(csums)

    def body(*refs):
        src = refs[:n]
        rem = refs[n:2 * n]
        sib = refs[2 * n:3 * n]
        send_sems, recv_sems = refs[3 * n:]
        c = lax.axis_index("c")
        me, sibling, chips, chip_ids = _other_chips()

        def remote(i, k, src_ref, dst_ref, to):
            return pltpu.make_async_remote_copy(src_ref=src_ref, dst_ref=dst_ref, send_sem=send_sems.at[i, k],
                                                recv_sem=recv_sems.at[i, k], device_id=to, device_id_type=MESH)

        started = []
        for i in range(n):
            for j in range(3):
                started.append(remote(i, j, src[i].at[chip_ids[j]], rem[i].at[j, c], (*chips[j], c)))
            started.append(remote(i, 6, src[i].at[me], sib[i], sibling))
        for cp in started:
            cp.start()
        for i in range(n):
            for j in range(3):
                slot = rem[i].at[j, c]
                remote(i, j, slot, slot, sibling).wait_recv()
                fwd = remote(i, 3 + j, slot, slot, sibling)
                fwd.start()
                started.append(fwd)
        for i in range(n):
            for j in range(3):
                slot = rem[i].at[j, 1 - c]
                remote(i, 3 + j, slot, slot, sibling).wait_recv()
            remote(i, 6, sib[i], sib[i], sibling).wait_recv()
        for cp in started:
            cp.wait_send()

    out_shape = ([jax.ShapeDtypeStruct((N_CHIPS - 1, 2) + s.shape[1:], s.dtype) for s in csums]
                 + [jax.ShapeDtypeStruct(s.shape[1:], s.dtype) for s in csums])
    outs = pl.pallas_call(
        body, name=name,
        in_specs=[ANY] * n, out_specs=[ANY] * (2 * n), out_shape=out_shape,
        scratch_shapes=[pltpu.SemaphoreType.DMA((n, 7)), pltpu.SemaphoreType.DMA((n, 7))],
    )(*csums)
    return outs[:n], outs[n:]


def _sibling_exchange(g0, g1, name):
    n = len(g0)

    def body(*refs):
        layers = (refs[:n], refs[n:2 * n])
        theirs = refs[2 * n:3 * n]
        send_sems, recv_sems = refs[3 * n:]
        x, y, c = _mesh_position()

        def remote(i, src_ref):
            return pltpu.make_async_remote_copy(src_ref=src_ref, dst_ref=theirs[i], send_sem=send_sems.at[i],
                                                recv_sem=recv_sems.at[i], device_id=(x, y, 1 - c), device_id_type=MESH)

        for keep in range(2):
            @pl.when(c == keep)
            def _():
                for i in range(n):
                    remote(i, layers[1 - keep][i]).start()

        for i in range(n):
            remote(i, layers[0][i]).wait()

    return pl.pallas_call(
        body, name=name,
        in_specs=[ANY] * (2 * n), out_specs=[ANY] * n, out_shape=[jax.ShapeDtypeStruct(g.shape, g.dtype) for g in g0],
        scratch_shapes=[pltpu.SemaphoreType.DMA((n,)), pltpu.SemaphoreType.DMA((n,))],
    )(*g0, *g1)


def _all_reduce_small(v, name):
    _, rows, _ = v.shape

    def body(v_ref, o_ref, recv_ref, send_sems, recv_sems):
        x, y, c = _mesh_position()
        me = 4 * x + 2 * y + c
        peers = []
        for d in range(1, N_DEV):
            px, py, pc = x ^ ((d >> 2) & 1), y ^ ((d >> 1) & 1), c ^ (d & 1)
            peers.append(((px, py, pc), 4 * px + 2 * py + pc))

        def remote(k, src_ref, dst_ref, to):
            return pltpu.make_async_remote_copy(src_ref=src_ref, dst_ref=dst_ref, send_sem=send_sems.at[k],
                                                recv_sem=recv_sems.at[k], device_id=to, device_id_type=MESH)

        scatter = [remote(d, v_ref.at[pid], recv_ref.at[me], to) for d, (to, pid) in enumerate(peers)]
        for cp in scatter:
            cp.start()
        recv_ref[pl.ds(me, 1)] = v_ref[pl.ds(me, 1)]
        for d, (to, pid) in enumerate(peers):
            remote(d, v_ref.at[pid], recv_ref.at[pid], to).wait_recv()
        total = recv_ref[0]
        for s in range(1, N_DEV):
            total = total + recv_ref[s]
        o_ref[pl.ds(me, 1)] = total[None]
        gather = [remote(N_DEV - 1 + d, o_ref.at[me], o_ref.at[me], to) for d, (to, pid) in enumerate(peers)]
        for cp in gather:
            cp.start()
        for d, (to, pid) in enumerate(peers):
            remote(N_DEV - 1 + d, o_ref.at[pid], o_ref.at[pid], to).wait_recv()
        for cp in scatter + gather:
            cp.wait_send()

    vm = pl.BlockSpec(memory_space=pltpu.VMEM)
    return pl.pallas_call(
        body, name=name,
        in_specs=[vm], out_specs=vm, out_shape=jax.ShapeDtypeStruct(v.shape, F32),
        scratch_shapes=[pltpu.VMEM(v.shape, F32), pltpu.SemaphoreType.DMA((2 * (N_DEV - 1),)),
                        pltpu.SemaphoreType.DMA((2 * (N_DEV - 1),))],
        compiler_params=_cparams(4 * _nbytes(v.shape, F32)),
    )(v)


SHARDED_MATS = ("ffn1_w_gu", "ffn1_w_down", "mix_w_in", "conv_w_proj", "rnn_w_proj", "mix_w_out", "ffn2_w_gu", "ffn2_w_down")
COL_SHARDED = ("ffn1_w_gu", "mix_w_in", "ffn2_w_gu")
SHARDED_VECS = ("conv_dw_w", "rnn_conv_w")
WEIGHT_NAMES = ("ffn1_w_gu", "ffn1_w_down", "ln1_g", "ln1_b", "mix_w_in", "mix_b_in", "conv_dw_w", "conv_dw_b", "conv_gn_g",
                "conv_gn_b", "conv_w_proj", "rnn_conv_w", "rnn_conv_b", "rnn_w_a", "rnn_b_a", "rnn_w_x", "rnn_b_x",
                "rnn_lambda", "rnn_w_proj", "mix_w_out", "ln2_g", "ln2_b", "ffn2_w_gu", "ffn2_w_down", "ln3_g", "ln3_b")
SMALL_NAMES = tuple(n for n in WEIGHT_NAMES if n not in SHARDED_MATS)


def _unshard_cols(gathered):
    k4, K, n = gathered.shape
    return jnp.transpose(gathered, (1, 0, 2)).reshape(K, k4 * n)


def _shard_cols(full):
    K, N = full.shape
    return jnp.transpose(full.reshape(K, N_CHIPS, N // N_CHIPS), (1, 0, 2))


def _block_diag(w):
    H, bk, _ = w.shape
    eye = jnp.eye(H, dtype=w.dtype)
    return jnp.einsum("hij,hk->hikj", w, eye).reshape(H * bk, H * bk)


def _diag_blocks(dense, H):
    bk = dense.shape[0] // H
    d4 = dense.reshape(H, bk, H, bk)
    return jnp.transpose(jnp.diagonal(d4, axis1=0, axis2=2), (2, 0, 1))


def _row(v):
    return v.reshape(1, -1)


def _layer_forward(x0, p, alpha, l):
    t = f"l{l}_"
    sv = {"x0": x0}
    x1, sv["z1"], sv["hg1"], sv["hu1"] = _ffn_fwd(x0, p["wg1"], p["wu1"], p["wd1"], p["ln1_g"], p["ln1_b"], alpha, t + "ffn1_fwd")
    sv["x1"] = x1
    sec = {}
    for s in ("cv", "cg", "rx", "rg", "gc", "gr"):
        sec[s] = _mm_bias(x1, p["win_" + s], p["bin_" + s], t + "win_" + s)
    sv.update(sec)
    sv["c2"], c4 = _conv_branch_fwd(sec["cv"], sec["cg"], p["conv_dw_w"], p["conv_dw_b"], p["conv_gn_g"], p["conv_gn_b"], t + "conv_fwd")
    sv["c4"] = c4
    r1 = _short_conv_fwd(sec["rx"], p["rnn_conv_w"], p["rnn_conv_b"], t + "rconv_fwd")
    sv["r1"] = r1
    sv["ra"], sv["ri"], a, uu = _gates_fwd(r1, p["wa"], p["wx"], p["rnn_b_a"], p["rnn_b_x"], p["rnn_lambda"], t + "gates_fwd")
    sv["a"] = a
    h = _scan_fwd(a, uu, t + "scan_fwd")
    sv["h"] = h
    sv["yc"], sv["yr"], sv["z2"], x2 = _mixer_out_fwd(c4, h, sec["rg"], sec["gc"], sec["gr"], x1, p["wc"], p["wr"], p["wo"],
                                                      p["ln2_g"], p["ln2_b"], alpha, t + "mixout_fwd")
    sv["x2"] = x2
    x3, sv["z3"], sv["hg2"], sv["hu2"] = _ffn_fwd(x2, p["wg2"], p["wu2"], p["wd2"], p["ln3_g"], p["ln3_b"], alpha, t + "ffn2_fwd")
    return x3, sv


def _layer_backward(dy, p, sv, alpha, l):
    t = f"l{l}_"
    g = {}
    dx2, df, a_act, dhg, dhu, g["ln3_g"], g["ln3_b"] = _ffn_bwd(dy, sv["z3"], sv["hg2"], sv["hu2"], p["wg2"], p["wu2"], p["wd2"],
                                                                 p["ln3_g"], alpha, t + "ffn2_bwd")
    g["ffn2_w_down"] = _mm_tn(a_act, df, t + "dwd2")
    g["ffn2_w_gu"] = jnp.concatenate([_mm_tn(sv["x2"], dhg, t + "dwg2"), _mm_tn(sv["x2"], dhu, t + "dwu2")], axis=1)
    (dz2, dz2b, m_b, dyc, dyr, dgc, dgr, s_gc, s_gr, g["ln2_g"], g["ln2_b"]) = _mixer_out_bwd(
        dx2, sv["z2"], p["ln2_g"], p["wo"], sv["yc"], sv["yr"], sv["gc"], sv["gr"], t + "mixout_bwd")
    g["mix_w_out"] = _mm_tn(m_b, dz2b, t + "dwo")
    dc4, dh, drg, q_b, s_rg = _branch_bwd(dyc, dyr, p["wc"], p["wr"], sv["h"], sv["rg"], t + "branch_bwd")
    g["conv_w_proj"] = _mm_tn(sv["c4"], dyc, t + "dwc")
    g["rnn_w_proj"] = _mm_tn(q_b, dyr, t + "dwr")
    (dcv, dcg, g["conv_dw_w"], g["conv_dw_b"], g["conv_gn_g"], g["conv_gn_b"], s_cv, s_cg) = _conv_branch_bwd(
        dc4, sv["c2"], sv["cv"], sv["cg"], p["conv_dw_w"], p["conv_gn_g"], p["conv_gn_b"], t + "conv_bwd")
    guu, da = _scan_bwd(sv["a"], dh, sv["h"], t + "scan_bwd")
    dr1, dpa, dpx, g["rnn_b_a"], g["rnn_b_x"], g["rnn_lambda"] = _gates_bwd(
        guu, da, sv["ra"], sv["ri"], sv["r1"], p["wa"], p["wx"], p["rnn_lambda"], t + "gates_bwd")
    g["rnn_w_a"] = _diag_blocks(_mm_tn(sv["r1"], dpa, t + "dwa"), RNN_BLOCKS)
    g["rnn_w_x"] = _diag_blocks(_mm_tn(sv["r1"], dpx, t + "dwx"), RNN_BLOCKS)
    drx, g["rnn_conv_w"], g["rnn_conv_b"], s_rx = _short_conv_bwd(dr1, sv["rx"], p["rnn_conv_w"], t + "rconv_bwd")
    du = {"cv": dcv, "cg": dcg, "rx": drx, "rg": drg, "gc": dgc, "gr": dgr}
    order = ("cv", "cg", "rx", "rg", "gc", "gr")
    g["mix_w_in"] = jnp.concatenate([_mm_tn(sv["x1"], du[s], t + "dwin_" + s) for s in order], axis=1)
    g["mix_b_in"] = jnp.concatenate([s_cv, s_cg, s_rx, s_rg, s_gc, s_gr], axis=1)
    dx1 = _mix_dx(dz2, [du[s] for s in order], [p["win_" + s] for s in order], alpha, t + "mix_dx")
    dx0, df, a_act, dhg, dhu, g["ln1_g"], g["ln1_b"] = _ffn_bwd(dx1, sv["z1"], sv["hg1"], sv["hu1"], p["wg1"], p["wu1"], p["wd1"],
                                                                 p["ln1_g"], alpha, t + "ffn1_bwd")
    g["ffn1_w_down"] = _mm_tn(a_act, df, t + "dwd1")
    g["ffn1_w_gu"] = jnp.concatenate([_mm_tn(sv["x0"], dhg, t + "dwg1"), _mm_tn(sv["x0"], dhu, t + "dwu1")], axis=1)
    return dx0, g


def _pack_small(arrays, piece_rows):
    flat = jnp.concatenate([a.reshape(-1) for a in arrays])
    total = N_DEV * piece_rows * LANES
    return jnp.pad(flat, (0, total - flat.shape[0])).reshape(N_DEV, piece_rows, LANES)


def _unpack_small(packed, shapes):
    flat = packed.reshape(-1)
    out, off = [], 0
    for shp in shapes:
        n = 1
        for s in shp:
            n *= s
        out.append(flat[off:off + n].reshape(shp))
        off += n
    return out


def kernel(x, ffn1_w_gu, ffn1_w_down, ln1_g, ln1_b, mix_w_in, mix_b_in, conv_dw_w, conv_dw_b, conv_gn_g, conv_gn_b, conv_w_proj, rnn_conv_w, rnn_conv_b, rnn_w_a, rnn_b_a, rnn_w_x, rnn_b_x, rnn_lambda, rnn_w_proj, mix_w_out, ln2_g, ln2_b, ffn2_w_gu, ffn2_w_down, ln3_g, ln3_b, loss_target, m_ffn1_w_gu, m_ffn1_w_down, m_ln1_g, m_ln1_b, m_mix_w_in, m_mix_b_in, m_conv_dw_w, m_conv_dw_b, m_conv_gn_g, m_conv_gn_b, m_conv_w_proj, m_rnn_conv_w, m_rnn_conv_b, m_rnn_w_a, m_rnn_b_a, m_rnn_w_x, m_rnn_b_x, m_rnn_lambda, m_rnn_w_proj, m_mix_w_out, m_ln2_g, m_ln2_b, m_ffn2_w_gu, m_ffn2_w_down, m_ln3_g, m_ln3_b, v_ffn1_w_gu, v_ffn1_w_down, v_ln1_g, v_ln1_b, v_mix_w_in, v_mix_b_in, v_conv_dw_w, v_conv_dw_b, v_conv_gn_g, v_conv_gn_b, v_conv_w_proj, v_rnn_conv_w, v_rnn_conv_b, v_rnn_w_a, v_rnn_b_a, v_rnn_w_x, v_rnn_b_x, v_rnn_lambda, v_rnn_w_proj, v_mix_w_out, v_ln2_g, v_ln2_b, v_ffn2_w_gu, v_ffn2_w_down, v_ln3_g, v_ln3_b):
    args = locals()
    W = {n: args[n] for n in WEIGHT_NAMES}
    M = {n: args["m_" + n] for n in WEIGHT_NAMES}
    V = {n: args["v_" + n] for n in WEIGHT_NAMES}
    depth = ln1_g.shape[0]
    assert depth == 2, "each core of a chip moves one layer's weights and gradients"
    alpha = float((2 * depth) ** 0.25)
    S, D = x.shape[1], x.shape[2]
    F = ffn1_w_down.shape[1] * N_CHIPS
    R = rnn_w_proj.shape[1] * N_CHIPS
    chip = 2 * lax.axis_index("x") + lax.axis_index("y")

    srcs = [W[n].astype(BF16) for n in SHARDED_MATS] + [W[n] for n in SHARDED_VECS]
    gathered = dict(zip(SHARDED_MATS + SHARDED_VECS, _gather_weights(srcs, "gather_weights")))

    sections = (("cv", 0, D), ("cg", D, D), ("rx", 2 * D, R), ("rg", 2 * D + R, R), ("gc", 2 * D + 2 * R, D),
                ("gr", 3 * D + 2 * R, D))
    params = []
    for l in range(depth):
        p = {}
        for tag, name in (("1", "ffn1_w_gu"), ("2", "ffn2_w_gu")):
            full = _unshard_cols(gathered[name][l])
            p["wg" + tag], p["wu" + tag] = full[:, :F], full[:, F:]
        p["wd1"] = gathered["ffn1_w_down"][l].reshape(F, D)
        p["wd2"] = gathered["ffn2_w_down"][l].reshape(F, D)
        w_in = _unshard_cols(gathered["mix_w_in"][l])
        for s, off, width in sections:
            p["win_" + s] = w_in[:, off:off + width]
            p["bin_" + s] = _row(mix_b_in[l, off:off + width])
        p["wc"] = gathered["conv_w_proj"][l].reshape(D, D)
        p["wr"] = gathered["rnn_w_proj"][l].reshape(R, D)
        p["wo"] = gathered["mix_w_out"][l].reshape(D, D)
        p["conv_dw_w"] = _unshard_cols(gathered["conv_dw_w"][l])
        p["rnn_conv_w"] = _unshard_cols(gathered["rnn_conv_w"][l])
        p["wa"] = _block_diag(rnn_w_a[l]).astype(BF16)
        p["wx"] = _block_diag(rnn_w_x[l]).astype(BF16)
        for n in ("ln1_g", "ln1_b", "ln2_g", "ln2_b", "ln3_g", "ln3_b", "conv_dw_b", "conv_gn_g", "conv_gn_b", "rnn_conv_b",
                  "rnn_b_a", "rnn_b_x", "rnn_lambda"):
            p[n] = _row(W[n][l])
        params.append(p)

    h = x[0]
    saved = []
    for l in range(depth):
        h, sv = _layer_forward(h, params[l], alpha, l)
        saved.append(sv)
    loss_part, dy = _loss_head(h, loss_target[0], "loss_head")
    loss = lax.psum(loss_part[0, 0], ("x", "y", "c"))
    grads = [None] * depth
    for l in reversed(range(depth)):
        dy, grads[l] = _layer_backward(dy, params[l], saved[l], alpha, l)
    grad_x = dy[None]

    def chip_major(name, l):
        gl = grads[l][name]
        if name in COL_SHARDED:
            return _shard_cols(gl)
        return gl.reshape((N_CHIPS, gl.shape[0] // N_CHIPS) + gl.shape[1:])

    g0 = [chip_major(n, 0) for n in SHARDED_MATS]
    g1 = [chip_major(n, 1) for n in SHARDED_MATS]
    theirs = _sibling_exchange(g0, g1, "pair_exchange")
    chip_sums = []
    for n, a0, a1, b in zip(SHARDED_MATS, g0, g1, theirs):
        k4, r, c = b.shape
        flat = (k4 * r, c)
        chip_sums.append(_pair_sum_bf16(a0.reshape(flat), a1.reshape(flat), b.reshape(flat), "pair_sum_" + n).reshape(k4, r, c))
    rem, sib = _scatter_grads(chip_sums, "scatter_grads")
    out_g, out_d, out_m, out_v = {}, {}, {}, {}
    for n, cs, rm, sb in zip(SHARDED_MATS, chip_sums, rem, sib):
        own = lax.dynamic_index_in_dim(cs, chip, axis=0, keepdims=False)
        out_g[n], out_d[n], out_m[n], out_v[n] = _adamw_sharded(W[n], M[n], V[n], own, sb, rm, "adamw_" + n)

    small_grads = [jnp.stack([grads[l][n].reshape(W[n].shape[1:] if n not in SHARDED_VECS else
                                                   (W[n].shape[1], W[n].shape[2] * N_CHIPS)) for l in range(depth)])
                   for n in SMALL_NAMES]
    n_small = sum(int(a.size) for a in small_grads)
    piece_rows = -(-n_small // (N_DEV * LANES * SUBLANES)) * SUBLANES
    reduced = _unpack_small(_all_reduce_small(_pack_small(small_grads, piece_rows), "all_reduce_small"),
                            [a.shape for a in small_grads])
    local_g = []
    for n, gr in zip(SMALL_NAMES, reduced):
        if n in SHARDED_VECS:
            width = W[n].shape[2]
            gr = lax.dynamic_slice_in_dim(gr, chip * width, width, axis=2)
        local_g.append(gr)
    n_local = sum(int(a.size) for a in local_g)
    flat_rows = -(-n_local // (N_DEV * LANES * SUBLANES)) * SUBLANES * N_DEV
    pack = lambda arrs: _pack_small(arrs, flat_rows // N_DEV).reshape(flat_rows, LANES)
    shapes = [a.shape for a in local_g]
    deltas, new_m, new_v = _adamw_flat(pack([W[n] for n in SMALL_NAMES]), pack([M[n] for n in SMALL_NAMES]),
                                       pack([V[n] for n in SMALL_NAMES]), pack(local_g), "adamw_small")
    for n, gr, d_, m_, v_ in zip(SMALL_NAMES, local_g, _unpack_small(deltas, shapes), _unpack_small(new_m, shapes),
                                 _unpack_small(new_v, shapes)):
        out_g[n], out_d[n], out_m[n], out_v[n] = gr, d_, m_, v_

    return (loss, grad_x, *[out_g[n] for n in WEIGHT_NAMES], *[out_d[n] for n in WEIGHT_NAMES],
            *[out_m[n] for n in WEIGHT_NAMES], *[out_v[n] for n in WEIGHT_NAMES])
```

```python
import functools

import jax
import jax.numpy as jnp
from jax import lax
from jax.experimental import pallas as pl
from jax.experimental.pallas import tpu as pltpu

F32 = jnp.float32
BF16 = jnp.bfloat16
MESH = pl.DeviceIdType.MESH

LN_EPS = 1e-5
CONV_GROUPS = 8
RNN_BLOCKS = 16
RG_LRU_C = 8.0
ADAM_LR = 0.001
ADAM_B1 = 0.9
ADAM_B2 = 0.999
ADAM_EPS = 1e-08
ADAM_WD = 0.01
ADAM_STEP = 10

LANES = 128
SUBLANES = 8
V7X_VMEM_BYTES = 64 << 20
VMEM_LIMIT_CAP = V7X_VMEM_BYTES - (6 << 20)
N_CHIPS = 4
N_DEV = 8
CONV_ROWS = 64
EW_ROWS = 1024
SCAN_SEGMENTS = 32
SCAN_UNROLL = 4


def _cparams(block_bytes):
    limit = min(VMEM_LIMIT_CAP, max(int(block_bytes) + (8 << 20), 24 << 20))
    return pltpu.CompilerParams(vmem_limit_bytes=limit)


def _nbytes(shape, dtype):
    n = 1
    for s in shape:
        n *= s
    return n * jnp.dtype(dtype).itemsize


def _divisor_tile(n, limit, quantum):
    if n <= limit:
        return n
    best = None
    for t in range(quantum, limit + 1, quantum):
        if n % t == 0:
            best = t
    assert best is not None, (n, limit, quantum)
    return best


def _bs(shape, imap, **kw):
    return pl.BlockSpec(shape, imap, **kw)


def _resident(shape):
    nd = len(shape)
    return pl.BlockSpec(shape, lambda *_: (0,) * nd, pipeline_mode=pl.Buffered(1))


def _layer_block(w, block, imap, **kw):
    arr, layer = w
    return arr, pl.BlockSpec((None,) + block, lambda *ids: (layer,) + imap(*ids), **kw)


def _layer_resident(w):
    arr, _ = w
    return _layer_block(w, arr.shape[1:], lambda *_: (0, 0), pipeline_mode=pl.Buffered(1))


def _sigmoid(x):
    return jax.nn.sigmoid(x)


def _dot(a, b):
    return jnp.dot(a, b, preferred_element_type=F32)


def _dot_nt(a, b):
    return lax.dot_general(a, b, (((1,), (1,)), ((), ())), preferred_element_type=F32)


def _dot_tn(a, b):
    return lax.dot_general(a, b, (((0,), (0,)), ((), ())), preferred_element_type=F32)


def _row_mean(z):
    return jnp.mean(z, axis=-1, keepdims=True)


def _lane_mean(z):
    hi = z.astype(BF16)
    lo = (z - hi.astype(F32)).astype(BF16)
    ones = jnp.full((2 * LANES, LANES), 1.0 / LANES, BF16)
    return jnp.dot(jnp.concatenate([hi, lo], axis=-1), ones, preferred_element_type=F32)


def _norm_fwd(z, g, b, mean=_row_mean):
    mu = mean(z)
    xc = z - mu
    var = mean(xc * xc)
    return xc * lax.rsqrt(var + LN_EPS) * g + b


def _norm_bwd(z, g, dy, mean=_row_mean):
    mu = mean(z)
    xc = z - mu
    var = mean(xc * xc)
    rstd = lax.rsqrt(var + LN_EPS)
    xhat = xc * rstd
    dxh = dy * g
    m1 = mean(dxh)
    m2 = mean(dxh * xhat)
    return rstd * (dxh - m1 - xhat * m2), xhat


GELU_K = 0.7978845608028654
GELU_C = 0.044715


def _gelu(x):
    return 0.5 * x * (1.0 + jnp.tanh(GELU_K * (x + GELU_C * x * x * x)))


def _gelu_grad(x):
    t = jnp.tanh(GELU_K * (x + GELU_C * x * x * x))
    return 0.5 * (1.0 + t) + 0.5 * x * (1.0 - t * t) * GELU_K * (1.0 + 3.0 * GELU_C * x * x)


def _softplus(y):
    return jnp.maximum(y, 0.0) + jnp.log1p(jnp.exp(-jnp.abs(y)))


def _neg_expm1(y):
    series = -y * (1.0 + y * (0.5 + y * (1.0 / 6.0 + y * (1.0 / 24.0 + y * (1.0 / 120.0 + y * (1.0 / 720.0))))))
    return jnp.where(y > -0.25, series, 1.0 - jnp.exp(y))


def _colsum(x):
    return jnp.sum(x, axis=0, keepdims=True)


def _shifted_taps(src_ref, base, rows, taps):
    acc = None
    for o, coef in taps:
        term = coef() * src_ref[pl.ds(base + o, rows), :]
        acc = term if acc is None else acc + term
    return acc


def _shifted_corr(src_ref, base, rows, d, acc_ref, offs):
    for k, o in enumerate(offs):
        prod = d * src_ref[pl.ds(base + o, rows), :]
        part = jnp.sum(prod.reshape(rows // SUBLANES, SUBLANES, prod.shape[-1]), axis=0)
        acc_ref[SUBLANES * k:SUBLANES * (k + 1), :] += part


def _front_pad(ktaps):
    return SUBLANES * ((ktaps - 1 + SUBLANES - 1) // SUBLANES)


def _pad_rows(ktaps):
    return _front_pad(ktaps) + SUBLANES


def _ffn_tiles(S, F):
    tm = _divisor_tile(S, 1024, 16)
    tf = _divisor_tile(F, 256, LANES)
    return tm, tf


def _ffn_fwd(x, wgu, wd, g, b, alpha, name):
    S, D = x.shape
    F = wd[0].shape[1]
    tm, tf = _ffn_tiles(S, F)
    nf = F // tf
    wg_arr, wg_spec = _layer_block(wgu, (D, tf), lambda i, j: (0, j))
    wu_arr, wu_spec = _layer_block(wgu, (D, tf), lambda i, j: (0, nf + j))
    wd_arr, wd_spec = _layer_block(wd, (tf, D), lambda i, j: (j, 0))

    def body(x_ref, wg_ref, wu_ref, wd_ref, g_ref, b_ref, y_ref, z_ref, hg_ref, hu_ref, acc_ref, xb_ref):
        j = pl.program_id(1)

        @pl.when(j == 0)
        def _():
            xb_ref[...] = x_ref[...].astype(BF16)
            acc_ref[...] = jnp.zeros_like(acc_ref)

        xb = xb_ref[...]
        hg = _dot(xb, wg_ref[...])
        hu = _dot(xb, wu_ref[...])
        hg_ref[...] = hg
        hu_ref[...] = hu
        a = (hg * _sigmoid(hg) * hu).astype(BF16)
        acc_ref[...] += _dot(a, wd_ref[...])

        @pl.when(j == nf - 1)
        def _():
            z = alpha * x_ref[...] + 0.5 * acc_ref[...]
            z_ref[...] = z
            y_ref[...] = _norm_fwd(z, g_ref[...], b_ref[...])

    blk = 2 * (3 * tm * D * 4 + 2 * tm * tf * 4 + 3 * D * tf * 2) + tm * D * 6 + 3 * tm * tf * 4
    return pl.pallas_call(
        body, name=name, grid=(S // tm, nf),
        in_specs=[_bs((tm, D), lambda i, j: (i, 0)), wg_spec, wu_spec, wd_spec,
                  _bs((1, D), lambda i, j: (0, 0)), _bs((1, D), lambda i, j: (0, 0))],
        out_specs=[_bs((tm, D), lambda i, j: (i, 0)), _bs((tm, D), lambda i, j: (i, 0)),
                   _bs((tm, tf), lambda i, j: (i, j)), _bs((tm, tf), lambda i, j: (i, j))],
        out_shape=[jax.ShapeDtypeStruct((S, D), F32), jax.ShapeDtypeStruct((S, D), F32),
                   jax.ShapeDtypeStruct((S, F), F32), jax.ShapeDtypeStruct((S, F), F32)],
        scratch_shapes=[pltpu.VMEM((tm, D), F32), pltpu.VMEM((tm, D), BF16)],
        compiler_params=_cparams(blk),
    )(x, wg_arr, wu_arr, wd_arr, g, b)


def _ffn_bwd(dy, z, hg, hu, wgu, wd, g, alpha, name):
    S, D = dy.shape
    F = wd[0].shape[1]
    tm, tf = _ffn_tiles(S, F)
    nf = F // tf
    wg_arr, wg_spec = _layer_block(wgu, (D, tf), lambda i, j: (0, j))
    wu_arr, wu_spec = _layer_block(wgu, (D, tf), lambda i, j: (0, nf + j))
    wd_arr, wd_spec = _layer_block(wd, (tf, D), lambda i, j: (j, 0))

    def body(dy_ref, z_ref, hg_ref, hu_ref, wg_ref, wu_ref, wd_ref, g_ref,
             dx_ref, df_ref, a_ref, dhg_ref, dhu_ref, dg_ref, db_ref, acc_ref):
        i = pl.program_id(0)
        j = pl.program_id(1)

        @pl.when((i == 0) & (j == 0))
        def _():
            dg_ref[...] = jnp.zeros_like(dg_ref)
            db_ref[...] = jnp.zeros_like(db_ref)

        @pl.when(j == 0)
        def _():
            dy_ = dy_ref[...]
            dz, xhat = _norm_bwd(z_ref[...], g_ref[...], dy_)
            dg_ref[...] += _colsum(dy_ * xhat)
            db_ref[...] += _colsum(dy_)
            acc_ref[...] = alpha * dz
            df_ref[...] = (0.5 * dz).astype(BF16)

        da = _dot_nt(df_ref[...], wd_ref[...])
        hg_ = hg_ref[...]
        hu_ = hu_ref[...]
        s = _sigmoid(hg_)
        sl = hg_ * s
        dgate = (da * hu_ * (s * (1.0 + hg_ * (1.0 - s)))).astype(BF16)
        dup = (da * sl).astype(BF16)
        a_ref[...] = (sl * hu_).astype(BF16)
        dhg_ref[...] = dgate
        dhu_ref[...] = dup
        acc_ref[...] += _dot_nt(dgate, wg_ref[...]) + _dot_nt(dup, wu_ref[...])

        @pl.when(j == nf - 1)
        def _():
            dx_ref[...] = acc_ref[...]

    blk = 2 * (2 * tm * D * 4 + tm * D * 2 + 2 * tm * tf * 4 + 3 * tm * tf * 2 + 3 * D * tf * 2) + 3 * tm * D * 4 + 8 * tm * tf * 4
    once = dict(pipeline_mode=pl.Buffered(1))
    return pl.pallas_call(
        body, name=name, grid=(S // tm, nf),
        in_specs=[_bs((tm, D), lambda i, j: (i, 0), **once), _bs((tm, D), lambda i, j: (i, 0), **once),
                  _bs((tm, tf), lambda i, j: (i, j)), _bs((tm, tf), lambda i, j: (i, j)),
                  wg_spec, wu_spec, wd_spec, _bs((1, D), lambda i, j: (0, 0))],
        out_specs=[_bs((tm, D), lambda i, j: (i, 0)), _bs((tm, D), lambda i, j: (i, 0)),
                   _bs((tm, tf), lambda i, j: (i, j)), _bs((tm, tf), lambda i, j: (i, j)), _bs((tm, tf), lambda i, j: (i, j)),
                   _bs((1, D), lambda i, j: (0, 0)), _bs((1, D), lambda i, j: (0, 0))],
        out_shape=[jax.ShapeDtypeStruct((S, D), F32), jax.ShapeDtypeStruct((S, D), BF16),
                   jax.ShapeDtypeStruct((S, F), BF16), jax.ShapeDtypeStruct((S, F), BF16), jax.ShapeDtypeStruct((S, F), BF16),
                   jax.ShapeDtypeStruct((1, D), F32), jax.ShapeDtypeStruct((1, D), F32)],
        scratch_shapes=[pltpu.VMEM((tm, D), F32)],
        compiler_params=_cparams(blk),
    )(dy, z, hg, hu, wg_arr, wu_arr, wd_arr, g)


def _mm_bias(x, wt, bias, name):
    S, K = x.shape
    N = wt.shape[0]
    tm = _divisor_tile(S, 512, 16)

    def body(x_ref, w_ref, b_ref, o_ref):
        o_ref[...] = _dot_nt(x_ref[...].astype(BF16), w_ref[...]) + b_ref[...]

    blk = 2 * (tm * K * 4 + tm * N * 4) + K * N * 2 + tm * K * 2
    return pl.pallas_call(
        body, name=name, grid=(S // tm,),
        in_specs=[_bs((tm, K), lambda i: (i, 0)), _resident((N, K)), _resident((1, N))],
        out_specs=_bs((tm, N), lambda i: (i, 0)),
        out_shape=jax.ShapeDtypeStruct((S, N), F32),
        compiler_params=_cparams(blk),
    )(x, wt, bias)


def _mm_tn(a, b, name):
    S, M = a.shape
    N = b.shape[1]
    bm = _divisor_tile(M, 1408, LANES)
    bn = _divisor_tile(N, 1408, LANES)
    tk = _divisor_tile(S, 512, 16)

    def body(a_ref, b_ref, o_ref):
        @pl.when(pl.program_id(2) == 0)
        def _():
            o_ref[...] = jnp.zeros_like(o_ref)

        o_ref[...] += _dot_tn(a_ref[...].astype(BF16), b_ref[...].astype(BF16))

    blk = 2 * (tk * bm * a.dtype.itemsize + tk * bn * b.dtype.itemsize + bm * bn * 4) + tk * bm * 4 + bm * bn * 4
    return pl.pallas_call(
        body, name=name, grid=(M // bm, N // bn, S // tk),
        in_specs=[_bs((tk, bm), lambda i, j, k: (k, i)), _bs((tk, bn), lambda i, j, k: (k, j))],
        out_specs=_bs((bm, bn), lambda i, j, k: (i, j)),
        out_shape=jax.ShapeDtypeStruct((M, N), F32),
        compiler_params=_cparams(blk),
    )(a, b)


def _mm_tn_pair(a, b0, b1, name):
    S, M = a.shape
    N = b0.shape[1]
    assert b1.shape == b0.shape
    bm = _divisor_tile(M, 1408, LANES)
    bn = _divisor_tile(N, 1408, LANES)
    tk = _divisor_tile(S, 512, 16)
    nb = N // bn

    def body(a_ref, b0_ref, b1_ref, o_ref):
        j = pl.program_id(1)

        @pl.when(pl.program_id(2) == 0)
        def _():
            o_ref[...] = jnp.zeros_like(o_ref)

        ab = a_ref[...].astype(BF16)

        @pl.when(j < nb)
        def _():
            o_ref[...] += _dot_tn(ab, b0_ref[...])

        @pl.when(j >= nb)
        def _():
            o_ref[...] += _dot_tn(ab, b1_ref[...])

    b0_map = lambda i, j, k: (jnp.where(j < nb, k, S // tk - 1), jnp.minimum(j, nb - 1))
    b1_map = lambda i, j, k: (jnp.where(j >= nb, k, 0), jnp.maximum(j - nb, 0))
    blk = 2 * (tk * bm * a.dtype.itemsize + 2 * tk * bn * 2 + bm * bn * 4) + tk * bm * 4 + bm * bn * 4
    return pl.pallas_call(
        body, name=name, grid=(M // bm, 2 * nb, S // tk),
        in_specs=[_bs((tk, bm), lambda i, j, k: (k, i)), _bs((tk, bn), b0_map), _bs((tk, bn), b1_map)],
        out_specs=_bs((bm, bn), lambda i, j, k: (i, j)),
        out_shape=jax.ShapeDtypeStruct((M, 2 * N), F32),
        compiler_params=_cparams(blk),
    )(a, b0, b1)


def _mix_dx(dz, parts, weights, alpha, name):
    S, D = dz.shape
    tm = _divisor_tile(S, 256, 16)
    n = len(parts)

    def body(*refs):
        dz_ref = refs[0]
        p_refs = refs[1:1 + n]
        w_refs = refs[1 + n:1 + 2 * n]
        o_ref = refs[1 + 2 * n]
        acc = alpha * dz_ref[...]
        for p_ref, w_ref in zip(p_refs, w_refs):
            acc = acc + _dot(p_ref[...], w_ref[...])
        o_ref[...] = acc

    widths = [p.shape[1] for p in parts]
    blk = 2 * (2 * tm * D * 4 + sum(tm * w * 2 for w in widths)) + sum(D * w * 2 for w in widths) + 2 * tm * D * 4
    return pl.pallas_call(
        body, name=name, grid=(S // tm,),
        in_specs=[_bs((tm, D), lambda i: (i, 0))] + [_bs((tm, w), lambda i: (i, 0)) for w in widths]
                 + [_resident((w, D)) for w in widths],
        out_specs=_bs((tm, D), lambda i: (i, 0)),
        out_shape=jax.ShapeDtypeStruct((S, D), F32),
        compiler_params=_cparams(blk),
    )(dz, *parts, *weights)


def _conv_branch_fwd(cv, cg, w, b, gg, gb, name):
    S, C = cv.shape
    K = w.shape[0]
    assert C // CONV_GROUPS == LANES
    padf = _front_pad(K)
    R = min(CONV_ROWS, S)
    E = min(EW_ROWS, S)

    def body(cv_ref, cg_ref, w_ref, b_ref, gg_ref, gb_ref, c2_ref, c4_ref, pad_ref):
        pad_ref[0:padf, :] = jnp.zeros((padf, LANES), F32)
        pad_ref[S + padf:S + padf + SUBLANES, :] = jnp.zeros((SUBLANES, LANES), F32)

        def fill(i, carry):
            r = pl.multiple_of(i * E, E)
            pad_ref[pl.ds(r + padf, E), :] = cv_ref[pl.ds(r, E), :] * _sigmoid(cg_ref[pl.ds(r, E), :])
            return carry

        lax.fori_loop(0, S // E, fill, 0)
        taps = [(padf - (K - 1) + k, functools.partial(lambda k: w_ref[k:k + 1, :], k)) for k in range(K)]

        def conv(i, carry):
            r = pl.multiple_of(i * R, R)
            c2_ref[pl.ds(r, R), :] = _shifted_taps(pad_ref, r, R, taps) + b_ref[...]
            return carry

        lax.fori_loop(0, S // R, conv, 0)

        def norm(i, carry):
            r = pl.multiple_of(i * E, E)
            c3 = _norm_fwd(c2_ref[pl.ds(r, E), :], gg_ref[...], gb_ref[...], _lane_mean)
            c4_ref[pl.ds(r, E), :] = (c3 * _sigmoid(c3)).astype(BF16)
            return carry

        lax.fori_loop(0, S // E, norm, 0)

    col = lambda i: (0, i)
    blk = 2 * (3 * S * LANES * 4 + S * LANES * 2) + (S + _pad_rows(K)) * LANES * 4
    return pl.pallas_call(
        body, name=name, grid=(C // LANES,),
        in_specs=[_bs((S, LANES), col), _bs((S, LANES), col), _bs((K, LANES), col),
                  _bs((1, LANES), col), _bs((1, LANES), col), _bs((1, LANES), col)],
        out_specs=[_bs((S, LANES), col), _bs((S, LANES), col)],
        out_shape=[jax.ShapeDtypeStruct((S, C), F32), jax.ShapeDtypeStruct((S, C), BF16)],
        scratch_shapes=[pltpu.VMEM((S + _pad_rows(K), LANES), F32)],
        compiler_params=_cparams(blk),
    )(cv, cg, w, b, gg, gb)


def _conv_branch_bwd(dc4, c2, cv, cg, w, gg, gb, name):
    S, C = cv.shape
    K = w.shape[0]
    padf = _front_pad(K)
    R = min(CONV_ROWS, S)
    E = min(EW_ROWS, S)

    def body(dc4_ref, c2_ref, cv_ref, cg_ref, w_ref, gg_ref, gb_ref,
             dcv_ref, dcg_ref, dw_ref, dwb_ref, dgg_ref, dgb_ref, scv_ref, scg_ref,
             dpad_ref, cpad_ref, dwacc_ref):
        cpad_ref[0:padf, :] = jnp.zeros((padf, LANES), F32)
        cpad_ref[S + padf:S + padf + SUBLANES, :] = jnp.zeros((SUBLANES, LANES), F32)
        dpad_ref[S:S + padf + SUBLANES, :] = jnp.zeros((padf + SUBLANES, LANES), F32)
        dwacc_ref[...] = jnp.zeros_like(dwacc_ref)
        for ref in (dwb_ref, dgg_ref, dgb_ref, scv_ref, scg_ref):
            ref[...] = jnp.zeros_like(ref)

        def norm_pass(i, carry):
            r = pl.multiple_of(i * E, E)
            g_ = gg_ref[...]
            c2 = c2_ref[pl.ds(r, E), :]
            xc = c2 - _lane_mean(c2)
            rstd = lax.rsqrt(_lane_mean(xc * xc) + LN_EPS)
            xhat = xc * rstd
            c3 = xhat * g_ + gb_ref[...]
            s = _sigmoid(c3)
            dc3 = dc4_ref[pl.ds(r, E), :] * (s * (1.0 + c3 * (1.0 - s)))
            dgg_ref[...] += _colsum(dc3 * xhat)
            dgb_ref[...] += _colsum(dc3)
            dxh = dc3 * g_
            dc2 = rstd * (dxh - _lane_mean(dxh) - xhat * _lane_mean(dxh * xhat))
            dpad_ref[pl.ds(r, E), :] = dc2
            dwb_ref[...] += _colsum(dc2)
            cpad_ref[pl.ds(r + padf, E), :] = cv_ref[pl.ds(r, E), :] * _sigmoid(cg_ref[pl.ds(r, E), :])
            return carry

        lax.fori_loop(0, S // E, norm_pass, 0)
        taps = [(K - 1 - k, functools.partial(lambda k: w_ref[k:k + 1, :], k)) for k in range(K)]
        offs = [padf - (K - 1) + k for k in range(K)]

        def conv_pass(i, carry):
            r = pl.multiple_of(i * R, R)
            dc1 = _shifted_taps(dpad_ref, r, R, taps)
            sg = _sigmoid(cg_ref[pl.ds(r, R), :])
            cv_ = cv_ref[pl.ds(r, R), :]
            dcv = dc1 * sg
            dcg = dc1 * cv_ * sg * (1.0 - sg)
            dcv_ref[pl.ds(r, R), :] = dcv.astype(BF16)
            dcg_ref[pl.ds(r, R), :] = dcg.astype(BF16)
            scv_ref[...] += _colsum(dcv)
            scg_ref[...] += _colsum(dcg)
            _shifted_corr(cpad_ref, r, R, dpad_ref[pl.ds(r, R), :], dwacc_ref, offs)
            return carry

        lax.fori_loop(0, S // R, conv_pass, 0)
        for k in range(K):
            dw_ref[k:k + 1, :] = _colsum(dwacc_ref[SUBLANES * k:SUBLANES * (k + 1), :])

    col = lambda i: (0, i)
    row = jax.ShapeDtypeStruct((1, C), F32)
    blk = 2 * (4 * S * LANES * 4 + 2 * S * LANES * 2) + 2 * (S + _pad_rows(K)) * LANES * 4
    return pl.pallas_call(
        body, name=name, grid=(C // LANES,),
        in_specs=[_bs((S, LANES), col)] * 4 + [_bs((K, LANES), col), _bs((1, LANES), col), _bs((1, LANES), col)],
        out_specs=[_bs((S, LANES), col), _bs((S, LANES), col), _bs((K, LANES), col)] + [_bs((1, LANES), col)] * 5,
        out_shape=[jax.ShapeDtypeStruct((S, C), BF16), jax.ShapeDtypeStruct((S, C), BF16),
                   jax.ShapeDtypeStruct((K, C), F32), row, row, row, row, row],
        scratch_shapes=[pltpu.VMEM((S + _pad_rows(K), LANES), F32), pltpu.VMEM((S + _pad_rows(K), LANES), F32),
                        pltpu.VMEM((SUBLANES * K, LANES), F32)],
        compiler_params=_cparams(blk),
    )(dc4, c2, cv, cg, w, gg, gb)


def _short_conv_fwd(xin, w, b, name):
    S, C = xin.shape
    K = w.shape[0]
    padf = _front_pad(K)
    R = min(CONV_ROWS, S)
    E = min(EW_ROWS, S)

    def body(x_ref, w_ref, b_ref, o_ref, pad_ref):
        pad_ref[0:padf, :] = jnp.zeros((padf, LANES), F32)
        pad_ref[S + padf:S + padf + SUBLANES, :] = jnp.zeros((SUBLANES, LANES), F32)

        def fill(i, carry):
            r = pl.multiple_of(i * E, E)
            pad_ref[pl.ds(r + padf, E), :] = x_ref[pl.ds(r, E), :]
            return carry

        lax.fori_loop(0, S // E, fill, 0)
        taps = [(padf - (K - 1) + k, functools.partial(lambda k: w_ref[k:k + 1, :], k)) for k in range(K)]

        def conv(i, carry):
            r = pl.multiple_of(i * R, R)
            o_ref[pl.ds(r, R), :] = _shifted_taps(pad_ref, r, R, taps) + b_ref[...]
            return carry

        lax.fori_loop(0, S // R, conv, 0)

    col = lambda i: (0, i)
    blk = 2 * (2 * S * LANES * 4) + (S + _pad_rows(K)) * LANES * 4
    return pl.pallas_call(
        body, name=name, grid=(C // LANES,),
        in_specs=[_bs((S, LANES), col), _bs((K, LANES), col), _bs((1, LANES), col)],
        out_specs=_bs((S, LANES), col),
        out_shape=jax.ShapeDtypeStruct((S, C), F32),
        scratch_shapes=[pltpu.VMEM((S + _pad_rows(K), LANES), F32)],
        compiler_params=_cparams(blk),
    )(xin, w, b)


def _short_conv_bwd(dy, xin, w, name):
    S, C = xin.shape
    K = w.shape[0]
    padf = _front_pad(K)
    R = min(CONV_ROWS, S)
    E = min(EW_ROWS, S)

    def body(dy_ref, x_ref, w_ref, dx_ref, dw_ref, db_ref, sx_ref, dpad_ref, xpad_ref, dwacc_ref):
        xpad_ref[0:padf, :] = jnp.zeros((padf, LANES), F32)
        xpad_ref[S + padf:S + padf + SUBLANES, :] = jnp.zeros((SUBLANES, LANES), F32)
        dpad_ref[S:S + padf + SUBLANES, :] = jnp.zeros((padf + SUBLANES, LANES), F32)
        dwacc_ref[...] = jnp.zeros_like(dwacc_ref)
        db_ref[...] = jnp.zeros_like(db_ref)
        sx_ref[...] = jnp.zeros_like(sx_ref)

        def fill(i, carry):
            r = pl.multiple_of(i * E, E)
            d = dy_ref[pl.ds(r, E), :]
            dpad_ref[pl.ds(r, E), :] = d
            db_ref[...] += _colsum(d)
            xpad_ref[pl.ds(r + padf, E), :] = x_ref[pl.ds(r, E), :]
            return carry

        lax.fori_loop(0, S // E, fill, 0)
        taps = [(K - 1 - k, functools.partial(lambda k: w_ref[k:k + 1, :], k)) for k in range(K)]
        offs = [padf - (K - 1) + k for k in range(K)]

        def conv_pass(i, carry):
            r = pl.multiple_of(i * R, R)
            dx = _shifted_taps(dpad_ref, r, R, taps)
            dx_ref[pl.ds(r, R), :] = dx.astype(BF16)
            sx_ref[...] += _colsum(dx)
            _shifted_corr(xpad_ref, r, R, dpad_ref[pl.ds(r, R), :], dwacc_ref, offs)
            return carry

        lax.fori_loop(0, S // R, conv_pass, 0)
        for k in range(K):
            dw_ref[k:k + 1, :] = _colsum(dwacc_ref[SUBLANES * k:SUBLANES * (k + 1), :])

    col = lambda i: (0, i)
    row = jax.ShapeDtypeStruct((1, C), F32)
    blk = 2 * (2 * S * LANES * 4 + S * LANES * 2) + 2 * (S + _pad_rows(K)) * LANES * 4
    return pl.pallas_call(
        body, name=name, grid=(C // LANES,),
        in_specs=[_bs((S, LANES), col), _bs((S, LANES), col), _bs((K, LANES), col)],
        out_specs=[_bs((S, LANES), col), _bs((K, LANES), col), _bs((1, LANES), col), _bs((1, LANES), col)],
        out_shape=[jax.ShapeDtypeStruct((S, C), BF16), jax.ShapeDtypeStruct((K, C), F32), row, row],
        scratch_shapes=[pltpu.VMEM((S + _pad_rows(K), LANES), F32), pltpu.VMEM((S + _pad_rows(K), LANES), F32),
                        pltpu.VMEM((SUBLANES * K, LANES), F32)],
        compiler_params=_cparams(blk),
    )(dy, xin, w)


def _band_panels(width, block):
    assert width % LANES == 0 and block <= LANES
    panels = []
    for c0 in range(0, width, 2 * LANES):
        c1 = min(width, c0 + 2 * LANES)
        r0 = (c0 // block) * block // LANES * LANES
        r1 = min(width, -(-(-(-c1 // block) * block) // LANES) * LANES)
        panels.append((r0, r1, c0, c1))
    return panels


def _gates_fwd(r1, wa, wx, ba, bx, lam, name):
    S, R = r1.shape
    tm = _divisor_tile(S, 256, 16)
    panels = _band_panels(R, R // RNN_BLOCKS)

    def body(r1_ref, wa_ref, wx_ref, ba_ref, bx_ref, lam_ref, ra_ref, ri_ref, a_ref, uu_ref):
        for r0, r1e, c0, c1 in panels:
            rb = r1_ref[:, r0:r1e].astype(BF16)
            ra = _sigmoid(_dot(rb, wa_ref[r0:r1e, c0:c1]) + ba_ref[:, c0:c1])
            ri = _sigmoid(_dot(rb, wx_ref[r0:r1e, c0:c1]) + bx_ref[:, c0:c1])
            log_a = -RG_LRU_C * ra * _softplus(-lam_ref[:, c0:c1])
            ra_ref[:, c0:c1] = ra
            ri_ref[:, c0:c1] = ri
            a_ref[:, c0:c1] = jnp.exp(log_a)
            uu_ref[:, c0:c1] = jnp.sqrt(_neg_expm1(2.0 * log_a)) * (ri * r1_ref[:, c0:c1])

    blk = 2 * (5 * tm * R * 4) + 2 * R * R * 2 + 6 * tm * R * 4
    tile = _bs((tm, R), lambda i: (i, 0))
    return pl.pallas_call(
        body, name=name, grid=(S // tm,),
        in_specs=[tile, _resident((R, R)), _resident((R, R)), _resident((1, R)), _resident((1, R)), _resident((1, R))],
        out_specs=[tile] * 4,
        out_shape=[jax.ShapeDtypeStruct((S, R), F32)] * 4,
        compiler_params=_cparams(blk),
    )(r1, wa, wx, ba, bx, lam)


def _gates_bwd(guu, da, ra, ri, r1, wa, wx, lam, name):
    S, R = r1.shape
    tm = _divisor_tile(S, 256, 16)
    nsteps = S // tm
    panels = _band_panels(R, R // RNN_BLOCKS)

    def body(g_ref, da_ref, ra_ref, ri_ref, r1_ref, wa_ref, wx_ref, lam_ref,
             dr1_ref, dpa_ref, dpx_ref, dba_ref, dbx_ref, dlam_ref):
        i = pl.program_id(0)

        @pl.when(i == 0)
        def _():
            dba_ref[...] = jnp.zeros_like(dba_ref)
            dbx_ref[...] = jnp.zeros_like(dbx_ref)
            dlam_ref[...] = jnp.zeros_like(dlam_ref)

        g = g_ref[...]
        ra = ra_ref[...]
        ri = ri_ref[...]
        r1_ = r1_ref[...]
        sp = _softplus(-lam_ref[...])
        log_a = -RG_LRU_C * ra * sp
        a = jnp.exp(log_a)
        mult = jnp.sqrt(_neg_expm1(2.0 * log_a))
        d_ri = g * mult * r1_
        dr1 = g * mult * ri
        dmult = g * ri * r1_
        dlog_a = da_ref[...] * a - dmult * (a * a) / mult
        dra = dlog_a * (-RG_LRU_C * sp)
        dlam_ref[...] += _colsum(dlog_a * (-RG_LRU_C * ra))
        dpa = dra * ra * (1.0 - ra)
        dpx = d_ri * ri * (1.0 - ri)
        dba_ref[...] += _colsum(dpa)
        dbx_ref[...] += _colsum(dpx)
        dpa_b = dpa.astype(BF16)
        dpx_b = dpx.astype(BF16)
        dpa_ref[...] = dpa_b
        dpx_ref[...] = dpx_b
        dr1_ref[...] = dr1
        for k0, k1, c0, c1 in panels:
            dr1_ref[:, c0:c1] += (_dot_nt(dpa_ref[:, k0:k1], wa_ref[c0:c1, k0:k1])
                                  + _dot_nt(dpx_ref[:, k0:k1], wx_ref[c0:c1, k0:k1]))

        @pl.when(i == nsteps - 1)
        def _():
            dlam_ref[...] = dlam_ref[...] * (-_sigmoid(-lam_ref[...]))

    blk = 2 * (6 * tm * R * 4 + 2 * tm * R * 2) + 2 * R * R * 2 + 10 * tm * R * 4
    tile = _bs((tm, R), lambda i: (i, 0))
    rowspec = _bs((1, R), lambda i: (0, 0))
    row = jax.ShapeDtypeStruct((1, R), F32)
    return pl.pallas_call(
        body, name=name, grid=(nsteps,),
        in_specs=[tile] * 5 + [_resident((R, R)), _resident((R, R)), _resident((1, R))],
        out_specs=[tile, tile, tile, rowspec, rowspec, rowspec],
        out_shape=[jax.ShapeDtypeStruct((S, R), F32), jax.ShapeDtypeStruct((S, R), BF16), jax.ShapeDtypeStruct((S, R), BF16),
                   row, row, row],
        compiler_params=_cparams(blk),
    )(guu, da, ra, ri, r1, wa, wx, lam)


def _embed_blocks(w, name):
    H, bk, _ = w.shape

    def body(w_ref, o_ref):
        o_ref[...] = jnp.zeros_like(o_ref)
        for h in range(H):
            o_ref[bk * h:bk * (h + 1), bk * h:bk * (h + 1)] = w_ref[h].astype(BF16)

    return pl.pallas_call(body, name=name, out_shape=jax.ShapeDtypeStruct((H * bk, H * bk), BF16),
                          compiler_params=_cparams(3 * H * bk * H * bk * 2))(w)


def _block_grads(r1, dpa, dpx, name):
    S, R = r1.shape
    bk = R // RNN_BLOCKS
    tk = _divisor_tile(S, 512, 16)
    nsteps = S // tk
    panels = _band_panels(R, bk)

    def body(r1_ref, dpa_ref, dpx_ref, ga_ref, gx_ref, acca_ref, accx_ref):
        k = pl.program_id(0)

        @pl.when(k == 0)
        def _():
            acca_ref[...] = jnp.zeros_like(acca_ref)
            accx_ref[...] = jnp.zeros_like(accx_ref)

        for k0, k1, c0, c1 in panels:
            rb = r1_ref[:, k0:k1].astype(BF16)
            acca_ref[k0:k1, c0:c1] += _dot_tn(rb, dpa_ref[:, c0:c1])
            accx_ref[k0:k1, c0:c1] += _dot_tn(rb, dpx_ref[:, c0:c1])

        @pl.when(k == nsteps - 1)
        def _():
            for h in range(RNN_BLOCKS):
                ga_ref[h] = acca_ref[bk * h:bk * (h + 1), bk * h:bk * (h + 1)]
                gx_ref[h] = accx_ref[bk * h:bk * (h + 1), bk * h:bk * (h + 1)]

    tile = lambda: _bs((tk, R), lambda k: (k, 0))
    out = _bs((RNN_BLOCKS, bk, bk), lambda k: (0, 0, 0))
    sds = jax.ShapeDtypeStruct((RNN_BLOCKS, bk, bk), F32)
    return pl.pallas_call(
        body, name=name, grid=(nsteps,),
        in_specs=[tile(), tile(), tile()], out_specs=[out, out], out_shape=[sds, sds],
        scratch_shapes=[pltpu.VMEM((R, R), F32), pltpu.VMEM((R, R), F32)],
        compiler_params=_cparams(2 * (tk * R * 8) + 2 * R * R * 4 + 4 * tk * R * 4),
    )(r1, dpa, dpx)


def _scan_geometry(S):
    nseg = SCAN_SEGMENTS if S % (SCAN_SEGMENTS * SUBLANES) == 0 else SUBLANES
    return nseg, S // nseg, nseg // SUBLANES


def _steps(n, step, init):
    u = SCAN_UNROLL if n % SCAN_UNROLL == 0 else 1

    def trip(t, carry):
        for k in range(u):
            carry = step(t * u + k, carry)
        return carry

    return lax.fori_loop(0, n // u, trip, init)


def _scan_fwd(a, u, name):
    S, C = a.shape
    nseg, L, V = _scan_geometry(S)

    def body(a_ref, u_ref, h_ref, e_ref, p_ref, init_ref):
        def rows(ref, v, j):
            return ref[pl.ds(v * SUBLANES * L + j, SUBLANES, stride=L), :]

        def run1(j, carry):
            hs, ps = carry
            nh, npd = [], []
            for v in range(V):
                aj = rows(a_ref, v, j)
                nh.append(aj * hs[v] + rows(u_ref, v, j))
                npd.append(aj * ps[v])
            return tuple(nh), tuple(npd)

        zero = jnp.zeros((SUBLANES, LANES), F32)
        one = jnp.ones((SUBLANES, LANES), F32)
        hs, ps = _steps(L, run1, ((zero,) * V, (one,) * V))
        for v in range(V):
            e_ref[SUBLANES * v:SUBLANES * (v + 1), :] = hs[v]
            p_ref[SUBLANES * v:SUBLANES * (v + 1), :] = ps[v]
        init_ref[0:1, :] = jnp.zeros((1, LANES), F32)
        for s in range(1, nseg):
            init_ref[s:s + 1, :] = e_ref[s - 1:s, :] + p_ref[s - 1:s, :] * init_ref[s - 1:s, :]

        def run2(j, hs):
            nh = []
            for v in range(V):
                h = rows(a_ref, v, j) * hs[v] + rows(u_ref, v, j)
                h_ref[pl.ds(v * SUBLANES * L + j, SUBLANES, stride=L), :] = h
                nh.append(h)
            return tuple(nh)

        _steps(L, run2, tuple(init_ref[SUBLANES * v:SUBLANES * (v + 1), :] for v in range(V)))

    col = lambda i: (0, i)
    blk = 2 * (3 * S * LANES * 4)
    return pl.pallas_call(
        body, name=name, grid=(C // LANES,),
        in_specs=[_bs((S, LANES), col), _bs((S, LANES), col)],
        out_specs=_bs((S, LANES), col),
        out_shape=jax.ShapeDtypeStruct((S, C), F32),
        scratch_shapes=[pltpu.VMEM((nseg, LANES), F32)] * 3,
        compiler_params=_cparams(blk),
    )(a, u)


def _scan_bwd(a, dh, h, name):
    S, C = a.shape
    nseg, L, V = _scan_geometry(S)
    E = min(EW_ROWS, S)

    def body(a_ref, d_ref, h_ref, g_ref, da_ref, apad_ref, hpad_ref, e_ref, p_ref, init_ref):
        apad_ref[S:S + SUBLANES, :] = jnp.zeros((SUBLANES, LANES), F32)
        hpad_ref[0:SUBLANES, :] = jnp.zeros((SUBLANES, LANES), F32)

        def fill(i, carry):
            r = pl.multiple_of(i * E, E)
            apad_ref[pl.ds(r, E), :] = a_ref[pl.ds(r, E), :]
            hpad_ref[pl.ds(r + SUBLANES, E), :] = h_ref[pl.ds(r, E), :]
            return carry

        lax.fori_loop(0, S // E, fill, 0)

        def rows(ref, v, j, shift=0):
            return ref[pl.ds(v * SUBLANES * L + j + shift, SUBLANES, stride=L), :]

        def run1(jj, carry):
            j = L - 1 - jj
            gs, ps = carry
            ng, npd = [], []
            for v in range(V):
                bj = rows(apad_ref, v, j, 1)
                ng.append(bj * gs[v] + rows(d_ref, v, j))
                npd.append(bj * ps[v])
            return tuple(ng), tuple(npd)

        zero = jnp.zeros((SUBLANES, LANES), F32)
        one = jnp.ones((SUBLANES, LANES), F32)
        gs, ps = _steps(L, run1, ((zero,) * V, (one,) * V))
        for v in range(V):
            e_ref[SUBLANES * v:SUBLANES * (v + 1), :] = gs[v]
            p_ref[SUBLANES * v:SUBLANES * (v + 1), :] = ps[v]
        init_ref[nseg - 1:nseg, :] = jnp.zeros((1, LANES), F32)
        for s in range(nseg - 2, -1, -1):
            init_ref[s:s + 1, :] = e_ref[s + 1:s + 2, :] + p_ref[s + 1:s + 2, :] * init_ref[s + 1:s + 2, :]

        def run2(jj, gs):
            j = L - 1 - jj
            ng = []
            for v in range(V):
                g = rows(apad_ref, v, j, 1) * gs[v] + rows(d_ref, v, j)
                g_ref[pl.ds(v * SUBLANES * L + j, SUBLANES, stride=L), :] = g
                da_ref[pl.ds(v * SUBLANES * L + j, SUBLANES, stride=L), :] = g * rows(hpad_ref, v, j, SUBLANES - 1)
                ng.append(g)
            return tuple(ng)

        _steps(L, run2, tuple(init_ref[SUBLANES * v:SUBLANES * (v + 1), :] for v in range(V)))

    col = lambda i: (0, i)
    blk = 2 * (5 * S * LANES * 4) + 2 * (S + SUBLANES) * LANES * 4
    return pl.pallas_call(
        body, name=name, grid=(C // LANES,),
        in_specs=[_bs((S, LANES), col)] * 3,
        out_specs=[_bs((S, LANES), col)] * 2,
        out_shape=[jax.ShapeDtypeStruct((S, C), F32)] * 2,
        scratch_shapes=[pltpu.VMEM((S + SUBLANES, LANES), F32), pltpu.VMEM((S + SUBLANES, LANES), F32)]
                       + [pltpu.VMEM((nseg, LANES), F32)] * 3,
        compiler_params=_cparams(blk),
    )(a, dh, h)


def _mixer_out_fwd(c4, h, rg, gc, gr, x1, wc, wr, wo, g, b, alpha, name):
    S, D = x1.shape
    R = h.shape[1]
    tm = _divisor_tile(S, 256, 16)

    def body(c4_ref, h_ref, rg_ref, gc_ref, gr_ref, x_ref, wc_ref, wr_ref, wo_ref, g_ref, b_ref,
             yc_ref, yr_ref, z_ref, y_ref):
        yc = _dot(c4_ref[...], wc_ref[...])
        q = (h_ref[...] * _gelu(rg_ref[...])).astype(BF16)
        yr = _dot(q, wr_ref[...])
        yc_ref[...] = yc
        yr_ref[...] = yr
        m = (_sigmoid(gc_ref[...]) * yc + _sigmoid(gr_ref[...]) * yr).astype(BF16)
        z = alpha * x_ref[...] + _dot(m, wo_ref[...])
        z_ref[...] = z
        y_ref[...] = _norm_fwd(z, g_ref[...], b_ref[...])

    blk = 2 * (tm * D * 2 + 2 * tm * R * 4 + 7 * tm * D * 4) + (2 * D * D + R * D) * 2 + 6 * tm * D * 4
    td = _bs((tm, D), lambda i: (i, 0))
    tr = _bs((tm, R), lambda i: (i, 0))
    return pl.pallas_call(
        body, name=name, grid=(S // tm,),
        in_specs=[td, tr, tr, td, td, td, _layer_resident(wc)[1], _layer_resident(wr)[1], _layer_resident(wo)[1],
                  _resident((1, D)), _resident((1, D))],
        out_specs=[td] * 4,
        out_shape=[jax.ShapeDtypeStruct((S, D), F32)] * 4,
        compiler_params=_cparams(blk),
    )(c4, h, rg, gc, gr, x1, wc[0], wr[0], wo[0], g, b)


def _mixer_out_bwd(dy, z, g, wo, yc, yr, gc, gr, name):
    S, D = dy.shape
    tm = _divisor_tile(S, 256, 16)

    def body(dy_ref, z_ref, g_ref, wo_ref, yc_ref, yr_ref, gc_ref, gr_ref,
             dz_ref, dzb_ref, m_ref, dyc_ref, dyr_ref, dgc_ref, dgr_ref, sgc_ref, sgr_ref, dg_ref, db_ref):
        @pl.when(pl.program_id(0) == 0)
        def _():
            for ref in (sgc_ref, sgr_ref, dg_ref, db_ref):
                ref[...] = jnp.zeros_like(ref)

        dy_ = dy_ref[...]
        dz, xhat = _norm_bwd(z_ref[...], g_ref[...], dy_)
        dg_ref[...] += _colsum(dy_ * xhat)
        db_ref[...] += _colsum(dy_)
        dz_ref[...] = dz
        dzb = dz.astype(BF16)
        dzb_ref[...] = dzb
        dm = _dot_nt(dzb, wo_ref[...])
        yc = yc_ref[...]
        yr = yr_ref[...]
        sc = _sigmoid(gc_ref[...])
        sr = _sigmoid(gr_ref[...])
        m_ref[...] = (sc * yc + sr * yr).astype(BF16)
        dyc_ref[...] = (dm * sc).astype(BF16)
        dyr_ref[...] = (dm * sr).astype(BF16)
        dgc = dm * yc * sc * (1.0 - sc)
        dgr = dm * yr * sr * (1.0 - sr)
        dgc_ref[...] = dgc.astype(BF16)
        dgr_ref[...] = dgr.astype(BF16)
        sgc_ref[...] += _colsum(dgc)
        sgr_ref[...] += _colsum(dgr)

    blk = 2 * (7 * tm * D * 4 + 6 * tm * D * 2) + D * D * 2 + 8 * tm * D * 4
    td = _bs((tm, D), lambda i: (i, 0))
    rowspec = _bs((1, D), lambda i: (0, 0))
    row = jax.ShapeDtypeStruct((1, D), F32)
    bfd = jax.ShapeDtypeStruct((S, D), BF16)
    return pl.pallas_call(
        body, name=name, grid=(S // tm,),
        in_specs=[td, td, _resident((1, D)), _layer_resident(wo)[1], td, td, td, td],
        out_specs=[td] * 7 + [rowspec] * 4,
        out_shape=[jax.ShapeDtypeStruct((S, D), F32), bfd, bfd, bfd, bfd, bfd, bfd, row, row, row, row],
        compiler_params=_cparams(blk),
    )(dy, z, g, wo[0], yc, yr, gc, gr)


def _branch_bwd(dyc, dyr, wc, wr, h, rg, name):
    S, D = dyc.shape
    R = h.shape[1]
    tm = _divisor_tile(S, 256, 16)

    def body(dyc_ref, dyr_ref, wc_ref, wr_ref, h_ref, rg_ref, dc4_ref, dh_ref, drg_ref, q_ref, srg_ref):
        @pl.when(pl.program_id(0) == 0)
        def _():
            srg_ref[...] = jnp.zeros_like(srg_ref)

        dc4_ref[...] = _dot_nt(dyc_ref[...], wc_ref[...])
        dq = _dot_nt(dyr_ref[...], wr_ref[...])
        h_ = h_ref[...]
        rg_ = rg_ref[...]
        ge = _gelu(rg_)
        dh_ref[...] = dq * ge
        drg = dq * h_ * _gelu_grad(rg_)
        drg_ref[...] = drg.astype(BF16)
        srg_ref[...] += _colsum(drg)
        q_ref[...] = (h_ * ge).astype(BF16)

    blk = 2 * (2 * tm * D * 2 + tm * D * 4 + 3 * tm * R * 4 + 2 * tm * R * 2) + (D * D + R * D) * 2 + 6 * tm * R * 4
    td = _bs((tm, D), lambda i: (i, 0))
    tr = _bs((tm, R), lambda i: (i, 0))
    return pl.pallas_call(
        body, name=name, grid=(S // tm,),
        in_specs=[td, td, _layer_resident(wc)[1], _layer_resident(wr)[1], tr, tr],
        out_specs=[td, tr, tr, tr, _bs((1, R), lambda i: (0, 0))],
        out_shape=[jax.ShapeDtypeStruct((S, D), F32), jax.ShapeDtypeStruct((S, R), F32), jax.ShapeDtypeStruct((S, R), BF16),
                   jax.ShapeDtypeStruct((S, R), BF16), jax.ShapeDtypeStruct((1, R), F32)],
        compiler_params=_cparams(blk),
    )(dyc, dyr, wc[0], wr[0], h, rg)


def _loss_head(y, target, name):
    S, D = y.shape
    tm = _divisor_tile(S, 512, 16)
    nsteps = S // tm

    def body(y_ref, t_ref, loss_ref, dy_ref, acc_ref):
        i = pl.program_id(0)

        @pl.when(i == 0)
        def _():
            acc_ref[...] = jnp.zeros_like(acc_ref)

        err = y_ref[...] - t_ref[...]
        dy_ref[...] = err * (1.0 / D)
        acc_ref[...] += _colsum(err * err)

        @pl.when(i == nsteps - 1)
        def _():
            loss_ref[...] = jnp.sum(acc_ref[...], axis=-1, keepdims=True) * (0.5 / D)

    td = _bs((tm, D), lambda i: (i, 0))
    return pl.pallas_call(
        body, name=name, grid=(nsteps,),
        in_specs=[td, td],
        out_specs=[_bs((1, 1), lambda i: (0, 0)), td],
        out_shape=[jax.ShapeDtypeStruct((1, 1), F32), jax.ShapeDtypeStruct((S, D), F32)],
        scratch_shapes=[pltpu.VMEM((1, D), F32)],
        compiler_params=_cparams(2 * 3 * tm * D * 4),
    )(y, target)


def _adamw_math(w, g, m, v):
    m = ADAM_B1 * m + (1.0 - ADAM_B1) * g
    v = ADAM_B2 * v + (1.0 - ADAM_B2) * (g * g)
    m_hat = m / (1.0 - ADAM_B1 ** ADAM_STEP)
    v_hat = v / (1.0 - ADAM_B2 ** ADAM_STEP)
    delta = -ADAM_LR * (m_hat / (jnp.sqrt(v_hat) + ADAM_EPS) + ADAM_WD * w)
    return delta, m, v


def _adamw_sharded(w, m, v, own, sib, rem, name):
    _, r, c = w.shape
    tr = _divisor_tile(r, max(16, (1 << 20) // (4 * c) // 16 * 16), 16)

    def body(w_ref, m_ref, v_ref, own_ref, sib_ref, rem_ref, g_ref, d_ref, nm_ref, nv_ref):
        mine = pl.program_id(0) == lax.axis_index("c")
        g = jnp.where(mine, own_ref[...], sib_ref[...]).astype(F32)
        for j in range(N_CHIPS - 1):
            g = g + rem_ref[j].astype(F32)
        delta, nm, nv = _adamw_math(w_ref[...], g, m_ref[...], v_ref[...])
        g_ref[...] = g
        d_ref[...] = delta
        nm_ref[...] = nm
        nv_ref[...] = nv

    tile = _bs((None, tr, c), lambda l, i: (l, i, 0))
    flat = _bs((tr, c), lambda l, i: (i, 0))
    sds = jax.ShapeDtypeStruct(w.shape, F32)
    return pl.pallas_call(
        body, name=name, grid=(2, r // tr),
        in_specs=[tile, tile, tile, flat, flat, _bs((N_CHIPS - 1, None, tr, c), lambda l, i: (0, l, i, 0))],
        out_specs=[tile] * 4,
        out_shape=[sds] * 4,
        compiler_params=_cparams(2 * (7 * tr * c * 4 + (N_CHIPS + 1) * tr * c * 2) + 6 * tr * c * 4),
    )(w, m, v, own, sib, rem)


def _adamw_flat(w, m, v, g, name):
    rows = w.shape[0]
    tr = _divisor_tile(rows, 1024, SUBLANES)

    def body(w_ref, m_ref, v_ref, g_ref, d_ref, nm_ref, nv_ref):
        delta, nm, nv = _adamw_math(w_ref[...], g_ref[...], m_ref[...], v_ref[...])
        d_ref[...] = delta
        nm_ref[...] = nm
        nv_ref[...] = nv

    tile = _bs((tr, LANES), lambda i: (i, 0))
    sds = jax.ShapeDtypeStruct(w.shape, F32)
    return pl.pallas_call(
        body, name=name, grid=(rows // tr,),
        in_specs=[tile] * 4, out_specs=[tile] * 3, out_shape=[sds] * 3,
        compiler_params=_cparams(2 * 7 * tr * LANES * 4),
    )(w, m, v, g)


def _pair_sum_bf16(g0, g1, theirs, name):
    rows, c = g0.shape
    tr = _divisor_tile(rows, max(16, (1 << 20) // (4 * c) // 16 * 16), 16)

    def body(g0_ref, g1_ref, t_ref, o_ref):
        mine = jnp.where(lax.axis_index("c") == 0, g0_ref[...], g1_ref[...])
        o_ref[...] = (mine + t_ref[...]).astype(BF16)

    tile = _bs((tr, c), lambda i: (i, 0))
    return pl.pallas_call(
        body, name=name, grid=(rows // tr,),
        in_specs=[tile, tile, tile], out_specs=tile, out_shape=jax.ShapeDtypeStruct((rows, c), BF16),
        compiler_params=_cparams(2 * 4 * tr * c * 4),
    )(g0, g1, theirs)


ANY = pl.BlockSpec(memory_space=pl.ANY)


def _mesh_position():
    return lax.axis_index("x"), lax.axis_index("y"), lax.axis_index("c")


def _other_chips():
    x, y, c = _mesh_position()
    chips = [(1 - x, y), (x, 1 - y), (1 - x, 1 - y)]
    return 2 * x + y, (x, y, 1 - c), chips, [2 * cx + cy for cx, cy in chips]


def _chip_slab(ref, k, width, by_cols):
    if by_cols:
        return ref.at[:, pl.ds(pl.multiple_of(k * width, LANES), width)]
    return ref.at[k]


def _gather_weights(srcs, by_cols, name):
    n = len(srcs)

    def body(*refs):
        src = refs[:n]
        out = refs[n:2 * n]
        send_sems, recv_sems = refs[2 * n:]
        c = lax.axis_index("c")
        me, sibling, chips, chip_ids = _other_chips()

        def remote(i, k, src_ref, dst_ref, to):
            return pltpu.make_async_remote_copy(src_ref=src_ref, dst_ref=dst_ref, send_sem=send_sems.at[i, k],
                                                recv_sem=recv_sems.at[i, k], device_id=to, device_id_type=MESH)

        def slot(i, l, k):
            return _chip_slab(out[i].at[l], k, srcs[i].shape[-1], by_cols[i])

        started = []
        for i in range(n):
            for j in range(3):
                started.append(remote(i, j, src[i].at[c], slot(i, c, me), (*chips[j], c)))
            for l in range(2):
                started.append(remote(i, 6 + l, src[i].at[l], slot(i, l, me), sibling))
        for cp in started:
            cp.start()
        for i in range(n):
            for j in range(3):
                arrived = slot(i, c, chip_ids[j])
                remote(i, j, arrived, arrived, sibling).wait_recv()
                fwd = remote(i, 3 + j, arrived, arrived, sibling)
                fwd.start()
                started.append(fwd)
        for i in range(n):
            for j in range(3):
                passed = slot(i, 1 - c, chip_ids[j])
                remote(i, 3 + j, passed, passed, sibling).wait_recv()
            for l in range(2):
                own = slot(i, l, me)
                remote(i, 6 + l, own, own, sibling).wait_recv()
        for cp in started:
            cp.wait_send()

    out_shape = [jax.ShapeDtypeStruct((2,) + s.shape[1:-1] + (N_CHIPS * s.shape[-1],) if bc else (2, N_CHIPS) + s.shape[1:],
                                      s.dtype) for s, bc in zip(srcs, by_cols)]
    return pl.pallas_call(
        body, name=name,
        in_specs=[ANY] * n, out_specs=[ANY] * n, out_shape=out_shape,
        scratch_shapes=[pltpu.SemaphoreType.DMA((n, 8)), pltpu.SemaphoreType.DMA((n, 8))],
    )(*srcs)


def _scatter_grads(csums, by_cols, name):
    n = len(csums)
    shard = [(s.shape[0], s.shape[1] // N_CHIPS) if bc else s.shape[1:] for s, bc in zip(csums, by_cols)]

    def body(*refs):
        src = refs[:n]
        rem = refs[n:2 * n]
        sib = refs[2 * n:3 * n]
        send_sems, recv_sems = refs[3 * n:]
        c = lax.axis_index("c")
        me, sibling, chips, chip_ids = _other_chips()

        def remote(i, k, src_ref, dst_ref, to):
            return pltpu.make_async_remote_copy(src_ref=src_ref, dst_ref=dst_ref, send_sem=send_sems.at[i, k],
                                                recv_sem=recv_sems.at[i, k], device_id=to, device_id_type=MESH)

        def part(i, k):
            return _chip_slab(src[i], k, shard[i][-1], by_cols[i])

        started = []
        for i in range(n):
            for j in range(3):
                started.append(remote(i, j, part(i, chip_ids[j]), rem[i].at[j, c], (*chips[j], c)))
            started.append(remote(i, 6, part(i, me), sib[i], sibling))
        for cp in started:
            cp.start()
        for i in range(n):
            for j in range(3):
                slot = rem[i].at[j, c]
                remote(i, j, slot, slot, sibling).wait_recv()
                fwd = remote(i, 3 + j, slot, slot, sibling)
                fwd.start()
                started.append(fwd)
        for i in range(n):
            for j in range(3):
                slot = rem[i].at[j, 1 - c]
                remote(i, 3 + j, slot, slot, sibling).wait_recv()
            remote(i, 6, sib[i], sib[i], sibling).wait_recv()
        for cp in started:
            cp.wait_send()

    out_shape = ([jax.ShapeDtypeStruct((N_CHIPS - 1, 2) + tuple(sh), s.dtype) for s, sh in zip(csums, shard)]
                 + [jax.ShapeDtypeStruct(tuple(sh), s.dtype) for s, sh in zip(csums, shard)])
    outs = pl.pallas_call(
        body, name=name,
        in_specs=[ANY] * n, out_specs=[ANY] * (2 * n), out_shape=out_shape,
        scratch_shapes=[pltpu.SemaphoreType.DMA((n, 7)), pltpu.SemaphoreType.DMA((n, 7))],
    )(*csums)
    return outs[:n], outs[n:]


def _sibling_exchange(g0, g1, name):
    n = len(g0)

    def body(*refs):
        layers = (refs[:n], refs[n:2 * n])
        theirs = refs[2 * n:3 * n]
        send_sems, recv_sems = refs[3 * n:]
        x, y, c = _mesh_position()

        def remote(i, src_ref):
            return pltpu.make_async_remote_copy(src_ref=src_ref, dst_ref=theirs[i], send_sem=send_sems.at[i],
                                                recv_sem=recv_sems.at[i], device_id=(x, y, 1 - c), device_id_type=MESH)

        for keep in range(2):
            @pl.when(c == keep)
            def _():
                for i in range(n):
                    remote(i, layers[1 - keep][i]).start()

        for i in range(n):
            remote(i, layers[0][i]).wait()

    return pl.pallas_call(
        body, name=name,
        in_specs=[ANY] * (2 * n), out_specs=[ANY] * n, out_shape=[jax.ShapeDtypeStruct(g.shape, g.dtype) for g in g0],
        scratch_shapes=[pltpu.SemaphoreType.DMA((n,)), pltpu.SemaphoreType.DMA((n,))],
    )(*g0, *g1)


def _all_reduce_small(v, name):
    _, rows, _ = v.shape

    def body(v_ref, o_ref, recv_ref, send_sems, recv_sems):
        x, y, c = _mesh_position()
        me = 4 * x + 2 * y + c
        peers = []
        for d in range(1, N_DEV):
            px, py, pc = x ^ ((d >> 2) & 1), y ^ ((d >> 1) & 1), c ^ (d & 1)
            peers.append(((px, py, pc), 4 * px + 2 * py + pc))

        def remote(k, src_ref, dst_ref, to):
            return pltpu.make_async_remote_copy(src_ref=src_ref, dst_ref=dst_ref, send_sem=send_sems.at[k],
                                                recv_sem=recv_sems.at[k], device_id=to, device_id_type=MESH)

        scatter = [remote(d, v_ref.at[pid], recv_ref.at[me], to) for d, (to, pid) in enumerate(peers)]
        for cp in scatter:
            cp.start()
        recv_ref[pl.ds(me, 1)] = v_ref[pl.ds(me, 1)]
        for d, (to, pid) in enumerate(peers):
            remote(d, v_ref.at[pid], recv_ref.at[pid], to).wait_recv()
        total = recv_ref[0]
        for s in range(1, N_DEV):
            total = total + recv_ref[s]
        o_ref[pl.ds(me, 1)] = total[None]
        gather = [remote(N_DEV - 1 + d, o_ref.at[me], o_ref.at[me], to) for d, (to, pid) in enumerate(peers)]
        for cp in gather:
            cp.start()
        for d, (to, pid) in enumerate(peers):
            remote(N_DEV - 1 + d, o_ref.at[pid], o_ref.at[pid], to).wait_recv()
        for cp in scatter + gather:
            cp.wait_send()

    vm = pl.BlockSpec(memory_space=pltpu.VMEM)
    return pl.pallas_call(
        body, name=name,
        in_specs=[vm], out_specs=vm, out_shape=jax.ShapeDtypeStruct(v.shape, F32),
        scratch_shapes=[pltpu.VMEM(v.shape, F32), pltpu.SemaphoreType.DMA((2 * (N_DEV - 1),)),
                        pltpu.SemaphoreType.DMA((2 * (N_DEV - 1),))],
        compiler_params=_cparams(4 * _nbytes(v.shape, F32)),
    )(v)


SHARDED_MATS = ("ffn1_w_gu", "ffn1_w_down", "mix_w_in", "conv_w_proj", "rnn_w_proj", "mix_w_out", "ffn2_w_gu", "ffn2_w_down")
COL_SHARDED = ("ffn1_w_gu", "ffn2_w_gu", "conv_dw_w")
SHARDED_VECS = ("conv_dw_w", "rnn_conv_w")
WEIGHT_NAMES = ("ffn1_w_gu", "ffn1_w_down", "ln1_g", "ln1_b", "mix_w_in", "mix_b_in", "conv_dw_w", "conv_dw_b", "conv_gn_g",
                "conv_gn_b", "conv_w_proj", "rnn_conv_w", "rnn_conv_b", "rnn_w_a", "rnn_b_a", "rnn_w_x", "rnn_b_x",
                "rnn_lambda", "rnn_w_proj", "mix_w_out", "ln2_g", "ln2_b", "ffn2_w_gu", "ffn2_w_down", "ln3_g", "ln3_b")
SMALL_NAMES = tuple(n for n in WEIGHT_NAMES if n not in SHARDED_MATS)


def _unshard_cols(gathered):
    k4, K, n = gathered.shape
    return jnp.transpose(gathered, (1, 0, 2)).reshape(K, k4 * n)


def _row(v):
    return v.reshape(1, -1)


def _layer_forward(x0, p, alpha, l):
    t = f"l{l}_"
    sv = {"x0": x0}
    x1, sv["z1"], sv["hg1"], sv["hu1"] = _ffn_fwd(x0, p["wgu1"], p["wd1"], p["ln1_g"], p["ln1_b"], alpha, t + "ffn1_fwd")
    sv["x1"] = x1
    sec = {}
    for s in ("cv", "cg", "rx", "rg", "gc", "gr"):
        sec[s] = _mm_bias(x1, p["win_" + s], p["bin_" + s], t + "win_" + s)
    sv.update(sec)
    sv["c2"], c4 = _conv_branch_fwd(sec["cv"], sec["cg"], p["conv_dw_w"], p["conv_dw_b"], p["conv_gn_g"], p["conv_gn_b"], t + "conv_fwd")
    sv["c4"] = c4
    r1 = _short_conv_fwd(sec["rx"], p["rnn_conv_w"], p["rnn_conv_b"], t + "rconv_fwd")
    sv["r1"] = r1
    sv["ra"], sv["ri"], a, uu = _gates_fwd(r1, p["wa"], p["wx"], p["rnn_b_a"], p["rnn_b_x"], p["rnn_lambda"], t + "gates_fwd")
    sv["a"] = a
    h = _scan_fwd(a, uu, t + "scan_fwd")
    sv["h"] = h
    sv["yc"], sv["yr"], sv["z2"], x2 = _mixer_out_fwd(c4, h, sec["rg"], sec["gc"], sec["gr"], x1, p["wc"], p["wr"], p["wo"],
                                                      p["ln2_g"], p["ln2_b"], alpha, t + "mixout_fwd")
    sv["x2"] = x2
    x3, sv["z3"], sv["hg2"], sv["hu2"] = _ffn_fwd(x2, p["wgu2"], p["wd2"], p["ln3_g"], p["ln3_b"], alpha, t + "ffn2_fwd")
    return x3, sv


def _layer_backward(dy, p, sv, alpha, l):
    t = f"l{l}_"
    g = {}
    dx2, df, a_act, dhg, dhu, g["ln3_g"], g["ln3_b"] = _ffn_bwd(dy, sv["z3"], sv["hg2"], sv["hu2"], p["wgu2"], p["wd2"],
                                                                 p["ln3_g"], alpha, t + "ffn2_bwd")
    g["ffn2_w_down"] = _mm_tn(a_act, df, t + "dwd2")
    g["ffn2_w_gu"] = _mm_tn_pair(sv["x2"], dhg, dhu, t + "dwgu2")
    (dz2, dz2b, m_b, dyc, dyr, dgc, dgr, s_gc, s_gr, g["ln2_g"], g["ln2_b"]) = _mixer_out_bwd(
        dx2, sv["z2"], p["ln2_g"], p["wo"], sv["yc"], sv["yr"], sv["gc"], sv["gr"], t + "mixout_bwd")
    g["mix_w_out"] = _mm_tn(m_b, dz2b, t + "dwo")
    dc4, dh, drg, q_b, s_rg = _branch_bwd(dyc, dyr, p["wc"], p["wr"], sv["h"], sv["rg"], t + "branch_bwd")
    g["conv_w_proj"] = _mm_tn(sv["c4"], dyc, t + "dwc")
    g["rnn_w_proj"] = _mm_tn(q_b, dyr, t + "dwr")
    (dcv, dcg, g["conv_dw_w"], g["conv_dw_b"], g["conv_gn_g"], g["conv_gn_b"], s_cv, s_cg) = _conv_branch_bwd(
        dc4, sv["c2"], sv["cv"], sv["cg"], p["conv_dw_w"], p["conv_gn_g"], p["conv_gn_b"], t + "conv_bwd")
    guu, da = _scan_bwd(sv["a"], dh, sv["h"], t + "scan_bwd")
    dr1, dpa, dpx, g["rnn_b_a"], g["rnn_b_x"], g["rnn_lambda"] = _gates_bwd(
        guu, da, sv["ra"], sv["ri"], sv["r1"], p["wa"], p["wx"], p["rnn_lambda"], t + "gates_bwd")
    g["rnn_w_a"], g["rnn_w_x"] = _block_grads(sv["r1"], dpa, dpx, t + "dwax")
    drx, g["rnn_conv_w"], g["rnn_conv_b"], s_rx = _short_conv_bwd(dr1, sv["rx"], p["rnn_conv_w"], t + "rconv_bwd")
    du = {"cv": dcv, "cg": dcg, "rx": drx, "rg": drg, "gc": dgc, "gr": dgr}
    order = ("cv", "cg", "rx", "rg", "gc", "gr")
    g["mix_w_in"] = jnp.concatenate([_mm_tn(du[s], sv["x1"], t + "dwin_" + s) for s in order], axis=0)
    g["mix_b_in"] = jnp.concatenate([s_cv, s_cg, s_rx, s_rg, s_gc, s_gr], axis=1)
    dx1 = _mix_dx(dz2, [du[s] for s in order], [p["win_" + s] for s in order], alpha, t + "mix_dx")
    dx0, df, a_act, dhg, dhu, g["ln1_g"], g["ln1_b"] = _ffn_bwd(dx1, sv["z1"], sv["hg1"], sv["hu1"], p["wgu1"], p["wd1"],
                                                                 p["ln1_g"], alpha, t + "ffn1_bwd")
    g["ffn1_w_down"] = _mm_tn(a_act, df, t + "dwd1")
    g["ffn1_w_gu"] = _mm_tn_pair(sv["x0"], dhg, dhu, t + "dwgu1")
    return dx0, g


def _pack_small(arrays, piece_rows):
    flat = jnp.concatenate([a.reshape(-1) for a in arrays])
    total = N_DEV * piece_rows * LANES
    return jnp.pad(flat, (0, total - flat.shape[0])).reshape(N_DEV, piece_rows, LANES)


def _unpack_small(packed, shapes):
    flat = packed.reshape(-1)
    out, off = [], 0
    for shp in shapes:
        n = 1
        for s in shp:
            n *= s
        out.append(flat[off:off + n].reshape(shp))
        off += n
    return out


def kernel(x, ffn1_w_gu, ffn1_w_down, ln1_g, ln1_b, mix_w_in, mix_b_in, conv_dw_w, conv_dw_b, conv_gn_g, conv_gn_b, conv_w_proj, rnn_conv_w, rnn_conv_b, rnn_w_a, rnn_b_a, rnn_w_x, rnn_b_x, rnn_lambda, rnn_w_proj, mix_w_out, ln2_g, ln2_b, ffn2_w_gu, ffn2_w_down, ln3_g, ln3_b, loss_target, m_ffn1_w_gu, m_ffn1_w_down, m_ln1_g, m_ln1_b, m_mix_w_in, m_mix_b_in, m_conv_dw_w, m_conv_dw_b, m_conv_gn_g, m_conv_gn_b, m_conv_w_proj, m_rnn_conv_w, m_rnn_conv_b, m_rnn_w_a, m_rnn_b_a, m_rnn_w_x, m_rnn_b_x, m_rnn_lambda, m_rnn_w_proj, m_mix_w_out, m_ln2_g, m_ln2_b, m_ffn2_w_gu, m_ffn2_w_down, m_ln3_g, m_ln3_b, v_ffn1_w_gu, v_ffn1_w_down, v_ln1_g, v_ln1_b, v_mix_w_in, v_mix_b_in, v_conv_dw_w, v_conv_dw_b, v_conv_gn_g, v_conv_gn_b, v_conv_w_proj, v_rnn_conv_w, v_rnn_conv_b, v_rnn_w_a, v_rnn_b_a, v_rnn_w_x, v_rnn_b_x, v_rnn_lambda, v_rnn_w_proj, v_mix_w_out, v_ln2_g, v_ln2_b, v_ffn2_w_gu, v_ffn2_w_down, v_ln3_g, v_ln3_b):
    args = locals()
    W = {n: args[n] for n in WEIGHT_NAMES}
    M = {n: args["m_" + n] for n in WEIGHT_NAMES}
    V = {n: args["v_" + n] for n in WEIGHT_NAMES}
    depth = ln1_g.shape[0]
    assert depth == 2, "each core of a chip moves one layer's weights and gradients"
    alpha = float((2 * depth) ** 0.25)
    S, D = x.shape[1], x.shape[2]
    F = ffn1_w_down.shape[1] * N_CHIPS
    R = rnn_w_proj.shape[1] * N_CHIPS
    chip = 2 * lax.axis_index("x") + lax.axis_index("y")

    for d in (W, M, V):
        d["mix_w_in"] = jnp.transpose(d["mix_w_in"], (0, 2, 1))

    names = SHARDED_MATS + SHARDED_VECS
    by_cols = [n in COL_SHARDED for n in names]
    srcs = [W[n].astype(BF16) for n in SHARDED_MATS] + [W[n] for n in SHARDED_VECS]
    gathered = dict(zip(names, _gather_weights(srcs, by_cols, "gather_weights")))
    stacked = {n: gathered[n] if n in COL_SHARDED else
               gathered[n].reshape((depth, N_CHIPS * gathered[n].shape[2]) + gathered[n].shape[3:]) for n in SHARDED_MATS}

    sections = (("cv", 0, D), ("cg", D, D), ("rx", 2 * D, R), ("rg", 2 * D + R, R), ("gc", 2 * D + 2 * R, D),
                ("gr", 3 * D + 2 * R, D))
    params = []
    for l in range(depth):
        p = {}
        for key, name in (("wgu1", "ffn1_w_gu"), ("wd1", "ffn1_w_down"), ("wgu2", "ffn2_w_gu"), ("wd2", "ffn2_w_down"),
                          ("wc", "conv_w_proj"), ("wr", "rnn_w_proj"), ("wo", "mix_w_out")):
            p[key] = (stacked[name], l)
        for s, off, width in sections:
            p["win_" + s] = stacked["mix_w_in"][l, off:off + width]
            p["bin_" + s] = _row(mix_b_in[l, off:off + width])
        p["conv_dw_w"] = gathered["conv_dw_w"][l]
        p["rnn_conv_w"] = _unshard_cols(gathered["rnn_conv_w"][l])
        p["wa"] = _embed_blocks(rnn_w_a[l], f"l{l}_embed_wa")
        p["wx"] = _embed_blocks(rnn_w_x[l], f"l{l}_embed_wx")
        for n in ("ln1_g", "ln1_b", "ln2_g", "ln2_b", "ln3_g", "ln3_b", "conv_dw_b", "conv_gn_g", "conv_gn_b", "rnn_conv_b",
                  "rnn_b_a", "rnn_b_x", "rnn_lambda"):
            p[n] = _row(W[n][l])
        params.append(p)

    h = x[0]
    saved = []
    for l in range(depth):
        h, sv = _layer_forward(h, params[l], alpha, l)
        saved.append(sv)
    loss_part, dy = _loss_head(h, loss_target[0], "loss_head")
    loss = lax.psum(loss_part[0, 0], ("x", "y", "c"))
    grads = [None] * depth
    for l in reversed(range(depth)):
        dy, grads[l] = _layer_backward(dy, params[l], saved[l], alpha, l)
    grad_x = dy[None]

    g0 = [grads[0][n] for n in SHARDED_MATS]
    g1 = [grads[1][n] for n in SHARDED_MATS]
    theirs = _sibling_exchange(g0, g1, "pair_exchange")
    mat_cols = [n in COL_SHARDED for n in SHARDED_MATS]
    chip_sums = []
    for n, bc, a0, a1, b in zip(SHARDED_MATS, mat_cols, g0, g1, theirs):
        cs = _pair_sum_bf16(a0, a1, b, "pair_sum_" + n)
        chip_sums.append(cs if bc else cs.reshape((N_CHIPS, cs.shape[0] // N_CHIPS, cs.shape[1])))
    rem, sib = _scatter_grads(chip_sums, mat_cols, "scatter_grads")
    out_g, out_d, out_m, out_v = {}, {}, {}, {}
    for n, bc, cs, rm, sb in zip(SHARDED_MATS, mat_cols, chip_sums, rem, sib):
        if bc:
            width = cs.shape[1] // N_CHIPS
            own = lax.dynamic_slice_in_dim(cs, chip * width, width, axis=1)
        else:
            own = lax.dynamic_index_in_dim(cs, chip, axis=0, keepdims=False)
        out_g[n], out_d[n], out_m[n], out_v[n] = _adamw_sharded(W[n], M[n], V[n], own, sb, rm, "adamw_" + n)
    for d in (out_g, out_d, out_m, out_v):
        d["mix_w_in"] = jnp.transpose(d["mix_w_in"], (0, 2, 1))

    small_grads = [jnp.stack([grads[l][n].reshape(W[n].shape[1:] if n not in SHARDED_VECS else
                                                   (W[n].shape[1], W[n].shape[2] * N_CHIPS)) for l in range(depth)])
                   for n in SMALL_NAMES]
    n_small = sum(int(a.size) for a in small_grads)
    piece_rows = -(-n_small // (N_DEV * LANES * SUBLANES)) * SUBLANES
    reduced = _unpack_small(_all_reduce_small(_pack_small(small_grads, piece_rows), "all_reduce_small"),
                            [a.shape for a in small_grads])
    local_g = []
    for n, gr in zip(SMALL_NAMES, reduced):
        if n in SHARDED_VECS:
            width = W[n].shape[2]
            gr = lax.dynamic_slice_in_dim(gr, chip * width, width, axis=2)
        local_g.append(gr)
    n_local = sum(int(a.size) for a in local_g)
    flat_rows = -(-n_local // (N_DEV * LANES * SUBLANES)) * SUBLANES * N_DEV
    pack = lambda arrs: _pack_small(arrs, flat_rows // N_DEV).reshape(flat_rows, LANES)
    shapes = [a.shape for a in local_g]
    deltas, new_m, new_v = _adamw_flat(pack([W[n] for n in SMALL_NAMES]), pack([M[n] for n in SMALL_NAMES]),
                                       pack([V[n] for n in SMALL_NAMES]), pack(local_g), "adamw_small")
    for n, gr, d_, m_, v_ in zip(SMALL_NAMES, local_g, _unpack_small(deltas, shapes), _unpack_small(new_m, shapes),
                                 _unpack_small(new_v, shapes)):
        out_g[n], out_d[n], out_m[n], out_v[n] = gr, d_, m_, v_

    return (loss, grad_x, *[out_g[n] for n in WEIGHT_NAMES], *[out_d[n] for n in WEIGHT_NAMES],
            *[out_m[n] for n in WEIGHT_NAMES], *[out_v[n] for n in WEIGHT_NAMES])
```

```python
import functools

import jax
import jax.numpy as jnp
from jax import lax
from jax.experimental import pallas as pl
from jax.experimental.pallas import tpu as pltpu

F32 = jnp.float32
BF16 = jnp.bfloat16
MESH = pl.DeviceIdType.MESH

LN_EPS = 1e-5
CONV_GROUPS = 8
RNN_BLOCKS = 16
RG_LRU_C = 8.0
ADAM_LR = 0.001
ADAM_B1 = 0.9
ADAM_B2 = 0.999
ADAM_EPS = 1e-08
ADAM_WD = 0.01
ADAM_STEP = 10

LANES = 128
SUBLANES = 8
V7X_VMEM_BYTES = 64 << 20
VMEM_LIMIT_CAP = V7X_VMEM_BYTES - (6 << 20)
N_CHIPS = 4
N_DEV = 8
CONV_ROWS = 64
EW_ROWS = 1024
SCAN_SEGMENTS = 32
SCAN_UNROLL = 4


def _cparams(block_bytes):
    limit = min(VMEM_LIMIT_CAP, max(int(block_bytes) + (8 << 20), 24 << 20))
    return pltpu.CompilerParams(vmem_limit_bytes=limit)


def _nbytes(shape, dtype):
    n = 1
    for s in shape:
        n *= s
    return n * jnp.dtype(dtype).itemsize


def _divisor_tile(n, limit, quantum):
    if n <= limit:
        return n
    best = None
    for t in range(quantum, limit + 1, quantum):
        if n % t == 0:
            best = t
    assert best is not None, (n, limit, quantum)
    return best


def _bs(shape, imap, **kw):
    return pl.BlockSpec(shape, imap, **kw)


def _resident(shape):
    nd = len(shape)
    return pl.BlockSpec(shape, lambda *_: (0,) * nd, pipeline_mode=pl.Buffered(1))


def _layer_block(w, block, imap, **kw):
    arr, layer = w
    return arr, pl.BlockSpec((None,) + block, lambda *ids: (layer,) + imap(*ids), **kw)


def _layer_resident(w):
    arr, _ = w
    return _layer_block(w, arr.shape[1:], lambda *_: (0, 0), pipeline_mode=pl.Buffered(1))


def _sigmoid(x):
    return jax.nn.sigmoid(x)


def _dot(a, b):
    return jnp.dot(a, b, preferred_element_type=F32)


def _dot_nt(a, b):
    return lax.dot_general(a, b, (((1,), (1,)), ((), ())), preferred_element_type=F32)


def _dot_tn(a, b):
    return lax.dot_general(a, b, (((0,), (0,)), ((), ())), preferred_element_type=F32)


def _row_mean(z):
    return jnp.mean(z, axis=-1, keepdims=True)


def _lane_mean(z):
    hi = z.astype(BF16)
    lo = (z - hi.astype(F32)).astype(BF16)
    ones = jnp.full((2 * LANES, LANES), 1.0 / LANES, BF16)
    return jnp.dot(jnp.concatenate([hi, lo], axis=-1), ones, preferred_element_type=F32)


def _norm_fwd(z, g, b, mean=_row_mean):
    mu = mean(z)
    xc = z - mu
    var = mean(xc * xc)
    return xc * lax.rsqrt(var + LN_EPS) * g + b


def _norm_bwd(z, g, dy, mean=_row_mean):
    mu = mean(z)
    xc = z - mu
    var = mean(xc * xc)
    rstd = lax.rsqrt(var + LN_EPS)
    xhat = xc * rstd
    dxh = dy * g
    m1 = mean(dxh)
    m2 = mean(dxh * xhat)
    return rstd * (dxh - m1 - xhat * m2), xhat


GELU_K = 0.7978845608028654
GELU_C = 0.044715


def _gelu(x):
    return 0.5 * x * (1.0 + jnp.tanh(GELU_K * (x + GELU_C * x * x * x)))


def _gelu_grad(x):
    t = jnp.tanh(GELU_K * (x + GELU_C * x * x * x))
    return 0.5 * (1.0 + t) + 0.5 * x * (1.0 - t * t) * GELU_K * (1.0 + 3.0 * GELU_C * x * x)


def _softplus(y):
    return jnp.maximum(y, 0.0) + jnp.log1p(jnp.exp(-jnp.abs(y)))


def _neg_expm1(y):
    series = -y * (1.0 + y * (0.5 + y * (1.0 / 6.0 + y * (1.0 / 24.0 + y * (1.0 / 120.0 + y * (1.0 / 720.0))))))
    return jnp.where(y > -0.25, series, 1.0 - jnp.exp(y))


def _colsum(x):
    return jnp.sum(x, axis=0, keepdims=True)


def _shifted_taps(src_ref, base, rows, taps):
    acc = None
    for o, coef in taps:
        term = coef() * src_ref[pl.ds(base + o, rows), :]
        acc = term if acc is None else acc + term
    return acc


def _shifted_corr(src_ref, base, rows, d, acc_ref, offs):
    for k, o in enumerate(offs):
        prod = d * src_ref[pl.ds(base + o, rows), :]
        part = jnp.sum(prod.reshape(rows // SUBLANES, SUBLANES, prod.shape[-1]), axis=0)
        acc_ref[SUBLANES * k:SUBLANES * (k + 1), :] += part


def _front_pad(ktaps):
    return SUBLANES * ((ktaps - 1 + SUBLANES - 1) // SUBLANES)


def _pad_rows(ktaps):
    return _front_pad(ktaps) + SUBLANES


def _ffn_tiles(S, F):
    tm = _divisor_tile(S, 1024, 16)
    tf = _divisor_tile(F, 256, LANES)
    return tm, tf


def _ffn_fwd(x, wgu, wd, g, b, alpha, name):
    S, D = x.shape
    F = wd[0].shape[1]
    tm, tf = _ffn_tiles(S, F)
    nf = F // tf
    wg_arr, wg_spec = _layer_block(wgu, (D, tf), lambda i, j: (0, j))
    wu_arr, wu_spec = _layer_block(wgu, (D, tf), lambda i, j: (0, nf + j))
    wd_arr, wd_spec = _layer_block(wd, (tf, D), lambda i, j: (j, 0))

    def body(x_ref, wg_ref, wu_ref, wd_ref, g_ref, b_ref, y_ref, z_ref, hg_ref, hu_ref, acc_ref, xb_ref):
        j = pl.program_id(1)

        @pl.when(j == 0)
        def _():
            xb_ref[...] = x_ref[...].astype(BF16)
            acc_ref[...] = jnp.zeros_like(acc_ref)

        xb = xb_ref[...]
        hg = _dot(xb, wg_ref[...])
        hu = _dot(xb, wu_ref[...])
        hg_ref[...] = hg
        hu_ref[...] = hu
        a = (hg * _sigmoid(hg) * hu).astype(BF16)
        acc_ref[...] += _dot(a, wd_ref[...])

        @pl.when(j == nf - 1)
        def _():
            z = alpha * x_ref[...] + 0.5 * acc_ref[...]
            z_ref[...] = z
            y_ref[...] = _norm_fwd(z, g_ref[...], b_ref[...])

    blk = 2 * (3 * tm * D * 4 + 2 * tm * tf * 4 + 3 * D * tf * 2) + tm * D * 6 + 3 * tm * tf * 4
    return pl.pallas_call(
        body, name=name, grid=(S // tm, nf),
        in_specs=[_bs((tm, D), lambda i, j: (i, 0)), wg_spec, wu_spec, wd_spec,
                  _bs((1, D), lambda i, j: (0, 0)), _bs((1, D), lambda i, j: (0, 0))],
        out_specs=[_bs((tm, D), lambda i, j: (i, 0)), _bs((tm, D), lambda i, j: (i, 0)),
                   _bs((tm, tf), lambda i, j: (i, j)), _bs((tm, tf), lambda i, j: (i, j))],
        out_shape=[jax.ShapeDtypeStruct((S, D), F32), jax.ShapeDtypeStruct((S, D), F32),
                   jax.ShapeDtypeStruct((S, F), F32), jax.ShapeDtypeStruct((S, F), F32)],
        scratch_shapes=[pltpu.VMEM((tm, D), F32), pltpu.VMEM((tm, D), BF16)],
        compiler_params=_cparams(blk),
    )(x, wg_arr, wu_arr, wd_arr, g, b)


def _ffn_bwd(dy, z, hg, hu, wgu, wd, g, alpha, name):
    S, D = dy.shape
    F = wd[0].shape[1]
    tm, tf = _ffn_tiles(S, F)
    nf = F // tf
    wg_arr, wg_spec = _layer_block(wgu, (D, tf), lambda i, j: (0, j))
    wu_arr, wu_spec = _layer_block(wgu, (D, tf), lambda i, j: (0, nf + j))
    wd_arr, wd_spec = _layer_block(wd, (tf, D), lambda i, j: (j, 0))

    def body(dy_ref, z_ref, hg_ref, hu_ref, wg_ref, wu_ref, wd_ref, g_ref,
             dx_ref, df_ref, a_ref, dhg_ref, dhu_ref, dg_ref, db_ref, acc_ref):
        i = pl.program_id(0)
        j = pl.program_id(1)

        @pl.when((i == 0) & (j == 0))
        def _():
            dg_ref[...] = jnp.zeros_like(dg_ref)
            db_ref[...] = jnp.zeros_like(db_ref)

        @pl.when(j == 0)
        def _():
            dy_ = dy_ref[...]
            dz, xhat = _norm_bwd(z_ref[...], g_ref[...], dy_)
            dg_ref[...] += _colsum(dy_ * xhat)
            db_ref[...] += _colsum(dy_)
            acc_ref[...] = alpha * dz
            df_ref[...] = (0.5 * dz).astype(BF16)

        da = _dot_nt(df_ref[...], wd_ref[...])
        hg_ = hg_ref[...]
        hu_ = hu_ref[...]
        s = _sigmoid(hg_)
        sl = hg_ * s
        dgate = (da * hu_ * (s * (1.0 + hg_ * (1.0 - s)))).astype(BF16)
        dup = (da * sl).astype(BF16)
        a_ref[...] = (sl * hu_).astype(BF16)
        dhg_ref[...] = dgate
        dhu_ref[...] = dup
        acc_ref[...] += _dot_nt(dgate, wg_ref[...]) + _dot_nt(dup, wu_ref[...])

        @pl.when(j == nf - 1)
        def _():
            dx_ref[...] = acc_ref[...]

    blk = 2 * (2 * tm * D * 4 + tm * D * 2 + 2 * tm * tf * 4 + 3 * tm * tf * 2 + 3 * D * tf * 2) + 3 * tm * D * 4 + 8 * tm * tf * 4
    once = dict(pipeline_mode=pl.Buffered(1))
    return pl.pallas_call(
        body, name=name, grid=(S // tm, nf),
        in_specs=[_bs((tm, D), lambda i, j: (i, 0), **once), _bs((tm, D), lambda i, j: (i, 0), **once),
                  _bs((tm, tf), lambda i, j: (i, j)), _bs((tm, tf), lambda i, j: (i, j)),
                  wg_spec, wu_spec, wd_spec, _bs((1, D), lambda i, j: (0, 0))],
        out_specs=[_bs((tm, D), lambda i, j: (i, 0)), _bs((tm, D), lambda i, j: (i, 0)),
                   _bs((tm, tf), lambda i, j: (i, j)), _bs((tm, tf), lambda i, j: (i, j)), _bs((tm, tf), lambda i, j: (i, j)),
                   _bs((1, D), lambda i, j: (0, 0)), _bs((1, D), lambda i, j: (0, 0))],
        out_shape=[jax.ShapeDtypeStruct((S, D), F32), jax.ShapeDtypeStruct((S, D), BF16),
                   jax.ShapeDtypeStruct((S, F), BF16), jax.ShapeDtypeStruct((S, F), BF16), jax.ShapeDtypeStruct((S, F), BF16),
                   jax.ShapeDtypeStruct((1, D), F32), jax.ShapeDtypeStruct((1, D), F32)],
        scratch_shapes=[pltpu.VMEM((tm, D), F32)],
        compiler_params=_cparams(blk),
    )(dy, z, hg, hu, wg_arr, wu_arr, wd_arr, g)


def _mm_bias(x, wt, bias, name):
    S, K = x.shape
    N = wt.shape[0]
    tm = _divisor_tile(S, 512, 16)

    def body(x_ref, w_ref, b_ref, o_ref):
        o_ref[...] = _dot_nt(x_ref[...].astype(BF16), w_ref[...]) + b_ref[...]

    blk = 2 * (tm * K * 4 + tm * N * 4) + K * N * 2 + tm * K * 2
    return pl.pallas_call(
        body, name=name, grid=(S // tm,),
        in_specs=[_bs((tm, K), lambda i: (i, 0)), _resident((N, K)), _resident((1, N))],
        out_specs=_bs((tm, N), lambda i: (i, 0)),
        out_shape=jax.ShapeDtypeStruct((S, N), F32),
        compiler_params=_cparams(blk),
    )(x, wt, bias)


def _mm_tn(a, b, name):
    S, M = a.shape
    N = b.shape[1]
    bm = _divisor_tile(M, 1408, LANES)
    bn = _divisor_tile(N, 1408, LANES)
    tk = _divisor_tile(S, 512, 16)

    def body(a_ref, b_ref, o_ref):
        @pl.when(pl.program_id(2) == 0)
        def _():
            o_ref[...] = jnp.zeros_like(o_ref)

        o_ref[...] += _dot_tn(a_ref[...].astype(BF16), b_ref[...].astype(BF16))

    blk = 2 * (tk * bm * a.dtype.itemsize + tk * bn * b.dtype.itemsize + bm * bn * 4) + tk * bm * 4 + bm * bn * 4
    return pl.pallas_call(
        body, name=name, grid=(M // bm, N // bn, S // tk),
        in_specs=[_bs((tk, bm), lambda i, j, k: (k, i)), _bs((tk, bn), lambda i, j, k: (k, j))],
        out_specs=_bs((bm, bn), lambda i, j, k: (i, j)),
        out_shape=jax.ShapeDtypeStruct((M, N), F32),
        compiler_params=_cparams(blk),
    )(a, b)


def _mm_tn_pair(a, b0, b1, name):
    S, M = a.shape
    N = b0.shape[1]
    assert b1.shape == b0.shape
    bm = _divisor_tile(M, 1408, LANES)
    bn = _divisor_tile(N, 1408, LANES)
    tk = _divisor_tile(S, 512, 16)
    nb = N // bn

    def body(a_ref, b0_ref, b1_ref, o_ref):
        j = pl.program_id(1)

        @pl.when(pl.program_id(2) == 0)
        def _():
            o_ref[...] = jnp.zeros_like(o_ref)

        ab = a_ref[...].astype(BF16)

        @pl.when(j < nb)
        def _():
            o_ref[...] += _dot_tn(ab, b0_ref[...])

        @pl.when(j >= nb)
        def _():
            o_ref[...] += _dot_tn(ab, b1_ref[...])

    b0_map = lambda i, j, k: (jnp.where(j < nb, k, S // tk - 1), jnp.minimum(j, nb - 1))
    b1_map = lambda i, j, k: (jnp.where(j >= nb, k, 0), jnp.maximum(j - nb, 0))
    blk = 2 * (tk * bm * a.dtype.itemsize + 2 * tk * bn * 2 + bm * bn * 4) + tk * bm * 4 + bm * bn * 4
    return pl.pallas_call(
        body, name=name, grid=(M // bm, 2 * nb, S // tk),
        in_specs=[_bs((tk, bm), lambda i, j, k: (k, i)), _bs((tk, bn), b0_map), _bs((tk, bn), b1_map)],
        out_specs=_bs((bm, bn), lambda i, j, k: (i, j)),
        out_shape=jax.ShapeDtypeStruct((M, 2 * N), F32),
        compiler_params=_cparams(blk),
    )(a, b0, b1)


def _mix_dx(dz, parts, weights, alpha, name):
    S, D = dz.shape
    tm = _divisor_tile(S, 256, 16)
    n = len(parts)

    def body(*refs):
        dz_ref = refs[0]
        p_refs = refs[1:1 + n]
        w_refs = refs[1 + n:1 + 2 * n]
        o_ref = refs[1 + 2 * n]
        acc = alpha * dz_ref[...]
        for p_ref, w_ref in zip(p_refs, w_refs):
            acc = acc + _dot(p_ref[...], w_ref[...])
        o_ref[...] = acc

    widths = [p.shape[1] for p in parts]
    blk = 2 * (2 * tm * D * 4 + sum(tm * w * 2 for w in widths)) + sum(D * w * 2 for w in widths) + 2 * tm * D * 4
    return pl.pallas_call(
        body, name=name, grid=(S // tm,),
        in_specs=[_bs((tm, D), lambda i: (i, 0))] + [_bs((tm, w), lambda i: (i, 0)) for w in widths]
                 + [_resident((w, D)) for w in widths],
        out_specs=_bs((tm, D), lambda i: (i, 0)),
        out_shape=jax.ShapeDtypeStruct((S, D), F32),
        compiler_params=_cparams(blk),
    )(dz, *parts, *weights)


def _conv_branch_fwd(cv, cg, w, b, gg, gb, name):
    S, C = cv.shape
    K = w.shape[0]
    assert C // CONV_GROUPS == LANES
    padf = _front_pad(K)
    R = min(CONV_ROWS, S)
    E = min(EW_ROWS, S)

    def body(cv_ref, cg_ref, w_ref, b_ref, gg_ref, gb_ref, c2_ref, c4_ref, pad_ref):
        pad_ref[0:padf, :] = jnp.zeros((padf, LANES), F32)
        pad_ref[S + padf:S + padf + SUBLANES, :] = jnp.zeros((SUBLANES, LANES), F32)

        def fill(i, carry):
            r = pl.multiple_of(i * E, E)
            pad_ref[pl.ds(r + padf, E), :] = cv_ref[pl.ds(r, E), :] * _sigmoid(cg_ref[pl.ds(r, E), :])
            return carry

        lax.fori_loop(0, S // E, fill, 0)
        taps = [(padf - (K - 1) + k, functools.partial(lambda k: w_ref[k:k + 1, :], k)) for k in range(K)]

        def conv(i, carry):
            r = pl.multiple_of(i * R, R)
            c2_ref[pl.ds(r, R), :] = _shifted_taps(pad_ref, r, R, taps) + b_ref[...]
            return carry

        lax.fori_loop(0, S // R, conv, 0)

        def norm(i, carry):
            r = pl.multiple_of(i * E, E)
            c3 = _norm_fwd(c2_ref[pl.ds(r, E), :], gg_ref[...], gb_ref[...], _lane_mean)
            c4_ref[pl.ds(r, E), :] = (c3 * _sigmoid(c3)).astype(BF16)
            return carry

        lax.fori_loop(0, S // E, norm, 0)

    col = lambda i: (0, i)
    blk = 2 * (3 * S * LANES * 4 + S * LANES * 2) + (S + _pad_rows(K)) * LANES * 4
    return pl.pallas_call(
        body, name=name, grid=(C // LANES,),
        in_specs=[_bs((S, LANES), col), _bs((S, LANES), col), _bs((K, LANES), col),
                  _bs((1, LANES), col), _bs((1, LANES), col), _bs((1, LANES), col)],
        out_specs=[_bs((S, LANES), col), _bs((S, LANES), col)],
        out_shape=[jax.ShapeDtypeStruct((S, C), F32), jax.ShapeDtypeStruct((S, C), BF16)],
        scratch_shapes=[pltpu.VMEM((S + _pad_rows(K), LANES), F32)],
        compiler_params=_cparams(blk),
    )(cv, cg, w, b, gg, gb)


def _conv_branch_bwd(dc4, c2, cv, cg, w, gg, gb, name):
    S, C = cv.shape
    K = w.shape[0]
    padf = _front_pad(K)
    R = min(CONV_ROWS, S)
    E = min(EW_ROWS, S)

    def body(dc4_ref, c2_ref, cv_ref, cg_ref, w_ref, gg_ref, gb_ref,
             dcv_ref, dcg_ref, dw_ref, dwb_ref, dgg_ref, dgb_ref, scv_ref, scg_ref,
             dpad_ref, cpad_ref, dwacc_ref):
        cpad_ref[0:padf, :] = jnp.zeros((padf, LANES), F32)
        cpad_ref[S + padf:S + padf + SUBLANES, :] = jnp.zeros((SUBLANES, LANES), F32)
        dpad_ref[S:S + padf + SUBLANES, :] = jnp.zeros((padf + SUBLANES, LANES), F32)
        dwacc_ref[...] = jnp.zeros_like(dwacc_ref)
        for ref in (dwb_ref, dgg_ref, dgb_ref, scv_ref, scg_ref):
            ref[...] = jnp.zeros_like(ref)

        def norm_pass(i, carry):
            r = pl.multiple_of(i * E, E)
            g_ = gg_ref[...]
            c2 = c2_ref[pl.ds(r, E), :]
            xc = c2 - _lane_mean(c2)
            rstd = lax.rsqrt(_lane_mean(xc * xc) + LN_EPS)
            xhat = xc * rstd
            c3 = xhat * g_ + gb_ref[...]
            s = _sigmoid(c3)
            dc3 = dc4_ref[pl.ds(r, E), :] * (s * (1.0 + c3 * (1.0 - s)))
            dgg_ref[...] += _colsum(dc3 * xhat)
            dgb_ref[...] += _colsum(dc3)
            dxh = dc3 * g_
            dc2 = rstd * (dxh - _lane_mean(dxh) - xhat * _lane_mean(dxh * xhat))
            dpad_ref[pl.ds(r, E), :] = dc2
            dwb_ref[...] += _colsum(dc2)
            cpad_ref[pl.ds(r + padf, E), :] = cv_ref[pl.ds(r, E), :] * _sigmoid(cg_ref[pl.ds(r, E), :])
            return carry

        lax.fori_loop(0, S // E, norm_pass, 0)
        taps = [(K - 1 - k, functools.partial(lambda k: w_ref[k:k + 1, :], k)) for k in range(K)]
        offs = [padf - (K - 1) + k for k in range(K)]

        def conv_pass(i, carry):
            r = pl.multiple_of(i * R, R)
            dc1 = _shifted_taps(dpad_ref, r, R, taps)
            sg = _sigmoid(cg_ref[pl.ds(r, R), :])
            cv_ = cv_ref[pl.ds(r, R), :]
            dcv = dc1 * sg
            dcg = dc1 * cv_ * sg * (1.0 - sg)
            dcv_ref[pl.ds(r, R), :] = dcv.astype(BF16)
            dcg_ref[pl.ds(r, R), :] = dcg.astype(BF16)
            scv_ref[...] += _colsum(dcv)
            scg_ref[...] += _colsum(dcg)
            _shifted_corr(cpad_ref, r, R, dpad_ref[pl.ds(r, R), :], dwacc_ref, offs)
            return carry

        lax.fori_loop(0, S // R, conv_pass, 0)
        for k in range(K):
            dw_ref[k:k + 1, :] = _colsum(dwacc_ref[SUBLANES * k:SUBLANES * (k + 1), :])

    col = lambda i: (0, i)
    row = jax.ShapeDtypeStruct((1, C), F32)
    blk = 2 * (4 * S * LANES * 4 + 2 * S * LANES * 2) + 2 * (S + _pad_rows(K)) * LANES * 4
    return pl.pallas_call(
        body, name=name, grid=(C // LANES,),
        in_specs=[_bs((S, LANES), col)] * 4 + [_bs((K, LANES), col), _bs((1, LANES), col), _bs((1, LANES), col)],
        out_specs=[_bs((S, LANES), col), _bs((S, LANES), col), _bs((K, LANES), col)] + [_bs((1, LANES), col)] * 5,
        out_shape=[jax.ShapeDtypeStruct((S, C), BF16), jax.ShapeDtypeStruct((S, C), BF16),
                   jax.ShapeDtypeStruct((K, C), F32), row, row, row, row, row],
        scratch_shapes=[pltpu.VMEM((S + _pad_rows(K), LANES), F32), pltpu.VMEM((S + _pad_rows(K), LANES), F32),
                        pltpu.VMEM((SUBLANES * K, LANES), F32)],
        compiler_params=_cparams(blk),
    )(dc4, c2, cv, cg, w, gg, gb)


def _short_conv_fwd(xin, w, b, name):
    S, C = xin.shape
    K = w.shape[0]
    padf = _front_pad(K)
    R = min(CONV_ROWS, S)
    E = min(EW_ROWS, S)

    def body(x_ref, w_ref, b_ref, o_ref, pad_ref):
        pad_ref[0:padf, :] = jnp.zeros((padf, LANES), F32)
        pad_ref[S + padf:S + padf + SUBLANES, :] = jnp.zeros((SUBLANES, LANES), F32)

        def fill(i, carry):
            r = pl.multiple_of(i * E, E)
            pad_ref[pl.ds(r + padf, E), :] = x_ref[pl.ds(r, E), :]
            return carry

        lax.fori_loop(0, S // E, fill, 0)
        taps = [(padf - (K - 1) + k, functools.partial(lambda k: w_ref[k:k + 1, :], k)) for k in range(K)]

        def conv(i, carry):
            r = pl.multiple_of(i * R, R)
            o_ref[pl.ds(r, R), :] = _shifted_taps(pad_ref, r, R, taps) + b_ref[...]
            return carry

        lax.fori_loop(0, S // R, conv, 0)

    col = lambda i: (0, i)
    blk = 2 * (2 * S * LANES * 4) + (S + _pad_rows(K)) * LANES * 4
    return pl.pallas_call(
        body, name=name, grid=(C // LANES,),
        in_specs=[_bs((S, LANES), col), _bs((K, LANES), col), _bs((1, LANES), col)],
        out_specs=_bs((S, LANES), col),
        out_shape=jax.ShapeDtypeStruct((S, C), F32),
        scratch_shapes=[pltpu.VMEM((S + _pad_rows(K), LANES), F32)],
        compiler_params=_cparams(blk),
    )(xin, w, b)


def _short_conv_bwd(dy, xin, w, name):
    S, C = xin.shape
    K = w.shape[0]
    padf = _front_pad(K)
    R = min(CONV_ROWS, S)
    E = min(EW_ROWS, S)

    def body(dy_ref, x_ref, w_ref, dx_ref, dw_ref, db_ref, sx_ref, dpad_ref, xpad_ref, dwacc_ref):
        xpad_ref[0:padf, :] = jnp.zeros((padf, LANES), F32)
        xpad_ref[S + padf:S + padf + SUBLANES, :] = jnp.zeros((SUBLANES, LANES), F32)
        dpad_ref[S:S + padf + SUBLANES, :] = jnp.zeros((padf + SUBLANES, LANES), F32)
        dwacc_ref[...] = jnp.zeros_like(dwacc_ref)
        db_ref[...] = jnp.zeros_like(db_ref)
        sx_ref[...] = jnp.zeros_like(sx_ref)

        def fill(i, carry):
            r = pl.multiple_of(i * E, E)
            d = dy_ref[pl.ds(r, E), :]
            dpad_ref[pl.ds(r, E), :] = d
            db_ref[...] += _colsum(d)
            xpad_ref[pl.ds(r + padf, E), :] = x_ref[pl.ds(r, E), :]
            return carry

        lax.fori_loop(0, S // E, fill, 0)
        taps = [(K - 1 - k, functools.partial(lambda k: w_ref[k:k + 1, :], k)) for k in range(K)]
        offs = [padf - (K - 1) + k for k in range(K)]

        def conv_pass(i, carry):
            r = pl.multiple_of(i * R, R)
            dx = _shifted_taps(dpad_ref, r, R, taps)
            dx_ref[pl.ds(r, R), :] = dx.astype(BF16)
            sx_ref[...] += _colsum(dx)
            _shifted_corr(xpad_ref, r, R, dpad_ref[pl.ds(r, R), :], dwacc_ref, offs)
            return carry

        lax.fori_loop(0, S // R, conv_pass, 0)
        for k in range(K):
            dw_ref[k:k + 1, :] = _colsum(dwacc_ref[SUBLANES * k:SUBLANES * (k + 1), :])

    col = lambda i: (0, i)
    row = jax.ShapeDtypeStruct((1, C), F32)
    blk = 2 * (2 * S * LANES * 4 + S * LANES * 2) + 2 * (S + _pad_rows(K)) * LANES * 4
    return pl.pallas_call(
        body, name=name, grid=(C // LANES,),
        in_specs=[_bs((S, LANES), col), _bs((S, LANES), col), _bs((K, LANES), col)],
        out_specs=[_bs((S, LANES), col), _bs((K, LANES), col), _bs((1, LANES), col), _bs((1, LANES), col)],
        out_shape=[jax.ShapeDtypeStruct((S, C), BF16), jax.ShapeDtypeStruct((K, C), F32), row, row],
        scratch_shapes=[pltpu.VMEM((S + _pad_rows(K), LANES), F32), pltpu.VMEM((S + _pad_rows(K), LANES), F32),
                        pltpu.VMEM((SUBLANES * K, LANES), F32)],
        compiler_params=_cparams(blk),
    )(dy, xin, w)


def _band_panels(width, block):
    assert width % LANES == 0 and block <= LANES
    panels = []
    for c0 in range(0, width, 2 * LANES):
        c1 = min(width, c0 + 2 * LANES)
        r0 = (c0 // block) * block // LANES * LANES
        r1 = min(width, -(-(-(-c1 // block) * block) // LANES) * LANES)
        panels.append((r0, r1, c0, c1))
    return panels


def _gates_fwd(r1, wa, wx, ba, bx, lam, name):
    S, R = r1.shape
    tm = _divisor_tile(S, 256, 16)
    panels = _band_panels(R, R // RNN_BLOCKS)

    def body(r1_ref, wa_ref, wx_ref, ba_ref, bx_ref, lam_ref, ra_ref, ri_ref, a_ref, uu_ref):
        for r0, r1e, c0, c1 in panels:
            rb = r1_ref[:, r0:r1e].astype(BF16)
            ra = _sigmoid(_dot(rb, wa_ref[r0:r1e, c0:c1]) + ba_ref[:, c0:c1])
            ri = _sigmoid(_dot(rb, wx_ref[r0:r1e, c0:c1]) + bx_ref[:, c0:c1])
            log_a = -RG_LRU_C * ra * _softplus(-lam_ref[:, c0:c1])
            ra_ref[:, c0:c1] = ra
            ri_ref[:, c0:c1] = ri
            a_ref[:, c0:c1] = jnp.exp(log_a)
            uu_ref[:, c0:c1] = jnp.sqrt(_neg_expm1(2.0 * log_a)) * (ri * r1_ref[:, c0:c1])

    blk = 2 * (5 * tm * R * 4) + 2 * R * R * 2 + 6 * tm * R * 4
    tile = _bs((tm, R), lambda i: (i, 0))
    return pl.pallas_call(
        body, name=name, grid=(S // tm,),
        in_specs=[tile, _resident((R, R)), _resident((R, R)), _resident((1, R)), _resident((1, R)), _resident((1, R))],
        out_specs=[tile] * 4,
        out_shape=[jax.ShapeDtypeStruct((S, R), F32)] * 4,
        compiler_params=_cparams(blk),
    )(r1, wa, wx, ba, bx, lam)


def _gates_bwd(guu, da, ra, ri, r1, wa, wx, lam, name):
    S, R = r1.shape
    tm = _divisor_tile(S, 256, 16)
    nsteps = S // tm
    panels = _band_panels(R, R // RNN_BLOCKS)

    def body(g_ref, da_ref, ra_ref, ri_ref, r1_ref, wa_ref, wx_ref, lam_ref,
             dr1_ref, dpa_ref, dpx_ref, dba_ref, dbx_ref, dlam_ref):
        i = pl.program_id(0)

        @pl.when(i == 0)
        def _():
            dba_ref[...] = jnp.zeros_like(dba_ref)
            dbx_ref[...] = jnp.zeros_like(dbx_ref)
            dlam_ref[...] = jnp.zeros_like(dlam_ref)

        g = g_ref[...]
        ra = ra_ref[...]
        ri = ri_ref[...]
        r1_ = r1_ref[...]
        sp = _softplus(-lam_ref[...])
        log_a = -RG_LRU_C * ra * sp
        a = jnp.exp(log_a)
        mult = jnp.sqrt(_neg_expm1(2.0 * log_a))
        d_ri = g * mult * r1_
        dr1 = g * mult * ri
        dmult = g * ri * r1_
        dlog_a = da_ref[...] * a - dmult * (a * a) / mult
        dra = dlog_a * (-RG_LRU_C * sp)
        dlam_ref[...] += _colsum(dlog_a * (-RG_LRU_C * ra))
        dpa = dra * ra * (1.0 - ra)
        dpx = d_ri * ri * (1.0 - ri)
        dba_ref[...] += _colsum(dpa)
        dbx_ref[...] += _colsum(dpx)
        dpa_b = dpa.astype(BF16)
        dpx_b = dpx.astype(BF16)
        dpa_ref[...] = dpa_b
        dpx_ref[...] = dpx_b
        dr1_ref[...] = dr1
        for k0, k1, c0, c1 in panels:
            dr1_ref[:, c0:c1] += (_dot_nt(dpa_ref[:, k0:k1], wa_ref[c0:c1, k0:k1])
                                  + _dot_nt(dpx_ref[:, k0:k1], wx_ref[c0:c1, k0:k1]))

        @pl.when(i == nsteps - 1)
        def _():
            dlam_ref[...] = dlam_ref[...] * (-_sigmoid(-lam_ref[...]))

    blk = 2 * (6 * tm * R * 4 + 2 * tm * R * 2) + 2 * R * R * 2 + 10 * tm * R * 4
    tile = _bs((tm, R), lambda i: (i, 0))
    rowspec = _bs((1, R), lambda i: (0, 0))
    row = jax.ShapeDtypeStruct((1, R), F32)
    return pl.pallas_call(
        body, name=name, grid=(nsteps,),
        in_specs=[tile] * 5 + [_resident((R, R)), _resident((R, R)), _resident((1, R))],
        out_specs=[tile, tile, tile, rowspec, rowspec, rowspec],
        out_shape=[jax.ShapeDtypeStruct((S, R), F32), jax.ShapeDtypeStruct((S, R), BF16), jax.ShapeDtypeStruct((S, R), BF16),
                   row, row, row],
        compiler_params=_cparams(blk),
    )(guu, da, ra, ri, r1, wa, wx, lam)


def _embed_blocks(w, name):
    H, bk, _ = w.shape

    def body(w_ref, o_ref):
        o_ref[...] = jnp.zeros_like(o_ref)
        for h in range(H):
            o_ref[bk * h:bk * (h + 1), bk * h:bk * (h + 1)] = w_ref[h].astype(BF16)

    return pl.pallas_call(body, name=name, out_shape=jax.ShapeDtypeStruct((H * bk, H * bk), BF16),
                          compiler_params=_cparams(3 * H * bk * H * bk * 2))(w)


def _block_grads(r1, dpa, dpx, name):
    S, R = r1.shape
    bk = R // RNN_BLOCKS
    tk = _divisor_tile(S, 512, 16)
    nsteps = S // tk
    panels = _band_panels(R, bk)

    def body(r1_ref, dpa_ref, dpx_ref, ga_ref, gx_ref, acca_ref, accx_ref):
        k = pl.program_id(0)

        @pl.when(k == 0)
        def _():
            acca_ref[...] = jnp.zeros_like(acca_ref)
            accx_ref[...] = jnp.zeros_like(accx_ref)

        for k0, k1, c0, c1 in panels:
            rb = r1_ref[:, k0:k1].astype(BF16)
            acca_ref[k0:k1, c0:c1] += _dot_tn(rb, dpa_ref[:, c0:c1])
            accx_ref[k0:k1, c0:c1] += _dot_tn(rb, dpx_ref[:, c0:c1])

        @pl.when(k == nsteps - 1)
        def _():
            for h in range(RNN_BLOCKS):
                ga_ref[h] = acca_ref[bk * h:bk * (h + 1), bk * h:bk * (h + 1)]
                gx_ref[h] = accx_ref[bk * h:bk * (h + 1), bk * h:bk * (h + 1)]

    tile = lambda: _bs((tk, R), lambda k: (k, 0))
    out = _bs((RNN_BLOCKS, bk, bk), lambda k: (0, 0, 0))
    sds = jax.ShapeDtypeStruct((RNN_BLOCKS, bk, bk), F32)
    return pl.pallas_call(
        body, name=name, grid=(nsteps,),
        in_specs=[tile(), tile(), tile()], out_specs=[out, out], out_shape=[sds, sds],
        scratch_shapes=[pltpu.VMEM((R, R), F32), pltpu.VMEM((R, R), F32)],
        compiler_params=_cparams(2 * (tk * R * 8) + 2 * R * R * 4 + 4 * tk * R * 4),
    )(r1, dpa, dpx)


def _scan_geometry(S):
    nseg = SCAN_SEGMENTS if S % (SCAN_SEGMENTS * SUBLANES) == 0 else SUBLANES
    return nseg, S // nseg, nseg // SUBLANES


def _steps(n, step, init):
    u = SCAN_UNROLL if n % SCAN_UNROLL == 0 else 1

    def trip(t, carry):
        for k in range(u):
            carry = step(t * u + k, carry)
        return carry

    return lax.fori_loop(0, n // u, trip, init)


def _scan_fwd(a, u, name):
    S, C = a.shape
    nseg, L, V = _scan_geometry(S)

    def body(a_ref, u_ref, h_ref, e_ref, p_ref, init_ref):
        def rows(ref, v, j):
            return ref[pl.ds(v * SUBLANES * L + j, SUBLANES, stride=L), :]

        def run1(j, carry):
            hs, ps = carry
            nh, npd = [], []
            for v in range(V):
                aj = rows(a_ref, v, j)
                nh.append(aj * hs[v] + rows(u_ref, v, j))
                npd.append(aj * ps[v])
            return tuple(nh), tuple(npd)

        zero = jnp.zeros((SUBLANES, LANES), F32)
        one = jnp.ones((SUBLANES, LANES), F32)
        hs, ps = _steps(L, run1, ((zero,) * V, (one,) * V))
        for v in range(V):
            e_ref[SUBLANES * v:SUBLANES * (v + 1), :] = hs[v]
            p_ref[SUBLANES * v:SUBLANES * (v + 1), :] = ps[v]
        init_ref[0:1, :] = jnp.zeros((1, LANES), F32)
        for s in range(1, nseg):
            init_ref[s:s + 1, :] = e_ref[s - 1:s, :] + p_ref[s - 1:s, :] * init_ref[s - 1:s, :]

        def run2(j, hs):
            nh = []
            for v in range(V):
                h = rows(a_ref, v, j) * hs[v] + rows(u_ref, v, j)
                h_ref[pl.ds(v * SUBLANES * L + j, SUBLANES, stride=L), :] = h
                nh.append(h)
            return tuple(nh)

        _steps(L, run2, tuple(init_ref[SUBLANES * v:SUBLANES * (v + 1), :] for v in range(V)))

    col = lambda i: (0, i)
    blk = 2 * (3 * S * LANES * 4)
    return pl.pallas_call(
        body, name=name, grid=(C // LANES,),
        in_specs=[_bs((S, LANES), col), _bs((S, LANES), col)],
        out_specs=_bs((S, LANES), col),
        out_shape=jax.ShapeDtypeStruct((S, C), F32),
        scratch_shapes=[pltpu.VMEM((nseg, LANES), F32)] * 3,
        compiler_params=_cparams(blk),
    )(a, u)


def _scan_bwd(a, dh, h, name):
    S, C = a.shape
    nseg, L, V = _scan_geometry(S)
    E = min(EW_ROWS, S)

    def body(a_ref, d_ref, h_ref, g_ref, da_ref, apad_ref, hpad_ref, e_ref, p_ref, init_ref):
        apad_ref[S:S + SUBLANES, :] = jnp.zeros((SUBLANES, LANES), F32)
        hpad_ref[0:SUBLANES, :] = jnp.zeros((SUBLANES, LANES), F32)

        def fill(i, carry):
            r = pl.multiple_of(i * E, E)
            apad_ref[pl.ds(r, E), :] = a_ref[pl.ds(r, E), :]
            hpad_ref[pl.ds(r + SUBLANES, E), :] = h_ref[pl.ds(r, E), :]
            return carry

        lax.fori_loop(0, S // E, fill, 0)

        def rows(ref, v, j, shift=0):
            return ref[pl.ds(v * SUBLANES * L + j + shift, SUBLANES, stride=L), :]

        def run1(jj, carry):
            j = L - 1 - jj
            gs, ps = carry
            ng, npd = [], []
            for v in range(V):
                bj = rows(apad_ref, v, j, 1)
                ng.append(bj * gs[v] + rows(d_ref, v, j))
                npd.append(bj * ps[v])
            return tuple(ng), tuple(npd)

        zero = jnp.zeros((SUBLANES, LANES), F32)
        one = jnp.ones((SUBLANES, LANES), F32)
        gs, ps = _steps(L, run1, ((zero,) * V, (one,) * V))
        for v in range(V):
            e_ref[SUBLANES * v:SUBLANES * (v + 1), :] = gs[v]
            p_ref[SUBLANES * v:SUBLANES * (v + 1), :] = ps[v]
        init_ref[nseg - 1:nseg, :] = jnp.zeros((1, LANES), F32)
        for s in range(nseg - 2, -1, -1):
            init_ref[s:s + 1, :] = e_ref[s + 1:s + 2, :] + p_ref[s + 1:s + 2, :] * init_ref[s + 1:s + 2, :]

        def run2(jj, gs):
            j = L - 1 - jj
            ng = []
            for v in range(V):
                g = rows(apad_ref, v, j, 1) * gs[v] + rows(d_ref, v, j)
                g_ref[pl.ds(v * SUBLANES * L + j, SUBLANES, stride=L), :] = g
                da_ref[pl.ds(v * SUBLANES * L + j, SUBLANES, stride=L), :] = g * rows(hpad_ref, v, j, SUBLANES - 1)
                ng.append(g)
            return tuple(ng)

        _steps(L, run2, tuple(init_ref[SUBLANES * v:SUBLANES * (v + 1), :] for v in range(V)))

    col = lambda i: (0, i)
    blk = 2 * (5 * S * LANES * 4) + 2 * (S + SUBLANES) * LANES * 4
    return pl.pallas_call(
        body, name=name, grid=(C // LANES,),
        in_specs=[_bs((S, LANES), col)] * 3,
        out_specs=[_bs((S, LANES), col)] * 2,
        out_shape=[jax.ShapeDtypeStruct((S, C), F32)] * 2,
        scratch_shapes=[pltpu.VMEM((S + SUBLANES, LANES), F32), pltpu.VMEM((S + SUBLANES, LANES), F32)]
                       + [pltpu.VMEM((nseg, LANES), F32)] * 3,
        compiler_params=_cparams(blk),
    )(a, dh, h)


def _mixer_out_fwd(c4, h, rg, gc, gr, x1, wc, wr, wo, g, b, alpha, name):
    S, D = x1.shape
    R = h.shape[1]
    tm = _divisor_tile(S, 256, 16)

    def body(c4_ref, h_ref, rg_ref, gc_ref, gr_ref, x_ref, wc_ref, wr_ref, wo_ref, g_ref, b_ref,
             yc_ref, yr_ref, z_ref, y_ref):
        yc = _dot(c4_ref[...], wc_ref[...])
        q = (h_ref[...] * _gelu(rg_ref[...])).astype(BF16)
        yr = _dot(q, wr_ref[...])
        yc_ref[...] = yc
        yr_ref[...] = yr
        m = (_sigmoid(gc_ref[...]) * yc + _sigmoid(gr_ref[...]) * yr).astype(BF16)
        z = alpha * x_ref[...] + _dot(m, wo_ref[...])
        z_ref[...] = z
        y_ref[...] = _norm_fwd(z, g_ref[...], b_ref[...])

    blk = 2 * (tm * D * 2 + 2 * tm * R * 4 + 7 * tm * D * 4) + (2 * D * D + R * D) * 2 + 6 * tm * D * 4
    td = _bs((tm, D), lambda i: (i, 0))
    tr = _bs((tm, R), lambda i: (i, 0))
    return pl.pallas_call(
        body, name=name, grid=(S // tm,),
        in_specs=[td, tr, tr, td, td, td, _layer_resident(wc)[1], _layer_resident(wr)[1], _layer_resident(wo)[1],
                  _resident((1, D)), _resident((1, D))],
        out_specs=[td] * 4,
        out_shape=[jax.ShapeDtypeStruct((S, D), F32)] * 4,
        compiler_params=_cparams(blk),
    )(c4, h, rg, gc, gr, x1, wc[0], wr[0], wo[0], g, b)


def _mixer_out_bwd(dy, z, g, wo, yc, yr, gc, gr, name):
    S, D = dy.shape
    tm = _divisor_tile(S, 256, 16)

    def body(dy_ref, z_ref, g_ref, wo_ref, yc_ref, yr_ref, gc_ref, gr_ref,
             dz_ref, dzb_ref, m_ref, dyc_ref, dyr_ref, dgc_ref, dgr_ref, sgc_ref, sgr_ref, dg_ref, db_ref):
        @pl.when(pl.program_id(0) == 0)
        def _():
            for ref in (sgc_ref, sgr_ref, dg_ref, db_ref):
                ref[...] = jnp.zeros_like(ref)

        dy_ = dy_ref[...]
        dz, xhat = _norm_bwd(z_ref[...], g_ref[...], dy_)
        dg_ref[...] += _colsum(dy_ * xhat)
        db_ref[...] += _colsum(dy_)
        dz_ref[...] = dz
        dzb = dz.astype(BF16)
        dzb_ref[...] = dzb
        dm = _dot_nt(dzb, wo_ref[...])
        yc = yc_ref[...]
        yr = yr_ref[...]
        sc = _sigmoid(gc_ref[...])
        sr = _sigmoid(gr_ref[...])
        m_ref[...] = (sc * yc + sr * yr).astype(BF16)
        dyc_ref[...] = (dm * sc).astype(BF16)
        dyr_ref[...] = (dm * sr).astype(BF16)
        dgc = dm * yc * sc * (1.0 - sc)
        dgr = dm * yr * sr * (1.0 - sr)
        dgc_ref[...] = dgc.astype(BF16)
        dgr_ref[...] = dgr.astype(BF16)
        sgc_ref[...] += _colsum(dgc)
        sgr_ref[...] += _colsum(dgr)

    blk = 2 * (7 * tm * D * 4 + 6 * tm * D * 2) + D * D * 2 + 8 * tm * D * 4
    td = _bs((tm, D), lambda i: (i, 0))
    rowspec = _bs((1, D), lambda i: (0, 0))
    row = jax.ShapeDtypeStruct((1, D), F32)
    bfd = jax.ShapeDtypeStruct((S, D), BF16)
    return pl.pallas_call(
        body, name=name, grid=(S // tm,),
        in_specs=[td, td, _resident((1, D)), _layer_resident(wo)[1], td, td, td, td],
        out_specs=[td] * 7 + [rowspec] * 4,
        out_shape=[jax.ShapeDtypeStruct((S, D), F32), bfd, bfd, bfd, bfd, bfd, bfd, row, row, row, row],
        compiler_params=_cparams(blk),
    )(dy, z, g, wo[0], yc, yr, gc, gr)


def _branch_bwd(dyc, dyr, wc, wr, h, rg, name):
    S, D = dyc.shape
    R = h.shape[1]
    tm = _divisor_tile(S, 256, 16)

    def body(dyc_ref, dyr_ref, wc_ref, wr_ref, h_ref, rg_ref, dc4_ref, dh_ref, drg_ref, q_ref, srg_ref):
        @pl.when(pl.program_id(0) == 0)
        def _():
            srg_ref[...] = jnp.zeros_like(srg_ref)

        dc4_ref[...] = _dot_nt(dyc_ref[...], wc_ref[...])
        dq = _dot_nt(dyr_ref[...], wr_ref[...])
        h_ = h_ref[...]
        rg_ = rg_ref[...]
        ge = _gelu(rg_)
        dh_ref[...] = dq * ge
        drg = dq * h_ * _gelu_grad(rg_)
        drg_ref[...] = drg.astype(BF16)
        srg_ref[...] += _colsum(drg)
        q_ref[...] = (h_ * ge).astype(BF16)

    blk = 2 * (2 * tm * D * 2 + tm * D * 4 + 3 * tm * R * 4 + 2 * tm * R * 2) + (D * D + R * D) * 2 + 6 * tm * R * 4
    td = _bs((tm, D), lambda i: (i, 0))
    tr = _bs((tm, R), lambda i: (i, 0))
    return pl.pallas_call(
        body, name=name, grid=(S // tm,),
        in_specs=[td, td, _layer_resident(wc)[1], _layer_resident(wr)[1], tr, tr],
        out_specs=[td, tr, tr, tr, _bs((1, R), lambda i: (0, 0))],
        out_shape=[jax.ShapeDtypeStruct((S, D), F32), jax.ShapeDtypeStruct((S, R), F32), jax.ShapeDtypeStruct((S, R), BF16),
                   jax.ShapeDtypeStruct((S, R), BF16), jax.ShapeDtypeStruct((1, R), F32)],
        compiler_params=_cparams(blk),
    )(dyc, dyr, wc[0], wr[0], h, rg)


def _loss_head(y, target, name):
    S, D = y.shape
    tm = _divisor_tile(S, 512, 16)
    nsteps = S // tm

    def body(y_ref, t_ref, loss_ref, dy_ref, acc_ref):
        i = pl.program_id(0)

        @pl.when(i == 0)
        def _():
            acc_ref[...] = jnp.zeros_like(acc_ref)

        err = y_ref[...] - t_ref[...]
        dy_ref[...] = err * (1.0 / D)
        acc_ref[...] += _colsum(err * err)

        @pl.when(i == nsteps - 1)
        def _():
            loss_ref[...] = jnp.sum(acc_ref[...], axis=-1, keepdims=True) * (0.5 / D)

    td = _bs((tm, D), lambda i: (i, 0))
    return pl.pallas_call(
        body, name=name, grid=(nsteps,),
        in_specs=[td, td],
        out_specs=[_bs((1, 1), lambda i: (0, 0)), td],
        out_shape=[jax.ShapeDtypeStruct((1, 1), F32), jax.ShapeDtypeStruct((S, D), F32)],
        scratch_shapes=[pltpu.VMEM((1, D), F32)],
        compiler_params=_cparams(2 * 3 * tm * D * 4),
    )(y, target)


def _adamw_math(w, g, m, v):
    m = ADAM_B1 * m + (1.0 - ADAM_B1) * g
    v = ADAM_B2 * v + (1.0 - ADAM_B2) * (g * g)
    m_hat = m / (1.0 - ADAM_B1 ** ADAM_STEP)
    v_hat = v / (1.0 - ADAM_B2 ** ADAM_STEP)
    delta = -ADAM_LR * (m_hat / (jnp.sqrt(v_hat) + ADAM_EPS) + ADAM_WD * w)
    return delta, m, v


def _adamw_sharded(w, m, v, own, sib, rem, name):
    _, r, c = w.shape
    tr = _divisor_tile(r, max(16, (1 << 20) // (4 * c) // 16 * 16), 16)

    def body(w_ref, m_ref, v_ref, own_ref, sib_ref, rem_ref, g_ref, d_ref, nm_ref, nv_ref):
        mine = pl.program_id(0) == lax.axis_index("c")
        g = jnp.where(mine, own_ref[...], sib_ref[...]).astype(F32)
        for j in range(N_CHIPS - 1):
            g = g + rem_ref[j].astype(F32)
        delta, nm, nv = _adamw_math(w_ref[...], g, m_ref[...], v_ref[...])
        g_ref[...] = g
        d_ref[...] = delta
        nm_ref[...] = nm
        nv_ref[...] = nv

    tile = _bs((None, tr, c), lambda l, i: (l, i, 0))
    flat = _bs((tr, c), lambda l, i: (i, 0))
    sds = jax.ShapeDtypeStruct(w.shape, F32)
    return pl.pallas_call(
        body, name=name, grid=(2, r // tr),
        in_specs=[tile, tile, tile, flat, flat, _bs((N_CHIPS - 1, None, tr, c), lambda l, i: (0, l, i, 0))],
        out_specs=[tile] * 4,
        out_shape=[sds] * 4,
        compiler_params=_cparams(2 * (7 * tr * c * 4 + (N_CHIPS + 1) * tr * c * 2) + 6 * tr * c * 4),
    )(w, m, v, own, sib, rem)


def _adamw_flat(w, m, v, g, name):
    rows = w.shape[0]
    tr = _divisor_tile(rows, 1024, SUBLANES)

    def body(w_ref, m_ref, v_ref, g_ref, d_ref, nm_ref, nv_ref):
        delta, nm, nv = _adamw_math(w_ref[...], g_ref[...], m_ref[...], v_ref[...])
        d_ref[...] = delta
        nm_ref[...] = nm
        nv_ref[...] = nv

    tile = _bs((tr, LANES), lambda i: (i, 0))
    sds = jax.ShapeDtypeStruct(w.shape, F32)
    return pl.pallas_call(
        body, name=name, grid=(rows // tr,),
        in_specs=[tile] * 4, out_specs=[tile] * 3, out_shape=[sds] * 3,
        compiler_params=_cparams(2 * 7 * tr * LANES * 4),
    )(w, m, v, g)


def _pair_sum_bf16(g0, g1, theirs, name):
    rows, c = g0.shape
    tr = _divisor_tile(rows, max(16, (1 << 20) // (4 * c) // 16 * 16), 16)

    def body(g0_ref, g1_ref, t_ref, o_ref):
        mine = jnp.where(lax.axis_index("c") == 0, g0_ref[...], g1_ref[...])
        o_ref[...] = (mine + t_ref[...]).astype(BF16)

    tile = _bs((tr, c), lambda i: (i, 0))
    return pl.pallas_call(
        body, name=name, grid=(rows // tr,),
        in_specs=[tile, tile, tile], out_specs=tile, out_shape=jax.ShapeDtypeStruct((rows, c), BF16),
        compiler_params=_cparams(2 * 4 * tr * c * 4),
    )(g0, g1, theirs)


ANY = pl.BlockSpec(memory_space=pl.ANY)


def _mesh_position():
    return lax.axis_index("x"), lax.axis_index("y"), lax.axis_index("c")


def _other_chips():
    x, y, c = _mesh_position()
    chips = [(1 - x, y), (x, 1 - y), (1 - x, 1 - y)]
    return 2 * x + y, (x, y, 1 - c), chips, [2 * cx + cy for cx, cy in chips]


def _chip_slab(ref, k, width, by_cols):
    if by_cols:
        start = k * width if isinstance(k, int) else pl.multiple_of(k * width, LANES)
        return ref.at[:, pl.ds(start, width)]
    return ref.at[k]


HBM = pl.BlockSpec(memory_space=pltpu.HBM)
SEM = pl.BlockSpec(memory_space=pltpu.SEMAPHORE)
DATAFLOW = pltpu.SideEffectType.DATAFLOW_SIDE_EFFECTING
N_GATHER_COPIES = 4


def _land_shape(src, by_cols):
    return src.shape[:-1] + (N_CHIPS * src.shape[-1],) if by_cols else (N_CHIPS,) + src.shape


def _gather_copy(src_ref, land_ref, by_cols, send_sems, recv_sems, pos, j, slab, to):
    width = src_ref.shape[-1]
    return pltpu.make_async_remote_copy(src_ref=src_ref, dst_ref=_chip_slab(land_ref, slab, width, by_cols),
                                        send_sem=send_sems.at[N_GATHER_COPIES * pos + j],
                                        recv_sem=recv_sems.at[N_GATHER_COPIES * pos + j],
                                        device_id=to, device_id_type=MESH)


def _gather_start(srcs, by_cols, groups, name):
    U = len(srcs)
    G = len(groups)
    lands = [lax.empty(_land_shape(s, bc), s.dtype) for s, bc in zip(srcs, by_cols)]

    def body(*refs):
        src = refs[:U]
        land = refs[U:2 * U]
        send_sems = refs[2 * U:2 * U + G]
        recv_sems = refs[2 * U + G:2 * U + 2 * G]
        token = refs[-1]
        c = lax.axis_index("c")
        me, sibling, chips, _ = _other_chips()
        targets = [(*chip, c) for chip in chips] + [sibling]
        for g, members in enumerate(groups):
            for pos, u in enumerate(members):
                for j, to in enumerate(targets):
                    _gather_copy(src[u], land[u], by_cols[u], send_sems[g], recv_sems[g], pos, j, me, to).start()
        token[...] = jnp.zeros_like(token)

    sem_shapes = [pltpu.SemaphoreType.DMA((len(m) * N_GATHER_COPIES,)) for m in groups]
    outs = pl.pallas_call(
        body, name=name,
        out_shape=tuple(sem_shapes + sem_shapes + [pltpu.HBM(s.shape, s.dtype) for s in srcs]
                        + [pltpu.HBM(v.shape, v.dtype) for v in lands] + [jax.ShapeDtypeStruct((SUBLANES, LANES), F32)]),
        in_specs=[HBM] * (2 * U),
        out_specs=tuple([SEM] * (2 * G) + [HBM] * (2 * U) + [pl.BlockSpec(memory_space=pltpu.VMEM)]),
        input_output_aliases={i: 2 * G + i for i in range(2 * U)},
        compiler_params=pltpu.CompilerParams(has_side_effects=DATAFLOW),
    )(*[pltpu.with_memory_space_constraint(a, pltpu.HBM) for a in list(srcs) + lands])
    return outs[:G], outs[G:2 * G], outs[2 * G:2 * G + U], outs[2 * G + U:2 * G + 2 * U]


def _gather_wait(srcs, lands, by_cols, send_sems, recv_sems, after, name):
    n = len(srcs)

    def body(*refs):
        src = refs[:n]
        land = refs[n:2 * n]
        send_ref, recv_ref = refs[2 * n:2 * n + 2]
        _, sibling, _, _ = _other_chips()
        for pos in range(n):
            for j in range(N_GATHER_COPIES):
                cp = _gather_copy(src[pos], land[pos], by_cols[pos], send_ref, recv_ref, pos, j, 0, sibling)
                cp.wait_send()
                cp.wait_recv()

    outs = pl.pallas_call(
        body, name=name,
        out_shape=tuple([pltpu.HBM(s.shape, s.dtype) for s in srcs] + [pltpu.HBM(v.shape, v.dtype) for v in lands]),
        in_specs=[HBM] * (2 * n) + [SEM, SEM, pl.BlockSpec(memory_space=pl.ANY)],
        out_specs=tuple([HBM] * (2 * n)),
        input_output_aliases={i: i for i in range(2 * n)},
        compiler_params=pltpu.CompilerParams(has_side_effects=DATAFLOW),
    )(*srcs, *lands, send_sems, recv_sems, after)
    return outs[n:]


def _scatter_grads(csums, by_cols, name):
    n = len(csums)
    shard = [(s.shape[0], s.shape[1] // N_CHIPS) if bc else s.shape[1:] for s, bc in zip(csums, by_cols)]

    def body(*refs):
        src = refs[:n]
        rem = refs[n:2 * n]
        sib = refs[2 * n:3 * n]
        send_sems, recv_sems = refs[3 * n:]
        c = lax.axis_index("c")
        me, sibling, chips, chip_ids = _other_chips()

        def remote(i, k, src_ref, dst_ref, to):
            return pltpu.make_async_remote_copy(src_ref=src_ref, dst_ref=dst_ref, send_sem=send_sems.at[i, k],
                                                recv_sem=recv_sems.at[i, k], device_id=to, device_id_type=MESH)

        def part(i, k):
            return _chip_slab(src[i], k, shard[i][-1], by_cols[i])

        started = []
        for i in range(n):
            for j in range(3):
                started.append(remote(i, j, part(i, chip_ids[j]), rem[i].at[j, c], (*chips[j], c)))
            started.append(remote(i, 6, part(i, me), sib[i], sibling))
        for cp in started:
            cp.start()
        for i in range(n):
            for j in range(3):
                slot = rem[i].at[j, c]
                remote(i, j, slot, slot, sibling).wait_recv()
                fwd = remote(i, 3 + j, slot, slot, sibling)
                fwd.start()
                started.append(fwd)
        for i in range(n):
            for j in range(3):
                slot = rem[i].at[j, 1 - c]
                remote(i, 3 + j, slot, slot, sibling).wait_recv()
            remote(i, 6, sib[i], sib[i], sibling).wait_recv()
        for cp in started:
            cp.wait_send()

    out_shape = ([jax.ShapeDtypeStruct((N_CHIPS - 1, 2) + tuple(sh), s.dtype) for s, sh in zip(csums, shard)]
                 + [jax.ShapeDtypeStruct(tuple(sh), s.dtype) for s, sh in zip(csums, shard)])
    outs = pl.pallas_call(
        body, name=name,
        in_specs=[ANY] * n, out_specs=[ANY] * (2 * n), out_shape=out_shape,
        scratch_shapes=[pltpu.SemaphoreType.DMA((n, 7)), pltpu.SemaphoreType.DMA((n, 7))],
    )(*csums)
    return outs[:n], outs[n:]


def _sibling_exchange(g0, g1, name):
    n = len(g0)

    def body(*refs):
        layers = (refs[:n], refs[n:2 * n])
        theirs = refs[2 * n:3 * n]
        send_sems, recv_sems = refs[3 * n:]
        x, y, c = _mesh_position()

        def remote(i, src_ref):
            return pltpu.make_async_remote_copy(src_ref=src_ref, dst_ref=theirs[i], send_sem=send_sems.at[i],
                                                recv_sem=recv_sems.at[i], device_id=(x, y, 1 - c), device_id_type=MESH)

        for keep in range(2):
            @pl.when(c == keep)
            def _():
                for i in range(n):
                    remote(i, layers[1 - keep][i]).start()

        for i in range(n):
            remote(i, layers[0][i]).wait()

    return pl.pallas_call(
        body, name=name,
        in_specs=[ANY] * (2 * n), out_specs=[ANY] * n, out_shape=[jax.ShapeDtypeStruct(g.shape, g.dtype) for g in g0],
        scratch_shapes=[pltpu.SemaphoreType.DMA((n,)), pltpu.SemaphoreType.DMA((n,))],
    )(*g0, *g1)


def _all_reduce_small(v, name):
    _, rows, _ = v.shape

    def body(v_ref, o_ref, recv_ref, send_sems, recv_sems):
        x, y, c = _mesh_position()
        me = 4 * x + 2 * y + c
        peers = []
        for d in range(1, N_DEV):
            px, py, pc = x ^ ((d >> 2) & 1), y ^ ((d >> 1) & 1), c ^ (d & 1)
            peers.append(((px, py, pc), 4 * px + 2 * py + pc))

        def remote(k, src_ref, dst_ref, to):
            return pltpu.make_async_remote_copy(src_ref=src_ref, dst_ref=dst_ref, send_sem=send_sems.at[k],
                                                recv_sem=recv_sems.at[k], device_id=to, device_id_type=MESH)

        scatter = [remote(d, v_ref.at[pid], recv_ref.at[me], to) for d, (to, pid) in enumerate(peers)]
        for cp in scatter:
            cp.start()
        recv_ref[pl.ds(me, 1)] = v_ref[pl.ds(me, 1)]
        for d, (to, pid) in enumerate(peers):
            remote(d, v_ref.at[pid], recv_ref.at[pid], to).wait_recv()
        total = recv_ref[0]
        for s in range(1, N_DEV):
            total = total + recv_ref[s]
        o_ref[pl.ds(me, 1)] = total[None]
        gather = [remote(N_DEV - 1 + d, o_ref.at[me], o_ref.at[me], to) for d, (to, pid) in enumerate(peers)]
        for cp in gather:
            cp.start()
        for d, (to, pid) in enumerate(peers):
            remote(N_DEV - 1 + d, o_ref.at[pid], o_ref.at[pid], to).wait_recv()
        for cp in scatter + gather:
            cp.wait_send()

    vm = pl.BlockSpec(memory_space=pltpu.VMEM)
    return pl.pallas_call(
        body, name=name,
        in_specs=[vm], out_specs=vm, out_shape=jax.ShapeDtypeStruct(v.shape, F32),
        scratch_shapes=[pltpu.VMEM(v.shape, F32), pltpu.SemaphoreType.DMA((2 * (N_DEV - 1),)),
                        pltpu.SemaphoreType.DMA((2 * (N_DEV - 1),))],
        compiler_params=_cparams(4 * _nbytes(v.shape, F32)),
    )(v)


SHARDED_MATS = ("ffn1_w_gu", "ffn1_w_down", "mix_w_in", "conv_w_proj", "rnn_w_proj", "mix_w_out", "ffn2_w_gu", "ffn2_w_down")
COL_SHARDED = ("ffn1_w_gu", "ffn2_w_gu", "conv_dw_w")
SHARDED_VECS = ("conv_dw_w", "rnn_conv_w")
WEIGHT_NAMES = ("ffn1_w_gu", "ffn1_w_down", "ln1_g", "ln1_b", "mix_w_in", "mix_b_in", "conv_dw_w", "conv_dw_b", "conv_gn_g",
                "conv_gn_b", "conv_w_proj", "rnn_conv_w", "rnn_conv_b", "rnn_w_a", "rnn_b_a", "rnn_w_x", "rnn_b_x",
                "rnn_lambda", "rnn_w_proj", "mix_w_out", "ln2_g", "ln2_b", "ffn2_w_gu", "ffn2_w_down", "ln3_g", "ln3_b")
SMALL_NAMES = tuple(n for n in WEIGHT_NAMES if n not in SHARDED_MATS)


def _unshard_cols(gathered):
    k4, K, n = gathered.shape
    return jnp.transpose(gathered, (1, 0, 2)).reshape(K, k4 * n)


def _row(v):
    return v.reshape(1, -1)


def _layer_forward(x0, p, alpha, l, hooks):
    t = f"l{l}_"
    sv = {"x0": x0}
    x1, sv["z1"], sv["hg1"], sv["hu1"] = _ffn_fwd(x0, p["wgu1"], p["wd1"], p["ln1_g"], p["ln1_b"], alpha, t + "ffn1_fwd")
    sv["x1"] = x1
    hooks.get("after_ffn1", lambda v: None)(x1)
    sec = {}
    for s in ("cv", "cg", "rx", "rg", "gc", "gr"):
        sec[s] = _mm_bias(x1, p["win_" + s], p["bin_" + s], t + "win_" + s)
    sv.update(sec)
    sv["c2"], c4 = _conv_branch_fwd(sec["cv"], sec["cg"], p["conv_dw_w"], p["conv_dw_b"], p["conv_gn_g"], p["conv_gn_b"], t + "conv_fwd")
    sv["c4"] = c4
    r1 = _short_conv_fwd(sec["rx"], p["rnn_conv_w"], p["rnn_conv_b"], t + "rconv_fwd")
    sv["r1"] = r1
    sv["ra"], sv["ri"], a, uu = _gates_fwd(r1, p["wa"], p["wx"], p["rnn_b_a"], p["rnn_b_x"], p["rnn_lambda"], t + "gates_fwd")
    sv["a"] = a
    h = _scan_fwd(a, uu, t + "scan_fwd")
    sv["h"] = h
    sv["yc"], sv["yr"], sv["z2"], x2 = _mixer_out_fwd(c4, h, sec["rg"], sec["gc"], sec["gr"], x1, p["wc"], p["wr"], p["wo"],
                                                      p["ln2_g"], p["ln2_b"], alpha, t + "mixout_fwd")
    sv["x2"] = x2
    hooks.get("after_mixer", lambda v: None)(x2)
    x3, sv["z3"], sv["hg2"], sv["hu2"] = _ffn_fwd(x2, p["wgu2"], p["wd2"], p["ln3_g"], p["ln3_b"], alpha, t + "ffn2_fwd")
    return x3, sv


def _layer_backward(dy, p, sv, alpha, l):
    t = f"l{l}_"
    g = {}
    dx2, df, a_act, dhg, dhu, g["ln3_g"], g["ln3_b"] = _ffn_bwd(dy, sv["z3"], sv["hg2"], sv["hu2"], p["wgu2"], p["wd2"],
                                                                 p["ln3_g"], alpha, t + "ffn2_bwd")
    g["ffn2_w_down"] = _mm_tn(a_act, df, t + "dwd2")
    g["ffn2_w_gu"] = _mm_tn_pair(sv["x2"], dhg, dhu, t + "dwgu2")
    (dz2, dz2b, m_b, dyc, dyr, dgc, dgr, s_gc, s_gr, g["ln2_g"], g["ln2_b"]) = _mixer_out_bwd(
        dx2, sv["z2"], p["ln2_g"], p["wo"], sv["yc"], sv["yr"], sv["gc"], sv["gr"], t + "mixout_bwd")
    g["mix_w_out"] = _mm_tn(m_b, dz2b, t + "dwo")
    dc4, dh, drg, q_b, s_rg = _branch_bwd(dyc, dyr, p["wc"], p["wr"], sv["h"], sv["rg"], t + "branch_bwd")
    g["conv_w_proj"] = _mm_tn(sv["c4"], dyc, t + "dwc")
    g["rnn_w_proj"] = _mm_tn(q_b, dyr, t + "dwr")
    (dcv, dcg, g["conv_dw_w"], g["conv_dw_b"], g["conv_gn_g"], g["conv_gn_b"], s_cv, s_cg) = _conv_branch_bwd(
        dc4, sv["c2"], sv["cv"], sv["cg"], p["conv_dw_w"], p["conv_gn_g"], p["conv_gn_b"], t + "conv_bwd")
    guu, da = _scan_bwd(sv["a"], dh, sv["h"], t + "scan_bwd")
    dr1, dpa, dpx, g["rnn_b_a"], g["rnn_b_x"], g["rnn_lambda"] = _gates_bwd(
        guu, da, sv["ra"], sv["ri"], sv["r1"], p["wa"], p["wx"], p["rnn_lambda"], t + "gates_bwd")
    g["rnn_w_a"], g["rnn_w_x"] = _block_grads(sv["r1"], dpa, dpx, t + "dwax")
    drx, g["rnn_conv_w"], g["rnn_conv_b"], s_rx = _short_conv_bwd(dr1, sv["rx"], p["rnn_conv_w"], t + "rconv_bwd")
    du = {"cv": dcv, "cg": dcg, "rx": drx, "rg": drg, "gc": dgc, "gr": dgr}
    order = ("cv", "cg", "rx", "rg", "gc", "gr")
    g["mix_w_in"] = jnp.concatenate([_mm_tn(du[s], sv["x1"], t + "dwin_" + s) for s in order], axis=0)
    g["mix_b_in"] = jnp.concatenate([s_cv, s_cg, s_rx, s_rg, s_gc, s_gr], axis=1)
    dx1 = _mix_dx(dz2, [du[s] for s in order], [p["win_" + s] for s in order], alpha, t + "mix_dx")
    dx0, df, a_act, dhg, dhu, g["ln1_g"], g["ln1_b"] = _ffn_bwd(dx1, sv["z1"], sv["hg1"], sv["hu1"], p["wgu1"], p["wd1"],
                                                                 p["ln1_g"], alpha, t + "ffn1_bwd")
    g["ffn1_w_down"] = _mm_tn(a_act, df, t + "dwd1")
    g["ffn1_w_gu"] = _mm_tn_pair(sv["x0"], dhg, dhu, t + "dwgu1")
    return dx0, g


def _pack_small(arrays, piece_rows):
    flat = jnp.concatenate([a.reshape(-1) for a in arrays])
    total = N_DEV * piece_rows * LANES
    return jnp.pad(flat, (0, total - flat.shape[0])).reshape(N_DEV, piece_rows, LANES)


def _unpack_small(packed, shapes):
    flat = packed.reshape(-1)
    out, off = [], 0
    for shp in shapes:
        n = 1
        for s in shp:
            n *= s
        out.append(flat[off:off + n].reshape(shp))
        off += n
    return out


def kernel(x, ffn1_w_gu, ffn1_w_down, ln1_g, ln1_b, mix_w_in, mix_b_in, conv_dw_w, conv_dw_b, conv_gn_g, conv_gn_b, conv_w_proj, rnn_conv_w, rnn_conv_b, rnn_w_a, rnn_b_a, rnn_w_x, rnn_b_x, rnn_lambda, rnn_w_proj, mix_w_out, ln2_g, ln2_b, ffn2_w_gu, ffn2_w_down, ln3_g, ln3_b, loss_target, m_ffn1_w_gu, m_ffn1_w_down, m_ln1_g, m_ln1_b, m_mix_w_in, m_mix_b_in, m_conv_dw_w, m_conv_dw_b, m_conv_gn_g, m_conv_gn_b, m_conv_w_proj, m_rnn_conv_w, m_rnn_conv_b, m_rnn_w_a, m_rnn_b_a, m_rnn_w_x, m_rnn_b_x, m_rnn_lambda, m_rnn_w_proj, m_mix_w_out, m_ln2_g, m_ln2_b, m_ffn2_w_gu, m_ffn2_w_down, m_ln3_g, m_ln3_b, v_ffn1_w_gu, v_ffn1_w_down, v_ln1_g, v_ln1_b, v_mix_w_in, v_mix_b_in, v_conv_dw_w, v_conv_dw_b, v_conv_gn_g, v_conv_gn_b, v_conv_w_proj, v_rnn_conv_w, v_rnn_conv_b, v_rnn_w_a, v_rnn_b_a, v_rnn_w_x, v_rnn_b_x, v_rnn_lambda, v_rnn_w_proj, v_mix_w_out, v_ln2_g, v_ln2_b, v_ffn2_w_gu, v_ffn2_w_down, v_ln3_g, v_ln3_b):
    args = locals()
    W = {n: args[n] for n in WEIGHT_NAMES}
    M = {n: args["m_" + n] for n in WEIGHT_NAMES}
    V = {n: args["v_" + n] for n in WEIGHT_NAMES}
    depth = ln1_g.shape[0]
    assert depth == 2, "each core of a chip moves one layer's weights and gradients"
    alpha = float((2 * depth) ** 0.25)
    S, D = x.shape[1], x.shape[2]
    F = ffn1_w_down.shape[1] * N_CHIPS
    R = rnn_w_proj.shape[1] * N_CHIPS
    chip = 2 * lax.axis_index("x") + lax.axis_index("y")

    for d in (W, M, V):
        d["mix_w_in"] = jnp.transpose(d["mix_w_in"], (0, 2, 1))

    names = SHARDED_MATS + SHARDED_VECS
    first = [(0, "ffn1_w_gu"), (0, "ffn1_w_down")]
    unit_groups = [first, [(0, n) for n in names if (0, n) not in first], [(1, n) for n in names]]
    order = [u for g in unit_groups for u in g]
    index = {u: i for i, u in enumerate(order)}
    groups = [[index[u] for u in g] for g in unit_groups]
    srcs = [W[n][l].astype(BF16) if n in SHARDED_MATS else W[n][l] for l, n in order]
    by_cols = [n in COL_SHARDED for _, n in order]
    send_sems, recv_sems, src_thru, land_thru = _gather_start(srcs, by_cols, groups, "gather_start")

    sections = (("cv", 0, D), ("cg", D, D), ("rx", 2 * D, R), ("rg", 2 * D + R, R), ("gc", 2 * D + 2 * R, D),
                ("gr", 3 * D + 2 * R, D))
    keys = {"ffn1_w_gu": "wgu1", "ffn1_w_down": "wd1", "ffn2_w_gu": "wgu2", "ffn2_w_down": "wd2", "conv_w_proj": "wc",
            "rnn_w_proj": "wr", "mix_w_out": "wo"}
    params = []
    for l in range(depth):
        p = {"wa": _embed_blocks(rnn_w_a[l], f"l{l}_embed_wa"), "wx": _embed_blocks(rnn_w_x[l], f"l{l}_embed_wx")}
        for n in ("ln1_g", "ln1_b", "ln2_g", "ln2_b", "ln3_g", "ln3_b", "conv_dw_b", "conv_gn_g", "conv_gn_b", "rnn_conv_b",
                  "rnn_b_a", "rnn_b_x", "rnn_lambda"):
            p[n] = _row(W[n][l])
        for s, off, width in sections:
            p["bin_" + s] = _row(mix_b_in[l, off:off + width])
        params.append(p)

    def wait_group(g, after):
        ids = groups[g]
        landed = _gather_wait([src_thru[i] for i in ids], [land_thru[i] for i in ids], [by_cols[i] for i in ids],
                              send_sems[g], recv_sems[g], after, f"gather_wait{g}")
        for i, full in zip(ids, landed):
            l, n = order[i]
            p = params[l]
            if n not in COL_SHARDED:
                full = full.reshape((N_CHIPS * full.shape[1],) + full.shape[2:])
            if n == "mix_w_in":
                for s, off, width in sections:
                    p["win_" + s] = full[off:off + width]
            elif n == "rnn_conv_w":
                p[n] = _unshard_cols(landed[ids.index(i)])
            elif n == "conv_dw_w":
                p[n] = full
            else:
                p[keys[n]] = (full[None], 0)

    h = x[0]
    wait_group(0, h)
    saved = []
    hooks = [{"after_ffn1": lambda v: wait_group(1, v), "after_mixer": lambda v: wait_group(2, v)}, {}]
    for l in range(depth):
        h, sv = _layer_forward(h, params[l], alpha, l, hooks[l])
        saved.append(sv)
    loss_part, dy = _loss_head(h, loss_target[0], "loss_head")
    loss = lax.psum(loss_part[0, 0], ("x", "y", "c"))
    grads = [None] * depth
    for l in reversed(range(depth)):
        dy, grads[l] = _layer_backward(dy, params[l], saved[l], alpha, l)
    grad_x = dy[None]

    g0 = [grads[0][n] for n in SHARDED_MATS]
    g1 = [grads[1][n] for n in SHARDED_MATS]
    theirs = _sibling_exchange(g0, g1, "pair_exchange")
    mat_cols = [n in COL_SHARDED for n in SHARDED_MATS]
    chip_sums = []
    for n, bc, a0, a1, b in zip(SHARDED_MATS, mat_cols, g0, g1, theirs):
        cs = _pair_sum_bf16(a0, a1, b, "pair_sum_" + n)
        chip_sums.append(cs if bc else cs.reshape((N_CHIPS, cs.shape[0] // N_CHIPS, cs.shape[1])))
    rem, sib = _scatter_grads(chip_sums, mat_cols, "scatter_grads")
    out_g, out_d, out_m, out_v = {}, {}, {}, {}
    for n, bc, cs, rm, sb in zip(SHARDED_MATS, mat_cols, chip_sums, rem, sib):
        if bc:
            width = cs.shape[1] // N_CHIPS
            own = lax.dynamic_slice_in_dim(cs, chip * width, width, axis=1)
        else:
            own = lax.dynamic_index_in_dim(cs, chip, axis=0, keepdims=False)
        out_g[n], out_d[n], out_m[n], out_v[n] = _adamw_sharded(W[n], M[n], V[n], own, sb, rm, "adamw_" + n)
    for d in (out_g, out_d, out_m, out_v):
        d["mix_w_in"] = jnp.transpose(d["mix_w_in"], (0, 2, 1))

    small_grads = [jnp.stack([grads[l][n].reshape(W[n].shape[1:] if n not in SHARDED_VECS else
                                                   (W[n].shape[1], W[n].shape[2] * N_CHIPS)) for l in range(depth)])
                   for n in SMALL_NAMES]
    n_small = sum(int(a.size) for a in small_grads)
    piece_rows = -(-n_small // (N_DEV * LANES * SUBLANES)) * SUBLANES
    reduced = _unpack_small(_all_reduce_small(_pack_small(small_grads, piece_rows), "all_reduce_small"),
                            [a.shape for a in small_grads])
    local_g = []
    for n, gr in zip(SMALL_NAMES, reduced):
        if n in SHARDED_VECS:
            width = W[n].shape[2]
            gr = lax.dynamic_slice_in_dim(gr, chip * width, width, axis=2)
        local_g.append(gr)
    n_local = sum(int(a.size) for a in local_g)
    flat_rows = -(-n_local // (N_DEV * LANES * SUBLANES)) * SUBLANES * N_DEV
    pack = lambda arrs: _pack_small(arrs, flat_rows // N_DEV).reshape(flat_rows, LANES)
    shapes = [a.shape for a in local_g]
    deltas, new_m, new_v = _adamw_flat(pack([W[n] for n in SMALL_NAMES]), pack([M[n] for n in SMALL_NAMES]),
                                       pack([V[n] for n in SMALL_NAMES]), pack(local_g), "adamw_small")
    for n, gr, d_, m_, v_ in zip(SMALL_NAMES, local_g, _unpack_small(deltas, shapes), _unpack_small(new_m, shapes),
                                 _unpack_small(new_v, shapes)):
        out_g[n], out_d[n], out_m[n], out_v[n] = gr, d_, m_, v_

    return (loss, grad_x, *[out_g[n] for n in WEIGHT_NAMES], *[out_d[n] for n in WEIGHT_NAMES],
            *[out_m[n] for n in WEIGHT_NAMES], *[out_v[n] for n in WEIGHT_NAMES])
```

```python
import functools

import jax
import jax.numpy as jnp
from jax import lax
from jax.experimental import pallas as pl
from jax.experimental.pallas import tpu as pltpu

F32 = jnp.float32
BF16 = jnp.bfloat16
MESH = pl.DeviceIdType.MESH

LN_EPS = 1e-5
CONV_GROUPS = 8
RNN_BLOCKS = 16
RG_LRU_C = 8.0
ADAM_LR = 0.001
ADAM_B1 = 0.9
ADAM_B2 = 0.999
ADAM_EPS = 1e-08
ADAM_WD = 0.01
ADAM_STEP = 10

LANES = 128
SUBLANES = 8
V7X_VMEM_BYTES = 64 << 20
VMEM_LIMIT_CAP = V7X_VMEM_BYTES - (6 << 20)
N_CHIPS = 4
N_DEV = 8
CONV_ROWS = 64
EW_ROWS = 1024
SCAN_SEGMENTS = 32
SCAN_UNROLL = 4


def _cparams(block_bytes):
    limit = min(VMEM_LIMIT_CAP, max(int(block_bytes) + (8 << 20), 24 << 20))
    return pltpu.CompilerParams(vmem_limit_bytes=limit)


def _nbytes(shape, dtype):
    n = 1
    for s in shape:
        n *= s
    return n * jnp.dtype(dtype).itemsize


def _divisor_tile(n, limit, quantum):
    if n <= limit:
        return n
    best = None
    for t in range(quantum, limit + 1, quantum):
        if n % t == 0:
            best = t
    assert best is not None, (n, limit, quantum)
    return best


def _bs(shape, imap, **kw):
    return pl.BlockSpec(shape, imap, **kw)


def _resident(shape):
    nd = len(shape)
    return pl.BlockSpec(shape, lambda *_: (0,) * nd, pipeline_mode=pl.Buffered(1))


def _layer_block(w, block, imap, **kw):
    arr, layer = w
    return arr, pl.BlockSpec((None,) + block, lambda *ids: (layer,) + imap(*ids), **kw)


def _layer_resident(w):
    arr, _ = w
    return _layer_block(w, arr.shape[1:], lambda *_: (0, 0), pipeline_mode=pl.Buffered(1))


def _sigmoid(x):
    return jax.nn.sigmoid(x)


def _dot(a, b):
    return jnp.dot(a, b, preferred_element_type=F32)


def _dot_nt(a, b):
    return lax.dot_general(a, b, (((1,), (1,)), ((), ())), preferred_element_type=F32)


def _dot_tn(a, b):
    return lax.dot_general(a, b, (((0,), (0,)), ((), ())), preferred_element_type=F32)


def _row_mean(z):
    return jnp.mean(z, axis=-1, keepdims=True)


def _lane_mean(z):
    hi = z.astype(BF16)
    lo = (z - hi.astype(F32)).astype(BF16)
    ones = jnp.full((2 * LANES, LANES), 1.0 / LANES, BF16)
    return jnp.dot(jnp.concatenate([hi, lo], axis=-1), ones, preferred_element_type=F32)


def _norm_fwd(z, g, b, mean=_row_mean):
    mu = mean(z)
    xc = z - mu
    var = mean(xc * xc)
    return xc * lax.rsqrt(var + LN_EPS) * g + b


def _norm_bwd(z, g, dy, mean=_row_mean):
    mu = mean(z)
    xc = z - mu
    var = mean(xc * xc)
    rstd = lax.rsqrt(var + LN_EPS)
    xhat = xc * rstd
    dxh = dy * g
    m1 = mean(dxh)
    m2 = mean(dxh * xhat)
    return rstd * (dxh - m1 - xhat * m2), xhat


GELU_K = 0.7978845608028654
GELU_C = 0.044715


def _gelu(x):
    return 0.5 * x * (1.0 + jnp.tanh(GELU_K * (x + GELU_C * x * x * x)))


def _gelu_grad(x):
    t = jnp.tanh(GELU_K * (x + GELU_C * x * x * x))
    return 0.5 * (1.0 + t) + 0.5 * x * (1.0 - t * t) * GELU_K * (1.0 + 3.0 * GELU_C * x * x)


def _softplus(y):
    return jnp.maximum(y, 0.0) + jnp.log1p(jnp.exp(-jnp.abs(y)))


def _neg_expm1(y):
    series = -y * (1.0 + y * (0.5 + y * (1.0 / 6.0 + y * (1.0 / 24.0 + y * (1.0 / 120.0 + y * (1.0 / 720.0))))))
    return jnp.where(y > -0.25, series, 1.0 - jnp.exp(y))


def _colsum(x):
    return jnp.sum(x, axis=0, keepdims=True)


def _shifted_taps(src_ref, base, rows, taps):
    acc = None
    for o, coef in taps:
        term = coef() * src_ref[pl.ds(base + o, rows), :]
        acc = term if acc is None else acc + term
    return acc


def _shifted_corr(src_ref, base, rows, d, acc_ref, offs):
    for k, o in enumerate(offs):
        prod = d * src_ref[pl.ds(base + o, rows), :]
        part = jnp.sum(prod.reshape(rows // SUBLANES, SUBLANES, prod.shape[-1]), axis=0)
        acc_ref[SUBLANES * k:SUBLANES * (k + 1), :] += part


def _front_pad(ktaps):
    return SUBLANES * ((ktaps - 1 + SUBLANES - 1) // SUBLANES)


def _pad_rows(ktaps):
    return _front_pad(ktaps) + SUBLANES


def _ffn_tiles(S, F):
    tm = _divisor_tile(S, 1024, 16)
    tf = _divisor_tile(F, 256, LANES)
    return tm, tf


def _ffn_fwd(x, wgu, wd, g, b, alpha, name):
    S, D = x.shape
    F = wd[0].shape[1]
    tm, tf = _ffn_tiles(S, F)
    nf = F // tf
    wg_arr, wg_spec = _layer_block(wgu, (D, tf), lambda i, j: (0, j))
    wu_arr, wu_spec = _layer_block(wgu, (D, tf), lambda i, j: (0, nf + j))
    wd_arr, wd_spec = _layer_block(wd, (tf, D), lambda i, j: (j, 0))

    def body(x_ref, wg_ref, wu_ref, wd_ref, g_ref, b_ref, y_ref, z_ref, hg_ref, hu_ref, acc_ref, xb_ref):
        j = pl.program_id(1)

        @pl.when(j == 0)
        def _():
            xb_ref[...] = x_ref[...].astype(BF16)
            acc_ref[...] = jnp.zeros_like(acc_ref)

        xb = xb_ref[...]
        hg = _dot(xb, wg_ref[...])
        hu = _dot(xb, wu_ref[...])
        hg_ref[...] = hg
        hu_ref[...] = hu
        a = (hg * _sigmoid(hg) * hu).astype(BF16)
        acc_ref[...] += _dot(a, wd_ref[...])

        @pl.when(j == nf - 1)
        def _():
            z = alpha * x_ref[...] + 0.5 * acc_ref[...]
            z_ref[...] = z
            y_ref[...] = _norm_fwd(z, g_ref[...], b_ref[...])

    blk = 2 * (3 * tm * D * 4 + 2 * tm * tf * 4 + 3 * D * tf * 2) + tm * D * 6 + 3 * tm * tf * 4
    return pl.pallas_call(
        body, name=name, grid=(S // tm, nf),
        in_specs=[_bs((tm, D), lambda i, j: (i, 0)), wg_spec, wu_spec, wd_spec,
                  _bs((1, D), lambda i, j: (0, 0)), _bs((1, D), lambda i, j: (0, 0))],
        out_specs=[_bs((tm, D), lambda i, j: (i, 0)), _bs((tm, D), lambda i, j: (i, 0)),
                   _bs((tm, tf), lambda i, j: (i, j)), _bs((tm, tf), lambda i, j: (i, j))],
        out_shape=[jax.ShapeDtypeStruct((S, D), F32), jax.ShapeDtypeStruct((S, D), F32),
                   jax.ShapeDtypeStruct((S, F), F32), jax.ShapeDtypeStruct((S, F), F32)],
        scratch_shapes=[pltpu.VMEM((tm, D), F32), pltpu.VMEM((tm, D), BF16)],
        compiler_params=_cparams(blk),
    )(x, wg_arr, wu_arr, wd_arr, g, b)


def _ffn_bwd(dy, z, hg, hu, wgu, wd, g, alpha, name):
    S, D = dy.shape
    F = wd[0].shape[1]
    tm, tf = _ffn_tiles(S, F)
    nf = F // tf
    wg_arr, wg_spec = _layer_block(wgu, (D, tf), lambda i, j: (0, j))
    wu_arr, wu_spec = _layer_block(wgu, (D, tf), lambda i, j: (0, nf + j))
    wd_arr, wd_spec = _layer_block(wd, (tf, D), lambda i, j: (j, 0))

    def body(dy_ref, z_ref, hg_ref, hu_ref, wg_ref, wu_ref, wd_ref, g_ref,
             dx_ref, df_ref, a_ref, dhg_ref, dhu_ref, dg_ref, db_ref, acc_ref):
        i = pl.program_id(0)
        j = pl.program_id(1)

        @pl.when((i == 0) & (j == 0))
        def _():
            dg_ref[...] = jnp.zeros_like(dg_ref)
            db_ref[...] = jnp.zeros_like(db_ref)

        @pl.when(j == 0)
        def _():
            dy_ = dy_ref[...]
            dz, xhat = _norm_bwd(z_ref[...], g_ref[...], dy_)
            dg_ref[...] += _colsum(dy_ * xhat)
            db_ref[...] += _colsum(dy_)
            acc_ref[...] = alpha * dz
            df_ref[...] = (0.5 * dz).astype(BF16)

        da = _dot_nt(df_ref[...], wd_ref[...])
        hg_ = hg_ref[...]
        hu_ = hu_ref[...]
        s = _sigmoid(hg_)
        sl = hg_ * s
        dgate = (da * hu_ * (s * (1.0 + hg_ * (1.0 - s)))).astype(BF16)
        dup = (da * sl).astype(BF16)
        a_ref[...] = (sl * hu_).astype(BF16)
        dhg_ref[...] = dgate
        dhu_ref[...] = dup
        acc_ref[...] += _dot_nt(dgate, wg_ref[...]) + _dot_nt(dup, wu_ref[...])

        @pl.when(j == nf - 1)
        def _():
            dx_ref[...] = acc_ref[...]

    blk = 2 * (2 * tm * D * 4 + tm * D * 2 + 2 * tm * tf * 4 + 3 * tm * tf * 2 + 3 * D * tf * 2) + 3 * tm * D * 4 + 8 * tm * tf * 4
    once = dict(pipeline_mode=pl.Buffered(1))
    return pl.pallas_call(
        body, name=name, grid=(S // tm, nf),
        in_specs=[_bs((tm, D), lambda i, j: (i, 0), **once), _bs((tm, D), lambda i, j: (i, 0), **once),
                  _bs((tm, tf), lambda i, j: (i, j)), _bs((tm, tf), lambda i, j: (i, j)),
                  wg_spec, wu_spec, wd_spec, _bs((1, D), lambda i, j: (0, 0))],
        out_specs=[_bs((tm, D), lambda i, j: (i, 0)), _bs((tm, D), lambda i, j: (i, 0)),
                   _bs((tm, tf), lambda i, j: (i, j)), _bs((tm, tf), lambda i, j: (i, j)), _bs((tm, tf), lambda i, j: (i, j)),
                   _bs((1, D), lambda i, j: (0, 0)), _bs((1, D), lambda i, j: (0, 0))],
        out_shape=[jax.ShapeDtypeStruct((S, D), F32), jax.ShapeDtypeStruct((S, D), BF16),
                   jax.ShapeDtypeStruct((S, F), BF16), jax.ShapeDtypeStruct((S, F), BF16), jax.ShapeDtypeStruct((S, F), BF16),
                   jax.ShapeDtypeStruct((1, D), F32), jax.ShapeDtypeStruct((1, D), F32)],
        scratch_shapes=[pltpu.VMEM((tm, D), F32)],
        compiler_params=_cparams(blk),
    )(dy, z, hg, hu, wg_arr, wu_arr, wd_arr, g)


def _mm_bias(x, wt, bias, name):
    S, K = x.shape
    N = wt.shape[0]
    tm = _divisor_tile(S, 512, 16)

    def body(x_ref, w_ref, b_ref, o_ref):
        o_ref[...] = _dot_nt(x_ref[...].astype(BF16), w_ref[...]) + b_ref[...]

    blk = 2 * (tm * K * 4 + tm * N * 4) + K * N * 2 + tm * K * 2
    return pl.pallas_call(
        body, name=name, grid=(S // tm,),
        in_specs=[_bs((tm, K), lambda i: (i, 0)), _resident((N, K)), _resident((1, N))],
        out_specs=_bs((tm, N), lambda i: (i, 0)),
        out_shape=jax.ShapeDtypeStruct((S, N), F32),
        compiler_params=_cparams(blk),
    )(x, wt, bias)


def _mm_tn(a, b, name):
    S, M = a.shape
    N = b.shape[1]
    bm = _divisor_tile(M, 1408, LANES)
    bn = _divisor_tile(N, 1408, LANES)
    tk = _divisor_tile(S, 512, 16)

    def body(a_ref, b_ref, o_ref):
        @pl.when(pl.program_id(2) == 0)
        def _():
            o_ref[...] = jnp.zeros_like(o_ref)

        o_ref[...] += _dot_tn(a_ref[...].astype(BF16), b_ref[...].astype(BF16))

    blk = 2 * (tk * bm * a.dtype.itemsize + tk * bn * b.dtype.itemsize + bm * bn * 4) + tk * bm * 4 + bm * bn * 4
    return pl.pallas_call(
        body, name=name, grid=(M // bm, N // bn, S // tk),
        in_specs=[_bs((tk, bm), lambda i, j, k: (k, i)), _bs((tk, bn), lambda i, j, k: (k, j))],
        out_specs=_bs((bm, bn), lambda i, j, k: (i, j)),
        out_shape=jax.ShapeDtypeStruct((M, N), F32),
        compiler_params=_cparams(blk),
    )(a, b)


def _mm_tn_pair(a, b0, b1, name):
    S, M = a.shape
    N = b0.shape[1]
    assert b1.shape == b0.shape
    bm = _divisor_tile(M, 1408, LANES)
    bn = _divisor_tile(N, 1408, LANES)
    tk = _divisor_tile(S, 512, 16)
    nb = N // bn

    def body(a_ref, b0_ref, b1_ref, o_ref):
        j = pl.program_id(1)

        @pl.when(pl.program_id(2) == 0)
        def _():
            o_ref[...] = jnp.zeros_like(o_ref)

        ab = a_ref[...].astype(BF16)

        @pl.when(j < nb)
        def _():
            o_ref[...] += _dot_tn(ab, b0_ref[...])

        @pl.when(j >= nb)
        def _():
            o_ref[...] += _dot_tn(ab, b1_ref[...])

    b0_map = lambda i, j, k: (jnp.where(j < nb, k, S // tk - 1), jnp.minimum(j, nb - 1))
    b1_map = lambda i, j, k: (jnp.where(j >= nb, k, 0), jnp.maximum(j - nb, 0))
    blk = 2 * (tk * bm * a.dtype.itemsize + 2 * tk * bn * 2 + bm * bn * 4) + tk * bm * 4 + bm * bn * 4
    return pl.pallas_call(
        body, name=name, grid=(M // bm, 2 * nb, S // tk),
        in_specs=[_bs((tk, bm), lambda i, j, k: (k, i)), _bs((tk, bn), b0_map), _bs((tk, bn), b1_map)],
        out_specs=_bs((bm, bn), lambda i, j, k: (i, j)),
        out_shape=jax.ShapeDtypeStruct((M, 2 * N), F32),
        compiler_params=_cparams(blk),
    )(a, b0, b1)


def _mix_dx(dz, parts, weights, alpha, name):
    S, D = dz.shape
    tm = _divisor_tile(S, 256, 16)
    n = len(parts)

    def body(*refs):
        dz_ref = refs[0]
        p_refs = refs[1:1 + n]
        w_refs = refs[1 + n:1 + 2 * n]
        o_ref = refs[1 + 2 * n]
        acc = alpha * dz_ref[...]
        for p_ref, w_ref in zip(p_refs, w_refs):
            acc = acc + _dot(p_ref[...], w_ref[...])
        o_ref[...] = acc

    widths = [p.shape[1] for p in parts]
    blk = 2 * (2 * tm * D * 4 + sum(tm * w * 2 for w in widths)) + sum(D * w * 2 for w in widths) + 2 * tm * D * 4
    return pl.pallas_call(
        body, name=name, grid=(S // tm,),
        in_specs=[_bs((tm, D), lambda i: (i, 0))] + [_bs((tm, w), lambda i: (i, 0)) for w in widths]
                 + [_resident((w, D)) for w in widths],
        out_specs=_bs((tm, D), lambda i: (i, 0)),
        out_shape=jax.ShapeDtypeStruct((S, D), F32),
        compiler_params=_cparams(blk),
    )(dz, *parts, *weights)


def _conv_branch_fwd(cv, cg, w, b, gg, gb, name):
    S, C = cv.shape
    K = w.shape[0]
    assert C // CONV_GROUPS == LANES
    padf = _front_pad(K)
    R = min(CONV_ROWS, S)
    E = min(EW_ROWS, S)

    def body(cv_ref, cg_ref, w_ref, b_ref, gg_ref, gb_ref, c2_ref, c4_ref, pad_ref):
        pad_ref[0:padf, :] = jnp.zeros((padf, LANES), F32)
        pad_ref[S + padf:S + padf + SUBLANES, :] = jnp.zeros((SUBLANES, LANES), F32)

        def fill(i, carry):
            r = pl.multiple_of(i * E, E)
            pad_ref[pl.ds(r + padf, E), :] = cv_ref[pl.ds(r, E), :] * _sigmoid(cg_ref[pl.ds(r, E), :])
            return carry

        lax.fori_loop(0, S // E, fill, 0)
        taps = [(padf - (K - 1) + k, functools.partial(lambda k: w_ref[k:k + 1, :], k)) for k in range(K)]

        def conv(i, carry):
            r = pl.multiple_of(i * R, R)
            c2_ref[pl.ds(r, R), :] = _shifted_taps(pad_ref, r, R, taps) + b_ref[...]
            return carry

        lax.fori_loop(0, S // R, conv, 0)

        def norm(i, carry):
            r = pl.multiple_of(i * E, E)
            c3 = _norm_fwd(c2_ref[pl.ds(r, E), :], gg_ref[...], gb_ref[...], _lane_mean)
            c4_ref[pl.ds(r, E), :] = (c3 * _sigmoid(c3)).astype(BF16)
            return carry

        lax.fori_loop(0, S // E, norm, 0)

    col = lambda i: (0, i)
    blk = 2 * (3 * S * LANES * 4 + S * LANES * 2) + (S + _pad_rows(K)) * LANES * 4
    return pl.pallas_call(
        body, name=name, grid=(C // LANES,),
        in_specs=[_bs((S, LANES), col), _bs((S, LANES), col), _bs((K, LANES), col),
                  _bs((1, LANES), col), _bs((1, LANES), col), _bs((1, LANES), col)],
        out_specs=[_bs((S, LANES), col), _bs((S, LANES), col)],
        out_shape=[jax.ShapeDtypeStruct((S, C), F32), jax.ShapeDtypeStruct((S, C), BF16)],
        scratch_shapes=[pltpu.VMEM((S + _pad_rows(K), LANES), F32)],
        compiler_params=_cparams(blk),
    )(cv, cg, w, b, gg, gb)


def _conv_branch_bwd(dc4, c2, cv, cg, w, gg, gb, name):
    S, C = cv.shape
    K = w.shape[0]
    padf = _front_pad(K)
    R = min(CONV_ROWS, S)
    E = min(EW_ROWS, S)

    def body(dc4_ref, c2_ref, cv_ref, cg_ref, w_ref, gg_ref, gb_ref,
             dcv_ref, dcg_ref, dw_ref, dwb_ref, dgg_ref, dgb_ref, scv_ref, scg_ref,
             dpad_ref, cpad_ref, dwacc_ref):
        cpad_ref[0:padf, :] = jnp.zeros((padf, LANES), F32)
        cpad_ref[S + padf:S + padf + SUBLANES, :] = jnp.zeros((SUBLANES, LANES), F32)
        dpad_ref[S:S + padf + SUBLANES, :] = jnp.zeros((padf + SUBLANES, LANES), F32)
        dwacc_ref[...] = jnp.zeros_like(dwacc_ref)
        for ref in (dwb_ref, dgg_ref, dgb_ref, scv_ref, scg_ref):
            ref[...] = jnp.zeros_like(ref)

        def norm_pass(i, carry):
            r = pl.multiple_of(i * E, E)
            g_ = gg_ref[...]
            c2 = c2_ref[pl.ds(r, E), :]
            xc = c2 - _lane_mean(c2)
            rstd = lax.rsqrt(_lane_mean(xc * xc) + LN_EPS)
            xhat = xc * rstd
            c3 = xhat * g_ + gb_ref[...]
            s = _sigmoid(c3)
            dc3 = dc4_ref[pl.ds(r, E), :] * (s * (1.0 + c3 * (1.0 - s)))
            dgg_ref[...] += _colsum(dc3 * xhat)
            dgb_ref[...] += _colsum(dc3)
            dxh = dc3 * g_
            dc2 = rstd * (dxh - _lane_mean(dxh) - xhat * _lane_mean(dxh * xhat))
            dpad_ref[pl.ds(r, E), :] = dc2
            dwb_ref[...] += _colsum(dc2)
            cpad_ref[pl.ds(r + padf, E), :] = cv_ref[pl.ds(r, E), :] * _sigmoid(cg_ref[pl.ds(r, E), :])
            return carry

        lax.fori_loop(0, S // E, norm_pass, 0)
        taps = [(K - 1 - k, functools.partial(lambda k: w_ref[k:k + 1, :], k)) for k in range(K)]
        offs = [padf - (K - 1) + k for k in range(K)]

        def conv_pass(i, carry):
            r = pl.multiple_of(i * R, R)
            dc1 = _shifted_taps(dpad_ref, r, R, taps)
            sg = _sigmoid(cg_ref[pl.ds(r, R), :])
            cv_ = cv_ref[pl.ds(r, R), :]
            dcv = dc1 * sg
            dcg = dc1 * cv_ * sg * (1.0 - sg)
            dcv_ref[pl.ds(r, R), :] = dcv.astype(BF16)
            dcg_ref[pl.ds(r, R), :] = dcg.astype(BF16)
            scv_ref[...] += _colsum(dcv)
            scg_ref[...] += _colsum(dcg)
            _shifted_corr(cpad_ref, r, R, dpad_ref[pl.ds(r, R), :], dwacc_ref, offs)
            return carry

        lax.fori_loop(0, S // R, conv_pass, 0)
        for k in range(K):
            dw_ref[k:k + 1, :] = _colsum(dwacc_ref[SUBLANES * k:SUBLANES * (k + 1), :])

    col = lambda i: (0, i)
    row = jax.ShapeDtypeStruct((1, C), F32)
    blk = 2 * (4 * S * LANES * 4 + 2 * S * LANES * 2) + 2 * (S + _pad_rows(K)) * LANES * 4
    return pl.pallas_call(
        body, name=name, grid=(C // LANES,),
        in_specs=[_bs((S, LANES), col)] * 4 + [_bs((K, LANES), col), _bs((1, LANES), col), _bs((1, LANES), col)],
        out_specs=[_bs((S, LANES), col), _bs((S, LANES), col), _bs((K, LANES), col)] + [_bs((1, LANES), col)] * 5,
        out_shape=[jax.ShapeDtypeStruct((S, C), BF16), jax.ShapeDtypeStruct((S, C), BF16),
                   jax.ShapeDtypeStruct((K, C), F32), row, row, row, row, row],
        scratch_shapes=[pltpu.VMEM((S + _pad_rows(K), LANES), F32), pltpu.VMEM((S + _pad_rows(K), LANES), F32),
                        pltpu.VMEM((SUBLANES * K, LANES), F32)],
        compiler_params=_cparams(blk),
    )(dc4, c2, cv, cg, w, gg, gb)


def _short_conv_fwd(xin, w, b, name):
    S, C = xin.shape
    K = w.shape[0]
    padf = _front_pad(K)
    R = min(CONV_ROWS, S)
    E = min(EW_ROWS, S)

    def body(x_ref, w_ref, b_ref, o_ref, pad_ref):
        pad_ref[0:padf, :] = jnp.zeros((padf, LANES), F32)
        pad_ref[S + padf:S + padf + SUBLANES, :] = jnp.zeros((SUBLANES, LANES), F32)

        def fill(i, carry):
            r = pl.multiple_of(i * E, E)
            pad_ref[pl.ds(r + padf, E), :] = x_ref[pl.ds(r, E), :]
            return carry

        lax.fori_loop(0, S // E, fill, 0)
        taps = [(padf - (K - 1) + k, functools.partial(lambda k: w_ref[k:k + 1, :], k)) for k in range(K)]

        def conv(i, carry):
            r = pl.multiple_of(i * R, R)
            o_ref[pl.ds(r, R), :] = _shifted_taps(pad_ref, r, R, taps) + b_ref[...]
            return carry

        lax.fori_loop(0, S // R, conv, 0)

    col = lambda i: (0, i)
    blk = 2 * (2 * S * LANES * 4) + (S + _pad_rows(K)) * LANES * 4
    return pl.pallas_call(
        body, name=name, grid=(C // LANES,),
        in_specs=[_bs((S, LANES), col), _bs((K, LANES), col), _bs((1, LANES), col)],
        out_specs=_bs((S, LANES), col),
        out_shape=jax.ShapeDtypeStruct((S, C), F32),
        scratch_shapes=[pltpu.VMEM((S + _pad_rows(K), LANES), F32)],
        compiler_params=_cparams(blk),
    )(xin, w, b)


def _short_conv_bwd(dy, xin, w, name):
    S, C = xin.shape
    K = w.shape[0]
    padf = _front_pad(K)
    R = min(CONV_ROWS, S)
    E = min(EW_ROWS, S)

    def body(dy_ref, x_ref, w_ref, dx_ref, dw_ref, db_ref, sx_ref, dpad_ref, xpad_ref, dwacc_ref):
        xpad_ref[0:padf, :] = jnp.zeros((padf, LANES), F32)
        xpad_ref[S + padf:S + padf + SUBLANES, :] = jnp.zeros((SUBLANES, LANES), F32)
        dpad_ref[S:S + padf + SUBLANES, :] = jnp.zeros((padf + SUBLANES, LANES), F32)
        dwacc_ref[...] = jnp.zeros_like(dwacc_ref)
        db_ref[...] = jnp.zeros_like(db_ref)
        sx_ref[...] = jnp.zeros_like(sx_ref)

        def fill(i, carry):
            r = pl.multiple_of(i * E, E)
            d = dy_ref[pl.ds(r, E), :]
            dpad_ref[pl.ds(r, E), :] = d
            db_ref[...] += _colsum(d)
            xpad_ref[pl.ds(r + padf, E), :] = x_ref[pl.ds(r, E), :]
            return carry

        lax.fori_loop(0, S // E, fill, 0)
        taps = [(K - 1 - k, functools.partial(lambda k: w_ref[k:k + 1, :], k)) for k in range(K)]
        offs = [padf - (K - 1) + k for k in range(K)]

        def conv_pass(i, carry):
            r = pl.multiple_of(i * R, R)
            dx = _shifted_taps(dpad_ref, r, R, taps)
            dx_ref[pl.ds(r, R), :] = dx.astype(BF16)
            sx_ref[...] += _colsum(dx)
            _shifted_corr(xpad_ref, r, R, dpad_ref[pl.ds(r, R), :], dwacc_ref, offs)
            return carry

        lax.fori_loop(0, S // R, conv_pass, 0)
        for k in range(K):
            dw_ref[k:k + 1, :] = _colsum(dwacc_ref[SUBLANES * k:SUBLANES * (k + 1), :])

    col = lambda i: (0, i)
    row = jax.ShapeDtypeStruct((1, C), F32)
    blk = 2 * (2 * S * LANES * 4 + S * LANES * 2) + 2 * (S + _pad_rows(K)) * LANES * 4
    return pl.pallas_call(
        body, name=name, grid=(C // LANES,),
        in_specs=[_bs((S, LANES), col), _bs((S, LANES), col), _bs((K, LANES), col)],
        out_specs=[_bs((S, LANES), col), _bs((K, LANES), col), _bs((1, LANES), col), _bs((1, LANES), col)],
        out_shape=[jax.ShapeDtypeStruct((S, C), BF16), jax.ShapeDtypeStruct((K, C), F32), row, row],
        scratch_shapes=[pltpu.VMEM((S + _pad_rows(K), LANES), F32), pltpu.VMEM((S + _pad_rows(K), LANES), F32),
                        pltpu.VMEM((SUBLANES * K, LANES), F32)],
        compiler_params=_cparams(blk),
    )(dy, xin, w)


def _band_panels(width, block):
    assert width % LANES == 0 and block <= LANES
    panels = []
    for c0 in range(0, width, 2 * LANES):
        c1 = min(width, c0 + 2 * LANES)
        r0 = (c0 // block) * block // LANES * LANES
        r1 = min(width, -(-(-(-c1 // block) * block) // LANES) * LANES)
        panels.append((r0, r1, c0, c1))
    return panels


def _gates_fwd(r1, wa, wx, ba, bx, lam, name):
    S, R = r1.shape
    tm = _divisor_tile(S, 256, 16)
    panels = _band_panels(R, R // RNN_BLOCKS)

    def body(r1_ref, wa_ref, wx_ref, ba_ref, bx_ref, lam_ref, ra_ref, ri_ref, a_ref, uu_ref):
        for r0, r1e, c0, c1 in panels:
            rb = r1_ref[:, r0:r1e].astype(BF16)
            ra = _sigmoid(_dot(rb, wa_ref[r0:r1e, c0:c1]) + ba_ref[:, c0:c1])
            ri = _sigmoid(_dot(rb, wx_ref[r0:r1e, c0:c1]) + bx_ref[:, c0:c1])
            log_a = -RG_LRU_C * ra * _softplus(-lam_ref[:, c0:c1])
            ra_ref[:, c0:c1] = ra
            ri_ref[:, c0:c1] = ri
            a_ref[:, c0:c1] = jnp.exp(log_a)
            uu_ref[:, c0:c1] = jnp.sqrt(_neg_expm1(2.0 * log_a)) * (ri * r1_ref[:, c0:c1])

    blk = 2 * (5 * tm * R * 4) + 2 * R * R * 2 + 6 * tm * R * 4
    tile = _bs((tm, R), lambda i: (i, 0))
    return pl.pallas_call(
        body, name=name, grid=(S // tm,),
        in_specs=[tile, _resident((R, R)), _resident((R, R)), _resident((1, R)), _resident((1, R)), _resident((1, R))],
        out_specs=[tile] * 4,
        out_shape=[jax.ShapeDtypeStruct((S, R), F32)] * 4,
        compiler_params=_cparams(blk),
    )(r1, wa, wx, ba, bx, lam)


def _gates_bwd(guu, da, ra, ri, r1, wa, wx, lam, name):
    S, R = r1.shape
    tm = _divisor_tile(S, 256, 16)
    nsteps = S // tm
    panels = _band_panels(R, R // RNN_BLOCKS)

    def body(g_ref, da_ref, ra_ref, ri_ref, r1_ref, wa_ref, wx_ref, lam_ref,
             dr1_ref, dpa_ref, dpx_ref, dba_ref, dbx_ref, dlam_ref):
        i = pl.program_id(0)

        @pl.when(i == 0)
        def _():
            dba_ref[...] = jnp.zeros_like(dba_ref)
            dbx_ref[...] = jnp.zeros_like(dbx_ref)
            dlam_ref[...] = jnp.zeros_like(dlam_ref)

        g = g_ref[...]
        ra = ra_ref[...]
        ri = ri_ref[...]
        r1_ = r1_ref[...]
        sp = _softplus(-lam_ref[...])
        log_a = -RG_LRU_C * ra * sp
        a = jnp.exp(log_a)
        mult = jnp.sqrt(_neg_expm1(2.0 * log_a))
        d_ri = g * mult * r1_
        dr1 = g * mult * ri
        dmult = g * ri * r1_
        dlog_a = da_ref[...] * a - dmult * (a * a) / mult
        dra = dlog_a * (-RG_LRU_C * sp)
        dlam_ref[...] += _colsum(dlog_a * (-RG_LRU_C * ra))
        dpa = dra * ra * (1.0 - ra)
        dpx = d_ri * ri * (1.0 - ri)
        dba_ref[...] += _colsum(dpa)
        dbx_ref[...] += _colsum(dpx)
        dpa_b = dpa.astype(BF16)
        dpx_b = dpx.astype(BF16)
        dpa_ref[...] = dpa_b
        dpx_ref[...] = dpx_b
        dr1_ref[...] = dr1
        for k0, k1, c0, c1 in panels:
            dr1_ref[:, c0:c1] += (_dot_nt(dpa_ref[:, k0:k1], wa_ref[c0:c1, k0:k1])
                                  + _dot_nt(dpx_ref[:, k0:k1], wx_ref[c0:c1, k0:k1]))

        @pl.when(i == nsteps - 1)
        def _():
            dlam_ref[...] = dlam_ref[...] * (-_sigmoid(-lam_ref[...]))

    blk = 2 * (6 * tm * R * 4 + 2 * tm * R * 2) + 2 * R * R * 2 + 10 * tm * R * 4
    tile = _bs((tm, R), lambda i: (i, 0))
    rowspec = _bs((1, R), lambda i: (0, 0))
    row = jax.ShapeDtypeStruct((1, R), F32)
    return pl.pallas_call(
        body, name=name, grid=(nsteps,),
        in_specs=[tile] * 5 + [_resident((R, R)), _resident((R, R)), _resident((1, R))],
        out_specs=[tile, tile, tile, rowspec, rowspec, rowspec],
        out_shape=[jax.ShapeDtypeStruct((S, R), F32), jax.ShapeDtypeStruct((S, R), BF16), jax.ShapeDtypeStruct((S, R), BF16),
                   row, row, row],
        compiler_params=_cparams(blk),
    )(guu, da, ra, ri, r1, wa, wx, lam)


def _embed_blocks(w, name):
    H, bk, _ = w.shape

    def body(w_ref, o_ref):
        o_ref[...] = jnp.zeros_like(o_ref)
        for h in range(H):
            o_ref[bk * h:bk * (h + 1), bk * h:bk * (h + 1)] = w_ref[h].astype(BF16)

    return pl.pallas_call(body, name=name, out_shape=jax.ShapeDtypeStruct((H * bk, H * bk), BF16),
                          compiler_params=_cparams(3 * H * bk * H * bk * 2))(w)


def _block_grads(r1, dpa, dpx, name):
    S, R = r1.shape
    bk = R // RNN_BLOCKS
    tk = _divisor_tile(S, 512, 16)
    nsteps = S // tk
    panels = _band_panels(R, bk)

    def body(r1_ref, dpa_ref, dpx_ref, ga_ref, gx_ref, acca_ref, accx_ref):
        k = pl.program_id(0)

        @pl.when(k == 0)
        def _():
            acca_ref[...] = jnp.zeros_like(acca_ref)
            accx_ref[...] = jnp.zeros_like(accx_ref)

        for k0, k1, c0, c1 in panels:
            rb = r1_ref[:, k0:k1].astype(BF16)
            acca_ref[k0:k1, c0:c1] += _dot_tn(rb, dpa_ref[:, c0:c1])
            accx_ref[k0:k1, c0:c1] += _dot_tn(rb, dpx_ref[:, c0:c1])

        @pl.when(k == nsteps - 1)
        def _():
            for h in range(RNN_BLOCKS):
                ga_ref[h] = acca_ref[bk * h:bk * (h + 1), bk * h:bk * (h + 1)]
                gx_ref[h] = accx_ref[bk * h:bk * (h + 1), bk * h:bk * (h + 1)]

    tile = lambda: _bs((tk, R), lambda k: (k, 0))
    out = _bs((RNN_BLOCKS, bk, bk), lambda k: (0, 0, 0))
    sds = jax.ShapeDtypeStruct((RNN_BLOCKS, bk, bk), F32)
    return pl.pallas_call(
        body, name=name, grid=(nsteps,),
        in_specs=[tile(), tile(), tile()], out_specs=[out, out], out_shape=[sds, sds],
        scratch_shapes=[pltpu.VMEM((R, R), F32), pltpu.VMEM((R, R), F32)],
        compiler_params=_cparams(2 * (tk * R * 8) + 2 * R * R * 4 + 4 * tk * R * 4),
    )(r1, dpa, dpx)


def _scan_geometry(S):
    nseg = SCAN_SEGMENTS if S % (SCAN_SEGMENTS * SUBLANES) == 0 else SUBLANES
    return nseg, S // nseg


def _steps(n, step, init):
    u = SCAN_UNROLL

    def trip(t, carry):
        for k in range(u):
            carry = step(t * u + k, carry)
        return carry

    carry = lax.fori_loop(0, n // u, trip, init)
    for j in range(n - n % u, n):
        carry = step(j, carry)
    return carry


def _scan_fwd(a, u, name):
    S, C = a.shape
    nseg, L = _scan_geometry(S)
    T = min(SUBLANES, L)

    def body(a3, u3, h3, ta_ref, tu_ref, e_ref, p_ref, init_ref):

        def to_steps(i, carry):
            j0 = pl.multiple_of(i * T, T)
            ta_ref[pl.ds(j0, T)] = jnp.swapaxes(a3[:, pl.ds(j0, T), :], 0, 1)
            tu_ref[pl.ds(j0, T)] = jnp.swapaxes(u3[:, pl.ds(j0, T), :], 0, 1)
            return carry

        lax.fori_loop(0, L // T, to_steps, 0)

        def run1(j, carry):
            hs, ps = carry
            aj = ta_ref[j]
            return aj * hs + tu_ref[j], aj * ps

        e_ref[...], p_ref[...] = _steps(L, run1, (jnp.zeros((nseg, LANES), F32), jnp.ones((nseg, LANES), F32)))
        init_ref[0:1, :] = jnp.zeros((1, LANES), F32)
        for s in range(1, nseg):
            init_ref[s:s + 1, :] = e_ref[s - 1:s, :] + p_ref[s - 1:s, :] * init_ref[s - 1:s, :]

        def run2(j, hs):
            hs = ta_ref[j] * hs + tu_ref[j]
            tu_ref[j] = hs
            return hs

        _steps(L, run2, init_ref[...])

        def from_steps(i, carry):
            j0 = pl.multiple_of(i * T, T)
            h3[:, pl.ds(j0, T), :] = jnp.swapaxes(tu_ref[pl.ds(j0, T)], 0, 1)
            return carry

        lax.fori_loop(0, L // T, from_steps, 0)

    seg_block = _bs((nseg, L, LANES), lambda i: (0, 0, i))
    blk = 2 * (3 * S * LANES * 4) + 2 * S * LANES * 4
    return pl.pallas_call(
        body, name=name, grid=(C // LANES,),
        in_specs=[seg_block, seg_block],
        out_specs=seg_block,
        out_shape=jax.ShapeDtypeStruct((nseg, L, C), F32),
        scratch_shapes=[pltpu.VMEM((L, nseg, LANES), F32)] * 2 + [pltpu.VMEM((nseg, LANES), F32)] * 3,
        compiler_params=_cparams(blk),
    )(a.reshape(nseg, L, C), u.reshape(nseg, L, C)).reshape(S, C)


def _scan_bwd(a, dh, h, name):
    S, C = a.shape
    nseg, L = _scan_geometry(S)
    T = min(SUBLANES, L)
    assert L >= 2

    def body(a3, d3, h3, g3, da3, ta_ref, td_ref, th_ref, e_ref, p_ref, init_ref):

        def to_steps(i, carry):
            j0 = pl.multiple_of(i * T, T)
            for src, dst in ((a3, ta_ref), (d3, td_ref), (h3, th_ref)):
                dst[pl.ds(j0, T)] = jnp.swapaxes(src[:, pl.ds(j0, T), :], 0, 1)
            return carry

        lax.fori_loop(0, L // T, to_steps, 0)
        seg = lax.broadcasted_iota(jnp.int32, (nseg, LANES), 0)
        b_last = jnp.where(seg == nseg - 1, 0.0, pltpu.roll(ta_ref[0], nseg - 1, axis=0))
        h_first = jnp.where(seg == 0, 0.0, pltpu.roll(th_ref[L - 1], 1, axis=0))

        def run1(jj, carry):
            gs, ps = carry
            j = L - 2 - jj
            bj = ta_ref[j + 1]
            return bj * gs + td_ref[j], bj * ps

        e_ref[...], p_ref[...] = _steps(L - 1, run1, (td_ref[L - 1], b_last))
        init_ref[nseg - 1:nseg, :] = jnp.zeros((1, LANES), F32)
        for s in range(nseg - 2, -1, -1):
            init_ref[s:s + 1, :] = e_ref[s + 1:s + 2, :] + p_ref[s + 1:s + 2, :] * init_ref[s + 1:s + 2, :]

        gs = b_last * init_ref[...] + td_ref[L - 1]
        td_ref[L - 1] = gs
        th_ref[L - 1] = gs * th_ref[L - 2]

        def run2(jj, gs):
            j = L - 2 - jj
            gs = ta_ref[j + 1] * gs + td_ref[j]
            td_ref[j] = gs
            th_ref[j] = gs * th_ref[j - 1]
            return gs

        gs = _steps(L - 2, run2, gs)
        gs = ta_ref[1] * gs + td_ref[0]
        td_ref[0] = gs
        th_ref[0] = gs * h_first

        def from_steps(i, carry):
            j0 = pl.multiple_of(i * T, T)
            g3[:, pl.ds(j0, T), :] = jnp.swapaxes(td_ref[pl.ds(j0, T)], 0, 1)
            da3[:, pl.ds(j0, T), :] = jnp.swapaxes(th_ref[pl.ds(j0, T)], 0, 1)
            return carry

        lax.fori_loop(0, L // T, from_steps, 0)

    seg_block = _bs((nseg, L, LANES), lambda i: (0, 0, i))
    blk = 2 * (5 * S * LANES * 4) + 3 * S * LANES * 4
    g, da = pl.pallas_call(
        body, name=name, grid=(C // LANES,),
        in_specs=[seg_block] * 3,
        out_specs=[seg_block] * 2,
        out_shape=[jax.ShapeDtypeStruct((nseg, L, C), F32)] * 2,
        scratch_shapes=[pltpu.VMEM((L, nseg, LANES), F32)] * 3 + [pltpu.VMEM((nseg, LANES), F32)] * 3,
        compiler_params=_cparams(blk),
    )(a.reshape(nseg, L, C), dh.reshape(nseg, L, C), h.reshape(nseg, L, C))
    return g.reshape(S, C), da.reshape(S, C)


def _mixer_out_fwd(c4, h, rg, gc, gr, x1, wc, wr, wo, g, b, alpha, name):
    S, D = x1.shape
    R = h.shape[1]
    tm = _divisor_tile(S, 256, 16)

    def body(c4_ref, h_ref, rg_ref, gc_ref, gr_ref, x_ref, wc_ref, wr_ref, wo_ref, g_ref, b_ref,
             yc_ref, yr_ref, z_ref, y_ref):
        yc = _dot(c4_ref[...], wc_ref[...])
        q = (h_ref[...] * _gelu(rg_ref[...])).astype(BF16)
        yr = _dot(q, wr_ref[...])
        yc_ref[...] = yc
        yr_ref[...] = yr
        m = (_sigmoid(gc_ref[...]) * yc + _sigmoid(gr_ref[...]) * yr).astype(BF16)
        z = alpha * x_ref[...] + _dot(m, wo_ref[...])
        z_ref[...] = z
        y_ref[...] = _norm_fwd(z, g_ref[...], b_ref[...])

    blk = 2 * (tm * D * 2 + 2 * tm * R * 4 + 7 * tm * D * 4) + (2 * D * D + R * D) * 2 + 6 * tm * D * 4
    td = _bs((tm, D), lambda i: (i, 0))
    tr = _bs((tm, R), lambda i: (i, 0))
    return pl.pallas_call(
        body, name=name, grid=(S // tm,),
        in_specs=[td, tr, tr, td, td, td, _layer_resident(wc)[1], _layer_resident(wr)[1], _layer_resident(wo)[1],
                  _resident((1, D)), _resident((1, D))],
        out_specs=[td] * 4,
        out_shape=[jax.ShapeDtypeStruct((S, D), F32)] * 4,
        compiler_params=_cparams(blk),
    )(c4, h, rg, gc, gr, x1, wc[0], wr[0], wo[0], g, b)


def _mixer_out_bwd(dy, z, g, wo, yc, yr, gc, gr, name):
    S, D = dy.shape
    tm = _divisor_tile(S, 256, 16)

    def body(dy_ref, z_ref, g_ref, wo_ref, yc_ref, yr_ref, gc_ref, gr_ref,
             dz_ref, dzb_ref, m_ref, dyc_ref, dyr_ref, dgc_ref, dgr_ref, sgc_ref, sgr_ref, dg_ref, db_ref):
        @pl.when(pl.program_id(0) == 0)
        def _():
            for ref in (sgc_ref, sgr_ref, dg_ref, db_ref):
                ref[...] = jnp.zeros_like(ref)

        dy_ = dy_ref[...]
        dz, xhat = _norm_bwd(z_ref[...], g_ref[...], dy_)
        dg_ref[...] += _colsum(dy_ * xhat)
        db_ref[...] += _colsum(dy_)
        dz_ref[...] = dz
        dzb = dz.astype(BF16)
        dzb_ref[...] = dzb
        dm = _dot_nt(dzb, wo_ref[...])
        yc = yc_ref[...]
        yr = yr_ref[...]
        sc = _sigmoid(gc_ref[...])
        sr = _sigmoid(gr_ref[...])
        m_ref[...] = (sc * yc + sr * yr).astype(BF16)
        dyc_ref[...] = (dm * sc).astype(BF16)
        dyr_ref[...] = (dm * sr).astype(BF16)
        dgc = dm * yc * sc * (1.0 - sc)
        dgr = dm * yr * sr * (1.0 - sr)
        dgc_ref[...] = dgc.astype(BF16)
        dgr_ref[...] = dgr.astype(BF16)
        sgc_ref[...] += _colsum(dgc)
        sgr_ref[...] += _colsum(dgr)

    blk = 2 * (7 * tm * D * 4 + 6 * tm * D * 2) + D * D * 2 + 8 * tm * D * 4
    td = _bs((tm, D), lambda i: (i, 0))
    rowspec = _bs((1, D), lambda i: (0, 0))
    row = jax.ShapeDtypeStruct((1, D), F32)
    bfd = jax.ShapeDtypeStruct((S, D), BF16)
    return pl.pallas_call(
        body, name=name, grid=(S // tm,),
        in_specs=[td, td, _resident((1, D)), _layer_resident(wo)[1], td, td, td, td],
        out_specs=[td] * 7 + [rowspec] * 4,
        out_shape=[jax.ShapeDtypeStruct((S, D), F32), bfd, bfd, bfd, bfd, bfd, bfd, row, row, row, row],
        compiler_params=_cparams(blk),
    )(dy, z, g, wo[0], yc, yr, gc, gr)


def _branch_bwd(dyc, dyr, wc, wr, h, rg, name):
    S, D = dyc.shape
    R = h.shape[1]
    tm = _divisor_tile(S, 256, 16)

    def body(dyc_ref, dyr_ref, wc_ref, wr_ref, h_ref, rg_ref, dc4_ref, dh_ref, drg_ref, q_ref, srg_ref):
        @pl.when(pl.program_id(0) == 0)
        def _():
            srg_ref[...] = jnp.zeros_like(srg_ref)

        dc4_ref[...] = _dot_nt(dyc_ref[...], wc_ref[...])
        dq = _dot_nt(dyr_ref[...], wr_ref[...])
        h_ = h_ref[...]
        rg_ = rg_ref[...]
        ge = _gelu(rg_)
        dh_ref[...] = dq * ge
        drg = dq * h_ * _gelu_grad(rg_)
        drg_ref[...] = drg.astype(BF16)
        srg_ref[...] += _colsum(drg)
        q_ref[...] = (h_ * ge).astype(BF16)

    blk = 2 * (2 * tm * D * 2 + tm * D * 4 + 3 * tm * R * 4 + 2 * tm * R * 2) + (D * D + R * D) * 2 + 6 * tm * R * 4
    td = _bs((tm, D), lambda i: (i, 0))
    tr = _bs((tm, R), lambda i: (i, 0))
    return pl.pallas_call(
        body, name=name, grid=(S // tm,),
        in_specs=[td, td, _layer_resident(wc)[1], _layer_resident(wr)[1], tr, tr],
        out_specs=[td, tr, tr, tr, _bs((1, R), lambda i: (0, 0))],
        out_shape=[jax.ShapeDtypeStruct((S, D), F32), jax.ShapeDtypeStruct((S, R), F32), jax.ShapeDtypeStruct((S, R), BF16),
                   jax.ShapeDtypeStruct((S, R), BF16), jax.ShapeDtypeStruct((1, R), F32)],
        compiler_params=_cparams(blk),
    )(dyc, dyr, wc[0], wr[0], h, rg)


def _loss_head(y, target, name):
    S, D = y.shape
    tm = _divisor_tile(S, 512, 16)
    nsteps = S // tm

    def body(y_ref, t_ref, loss_ref, dy_ref, acc_ref):
        i = pl.program_id(0)

        @pl.when(i == 0)
        def _():
            acc_ref[...] = jnp.zeros_like(acc_ref)

        err = y_ref[...] - t_ref[...]
        dy_ref[...] = err * (1.0 / D)
        acc_ref[...] += _colsum(err * err)

        @pl.when(i == nsteps - 1)
        def _():
            loss_ref[...] = jnp.sum(acc_ref[...], axis=-1, keepdims=True) * (0.5 / D)

    td = _bs((tm, D), lambda i: (i, 0))
    return pl.pallas_call(
        body, name=name, grid=(nsteps,),
        in_specs=[td, td],
        out_specs=[_bs((1, 1), lambda i: (0, 0)), td],
        out_shape=[jax.ShapeDtypeStruct((1, 1), F32), jax.ShapeDtypeStruct((S, D), F32)],
        scratch_shapes=[pltpu.VMEM((1, D), F32)],
        compiler_params=_cparams(2 * 3 * tm * D * 4),
    )(y, target)


def _adamw_math(w, g, m, v):
    m = ADAM_B1 * m + (1.0 - ADAM_B1) * g
    v = ADAM_B2 * v + (1.0 - ADAM_B2) * (g * g)
    m_hat = m / (1.0 - ADAM_B1 ** ADAM_STEP)
    v_hat = v / (1.0 - ADAM_B2 ** ADAM_STEP)
    delta = -ADAM_LR * (m_hat / (jnp.sqrt(v_hat) + ADAM_EPS) + ADAM_WD * w)
    return delta, m, v


def _adamw_sharded(w, m, v, own, sib, rem, name):
    _, r, c = w.shape
    tr = _divisor_tile(r, max(16, (1 << 20) // (4 * c) // 16 * 16), 16)

    def body(w_ref, m_ref, v_ref, own_ref, sib_ref, rem_ref, g_ref, d_ref, nm_ref, nv_ref):
        mine = pl.program_id(0) == lax.axis_index("c")
        g = jnp.where(mine, own_ref[...], sib_ref[...]).astype(F32)
        for j in range(N_CHIPS - 1):
            g = g + rem_ref[j].astype(F32)
        delta, nm, nv = _adamw_math(w_ref[...], g, m_ref[...], v_ref[...])
        g_ref[...] = g
        d_ref[...] = delta
        nm_ref[...] = nm
        nv_ref[...] = nv

    tile = _bs((None, tr, c), lambda l, i: (l, i, 0))
    flat = _bs((tr, c), lambda l, i: (i, 0))
    sds = jax.ShapeDtypeStruct(w.shape, F32)
    return pl.pallas_call(
        body, name=name, grid=(2, r // tr),
        in_specs=[tile, tile, tile, flat, flat, _bs((N_CHIPS - 1, None, tr, c), lambda l, i: (0, l, i, 0))],
        out_specs=[tile] * 4,
        out_shape=[sds] * 4,
        compiler_params=_cparams(2 * (7 * tr * c * 4 + (N_CHIPS + 1) * tr * c * 2) + 6 * tr * c * 4),
    )(w, m, v, own, sib, rem)


def _adamw_flat(w, m, v, g, name):
    rows = w.shape[0]
    tr = _divisor_tile(rows, 1024, SUBLANES)

    def body(w_ref, m_ref, v_ref, g_ref, d_ref, nm_ref, nv_ref):
        delta, nm, nv = _adamw_math(w_ref[...], g_ref[...], m_ref[...], v_ref[...])
        d_ref[...] = delta
        nm_ref[...] = nm
        nv_ref[...] = nv

    tile = _bs((tr, LANES), lambda i: (i, 0))
    sds = jax.ShapeDtypeStruct(w.shape, F32)
    return pl.pallas_call(
        body, name=name, grid=(rows // tr,),
        in_specs=[tile] * 4, out_specs=[tile] * 3, out_shape=[sds] * 3,
        compiler_params=_cparams(2 * 7 * tr * LANES * 4),
    )(w, m, v, g)


def _pair_sum_bf16(g0, g1, theirs, name):
    rows, c = g0.shape
    tr = _divisor_tile(rows, max(16, (1 << 20) // (4 * c) // 16 * 16), 16)

    def body(g0_ref, g1_ref, t_ref, o_ref):
        mine = jnp.where(lax.axis_index("c") == 0, g0_ref[...], g1_ref[...])
        o_ref[...] = (mine + t_ref[...]).astype(BF16)

    tile = _bs((tr, c), lambda i: (i, 0))
    return pl.pallas_call(
        body, name=name, grid=(rows // tr,),
        in_specs=[tile, tile, tile], out_specs=tile, out_shape=jax.ShapeDtypeStruct((rows, c), BF16),
        compiler_params=_cparams(2 * 4 * tr * c * 4),
    )(g0, g1, theirs)


ANY = pl.BlockSpec(memory_space=pl.ANY)


def _mesh_position():
    return lax.axis_index("x"), lax.axis_index("y"), lax.axis_index("c")


def _other_chips():
    x, y, c = _mesh_position()
    chips = [(1 - x, y), (x, 1 - y), (1 - x, 1 - y)]
    return 2 * x + y, (x, y, 1 - c), chips, [2 * cx + cy for cx, cy in chips]


def _chip_slab(ref, k, width, by_cols):
    if by_cols:
        start = k * width if isinstance(k, int) else pl.multiple_of(k * width, LANES)
        return ref.at[:, pl.ds(start, width)]
    return ref.at[k]


HBM = pl.BlockSpec(memory_space=pltpu.HBM)
SEM = pl.BlockSpec(memory_space=pltpu.SEMAPHORE)
DATAFLOW = pltpu.SideEffectType.DATAFLOW_SIDE_EFFECTING
N_GATHER_COPIES = 4


def _land_shape(src, by_cols):
    return src.shape[:-1] + (N_CHIPS * src.shape[-1],) if by_cols else (N_CHIPS,) + src.shape


def _gather_copy(src_ref, land_ref, by_cols, send_sems, recv_sems, pos, j, slab, to):
    width = src_ref.shape[-1]
    return pltpu.make_async_remote_copy(src_ref=src_ref, dst_ref=_chip_slab(land_ref, slab, width, by_cols),
                                        send_sem=send_sems.at[N_GATHER_COPIES * pos + j],
                                        recv_sem=recv_sems.at[N_GATHER_COPIES * pos + j],
                                        device_id=to, device_id_type=MESH)


def _gather_start(srcs, by_cols, groups, name):
    U = len(srcs)
    G = len(groups)
    lands = [lax.empty(_land_shape(s, bc), s.dtype) for s, bc in zip(srcs, by_cols)]

    def body(*refs):
        src = refs[:U]
        land = refs[U:2 * U]
        send_sems = refs[2 * U:2 * U + G]
        recv_sems = refs[2 * U + G:2 * U + 2 * G]
        token = refs[-1]
        c = lax.axis_index("c")
        me, sibling, chips, _ = _other_chips()
        targets = [(*chip, c) for chip in chips] + [sibling]
        for g, members in enumerate(groups):
            for pos, u in enumerate(members):
                for j, to in enumerate(targets):
                    _gather_copy(src[u], land[u], by_cols[u], send_sems[g], recv_sems[g], pos, j, me, to).start()
        token[...] = jnp.zeros_like(token)

    sem_shapes = [pltpu.SemaphoreType.DMA((len(m) * N_GATHER_COPIES,)) for m in groups]
    outs = pl.pallas_call(
        body, name=name,
        out_shape=tuple(sem_shapes + sem_shapes + [pltpu.HBM(s.shape, s.dtype) for s in srcs]
                        + [pltpu.HBM(v.shape, v.dtype) for v in lands] + [jax.ShapeDtypeStruct((SUBLANES, LANES), F32)]),
        in_specs=[HBM] * (2 * U),
        out_specs=tuple([SEM] * (2 * G) + [HBM] * (2 * U) + [pl.BlockSpec(memory_space=pltpu.VMEM)]),
        input_output_aliases={i: 2 * G + i for i in range(2 * U)},
        compiler_params=pltpu.CompilerParams(has_side_effects=DATAFLOW),
    )(*[pltpu.with_memory_space_constraint(a, pltpu.HBM) for a in list(srcs) + lands])
    return outs[:G], outs[G:2 * G], outs[2 * G:2 * G + U], outs[2 * G + U:2 * G + 2 * U]


def _gather_wait(srcs, lands, by_cols, send_sems, recv_sems, after, name):
    n = len(srcs)

    def body(*refs):
        src = refs[:n]
        land = refs[n:2 * n]
        send_ref, recv_ref = refs[2 * n:2 * n + 2]
        _, sibling, _, _ = _other_chips()
        for pos in range(n):
            for j in range(N_GATHER_COPIES):
                cp = _gather_copy(src[pos], land[pos], by_cols[pos], send_ref, recv_ref, pos, j, 0, sibling)
                cp.wait_send()
                cp.wait_recv()

    outs = pl.pallas_call(
        body, name=name,
        out_shape=tuple([pltpu.HBM(s.shape, s.dtype) for s in srcs] + [pltpu.HBM(v.shape, v.dtype) for v in lands]),
        in_specs=[HBM] * (2 * n) + [SEM, SEM, pl.BlockSpec(memory_space=pl.ANY)],
        out_specs=tuple([HBM] * (2 * n)),
        input_output_aliases={i: i for i in range(2 * n)},
        compiler_params=pltpu.CompilerParams(has_side_effects=DATAFLOW),
    )(*srcs, *lands, send_sems, recv_sems, after)
    return outs[n:]


def _scatter_grads(csums, by_cols, name):
    n = len(csums)
    shard = [(s.shape[0], s.shape[1] // N_CHIPS) if bc else s.shape[1:] for s, bc in zip(csums, by_cols)]

    def body(*refs):
        src = refs[:n]
        rem = refs[n:2 * n]
        sib = refs[2 * n:3 * n]
        send_sems, recv_sems = refs[3 * n:]
        c = lax.axis_index("c")
        me, sibling, chips, chip_ids = _other_chips()

        def remote(i, k, src_ref, dst_ref, to):
            return pltpu.make_async_remote_copy(src_ref=src_ref, dst_ref=dst_ref, send_sem=send_sems.at[i, k],
                                                recv_sem=recv_sems.at[i, k], device_id=to, device_id_type=MESH)

        def part(i, k):
            return _chip_slab(src[i], k, shard[i][-1], by_cols[i])

        started = []
        for i in range(n):
            for j in range(3):
                started.append(remote(i, j, part(i, chip_ids[j]), rem[i].at[j, c], (*chips[j], c)))
            started.append(remote(i, 6, part(i, me), sib[i], sibling))
        for cp in started:
            cp.start()
        for i in range(n):
            for j in range(3):
                slot = rem[i].at[j, c]
                remote(i, j, slot, slot, sibling).wait_recv()
                fwd = remote(i, 3 + j, slot, slot, sibling)
                fwd.start()
                started.append(fwd)
        for i in range(n):
            for j in range(3):
                slot = rem[i].at[j, 1 - c]
                remote(i, 3 + j, slot, slot, sibling).wait_recv()
            remote(i, 6, sib[i], sib[i], sibling).wait_recv()
        for cp in started:
            cp.wait_send()

    out_shape = ([jax.ShapeDtypeStruct((N_CHIPS - 1, 2) + tuple(sh), s.dtype) for s, sh in zip(csums, shard)]
                 + [jax.ShapeDtypeStruct(tuple(sh), s.dtype) for s, sh in zip(csums, shard)])
    outs = pl.pallas_call(
        body, name=name,
        in_specs=[ANY] * n, out_specs=[ANY] * (2 * n), out_shape=out_shape,
        scratch_shapes=[pltpu.SemaphoreType.DMA((n, 7)), pltpu.SemaphoreType.DMA((n, 7))],
    )(*csums)
    return outs[:n], outs[n:]


def _sibling_exchange(g0, g1, name):
    n = len(g0)

    def body(*refs):
        layers = (refs[:n], refs[n:2 * n])
        theirs = refs[2 * n:3 * n]
        send_sems, recv_sems = refs[3 * n:]
        x, y, c = _mesh_position()

        def remote(i, src_ref):
            return pltpu.make_async_remote_copy(src_ref=src_ref, dst_ref=theirs[i], send_sem=send_sems.at[i],
                                                recv_sem=recv_sems.at[i], device_id=(x, y, 1 - c), device_id_type=MESH)

        for keep in range(2):
            @pl.when(c == keep)
            def _():
                for i in range(n):
                    remote(i, layers[1 - keep][i]).start()

        for i in range(n):
            remote(i, layers[0][i]).wait()

    return pl.pallas_call(
        body, name=name,
        in_specs=[ANY] * (2 * n), out_specs=[ANY] * n, out_shape=[jax.ShapeDtypeStruct(g.shape, g.dtype) for g in g0],
        scratch_shapes=[pltpu.SemaphoreType.DMA((n,)), pltpu.SemaphoreType.DMA((n,))],
    )(*g0, *g1)


def _all_reduce_small(v, name):
    _, rows, _ = v.shape

    def body(v_ref, o_ref, recv_ref, send_sems, recv_sems):
        x, y, c = _mesh_position()
        me = 4 * x + 2 * y + c
        peers = []
        for d in range(1, N_DEV):
            px, py, pc = x ^ ((d >> 2) & 1), y ^ ((d >> 1) & 1), c ^ (d & 1)
            peers.append(((px, py, pc), 4 * px + 2 * py + pc))

        def remote(k, src_ref, dst_ref, to):
            return pltpu.make_async_remote_copy(src_ref=src_ref, dst_ref=dst_ref, send_sem=send_sems.at[k],
                                                recv_sem=recv_sems.at[k], device_id=to, device_id_type=MESH)

        scatter = [remote(d, v_ref.at[pid], recv_ref.at[me], to) for d, (to, pid) in enumerate(peers)]
        for cp in scatter:
            cp.start()
        recv_ref[pl.ds(me, 1)] = v_ref[pl.ds(me, 1)]
        for d, (to, pid) in enumerate(peers):
            remote(d, v_ref.at[pid], recv_ref.at[pid], to).wait_recv()
        total = recv_ref[0]
        for s in range(1, N_DEV):
            total = total + recv_ref[s]
        o_ref[pl.ds(me, 1)] = total[None]
        gather = [remote(N_DEV - 1 + d, o_ref.at[me], o_ref.at[me], to) for d, (to, pid) in enumerate(peers)]
        for cp in gather:
            cp.start()
        for d, (to, pid) in enumerate(peers):
            remote(N_DEV - 1 + d, o_ref.at[pid], o_ref.at[pid], to).wait_recv()
        for cp in scatter + gather:
            cp.wait_send()

    vm = pl.BlockSpec(memory_space=pltpu.VMEM)
    return pl.pallas_call(
        body, name=name,
        in_specs=[vm], out_specs=vm, out_shape=jax.ShapeDtypeStruct(v.shape, F32),
        scratch_shapes=[pltpu.VMEM(v.shape, F32), pltpu.SemaphoreType.DMA((2 * (N_DEV - 1),)),
                        pltpu.SemaphoreType.DMA((2 * (N_DEV - 1),))],
        compiler_params=_cparams(4 * _nbytes(v.shape, F32)),
    )(v)


SHARDED_MATS = ("ffn1_w_gu", "ffn1_w_down", "mix_w_in", "conv_w_proj", "rnn_w_proj", "mix_w_out", "ffn2_w_gu", "ffn2_w_down")
COL_SHARDED = ("ffn1_w_gu", "ffn2_w_gu", "conv_dw_w")
SHARDED_VECS = ("conv_dw_w", "rnn_conv_w")
WEIGHT_NAMES = ("ffn1_w_gu", "ffn1_w_down", "ln1_g", "ln1_b", "mix_w_in", "mix_b_in", "conv_dw_w", "conv_dw_b", "conv_gn_g",
                "conv_gn_b", "conv_w_proj", "rnn_conv_w", "rnn_conv_b", "rnn_w_a", "rnn_b_a", "rnn_w_x", "rnn_b_x",
                "rnn_lambda", "rnn_w_proj", "mix_w_out", "ln2_g", "ln2_b", "ffn2_w_gu", "ffn2_w_down", "ln3_g", "ln3_b")
SMALL_NAMES = tuple(n for n in WEIGHT_NAMES if n not in SHARDED_MATS)


def _unshard_cols(gathered):
    k4, K, n = gathered.shape
    return jnp.transpose(gathered, (1, 0, 2)).reshape(K, k4 * n)


def _row(v):
    return v.reshape(1, -1)


def _layer_forward(x0, p, alpha, l, hooks):
    t = f"l{l}_"
    sv = {"x0": x0}
    x1, sv["z1"], sv["hg1"], sv["hu1"] = _ffn_fwd(x0, p["wgu1"], p["wd1"], p["ln1_g"], p["ln1_b"], alpha, t + "ffn1_fwd")
    sv["x1"] = x1
    hooks.get("after_ffn1", lambda v: None)(x1)
    sec = {}
    for s in ("cv", "cg", "rx", "rg", "gc", "gr"):
        sec[s] = _mm_bias(x1, p["win_" + s], p["bin_" + s], t + "win_" + s)
    sv.update(sec)
    sv["c2"], c4 = _conv_branch_fwd(sec["cv"], sec["cg"], p["conv_dw_w"], p["conv_dw_b"], p["conv_gn_g"], p["conv_gn_b"], t + "conv_fwd")
    sv["c4"] = c4
    r1 = _short_conv_fwd(sec["rx"], p["rnn_conv_w"], p["rnn_conv_b"], t + "rconv_fwd")
    sv["r1"] = r1
    sv["ra"], sv["ri"], a, uu = _gates_fwd(r1, p["wa"], p["wx"], p["rnn_b_a"], p["rnn_b_x"], p["rnn_lambda"], t + "gates_fwd")
    sv["a"] = a
    h = _scan_fwd(a, uu, t + "scan_fwd")
    sv["h"] = h
    hooks.get("after_scan", lambda v: None)(h)
    sv["yc"], sv["yr"], sv["z2"], x2 = _mixer_out_fwd(c4, h, sec["rg"], sec["gc"], sec["gr"], x1, p["wc"], p["wr"], p["wo"],
                                                      p["ln2_g"], p["ln2_b"], alpha, t + "mixout_fwd")
    sv["x2"] = x2
    hooks.get("after_mixer", lambda v: None)(x2)
    x3, sv["z3"], sv["hg2"], sv["hu2"] = _ffn_fwd(x2, p["wgu2"], p["wd2"], p["ln3_g"], p["ln3_b"], alpha, t + "ffn2_fwd")
    hooks.get("after_layer", lambda v: None)(x3)
    return x3, sv


def _layer_backward(dy, p, sv, alpha, l):
    t = f"l{l}_"
    g = {}
    dx2, df, a_act, dhg, dhu, g["ln3_g"], g["ln3_b"] = _ffn_bwd(dy, sv["z3"], sv["hg2"], sv["hu2"], p["wgu2"], p["wd2"],
                                                                 p["ln3_g"], alpha, t + "ffn2_bwd")
    g["ffn2_w_down"] = _mm_tn(a_act, df, t + "dwd2")
    g["ffn2_w_gu"] = _mm_tn_pair(sv["x2"], dhg, dhu, t + "dwgu2")
    (dz2, dz2b, m_b, dyc, dyr, dgc, dgr, s_gc, s_gr, g["ln2_g"], g["ln2_b"]) = _mixer_out_bwd(
        dx2, sv["z2"], p["ln2_g"], p["wo"], sv["yc"], sv["yr"], sv["gc"], sv["gr"], t + "mixout_bwd")
    g["mix_w_out"] = _mm_tn(m_b, dz2b, t + "dwo")
    dc4, dh, drg, q_b, s_rg = _branch_bwd(dyc, dyr, p["wc"], p["wr"], sv["h"], sv["rg"], t + "branch_bwd")
    g["conv_w_proj"] = _mm_tn(sv["c4"], dyc, t + "dwc")
    g["rnn_w_proj"] = _mm_tn(q_b, dyr, t + "dwr")
    (dcv, dcg, g["conv_dw_w"], g["conv_dw_b"], g["conv_gn_g"], g["conv_gn_b"], s_cv, s_cg) = _conv_branch_bwd(
        dc4, sv["c2"], sv["cv"], sv["cg"], p["conv_dw_w"], p["conv_gn_g"], p["conv_gn_b"], t + "conv_bwd")
    guu, da = _scan_bwd(sv["a"], dh, sv["h"], t + "scan_bwd")
    dr1, dpa, dpx, g["rnn_b_a"], g["rnn_b_x"], g["rnn_lambda"] = _gates_bwd(
        guu, da, sv["ra"], sv["ri"], sv["r1"], p["wa"], p["wx"], p["rnn_lambda"], t + "gates_bwd")
    g["rnn_w_a"], g["rnn_w_x"] = _block_grads(sv["r1"], dpa, dpx, t + "dwax")
    drx, g["rnn_conv_w"], g["rnn_conv_b"], s_rx = _short_conv_bwd(dr1, sv["rx"], p["rnn_conv_w"], t + "rconv_bwd")
    du = {"cv": dcv, "cg": dcg, "rx": drx, "rg": drg, "gc": dgc, "gr": dgr}
    order = ("cv", "cg", "rx", "rg", "gc", "gr")
    g["mix_w_in"] = jnp.concatenate([_mm_tn(du[s], sv["x1"], t + "dwin_" + s) for s in order], axis=0)
    g["mix_b_in"] = jnp.concatenate([s_cv, s_cg, s_rx, s_rg, s_gc, s_gr], axis=1)
    dx1 = _mix_dx(dz2, [du[s] for s in order], [p["win_" + s] for s in order], alpha, t + "mix_dx")
    dx0, df, a_act, dhg, dhu, g["ln1_g"], g["ln1_b"] = _ffn_bwd(dx1, sv["z1"], sv["hg1"], sv["hu1"], p["wgu1"], p["wd1"],
                                                                 p["ln1_g"], alpha, t + "ffn1_bwd")
    g["ffn1_w_down"] = _mm_tn(a_act, df, t + "dwd1")
    g["ffn1_w_gu"] = _mm_tn_pair(sv["x0"], dhg, dhu, t + "dwgu1")
    return dx0, g


def _pack_small(arrays, piece_rows):
    flat = jnp.concatenate([a.reshape(-1) for a in arrays])
    total = N_DEV * piece_rows * LANES
    return jnp.pad(flat, (0, total - flat.shape[0])).reshape(N_DEV, piece_rows, LANES)


def _unpack_small(packed, shapes):
    flat = packed.reshape(-1)
    out, off = [], 0
    for shp in shapes:
        n = 1
        for s in shp:
            n *= s
        out.append(flat[off:off + n].reshape(shp))
        off += n
    return out


def kernel(x, ffn1_w_gu, ffn1_w_down, ln1_g, ln1_b, mix_w_in, mix_b_in, conv_dw_w, conv_dw_b, conv_gn_g, conv_gn_b, conv_w_proj, rnn_conv_w, rnn_conv_b, rnn_w_a, rnn_b_a, rnn_w_x, rnn_b_x, rnn_lambda, rnn_w_proj, mix_w_out, ln2_g, ln2_b, ffn2_w_gu, ffn2_w_down, ln3_g, ln3_b, loss_target, m_ffn1_w_gu, m_ffn1_w_down, m_ln1_g, m_ln1_b, m_mix_w_in, m_mix_b_in, m_conv_dw_w, m_conv_dw_b, m_conv_gn_g, m_conv_gn_b, m_conv_w_proj, m_rnn_conv_w, m_rnn_conv_b, m_rnn_w_a, m_rnn_b_a, m_rnn_w_x, m_rnn_b_x, m_rnn_lambda, m_rnn_w_proj, m_mix_w_out, m_ln2_g, m_ln2_b, m_ffn2_w_gu, m_ffn2_w_down, m_ln3_g, m_ln3_b, v_ffn1_w_gu, v_ffn1_w_down, v_ln1_g, v_ln1_b, v_mix_w_in, v_mix_b_in, v_conv_dw_w, v_conv_dw_b, v_conv_gn_g, v_conv_gn_b, v_conv_w_proj, v_rnn_conv_w, v_rnn_conv_b, v_rnn_w_a, v_rnn_b_a, v_rnn_w_x, v_rnn_b_x, v_rnn_lambda, v_rnn_w_proj, v_mix_w_out, v_ln2_g, v_ln2_b, v_ffn2_w_gu, v_ffn2_w_down, v_ln3_g, v_ln3_b):
    args = locals()
    W = {n: args[n] for n in WEIGHT_NAMES}
    M = {n: args["m_" + n] for n in WEIGHT_NAMES}
    V = {n: args["v_" + n] for n in WEIGHT_NAMES}
    depth = ln1_g.shape[0]
    assert depth == 2, "each core of a chip moves one layer's weights and gradients"
    alpha = float((2 * depth) ** 0.25)
    S, D = x.shape[1], x.shape[2]
    F = ffn1_w_down.shape[1] * N_CHIPS
    R = rnn_w_proj.shape[1] * N_CHIPS
    chip = 2 * lax.axis_index("x") + lax.axis_index("y")

    for d in (W, M, V):
        d["mix_w_in"] = jnp.transpose(d["mix_w_in"], (0, 2, 1))

    names = SHARDED_MATS + SHARDED_VECS
    unit_groups = [[(0, "ffn1_w_gu"), (0, "ffn1_w_down")], [(0, "mix_w_in"), (0, "conv_dw_w"), (0, "rnn_conv_w")],
                   [(0, "conv_w_proj"), (0, "rnn_w_proj"), (0, "mix_w_out")], [(0, "ffn2_w_gu"), (0, "ffn2_w_down")],
                   [(1, n) for n in names]]
    order = [u for g in unit_groups for u in g]
    index = {u: i for i, u in enumerate(order)}
    groups = [[index[u] for u in g] for g in unit_groups]
    srcs = [W[n][l].astype(BF16) if n in SHARDED_MATS else W[n][l] for l, n in order]
    by_cols = [n in COL_SHARDED for _, n in order]
    send_sems, recv_sems, src_thru, land_thru = _gather_start(srcs, by_cols, groups, "gather_start")

    sections = (("cv", 0, D), ("cg", D, D), ("rx", 2 * D, R), ("rg", 2 * D + R, R), ("gc", 2 * D + 2 * R, D),
                ("gr", 3 * D + 2 * R, D))
    keys = {"ffn1_w_gu": "wgu1", "ffn1_w_down": "wd1", "ffn2_w_gu": "wgu2", "ffn2_w_down": "wd2", "conv_w_proj": "wc",
            "rnn_w_proj": "wr", "mix_w_out": "wo"}
    params = []
    for l in range(depth):
        p = {"wa": _embed_blocks(rnn_w_a[l], f"l{l}_embed_wa"), "wx": _embed_blocks(rnn_w_x[l], f"l{l}_embed_wx")}
        for n in ("ln1_g", "ln1_b", "ln2_g", "ln2_b", "ln3_g", "ln3_b", "conv_dw_b", "conv_gn_g", "conv_gn_b", "rnn_conv_b",
                  "rnn_b_a", "rnn_b_x", "rnn_lambda"):
            p[n] = _row(W[n][l])
        for s, off, width in sections:
            p["bin_" + s] = _row(mix_b_in[l, off:off + width])
        params.append(p)

    def wait_group(g, after):
        ids = groups[g]
        landed = _gather_wait([src_thru[i] for i in ids], [land_thru[i] for i in ids], [by_cols[i] for i in ids],
                              send_sems[g], recv_sems[g], after, f"gather_wait{g}")
        for i, full in zip(ids, landed):
            l, n = order[i]
            p = params[l]
            if n not in COL_SHARDED:
                full = full.reshape((N_CHIPS * full.shape[1],) + full.shape[2:])
            if n == "mix_w_in":
                for s, off, width in sections:
                    p["win_" + s] = full[off:off + width]
            elif n == "rnn_conv_w":
                p[n] = _unshard_cols(landed[ids.index(i)])
            elif n == "conv_dw_w":
                p[n] = full
            else:
                p[keys[n]] = (full[None], 0)

    h = x[0]
    wait_group(0, h)
    saved = []
    hooks = [{"after_ffn1": lambda v: wait_group(1, v), "after_scan": lambda v: wait_group(2, v),
              "after_mixer": lambda v: wait_group(3, v), "after_layer": lambda v: wait_group(4, v)}, {}]
    for l in range(depth):
        h, sv = _layer_forward(h, params[l], alpha, l, hooks[l])
        saved.append(sv)
    loss_part, dy = _loss_head(h, loss_target[0], "loss_head")
    loss = lax.psum(loss_part[0, 0], ("x", "y", "c"))
    grads = [None] * depth
    for l in reversed(range(depth)):
        dy, grads[l] = _layer_backward(dy, params[l], saved[l], alpha, l)
    grad_x = dy[None]

    g0 = [grads[0][n] for n in SHARDED_MATS]
    g1 = [grads[1][n] for n in SHARDED_MATS]
    theirs = _sibling_exchange(g0, g1, "pair_exchange")
    mat_cols = [n in COL_SHARDED for n in SHARDED_MATS]
    chip_sums = []
    for n, bc, a0, a1, b in zip(SHARDED_MATS, mat_cols, g0, g1, theirs):
        cs = _pair_sum_bf16(a0, a1, b, "pair_sum_" + n)
        chip_sums.append(cs if bc else cs.reshape((N_CHIPS, cs.shape[0] // N_CHIPS, cs.shape[1])))
    rem, sib = _scatter_grads(chip_sums, mat_cols, "scatter_grads")
    out_g, out_d, out_m, out_v = {}, {}, {}, {}
    for n, bc, cs, rm, sb in zip(SHARDED_MATS, mat_cols, chip_sums, rem, sib):
        if bc:
            width = cs.shape[1] // N_CHIPS
            own = lax.dynamic_slice_in_dim(cs, chip * width, width, axis=1)
        else:
            own = lax.dynamic_index_in_dim(cs, chip, axis=0, keepdims=False)
        out_g[n], out_d[n], out_m[n], out_v[n] = _adamw_sharded(W[n], M[n], V[n], own, sb, rm, "adamw_" + n)
    for d in (out_g, out_d, out_m, out_v):
        d["mix_w_in"] = jnp.transpose(d["mix_w_in"], (0, 2, 1))

    small_grads = [jnp.stack([grads[l][n].reshape(W[n].shape[1:] if n not in SHARDED_VECS else
                                                   (W[n].shape[1], W[n].shape[2] * N_CHIPS)) for l in range(depth)])
                   for n in SMALL_NAMES]
    n_small = sum(int(a.size) for a in small_grads)
    piece_rows = -(-n_small // (N_DEV * LANES * SUBLANES)) * SUBLANES
    reduced = _unpack_small(_all_reduce_small(_pack_small(small_grads, piece_rows), "all_reduce_small"),
                            [a.shape for a in small_grads])
    local_g = []
    for n, gr in zip(SMALL_NAMES, reduced):
        if n in SHARDED_VECS:
            width = W[n].shape[2]
            gr = lax.dynamic_slice_in_dim(gr, chip * width, width, axis=2)
        local_g.append(gr)
    n_local = sum(int(a.size) for a in local_g)
    flat_rows = -(-n_local // (N_DEV * LANES * SUBLANES)) * SUBLANES * N_DEV
    pack = lambda arrs: _pack_small(arrs, flat_rows // N_DEV).reshape(flat_rows, LANES)
    shapes = [a.shape for a in local_g]
    deltas, new_m, new_v = _adamw_flat(pack([W[n] for n in SMALL_NAMES]), pack([M[n] for n in SMALL_NAMES]),
                                       pack([V[n] for n in SMALL_NAMES]), pack(local_g), "adamw_small")
    for n, gr, d_, m_, v_ in zip(SMALL_NAMES, local_g, _unpack_small(deltas, shapes), _unpack_small(new_m, shapes),
                                 _unpack_small(new_v, shapes)):
        out_g[n], out_d[n], out_m[n], out_v[n] = gr, d_, m_, v_

    return (loss, grad_x, *[out_g[n] for n in WEIGHT_NAMES], *[out_d[n] for n in WEIGHT_NAMES],
            *[out_m[n] for n in WEIGHT_NAMES], *[out_v[n] for n in WEIGHT_NAMES])
```

```python
import functools

import jax
import jax.numpy as jnp
from jax import lax
from jax.experimental import pallas as pl
from jax.experimental.pallas import tpu as pltpu

F32 = jnp.float32
BF16 = jnp.bfloat16
MESH = pl.DeviceIdType.MESH

LN_EPS = 1e-5
CONV_GROUPS = 8
RNN_BLOCKS = 16
RG_LRU_C = 8.0
ADAM_LR = 0.001
ADAM_B1 = 0.9
ADAM_B2 = 0.999
ADAM_EPS = 1e-08
ADAM_WD = 0.01
ADAM_STEP = 10

LANES = 128
SUBLANES = 8
V7X_VMEM_BYTES = 64 << 20
VMEM_LIMIT_CAP = V7X_VMEM_BYTES - (6 << 20)
N_CHIPS = 4
N_DEV = 8
CONV_ROWS = 64
EW_ROWS = 1024
SCAN_SEGMENTS = 32
SCAN_UNROLL = 4


def _cparams(block_bytes):
    limit = min(VMEM_LIMIT_CAP, max(int(block_bytes) + (8 << 20), 24 << 20))
    return pltpu.CompilerParams(vmem_limit_bytes=limit)


def _nbytes(shape, dtype):
    n = 1
    for s in shape:
        n *= s
    return n * jnp.dtype(dtype).itemsize


def _divisor_tile(n, limit, quantum):
    if n <= limit:
        return n
    best = None
    for t in range(quantum, limit + 1, quantum):
        if n % t == 0:
            best = t
    assert best is not None, (n, limit, quantum)
    return best


def _bs(shape, imap, **kw):
    return pl.BlockSpec(shape, imap, **kw)


def _resident(shape):
    nd = len(shape)
    return pl.BlockSpec(shape, lambda *_: (0,) * nd, pipeline_mode=pl.Buffered(1))


def _layer_block(w, block, imap, **kw):
    arr, layer = w
    return arr, pl.BlockSpec((None,) + block, lambda *ids: (layer,) + imap(*ids), **kw)


def _layer_resident(w):
    arr, _ = w
    return _layer_block(w, arr.shape[1:], lambda *_: (0, 0), pipeline_mode=pl.Buffered(1))


def _sigmoid(x):
    return jax.nn.sigmoid(x)


def _dot(a, b):
    return jnp.dot(a, b, preferred_element_type=F32)


def _dot_nt(a, b):
    return lax.dot_general(a, b, (((1,), (1,)), ((), ())), preferred_element_type=F32)


def _dot_tn(a, b):
    return lax.dot_general(a, b, (((0,), (0,)), ((), ())), preferred_element_type=F32)


def _row_mean(z):
    return jnp.mean(z, axis=-1, keepdims=True)


def _lane_mean(z):
    hi = z.astype(BF16)
    lo = (z - hi.astype(F32)).astype(BF16)
    ones = jnp.full((2 * LANES, LANES), 1.0 / LANES, BF16)
    return jnp.dot(jnp.concatenate([hi, lo], axis=-1), ones, preferred_element_type=F32)


def _norm_fwd(z, g, b, mean=_row_mean):
    mu = mean(z)
    xc = z - mu
    var = mean(xc * xc)
    return xc * lax.rsqrt(var + LN_EPS) * g + b


def _norm_bwd(z, g, dy, mean=_row_mean):
    mu = mean(z)
    xc = z - mu
    var = mean(xc * xc)
    rstd = lax.rsqrt(var + LN_EPS)
    xhat = xc * rstd
    dxh = dy * g
    m1 = mean(dxh)
    m2 = mean(dxh * xhat)
    return rstd * (dxh - m1 - xhat * m2), xhat


GELU_K = 0.7978845608028654
GELU_C = 0.044715


def _gelu(x):
    return 0.5 * x * (1.0 + jnp.tanh(GELU_K * (x + GELU_C * x * x * x)))


def _gelu_grad(x):
    t = jnp.tanh(GELU_K * (x + GELU_C * x * x * x))
    return 0.5 * (1.0 + t) + 0.5 * x * (1.0 - t * t) * GELU_K * (1.0 + 3.0 * GELU_C * x * x)


def _softplus(y):
    return jnp.maximum(y, 0.0) + jnp.log1p(jnp.exp(-jnp.abs(y)))


def _neg_expm1(y):
    series = -y * (1.0 + y * (0.5 + y * (1.0 / 6.0 + y * (1.0 / 24.0 + y * (1.0 / 120.0 + y * (1.0 / 720.0))))))
    return jnp.where(y > -0.25, series, 1.0 - jnp.exp(y))


def _colsum(x):
    return jnp.sum(x, axis=0, keepdims=True)


def _shifted_taps(src_ref, base, rows, taps):
    acc = None
    for o, coef in taps:
        term = coef() * src_ref[pl.ds(base + o, rows), :]
        acc = term if acc is None else acc + term
    return acc


def _shifted_corr(src_ref, base, rows, d, acc_ref, offs):
    for k, o in enumerate(offs):
        prod = d * src_ref[pl.ds(base + o, rows), :]
        part = jnp.sum(prod.reshape(rows // SUBLANES, SUBLANES, prod.shape[-1]), axis=0)
        acc_ref[SUBLANES * k:SUBLANES * (k + 1), :] += part


def _front_pad(ktaps):
    return SUBLANES * ((ktaps - 1 + SUBLANES - 1) // SUBLANES)


def _pad_rows(ktaps):
    return _front_pad(ktaps) + SUBLANES


def _ffn_tiles(S, F):
    tm = _divisor_tile(S, 1024, 16)
    tf = _divisor_tile(F, 256, LANES)
    return tm, tf


def _ffn_fwd(x, wgu, wd, g, b, alpha, name):
    S, D = x.shape
    F = wd[0].shape[1]
    tm, tf = _ffn_tiles(S, F)
    nf = F // tf
    wg_arr, wg_spec = _layer_block(wgu, (D, tf), lambda i, j: (0, j))
    wu_arr, wu_spec = _layer_block(wgu, (D, tf), lambda i, j: (0, nf + j))
    wd_arr, wd_spec = _layer_block(wd, (tf, D), lambda i, j: (j, 0))

    def body(x_ref, wg_ref, wu_ref, wd_ref, g_ref, b_ref, y_ref, z_ref, hg_ref, hu_ref, acc_ref, xb_ref):
        j = pl.program_id(1)

        @pl.when(j == 0)
        def _():
            xb_ref[...] = x_ref[...].astype(BF16)
            acc_ref[...] = jnp.zeros_like(acc_ref)

        xb = xb_ref[...]
        hg = _dot(xb, wg_ref[...])
        hu = _dot(xb, wu_ref[...])
        hg_ref[...] = hg
        hu_ref[...] = hu
        a = (hg * _sigmoid(hg) * hu).astype(BF16)
        acc_ref[...] += _dot(a, wd_ref[...])

        @pl.when(j == nf - 1)
        def _():
            z = alpha * x_ref[...] + 0.5 * acc_ref[...]
            z_ref[...] = z
            y_ref[...] = _norm_fwd(z, g_ref[...], b_ref[...])

    blk = 2 * (3 * tm * D * 4 + 2 * tm * tf * 4 + 3 * D * tf * 2) + tm * D * 6 + 3 * tm * tf * 4
    return pl.pallas_call(
        body, name=name, grid=(S // tm, nf),
        in_specs=[_bs((tm, D), lambda i, j: (i, 0)), wg_spec, wu_spec, wd_spec,
                  _bs((1, D), lambda i, j: (0, 0)), _bs((1, D), lambda i, j: (0, 0))],
        out_specs=[_bs((tm, D), lambda i, j: (i, 0)), _bs((tm, D), lambda i, j: (i, 0)),
                   _bs((tm, tf), lambda i, j: (i, j)), _bs((tm, tf), lambda i, j: (i, j))],
        out_shape=[jax.ShapeDtypeStruct((S, D), F32), jax.ShapeDtypeStruct((S, D), F32),
                   jax.ShapeDtypeStruct((S, F), F32), jax.ShapeDtypeStruct((S, F), F32)],
        scratch_shapes=[pltpu.VMEM((tm, D), F32), pltpu.VMEM((tm, D), BF16)],
        compiler_params=_cparams(blk),
    )(x, wg_arr, wu_arr, wd_arr, g, b)


def _ffn_bwd(dy, z, hg, hu, wgu, wd, g, alpha, name):
    S, D = dy.shape
    F = wd[0].shape[1]
    tm, tf = _ffn_tiles(S, F)
    nf = F // tf
    wg_arr, wg_spec = _layer_block(wgu, (D, tf), lambda i, j: (0, j))
    wu_arr, wu_spec = _layer_block(wgu, (D, tf), lambda i, j: (0, nf + j))
    wd_arr, wd_spec = _layer_block(wd, (tf, D), lambda i, j: (j, 0))

    def body(dy_ref, z_ref, hg_ref, hu_ref, wg_ref, wu_ref, wd_ref, g_ref,
             dx_ref, df_ref, a_ref, dhg_ref, dhu_ref, dg_ref, db_ref, acc_ref):
        i = pl.program_id(0)
        j = pl.program_id(1)

        @pl.when((i == 0) & (j == 0))
        def _():
            dg_ref[...] = jnp.zeros_like(dg_ref)
            db_ref[...] = jnp.zeros_like(db_ref)

        @pl.when(j == 0)
        def _():
            dy_ = dy_ref[...]
            dz, xhat = _norm_bwd(z_ref[...], g_ref[...], dy_)
            dg_ref[...] += _colsum(dy_ * xhat)
            db_ref[...] += _colsum(dy_)
            acc_ref[...] = alpha * dz
            df_ref[...] = (0.5 * dz).astype(BF16)

        da = _dot_nt(df_ref[...], wd_ref[...])
        hg_ = hg_ref[...]
        hu_ = hu_ref[...]
        s = _sigmoid(hg_)
        sl = hg_ * s
        dgate = (da * hu_ * (s * (1.0 + hg_ * (1.0 - s)))).astype(BF16)
        dup = (da * sl).astype(BF16)
        a_ref[...] = (sl * hu_).astype(BF16)
        dhg_ref[...] = dgate
        dhu_ref[...] = dup
        acc_ref[...] += _dot_nt(dgate, wg_ref[...]) + _dot_nt(dup, wu_ref[...])

        @pl.when(j == nf - 1)
        def _():
            dx_ref[...] = acc_ref[...]

    blk = 2 * (2 * tm * D * 4 + tm * D * 2 + 2 * tm * tf * 4 + 3 * tm * tf * 2 + 3 * D * tf * 2) + 3 * tm * D * 4 + 8 * tm * tf * 4
    once = dict(pipeline_mode=pl.Buffered(1))
    return pl.pallas_call(
        body, name=name, grid=(S // tm, nf),
        in_specs=[_bs((tm, D), lambda i, j: (i, 0), **once), _bs((tm, D), lambda i, j: (i, 0), **once),
                  _bs((tm, tf), lambda i, j: (i, j)), _bs((tm, tf), lambda i, j: (i, j)),
                  wg_spec, wu_spec, wd_spec, _bs((1, D), lambda i, j: (0, 0))],
        out_specs=[_bs((tm, D), lambda i, j: (i, 0)), _bs((tm, D), lambda i, j: (i, 0)),
                   _bs((tm, tf), lambda i, j: (i, j)), _bs((tm, tf), lambda i, j: (i, j)), _bs((tm, tf), lambda i, j: (i, j)),
                   _bs((1, D), lambda i, j: (0, 0)), _bs((1, D), lambda i, j: (0, 0))],
        out_shape=[jax.ShapeDtypeStruct((S, D), F32), jax.ShapeDtypeStruct((S, D), BF16),
                   jax.ShapeDtypeStruct((S, F), BF16), jax.ShapeDtypeStruct((S, F), BF16), jax.ShapeDtypeStruct((S, F), BF16),
                   jax.ShapeDtypeStruct((1, D), F32), jax.ShapeDtypeStruct((1, D), F32)],
        scratch_shapes=[pltpu.VMEM((tm, D), F32)],
        compiler_params=_cparams(blk),
    )(dy, z, hg, hu, wg_arr, wu_arr, wd_arr, g)


def _mm_tn(a, b, name):
    S, M = a.shape
    N = b.shape[1]
    bm = _divisor_tile(M, 1408, LANES)
    bn = _divisor_tile(N, 1408, LANES)
    tk = _divisor_tile(S, 512, 16)

    def body(a_ref, b_ref, o_ref):
        @pl.when(pl.program_id(2) == 0)
        def _():
            o_ref[...] = jnp.zeros_like(o_ref)

        o_ref[...] += _dot_tn(a_ref[...].astype(BF16), b_ref[...].astype(BF16))

    blk = 2 * (tk * bm * a.dtype.itemsize + tk * bn * b.dtype.itemsize + bm * bn * 4) + tk * bm * 4 + bm * bn * 4
    return pl.pallas_call(
        body, name=name, grid=(M // bm, N // bn, S // tk),
        in_specs=[_bs((tk, bm), lambda i, j, k: (k, i)), _bs((tk, bn), lambda i, j, k: (k, j))],
        out_specs=_bs((bm, bn), lambda i, j, k: (i, j)),
        out_shape=jax.ShapeDtypeStruct((M, N), F32),
        compiler_params=_cparams(blk),
    )(a, b)


def _mm_tn_pair(a, b0, b1, name):
    S, M = a.shape
    N = b0.shape[1]
    assert b1.shape == b0.shape
    bm = _divisor_tile(M, 1408, LANES)
    bn = _divisor_tile(N, 1408, LANES)
    tk = _divisor_tile(S, 512, 16)
    nb = N // bn

    def body(a_ref, b0_ref, b1_ref, o_ref):
        j = pl.program_id(1)

        @pl.when(pl.program_id(2) == 0)
        def _():
            o_ref[...] = jnp.zeros_like(o_ref)

        ab = a_ref[...].astype(BF16)

        @pl.when(j < nb)
        def _():
            o_ref[...] += _dot_tn(ab, b0_ref[...])

        @pl.when(j >= nb)
        def _():
            o_ref[...] += _dot_tn(ab, b1_ref[...])

    b0_map = lambda i, j, k: (jnp.where(j < nb, k, S // tk - 1), jnp.minimum(j, nb - 1))
    b1_map = lambda i, j, k: (jnp.where(j >= nb, k, 0), jnp.maximum(j - nb, 0))
    blk = 2 * (tk * bm * a.dtype.itemsize + 2 * tk * bn * 2 + bm * bn * 4) + tk * bm * 4 + bm * bn * 4
    return pl.pallas_call(
        body, name=name, grid=(M // bm, 2 * nb, S // tk),
        in_specs=[_bs((tk, bm), lambda i, j, k: (k, i)), _bs((tk, bn), b0_map), _bs((tk, bn), b1_map)],
        out_specs=_bs((bm, bn), lambda i, j, k: (i, j)),
        out_shape=jax.ShapeDtypeStruct((M, 2 * N), F32),
        compiler_params=_cparams(blk),
    )(a, b0, b1)


def _mix_in(x, wt, bias, sections, name):
    S, D = x.shape
    tm = _divisor_tile(S, 256, 16)
    n = len(sections)

    def body(x_ref, w_ref, b_ref, *o_refs):
        xb = x_ref[...].astype(BF16)
        for (off, width, dtype), o_ref in zip(sections, o_refs):
            o_ref[...] = (_dot_nt(xb, w_ref[off:off + width, :]) + b_ref[:, off:off + width]).astype(dtype)

    total = wt.shape[0]
    blk = 2 * (tm * D * 4 + sum(tm * w * jnp.dtype(dt).itemsize for _, w, dt in sections)) + total * D * 2 + 3 * tm * D * 4
    return pl.pallas_call(
        body, name=name, grid=(S // tm,),
        in_specs=[_bs((tm, D), lambda i: (i, 0)), _resident((total, D)), _resident((1, total))],
        out_specs=[_bs((tm, w), lambda i: (i, 0)) for _, w, _ in sections],
        out_shape=[jax.ShapeDtypeStruct((S, w), dt) for _, w, dt in sections],
        compiler_params=_cparams(blk),
    )(x, wt, bias)


def _mix_dx(dz, parts, wt, sections, alpha, name):
    S, D = dz.shape
    tm = _divisor_tile(S, 256, 16)
    n = len(parts)

    def body(*refs):
        dz_ref = refs[0]
        p_refs = refs[1:1 + n]
        w_ref = refs[1 + n]
        o_ref = refs[2 + n]
        acc = alpha * dz_ref[...]
        for p_ref, (off, width, _) in zip(p_refs, sections):
            acc = acc + _dot(p_ref[...], w_ref[off:off + width, :])
        o_ref[...] = acc

    widths = [p.shape[1] for p in parts]
    total = wt.shape[0]
    blk = 2 * (2 * tm * D * 4 + sum(tm * w * 2 for w in widths)) + total * D * 2 + 2 * tm * D * 4
    return pl.pallas_call(
        body, name=name, grid=(S // tm,),
        in_specs=[_bs((tm, D), lambda i: (i, 0))] + [_bs((tm, w), lambda i: (i, 0)) for w in widths]
                 + [_resident((total, D))],
        out_specs=_bs((tm, D), lambda i: (i, 0)),
        out_shape=jax.ShapeDtypeStruct((S, D), F32),
        compiler_params=_cparams(blk),
    )(dz, *parts, wt)


def _conv_branch_fwd(cv, cg, w, b, gg, gb, name):
    S, C = cv.shape
    K = w.shape[0]
    assert C // CONV_GROUPS == LANES
    padf = _front_pad(K)
    R = min(CONV_ROWS, S)
    E = min(EW_ROWS, S)

    def body(cv_ref, cg_ref, w_ref, b_ref, gg_ref, gb_ref, c2_ref, c4_ref, pad_ref):
        pad_ref[0:padf, :] = jnp.zeros((padf, LANES), F32)
        pad_ref[S + padf:S + padf + SUBLANES, :] = jnp.zeros((SUBLANES, LANES), F32)

        def fill(i, carry):
            r = pl.multiple_of(i * E, E)
            pad_ref[pl.ds(r + padf, E), :] = cv_ref[pl.ds(r, E), :] * _sigmoid(cg_ref[pl.ds(r, E), :])
            return carry

        lax.fori_loop(0, S // E, fill, 0)
        taps = [(padf - (K - 1) + k, functools.partial(lambda k: w_ref[k:k + 1, :], k)) for k in range(K)]

        def conv(i, carry):
            r = pl.multiple_of(i * R, R)
            c2_ref[pl.ds(r, R), :] = _shifted_taps(pad_ref, r, R, taps) + b_ref[...]
            return carry

        lax.fori_loop(0, S // R, conv, 0)

        def norm(i, carry):
            r = pl.multiple_of(i * E, E)
            c3 = _norm_fwd(c2_ref[pl.ds(r, E), :], gg_ref[...], gb_ref[...], _lane_mean)
            c4_ref[pl.ds(r, E), :] = (c3 * _sigmoid(c3)).astype(BF16)
            return carry

        lax.fori_loop(0, S // E, norm, 0)

    col = lambda i: (0, i)
    blk = 2 * (3 * S * LANES * 4 + S * LANES * 2) + (S + _pad_rows(K)) * LANES * 4
    return pl.pallas_call(
        body, name=name, grid=(C // LANES,),
        in_specs=[_bs((S, LANES), col), _bs((S, LANES), col), _bs((K, LANES), col),
                  _bs((1, LANES), col), _bs((1, LANES), col), _bs((1, LANES), col)],
        out_specs=[_bs((S, LANES), col), _bs((S, LANES), col)],
        out_shape=[jax.ShapeDtypeStruct((S, C), F32), jax.ShapeDtypeStruct((S, C), BF16)],
        scratch_shapes=[pltpu.VMEM((S + _pad_rows(K), LANES), F32)],
        compiler_params=_cparams(blk),
    )(cv, cg, w, b, gg, gb)


def _conv_branch_bwd(dc4, c2, cv, cg, w, gg, gb, name):
    S, C = cv.shape
    K = w.shape[0]
    padf = _front_pad(K)
    R = min(CONV_ROWS, S)
    E = min(EW_ROWS, S)

    def body(dc4_ref, c2_ref, cv_ref, cg_ref, w_ref, gg_ref, gb_ref,
             dcv_ref, dcg_ref, dw_ref, dwb_ref, dgg_ref, dgb_ref, scv_ref, scg_ref,
             dpad_ref, cpad_ref, dwacc_ref):
        cpad_ref[0:padf, :] = jnp.zeros((padf, LANES), F32)
        cpad_ref[S + padf:S + padf + SUBLANES, :] = jnp.zeros((SUBLANES, LANES), F32)
        dpad_ref[S:S + padf + SUBLANES, :] = jnp.zeros((padf + SUBLANES, LANES), F32)
        dwacc_ref[...] = jnp.zeros_like(dwacc_ref)
        for ref in (dwb_ref, dgg_ref, dgb_ref, scv_ref, scg_ref):
            ref[...] = jnp.zeros_like(ref)

        def norm_pass(i, carry):
            r = pl.multiple_of(i * E, E)
            g_ = gg_ref[...]
            c2 = c2_ref[pl.ds(r, E), :]
            xc = c2 - _lane_mean(c2)
            rstd = lax.rsqrt(_lane_mean(xc * xc) + LN_EPS)
            xhat = xc * rstd
            c3 = xhat * g_ + gb_ref[...]
            s = _sigmoid(c3)
            dc3 = dc4_ref[pl.ds(r, E), :].astype(F32) * (s * (1.0 + c3 * (1.0 - s)))
            dgg_ref[...] += _colsum(dc3 * xhat)
            dgb_ref[...] += _colsum(dc3)
            dxh = dc3 * g_
            dc2 = rstd * (dxh - _lane_mean(dxh) - xhat * _lane_mean(dxh * xhat))
            dpad_ref[pl.ds(r, E), :] = dc2
            dwb_ref[...] += _colsum(dc2)
            cpad_ref[pl.ds(r + padf, E), :] = cv_ref[pl.ds(r, E), :] * _sigmoid(cg_ref[pl.ds(r, E), :])
            return carry

        lax.fori_loop(0, S // E, norm_pass, 0)
        taps = [(K - 1 - k, functools.partial(lambda k: w_ref[k:k + 1, :], k)) for k in range(K)]
        offs = [padf - (K - 1) + k for k in range(K)]

        def conv_pass(i, carry):
            r = pl.multiple_of(i * R, R)
            dc1 = _shifted_taps(dpad_ref, r, R, taps)
            sg = _sigmoid(cg_ref[pl.ds(r, R), :])
            cv_ = cv_ref[pl.ds(r, R), :]
            dcv = dc1 * sg
            dcg = dc1 * cv_ * sg * (1.0 - sg)
            dcv_ref[pl.ds(r, R), :] = dcv.astype(BF16)
            dcg_ref[pl.ds(r, R), :] = dcg.astype(BF16)
            scv_ref[...] += _colsum(dcv)
            scg_ref[...] += _colsum(dcg)
            _shifted_corr(cpad_ref, r, R, dpad_ref[pl.ds(r, R), :], dwacc_ref, offs)
            return carry

        lax.fori_loop(0, S // R, conv_pass, 0)
        for k in range(K):
            dw_ref[k:k + 1, :] = _colsum(dwacc_ref[SUBLANES * k:SUBLANES * (k + 1), :])

    col = lambda i: (0, i)
    row = jax.ShapeDtypeStruct((1, C), F32)
    blk = 2 * (4 * S * LANES * 4 + 2 * S * LANES * 2) + 2 * (S + _pad_rows(K)) * LANES * 4
    return pl.pallas_call(
        body, name=name, grid=(C // LANES,),
        in_specs=[_bs((S, LANES), col)] * 4 + [_bs((K, LANES), col), _bs((1, LANES), col), _bs((1, LANES), col)],
        out_specs=[_bs((S, LANES), col), _bs((S, LANES), col), _bs((K, LANES), col)] + [_bs((1, LANES), col)] * 5,
        out_shape=[jax.ShapeDtypeStruct((S, C), BF16), jax.ShapeDtypeStruct((S, C), BF16),
                   jax.ShapeDtypeStruct((K, C), F32), row, row, row, row, row],
        scratch_shapes=[pltpu.VMEM((S + _pad_rows(K), LANES), F32), pltpu.VMEM((S + _pad_rows(K), LANES), F32),
                        pltpu.VMEM((SUBLANES * K, LANES), F32)],
        compiler_params=_cparams(blk),
    )(dc4, c2, cv, cg, w, gg, gb)


def _short_conv_fwd(xin, w, b, name):
    S, C = xin.shape
    K = w.shape[0]
    padf = _front_pad(K)
    R = min(CONV_ROWS, S)
    E = min(EW_ROWS, S)

    def body(x_ref, w_ref, b_ref, o_ref, pad_ref):
        pad_ref[0:padf, :] = jnp.zeros((padf, LANES), F32)
        pad_ref[S + padf:S + padf + SUBLANES, :] = jnp.zeros((SUBLANES, LANES), F32)

        def fill(i, carry):
            r = pl.multiple_of(i * E, E)
            pad_ref[pl.ds(r + padf, E), :] = x_ref[pl.ds(r, E), :]
            return carry

        lax.fori_loop(0, S // E, fill, 0)
        taps = [(padf - (K - 1) + k, functools.partial(lambda k: w_ref[k:k + 1, :], k)) for k in range(K)]

        def conv(i, carry):
            r = pl.multiple_of(i * R, R)
            o_ref[pl.ds(r, R), :] = _shifted_taps(pad_ref, r, R, taps) + b_ref[...]
            return carry

        lax.fori_loop(0, S // R, conv, 0)

    col = lambda i: (0, i)
    blk = 2 * (2 * S * LANES * 4) + (S + _pad_rows(K)) * LANES * 4
    return pl.pallas_call(
        body, name=name, grid=(C // LANES,),
        in_specs=[_bs((S, LANES), col), _bs((K, LANES), col), _bs((1, LANES), col)],
        out_specs=_bs((S, LANES), col),
        out_shape=jax.ShapeDtypeStruct((S, C), F32),
        scratch_shapes=[pltpu.VMEM((S + _pad_rows(K), LANES), F32)],
        compiler_params=_cparams(blk),
    )(xin, w, b)


def _short_conv_bwd(dy, xin, w, name):
    S, C = xin.shape
    K = w.shape[0]
    padf = _front_pad(K)
    R = min(CONV_ROWS, S)
    E = min(EW_ROWS, S)

    def body(dy_ref, x_ref, w_ref, dx_ref, dw_ref, db_ref, sx_ref, dpad_ref, xpad_ref, dwacc_ref):
        xpad_ref[0:padf, :] = jnp.zeros((padf, LANES), F32)
        xpad_ref[S + padf:S + padf + SUBLANES, :] = jnp.zeros((SUBLANES, LANES), F32)
        dpad_ref[S:S + padf + SUBLANES, :] = jnp.zeros((padf + SUBLANES, LANES), F32)
        dwacc_ref[...] = jnp.zeros_like(dwacc_ref)
        db_ref[...] = jnp.zeros_like(db_ref)
        sx_ref[...] = jnp.zeros_like(sx_ref)

        def fill(i, carry):
            r = pl.multiple_of(i * E, E)
            d = dy_ref[pl.ds(r, E), :]
            dpad_ref[pl.ds(r, E), :] = d
            db_ref[...] += _colsum(d)
            xpad_ref[pl.ds(r + padf, E), :] = x_ref[pl.ds(r, E), :]
            return carry

        lax.fori_loop(0, S // E, fill, 0)
        taps = [(K - 1 - k, functools.partial(lambda k: w_ref[k:k + 1, :], k)) for k in range(K)]
        offs = [padf - (K - 1) + k for k in range(K)]

        def conv_pass(i, carry):
            r = pl.multiple_of(i * R, R)
            dx = _shifted_taps(dpad_ref, r, R, taps)
            dx_ref[pl.ds(r, R), :] = dx.astype(BF16)
            sx_ref[...] += _colsum(dx)
            _shifted_corr(xpad_ref, r, R, dpad_ref[pl.ds(r, R), :], dwacc_ref, offs)
            return carry

        lax.fori_loop(0, S // R, conv_pass, 0)
        for k in range(K):
            dw_ref[k:k + 1, :] = _colsum(dwacc_ref[SUBLANES * k:SUBLANES * (k + 1), :])

    col = lambda i: (0, i)
    row = jax.ShapeDtypeStruct((1, C), F32)
    blk = 2 * (2 * S * LANES * 4 + S * LANES * 2) + 2 * (S + _pad_rows(K)) * LANES * 4
    return pl.pallas_call(
        body, name=name, grid=(C // LANES,),
        in_specs=[_bs((S, LANES), col), _bs((S, LANES), col), _bs((K, LANES), col)],
        out_specs=[_bs((S, LANES), col), _bs((K, LANES), col), _bs((1, LANES), col), _bs((1, LANES), col)],
        out_shape=[jax.ShapeDtypeStruct((S, C), BF16), jax.ShapeDtypeStruct((K, C), F32), row, row],
        scratch_shapes=[pltpu.VMEM((S + _pad_rows(K), LANES), F32), pltpu.VMEM((S + _pad_rows(K), LANES), F32),
                        pltpu.VMEM((SUBLANES * K, LANES), F32)],
        compiler_params=_cparams(blk),
    )(dy, xin, w)


def _band_panels(width, block):
    assert width % LANES == 0 and block <= LANES
    panels = []
    for c0 in range(0, width, 2 * LANES):
        c1 = min(width, c0 + 2 * LANES)
        r0 = (c0 // block) * block // LANES * LANES
        r1 = min(width, -(-(-(-c1 // block) * block) // LANES) * LANES)
        panels.append((r0, r1, c0, c1))
    return panels


def _gates_fwd(r1, wa, wx, ba, bx, lam, name):
    S, R = r1.shape
    tm = _divisor_tile(S, 256, 16)
    panels = _band_panels(R, R // RNN_BLOCKS)

    def body(r1_ref, wa_ref, wx_ref, ba_ref, bx_ref, lam_ref, ra_ref, ri_ref, a_ref, uu_ref):
        for r0, r1e, c0, c1 in panels:
            rb = r1_ref[:, r0:r1e].astype(BF16)
            ra = _sigmoid(_dot(rb, wa_ref[r0:r1e, c0:c1]) + ba_ref[:, c0:c1])
            ri = _sigmoid(_dot(rb, wx_ref[r0:r1e, c0:c1]) + bx_ref[:, c0:c1])
            log_a = -RG_LRU_C * ra * _softplus(-lam_ref[:, c0:c1])
            ra_ref[:, c0:c1] = ra
            ri_ref[:, c0:c1] = ri
            a_ref[:, c0:c1] = jnp.exp(log_a)
            uu_ref[:, c0:c1] = jnp.sqrt(_neg_expm1(2.0 * log_a)) * (ri * r1_ref[:, c0:c1])

    blk = 2 * (5 * tm * R * 4) + 2 * R * R * 2 + 6 * tm * R * 4
    tile = _bs((tm, R), lambda i: (i, 0))
    return pl.pallas_call(
        body, name=name, grid=(S // tm,),
        in_specs=[tile, _resident((R, R)), _resident((R, R)), _resident((1, R)), _resident((1, R)), _resident((1, R))],
        out_specs=[tile] * 4,
        out_shape=[jax.ShapeDtypeStruct((S, R), F32)] * 4,
        compiler_params=_cparams(blk),
    )(r1, wa, wx, ba, bx, lam)


def _gates_bwd(guu, da, ra, ri, r1, wa, wx, lam, name):
    S, R = r1.shape
    tm = _divisor_tile(S, 256, 16)
    nsteps = S // tm
    panels = _band_panels(R, R // RNN_BLOCKS)

    def body(g_ref, da_ref, ra_ref, ri_ref, r1_ref, wa_ref, wx_ref, lam_ref,
             dr1_ref, dpa_ref, dpx_ref, dba_ref, dbx_ref, dlam_ref):
        i = pl.program_id(0)

        @pl.when(i == 0)
        def _():
            dba_ref[...] = jnp.zeros_like(dba_ref)
            dbx_ref[...] = jnp.zeros_like(dbx_ref)
            dlam_ref[...] = jnp.zeros_like(dlam_ref)

        g = g_ref[...]
        ra = ra_ref[...]
        ri = ri_ref[...]
        r1_ = r1_ref[...]
        sp = _softplus(-lam_ref[...])
        log_a = -RG_LRU_C * ra * sp
        a = jnp.exp(log_a)
        mult = jnp.sqrt(_neg_expm1(2.0 * log_a))
        d_ri = g * mult * r1_
        dr1 = g * mult * ri
        dmult = g * ri * r1_
        dlog_a = da_ref[...] * a - dmult * (a * a) / mult
        dra = dlog_a * (-RG_LRU_C * sp)
        dlam_ref[...] += _colsum(dlog_a * (-RG_LRU_C * ra))
        dpa = dra * ra * (1.0 - ra)
        dpx = d_ri * ri * (1.0 - ri)
        dba_ref[...] += _colsum(dpa)
        dbx_ref[...] += _colsum(dpx)
        dpa_b = dpa.astype(BF16)
        dpx_b = dpx.astype(BF16)
        dpa_ref[...] = dpa_b
        dpx_ref[...] = dpx_b
        dr1_ref[...] = dr1
        for k0, k1, c0, c1 in panels:
            dr1_ref[:, c0:c1] += (_dot_nt(dpa_ref[:, k0:k1], wa_ref[c0:c1, k0:k1])
                                  + _dot_nt(dpx_ref[:, k0:k1], wx_ref[c0:c1, k0:k1]))

        @pl.when(i == nsteps - 1)
        def _():
            dlam_ref[...] = dlam_ref[...] * (-_sigmoid(-lam_ref[...]))

    blk = 2 * (6 * tm * R * 4 + 2 * tm * R * 2) + 2 * R * R * 2 + 10 * tm * R * 4
    tile = _bs((tm, R), lambda i: (i, 0))
    rowspec = _bs((1, R), lambda i: (0, 0))
    row = jax.ShapeDtypeStruct((1, R), F32)
    return pl.pallas_call(
        body, name=name, grid=(nsteps,),
        in_specs=[tile] * 5 + [_resident((R, R)), _resident((R, R)), _resident((1, R))],
        out_specs=[tile, tile, tile, rowspec, rowspec, rowspec],
        out_shape=[jax.ShapeDtypeStruct((S, R), F32), jax.ShapeDtypeStruct((S, R), BF16), jax.ShapeDtypeStruct((S, R), BF16),
                   row, row, row],
        compiler_params=_cparams(blk),
    )(guu, da, ra, ri, r1, wa, wx, lam)


def _embed_blocks(w, name):
    H, bk, _ = w.shape

    def body(w_ref, o_ref):
        o_ref[...] = jnp.zeros_like(o_ref)
        for h in range(H):
            o_ref[bk * h:bk * (h + 1), bk * h:bk * (h + 1)] = w_ref[h].astype(BF16)

    return pl.pallas_call(body, name=name, out_shape=jax.ShapeDtypeStruct((H * bk, H * bk), BF16),
                          compiler_params=_cparams(3 * H * bk * H * bk * 2))(w)


def _block_grads(r1, dpa, dpx, name):
    S, R = r1.shape
    bk = R // RNN_BLOCKS
    tk = _divisor_tile(S, 512, 16)
    nsteps = S // tk
    panels = _band_panels(R, bk)

    def body(r1_ref, dpa_ref, dpx_ref, ga_ref, gx_ref, acca_ref, accx_ref):
        k = pl.program_id(0)

        @pl.when(k == 0)
        def _():
            acca_ref[...] = jnp.zeros_like(acca_ref)
            accx_ref[...] = jnp.zeros_like(accx_ref)

        for k0, k1, c0, c1 in panels:
            rb = r1_ref[:, k0:k1].astype(BF16)
            acca_ref[k0:k1, c0:c1] += _dot_tn(rb, dpa_ref[:, c0:c1])
            accx_ref[k0:k1, c0:c1] += _dot_tn(rb, dpx_ref[:, c0:c1])

        @pl.when(k == nsteps - 1)
        def _():
            for h in range(RNN_BLOCKS):
                ga_ref[h] = acca_ref[bk * h:bk * (h + 1), bk * h:bk * (h + 1)]
                gx_ref[h] = accx_ref[bk * h:bk * (h + 1), bk * h:bk * (h + 1)]

    tile = lambda: _bs((tk, R), lambda k: (k, 0))
    out = _bs((RNN_BLOCKS, bk, bk), lambda k: (0, 0, 0))
    sds = jax.ShapeDtypeStruct((RNN_BLOCKS, bk, bk), F32)
    return pl.pallas_call(
        body, name=name, grid=(nsteps,),
        in_specs=[tile(), tile(), tile()], out_specs=[out, out], out_shape=[sds, sds],
        scratch_shapes=[pltpu.VMEM((R, R), F32), pltpu.VMEM((R, R), F32)],
        compiler_params=_cparams(2 * (tk * R * 8) + 2 * R * R * 4 + 4 * tk * R * 4),
    )(r1, dpa, dpx)


def _scan_geometry(S):
    nseg = SCAN_SEGMENTS if S % (SCAN_SEGMENTS * SUBLANES) == 0 else SUBLANES
    return nseg, S // nseg


def _steps(n, step, init):
    u = SCAN_UNROLL

    def trip(t, carry):
        for k in range(u):
            carry = step(t * u + k, carry)
        return carry

    carry = lax.fori_loop(0, n // u, trip, init)
    for j in range(n - n % u, n):
        carry = step(j, carry)
    return carry


def _scan_fwd(a, u, name):
    S, C = a.shape
    nseg, L = _scan_geometry(S)
    T = min(SUBLANES, L)

    def body(a3, u3, h3, ta_ref, tu_ref, e_ref, p_ref, init_ref):

        def to_steps(i, carry):
            j0 = pl.multiple_of(i * T, T)
            ta_ref[pl.ds(j0, T)] = jnp.swapaxes(a3[:, pl.ds(j0, T), :], 0, 1)
            tu_ref[pl.ds(j0, T)] = jnp.swapaxes(u3[:, pl.ds(j0, T), :], 0, 1)
            return carry

        lax.fori_loop(0, L // T, to_steps, 0)

        def run1(j, carry):
            hs, ps = carry
            aj = ta_ref[j]
            return aj * hs + tu_ref[j], aj * ps

        e_ref[...], p_ref[...] = _steps(L, run1, (jnp.zeros((nseg, LANES), F32), jnp.ones((nseg, LANES), F32)))
        init_ref[0:1, :] = jnp.zeros((1, LANES), F32)
        for s in range(1, nseg):
            init_ref[s:s + 1, :] = e_ref[s - 1:s, :] + p_ref[s - 1:s, :] * init_ref[s - 1:s, :]

        def run2(j, hs):
            hs = ta_ref[j] * hs + tu_ref[j]
            tu_ref[j] = hs
            return hs

        _steps(L, run2, init_ref[...])

        def from_steps(i, carry):
            j0 = pl.multiple_of(i * T, T)
            h3[:, pl.ds(j0, T), :] = jnp.swapaxes(tu_ref[pl.ds(j0, T)], 0, 1)
            return carry

        lax.fori_loop(0, L // T, from_steps, 0)

    seg_block = _bs((nseg, L, LANES), lambda i: (0, 0, i))
    blk = 2 * (3 * S * LANES * 4) + 2 * S * LANES * 4
    return pl.pallas_call(
        body, name=name, grid=(C // LANES,),
        in_specs=[seg_block, seg_block],
        out_specs=seg_block,
        out_shape=jax.ShapeDtypeStruct((nseg, L, C), F32),
        scratch_shapes=[pltpu.VMEM((L, nseg, LANES), F32)] * 2 + [pltpu.VMEM((nseg, LANES), F32)] * 3,
        compiler_params=_cparams(blk),
    )(a.reshape(nseg, L, C), u.reshape(nseg, L, C)).reshape(S, C)


def _scan_bwd(a, dh, h, name):
    S, C = a.shape
    nseg, L = _scan_geometry(S)
    T = min(SUBLANES, L)
    assert L >= 2

    def body(a3, d3, h3, g3, da3, ta_ref, td_ref, th_ref, e_ref, p_ref, init_ref):

        def to_steps(i, carry):
            j0 = pl.multiple_of(i * T, T)
            for src, dst in ((a3, ta_ref), (d3, td_ref), (h3, th_ref)):
                dst[pl.ds(j0, T)] = jnp.swapaxes(src[:, pl.ds(j0, T), :], 0, 1)
            return carry

        lax.fori_loop(0, L // T, to_steps, 0)
        seg = lax.broadcasted_iota(jnp.int32, (nseg, LANES), 0)
        b_last = jnp.where(seg == nseg - 1, 0.0, pltpu.roll(ta_ref[0], nseg - 1, axis=0))
        h_first = jnp.where(seg == 0, 0.0, pltpu.roll(th_ref[L - 1], 1, axis=0))

        def run1(jj, carry):
            gs, ps = carry
            j = L - 2 - jj
            bj = ta_ref[j + 1]
            return bj * gs + td_ref[j], bj * ps

        e_ref[...], p_ref[...] = _steps(L - 1, run1, (td_ref[L - 1], b_last))
        init_ref[nseg - 1:nseg, :] = jnp.zeros((1, LANES), F32)
        for s in range(nseg - 2, -1, -1):
            init_ref[s:s + 1, :] = e_ref[s + 1:s + 2, :] + p_ref[s + 1:s + 2, :] * init_ref[s + 1:s + 2, :]

        gs = b_last * init_ref[...] + td_ref[L - 1]
        td_ref[L - 1] = gs
        th_ref[L - 1] = gs * th_ref[L - 2]

        def run2(jj, gs):
            j = L - 2 - jj
            gs = ta_ref[j + 1] * gs + td_ref[j]
            td_ref[j] = gs
            th_ref[j] = gs * th_ref[j - 1]
            return gs

        gs = _steps(L - 2, run2, gs)
        gs = ta_ref[1] * gs + td_ref[0]
        td_ref[0] = gs
        th_ref[0] = gs * h_first

        def from_steps(i, carry):
            j0 = pl.multiple_of(i * T, T)
            g3[:, pl.ds(j0, T), :] = jnp.swapaxes(td_ref[pl.ds(j0, T)], 0, 1)
            da3[:, pl.ds(j0, T), :] = jnp.swapaxes(th_ref[pl.ds(j0, T)], 0, 1)
            return carry

        lax.fori_loop(0, L // T, from_steps, 0)

    seg_block = _bs((nseg, L, LANES), lambda i: (0, 0, i))
    blk = 2 * (5 * S * LANES * 4) + 3 * S * LANES * 4
    g, da = pl.pallas_call(
        body, name=name, grid=(C // LANES,),
        in_specs=[seg_block] * 3,
        out_specs=[seg_block] * 2,
        out_shape=[jax.ShapeDtypeStruct((nseg, L, C), F32)] * 2,
        scratch_shapes=[pltpu.VMEM((L, nseg, LANES), F32)] * 3 + [pltpu.VMEM((nseg, LANES), F32)] * 3,
        compiler_params=_cparams(blk),
    )(a.reshape(nseg, L, C), dh.reshape(nseg, L, C), h.reshape(nseg, L, C))
    return g.reshape(S, C), da.reshape(S, C)


def _mixer_out_fwd(c4, h, rg, gc, gr, x1, wc, wr, wo, g, b, alpha, name):
    S, D = x1.shape
    R = h.shape[1]
    tm = _divisor_tile(S, 256, 16)

    def body(c4_ref, h_ref, rg_ref, gc_ref, gr_ref, x_ref, wc_ref, wr_ref, wo_ref, g_ref, b_ref,
             yc_ref, yr_ref, z_ref, y_ref):
        yc = _dot(c4_ref[...], wc_ref[...])
        q = (h_ref[...] * _gelu(rg_ref[...].astype(F32))).astype(BF16)
        yr = _dot(q, wr_ref[...])
        yc_ref[...] = yc.astype(BF16)
        yr_ref[...] = yr.astype(BF16)
        m = (_sigmoid(gc_ref[...].astype(F32)) * yc + _sigmoid(gr_ref[...].astype(F32)) * yr).astype(BF16)
        z = alpha * x_ref[...] + _dot(m, wo_ref[...])
        z_ref[...] = z
        y_ref[...] = _norm_fwd(z, g_ref[...], b_ref[...])

    blk = 2 * (tm * D * 2 + 2 * tm * R * 4 + 7 * tm * D * 4) + (2 * D * D + R * D) * 2 + 6 * tm * D * 4
    td = _bs((tm, D), lambda i: (i, 0))
    tr = _bs((tm, R), lambda i: (i, 0))
    return pl.pallas_call(
        body, name=name, grid=(S // tm,),
        in_specs=[td, tr, tr, td, td, td, _layer_resident(wc)[1], _layer_resident(wr)[1], _layer_resident(wo)[1],
                  _resident((1, D)), _resident((1, D))],
        out_specs=[td] * 4,
        out_shape=[jax.ShapeDtypeStruct((S, D), BF16)] * 2 + [jax.ShapeDtypeStruct((S, D), F32)] * 2,
        compiler_params=_cparams(blk),
    )(c4, h, rg, gc, gr, x1, wc[0], wr[0], wo[0], g, b)


def _mixer_out_bwd(dy, z, g, wo, yc, yr, gc, gr, name):
    S, D = dy.shape
    tm = _divisor_tile(S, 256, 16)

    def body(dy_ref, z_ref, g_ref, wo_ref, yc_ref, yr_ref, gc_ref, gr_ref,
             dz_ref, dzb_ref, m_ref, dyc_ref, dyr_ref, dgc_ref, dgr_ref, sgc_ref, sgr_ref, dg_ref, db_ref):
        @pl.when(pl.program_id(0) == 0)
        def _():
            for ref in (sgc_ref, sgr_ref, dg_ref, db_ref):
                ref[...] = jnp.zeros_like(ref)

        dy_ = dy_ref[...]
        dz, xhat = _norm_bwd(z_ref[...], g_ref[...], dy_)
        dg_ref[...] += _colsum(dy_ * xhat)
        db_ref[...] += _colsum(dy_)
        dz_ref[...] = dz
        dzb = dz.astype(BF16)
        dzb_ref[...] = dzb
        dm = _dot_nt(dzb, wo_ref[...])
        yc = yc_ref[...].astype(F32)
        yr = yr_ref[...].astype(F32)
        sc = _sigmoid(gc_ref[...].astype(F32))
        sr = _sigmoid(gr_ref[...].astype(F32))
        m_ref[...] = (sc * yc + sr * yr).astype(BF16)
        dyc_ref[...] = (dm * sc).astype(BF16)
        dyr_ref[...] = (dm * sr).astype(BF16)
        dgc = dm * yc * sc * (1.0 - sc)
        dgr = dm * yr * sr * (1.0 - sr)
        dgc_ref[...] = dgc.astype(BF16)
        dgr_ref[...] = dgr.astype(BF16)
        sgc_ref[...] += _colsum(dgc)
        sgr_ref[...] += _colsum(dgr)

    blk = 2 * (7 * tm * D * 4 + 6 * tm * D * 2) + D * D * 2 + 8 * tm * D * 4
    td = _bs((tm, D), lambda i: (i, 0))
    rowspec = _bs((1, D), lambda i: (0, 0))
    row = jax.ShapeDtypeStruct((1, D), F32)
    bfd = jax.ShapeDtypeStruct((S, D), BF16)
    return pl.pallas_call(
        body, name=name, grid=(S // tm,),
        in_specs=[td, td, _resident((1, D)), _layer_resident(wo)[1], td, td, td, td],
        out_specs=[td] * 7 + [rowspec] * 4,
        out_shape=[jax.ShapeDtypeStruct((S, D), F32), bfd, bfd, bfd, bfd, bfd, bfd, row, row, row, row],
        compiler_params=_cparams(blk),
    )(dy, z, g, wo[0], yc, yr, gc, gr)


def _branch_bwd(dyc, dyr, wc, wr, h, rg, name):
    S, D = dyc.shape
    R = h.shape[1]
    tm = _divisor_tile(S, 256, 16)

    def body(dyc_ref, dyr_ref, wc_ref, wr_ref, h_ref, rg_ref, dc4_ref, dh_ref, drg_ref, q_ref, srg_ref):
        @pl.when(pl.program_id(0) == 0)
        def _():
            srg_ref[...] = jnp.zeros_like(srg_ref)

        dc4_ref[...] = _dot_nt(dyc_ref[...], wc_ref[...]).astype(BF16)
        dq = _dot_nt(dyr_ref[...], wr_ref[...])
        h_ = h_ref[...]
        rg_ = rg_ref[...].astype(F32)
        ge = _gelu(rg_)
        dh_ref[...] = dq * ge
        drg = dq * h_ * _gelu_grad(rg_)
        drg_ref[...] = drg.astype(BF16)
        srg_ref[...] += _colsum(drg)
        q_ref[...] = (h_ * ge).astype(BF16)

    blk = 2 * (2 * tm * D * 2 + tm * D * 4 + 3 * tm * R * 4 + 2 * tm * R * 2) + (D * D + R * D) * 2 + 6 * tm * R * 4
    td = _bs((tm, D), lambda i: (i, 0))
    tr = _bs((tm, R), lambda i: (i, 0))
    return pl.pallas_call(
        body, name=name, grid=(S // tm,),
        in_specs=[td, td, _layer_resident(wc)[1], _layer_resident(wr)[1], tr, tr],
        out_specs=[td, tr, tr, tr, _bs((1, R), lambda i: (0, 0))],
        out_shape=[jax.ShapeDtypeStruct((S, D), BF16), jax.ShapeDtypeStruct((S, R), F32), jax.ShapeDtypeStruct((S, R), BF16),
                   jax.ShapeDtypeStruct((S, R), BF16), jax.ShapeDtypeStruct((1, R), F32)],
        compiler_params=_cparams(blk),
    )(dyc, dyr, wc[0], wr[0], h, rg)


def _loss_head(y, target, name):
    S, D = y.shape
    tm = _divisor_tile(S, 512, 16)
    nsteps = S // tm

    def body(y_ref, t_ref, loss_ref, dy_ref, acc_ref):
        i = pl.program_id(0)

        @pl.when(i == 0)
        def _():
            acc_ref[...] = jnp.zeros_like(acc_ref)

        err = y_ref[...] - t_ref[...]
        dy_ref[...] = err * (1.0 / D)
        acc_ref[...] += _colsum(err * err)

        @pl.when(i == nsteps - 1)
        def _():
            loss_ref[...] = jnp.sum(acc_ref[...], axis=-1, keepdims=True) * (0.5 / D)

    td = _bs((tm, D), lambda i: (i, 0))
    return pl.pallas_call(
        body, name=name, grid=(nsteps,),
        in_specs=[td, td],
        out_specs=[_bs((1, 1), lambda i: (0, 0)), td],
        out_shape=[jax.ShapeDtypeStruct((1, 1), F32), jax.ShapeDtypeStruct((S, D), F32)],
        scratch_shapes=[pltpu.VMEM((1, D), F32)],
        compiler_params=_cparams(2 * 3 * tm * D * 4),
    )(y, target)


def _adamw_math(w, g, m, v):
    m = ADAM_B1 * m + (1.0 - ADAM_B1) * g
    v = ADAM_B2 * v + (1.0 - ADAM_B2) * (g * g)
    m_hat = m / (1.0 - ADAM_B1 ** ADAM_STEP)
    v_hat = v / (1.0 - ADAM_B2 ** ADAM_STEP)
    delta = -ADAM_LR * (m_hat / (jnp.sqrt(v_hat) + ADAM_EPS) + ADAM_WD * w)
    return delta, m, v


def _adamw_sharded(w, m, v, own, sib, rem, name):
    _, r, c = w.shape
    tr = _divisor_tile(r, max(16, (1 << 20) // (4 * c) // 16 * 16), 16)

    def body(w_ref, m_ref, v_ref, own_ref, sib_ref, rem_ref, g_ref, d_ref, nm_ref, nv_ref):
        mine = pl.program_id(0) == lax.axis_index("c")
        g = jnp.where(mine, own_ref[...], sib_ref[...]).astype(F32)
        for j in range(N_CHIPS - 1):
            g = g + rem_ref[j].astype(F32)
        delta, nm, nv = _adamw_math(w_ref[...], g, m_ref[...], v_ref[...])
        g_ref[...] = g
        d_ref[...] = delta
        nm_ref[...] = nm
        nv_ref[...] = nv

    tile = _bs((None, tr, c), lambda l, i: (l, i, 0))
    flat = _bs((tr, c), lambda l, i: (i, 0))
    sds = jax.ShapeDtypeStruct(w.shape, F32)
    return pl.pallas_call(
        body, name=name, grid=(2, r // tr),
        in_specs=[tile, tile, tile, flat, flat, _bs((N_CHIPS - 1, None, tr, c), lambda l, i: (0, l, i, 0))],
        out_specs=[tile] * 4,
        out_shape=[sds] * 4,
        compiler_params=_cparams(2 * (7 * tr * c * 4 + (N_CHIPS + 1) * tr * c * 2) + 6 * tr * c * 4),
    )(w, m, v, own, sib, rem)


def _adamw_flat(w, m, v, g, name):
    rows = w.shape[0]
    tr = _divisor_tile(rows, 1024, SUBLANES)

    def body(w_ref, m_ref, v_ref, g_ref, d_ref, nm_ref, nv_ref):
        delta, nm, nv = _adamw_math(w_ref[...], g_ref[...], m_ref[...], v_ref[...])
        d_ref[...] = delta
        nm_ref[...] = nm
        nv_ref[...] = nv

    tile = _bs((tr, LANES), lambda i: (i, 0))
    sds = jax.ShapeDtypeStruct(w.shape, F32)
    return pl.pallas_call(
        body, name=name, grid=(rows // tr,),
        in_specs=[tile] * 4, out_specs=[tile] * 3, out_shape=[sds] * 3,
        compiler_params=_cparams(2 * 7 * tr * LANES * 4),
    )(w, m, v, g)


def _pair_sum_bf16(g0, g1, theirs, name):
    rows, c = g0.shape
    tr = _divisor_tile(rows, max(16, (1 << 20) // (4 * c) // 16 * 16), 16)

    def body(g0_ref, g1_ref, t_ref, o_ref):
        mine = jnp.where(lax.axis_index("c") == 0, g0_ref[...], g1_ref[...])
        o_ref[...] = (mine + t_ref[...]).astype(BF16)

    tile = _bs((tr, c), lambda i: (i, 0))
    return pl.pallas_call(
        body, name=name, grid=(rows // tr,),
        in_specs=[tile, tile, tile], out_specs=tile, out_shape=jax.ShapeDtypeStruct((rows, c), BF16),
        compiler_params=_cparams(2 * 4 * tr * c * 4),
    )(g0, g1, theirs)


ANY = pl.BlockSpec(memory_space=pl.ANY)


def _mesh_position():
    return lax.axis_index("x"), lax.axis_index("y"), lax.axis_index("c")


def _other_chips():
    x, y, c = _mesh_position()
    chips = [(1 - x, y), (x, 1 - y), (1 - x, 1 - y)]
    return 2 * x + y, (x, y, 1 - c), chips, [2 * cx + cy for cx, cy in chips]


def _chip_slab(ref, k, width, by_cols):
    if by_cols:
        start = k * width if isinstance(k, int) else pl.multiple_of(k * width, LANES)
        return ref.at[:, pl.ds(start, width)]
    return ref.at[k]


HBM = pl.BlockSpec(memory_space=pltpu.HBM)
SEM = pl.BlockSpec(memory_space=pltpu.SEMAPHORE)
DATAFLOW = pltpu.SideEffectType.DATAFLOW_SIDE_EFFECTING
N_GATHER_COPIES = 4


def _land_shape(src, by_cols):
    return src.shape[:-1] + (N_CHIPS * src.shape[-1],) if by_cols else (N_CHIPS,) + src.shape


def _gather_copy(src_ref, land_ref, by_cols, send_sems, recv_sems, pos, j, slab, to):
    width = src_ref.shape[-1]
    return pltpu.make_async_remote_copy(src_ref=src_ref, dst_ref=_chip_slab(land_ref, slab, width, by_cols),
                                        send_sem=send_sems.at[N_GATHER_COPIES * pos + j],
                                        recv_sem=recv_sems.at[N_GATHER_COPIES * pos + j],
                                        device_id=to, device_id_type=MESH)


def _gather_start(srcs, by_cols, groups, name):
    U = len(srcs)
    G = len(groups)
    lands = [lax.empty(_land_shape(s, bc), s.dtype) for s, bc in zip(srcs, by_cols)]

    def body(*refs):
        src = refs[:U]
        land = refs[U:2 * U]
        send_sems = refs[2 * U:2 * U + G]
        recv_sems = refs[2 * U + G:2 * U + 2 * G]
        token = refs[-1]
        c = lax.axis_index("c")
        me, sibling, chips, _ = _other_chips()
        targets = [(*chip, c) for chip in chips] + [sibling]
        for g, members in enumerate(groups):
            for pos, u in enumerate(members):
                for j, to in enumerate(targets):
                    _gather_copy(src[u], land[u], by_cols[u], send_sems[g], recv_sems[g], pos, j, me, to).start()
        token[...] = jnp.zeros_like(token)

    sem_shapes = [pltpu.SemaphoreType.DMA((len(m) * N_GATHER_COPIES,)) for m in groups]
    outs = pl.pallas_call(
        body, name=name,
        out_shape=tuple(sem_shapes + sem_shapes + [pltpu.HBM(s.shape, s.dtype) for s in srcs]
                        + [pltpu.HBM(v.shape, v.dtype) for v in lands] + [jax.ShapeDtypeStruct((SUBLANES, LANES), F32)]),
        in_specs=[HBM] * (2 * U),
        out_specs=tuple([SEM] * (2 * G) + [HBM] * (2 * U) + [pl.BlockSpec(memory_space=pltpu.VMEM)]),
        input_output_aliases={i: 2 * G + i for i in range(2 * U)},
        compiler_params=pltpu.CompilerParams(has_side_effects=DATAFLOW),
    )(*[pltpu.with_memory_space_constraint(a, pltpu.HBM) for a in list(srcs) + lands])
    return outs[:G], outs[G:2 * G], outs[2 * G:2 * G + U], outs[2 * G + U:2 * G + 2 * U]


def _gather_wait(srcs, lands, by_cols, send_sems, recv_sems, after, name):
    n = len(srcs)

    def body(*refs):
        src = refs[:n]
        land = refs[n:2 * n]
        send_ref, recv_ref = refs[2 * n:2 * n + 2]
        _, sibling, _, _ = _other_chips()
        for pos in range(n):
            for j in range(N_GATHER_COPIES):
                cp = _gather_copy(src[pos], land[pos], by_cols[pos], send_ref, recv_ref, pos, j, 0, sibling)
                cp.wait_send()
                cp.wait_recv()

    outs = pl.pallas_call(
        body, name=name,
        out_shape=tuple([pltpu.HBM(s.shape, s.dtype) for s in srcs] + [pltpu.HBM(v.shape, v.dtype) for v in lands]),
        in_specs=[HBM] * (2 * n) + [SEM, SEM, pl.BlockSpec(memory_space=pl.ANY)],
        out_specs=tuple([HBM] * (2 * n)),
        input_output_aliases={i: i for i in range(2 * n)},
        compiler_params=pltpu.CompilerParams(has_side_effects=DATAFLOW),
    )(*srcs, *lands, send_sems, recv_sems, after)
    return outs[n:]


def _scatter_grads(csums, by_cols, name):
    n = len(csums)
    shard = [(s.shape[0], s.shape[1] // N_CHIPS) if bc else s.shape[1:] for s, bc in zip(csums, by_cols)]

    def body(*refs):
        src = refs[:n]
        rem = refs[n:2 * n]
        sib = refs[2 * n:3 * n]
        send_sems, recv_sems = refs[3 * n:]
        c = lax.axis_index("c")
        me, sibling, chips, chip_ids = _other_chips()

        def remote(i, k, src_ref, dst_ref, to):
            return pltpu.make_async_remote_copy(src_ref=src_ref, dst_ref=dst_ref, send_sem=send_sems.at[i, k],
                                                recv_sem=recv_sems.at[i, k], device_id=to, device_id_type=MESH)

        def part(i, k):
            return _chip_slab(src[i], k, shard[i][-1], by_cols[i])

        started = []
        for i in range(n):
            for j in range(3):
                started.append(remote(i, j, part(i, chip_ids[j]), rem[i].at[j, c], (*chips[j], c)))
            started.append(remote(i, 6, part(i, me), sib[i], sibling))
        for cp in started:
            cp.start()
        for i in range(n):
            for j in range(3):
                slot = rem[i].at[j, c]
                remote(i, j, slot, slot, sibling).wait_recv()
                fwd = remote(i, 3 + j, slot, slot, sibling)
                fwd.start()
                started.append(fwd)
        for i in range(n):
            for j in range(3):
                slot = rem[i].at[j, 1 - c]
                remote(i, 3 + j, slot, slot, sibling).wait_recv()
            remote(i, 6, sib[i], sib[i], sibling).wait_recv()
        for cp in started:
            cp.wait_send()

    out_shape = ([jax.ShapeDtypeStruct((N_CHIPS - 1, 2) + tuple(sh), s.dtype) for s, sh in zip(csums, shard)]
                 + [jax.ShapeDtypeStruct(tuple(sh), s.dtype) for s, sh in zip(csums, shard)])
    outs = pl.pallas_call(
        body, name=name,
        in_specs=[ANY] * n, out_specs=[ANY] * (2 * n), out_shape=out_shape,
        scratch_shapes=[pltpu.SemaphoreType.DMA((n, 7)), pltpu.SemaphoreType.DMA((n, 7))],
    )(*csums)
    return outs[:n], outs[n:]


def _sibling_exchange(g0, g1, name):
    n = len(g0)

    def body(*refs):
        layers = (refs[:n], refs[n:2 * n])
        theirs = refs[2 * n:3 * n]
        send_sems, recv_sems = refs[3 * n:]
        x, y, c = _mesh_position()

        def remote(i, src_ref):
            return pltpu.make_async_remote_copy(src_ref=src_ref, dst_ref=theirs[i], send_sem=send_sems.at[i],
                                                recv_sem=recv_sems.at[i], device_id=(x, y, 1 - c), device_id_type=MESH)

        for keep in range(2):
            @pl.when(c == keep)
            def _():
                for i in range(n):
                    remote(i, layers[1 - keep][i]).start()

        for i in range(n):
            remote(i, layers[0][i]).wait()

    return pl.pallas_call(
        body, name=name,
        in_specs=[ANY] * (2 * n), out_specs=[ANY] * n, out_shape=[jax.ShapeDtypeStruct(g.shape, g.dtype) for g in g0],
        scratch_shapes=[pltpu.SemaphoreType.DMA((n,)), pltpu.SemaphoreType.DMA((n,))],
    )(*g0, *g1)


def _all_reduce_small(v, name):
    _, rows, _ = v.shape

    def body(v_ref, o_ref, recv_ref, send_sems, recv_sems):
        x, y, c = _mesh_position()
        me = 4 * x + 2 * y + c
        peers = []
        for d in range(1, N_DEV):
            px, py, pc = x ^ ((d >> 2) & 1), y ^ ((d >> 1) & 1), c ^ (d & 1)
            peers.append(((px, py, pc), 4 * px + 2 * py + pc))

        def remote(k, src_ref, dst_ref, to):
            return pltpu.make_async_remote_copy(src_ref=src_ref, dst_ref=dst_ref, send_sem=send_sems.at[k],
                                                recv_sem=recv_sems.at[k], device_id=to, device_id_type=MESH)

        scatter = [remote(d, v_ref.at[pid], recv_ref.at[me], to) for d, (to, pid) in enumerate(peers)]
        for cp in scatter:
            cp.start()
        recv_ref[pl.ds(me, 1)] = v_ref[pl.ds(me, 1)]
        for d, (to, pid) in enumerate(peers):
            remote(d, v_ref.at[pid], recv_ref.at[pid], to).wait_recv()
        total = recv_ref[0]
        for s in range(1, N_DEV):
            total = total + recv_ref[s]
        o_ref[pl.ds(me, 1)] = total[None]
        gather = [remote(N_DEV - 1 + d, o_ref.at[me], o_ref.at[me], to) for d, (to, pid) in enumerate(peers)]
        for cp in gather:
            cp.start()
        for d, (to, pid) in enumerate(peers):
            remote(N_DEV - 1 + d, o_ref.at[pid], o_ref.at[pid], to).wait_recv()
        for cp in scatter + gather:
            cp.wait_send()

    vm = pl.BlockSpec(memory_space=pltpu.VMEM)
    return pl.pallas_call(
        body, name=name,
        in_specs=[vm], out_specs=vm, out_shape=jax.ShapeDtypeStruct(v.shape, F32),
        scratch_shapes=[pltpu.VMEM(v.shape, F32), pltpu.SemaphoreType.DMA((2 * (N_DEV - 1),)),
                        pltpu.SemaphoreType.DMA((2 * (N_DEV - 1),))],
        compiler_params=_cparams(4 * _nbytes(v.shape, F32)),
    )(v)


SHARDED_MATS = ("ffn1_w_gu", "ffn1_w_down", "mix_w_in", "conv_w_proj", "rnn_w_proj", "mix_w_out", "ffn2_w_gu", "ffn2_w_down")
COL_SHARDED = ("ffn1_w_gu", "ffn2_w_gu", "conv_dw_w")
SHARDED_VECS = ("conv_dw_w", "rnn_conv_w")
WEIGHT_NAMES = ("ffn1_w_gu", "ffn1_w_down", "ln1_g", "ln1_b", "mix_w_in", "mix_b_in", "conv_dw_w", "conv_dw_b", "conv_gn_g",
                "conv_gn_b", "conv_w_proj", "rnn_conv_w", "rnn_conv_b", "rnn_w_a", "rnn_b_a", "rnn_w_x", "rnn_b_x",
                "rnn_lambda", "rnn_w_proj", "mix_w_out", "ln2_g", "ln2_b", "ffn2_w_gu", "ffn2_w_down", "ln3_g", "ln3_b")
SMALL_NAMES = tuple(n for n in WEIGHT_NAMES if n not in SHARDED_MATS)
SECTION_NAMES = ("cv", "cg", "rx", "rg", "gc", "gr")


def _unshard_cols(gathered):
    k4, K, n = gathered.shape
    return jnp.transpose(gathered, (1, 0, 2)).reshape(K, k4 * n)


def _row(v):
    return v.reshape(1, -1)


def _layer_forward(x0, p, alpha, l, hooks):
    t = f"l{l}_"
    sv = {"x0": x0}
    x1, sv["z1"], sv["hg1"], sv["hu1"] = _ffn_fwd(x0, p["wgu1"], p["wd1"], p["ln1_g"], p["ln1_b"], alpha, t + "ffn1_fwd")
    sv["x1"] = x1
    hooks.get("after_ffn1", lambda v: None)(x1)
    sec = dict(zip(SECTION_NAMES, _mix_in(x1, p["win"], p["bin"], p["sections"], t + "mix_in")))
    sv.update(sec)
    sv["c2"], c4 = _conv_branch_fwd(sec["cv"], sec["cg"], p["conv_dw_w"], p["conv_dw_b"], p["conv_gn_g"], p["conv_gn_b"], t + "conv_fwd")
    sv["c4"] = c4
    r1 = _short_conv_fwd(sec["rx"], p["rnn_conv_w"], p["rnn_conv_b"], t + "rconv_fwd")
    sv["r1"] = r1
    sv["ra"], sv["ri"], a, uu = _gates_fwd(r1, p["wa"], p["wx"], p["rnn_b_a"], p["rnn_b_x"], p["rnn_lambda"], t + "gates_fwd")
    sv["a"] = a
    h = _scan_fwd(a, uu, t + "scan_fwd")
    sv["h"] = h
    hooks.get("after_scan", lambda v: None)(h)
    sv["yc"], sv["yr"], sv["z2"], x2 = _mixer_out_fwd(c4, h, sec["rg"], sec["gc"], sec["gr"], x1, p["wc"], p["wr"], p["wo"],
                                                      p["ln2_g"], p["ln2_b"], alpha, t + "mixout_fwd")
    sv["x2"] = x2
    hooks.get("after_mixer", lambda v: None)(x2)
    x3, sv["z3"], sv["hg2"], sv["hu2"] = _ffn_fwd(x2, p["wgu2"], p["wd2"], p["ln3_g"], p["ln3_b"], alpha, t + "ffn2_fwd")
    hooks.get("after_layer", lambda v: None)(x3)
    return x3, sv


def _layer_backward(dy, p, sv, alpha, l):
    t = f"l{l}_"
    g = {}
    dx2, df, a_act, dhg, dhu, g["ln3_g"], g["ln3_b"] = _ffn_bwd(dy, sv["z3"], sv["hg2"], sv["hu2"], p["wgu2"], p["wd2"],
                                                                 p["ln3_g"], alpha, t + "ffn2_bwd")
    g["ffn2_w_down"] = _mm_tn(a_act, df, t + "dwd2")
    g["ffn2_w_gu"] = _mm_tn_pair(sv["x2"], dhg, dhu, t + "dwgu2")
    (dz2, dz2b, m_b, dyc, dyr, dgc, dgr, s_gc, s_gr, g["ln2_g"], g["ln2_b"]) = _mixer_out_bwd(
        dx2, sv["z2"], p["ln2_g"], p["wo"], sv["yc"], sv["yr"], sv["gc"], sv["gr"], t + "mixout_bwd")
    g["mix_w_out"] = _mm_tn(m_b, dz2b, t + "dwo")
    dc4, dh, drg, q_b, s_rg = _branch_bwd(dyc, dyr, p["wc"], p["wr"], sv["h"], sv["rg"], t + "branch_bwd")
    g["conv_w_proj"] = _mm_tn(sv["c4"], dyc, t + "dwc")
    g["rnn_w_proj"] = _mm_tn(q_b, dyr, t + "dwr")
    (dcv, dcg, g["conv_dw_w"], g["conv_dw_b"], g["conv_gn_g"], g["conv_gn_b"], s_cv, s_cg) = _conv_branch_bwd(
        dc4, sv["c2"], sv["cv"], sv["cg"], p["conv_dw_w"], p["conv_gn_g"], p["conv_gn_b"], t + "conv_bwd")
    guu, da = _scan_bwd(sv["a"], dh, sv["h"], t + "scan_bwd")
    dr1, dpa, dpx, g["rnn_b_a"], g["rnn_b_x"], g["rnn_lambda"] = _gates_bwd(
        guu, da, sv["ra"], sv["ri"], sv["r1"], p["wa"], p["wx"], p["rnn_lambda"], t + "gates_bwd")
    g["rnn_w_a"], g["rnn_w_x"] = _block_grads(sv["r1"], dpa, dpx, t + "dwax")
    drx, g["rnn_conv_w"], g["rnn_conv_b"], s_rx = _short_conv_bwd(dr1, sv["rx"], p["rnn_conv_w"], t + "rconv_bwd")
    du = {"cv": dcv, "cg": dcg, "rx": drx, "rg": drg, "gc": dgc, "gr": dgr}
    order = ("cv", "cg", "rx", "rg", "gc", "gr")
    g["mix_w_in"] = jnp.concatenate([_mm_tn(du[s], sv["x1"], t + "dwin_" + s) for s in order], axis=0)
    g["mix_b_in"] = jnp.concatenate([s_cv, s_cg, s_rx, s_rg, s_gc, s_gr], axis=1)
    dx1 = _mix_dx(dz2, [du[s] for s in order], p["win"], p["sections"], alpha, t + "mix_dx")
    dx0, df, a_act, dhg, dhu, g["ln1_g"], g["ln1_b"] = _ffn_bwd(dx1, sv["z1"], sv["hg1"], sv["hu1"], p["wgu1"], p["wd1"],
                                                                 p["ln1_g"], alpha, t + "ffn1_bwd")
    g["ffn1_w_down"] = _mm_tn(a_act, df, t + "dwd1")
    g["ffn1_w_gu"] = _mm_tn_pair(sv["x0"], dhg, dhu, t + "dwgu1")
    return dx0, g


def _pack_small(arrays, piece_rows):
    flat = jnp.concatenate([a.reshape(-1) for a in arrays])
    total = N_DEV * piece_rows * LANES
    return jnp.pad(flat, (0, total - flat.shape[0])).reshape(N_DEV, piece_rows, LANES)


def _unpack_small(packed, shapes):
    flat = packed.reshape(-1)
    out, off = [], 0
    for shp in shapes:
        n = 1
        for s in shp:
            n *= s
        out.append(flat[off:off + n].reshape(shp))
        off += n
    return out


def kernel(x, ffn1_w_gu, ffn1_w_down, ln1_g, ln1_b, mix_w_in, mix_b_in, conv_dw_w, conv_dw_b, conv_gn_g, conv_gn_b, conv_w_proj, rnn_conv_w, rnn_conv_b, rnn_w_a, rnn_b_a, rnn_w_x, rnn_b_x, rnn_lambda, rnn_w_proj, mix_w_out, ln2_g, ln2_b, ffn2_w_gu, ffn2_w_down, ln3_g, ln3_b, loss_target, m_ffn1_w_gu, m_ffn1_w_down, m_ln1_g, m_ln1_b, m_mix_w_in, m_mix_b_in, m_conv_dw_w, m_conv_dw_b, m_conv_gn_g, m_conv_gn_b, m_conv_w_proj, m_rnn_conv_w, m_rnn_conv_b, m_rnn_w_a, m_rnn_b_a, m_rnn_w_x, m_rnn_b_x, m_rnn_lambda, m_rnn_w_proj, m_mix_w_out, m_ln2_g, m_ln2_b, m_ffn2_w_gu, m_ffn2_w_down, m_ln3_g, m_ln3_b, v_ffn1_w_gu, v_ffn1_w_down, v_ln1_g, v_ln1_b, v_mix_w_in, v_mix_b_in, v_conv_dw_w, v_conv_dw_b, v_conv_gn_g, v_conv_gn_b, v_conv_w_proj, v_rnn_conv_w, v_rnn_conv_b, v_rnn_w_a, v_rnn_b_a, v_rnn_w_x, v_rnn_b_x, v_rnn_lambda, v_rnn_w_proj, v_mix_w_out, v_ln2_g, v_ln2_b, v_ffn2_w_gu, v_ffn2_w_down, v_ln3_g, v_ln3_b):
    args = locals()
    W = {n: args[n] for n in WEIGHT_NAMES}
    M = {n: args["m_" + n] for n in WEIGHT_NAMES}
    V = {n: args["v_" + n] for n in WEIGHT_NAMES}
    depth = ln1_g.shape[0]
    assert depth == 2, "each core of a chip moves one layer's weights and gradients"
    alpha = float((2 * depth) ** 0.25)
    S, D = x.shape[1], x.shape[2]
    F = ffn1_w_down.shape[1] * N_CHIPS
    R = rnn_w_proj.shape[1] * N_CHIPS
    chip = 2 * lax.axis_index("x") + lax.axis_index("y")

    for d in (W, M, V):
        d["mix_w_in"] = jnp.transpose(d["mix_w_in"], (0, 2, 1))

    names = SHARDED_MATS + SHARDED_VECS
    unit_groups = [[(0, "ffn1_w_gu"), (0, "ffn1_w_down")], [(0, "mix_w_in"), (0, "conv_dw_w"), (0, "rnn_conv_w")],
                   [(0, "conv_w_proj"), (0, "rnn_w_proj"), (0, "mix_w_out")], [(0, "ffn2_w_gu"), (0, "ffn2_w_down")],
                   [(1, n) for n in names]]
    order = [u for g in unit_groups for u in g]
    index = {u: i for i, u in enumerate(order)}
    groups = [[index[u] for u in g] for g in unit_groups]
    srcs = [W[n][l].astype(BF16) if n in SHARDED_MATS else W[n][l] for l, n in order]
    by_cols = [n in COL_SHARDED for _, n in order]
    send_sems, recv_sems, src_thru, land_thru = _gather_start(srcs, by_cols, groups, "gather_start")

    sections = ((0, D, F32), (D, D, F32), (2 * D, R, F32), (2 * D + R, R, BF16), (2 * D + 2 * R, D, BF16),
                (3 * D + 2 * R, D, BF16))
    keys = {"ffn1_w_gu": "wgu1", "ffn1_w_down": "wd1", "ffn2_w_gu": "wgu2", "ffn2_w_down": "wd2", "conv_w_proj": "wc",
            "rnn_w_proj": "wr", "mix_w_out": "wo"}
    params = []
    for l in range(depth):
        p = {"wa": _embed_blocks(rnn_w_a[l], f"l{l}_embed_wa"), "wx": _embed_blocks(rnn_w_x[l], f"l{l}_embed_wx")}
        for n in ("ln1_g", "ln1_b", "ln2_g", "ln2_b", "ln3_g", "ln3_b", "conv_dw_b", "conv_gn_g", "conv_gn_b", "rnn_conv_b",
                  "rnn_b_a", "rnn_b_x", "rnn_lambda"):
            p[n] = _row(W[n][l])
        p["bin"] = _row(mix_b_in[l])
        p["sections"] = sections
        params.append(p)

    def wait_group(g, after):
        ids = groups[g]
        landed = _gather_wait([src_thru[i] for i in ids], [land_thru[i] for i in ids], [by_cols[i] for i in ids],
                              send_sems[g], recv_sems[g], after, f"gather_wait{g}")
        for i, full in zip(ids, landed):
            l, n = order[i]
            p = params[l]
            if n not in COL_SHARDED:
                full = full.reshape((N_CHIPS * full.shape[1],) + full.shape[2:])
            if n == "mix_w_in":
                p["win"] = full
            elif n == "rnn_conv_w":
                p[n] = _unshard_cols(landed[ids.index(i)])
            elif n == "conv_dw_w":
                p[n] = full
            else:
                p[keys[n]] = (full[None], 0)

    h = x[0]
    wait_group(0, h)
    saved = []
    hooks = [{"after_ffn1": lambda v: wait_group(1, v), "after_scan": lambda v: wait_group(2, v),
              "after_mixer": lambda v: wait_group(3, v), "after_layer": lambda v: wait_group(4, v)}, {}]
    for l in range(depth):
        h, sv = _layer_forward(h, params[l], alpha, l, hooks[l])
        saved.append(sv)
    loss_part, dy = _loss_head(h, loss_target[0], "loss_head")
    loss = lax.psum(loss_part[0, 0], ("x", "y", "c"))
    grads = [None] * depth
    for l in reversed(range(depth)):
        dy, grads[l] = _layer_backward(dy, params[l], saved[l], alpha, l)
    grad_x = dy[None]

    g0 = [grads[0][n] for n in SHARDED_MATS]
    g1 = [grads[1][n] for n in SHARDED_MATS]
    theirs = _sibling_exchange(g0, g1, "pair_exchange")
    mat_cols = [n in COL_SHARDED for n in SHARDED_MATS]
    chip_sums = []
    for n, bc, a0, a1, b in zip(SHARDED_MATS, mat_cols, g0, g1, theirs):
        cs = _pair_sum_bf16(a0, a1, b, "pair_sum_" + n)
        chip_sums.append(cs if bc else cs.reshape((N_CHIPS, cs.shape[0] // N_CHIPS, cs.shape[1])))
    rem, sib = _scatter_grads(chip_sums, mat_cols, "scatter_grads")
    out_g, out_d, out_m, out_v = {}, {}, {}, {}
    for n, bc, cs, rm, sb in zip(SHARDED_MATS, mat_cols, chip_sums, rem, sib):
        if bc:
            width = cs.shape[1] // N_CHIPS
            own = lax.dynamic_slice_in_dim(cs, chip * width, width, axis=1)
        else:
            own = lax.dynamic_index_in_dim(cs, chip, axis=0, keepdims=False)
        out_g[n], out_d[n], out_m[n], out_v[n] = _adamw_sharded(W[n], M[n], V[n], own, sb, rm, "adamw_" + n)
    for d in (out_g, out_d, out_m, out_v):
        d["mix_w_in"] = jnp.transpose(d["mix_w_in"], (0, 2, 1))

    small_grads = [jnp.stack([grads[l][n].reshape(W[n].shape[1:] if n not in SHARDED_VECS else
                                                   (W[n].shape[1], W[n].shape[2] * N_CHIPS)) for l in range(depth)])
                   for n in SMALL_NAMES]
    n_small = sum(int(a.size) for a in small_grads)
    piece_rows = -(-n_small // (N_DEV * LANES * SUBLANES)) * SUBLANES
    reduced = _unpack_small(_all_reduce_small(_pack_small(small_grads, piece_rows), "all_reduce_small"),
                            [a.shape for a in small_grads])
    local_g = []
    for n, gr in zip(SMALL_NAMES, reduced):
        if n in SHARDED_VECS:
            width = W[n].shape[2]
            gr = lax.dynamic_slice_in_dim(gr, chip * width, width, axis=2)
        local_g.append(gr)
    n_local = sum(int(a.size) for a in local_g)
    flat_rows = -(-n_local // (N_DEV * LANES * SUBLANES)) * SUBLANES * N_DEV
    pack = lambda arrs: _pack_small(arrs, flat_rows // N_DEV).reshape(flat_rows, LANES)
    shapes = [a.shape for a in local_g]
    deltas, new_m, new_v = _adamw_flat(pack([W[n] for n in SMALL_NAMES]), pack([M[n] for n in SMALL_NAMES]),
                                       pack([V[n] for n in SMALL_NAMES]), pack(local_g), "adamw_small")
    for n, gr, d_, m_, v_ in zip(SMALL_NAMES, local_g, _unpack_small(deltas, shapes), _unpack_small(new_m, shapes),
                                 _unpack_small(new_v, shapes)):
        out_g[n], out_d[n], out_m[n], out_v[n] = gr, d_, m_, v_

    return (loss, grad_x, *[out_g[n] for n in WEIGHT_NAMES], *[out_d[n] for n in WEIGHT_NAMES],
            *[out_m[n] for n in WEIGHT_NAMES], *[out_v[n] for n in WEIGHT_NAMES])
```

```python
import functools

import jax
import jax.numpy as jnp
from jax import lax
from jax.experimental import pallas as pl
from jax.experimental.pallas import tpu as pltpu

F32 = jnp.float32
BF16 = jnp.bfloat16
MESH = pl.DeviceIdType.MESH

LN_EPS = 1e-5
CONV_GROUPS = 8
RNN_BLOCKS = 16
RG_LRU_C = 8.0
ADAM_LR = 0.001
ADAM_B1 = 0.9
ADAM_B2 = 0.999
ADAM_EPS = 1e-08
ADAM_WD = 0.01
ADAM_STEP = 10

LANES = 128
SUBLANES = 8
V7X_VMEM_BYTES = 64 << 20
VMEM_LIMIT_CAP = V7X_VMEM_BYTES - (6 << 20)
N_CHIPS = 4
N_DEV = 8
CONV_ROWS = 64
EW_ROWS = 1024
SCAN_SEGMENTS = 32
SCAN_UNROLL = 4


def _cparams(block_bytes):
    limit = min(VMEM_LIMIT_CAP, max(int(block_bytes) + (8 << 20), 24 << 20))
    return pltpu.CompilerParams(vmem_limit_bytes=limit)


def _nbytes(shape, dtype):
    n = 1
    for s in shape:
        n *= s
    return n * jnp.dtype(dtype).itemsize


def _divisor_tile(n, limit, quantum):
    if n <= limit:
        return n
    best = None
    for t in range(quantum, limit + 1, quantum):
        if n % t == 0:
            best = t
    assert best is not None, (n, limit, quantum)
    return best


def _bs(shape, imap, **kw):
    return pl.BlockSpec(shape, imap, **kw)


def _resident(shape):
    nd = len(shape)
    return pl.BlockSpec(shape, lambda *_: (0,) * nd, pipeline_mode=pl.Buffered(1))


def _streamed_call(body, **kw):
    call = pl.pallas_call(body, **kw)
    return lambda *operands: call(*[pltpu.with_memory_space_constraint(o, pltpu.HBM) for o in operands])


def _layer_block(w, block, imap, **kw):
    arr, layer = w
    return arr, pl.BlockSpec((None,) + block, lambda *ids: (layer,) + imap(*ids), **kw)


def _layer_resident(w):
    arr, _ = w
    return _layer_block(w, arr.shape[1:], lambda *_: (0, 0), pipeline_mode=pl.Buffered(1))


def _sigmoid(x):
    return jax.nn.sigmoid(x)


def _dot(a, b):
    return jnp.dot(a, b, preferred_element_type=F32)


def _dot_nt(a, b):
    return lax.dot_general(a, b, (((1,), (1,)), ((), ())), preferred_element_type=F32)


def _dot_tn(a, b):
    return lax.dot_general(a, b, (((0,), (0,)), ((), ())), preferred_element_type=F32)


def _row_mean(z):
    return jnp.mean(z, axis=-1, keepdims=True)


def _lane_mean(z):
    hi = z.astype(BF16)
    lo = (z - hi.astype(F32)).astype(BF16)
    ones = jnp.full((2 * LANES, LANES), 1.0 / LANES, BF16)
    return jnp.dot(jnp.concatenate([hi, lo], axis=-1), ones, preferred_element_type=F32)


def _norm_fwd(z, g, b, mean=_row_mean):
    mu = mean(z)
    xc = z - mu
    var = mean(xc * xc)
    return xc * lax.rsqrt(var + LN_EPS) * g + b


def _norm_bwd(z, g, dy, mean=_row_mean):
    mu = mean(z)
    xc = z - mu
    var = mean(xc * xc)
    rstd = lax.rsqrt(var + LN_EPS)
    xhat = xc * rstd
    dxh = dy * g
    m1 = mean(dxh)
    m2 = mean(dxh * xhat)
    return rstd * (dxh - m1 - xhat * m2), xhat


GELU_K = 0.7978845608028654
GELU_C = 0.044715


def _gelu(x):
    return 0.5 * x * (1.0 + jnp.tanh(GELU_K * (x + GELU_C * x * x * x)))


def _gelu_grad(x):
    t = jnp.tanh(GELU_K * (x + GELU_C * x * x * x))
    return 0.5 * (1.0 + t) + 0.5 * x * (1.0 - t * t) * GELU_K * (1.0 + 3.0 * GELU_C * x * x)


def _softplus(y):
    return jnp.maximum(y, 0.0) + jnp.log1p(jnp.exp(-jnp.abs(y)))


def _neg_expm1(y):
    series = -y * (1.0 + y * (0.5 + y * (1.0 / 6.0 + y * (1.0 / 24.0 + y * (1.0 / 120.0 + y * (1.0 / 720.0))))))
    return jnp.where(y > -0.25, series, 1.0 - jnp.exp(y))


def _colsum(x):
    return jnp.sum(x, axis=0, keepdims=True)


def _shifted_taps(src_ref, base, rows, taps):
    acc = None
    for o, coef in taps:
        term = coef() * src_ref[pl.ds(base + o, rows), :]
        acc = term if acc is None else acc + term
    return acc


def _shifted_corr(src_ref, base, rows, d, acc_ref, offs):
    for k, o in enumerate(offs):
        prod = d * src_ref[pl.ds(base + o, rows), :]
        part = jnp.sum(prod.reshape(rows // SUBLANES, SUBLANES, prod.shape[-1]), axis=0)
        acc_ref[SUBLANES * k:SUBLANES * (k + 1), :] += part


def _front_pad(ktaps):
    return SUBLANES * ((ktaps - 1 + SUBLANES - 1) // SUBLANES)


def _pad_rows(ktaps):
    return _front_pad(ktaps) + SUBLANES


def _ffn_tiles(S, F):
    tm = _divisor_tile(S, 1024, 16)
    tf = _divisor_tile(F, 256, LANES)
    return tm, tf


def _ffn_fwd(x, wgu, wd, g, b, alpha, name):
    S, D = x.shape
    F = wd[0].shape[1]
    tm, tf = _ffn_tiles(S, F)
    nf = F // tf
    wg_arr, wg_spec = _layer_block(wgu, (D, tf), lambda i, j: (0, j))
    wu_arr, wu_spec = _layer_block(wgu, (D, tf), lambda i, j: (0, nf + j))
    wd_arr, wd_spec = _layer_block(wd, (tf, D), lambda i, j: (j, 0))

    def body(x_ref, wg_ref, wu_ref, wd_ref, g_ref, b_ref, y_ref, z_ref, hg_ref, hu_ref, acc_ref, xb_ref):
        j = pl.program_id(1)

        @pl.when(j == 0)
        def _():
            xb_ref[...] = x_ref[...].astype(BF16)
            acc_ref[...] = jnp.zeros_like(acc_ref)

        xb = xb_ref[...]
        hg = _dot(xb, wg_ref[...])
        hu = _dot(xb, wu_ref[...])
        hg_ref[...] = hg
        hu_ref[...] = hu
        a = (hg * _sigmoid(hg) * hu).astype(BF16)
        acc_ref[...] += _dot(a, wd_ref[...])

        @pl.when(j == nf - 1)
        def _():
            z = alpha * x_ref[...] + 0.5 * acc_ref[...]
            z_ref[...] = z
            y_ref[...] = _norm_fwd(z, g_ref[...], b_ref[...])

    blk = 2 * (3 * tm * D * 4 + 2 * tm * tf * 4 + 3 * D * tf * 2) + tm * D * 6 + 3 * tm * tf * 4
    return _streamed_call(
        body, name=name, grid=(S // tm, nf),
        in_specs=[_bs((tm, D), lambda i, j: (i, 0)), wg_spec, wu_spec, wd_spec,
                  _bs((1, D), lambda i, j: (0, 0)), _bs((1, D), lambda i, j: (0, 0))],
        out_specs=[_bs((tm, D), lambda i, j: (i, 0)), _bs((tm, D), lambda i, j: (i, 0)),
                   _bs((tm, tf), lambda i, j: (i, j)), _bs((tm, tf), lambda i, j: (i, j))],
        out_shape=[jax.ShapeDtypeStruct((S, D), F32), jax.ShapeDtypeStruct((S, D), F32),
                   jax.ShapeDtypeStruct((S, F), F32), jax.ShapeDtypeStruct((S, F), F32)],
        scratch_shapes=[pltpu.VMEM((tm, D), F32), pltpu.VMEM((tm, D), BF16)],
        compiler_params=_cparams(blk),
    )(x, wg_arr, wu_arr, wd_arr, g, b)


def _ffn_bwd(dy, z, hg, hu, wgu, wd, g, alpha, name):
    S, D = dy.shape
    F = wd[0].shape[1]
    tm, tf = _ffn_tiles(S, F)
    nf = F // tf
    wg_arr, wg_spec = _layer_block(wgu, (D, tf), lambda i, j: (0, j))
    wu_arr, wu_spec = _layer_block(wgu, (D, tf), lambda i, j: (0, nf + j))
    wd_arr, wd_spec = _layer_block(wd, (tf, D), lambda i, j: (j, 0))

    def body(dy_ref, z_ref, hg_ref, hu_ref, wg_ref, wu_ref, wd_ref, g_ref,
             dx_ref, df_ref, a_ref, dhg_ref, dhu_ref, dg_ref, db_ref, acc_ref):
        i = pl.program_id(0)
        j = pl.program_id(1)

        @pl.when((i == 0) & (j == 0))
        def _():
            dg_ref[...] = jnp.zeros_like(dg_ref)
            db_ref[...] = jnp.zeros_like(db_ref)

        @pl.when(j == 0)
        def _():
            dy_ = dy_ref[...]
            dz, xhat = _norm_bwd(z_ref[...], g_ref[...], dy_)
            dg_ref[...] += _colsum(dy_ * xhat)
            db_ref[...] += _colsum(dy_)
            acc_ref[...] = alpha * dz
            df_ref[...] = (0.5 * dz).astype(BF16)

        da = _dot_nt(df_ref[...], wd_ref[...])
        hg_ = hg_ref[...]
        hu_ = hu_ref[...]
        s = _sigmoid(hg_)
        sl = hg_ * s
        dgate = (da * hu_ * (s * (1.0 + hg_ * (1.0 - s)))).astype(BF16)
        dup = (da * sl).astype(BF16)
        a_ref[...] = (sl * hu_).astype(BF16)
        dhg_ref[...] = dgate
        dhu_ref[...] = dup
        acc_ref[...] += _dot_nt(dgate, wg_ref[...]) + _dot_nt(dup, wu_ref[...])

        @pl.when(j == nf - 1)
        def _():
            dx_ref[...] = acc_ref[...]

    blk = 2 * (2 * tm * D * 4 + tm * D * 2 + 2 * tm * tf * 4 + 3 * tm * tf * 2 + 3 * D * tf * 2) + 3 * tm * D * 4 + 8 * tm * tf * 4
    once = dict(pipeline_mode=pl.Buffered(1))
    return _streamed_call(
        body, name=name, grid=(S // tm, nf),
        in_specs=[_bs((tm, D), lambda i, j: (i, 0), **once), _bs((tm, D), lambda i, j: (i, 0), **once),
                  _bs((tm, tf), lambda i, j: (i, j)), _bs((tm, tf), lambda i, j: (i, j)),
                  wg_spec, wu_spec, wd_spec, _bs((1, D), lambda i, j: (0, 0))],
        out_specs=[_bs((tm, D), lambda i, j: (i, 0)), _bs((tm, D), lambda i, j: (i, 0)),
                   _bs((tm, tf), lambda i, j: (i, j)), _bs((tm, tf), lambda i, j: (i, j)), _bs((tm, tf), lambda i, j: (i, j)),
                   _bs((1, D), lambda i, j: (0, 0)), _bs((1, D), lambda i, j: (0, 0))],
        out_shape=[jax.ShapeDtypeStruct((S, D), F32), jax.ShapeDtypeStruct((S, D), BF16),
                   jax.ShapeDtypeStruct((S, F), BF16), jax.ShapeDtypeStruct((S, F), BF16), jax.ShapeDtypeStruct((S, F), BF16),
                   jax.ShapeDtypeStruct((1, D), F32), jax.ShapeDtypeStruct((1, D), F32)],
        scratch_shapes=[pltpu.VMEM((tm, D), F32)],
        compiler_params=_cparams(blk),
    )(dy, z, hg, hu, wg_arr, wu_arr, wd_arr, g)


def _mm_tn(a, b, name):
    S, M = a.shape
    N = b.shape[1]
    bm = _divisor_tile(M, 1408, LANES)
    bn = _divisor_tile(N, 1408, LANES)
    tk = _divisor_tile(S, 512, 16)

    def body(a_ref, b_ref, o_ref):
        @pl.when(pl.program_id(2) == 0)
        def _():
            o_ref[...] = jnp.zeros_like(o_ref)

        o_ref[...] += _dot_tn(a_ref[...].astype(BF16), b_ref[...].astype(BF16))

    blk = 2 * (tk * bm * a.dtype.itemsize + tk * bn * b.dtype.itemsize + bm * bn * 4) + tk * bm * 4 + bm * bn * 4
    return _streamed_call(
        body, name=name, grid=(M // bm, N // bn, S // tk),
        in_specs=[_bs((tk, bm), lambda i, j, k: (k, i)), _bs((tk, bn), lambda i, j, k: (k, j))],
        out_specs=_bs((bm, bn), lambda i, j, k: (i, j)),
        out_shape=jax.ShapeDtypeStruct((M, N), F32),
        compiler_params=_cparams(blk),
    )(a, b)


def _mm_tn_pair(a, b0, b1, name):
    S, M = a.shape
    N = b0.shape[1]
    assert b1.shape == b0.shape
    bm = _divisor_tile(M, 1408, LANES)
    bn = _divisor_tile(N, 1408, LANES)
    tk = _divisor_tile(S, 512, 16)
    nb = N // bn

    def body(a_ref, b0_ref, b1_ref, o_ref):
        j = pl.program_id(1)

        @pl.when(pl.program_id(2) == 0)
        def _():
            o_ref[...] = jnp.zeros_like(o_ref)

        ab = a_ref[...].astype(BF16)

        @pl.when(j < nb)
        def _():
            o_ref[...] += _dot_tn(ab, b0_ref[...])

        @pl.when(j >= nb)
        def _():
            o_ref[...] += _dot_tn(ab, b1_ref[...])

    b0_map = lambda i, j, k: (jnp.where(j < nb, k, S // tk - 1), jnp.minimum(j, nb - 1))
    b1_map = lambda i, j, k: (jnp.where(j >= nb, k, 0), jnp.maximum(j - nb, 0))
    blk = 2 * (tk * bm * a.dtype.itemsize + 2 * tk * bn * 2 + bm * bn * 4) + tk * bm * 4 + bm * bn * 4
    return _streamed_call(
        body, name=name, grid=(M // bm, 2 * nb, S // tk),
        in_specs=[_bs((tk, bm), lambda i, j, k: (k, i)), _bs((tk, bn), b0_map), _bs((tk, bn), b1_map)],
        out_specs=_bs((bm, bn), lambda i, j, k: (i, j)),
        out_shape=jax.ShapeDtypeStruct((M, 2 * N), F32),
        compiler_params=_cparams(blk),
    )(a, b0, b1)


def _mix_in(x, wt, bias, sections, name):
    S, D = x.shape
    tm = _divisor_tile(S, 256, 16)
    n = len(sections)

    def body(x_ref, w_ref, b_ref, *o_refs):
        xb = x_ref[...].astype(BF16)
        for (off, width, dtype), o_ref in zip(sections, o_refs):
            o_ref[...] = (_dot_nt(xb, w_ref[off:off + width, :]) + b_ref[:, off:off + width]).astype(dtype)

    total = wt.shape[0]
    blk = 2 * (tm * D * 4 + sum(tm * w * jnp.dtype(dt).itemsize for _, w, dt in sections)) + total * D * 2 + 3 * tm * D * 4
    return _streamed_call(
        body, name=name, grid=(S // tm,),
        in_specs=[_bs((tm, D), lambda i: (i, 0)), _resident((total, D)), _resident((1, total))],
        out_specs=[_bs((tm, w), lambda i: (i, 0)) for _, w, _ in sections],
        out_shape=[jax.ShapeDtypeStruct((S, w), dt) for _, w, dt in sections],
        compiler_params=_cparams(blk),
    )(x, wt, bias)


def _mix_dx(dz, parts, wt, sections, alpha, name):
    S, D = dz.shape
    tm = _divisor_tile(S, 256, 16)
    n = len(parts)

    def body(*refs):
        dz_ref = refs[0]
        p_refs = refs[1:1 + n]
        w_ref = refs[1 + n]
        o_ref = refs[2 + n]
        acc = alpha * dz_ref[...]
        for p_ref, (off, width, _) in zip(p_refs, sections):
            acc = acc + _dot(p_ref[...], w_ref[off:off + width, :])
        o_ref[...] = acc

    widths = [p.shape[1] for p in parts]
    total = wt.shape[0]
    blk = 2 * (2 * tm * D * 4 + sum(tm * w * 2 for w in widths)) + total * D * 2 + 2 * tm * D * 4
    return _streamed_call(
        body, name=name, grid=(S // tm,),
        in_specs=[_bs((tm, D), lambda i: (i, 0))] + [_bs((tm, w), lambda i: (i, 0)) for w in widths]
                 + [_resident((total, D))],
        out_specs=_bs((tm, D), lambda i: (i, 0)),
        out_shape=jax.ShapeDtypeStruct((S, D), F32),
        compiler_params=_cparams(blk),
    )(dz, *parts, wt)


def _conv_branch_fwd(cv, cg, w, b, gg, gb, name):
    S, C = cv.shape
    K = w.shape[0]
    assert C // CONV_GROUPS == LANES
    padf = _front_pad(K)
    R = min(CONV_ROWS, S)
    E = min(EW_ROWS, S)

    def body(cv_ref, cg_ref, w_ref, b_ref, gg_ref, gb_ref, c2_ref, c4_ref, pad_ref):
        pad_ref[0:padf, :] = jnp.zeros((padf, LANES), F32)
        pad_ref[S + padf:S + padf + SUBLANES, :] = jnp.zeros((SUBLANES, LANES), F32)

        def fill(i, carry):
            r = pl.multiple_of(i * E, E)
            pad_ref[pl.ds(r + padf, E), :] = cv_ref[pl.ds(r, E), :] * _sigmoid(cg_ref[pl.ds(r, E), :])
            return carry

        lax.fori_loop(0, S // E, fill, 0)
        taps = [(padf - (K - 1) + k, functools.partial(lambda k: w_ref[k:k + 1, :], k)) for k in range(K)]

        def conv(i, carry):
            r = pl.multiple_of(i * R, R)
            c2_ref[pl.ds(r, R), :] = _shifted_taps(pad_ref, r, R, taps) + b_ref[...]
            return carry

        lax.fori_loop(0, S // R, conv, 0)

        def norm(i, carry):
            r = pl.multiple_of(i * E, E)
            c3 = _norm_fwd(c2_ref[pl.ds(r, E), :], gg_ref[...], gb_ref[...], _lane_mean)
            c4_ref[pl.ds(r, E), :] = (c3 * _sigmoid(c3)).astype(BF16)
            return carry

        lax.fori_loop(0, S // E, norm, 0)

    col = lambda i: (0, i)
    blk = 2 * (3 * S * LANES * 4 + S * LANES * 2) + (S + _pad_rows(K)) * LANES * 4
    return _streamed_call(
        body, name=name, grid=(C // LANES,),
        in_specs=[_bs((S, LANES), col), _bs((S, LANES), col), _bs((K, LANES), col),
                  _bs((1, LANES), col), _bs((1, LANES), col), _bs((1, LANES), col)],
        out_specs=[_bs((S, LANES), col), _bs((S, LANES), col)],
        out_shape=[jax.ShapeDtypeStruct((S, C), F32), jax.ShapeDtypeStruct((S, C), BF16)],
        scratch_shapes=[pltpu.VMEM((S + _pad_rows(K), LANES), F32)],
        compiler_params=_cparams(blk),
    )(cv, cg, w, b, gg, gb)


def _conv_branch_bwd(dc4, c2, cv, cg, w, gg, gb, name):
    S, C = cv.shape
    K = w.shape[0]
    padf = _front_pad(K)
    R = min(CONV_ROWS, S)
    E = min(EW_ROWS, S)

    def body(dc4_ref, c2_ref, cv_ref, cg_ref, w_ref, gg_ref, gb_ref,
             dcv_ref, dcg_ref, dw_ref, dwb_ref, dgg_ref, dgb_ref, scv_ref, scg_ref,
             dpad_ref, cpad_ref, dwacc_ref):
        cpad_ref[0:padf, :] = jnp.zeros((padf, LANES), F32)
        cpad_ref[S + padf:S + padf + SUBLANES, :] = jnp.zeros((SUBLANES, LANES), F32)
        dpad_ref[S:S + padf + SUBLANES, :] = jnp.zeros((padf + SUBLANES, LANES), F32)
        dwacc_ref[...] = jnp.zeros_like(dwacc_ref)
        for ref in (dwb_ref, dgg_ref, dgb_ref, scv_ref, scg_ref):
            ref[...] = jnp.zeros_like(ref)

        def norm_pass(i, carry):
            r = pl.multiple_of(i * E, E)
            g_ = gg_ref[...]
            c2 = c2_ref[pl.ds(r, E), :]
            xc = c2 - _lane_mean(c2)
            rstd = lax.rsqrt(_lane_mean(xc * xc) + LN_EPS)
            xhat = xc * rstd
            c3 = xhat * g_ + gb_ref[...]
            s = _sigmoid(c3)
            dc3 = dc4_ref[pl.ds(r, E), :].astype(F32) * (s * (1.0 + c3 * (1.0 - s)))
            dgg_ref[...] += _colsum(dc3 * xhat)
            dgb_ref[...] += _colsum(dc3)
            dxh = dc3 * g_
            dc2 = rstd * (dxh - _lane_mean(dxh) - xhat * _lane_mean(dxh * xhat))
            dpad_ref[pl.ds(r, E), :] = dc2
            dwb_ref[...] += _colsum(dc2)
            cpad_ref[pl.ds(r + padf, E), :] = cv_ref[pl.ds(r, E), :] * _sigmoid(cg_ref[pl.ds(r, E), :])
            return carry

        lax.fori_loop(0, S // E, norm_pass, 0)
        taps = [(K - 1 - k, functools.partial(lambda k: w_ref[k:k + 1, :], k)) for k in range(K)]
        offs = [padf - (K - 1) + k for k in range(K)]

        def conv_pass(i, carry):
            r = pl.multiple_of(i * R, R)
            dc1 = _shifted_taps(dpad_ref, r, R, taps)
            sg = _sigmoid(cg_ref[pl.ds(r, R), :])
            cv_ = cv_ref[pl.ds(r, R), :]
            dcv = dc1 * sg
            dcg = dc1 * cv_ * sg * (1.0 - sg)
            dcv_ref[pl.ds(r, R), :] = dcv.astype(BF16)
            dcg_ref[pl.ds(r, R), :] = dcg.astype(BF16)
            scv_ref[...] += _colsum(dcv)
            scg_ref[...] += _colsum(dcg)
            _shifted_corr(cpad_ref, r, R, dpad_ref[pl.ds(r, R), :], dwacc_ref, offs)
            return carry

        lax.fori_loop(0, S // R, conv_pass, 0)
        for k in range(K):
            dw_ref[k:k + 1, :] = _colsum(dwacc_ref[SUBLANES * k:SUBLANES * (k + 1), :])

    col = lambda i: (0, i)
    row = jax.ShapeDtypeStruct((1, C), F32)
    blk = 2 * (4 * S * LANES * 4 + 2 * S * LANES * 2) + 2 * (S + _pad_rows(K)) * LANES * 4
    return _streamed_call(
        body, name=name, grid=(C // LANES,),
        in_specs=[_bs((S, LANES), col)] * 4 + [_bs((K, LANES), col), _bs((1, LANES), col), _bs((1, LANES), col)],
        out_specs=[_bs((S, LANES), col), _bs((S, LANES), col), _bs((K, LANES), col)] + [_bs((1, LANES), col)] * 5,
        out_shape=[jax.ShapeDtypeStruct((S, C), BF16), jax.ShapeDtypeStruct((S, C), BF16),
                   jax.ShapeDtypeStruct((K, C), F32), row, row, row, row, row],
        scratch_shapes=[pltpu.VMEM((S + _pad_rows(K), LANES), F32), pltpu.VMEM((S + _pad_rows(K), LANES), F32),
                        pltpu.VMEM((SUBLANES * K, LANES), F32)],
        compiler_params=_cparams(blk),
    )(dc4, c2, cv, cg, w, gg, gb)


def _short_conv_fwd(xin, w, b, name):
    S, C = xin.shape
    K = w.shape[0]
    padf = _front_pad(K)
    R = min(CONV_ROWS, S)
    E = min(EW_ROWS, S)

    def body(x_ref, w_ref, b_ref, o_ref, pad_ref):
        pad_ref[0:padf, :] = jnp.zeros((padf, LANES), F32)
        pad_ref[S + padf:S + padf + SUBLANES, :] = jnp.zeros((SUBLANES, LANES), F32)

        def fill(i, carry):
            r = pl.multiple_of(i * E, E)
            pad_ref[pl.ds(r + padf, E), :] = x_ref[pl.ds(r, E), :]
            return carry

        lax.fori_loop(0, S // E, fill, 0)
        taps = [(padf - (K - 1) + k, functools.partial(lambda k: w_ref[k:k + 1, :], k)) for k in range(K)]

        def conv(i, carry):
            r = pl.multiple_of(i * R, R)
            o_ref[pl.ds(r, R), :] = _shifted_taps(pad_ref, r, R, taps) + b_ref[...]
            return carry

        lax.fori_loop(0, S // R, conv, 0)

    col = lambda i: (0, i)
    blk = 2 * (2 * S * LANES * 4) + (S + _pad_rows(K)) * LANES * 4
    return _streamed_call(
        body, name=name, grid=(C // LANES,),
        in_specs=[_bs((S, LANES), col), _bs((K, LANES), col), _bs((1, LANES), col)],
        out_specs=_bs((S, LANES), col),
        out_shape=jax.ShapeDtypeStruct((S, C), F32),
        scratch_shapes=[pltpu.VMEM((S + _pad_rows(K), LANES), F32)],
        compiler_params=_cparams(blk),
    )(xin, w, b)


def _short_conv_bwd(dy, xin, w, name):
    S, C = xin.shape
    K = w.shape[0]
    padf = _front_pad(K)
    R = min(CONV_ROWS, S)
    E = min(EW_ROWS, S)

    def body(dy_ref, x_ref, w_ref, dx_ref, dw_ref, db_ref, sx_ref, dpad_ref, xpad_ref, dwacc_ref):
        xpad_ref[0:padf, :] = jnp.zeros((padf, LANES), F32)
        xpad_ref[S + padf:S + padf + SUBLANES, :] = jnp.zeros((SUBLANES, LANES), F32)
        dpad_ref[S:S + padf + SUBLANES, :] = jnp.zeros((padf + SUBLANES, LANES), F32)
        dwacc_ref[...] = jnp.zeros_like(dwacc_ref)
        db_ref[...] = jnp.zeros_like(db_ref)
        sx_ref[...] = jnp.zeros_like(sx_ref)

        def fill(i, carry):
            r = pl.multiple_of(i * E, E)
            d = dy_ref[pl.ds(r, E), :]
            dpad_ref[pl.ds(r, E), :] = d
            db_ref[...] += _colsum(d)
            xpad_ref[pl.ds(r + padf, E), :] = x_ref[pl.ds(r, E), :]
            return carry

        lax.fori_loop(0, S // E, fill, 0)
        taps = [(K - 1 - k, functools.partial(lambda k: w_ref[k:k + 1, :], k)) for k in range(K)]
        offs = [padf - (K - 1) + k for k in range(K)]

        def conv_pass(i, carry):
            r = pl.multiple_of(i * R, R)
            dx = _shifted_taps(dpad_ref, r, R, taps)
            dx_ref[pl.ds(r, R), :] = dx.astype(BF16)
            sx_ref[...] += _colsum(dx)
            _shifted_corr(xpad_ref, r, R, dpad_ref[pl.ds(r, R), :], dwacc_ref, offs)
            return carry

        lax.fori_loop(0, S // R, conv_pass, 0)
        for k in range(K):
            dw_ref[k:k + 1, :] = _colsum(dwacc_ref[SUBLANES * k:SUBLANES * (k + 1), :])

    col = lambda i: (0, i)
    row = jax.ShapeDtypeStruct((1, C), F32)
    blk = 2 * (2 * S * LANES * 4 + S * LANES * 2) + 2 * (S + _pad_rows(K)) * LANES * 4
    return _streamed_call(
        body, name=name, grid=(C // LANES,),
        in_specs=[_bs((S, LANES), col), _bs((S, LANES), col), _bs((K, LANES), col)],
        out_specs=[_bs((S, LANES), col), _bs((K, LANES), col), _bs((1, LANES), col), _bs((1, LANES), col)],
        out_shape=[jax.ShapeDtypeStruct((S, C), BF16), jax.ShapeDtypeStruct((K, C), F32), row, row],
        scratch_shapes=[pltpu.VMEM((S + _pad_rows(K), LANES), F32), pltpu.VMEM((S + _pad_rows(K), LANES), F32),
                        pltpu.VMEM((SUBLANES * K, LANES), F32)],
        compiler_params=_cparams(blk),
    )(dy, xin, w)


def _band_panels(width, block):
    assert width % LANES == 0 and block <= LANES
    panels = []
    for c0 in range(0, width, 2 * LANES):
        c1 = min(width, c0 + 2 * LANES)
        r0 = (c0 // block) * block // LANES * LANES
        r1 = min(width, -(-(-(-c1 // block) * block) // LANES) * LANES)
        panels.append((r0, r1, c0, c1))
    return panels


def _gates_fwd(r1, wa, wx, ba, bx, lam, name):
    S, R = r1.shape
    tm = _divisor_tile(S, 256, 16)
    panels = _band_panels(R, R // RNN_BLOCKS)

    def body(r1_ref, wa_ref, wx_ref, ba_ref, bx_ref, lam_ref, ra_ref, ri_ref, a_ref, uu_ref):
        for r0, r1e, c0, c1 in panels:
            rb = r1_ref[:, r0:r1e].astype(BF16)
            ra = _sigmoid(_dot(rb, wa_ref[r0:r1e, c0:c1]) + ba_ref[:, c0:c1])
            ri = _sigmoid(_dot(rb, wx_ref[r0:r1e, c0:c1]) + bx_ref[:, c0:c1])
            log_a = -RG_LRU_C * ra * _softplus(-lam_ref[:, c0:c1])
            ra_ref[:, c0:c1] = ra
            ri_ref[:, c0:c1] = ri
            a_ref[:, c0:c1] = jnp.exp(log_a)
            uu_ref[:, c0:c1] = jnp.sqrt(_neg_expm1(2.0 * log_a)) * (ri * r1_ref[:, c0:c1])

    blk = 2 * (5 * tm * R * 4) + 2 * R * R * 2 + 6 * tm * R * 4
    tile = _bs((tm, R), lambda i: (i, 0))
    return _streamed_call(
        body, name=name, grid=(S // tm,),
        in_specs=[tile, _resident((R, R)), _resident((R, R)), _resident((1, R)), _resident((1, R)), _resident((1, R))],
        out_specs=[tile] * 4,
        out_shape=[jax.ShapeDtypeStruct((S, R), F32)] * 4,
        compiler_params=_cparams(blk),
    )(r1, wa, wx, ba, bx, lam)


def _gates_bwd(guu, da, ra, ri, r1, wa, wx, lam, name):
    S, R = r1.shape
    tm = _divisor_tile(S, 256, 16)
    nsteps = S // tm
    panels = _band_panels(R, R // RNN_BLOCKS)

    def body(g_ref, da_ref, ra_ref, ri_ref, r1_ref, wa_ref, wx_ref, lam_ref,
             dr1_ref, dpa_ref, dpx_ref, dba_ref, dbx_ref, dlam_ref):
        i = pl.program_id(0)

        @pl.when(i == 0)
        def _():
            dba_ref[...] = jnp.zeros_like(dba_ref)
            dbx_ref[...] = jnp.zeros_like(dbx_ref)
            dlam_ref[...] = jnp.zeros_like(dlam_ref)

        g = g_ref[...]
        ra = ra_ref[...]
        ri = ri_ref[...]
        r1_ = r1_ref[...]
        sp = _softplus(-lam_ref[...])
        log_a = -RG_LRU_C * ra * sp
        a = jnp.exp(log_a)
        mult = jnp.sqrt(_neg_expm1(2.0 * log_a))
        d_ri = g * mult * r1_
        dr1 = g * mult * ri
        dmult = g * ri * r1_
        dlog_a = da_ref[...] * a - dmult * (a * a) / mult
        dra = dlog_a * (-RG_LRU_C * sp)
        dlam_ref[...] += _colsum(dlog_a * (-RG_LRU_C * ra))
        dpa = dra * ra * (1.0 - ra)
        dpx = d_ri * ri * (1.0 - ri)
        dba_ref[...] += _colsum(dpa)
        dbx_ref[...] += _colsum(dpx)
        dpa_b = dpa.astype(BF16)
        dpx_b = dpx.astype(BF16)
        dpa_ref[...] = dpa_b
        dpx_ref[...] = dpx_b
        dr1_ref[...] = dr1
        for k0, k1, c0, c1 in panels:
            dr1_ref[:, c0:c1] += (_dot_nt(dpa_ref[:, k0:k1], wa_ref[c0:c1, k0:k1])
                                  + _dot_nt(dpx_ref[:, k0:k1], wx_ref[c0:c1, k0:k1]))

        @pl.when(i == nsteps - 1)
        def _():
            dlam_ref[...] = dlam_ref[...] * (-_sigmoid(-lam_ref[...]))

    blk = 2 * (6 * tm * R * 4 + 2 * tm * R * 2) + 2 * R * R * 2 + 10 * tm * R * 4
    tile = _bs((tm, R), lambda i: (i, 0))
    rowspec = _bs((1, R), lambda i: (0, 0))
    row = jax.ShapeDtypeStruct((1, R), F32)
    return _streamed_call(
        body, name=name, grid=(nsteps,),
        in_specs=[tile] * 5 + [_resident((R, R)), _resident((R, R)), _resident((1, R))],
        out_specs=[tile, tile, tile, rowspec, rowspec, rowspec],
        out_shape=[jax.ShapeDtypeStruct((S, R), F32), jax.ShapeDtypeStruct((S, R), BF16), jax.ShapeDtypeStruct((S, R), BF16),
                   row, row, row],
        compiler_params=_cparams(blk),
    )(guu, da, ra, ri, r1, wa, wx, lam)


def _embed_blocks(w, name):
    H, bk, _ = w.shape

    def body(w_ref, o_ref):
        o_ref[...] = jnp.zeros_like(o_ref)
        for h in range(H):
            o_ref[bk * h:bk * (h + 1), bk * h:bk * (h + 1)] = w_ref[h].astype(BF16)

    return pl.pallas_call(body, name=name, out_shape=jax.ShapeDtypeStruct((H * bk, H * bk), BF16),
                          compiler_params=_cparams(3 * H * bk * H * bk * 2))(w)


def _block_grads(r1, dpa, dpx, name):
    S, R = r1.shape
    bk = R // RNN_BLOCKS
    tk = _divisor_tile(S, 512, 16)
    nsteps = S // tk
    panels = _band_panels(R, bk)

    def body(r1_ref, dpa_ref, dpx_ref, ga_ref, gx_ref, acca_ref, accx_ref):
        k = pl.program_id(0)

        @pl.when(k == 0)
        def _():
            acca_ref[...] = jnp.zeros_like(acca_ref)
            accx_ref[...] = jnp.zeros_like(accx_ref)

        for k0, k1, c0, c1 in panels:
            rb = r1_ref[:, k0:k1].astype(BF16)
            acca_ref[k0:k1, c0:c1] += _dot_tn(rb, dpa_ref[:, c0:c1])
            accx_ref[k0:k1, c0:c1] += _dot_tn(rb, dpx_ref[:, c0:c1])

        @pl.when(k == nsteps - 1)
        def _():
            for h in range(RNN_BLOCKS):
                ga_ref[h] = acca_ref[bk * h:bk * (h + 1), bk * h:bk * (h + 1)]
                gx_ref[h] = accx_ref[bk * h:bk * (h + 1), bk * h:bk * (h + 1)]

    tile = lambda: _bs((tk, R), lambda k: (k, 0))
    out = _bs((RNN_BLOCKS, bk, bk), lambda k: (0, 0, 0))
    sds = jax.ShapeDtypeStruct((RNN_BLOCKS, bk, bk), F32)
    return _streamed_call(
        body, name=name, grid=(nsteps,),
        in_specs=[tile(), tile(), tile()], out_specs=[out, out], out_shape=[sds, sds],
        scratch_shapes=[pltpu.VMEM((R, R), F32), pltpu.VMEM((R, R), F32)],
        compiler_params=_cparams(2 * (tk * R * 8) + 2 * R * R * 4 + 4 * tk * R * 4),
    )(r1, dpa, dpx)


def _scan_geometry(S):
    nseg = SCAN_SEGMENTS if S % (SCAN_SEGMENTS * SUBLANES) == 0 else SUBLANES
    return nseg, S // nseg


def _steps(n, step, init):
    u = SCAN_UNROLL

    def trip(t, carry):
        for k in range(u):
            carry = step(t * u + k, carry)
        return carry

    carry = lax.fori_loop(0, n // u, trip, init)
    for j in range(n - n % u, n):
        carry = step(j, carry)
    return carry


def _scan_fwd(a, u, name):
    S, C = a.shape
    nseg, L = _scan_geometry(S)
    T = min(SUBLANES, L)

    def body(a3, u3, h3, ta_ref, tu_ref, e_ref, p_ref, init_ref):

        def to_steps(i, carry):
            j0 = pl.multiple_of(i * T, T)
            ta_ref[pl.ds(j0, T)] = jnp.swapaxes(a3[:, pl.ds(j0, T), :], 0, 1)
            tu_ref[pl.ds(j0, T)] = jnp.swapaxes(u3[:, pl.ds(j0, T), :], 0, 1)
            return carry

        lax.fori_loop(0, L // T, to_steps, 0)

        def run1(j, carry):
            hs, ps = carry
            aj = ta_ref[j]
            return aj * hs + tu_ref[j], aj * ps

        e_ref[...], p_ref[...] = _steps(L, run1, (jnp.zeros((nseg, LANES), F32), jnp.ones((nseg, LANES), F32)))
        init_ref[0:1, :] = jnp.zeros((1, LANES), F32)
        for s in range(1, nseg):
            init_ref[s:s + 1, :] = e_ref[s - 1:s, :] + p_ref[s - 1:s, :] * init_ref[s - 1:s, :]

        def run2(j, hs):
            hs = ta_ref[j] * hs + tu_ref[j]
            tu_ref[j] = hs
            return hs

        _steps(L, run2, init_ref[...])

        def from_steps(i, carry):
            j0 = pl.multiple_of(i * T, T)
            h3[:, pl.ds(j0, T), :] = jnp.swapaxes(tu_ref[pl.ds(j0, T)], 0, 1)
            return carry

        lax.fori_loop(0, L // T, from_steps, 0)

    seg_block = _bs((nseg, L, LANES), lambda i: (0, 0, i))
    blk = 2 * (3 * S * LANES * 4) + 2 * S * LANES * 4
    return _streamed_call(
        body, name=name, grid=(C // LANES,),
        in_specs=[seg_block, seg_block],
        out_specs=seg_block,
        out_shape=jax.ShapeDtypeStruct((nseg, L, C), F32),
        scratch_shapes=[pltpu.VMEM((L, nseg, LANES), F32)] * 2 + [pltpu.VMEM((nseg, LANES), F32)] * 3,
        compiler_params=_cparams(blk),
    )(a.reshape(nseg, L, C), u.reshape(nseg, L, C)).reshape(S, C)


def _scan_bwd(a, dh, h, name):
    S, C = a.shape
    nseg, L = _scan_geometry(S)
    T = min(SUBLANES, L)
    assert L >= 2

    def body(a3, d3, h3, g3, da3, ta_ref, td_ref, th_ref, e_ref, p_ref, init_ref):

        def to_steps(i, carry):
            j0 = pl.multiple_of(i * T, T)
            for src, dst in ((a3, ta_ref), (d3, td_ref), (h3, th_ref)):
                dst[pl.ds(j0, T)] = jnp.swapaxes(src[:, pl.ds(j0, T), :], 0, 1)
            return carry

        lax.fori_loop(0, L // T, to_steps, 0)
        seg = lax.broadcasted_iota(jnp.int32, (nseg, LANES), 0)
        b_last = jnp.where(seg == nseg - 1, 0.0, pltpu.roll(ta_ref[0], nseg - 1, axis=0))
        h_first = jnp.where(seg == 0, 0.0, pltpu.roll(th_ref[L - 1], 1, axis=0))

        def run1(jj, carry):
            gs, ps = carry
            j = L - 2 - jj
            bj = ta_ref[j + 1]
            return bj * gs + td_ref[j], bj * ps

        e_ref[...], p_ref[...] = _steps(L - 1, run1, (td_ref[L - 1], b_last))
        init_ref[nseg - 1:nseg, :] = jnp.zeros((1, LANES), F32)
        for s in range(nseg - 2, -1, -1):
            init_ref[s:s + 1, :] = e_ref[s + 1:s + 2, :] + p_ref[s + 1:s + 2, :] * init_ref[s + 1:s + 2, :]

        gs = b_last * init_ref[...] + td_ref[L - 1]
        td_ref[L - 1] = gs
        th_ref[L - 1] = gs * th_ref[L - 2]

        def run2(jj, gs):
            j = L - 2 - jj
            gs = ta_ref[j + 1] * gs + td_ref[j]
            td_ref[j] = gs
            th_ref[j] = gs * th_ref[j - 1]
            return gs

        gs = _steps(L - 2, run2, gs)
        gs = ta_ref[1] * gs + td_ref[0]
        td_ref[0] = gs
        th_ref[0] = gs * h_first

        def from_steps(i, carry):
            j0 = pl.multiple_of(i * T, T)
            g3[:, pl.ds(j0, T), :] = jnp.swapaxes(td_ref[pl.ds(j0, T)], 0, 1)
            da3[:, pl.ds(j0, T), :] = jnp.swapaxes(th_ref[pl.ds(j0, T)], 0, 1)
            return carry

        lax.fori_loop(0, L // T, from_steps, 0)

    seg_block = _bs((nseg, L, LANES), lambda i: (0, 0, i))
    blk = 2 * (5 * S * LANES * 4) + 3 * S * LANES * 4
    g, da = _streamed_call(
        body, name=name, grid=(C // LANES,),
        in_specs=[seg_block] * 3,
        out_specs=[seg_block] * 2,
        out_shape=[jax.ShapeDtypeStruct((nseg, L, C), F32)] * 2,
        scratch_shapes=[pltpu.VMEM((L, nseg, LANES), F32)] * 3 + [pltpu.VMEM((nseg, LANES), F32)] * 3,
        compiler_params=_cparams(blk),
    )(a.reshape(nseg, L, C), dh.reshape(nseg, L, C), h.reshape(nseg, L, C))
    return g.reshape(S, C), da.reshape(S, C)


def _mixer_out_fwd(c4, h, rg, gc, gr, x1, wc, wr, wo, g, b, alpha, name):
    S, D = x1.shape
    R = h.shape[1]
    tm = _divisor_tile(S, 256, 16)

    def body(c4_ref, h_ref, rg_ref, gc_ref, gr_ref, x_ref, wc_ref, wr_ref, wo_ref, g_ref, b_ref,
             yc_ref, yr_ref, z_ref, y_ref):
        yc = _dot(c4_ref[...], wc_ref[...])
        q = (h_ref[...] * _gelu(rg_ref[...].astype(F32))).astype(BF16)
        yr = _dot(q, wr_ref[...])
        yc_ref[...] = yc.astype(BF16)
        yr_ref[...] = yr.astype(BF16)
        m = (_sigmoid(gc_ref[...].astype(F32)) * yc + _sigmoid(gr_ref[...].astype(F32)) * yr).astype(BF16)
        z = alpha * x_ref[...] + _dot(m, wo_ref[...])
        z_ref[...] = z
        y_ref[...] = _norm_fwd(z, g_ref[...], b_ref[...])

    blk = 2 * (tm * D * 2 + 2 * tm * R * 4 + 7 * tm * D * 4) + (2 * D * D + R * D) * 2 + 6 * tm * D * 4
    td = _bs((tm, D), lambda i: (i, 0))
    tr = _bs((tm, R), lambda i: (i, 0))
    return _streamed_call(
        body, name=name, grid=(S // tm,),
        in_specs=[td, tr, tr, td, td, td, _layer_resident(wc)[1], _layer_resident(wr)[1], _layer_resident(wo)[1],
                  _resident((1, D)), _resident((1, D))],
        out_specs=[td] * 4,
        out_shape=[jax.ShapeDtypeStruct((S, D), BF16)] * 2 + [jax.ShapeDtypeStruct((S, D), F32)] * 2,
        compiler_params=_cparams(blk),
    )(c4, h, rg, gc, gr, x1, wc[0], wr[0], wo[0], g, b)


def _mixer_out_bwd(dy, z, g, wo, yc, yr, gc, gr, name):
    S, D = dy.shape
    tm = _divisor_tile(S, 256, 16)

    def body(dy_ref, z_ref, g_ref, wo_ref, yc_ref, yr_ref, gc_ref, gr_ref,
             dz_ref, dzb_ref, m_ref, dyc_ref, dyr_ref, dgc_ref, dgr_ref, sgc_ref, sgr_ref, dg_ref, db_ref):
        @pl.when(pl.program_id(0) == 0)
        def _():
            for ref in (sgc_ref, sgr_ref, dg_ref, db_ref):
                ref[...] = jnp.zeros_like(ref)

        dy_ = dy_ref[...]
        dz, xhat = _norm_bwd(z_ref[...], g_ref[...], dy_)
        dg_ref[...] += _colsum(dy_ * xhat)
        db_ref[...] += _colsum(dy_)
        dz_ref[...] = dz
        dzb = dz.astype(BF16)
        dzb_ref[...] = dzb
        dm = _dot_nt(dzb, wo_ref[...])
        yc = yc_ref[...].astype(F32)
        yr = yr_ref[...].astype(F32)
        sc = _sigmoid(gc_ref[...].astype(F32))
        sr = _sigmoid(gr_ref[...].astype(F32))
        m_ref[...] = (sc * yc + sr * yr).astype(BF16)
        dyc_ref[...] = (dm * sc).astype(BF16)
        dyr_ref[...] = (dm * sr).astype(BF16)
        dgc = dm * yc * sc * (1.0 - sc)
        dgr = dm * yr * sr * (1.0 - sr)
        dgc_ref[...] = dgc.astype(BF16)
        dgr_ref[...] = dgr.astype(BF16)
        sgc_ref[...] += _colsum(dgc)
        sgr_ref[...] += _colsum(dgr)

    blk = 2 * (7 * tm * D * 4 + 6 * tm * D * 2) + D * D * 2 + 8 * tm * D * 4
    td = _bs((tm, D), lambda i: (i, 0))
    rowspec = _bs((1, D), lambda i: (0, 0))
    row = jax.ShapeDtypeStruct((1, D), F32)
    bfd = jax.ShapeDtypeStruct((S, D), BF16)
    return _streamed_call(
        body, name=name, grid=(S // tm,),
        in_specs=[td, td, _resident((1, D)), _layer_resident(wo)[1], td, td, td, td],
        out_specs=[td] * 7 + [rowspec] * 4,
        out_shape=[jax.ShapeDtypeStruct((S, D), F32), bfd, bfd, bfd, bfd, bfd, bfd, row, row, row, row],
        compiler_params=_cparams(blk),
    )(dy, z, g, wo[0], yc, yr, gc, gr)


def _branch_bwd(dyc, dyr, wc, wr, h, rg, name):
    S, D = dyc.shape
    R = h.shape[1]
    tm = _divisor_tile(S, 256, 16)

    def body(dyc_ref, dyr_ref, wc_ref, wr_ref, h_ref, rg_ref, dc4_ref, dh_ref, drg_ref, q_ref, srg_ref):
        @pl.when(pl.program_id(0) == 0)
        def _():
            srg_ref[...] = jnp.zeros_like(srg_ref)

        dc4_ref[...] = _dot_nt(dyc_ref[...], wc_ref[...]).astype(BF16)
        dq = _dot_nt(dyr_ref[...], wr_ref[...])
        h_ = h_ref[...]
        rg_ = rg_ref[...].astype(F32)
        ge = _gelu(rg_)
        dh_ref[...] = dq * ge
        drg = dq * h_ * _gelu_grad(rg_)
        drg_ref[...] = drg.astype(BF16)
        srg_ref[...] += _colsum(drg)
        q_ref[...] = (h_ * ge).astype(BF16)

    blk = 2 * (2 * tm * D * 2 + tm * D * 4 + 3 * tm * R * 4 + 2 * tm * R * 2) + (D * D + R * D) * 2 + 6 * tm * R * 4
    td = _bs((tm, D), lambda i: (i, 0))
    tr = _bs((tm, R), lambda i: (i, 0))
    return _streamed_call(
        body, name=name, grid=(S // tm,),
        in_specs=[td, td, _layer_resident(wc)[1], _layer_resident(wr)[1], tr, tr],
        out_specs=[td, tr, tr, tr, _bs((1, R), lambda i: (0, 0))],
        out_shape=[jax.ShapeDtypeStruct((S, D), BF16), jax.ShapeDtypeStruct((S, R), F32), jax.ShapeDtypeStruct((S, R), BF16),
                   jax.ShapeDtypeStruct((S, R), BF16), jax.ShapeDtypeStruct((1, R), F32)],
        compiler_params=_cparams(blk),
    )(dyc, dyr, wc[0], wr[0], h, rg)


def _loss_head(y, target, name):
    S, D = y.shape
    tm = _divisor_tile(S, 512, 16)
    nsteps = S // tm

    def body(y_ref, t_ref, loss_ref, dy_ref, acc_ref):
        i = pl.program_id(0)

        @pl.when(i == 0)
        def _():
            acc_ref[...] = jnp.zeros_like(acc_ref)

        err = y_ref[...] - t_ref[...]
        dy_ref[...] = err * (1.0 / D)
        acc_ref[...] += _colsum(err * err)

        @pl.when(i == nsteps - 1)
        def _():
            loss_ref[...] = jnp.sum(acc_ref[...], axis=-1, keepdims=True) * (0.5 / D)

    td = _bs((tm, D), lambda i: (i, 0))
    return _streamed_call(
        body, name=name, grid=(nsteps,),
        in_specs=[td, td],
        out_specs=[_bs((1, 1), lambda i: (0, 0)), td],
        out_shape=[jax.ShapeDtypeStruct((1, 1), F32), jax.ShapeDtypeStruct((S, D), F32)],
        scratch_shapes=[pltpu.VMEM((1, D), F32)],
        compiler_params=_cparams(2 * 3 * tm * D * 4),
    )(y, target)


def _adamw_math(w, g, m, v):
    m = ADAM_B1 * m + (1.0 - ADAM_B1) * g
    v = ADAM_B2 * v + (1.0 - ADAM_B2) * (g * g)
    m_hat = m / (1.0 - ADAM_B1 ** ADAM_STEP)
    v_hat = v / (1.0 - ADAM_B2 ** ADAM_STEP)
    delta = -ADAM_LR * (m_hat / (jnp.sqrt(v_hat) + ADAM_EPS) + ADAM_WD * w)
    return delta, m, v


def _adamw_sharded(w, m, v, own, sib, rem, name):
    _, r, c = w.shape
    tr = _divisor_tile(r, max(16, (1 << 20) // (4 * c) // 16 * 16), 16)

    def body(w_ref, m_ref, v_ref, own_ref, sib_ref, rem_ref, g_ref, d_ref, nm_ref, nv_ref):
        mine = pl.program_id(0) == lax.axis_index("c")
        g = jnp.where(mine, own_ref[...], sib_ref[...]).astype(F32)
        for j in range(N_CHIPS - 1):
            g = g + rem_ref[j].astype(F32)
        delta, nm, nv = _adamw_math(w_ref[...], g, m_ref[...], v_ref[...])
        g_ref[...] = g
        d_ref[...] = delta
        nm_ref[...] = nm
        nv_ref[...] = nv

    tile = _bs((None, tr, c), lambda l, i: (l, i, 0))
    flat = _bs((tr, c), lambda l, i: (i, 0))
    sds = jax.ShapeDtypeStruct(w.shape, F32)
    return _streamed_call(
        body, name=name, grid=(2, r // tr),
        in_specs=[tile, tile, tile, flat, flat, _bs((N_CHIPS - 1, None, tr, c), lambda l, i: (0, l, i, 0))],
        out_specs=[tile] * 4,
        out_shape=[sds] * 4,
        compiler_params=_cparams(2 * (7 * tr * c * 4 + (N_CHIPS + 1) * tr * c * 2) + 6 * tr * c * 4),
    )(w, m, v, own, sib, rem)


def _adamw_flat(w, m, v, g, name):
    rows = w.shape[0]
    tr = _divisor_tile(rows, 1024, SUBLANES)

    def body(w_ref, m_ref, v_ref, g_ref, d_ref, nm_ref, nv_ref):
        delta, nm, nv = _adamw_math(w_ref[...], g_ref[...], m_ref[...], v_ref[...])
        d_ref[...] = delta
        nm_ref[...] = nm
        nv_ref[...] = nv

    tile = _bs((tr, LANES), lambda i: (i, 0))
    sds = jax.ShapeDtypeStruct(w.shape, F32)
    return _streamed_call(
        body, name=name, grid=(rows // tr,),
        in_specs=[tile] * 4, out_specs=[tile] * 3, out_shape=[sds] * 3,
        compiler_params=_cparams(2 * 7 * tr * LANES * 4),
    )(w, m, v, g)


def _pair_sum_bf16(g0, g1, theirs, name):
    rows, c = g0.shape
    tr = _divisor_tile(rows, max(16, (1 << 20) // (4 * c) // 16 * 16), 16)

    def body(g0_ref, g1_ref, t_ref, o_ref):
        mine = jnp.where(lax.axis_index("c") == 0, g0_ref[...], g1_ref[...])
        o_ref[...] = (mine + t_ref[...]).astype(BF16)

    tile = _bs((tr, c), lambda i: (i, 0))
    return _streamed_call(
        body, name=name, grid=(rows // tr,),
        in_specs=[tile, tile, tile], out_specs=tile, out_shape=jax.ShapeDtypeStruct((rows, c), BF16),
        compiler_params=_cparams(2 * 4 * tr * c * 4),
    )(g0, g1, theirs)


ANY = pl.BlockSpec(memory_space=pl.ANY)


def _mesh_position():
    return lax.axis_index("x"), lax.axis_index("y"), lax.axis_index("c")


def _other_chips():
    x, y, c = _mesh_position()
    chips = [(1 - x, y), (x, 1 - y), (1 - x, 1 - y)]
    return 2 * x + y, (x, y, 1 - c), chips, [2 * cx + cy for cx, cy in chips]


def _chip_slab(ref, k, width, by_cols):
    if by_cols:
        start = k * width if isinstance(k, int) else pl.multiple_of(k * width, LANES)
        return ref.at[:, pl.ds(start, width)]
    return ref.at[k]


HBM = pl.BlockSpec(memory_space=pltpu.HBM)
SEM = pl.BlockSpec(memory_space=pltpu.SEMAPHORE)
DATAFLOW = pltpu.SideEffectType.DATAFLOW_SIDE_EFFECTING
N_GATHER_COPIES = 4


def _land_shape(src, by_cols):
    return src.shape[:-1] + (N_CHIPS * src.shape[-1],) if by_cols else (N_CHIPS,) + src.shape


def _gather_copy(src_ref, land_ref, by_cols, send_sems, recv_sems, pos, j, slab, to):
    width = src_ref.shape[-1]
    return pltpu.make_async_remote_copy(src_ref=src_ref, dst_ref=_chip_slab(land_ref, slab, width, by_cols),
                                        send_sem=send_sems.at[N_GATHER_COPIES * pos + j],
                                        recv_sem=recv_sems.at[N_GATHER_COPIES * pos + j],
                                        device_id=to, device_id_type=MESH)


def _gather_start(srcs, by_cols, groups, name):
    U = len(srcs)
    G = len(groups)
    lands = [lax.empty(_land_shape(s, bc), s.dtype) for s, bc in zip(srcs, by_cols)]

    def body(*refs):
        src = refs[:U]
        land = refs[U:2 * U]
        send_sems = refs[2 * U:2 * U + G]
        recv_sems = refs[2 * U + G:2 * U + 2 * G]
        token = refs[-1]
        c = lax.axis_index("c")
        me, sibling, chips, _ = _other_chips()
        targets = [(*chip, c) for chip in chips] + [sibling]
        for g, members in enumerate(groups):
            for pos, u in enumerate(members):
                for j, to in enumerate(targets):
                    _gather_copy(src[u], land[u], by_cols[u], send_sems[g], recv_sems[g], pos, j, me, to).start()
        token[...] = jnp.zeros_like(token)

    sem_shapes = [pltpu.SemaphoreType.DMA((len(m) * N_GATHER_COPIES,)) for m in groups]
    outs = pl.pallas_call(
        body, name=name,
        out_shape=tuple(sem_shapes + sem_shapes + [pltpu.HBM(s.shape, s.dtype) for s in srcs]
                        + [pltpu.HBM(v.shape, v.dtype) for v in lands] + [jax.ShapeDtypeStruct((SUBLANES, LANES), F32)]),
        in_specs=[HBM] * (2 * U),
        out_specs=tuple([SEM] * (2 * G) + [HBM] * (2 * U) + [pl.BlockSpec(memory_space=pltpu.VMEM)]),
        input_output_aliases={i: 2 * G + i for i in range(2 * U)},
        compiler_params=pltpu.CompilerParams(has_side_effects=DATAFLOW),
    )(*[pltpu.with_memory_space_constraint(a, pltpu.HBM) for a in list(srcs) + lands])
    return outs[:G], outs[G:2 * G], outs[2 * G:2 * G + U], outs[2 * G + U:2 * G + 2 * U]


def _gather_wait(srcs, lands, by_cols, send_sems, recv_sems, after, name):
    n = len(srcs)

    def body(*refs):
        src = refs[:n]
        land = refs[n:2 * n]
        send_ref, recv_ref = refs[2 * n:2 * n + 2]
        _, sibling, _, _ = _other_chips()
        for pos in range(n):
            for j in range(N_GATHER_COPIES):
                cp = _gather_copy(src[pos], land[pos], by_cols[pos], send_ref, recv_ref, pos, j, 0, sibling)
                cp.wait_send()
                cp.wait_recv()

    outs = pl.pallas_call(
        body, name=name,
        out_shape=tuple([pltpu.HBM(s.shape, s.dtype) for s in srcs] + [pltpu.HBM(v.shape, v.dtype) for v in lands]),
        in_specs=[HBM] * (2 * n) + [SEM, SEM, pl.BlockSpec(memory_space=pl.ANY)],
        out_specs=tuple([HBM] * (2 * n)),
        input_output_aliases={i: i for i in range(2 * n)},
        compiler_params=pltpu.CompilerParams(has_side_effects=DATAFLOW),
    )(*srcs, *lands, send_sems, recv_sems, after)
    return outs[n:]


def _scatter_grads(csums, by_cols, name):
    n = len(csums)
    shard = [(s.shape[0], s.shape[1] // N_CHIPS) if bc else s.shape[1:] for s, bc in zip(csums, by_cols)]

    def body(*refs):
        src = refs[:n]
        rem = refs[n:2 * n]
        sib = refs[2 * n:3 * n]
        send_sems, recv_sems = refs[3 * n:]
        c = lax.axis_index("c")
        me, sibling, chips, chip_ids = _other_chips()

        def remote(i, k, src_ref, dst_ref, to):
            return pltpu.make_async_remote_copy(src_ref=src_ref, dst_ref=dst_ref, send_sem=send_sems.at[i, k],
                                                recv_sem=recv_sems.at[i, k], device_id=to, device_id_type=MESH)

        def part(i, k):
            return _chip_slab(src[i], k, shard[i][-1], by_cols[i])

        started = []
        for i in range(n):
            for j in range(3):
                started.append(remote(i, j, part(i, chip_ids[j]), rem[i].at[j, c], (*chips[j], c)))
            started.append(remote(i, 6, part(i, me), sib[i], sibling))
        for cp in started:
            cp.start()
        for i in range(n):
            for j in range(3):
                slot = rem[i].at[j, c]
                remote(i, j, slot, slot, sibling).wait_recv()
                fwd = remote(i, 3 + j, slot, slot, sibling)
                fwd.start()
                started.append(fwd)
        for i in range(n):
            for j in range(3):
                slot = rem[i].at[j, 1 - c]
                remote(i, 3 + j, slot, slot, sibling).wait_recv()
            remote(i, 6, sib[i], sib[i], sibling).wait_recv()
        for cp in started:
            cp.wait_send()

    out_shape = ([jax.ShapeDtypeStruct((N_CHIPS - 1, 2) + tuple(sh), s.dtype) for s, sh in zip(csums, shard)]
                 + [jax.ShapeDtypeStruct(tuple(sh), s.dtype) for s, sh in zip(csums, shard)])
    outs = _streamed_call(
        body, name=name,
        in_specs=[ANY] * n, out_specs=[ANY] * (2 * n), out_shape=out_shape,
        scratch_shapes=[pltpu.SemaphoreType.DMA((n, 7)), pltpu.SemaphoreType.DMA((n, 7))],
    )(*csums)
    return outs[:n], outs[n:]


def _sibling_exchange(g0, g1, name):
    n = len(g0)

    def body(*refs):
        layers = (refs[:n], refs[n:2 * n])
        theirs = refs[2 * n:3 * n]
        send_sems, recv_sems = refs[3 * n:]
        x, y, c = _mesh_position()

        def remote(i, src_ref):
            return pltpu.make_async_remote_copy(src_ref=src_ref, dst_ref=theirs[i], send_sem=send_sems.at[i],
                                                recv_sem=recv_sems.at[i], device_id=(x, y, 1 - c), device_id_type=MESH)

        for keep in range(2):
            @pl.when(c == keep)
            def _():
                for i in range(n):
                    remote(i, layers[1 - keep][i]).start()

        for i in range(n):
            remote(i, layers[0][i]).wait()

    return _streamed_call(
        body, name=name,
        in_specs=[ANY] * (2 * n), out_specs=[ANY] * n, out_shape=[jax.ShapeDtypeStruct(g.shape, g.dtype) for g in g0],
        scratch_shapes=[pltpu.SemaphoreType.DMA((n,)), pltpu.SemaphoreType.DMA((n,))],
    )(*g0, *g1)


def _all_reduce_small(v, name):
    _, rows, _ = v.shape

    def body(v_ref, o_ref, recv_ref, send_sems, recv_sems):
        x, y, c = _mesh_position()
        me = 4 * x + 2 * y + c
        peers = []
        for d in range(1, N_DEV):
            px, py, pc = x ^ ((d >> 2) & 1), y ^ ((d >> 1) & 1), c ^ (d & 1)
            peers.append(((px, py, pc), 4 * px + 2 * py + pc))

        def remote(k, src_ref, dst_ref, to):
            return pltpu.make_async_remote_copy(src_ref=src_ref, dst_ref=dst_ref, send_sem=send_sems.at[k],
                                                recv_sem=recv_sems.at[k], device_id=to, device_id_type=MESH)

        scatter = [remote(d, v_ref.at[pid], recv_ref.at[me], to) for d, (to, pid) in enumerate(peers)]
        for cp in scatter:
            cp.start()
        recv_ref[pl.ds(me, 1)] = v_ref[pl.ds(me, 1)]
        for d, (to, pid) in enumerate(peers):
            remote(d, v_ref.at[pid], recv_ref.at[pid], to).wait_recv()
        total = recv_ref[0]
        for s in range(1, N_DEV):
            total = total + recv_ref[s]
        o_ref[pl.ds(me, 1)] = total[None]
        gather = [remote(N_DEV - 1 + d, o_ref.at[me], o_ref.at[me], to) for d, (to, pid) in enumerate(peers)]
        for cp in gather:
            cp.start()
        for d, (to, pid) in enumerate(peers):
            remote(N_DEV - 1 + d, o_ref.at[pid], o_ref.at[pid], to).wait_recv()
        for cp in scatter + gather:
            cp.wait_send()

    vm = pl.BlockSpec(memory_space=pltpu.VMEM)
    return pl.pallas_call(
        body, name=name,
        in_specs=[vm], out_specs=vm, out_shape=jax.ShapeDtypeStruct(v.shape, F32),
        scratch_shapes=[pltpu.VMEM(v.shape, F32), pltpu.SemaphoreType.DMA((2 * (N_DEV - 1),)),
                        pltpu.SemaphoreType.DMA((2 * (N_DEV - 1),))],
        compiler_params=_cparams(4 * _nbytes(v.shape, F32)),
    )(v)


SHARDED_MATS = ("ffn1_w_gu", "ffn1_w_down", "mix_w_in", "conv_w_proj", "rnn_w_proj", "mix_w_out", "ffn2_w_gu", "ffn2_w_down")
COL_SHARDED = ("ffn1_w_gu", "ffn2_w_gu", "conv_dw_w")
SHARDED_VECS = ("conv_dw_w", "rnn_conv_w")
WEIGHT_NAMES = ("ffn1_w_gu", "ffn1_w_down", "ln1_g", "ln1_b", "mix_w_in", "mix_b_in", "conv_dw_w", "conv_dw_b", "conv_gn_g",
                "conv_gn_b", "conv_w_proj", "rnn_conv_w", "rnn_conv_b", "rnn_w_a", "rnn_b_a", "rnn_w_x", "rnn_b_x",
                "rnn_lambda", "rnn_w_proj", "mix_w_out", "ln2_g", "ln2_b", "ffn2_w_gu", "ffn2_w_down", "ln3_g", "ln3_b")
SMALL_NAMES = tuple(n for n in WEIGHT_NAMES if n not in SHARDED_MATS)
SECTION_NAMES = ("cv", "cg", "rx", "rg", "gc", "gr")


def _unshard_cols(gathered):
    k4, K, n = gathered.shape
    return jnp.transpose(gathered, (1, 0, 2)).reshape(K, k4 * n)


def _row(v):
    return v.reshape(1, -1)


def _layer_forward(x0, p, alpha, l, hooks):
    t = f"l{l}_"
    sv = {"x0": x0}
    x1, sv["z1"], sv["hg1"], sv["hu1"] = _ffn_fwd(x0, p["wgu1"], p["wd1"], p["ln1_g"], p["ln1_b"], alpha, t + "ffn1_fwd")
    sv["x1"] = x1
    hooks.get("after_ffn1", lambda v: None)(x1)
    sec = dict(zip(SECTION_NAMES, _mix_in(x1, p["win"], p["bin"], p["sections"], t + "mix_in")))
    sv.update(sec)
    sv["c2"], c4 = _conv_branch_fwd(sec["cv"], sec["cg"], p["conv_dw_w"], p["conv_dw_b"], p["conv_gn_g"], p["conv_gn_b"], t + "conv_fwd")
    sv["c4"] = c4
    r1 = _short_conv_fwd(sec["rx"], p["rnn_conv_w"], p["rnn_conv_b"], t + "rconv_fwd")
    sv["r1"] = r1
    sv["ra"], sv["ri"], a, uu = _gates_fwd(r1, p["wa"], p["wx"], p["rnn_b_a"], p["rnn_b_x"], p["rnn_lambda"], t + "gates_fwd")
    sv["a"] = a
    h = _scan_fwd(a, uu, t + "scan_fwd")
    sv["h"] = h
    hooks.get("after_scan", lambda v: None)(h)
    sv["yc"], sv["yr"], sv["z2"], x2 = _mixer_out_fwd(c4, h, sec["rg"], sec["gc"], sec["gr"], x1, p["wc"], p["wr"], p["wo"],
                                                      p["ln2_g"], p["ln2_b"], alpha, t + "mixout_fwd")
    sv["x2"] = x2
    hooks.get("after_mixer", lambda v: None)(x2)
    x3, sv["z3"], sv["hg2"], sv["hu2"] = _ffn_fwd(x2, p["wgu2"], p["wd2"], p["ln3_g"], p["ln3_b"], alpha, t + "ffn2_fwd")
    hooks.get("after_layer", lambda v: None)(x3)
    return x3, sv


def _layer_backward(dy, p, sv, alpha, l):
    t = f"l{l}_"
    g = {}
    dx2, df, a_act, dhg, dhu, g["ln3_g"], g["ln3_b"] = _ffn_bwd(dy, sv["z3"], sv["hg2"], sv["hu2"], p["wgu2"], p["wd2"],
                                                                 p["ln3_g"], alpha, t + "ffn2_bwd")
    g["ffn2_w_down"] = _mm_tn(a_act, df, t + "dwd2")
    g["ffn2_w_gu"] = _mm_tn_pair(sv["x2"], dhg, dhu, t + "dwgu2")
    (dz2, dz2b, m_b, dyc, dyr, dgc, dgr, s_gc, s_gr, g["ln2_g"], g["ln2_b"]) = _mixer_out_bwd(
        dx2, sv["z2"], p["ln2_g"], p["wo"], sv["yc"], sv["yr"], sv["gc"], sv["gr"], t + "mixout_bwd")
    g["mix_w_out"] = _mm_tn(m_b, dz2b, t + "dwo")
    dc4, dh, drg, q_b, s_rg = _branch_bwd(dyc, dyr, p["wc"], p["wr"], sv["h"], sv["rg"], t + "branch_bwd")
    g["conv_w_proj"] = _mm_tn(sv["c4"], dyc, t + "dwc")
    g["rnn_w_proj"] = _mm_tn(q_b, dyr, t + "dwr")
    (dcv, dcg, g["conv_dw_w"], g["conv_dw_b"], g["conv_gn_g"], g["conv_gn_b"], s_cv, s_cg) = _conv_branch_bwd(
        dc4, sv["c2"], sv["cv"], sv["cg"], p["conv_dw_w"], p["conv_gn_g"], p["conv_gn_b"], t + "conv_bwd")
    guu, da = _scan_bwd(sv["a"], dh, sv["h"], t + "scan_bwd")
    dr1, dpa, dpx, g["rnn_b_a"], g["rnn_b_x"], g["rnn_lambda"] = _gates_bwd(
        guu, da, sv["ra"], sv["ri"], sv["r1"], p["wa"], p["wx"], p["rnn_lambda"], t + "gates_bwd")
    g["rnn_w_a"], g["rnn_w_x"] = _block_grads(sv["r1"], dpa, dpx, t + "dwax")
    drx, g["rnn_conv_w"], g["rnn_conv_b"], s_rx = _short_conv_bwd(dr1, sv["rx"], p["rnn_conv_w"], t + "rconv_bwd")
    du = {"cv": dcv, "cg": dcg, "rx": drx, "rg": drg, "gc": dgc, "gr": dgr}
    order = ("cv", "cg", "rx", "rg", "gc", "gr")
    g["mix_w_in"] = jnp.concatenate([_mm_tn(du[s], sv["x1"], t + "dwin_" + s) for s in order], axis=0)
    g["mix_b_in"] = jnp.concatenate([s_cv, s_cg, s_rx, s_rg, s_gc, s_gr], axis=1)
    dx1 = _mix_dx(dz2, [du[s] for s in order], p["win"], p["sections"], alpha, t + "mix_dx")
    dx0, df, a_act, dhg, dhu, g["ln1_g"], g["ln1_b"] = _ffn_bwd(dx1, sv["z1"], sv["hg1"], sv["hu1"], p["wgu1"], p["wd1"],
                                                                 p["ln1_g"], alpha, t + "ffn1_bwd")
    g["ffn1_w_down"] = _mm_tn(a_act, df, t + "dwd1")
    g["ffn1_w_gu"] = _mm_tn_pair(sv["x0"], dhg, dhu, t + "dwgu1")
    return dx0, g


def _pack_small(arrays, piece_rows):
    flat = jnp.concatenate([a.reshape(-1) for a in arrays])
    total = N_DEV * piece_rows * LANES
    return jnp.pad(flat, (0, total - flat.shape[0])).reshape(N_DEV, piece_rows, LANES)


def _unpack_small(packed, shapes):
    flat = packed.reshape(-1)
    out, off = [], 0
    for shp in shapes:
        n = 1
        for s in shp:
            n *= s
        out.append(flat[off:off + n].reshape(shp))
        off += n
    return out


def kernel(x, ffn1_w_gu, ffn1_w_down, ln1_g, ln1_b, mix_w_in, mix_b_in, conv_dw_w, conv_dw_b, conv_gn_g, conv_gn_b, conv_w_proj, rnn_conv_w, rnn_conv_b, rnn_w_a, rnn_b_a, rnn_w_x, rnn_b_x, rnn_lambda, rnn_w_proj, mix_w_out, ln2_g, ln2_b, ffn2_w_gu, ffn2_w_down, ln3_g, ln3_b, loss_target, m_ffn1_w_gu, m_ffn1_w_down, m_ln1_g, m_ln1_b, m_mix_w_in, m_mix_b_in, m_conv_dw_w, m_conv_dw_b, m_conv_gn_g, m_conv_gn_b, m_conv_w_proj, m_rnn_conv_w, m_rnn_conv_b, m_rnn_w_a, m_rnn_b_a, m_rnn_w_x, m_rnn_b_x, m_rnn_lambda, m_rnn_w_proj, m_mix_w_out, m_ln2_g, m_ln2_b, m_ffn2_w_gu, m_ffn2_w_down, m_ln3_g, m_ln3_b, v_ffn1_w_gu, v_ffn1_w_down, v_ln1_g, v_ln1_b, v_mix_w_in, v_mix_b_in, v_conv_dw_w, v_conv_dw_b, v_conv_gn_g, v_conv_gn_b, v_conv_w_proj, v_rnn_conv_w, v_rnn_conv_b, v_rnn_w_a, v_rnn_b_a, v_rnn_w_x, v_rnn_b_x, v_rnn_lambda, v_rnn_w_proj, v_mix_w_out, v_ln2_g, v_ln2_b, v_ffn2_w_gu, v_ffn2_w_down, v_ln3_g, v_ln3_b):
    args = locals()
    W = {n: args[n] for n in WEIGHT_NAMES}
    M = {n: args["m_" + n] for n in WEIGHT_NAMES}
    V = {n: args["v_" + n] for n in WEIGHT_NAMES}
    depth = ln1_g.shape[0]
    assert depth == 2, "each core of a chip moves one layer's weights and gradients"
    alpha = float((2 * depth) ** 0.25)
    S, D = x.shape[1], x.shape[2]
    F = ffn1_w_down.shape[1] * N_CHIPS
    R = rnn_w_proj.shape[1] * N_CHIPS
    chip = 2 * lax.axis_index("x") + lax.axis_index("y")

    for d in (W, M, V):
        d["mix_w_in"] = jnp.transpose(d["mix_w_in"], (0, 2, 1))

    names = SHARDED_MATS + SHARDED_VECS
    unit_groups = [[(0, "ffn1_w_gu"), (0, "ffn1_w_down")], [(0, "mix_w_in"), (0, "conv_dw_w"), (0, "rnn_conv_w")],
                   [(0, "conv_w_proj"), (0, "rnn_w_proj"), (0, "mix_w_out")], [(0, "ffn2_w_gu"), (0, "ffn2_w_down")],
                   [(1, n) for n in names]]
    order = [u for g in unit_groups for u in g]
    index = {u: i for i, u in enumerate(order)}
    groups = [[index[u] for u in g] for g in unit_groups]
    srcs = [W[n][l].astype(BF16) if n in SHARDED_MATS else W[n][l] for l, n in order]
    by_cols = [n in COL_SHARDED for _, n in order]
    send_sems, recv_sems, src_thru, land_thru = _gather_start(srcs, by_cols, groups, "gather_start")

    sections = ((0, D, F32), (D, D, F32), (2 * D, R, F32), (2 * D + R, R, BF16), (2 * D + 2 * R, D, BF16),
                (3 * D + 2 * R, D, BF16))
    keys = {"ffn1_w_gu": "wgu1", "ffn1_w_down": "wd1", "ffn2_w_gu": "wgu2", "ffn2_w_down": "wd2", "conv_w_proj": "wc",
            "rnn_w_proj": "wr", "mix_w_out": "wo"}
    params = []
    for l in range(depth):
        p = {"wa": _embed_blocks(rnn_w_a[l], f"l{l}_embed_wa"), "wx": _embed_blocks(rnn_w_x[l], f"l{l}_embed_wx")}
        for n in ("ln1_g", "ln1_b", "ln2_g", "ln2_b", "ln3_g", "ln3_b", "conv_dw_b", "conv_gn_g", "conv_gn_b", "rnn_conv_b",
                  "rnn_b_a", "rnn_b_x", "rnn_lambda"):
            p[n] = _row(W[n][l])
        p["bin"] = _row(mix_b_in[l])
        p["sections"] = sections
        params.append(p)

    def wait_group(g, after):
        ids = groups[g]
        landed = _gather_wait([src_thru[i] for i in ids], [land_thru[i] for i in ids], [by_cols[i] for i in ids],
                              send_sems[g], recv_sems[g], after, f"gather_wait{g}")
        for i, full in zip(ids, landed):
            l, n = order[i]
            p = params[l]
            if n not in COL_SHARDED:
                full = full.reshape((N_CHIPS * full.shape[1],) + full.shape[2:])
            if n == "mix_w_in":
                p["win"] = full
            elif n == "rnn_conv_w":
                p[n] = _unshard_cols(landed[ids.index(i)])
            elif n == "conv_dw_w":
                p[n] = full
            else:
                p[keys[n]] = (full[None], 0)

    h = x[0]
    wait_group(0, h)
    saved = []
    hooks = [{"after_ffn1": lambda v: wait_group(1, v), "after_scan": lambda v: wait_group(2, v),
              "after_mixer": lambda v: wait_group(3, v), "after_layer": lambda v: wait_group(4, v)}, {}]
    for l in range(depth):
        h, sv = _layer_forward(h, params[l], alpha, l, hooks[l])
        saved.append(sv)
    loss_part, dy = _loss_head(h, loss_target[0], "loss_head")
    loss = lax.psum(loss_part[0, 0], ("x", "y", "c"))
    grads = [None] * depth
    for l in reversed(range(depth)):
        dy, grads[l] = _layer_backward(dy, params[l], saved[l], alpha, l)
    grad_x = dy[None]

    g0 = [grads[0][n] for n in SHARDED_MATS]
    g1 = [grads[1][n] for n in SHARDED_MATS]
    theirs = _sibling_exchange(g0, g1, "pair_exchange")
    mat_cols = [n in COL_SHARDED for n in SHARDED_MATS]
    chip_sums = []
    for n, bc, a0, a1, b in zip(SHARDED_MATS, mat_cols, g0, g1, theirs):
        cs = _pair_sum_bf16(a0, a1, b, "pair_sum_" + n)
        chip_sums.append(cs if bc else cs.reshape((N_CHIPS, cs.shape[0] // N_CHIPS, cs.shape[1])))
    rem, sib = _scatter_grads(chip_sums, mat_cols, "scatter_grads")
    out_g, out_d, out_m, out_v = {}, {}, {}, {}
    for n, bc, cs, rm, sb in zip(SHARDED_MATS, mat_cols, chip_sums, rem, sib):
        if bc:
            width = cs.shape[1] // N_CHIPS
            own = lax.dynamic_slice_in_dim(cs, chip * width, width, axis=1)
        else:
            own = lax.dynamic_index_in_dim(cs, chip, axis=0, keepdims=False)
        out_g[n], out_d[n], out_m[n], out_v[n] = _adamw_sharded(W[n], M[n], V[n], own, sb, rm, "adamw_" + n)
    for d in (out_g, out_d, out_m, out_v):
        d["mix_w_in"] = jnp.transpose(d["mix_w_in"], (0, 2, 1))

    small_grads = [jnp.stack([grads[l][n].reshape(W[n].shape[1:] if n not in SHARDED_VECS else
                                                   (W[n].shape[1], W[n].shape[2] * N_CHIPS)) for l in range(depth)])
                   for n in SMALL_NAMES]
    n_small = sum(int(a.size) for a in small_grads)
    piece_rows = -(-n_small // (N_DEV * LANES * SUBLANES)) * SUBLANES
    reduced = _unpack_small(_all_reduce_small(_pack_small(small_grads, piece_rows), "all_reduce_small"),
                            [a.shape for a in small_grads])
    local_g = []
    for n, gr in zip(SMALL_NAMES, reduced):
        if n in SHARDED_VECS:
            width = W[n].shape[2]
            gr = lax.dynamic_slice_in_dim(gr, chip * width, width, axis=2)
        local_g.append(gr)
    n_local = sum(int(a.size) for a in local_g)
    flat_rows = -(-n_local // (N_DEV * LANES * SUBLANES)) * SUBLANES * N_DEV
    pack = lambda arrs: _pack_small(arrs, flat_rows // N_DEV).reshape(flat_rows, LANES)
    shapes = [a.shape for a in local_g]
    deltas, new_m, new_v = _adamw_flat(pack([W[n] for n in SMALL_NAMES]), pack([M[n] for n in SMALL_NAMES]),
                                       pack([V[n] for n in SMALL_NAMES]), pack(local_g), "adamw_small")
    for n, gr, d_, m_, v_ in zip(SMALL_NAMES, local_g, _unpack_small(deltas, shapes), _unpack_small(new_m, shapes),
                                 _unpack_small(new_v, shapes)):
        out_g[n], out_d[n], out_m[n], out_v[n] = gr, d_, m_, v_

    return (loss, grad_x, *[out_g[n] for n in WEIGHT_NAMES], *[out_d[n] for n in WEIGHT_NAMES],
            *[out_m[n] for n in WEIGHT_NAMES], *[out_v[n] for n in WEIGHT_NAMES])
```

```python
import functools

import jax
import jax.numpy as jnp
from jax import lax
from jax.experimental import pallas as pl
from jax.experimental.pallas import tpu as pltpu

F32 = jnp.float32
BF16 = jnp.bfloat16
MESH = pl.DeviceIdType.MESH

LN_EPS = 1e-5
CONV_GROUPS = 8
RNN_BLOCKS = 16
RG_LRU_C = 8.0
ADAM_LR = 0.001
ADAM_B1 = 0.9
ADAM_B2 = 0.999
ADAM_EPS = 1e-08
ADAM_WD = 0.01
ADAM_STEP = 10

LANES = 128
SUBLANES = 8
V7X_VMEM_BYTES = 64 << 20
VMEM_LIMIT_CAP = V7X_VMEM_BYTES - (6 << 20)
N_CHIPS = 4
N_DEV = 8
CONV_ROWS = 64
EW_ROWS = 1024
SCAN_SEGMENTS = 32
SCAN_UNROLL = 4


def _cparams(block_bytes):
    limit = min(VMEM_LIMIT_CAP, max(int(block_bytes) + (8 << 20), 24 << 20))
    return pltpu.CompilerParams(vmem_limit_bytes=limit)


def _nbytes(shape, dtype):
    n = 1
    for s in shape:
        n *= s
    return n * jnp.dtype(dtype).itemsize


def _divisor_tile(n, limit, quantum):
    if n <= limit:
        return n
    best = None
    for t in range(quantum, limit + 1, quantum):
        if n % t == 0:
            best = t
    assert best is not None, (n, limit, quantum)
    return best


def _bs(shape, imap, **kw):
    return pl.BlockSpec(shape, imap, **kw)


def _resident(shape):
    nd = len(shape)
    return pl.BlockSpec(shape, lambda *_: (0,) * nd, pipeline_mode=pl.Buffered(1))


def _streamed_call(body, **kw):
    call = pl.pallas_call(body, **kw)
    return lambda *operands: call(*[pltpu.with_memory_space_constraint(o, pltpu.HBM) for o in operands])


def _layer_block(w, block, imap, **kw):
    arr, layer = w
    return arr, pl.BlockSpec((None,) + block, lambda *ids: (layer,) + imap(*ids), **kw)


def _layer_resident(w):
    arr, _ = w
    return _layer_block(w, arr.shape[1:], lambda *_: (0, 0), pipeline_mode=pl.Buffered(1))


def _sigmoid(x):
    return jax.nn.sigmoid(x)


def _dot(a, b):
    return jnp.dot(a, b, preferred_element_type=F32)


def _dot_nt(a, b):
    return lax.dot_general(a, b, (((1,), (1,)), ((), ())), preferred_element_type=F32)


def _dot_tn(a, b):
    return lax.dot_general(a, b, (((0,), (0,)), ((), ())), preferred_element_type=F32)


def _row_mean(z):
    return jnp.mean(z, axis=-1, keepdims=True)


def _lane_mean(z):
    hi = z.astype(BF16)
    lo = (z - hi.astype(F32)).astype(BF16)
    ones = jnp.full((2 * LANES, LANES), 1.0 / LANES, BF16)
    return jnp.dot(jnp.concatenate([hi, lo], axis=-1), ones, preferred_element_type=F32)


def _norm_fwd(z, g, b, mean=_row_mean):
    mu = mean(z)
    xc = z - mu
    var = mean(xc * xc)
    return xc * lax.rsqrt(var + LN_EPS) * g + b


def _norm_bwd(z, g, dy, mean=_row_mean):
    mu = mean(z)
    xc = z - mu
    var = mean(xc * xc)
    rstd = lax.rsqrt(var + LN_EPS)
    xhat = xc * rstd
    dxh = dy * g
    m1 = mean(dxh)
    m2 = mean(dxh * xhat)
    return rstd * (dxh - m1 - xhat * m2), xhat


GELU_K = 0.7978845608028654
GELU_C = 0.044715


def _gelu(x):
    return 0.5 * x * (1.0 + jnp.tanh(GELU_K * (x + GELU_C * x * x * x)))


def _gelu_grad(x):
    t = jnp.tanh(GELU_K * (x + GELU_C * x * x * x))
    return 0.5 * (1.0 + t) + 0.5 * x * (1.0 - t * t) * GELU_K * (1.0 + 3.0 * GELU_C * x * x)


def _softplus(y):
    return jnp.maximum(y, 0.0) + jnp.log1p(jnp.exp(-jnp.abs(y)))


def _neg_expm1(y):
    series = -y * (1.0 + y * (0.5 + y * (1.0 / 6.0 + y * (1.0 / 24.0 + y * (1.0 / 120.0 + y * (1.0 / 720.0))))))
    return jnp.where(y > -0.25, series, 1.0 - jnp.exp(y))


def _colsum(x):
    return jnp.sum(x, axis=0, keepdims=True)


def _shifted_taps(src_ref, base, rows, taps):
    acc = None
    for o, coef in taps:
        term = coef() * src_ref[pl.ds(base + o, rows), :]
        acc = term if acc is None else acc + term
    return acc


def _shifted_corr(src_ref, base, rows, d, acc_ref, offs):
    for k, o in enumerate(offs):
        prod = d * src_ref[pl.ds(base + o, rows), :]
        part = jnp.sum(prod.reshape(rows // SUBLANES, SUBLANES, prod.shape[-1]), axis=0)
        acc_ref[SUBLANES * k:SUBLANES * (k + 1), :] += part


def _front_pad(ktaps):
    return SUBLANES * ((ktaps - 1 + SUBLANES - 1) // SUBLANES)


def _pad_rows(ktaps):
    return _front_pad(ktaps) + SUBLANES


def _ffn_tiles(S, F):
    tm = _divisor_tile(S, 1024, 16)
    tf = _divisor_tile(F, 256, LANES)
    return tm, tf


def _ffn_fwd(x, wgu, wd, g, b, alpha, name):
    S, D = x.shape
    F = wd[0].shape[1]
    tm, tf = _ffn_tiles(S, F)
    nf = F // tf
    wg_arr, wg_spec = _layer_block(wgu, (D, tf), lambda i, j: (0, j))
    wu_arr, wu_spec = _layer_block(wgu, (D, tf), lambda i, j: (0, nf + j))
    wd_arr, wd_spec = _layer_block(wd, (tf, D), lambda i, j: (j, 0))

    def body(x_ref, wg_ref, wu_ref, wd_ref, g_ref, b_ref, y_ref, z_ref, hg_ref, hu_ref, acc_ref, xb_ref):
        j = pl.program_id(1)

        @pl.when(j == 0)
        def _():
            xb_ref[...] = x_ref[...].astype(BF16)
            acc_ref[...] = jnp.zeros_like(acc_ref)

        xb = xb_ref[...]
        hg = _dot(xb, wg_ref[...])
        hu = _dot(xb, wu_ref[...])
        hg_ref[...] = hg
        hu_ref[...] = hu
        a = (hg * _sigmoid(hg) * hu).astype(BF16)
        acc_ref[...] += _dot(a, wd_ref[...])

        @pl.when(j == nf - 1)
        def _():
            z = alpha * x_ref[...] + 0.5 * acc_ref[...]
            z_ref[...] = z
            y_ref[...] = _norm_fwd(z, g_ref[...], b_ref[...])

    blk = 2 * (3 * tm * D * 4 + 2 * tm * tf * 4 + 3 * D * tf * 2) + tm * D * 6 + 3 * tm * tf * 4
    return _streamed_call(
        body, name=name, grid=(S // tm, nf),
        in_specs=[_bs((tm, D), lambda i, j: (i, 0)), wg_spec, wu_spec, wd_spec,
                  _bs((1, D), lambda i, j: (0, 0)), _bs((1, D), lambda i, j: (0, 0))],
        out_specs=[_bs((tm, D), lambda i, j: (i, 0)), _bs((tm, D), lambda i, j: (i, 0)),
                   _bs((tm, tf), lambda i, j: (i, j)), _bs((tm, tf), lambda i, j: (i, j))],
        out_shape=[jax.ShapeDtypeStruct((S, D), F32), jax.ShapeDtypeStruct((S, D), F32),
                   jax.ShapeDtypeStruct((S, F), F32), jax.ShapeDtypeStruct((S, F), F32)],
        scratch_shapes=[pltpu.VMEM((tm, D), F32), pltpu.VMEM((tm, D), BF16)],
        compiler_params=_cparams(blk),
    )(x, wg_arr, wu_arr, wd_arr, g, b)


def _ffn_bwd(dy, z, hg, hu, wgu, wd, g, alpha, name):
    S, D = dy.shape
    F = wd[0].shape[1]
    tm, tf = _ffn_tiles(S, F)
    nf = F // tf
    wg_arr, wg_spec = _layer_block(wgu, (D, tf), lambda i, j: (0, j))
    wu_arr, wu_spec = _layer_block(wgu, (D, tf), lambda i, j: (0, nf + j))
    wd_arr, wd_spec = _layer_block(wd, (tf, D), lambda i, j: (j, 0))

    def body(dy_ref, z_ref, hg_ref, hu_ref, wg_ref, wu_ref, wd_ref, g_ref,
             dx_ref, df_ref, a_ref, dhg_ref, dhu_ref, dg_ref, db_ref, acc_ref):
        i = pl.program_id(0)
        j = pl.program_id(1)

        @pl.when((i == 0) & (j == 0))
        def _():
            dg_ref[...] = jnp.zeros_like(dg_ref)
            db_ref[...] = jnp.zeros_like(db_ref)

        @pl.when(j == 0)
        def _():
            dy_ = dy_ref[...]
            dz, xhat = _norm_bwd(z_ref[...], g_ref[...], dy_)
            dg_ref[...] += _colsum(dy_ * xhat)
            db_ref[...] += _colsum(dy_)
            acc_ref[...] = alpha * dz
            df_ref[...] = (0.5 * dz).astype(BF16)

        da = _dot_nt(df_ref[...], wd_ref[...])
        hg_ = hg_ref[...]
        hu_ = hu_ref[...]
        s = _sigmoid(hg_)
        sl = hg_ * s
        dgate = (da * hu_ * (s * (1.0 + hg_ * (1.0 - s)))).astype(BF16)
        dup = (da * sl).astype(BF16)
        a_ref[...] = (sl * hu_).astype(BF16)
        dhg_ref[...] = dgate
        dhu_ref[...] = dup
        acc_ref[...] += _dot_nt(dgate, wg_ref[...]) + _dot_nt(dup, wu_ref[...])

        @pl.when(j == nf - 1)
        def _():
            dx_ref[...] = acc_ref[...]

    blk = 2 * (2 * tm * D * 4 + tm * D * 2 + 2 * tm * tf * 4 + 3 * tm * tf * 2 + 3 * D * tf * 2) + 3 * tm * D * 4 + 8 * tm * tf * 4
    once = dict(pipeline_mode=pl.Buffered(1))
    return _streamed_call(
        body, name=name, grid=(S // tm, nf),
        in_specs=[_bs((tm, D), lambda i, j: (i, 0), **once), _bs((tm, D), lambda i, j: (i, 0), **once),
                  _bs((tm, tf), lambda i, j: (i, j)), _bs((tm, tf), lambda i, j: (i, j)),
                  wg_spec, wu_spec, wd_spec, _bs((1, D), lambda i, j: (0, 0))],
        out_specs=[_bs((tm, D), lambda i, j: (i, 0)), _bs((tm, D), lambda i, j: (i, 0)),
                   _bs((tm, tf), lambda i, j: (i, j)), _bs((tm, tf), lambda i, j: (i, j)), _bs((tm, tf), lambda i, j: (i, j)),
                   _bs((1, D), lambda i, j: (0, 0)), _bs((1, D), lambda i, j: (0, 0))],
        out_shape=[jax.ShapeDtypeStruct((S, D), F32), jax.ShapeDtypeStruct((S, D), BF16),
                   jax.ShapeDtypeStruct((S, F), BF16), jax.ShapeDtypeStruct((S, F), BF16), jax.ShapeDtypeStruct((S, F), BF16),
                   jax.ShapeDtypeStruct((1, D), F32), jax.ShapeDtypeStruct((1, D), F32)],
        scratch_shapes=[pltpu.VMEM((tm, D), F32)],
        compiler_params=_cparams(blk),
    )(dy, z, hg, hu, wg_arr, wu_arr, wd_arr, g)


def _mm_tn(a, b, name):
    S, M = a.shape
    N = b.shape[1]
    bm = _divisor_tile(M, 1408, LANES)
    bn = _divisor_tile(N, 1408, LANES)
    tk = _divisor_tile(S, 512, 16)
    nk = S // tk

    def body(a_ref, b_ref, o_ref, ob_ref):
        k = pl.program_id(2)

        @pl.when(k == 0)
        def _():
            o_ref[...] = jnp.zeros_like(o_ref)

        o_ref[...] += _dot_tn(a_ref[...].astype(BF16), b_ref[...].astype(BF16))

        @pl.when(k == nk - 1)
        def _():
            ob_ref[...] = o_ref[...].astype(BF16)

    blk = 2 * (tk * bm * a.dtype.itemsize + tk * bn * b.dtype.itemsize + bm * bn * 6) + tk * bm * 4 + bm * bn * 4
    tile = _bs((bm, bn), lambda i, j, k: (i, j))
    return _streamed_call(
        body, name=name, grid=(M // bm, N // bn, nk),
        in_specs=[_bs((tk, bm), lambda i, j, k: (k, i)), _bs((tk, bn), lambda i, j, k: (k, j))],
        out_specs=[tile, tile],
        out_shape=[jax.ShapeDtypeStruct((M, N), F32), jax.ShapeDtypeStruct((M, N), BF16)],
        compiler_params=_cparams(blk),
    )(a, b)


def _mm_tn_pair(a, b0, b1, name):
    S, M = a.shape
    N = b0.shape[1]
    assert b1.shape == b0.shape
    bm = _divisor_tile(M, 1408, LANES)
    bn = _divisor_tile(N, 1408, LANES)
    tk = _divisor_tile(S, 512, 16)
    nb = N // bn
    nk = S // tk

    def body(a_ref, b0_ref, b1_ref, o_ref, ob_ref):
        j = pl.program_id(1)
        k = pl.program_id(2)

        @pl.when(k == 0)
        def _():
            o_ref[...] = jnp.zeros_like(o_ref)

        ab = a_ref[...].astype(BF16)

        @pl.when(j < nb)
        def _():
            o_ref[...] += _dot_tn(ab, b0_ref[...])

        @pl.when(j >= nb)
        def _():
            o_ref[...] += _dot_tn(ab, b1_ref[...])

        @pl.when(k == nk - 1)
        def _():
            ob_ref[...] = o_ref[...].astype(BF16)

    b0_map = lambda i, j, k: (jnp.where(j < nb, k, nk - 1), jnp.minimum(j, nb - 1))
    b1_map = lambda i, j, k: (jnp.where(j >= nb, k, 0), jnp.maximum(j - nb, 0))
    blk = 2 * (tk * bm * a.dtype.itemsize + 2 * tk * bn * 2 + bm * bn * 6) + tk * bm * 4 + bm * bn * 4
    tile = _bs((bm, bn), lambda i, j, k: (i, j))
    return _streamed_call(
        body, name=name, grid=(M // bm, 2 * nb, nk),
        in_specs=[_bs((tk, bm), lambda i, j, k: (k, i)), _bs((tk, bn), b0_map), _bs((tk, bn), b1_map)],
        out_specs=[tile, tile],
        out_shape=[jax.ShapeDtypeStruct((M, 2 * N), F32), jax.ShapeDtypeStruct((M, 2 * N), BF16)],
        compiler_params=_cparams(blk),
    )(a, b0, b1)


def _mix_in(x, wt, bias, sections, name):
    S, D = x.shape
    tm = _divisor_tile(S, 256, 16)
    n = len(sections)

    def body(x_ref, w_ref, b_ref, *o_refs):
        xb = x_ref[...].astype(BF16)
        for (off, width, dtype), o_ref in zip(sections, o_refs):
            o_ref[...] = (_dot_nt(xb, w_ref[off:off + width, :]) + b_ref[:, off:off + width]).astype(dtype)

    total = wt.shape[0]
    blk = 2 * (tm * D * 4 + sum(tm * w * jnp.dtype(dt).itemsize for _, w, dt in sections)) + total * D * 2 + 3 * tm * D * 4
    return _streamed_call(
        body, name=name, grid=(S // tm,),
        in_specs=[_bs((tm, D), lambda i: (i, 0)), _resident((total, D)), _resident((1, total))],
        out_specs=[_bs((tm, w), lambda i: (i, 0)) for _, w, _ in sections],
        out_shape=[jax.ShapeDtypeStruct((S, w), dt) for _, w, dt in sections],
        compiler_params=_cparams(blk),
    )(x, wt, bias)


def _mix_dx(dz, parts, wt, sections, alpha, name):
    S, D = dz.shape
    tm = _divisor_tile(S, 256, 16)
    n = len(parts)

    def body(*refs):
        dz_ref = refs[0]
        p_refs = refs[1:1 + n]
        w_ref = refs[1 + n]
        o_ref = refs[2 + n]
        acc = alpha * dz_ref[...]
        for p_ref, (off, width, _) in zip(p_refs, sections):
            acc = acc + _dot(p_ref[...], w_ref[off:off + width, :])
        o_ref[...] = acc

    widths = [p.shape[1] for p in parts]
    total = wt.shape[0]
    blk = 2 * (2 * tm * D * 4 + sum(tm * w * 2 for w in widths)) + total * D * 2 + 2 * tm * D * 4
    return _streamed_call(
        body, name=name, grid=(S // tm,),
        in_specs=[_bs((tm, D), lambda i: (i, 0))] + [_bs((tm, w), lambda i: (i, 0)) for w in widths]
                 + [_resident((total, D))],
        out_specs=_bs((tm, D), lambda i: (i, 0)),
        out_shape=jax.ShapeDtypeStruct((S, D), F32),
        compiler_params=_cparams(blk),
    )(dz, *parts, wt)


def _conv_branch_fwd(cv, cg, w, b, gg, gb, name):
    S, C = cv.shape
    K = w.shape[0]
    assert C // CONV_GROUPS == LANES
    padf = _front_pad(K)
    R = min(CONV_ROWS, S)
    E = min(EW_ROWS, S)

    def body(cv_ref, cg_ref, w_ref, b_ref, gg_ref, gb_ref, c2_ref, c4_ref, pad_ref):
        pad_ref[0:padf, :] = jnp.zeros((padf, LANES), F32)
        pad_ref[S + padf:S + padf + SUBLANES, :] = jnp.zeros((SUBLANES, LANES), F32)

        def fill(i, carry):
            r = pl.multiple_of(i * E, E)
            pad_ref[pl.ds(r + padf, E), :] = cv_ref[pl.ds(r, E), :] * _sigmoid(cg_ref[pl.ds(r, E), :])
            return carry

        lax.fori_loop(0, S // E, fill, 0)
        taps = [(padf - (K - 1) + k, functools.partial(lambda k: w_ref[k:k + 1, :], k)) for k in range(K)]

        def conv(i, carry):
            r = pl.multiple_of(i * R, R)
            c2_ref[pl.ds(r, R), :] = _shifted_taps(pad_ref, r, R, taps) + b_ref[...]
            return carry

        lax.fori_loop(0, S // R, conv, 0)

        def norm(i, carry):
            r = pl.multiple_of(i * E, E)
            c3 = _norm_fwd(c2_ref[pl.ds(r, E), :], gg_ref[...], gb_ref[...], _lane_mean)
            c4_ref[pl.ds(r, E), :] = (c3 * _sigmoid(c3)).astype(BF16)
            return carry

        lax.fori_loop(0, S // E, norm, 0)

    col = lambda i: (0, i)
    blk = 2 * (3 * S * LANES * 4 + S * LANES * 2) + (S + _pad_rows(K)) * LANES * 4
    return _streamed_call(
        body, name=name, grid=(C // LANES,),
        in_specs=[_bs((S, LANES), col), _bs((S, LANES), col), _bs((K, LANES), col),
                  _bs((1, LANES), col), _bs((1, LANES), col), _bs((1, LANES), col)],
        out_specs=[_bs((S, LANES), col), _bs((S, LANES), col)],
        out_shape=[jax.ShapeDtypeStruct((S, C), F32), jax.ShapeDtypeStruct((S, C), BF16)],
        scratch_shapes=[pltpu.VMEM((S + _pad_rows(K), LANES), F32)],
        compiler_params=_cparams(blk),
    )(cv, cg, w, b, gg, gb)


def _conv_branch_bwd(dc4, c2, cv, cg, w, gg, gb, name):
    S, C = cv.shape
    K = w.shape[0]
    padf = _front_pad(K)
    R = min(CONV_ROWS, S)
    E = min(EW_ROWS, S)

    def body(dc4_ref, c2_ref, cv_ref, cg_ref, w_ref, gg_ref, gb_ref,
             dcv_ref, dcg_ref, dw_ref, dwb_ref, dgg_ref, dgb_ref, scv_ref, scg_ref,
             dpad_ref, cpad_ref, dwacc_ref):
        cpad_ref[0:padf, :] = jnp.zeros((padf, LANES), F32)
        cpad_ref[S + padf:S + padf + SUBLANES, :] = jnp.zeros((SUBLANES, LANES), F32)
        dpad_ref[S:S + padf + SUBLANES, :] = jnp.zeros((padf + SUBLANES, LANES), F32)
        dwacc_ref[...] = jnp.zeros_like(dwacc_ref)
        for ref in (dwb_ref, dgg_ref, dgb_ref, scv_ref, scg_ref):
            ref[...] = jnp.zeros_like(ref)

        def norm_pass(i, carry):
            r = pl.multiple_of(i * E, E)
            g_ = gg_ref[...]
            c2 = c2_ref[pl.ds(r, E), :]
            xc = c2 - _lane_mean(c2)
            rstd = lax.rsqrt(_lane_mean(xc * xc) + LN_EPS)
            xhat = xc * rstd
            c3 = xhat * g_ + gb_ref[...]
            s = _sigmoid(c3)
            dc3 = dc4_ref[pl.ds(r, E), :].astype(F32) * (s * (1.0 + c3 * (1.0 - s)))
            dgg_ref[...] += _colsum(dc3 * xhat)
            dgb_ref[...] += _colsum(dc3)
            dxh = dc3 * g_
            dc2 = rstd * (dxh - _lane_mean(dxh) - xhat * _lane_mean(dxh * xhat))
            dpad_ref[pl.ds(r, E), :] = dc2
            dwb_ref[...] += _colsum(dc2)
            cpad_ref[pl.ds(r + padf, E), :] = cv_ref[pl.ds(r, E), :] * _sigmoid(cg_ref[pl.ds(r, E), :])
            return carry

        lax.fori_loop(0, S // E, norm_pass, 0)
        taps = [(K - 1 - k, functools.partial(lambda k: w_ref[k:k + 1, :], k)) for k in range(K)]
        offs = [padf - (K - 1) + k for k in range(K)]

        def conv_pass(i, carry):
            r = pl.multiple_of(i * R, R)
            dc1 = _shifted_taps(dpad_ref, r, R, taps)
            sg = _sigmoid(cg_ref[pl.ds(r, R), :])
            cv_ = cv_ref[pl.ds(r, R), :]
            dcv = dc1 * sg
            dcg = dc1 * cv_ * sg * (1.0 - sg)
            dcv_ref[pl.ds(r, R), :] = dcv.astype(BF16)
            dcg_ref[pl.ds(r, R), :] = dcg.astype(BF16)
            scv_ref[...] += _colsum(dcv)
            scg_ref[...] += _colsum(dcg)
            _shifted_corr(cpad_ref, r, R, dpad_ref[pl.ds(r, R), :], dwacc_ref, offs)
            return carry

        lax.fori_loop(0, S // R, conv_pass, 0)
        for k in range(K):
            dw_ref[k:k + 1, :] = _colsum(dwacc_ref[SUBLANES * k:SUBLANES * (k + 1), :])

    col = lambda i: (0, i)
    row = jax.ShapeDtypeStruct((1, C), F32)
    blk = 2 * (4 * S * LANES * 4 + 2 * S * LANES * 2) + 2 * (S + _pad_rows(K)) * LANES * 4
    return _streamed_call(
        body, name=name, grid=(C // LANES,),
        in_specs=[_bs((S, LANES), col)] * 4 + [_bs((K, LANES), col), _bs((1, LANES), col), _bs((1, LANES), col)],
        out_specs=[_bs((S, LANES), col), _bs((S, LANES), col), _bs((K, LANES), col)] + [_bs((1, LANES), col)] * 5,
        out_shape=[jax.ShapeDtypeStruct((S, C), BF16), jax.ShapeDtypeStruct((S, C), BF16),
                   jax.ShapeDtypeStruct((K, C), F32), row, row, row, row, row],
        scratch_shapes=[pltpu.VMEM((S + _pad_rows(K), LANES), F32), pltpu.VMEM((S + _pad_rows(K), LANES), F32),
                        pltpu.VMEM((SUBLANES * K, LANES), F32)],
        compiler_params=_cparams(blk),
    )(dc4, c2, cv, cg, w, gg, gb)


def _short_conv_fwd(xin, w, b, name):
    S, C = xin.shape
    K = w.shape[0]
    padf = _front_pad(K)
    R = min(CONV_ROWS, S)
    E = min(EW_ROWS, S)

    def body(x_ref, w_ref, b_ref, o_ref, pad_ref):
        pad_ref[0:padf, :] = jnp.zeros((padf, LANES), F32)
        pad_ref[S + padf:S + padf + SUBLANES, :] = jnp.zeros((SUBLANES, LANES), F32)

        def fill(i, carry):
            r = pl.multiple_of(i * E, E)
            pad_ref[pl.ds(r + padf, E), :] = x_ref[pl.ds(r, E), :]
            return carry

        lax.fori_loop(0, S // E, fill, 0)
        taps = [(padf - (K - 1) + k, functools.partial(lambda k: w_ref[k:k + 1, :], k)) for k in range(K)]

        def conv(i, carry):
            r = pl.multiple_of(i * R, R)
            o_ref[pl.ds(r, R), :] = _shifted_taps(pad_ref, r, R, taps) + b_ref[...]
            return carry

        lax.fori_loop(0, S // R, conv, 0)

    col = lambda i: (0, i)
    blk = 2 * (2 * S * LANES * 4) + (S + _pad_rows(K)) * LANES * 4
    return _streamed_call(
        body, name=name, grid=(C // LANES,),
        in_specs=[_bs((S, LANES), col), _bs((K, LANES), col), _bs((1, LANES), col)],
        out_specs=_bs((S, LANES), col),
        out_shape=jax.ShapeDtypeStruct((S, C), F32),
        scratch_shapes=[pltpu.VMEM((S + _pad_rows(K), LANES), F32)],
        compiler_params=_cparams(blk),
    )(xin, w, b)


def _short_conv_bwd(dy, xin, w, name):
    S, C = xin.shape
    K = w.shape[0]
    padf = _front_pad(K)
    R = min(CONV_ROWS, S)
    E = min(EW_ROWS, S)

    def body(dy_ref, x_ref, w_ref, dx_ref, dw_ref, db_ref, sx_ref, dpad_ref, xpad_ref, dwacc_ref):
        xpad_ref[0:padf, :] = jnp.zeros((padf, LANES), F32)
        xpad_ref[S + padf:S + padf + SUBLANES, :] = jnp.zeros((SUBLANES, LANES), F32)
        dpad_ref[S:S + padf + SUBLANES, :] = jnp.zeros((padf + SUBLANES, LANES), F32)
        dwacc_ref[...] = jnp.zeros_like(dwacc_ref)
        db_ref[...] = jnp.zeros_like(db_ref)
        sx_ref[...] = jnp.zeros_like(sx_ref)

        def fill(i, carry):
            r = pl.multiple_of(i * E, E)
            d = dy_ref[pl.ds(r, E), :]
            dpad_ref[pl.ds(r, E), :] = d
            db_ref[...] += _colsum(d)
            xpad_ref[pl.ds(r + padf, E), :] = x_ref[pl.ds(r, E), :]
            return carry

        lax.fori_loop(0, S // E, fill, 0)
        taps = [(K - 1 - k, functools.partial(lambda k: w_ref[k:k + 1, :], k)) for k in range(K)]
        offs = [padf - (K - 1) + k for k in range(K)]

        def conv_pass(i, carry):
            r = pl.multiple_of(i * R, R)
            dx = _shifted_taps(dpad_ref, r, R, taps)
            dx_ref[pl.ds(r, R), :] = dx.astype(BF16)
            sx_ref[...] += _colsum(dx)
            _shifted_corr(xpad_ref, r, R, dpad_ref[pl.ds(r, R), :], dwacc_ref, offs)
            return carry

        lax.fori_loop(0, S // R, conv_pass, 0)
        for k in range(K):
            dw_ref[k:k + 1, :] = _colsum(dwacc_ref[SUBLANES * k:SUBLANES * (k + 1), :])

    col = lambda i: (0, i)
    row = jax.ShapeDtypeStruct((1, C), F32)
    blk = 2 * (2 * S * LANES * 4 + S * LANES * 2) + 2 * (S + _pad_rows(K)) * LANES * 4
    return _streamed_call(
        body, name=name, grid=(C // LANES,),
        in_specs=[_bs((S, LANES), col), _bs((S, LANES), col), _bs((K, LANES), col)],
        out_specs=[_bs((S, LANES), col), _bs((K, LANES), col), _bs((1, LANES), col), _bs((1, LANES), col)],
        out_shape=[jax.ShapeDtypeStruct((S, C), BF16), jax.ShapeDtypeStruct((K, C), F32), row, row],
        scratch_shapes=[pltpu.VMEM((S + _pad_rows(K), LANES), F32), pltpu.VMEM((S + _pad_rows(K), LANES), F32),
                        pltpu.VMEM((SUBLANES * K, LANES), F32)],
        compiler_params=_cparams(blk),
    )(dy, xin, w)


def _band_panels(width, block):
    assert width % LANES == 0 and block <= LANES
    panels = []
    for c0 in range(0, width, 2 * LANES):
        c1 = min(width, c0 + 2 * LANES)
        r0 = (c0 // block) * block // LANES * LANES
        r1 = min(width, -(-(-(-c1 // block) * block) // LANES) * LANES)
        panels.append((r0, r1, c0, c1))
    return panels


def _gates_fwd(r1, wa, wx, ba, bx, lam, name):
    S, R = r1.shape
    tm = _divisor_tile(S, 256, 16)
    panels = _band_panels(R, R // RNN_BLOCKS)

    def body(r1_ref, wa_ref, wx_ref, ba_ref, bx_ref, lam_ref, ra_ref, ri_ref, a_ref, uu_ref):
        for r0, r1e, c0, c1 in panels:
            rb = r1_ref[:, r0:r1e].astype(BF16)
            ra = _sigmoid(_dot(rb, wa_ref[r0:r1e, c0:c1]) + ba_ref[:, c0:c1])
            ri = _sigmoid(_dot(rb, wx_ref[r0:r1e, c0:c1]) + bx_ref[:, c0:c1])
            log_a = -RG_LRU_C * ra * _softplus(-lam_ref[:, c0:c1])
            ra_ref[:, c0:c1] = ra
            ri_ref[:, c0:c1] = ri
            a_ref[:, c0:c1] = jnp.exp(log_a)
            uu_ref[:, c0:c1] = jnp.sqrt(_neg_expm1(2.0 * log_a)) * (ri * r1_ref[:, c0:c1])

    blk = 2 * (5 * tm * R * 4) + 2 * R * R * 2 + 6 * tm * R * 4
    tile = _bs((tm, R), lambda i: (i, 0))
    return _streamed_call(
        body, name=name, grid=(S // tm,),
        in_specs=[tile, _resident((R, R)), _resident((R, R)), _resident((1, R)), _resident((1, R)), _resident((1, R))],
        out_specs=[tile] * 4,
        out_shape=[jax.ShapeDtypeStruct((S, R), F32)] * 4,
        compiler_params=_cparams(blk),
    )(r1, wa, wx, ba, bx, lam)


def _gates_bwd(guu, da, ra, ri, r1, wa, wx, lam, name):
    S, R = r1.shape
    tm = _divisor_tile(S, 256, 16)
    nsteps = S // tm
    panels = _band_panels(R, R // RNN_BLOCKS)

    def body(g_ref, da_ref, ra_ref, ri_ref, r1_ref, wa_ref, wx_ref, lam_ref,
             dr1_ref, dpa_ref, dpx_ref, dba_ref, dbx_ref, dlam_ref):
        i = pl.program_id(0)

        @pl.when(i == 0)
        def _():
            dba_ref[...] = jnp.zeros_like(dba_ref)
            dbx_ref[...] = jnp.zeros_like(dbx_ref)
            dlam_ref[...] = jnp.zeros_like(dlam_ref)

        g = g_ref[...]
        ra = ra_ref[...]
        ri = ri_ref[...]
        r1_ = r1_ref[...]
        sp = _softplus(-lam_ref[...])
        log_a = -RG_LRU_C * ra * sp
        a = jnp.exp(log_a)
        mult = jnp.sqrt(_neg_expm1(2.0 * log_a))
        d_ri = g * mult * r1_
        dr1 = g * mult * ri
        dmult = g * ri * r1_
        dlog_a = da_ref[...] * a - dmult * (a * a) / mult
        dra = dlog_a * (-RG_LRU_C * sp)
        dlam_ref[...] += _colsum(dlog_a * (-RG_LRU_C * ra))
        dpa = dra * ra * (1.0 - ra)
        dpx = d_ri * ri * (1.0 - ri)
        dba_ref[...] += _colsum(dpa)
        dbx_ref[...] += _colsum(dpx)
        dpa_b = dpa.astype(BF16)
        dpx_b = dpx.astype(BF16)
        dpa_ref[...] = dpa_b
        dpx_ref[...] = dpx_b
        dr1_ref[...] = dr1
        for k0, k1, c0, c1 in panels:
            dr1_ref[:, c0:c1] += (_dot_nt(dpa_ref[:, k0:k1], wa_ref[c0:c1, k0:k1])
                                  + _dot_nt(dpx_ref[:, k0:k1], wx_ref[c0:c1, k0:k1]))

        @pl.when(i == nsteps - 1)
        def _():
            dlam_ref[...] = dlam_ref[...] * (-_sigmoid(-lam_ref[...]))

    blk = 2 * (6 * tm * R * 4 + 2 * tm * R * 2) + 2 * R * R * 2 + 10 * tm * R * 4
    tile = _bs((tm, R), lambda i: (i, 0))
    rowspec = _bs((1, R), lambda i: (0, 0))
    row = jax.ShapeDtypeStruct((1, R), F32)
    return _streamed_call(
        body, name=name, grid=(nsteps,),
        in_specs=[tile] * 5 + [_resident((R, R)), _resident((R, R)), _resident((1, R))],
        out_specs=[tile, tile, tile, rowspec, rowspec, rowspec],
        out_shape=[jax.ShapeDtypeStruct((S, R), F32), jax.ShapeDtypeStruct((S, R), BF16), jax.ShapeDtypeStruct((S, R), BF16),
                   row, row, row],
        compiler_params=_cparams(blk),
    )(guu, da, ra, ri, r1, wa, wx, lam)


def _embed_blocks(w, name):
    H, bk, _ = w.shape

    def body(w_ref, o_ref):
        o_ref[...] = jnp.zeros_like(o_ref)
        for h in range(H):
            o_ref[bk * h:bk * (h + 1), bk * h:bk * (h + 1)] = w_ref[h].astype(BF16)

    return pl.pallas_call(body, name=name, out_shape=jax.ShapeDtypeStruct((H * bk, H * bk), BF16),
                          compiler_params=_cparams(3 * H * bk * H * bk * 2))(w)


def _block_grads(r1, dpa, dpx, name):
    S, R = r1.shape
    bk = R // RNN_BLOCKS
    tk = _divisor_tile(S, 512, 16)
    nsteps = S // tk
    panels = _band_panels(R, bk)

    def body(r1_ref, dpa_ref, dpx_ref, ga_ref, gx_ref, acca_ref, accx_ref):
        k = pl.program_id(0)

        @pl.when(k == 0)
        def _():
            acca_ref[...] = jnp.zeros_like(acca_ref)
            accx_ref[...] = jnp.zeros_like(accx_ref)

        for k0, k1, c0, c1 in panels:
            rb = r1_ref[:, k0:k1].astype(BF16)
            acca_ref[k0:k1, c0:c1] += _dot_tn(rb, dpa_ref[:, c0:c1])
            accx_ref[k0:k1, c0:c1] += _dot_tn(rb, dpx_ref[:, c0:c1])

        @pl.when(k == nsteps - 1)
        def _():
            for h in range(RNN_BLOCKS):
                ga_ref[h] = acca_ref[bk * h:bk * (h + 1), bk * h:bk * (h + 1)]
                gx_ref[h] = accx_ref[bk * h:bk * (h + 1), bk * h:bk * (h + 1)]

    tile = lambda: _bs((tk, R), lambda k: (k, 0))
    out = _bs((RNN_BLOCKS, bk, bk), lambda k: (0, 0, 0))
    sds = jax.ShapeDtypeStruct((RNN_BLOCKS, bk, bk), F32)
    return _streamed_call(
        body, name=name, grid=(nsteps,),
        in_specs=[tile(), tile(), tile()], out_specs=[out, out], out_shape=[sds, sds],
        scratch_shapes=[pltpu.VMEM((R, R), F32), pltpu.VMEM((R, R), F32)],
        compiler_params=_cparams(2 * (tk * R * 8) + 2 * R * R * 4 + 4 * tk * R * 4),
    )(r1, dpa, dpx)


def _scan_geometry(S):
    nseg = SCAN_SEGMENTS if S % (SCAN_SEGMENTS * SUBLANES) == 0 else SUBLANES
    return nseg, S // nseg


def _steps(n, step, init):
    u = SCAN_UNROLL

    def trip(t, carry):
        for k in range(u):
            carry = step(t * u + k, carry)
        return carry

    carry = lax.fori_loop(0, n // u, trip, init)
    for j in range(n - n % u, n):
        carry = step(j, carry)
    return carry


def _scan_fwd(a, u, name):
    S, C = a.shape
    nseg, L = _scan_geometry(S)
    T = min(SUBLANES, L)

    def body(a3, u3, h3, ta_ref, tu_ref, e_ref, p_ref, init_ref):

        def to_steps(i, carry):
            j0 = pl.multiple_of(i * T, T)
            ta_ref[pl.ds(j0, T)] = jnp.swapaxes(a3[:, pl.ds(j0, T), :], 0, 1)
            tu_ref[pl.ds(j0, T)] = jnp.swapaxes(u3[:, pl.ds(j0, T), :], 0, 1)
            return carry

        lax.fori_loop(0, L // T, to_steps, 0)

        def run1(j, carry):
            hs, ps = carry
            aj = ta_ref[j]
            return aj * hs + tu_ref[j], aj * ps

        e_ref[...], p_ref[...] = _steps(L, run1, (jnp.zeros((nseg, LANES), F32), jnp.ones((nseg, LANES), F32)))
        init_ref[0:1, :] = jnp.zeros((1, LANES), F32)
        for s in range(1, nseg):
            init_ref[s:s + 1, :] = e_ref[s - 1:s, :] + p_ref[s - 1:s, :] * init_ref[s - 1:s, :]

        def run2(j, hs):
            hs = ta_ref[j] * hs + tu_ref[j]
            tu_ref[j] = hs
            return hs

        _steps(L, run2, init_ref[...])

        def from_steps(i, carry):
            j0 = pl.multiple_of(i * T, T)
            h3[:, pl.ds(j0, T), :] = jnp.swapaxes(tu_ref[pl.ds(j0, T)], 0, 1)
            return carry

        lax.fori_loop(0, L // T, from_steps, 0)

    seg_block = _bs((nseg, L, LANES), lambda i: (0, 0, i))
    blk = 2 * (3 * S * LANES * 4) + 2 * S * LANES * 4
    return _streamed_call(
        body, name=name, grid=(C // LANES,),
        in_specs=[seg_block, seg_block],
        out_specs=seg_block,
        out_shape=jax.ShapeDtypeStruct((nseg, L, C), F32),
        scratch_shapes=[pltpu.VMEM((L, nseg, LANES), F32)] * 2 + [pltpu.VMEM((nseg, LANES), F32)] * 3,
        compiler_params=_cparams(blk),
    )(a.reshape(nseg, L, C), u.reshape(nseg, L, C)).reshape(S, C)


def _scan_bwd(a, dh, h, name):
    S, C = a.shape
    nseg, L = _scan_geometry(S)
    T = min(SUBLANES, L)
    assert L >= 2

    def body(a3, d3, h3, g3, da3, ta_ref, td_ref, th_ref, e_ref, p_ref, init_ref):

        def to_steps(i, carry):
            j0 = pl.multiple_of(i * T, T)
            for src, dst in ((a3, ta_ref), (d3, td_ref), (h3, th_ref)):
                dst[pl.ds(j0, T)] = jnp.swapaxes(src[:, pl.ds(j0, T), :], 0, 1)
            return carry

        lax.fori_loop(0, L // T, to_steps, 0)
        seg = lax.broadcasted_iota(jnp.int32, (nseg, LANES), 0)
        b_last = jnp.where(seg == nseg - 1, 0.0, pltpu.roll(ta_ref[0], nseg - 1, axis=0))
        h_first = jnp.where(seg == 0, 0.0, pltpu.roll(th_ref[L - 1], 1, axis=0))

        def run1(jj, carry):
            gs, ps = carry
            j = L - 2 - jj
            bj = ta_ref[j + 1]
            return bj * gs + td_ref[j], bj * ps

        e_ref[...], p_ref[...] = _steps(L - 1, run1, (td_ref[L - 1], b_last))
        init_ref[nseg - 1:nseg, :] = jnp.zeros((1, LANES), F32)
        for s in range(nseg - 2, -1, -1):
            init_ref[s:s + 1, :] = e_ref[s + 1:s + 2, :] + p_ref[s + 1:s + 2, :] * init_ref[s + 1:s + 2, :]

        gs = b_last * init_ref[...] + td_ref[L - 1]
        td_ref[L - 1] = gs
        th_ref[L - 1] = gs * th_ref[L - 2]

        def run2(jj, gs):
            j = L - 2 - jj
            gs = ta_ref[j + 1] * gs + td_ref[j]
            td_ref[j] = gs
            th_ref[j] = gs * th_ref[j - 1]
            return gs

        gs = _steps(L - 2, run2, gs)
        gs = ta_ref[1] * gs + td_ref[0]
        td_ref[0] = gs
        th_ref[0] = gs * h_first

        def from_steps(i, carry):
            j0 = pl.multiple_of(i * T, T)
            g3[:, pl.ds(j0, T), :] = jnp.swapaxes(td_ref[pl.ds(j0, T)], 0, 1)
            da3[:, pl.ds(j0, T), :] = jnp.swapaxes(th_ref[pl.ds(j0, T)], 0, 1)
            return carry

        lax.fori_loop(0, L // T, from_steps, 0)

    seg_block = _bs((nseg, L, LANES), lambda i: (0, 0, i))
    blk = 2 * (5 * S * LANES * 4) + 3 * S * LANES * 4
    g, da = _streamed_call(
        body, name=name, grid=(C // LANES,),
        in_specs=[seg_block] * 3,
        out_specs=[seg_block] * 2,
        out_shape=[jax.ShapeDtypeStruct((nseg, L, C), F32)] * 2,
        scratch_shapes=[pltpu.VMEM((L, nseg, LANES), F32)] * 3 + [pltpu.VMEM((nseg, LANES), F32)] * 3,
        compiler_params=_cparams(blk),
    )(a.reshape(nseg, L, C), dh.reshape(nseg, L, C), h.reshape(nseg, L, C))
    return g.reshape(S, C), da.reshape(S, C)


def _mixer_out_fwd(c4, h, rg, gc, gr, x1, wc, wr, wo, g, b, alpha, name):
    S, D = x1.shape
    R = h.shape[1]
    tm = _divisor_tile(S, 256, 16)

    def body(c4_ref, h_ref, rg_ref, gc_ref, gr_ref, x_ref, wc_ref, wr_ref, wo_ref, g_ref, b_ref,
             yc_ref, yr_ref, z_ref, y_ref):
        yc = _dot(c4_ref[...], wc_ref[...])
        q = (h_ref[...] * _gelu(rg_ref[...].astype(F32))).astype(BF16)
        yr = _dot(q, wr_ref[...])
        yc_ref[...] = yc.astype(BF16)
        yr_ref[...] = yr.astype(BF16)
        m = (_sigmoid(gc_ref[...].astype(F32)) * yc + _sigmoid(gr_ref[...].astype(F32)) * yr).astype(BF16)
        z = alpha * x_ref[...] + _dot(m, wo_ref[...])
        z_ref[...] = z
        y_ref[...] = _norm_fwd(z, g_ref[...], b_ref[...])

    blk = 2 * (tm * D * 2 + 2 * tm * R * 4 + 7 * tm * D * 4) + (2 * D * D + R * D) * 2 + 6 * tm * D * 4
    td = _bs((tm, D), lambda i: (i, 0))
    tr = _bs((tm, R), lambda i: (i, 0))
    return _streamed_call(
        body, name=name, grid=(S // tm,),
        in_specs=[td, tr, tr, td, td, td, _layer_resident(wc)[1], _layer_resident(wr)[1], _layer_resident(wo)[1],
                  _resident((1, D)), _resident((1, D))],
        out_specs=[td] * 4,
        out_shape=[jax.ShapeDtypeStruct((S, D), BF16)] * 2 + [jax.ShapeDtypeStruct((S, D), F32)] * 2,
        compiler_params=_cparams(blk),
    )(c4, h, rg, gc, gr, x1, wc[0], wr[0], wo[0], g, b)


def _mixer_out_bwd(dy, z, g, wo, yc, yr, gc, gr, name):
    S, D = dy.shape
    tm = _divisor_tile(S, 256, 16)

    def body(dy_ref, z_ref, g_ref, wo_ref, yc_ref, yr_ref, gc_ref, gr_ref,
             dz_ref, dzb_ref, m_ref, dyc_ref, dyr_ref, dgc_ref, dgr_ref, sgc_ref, sgr_ref, dg_ref, db_ref):
        @pl.when(pl.program_id(0) == 0)
        def _():
            for ref in (sgc_ref, sgr_ref, dg_ref, db_ref):
                ref[...] = jnp.zeros_like(ref)

        dy_ = dy_ref[...]
        dz, xhat = _norm_bwd(z_ref[...], g_ref[...], dy_)
        dg_ref[...] += _colsum(dy_ * xhat)
        db_ref[...] += _colsum(dy_)
        dz_ref[...] = dz
        dzb = dz.astype(BF16)
        dzb_ref[...] = dzb
        dm = _dot_nt(dzb, wo_ref[...])
        yc = yc_ref[...].astype(F32)
        yr = yr_ref[...].astype(F32)
        sc = _sigmoid(gc_ref[...].astype(F32))
        sr = _sigmoid(gr_ref[...].astype(F32))
        m_ref[...] = (sc * yc + sr * yr).astype(BF16)
        dyc_ref[...] = (dm * sc).astype(BF16)
        dyr_ref[...] = (dm * sr).astype(BF16)
        dgc = dm * yc * sc * (1.0 - sc)
        dgr = dm * yr * sr * (1.0 - sr)
        dgc_ref[...] = dgc.astype(BF16)
        dgr_ref[...] = dgr.astype(BF16)
        sgc_ref[...] += _colsum(dgc)
        sgr_ref[...] += _colsum(dgr)

    blk = 2 * (7 * tm * D * 4 + 6 * tm * D * 2) + D * D * 2 + 8 * tm * D * 4
    td = _bs((tm, D), lambda i: (i, 0))
    rowspec = _bs((1, D), lambda i: (0, 0))
    row = jax.ShapeDtypeStruct((1, D), F32)
    bfd = jax.ShapeDtypeStruct((S, D), BF16)
    return _streamed_call(
        body, name=name, grid=(S // tm,),
        in_specs=[td, td, _resident((1, D)), _layer_resident(wo)[1], td, td, td, td],
        out_specs=[td] * 7 + [rowspec] * 4,
        out_shape=[jax.ShapeDtypeStruct((S, D), F32), bfd, bfd, bfd, bfd, bfd, bfd, row, row, row, row],
        compiler_params=_cparams(blk),
    )(dy, z, g, wo[0], yc, yr, gc, gr)


def _branch_bwd(dyc, dyr, wc, wr, h, rg, name):
    S, D = dyc.shape
    R = h.shape[1]
    tm = _divisor_tile(S, 256, 16)

    def body(dyc_ref, dyr_ref, wc_ref, wr_ref, h_ref, rg_ref, dc4_ref, dh_ref, drg_ref, q_ref, srg_ref):
        @pl.when(pl.program_id(0) == 0)
        def _():
            srg_ref[...] = jnp.zeros_like(srg_ref)

        dc4_ref[...] = _dot_nt(dyc_ref[...], wc_ref[...]).astype(BF16)
        dq = _dot_nt(dyr_ref[...], wr_ref[...])
        h_ = h_ref[...]
        rg_ = rg_ref[...].astype(F32)
        ge = _gelu(rg_)
        dh_ref[...] = dq * ge
        drg = dq * h_ * _gelu_grad(rg_)
        drg_ref[...] = drg.astype(BF16)
        srg_ref[...] += _colsum(drg)
        q_ref[...] = (h_ * ge).astype(BF16)

    blk = 2 * (2 * tm * D * 2 + tm * D * 4 + 3 * tm * R * 4 + 2 * tm * R * 2) + (D * D + R * D) * 2 + 6 * tm * R * 4
    td = _bs((tm, D), lambda i: (i, 0))
    tr = _bs((tm, R), lambda i: (i, 0))
    return _streamed_call(
        body, name=name, grid=(S // tm,),
        in_specs=[td, td, _layer_resident(wc)[1], _layer_resident(wr)[1], tr, tr],
        out_specs=[td, tr, tr, tr, _bs((1, R), lambda i: (0, 0))],
        out_shape=[jax.ShapeDtypeStruct((S, D), BF16), jax.ShapeDtypeStruct((S, R), F32), jax.ShapeDtypeStruct((S, R), BF16),
                   jax.ShapeDtypeStruct((S, R), BF16), jax.ShapeDtypeStruct((1, R), F32)],
        compiler_params=_cparams(blk),
    )(dyc, dyr, wc[0], wr[0], h, rg)


def _loss_head(y, target, name):
    S, D = y.shape
    tm = _divisor_tile(S, 512, 16)
    nsteps = S // tm

    def body(y_ref, t_ref, loss_ref, dy_ref, acc_ref):
        i = pl.program_id(0)

        @pl.when(i == 0)
        def _():
            acc_ref[...] = jnp.zeros_like(acc_ref)

        err = y_ref[...] - t_ref[...]
        dy_ref[...] = err * (1.0 / D)
        acc_ref[...] += _colsum(err * err)

        @pl.when(i == nsteps - 1)
        def _():
            loss_ref[...] = jnp.sum(acc_ref[...], axis=-1, keepdims=True) * (0.5 / D)

    td = _bs((tm, D), lambda i: (i, 0))
    return _streamed_call(
        body, name=name, grid=(nsteps,),
        in_specs=[td, td],
        out_specs=[_bs((1, 1), lambda i: (0, 0)), td],
        out_shape=[jax.ShapeDtypeStruct((1, 1), F32), jax.ShapeDtypeStruct((S, D), F32)],
        scratch_shapes=[pltpu.VMEM((1, D), F32)],
        compiler_params=_cparams(2 * 3 * tm * D * 4),
    )(y, target)


def _adamw_math(w, g, m, v):
    m = ADAM_B1 * m + (1.0 - ADAM_B1) * g
    v = ADAM_B2 * v + (1.0 - ADAM_B2) * (g * g)
    m_hat = m / (1.0 - ADAM_B1 ** ADAM_STEP)
    v_hat = v / (1.0 - ADAM_B2 ** ADAM_STEP)
    delta = -ADAM_LR * (m_hat / (jnp.sqrt(v_hat) + ADAM_EPS) + ADAM_WD * w)
    return delta, m, v


def _adamw_sharded(w, m, v, own, sib, rem, name):
    _, r, c = w.shape
    tr = _divisor_tile(r, max(16, (1 << 20) // (4 * c) // 16 * 16), 16)

    def body(w_ref, m_ref, v_ref, own_ref, sib_ref, rem_ref, g_ref, d_ref, nm_ref, nv_ref):
        mine = pl.program_id(0) == lax.axis_index("c")
        g = jnp.where(mine, own_ref[...], sib_ref[...]).astype(F32)
        for j in range(N_CHIPS - 1):
            g = g + rem_ref[j].astype(F32)
        delta, nm, nv = _adamw_math(w_ref[...], g, m_ref[...], v_ref[...])
        g_ref[...] = g
        d_ref[...] = delta
        nm_ref[...] = nm
        nv_ref[...] = nv

    tile = _bs((None, tr, c), lambda l, i: (l, i, 0))
    flat = _bs((tr, c), lambda l, i: (i, 0))
    sds = jax.ShapeDtypeStruct(w.shape, F32)
    return _streamed_call(
        body, name=name, grid=(2, r // tr),
        in_specs=[tile, tile, tile, flat, flat, _bs((N_CHIPS - 1, None, tr, c), lambda l, i: (0, l, i, 0))],
        out_specs=[tile] * 4,
        out_shape=[sds] * 4,
        compiler_params=_cparams(2 * (7 * tr * c * 4 + (N_CHIPS + 1) * tr * c * 2) + 6 * tr * c * 4),
    )(w, m, v, own, sib, rem)


def _adamw_flat(w, m, v, g, name):
    rows = w.shape[0]
    tr = _divisor_tile(rows, 1024, SUBLANES)

    def body(w_ref, m_ref, v_ref, g_ref, d_ref, nm_ref, nv_ref):
        delta, nm, nv = _adamw_math(w_ref[...], g_ref[...], m_ref[...], v_ref[...])
        d_ref[...] = delta
        nm_ref[...] = nm
        nv_ref[...] = nv

    tile = _bs((tr, LANES), lambda i: (i, 0))
    sds = jax.ShapeDtypeStruct(w.shape, F32)
    return _streamed_call(
        body, name=name, grid=(rows // tr,),
        in_specs=[tile] * 4, out_specs=[tile] * 3, out_shape=[sds] * 3,
        compiler_params=_cparams(2 * 7 * tr * LANES * 4),
    )(w, m, v, g)


def _pair_sum_bf16(g0, g1, theirs, name):
    rows, c = g0.shape
    tr = _divisor_tile(rows, max(16, (1 << 20) // (4 * c) // 16 * 16), 16)

    def body(g0_ref, g1_ref, t_ref, o_ref):
        mine = jnp.where(lax.axis_index("c") == 0, g0_ref[...], g1_ref[...])
        o_ref[...] = (mine + t_ref[...].astype(F32)).astype(BF16)

    tile = _bs((tr, c), lambda i: (i, 0))
    return _streamed_call(
        body, name=name, grid=(rows // tr,),
        in_specs=[tile, tile, tile], out_specs=tile, out_shape=jax.ShapeDtypeStruct((rows, c), BF16),
        compiler_params=_cparams(2 * 4 * tr * c * 4),
    )(g0, g1, theirs)


ANY = pl.BlockSpec(memory_space=pl.ANY)


def _mesh_position():
    return lax.axis_index("x"), lax.axis_index("y"), lax.axis_index("c")


def _other_chips():
    x, y, c = _mesh_position()
    chips = [(1 - x, y), (x, 1 - y), (1 - x, 1 - y)]
    return 2 * x + y, (x, y, 1 - c), chips, [2 * cx + cy for cx, cy in chips]


def _chip_slab(ref, k, width, by_cols):
    if by_cols:
        start = k * width if isinstance(k, int) else pl.multiple_of(k * width, LANES)
        return ref.at[:, pl.ds(start, width)]
    return ref.at[k]


HBM = pl.BlockSpec(memory_space=pltpu.HBM)
SEM = pl.BlockSpec(memory_space=pltpu.SEMAPHORE)
DATAFLOW = pltpu.SideEffectType.DATAFLOW_SIDE_EFFECTING
N_GATHER_COPIES = 4


def _land_shape(src, by_cols):
    return src.shape[:-1] + (N_CHIPS * src.shape[-1],) if by_cols else (N_CHIPS,) + src.shape


def _gather_copy(src_ref, land_ref, by_cols, send_sems, recv_sems, pos, j, slab, to):
    width = src_ref.shape[-1]
    return pltpu.make_async_remote_copy(src_ref=src_ref, dst_ref=_chip_slab(land_ref, slab, width, by_cols),
                                        send_sem=send_sems.at[N_GATHER_COPIES * pos + j],
                                        recv_sem=recv_sems.at[N_GATHER_COPIES * pos + j],
                                        device_id=to, device_id_type=MESH)


def _gather_start(srcs, by_cols, groups, name):
    U = len(srcs)
    G = len(groups)
    lands = [lax.empty(_land_shape(s, bc), s.dtype) for s, bc in zip(srcs, by_cols)]

    def body(*refs):
        src = refs[:U]
        land = refs[U:2 * U]
        send_sems = refs[2 * U:2 * U + G]
        recv_sems = refs[2 * U + G:2 * U + 2 * G]
        token = refs[-1]
        c = lax.axis_index("c")
        me, sibling, chips, _ = _other_chips()
        targets = [(*chip, c) for chip in chips] + [sibling]
        for g, members in enumerate(groups):
            for pos, u in enumerate(members):
                for j, to in enumerate(targets):
                    _gather_copy(src[u], land[u], by_cols[u], send_sems[g], recv_sems[g], pos, j, me, to).start()
        token[...] = jnp.zeros_like(token)

    sem_shapes = [pltpu.SemaphoreType.DMA((len(m) * N_GATHER_COPIES,)) for m in groups]
    outs = pl.pallas_call(
        body, name=name,
        out_shape=tuple(sem_shapes + sem_shapes + [pltpu.HBM(s.shape, s.dtype) for s in srcs]
                        + [pltpu.HBM(v.shape, v.dtype) for v in lands] + [jax.ShapeDtypeStruct((SUBLANES, LANES), F32)]),
        in_specs=[HBM] * (2 * U),
        out_specs=tuple([SEM] * (2 * G) + [HBM] * (2 * U) + [pl.BlockSpec(memory_space=pltpu.VMEM)]),
        input_output_aliases={i: 2 * G + i for i in range(2 * U)},
        compiler_params=pltpu.CompilerParams(has_side_effects=DATAFLOW),
    )(*[pltpu.with_memory_space_constraint(a, pltpu.HBM) for a in list(srcs) + lands])
    return outs[:G], outs[G:2 * G], outs[2 * G:2 * G + U], outs[2 * G + U:2 * G + 2 * U]


def _gather_wait(srcs, lands, by_cols, send_sems, recv_sems, after, name):
    n = len(srcs)

    def body(*refs):
        src = refs[:n]
        land = refs[n:2 * n]
        send_ref, recv_ref = refs[2 * n:2 * n + 2]
        _, sibling, _, _ = _other_chips()
        for pos in range(n):
            for j in range(N_GATHER_COPIES):
                cp = _gather_copy(src[pos], land[pos], by_cols[pos], send_ref, recv_ref, pos, j, 0, sibling)
                cp.wait_send()
                cp.wait_recv()

    outs = pl.pallas_call(
        body, name=name,
        out_shape=tuple([pltpu.HBM(s.shape, s.dtype) for s in srcs] + [pltpu.HBM(v.shape, v.dtype) for v in lands]),
        in_specs=[HBM] * (2 * n) + [SEM, SEM, pl.BlockSpec(memory_space=pl.ANY)],
        out_specs=tuple([HBM] * (2 * n)),
        input_output_aliases={i: i for i in range(2 * n)},
        compiler_params=pltpu.CompilerParams(has_side_effects=DATAFLOW),
    )(*srcs, *lands, send_sems, recv_sems, after)
    return outs[n:]


def _scatter_grads(csums, by_cols, name):
    n = len(csums)
    shard = [(s.shape[0], s.shape[1] // N_CHIPS) if bc else s.shape[1:] for s, bc in zip(csums, by_cols)]

    def body(*refs):
        src = refs[:n]
        rem = refs[n:2 * n]
        sib = refs[2 * n:3 * n]
        send_sems, recv_sems = refs[3 * n:]
        c = lax.axis_index("c")
        me, sibling, chips, chip_ids = _other_chips()

        def remote(i, k, src_ref, dst_ref, to):
            return pltpu.make_async_remote_copy(src_ref=src_ref, dst_ref=dst_ref, send_sem=send_sems.at[i, k],
                                                recv_sem=recv_sems.at[i, k], device_id=to, device_id_type=MESH)

        def part(i, k):
            return _chip_slab(src[i], k, shard[i][-1], by_cols[i])

        started = []
        for i in range(n):
            for j in range(3):
                started.append(remote(i, j, part(i, chip_ids[j]), rem[i].at[j, c], (*chips[j], c)))
            started.append(remote(i, 6, part(i, me), sib[i], sibling))
        for cp in started:
            cp.start()
        for i in range(n):
            for j in range(3):
                slot = rem[i].at[j, c]
                remote(i, j, slot, slot, sibling).wait_recv()
                fwd = remote(i, 3 + j, slot, slot, sibling)
                fwd.start()
                started.append(fwd)
        for i in range(n):
            for j in range(3):
                slot = rem[i].at[j, 1 - c]
                remote(i, 3 + j, slot, slot, sibling).wait_recv()
            remote(i, 6, sib[i], sib[i], sibling).wait_recv()
        for cp in started:
            cp.wait_send()

    out_shape = ([jax.ShapeDtypeStruct((N_CHIPS - 1, 2) + tuple(sh), s.dtype) for s, sh in zip(csums, shard)]
                 + [jax.ShapeDtypeStruct(tuple(sh), s.dtype) for s, sh in zip(csums, shard)])
    outs = _streamed_call(
        body, name=name,
        in_specs=[ANY] * n, out_specs=[ANY] * (2 * n), out_shape=out_shape,
        scratch_shapes=[pltpu.SemaphoreType.DMA((n, 7)), pltpu.SemaphoreType.DMA((n, 7))],
    )(*csums)
    return outs[:n], outs[n:]


def _sibling_exchange(g0, g1, name):
    n = len(g0)

    def body(*refs):
        layers = (refs[:n], refs[n:2 * n])
        theirs = refs[2 * n:3 * n]
        send_sems, recv_sems = refs[3 * n:]
        x, y, c = _mesh_position()

        def remote(i, src_ref):
            return pltpu.make_async_remote_copy(src_ref=src_ref, dst_ref=theirs[i], send_sem=send_sems.at[i],
                                                recv_sem=recv_sems.at[i], device_id=(x, y, 1 - c), device_id_type=MESH)

        for keep in range(2):
            @pl.when(c == keep)
            def _():
                for i in range(n):
                    remote(i, layers[1 - keep][i]).start()

        for i in range(n):
            remote(i, layers[0][i]).wait()

    return _streamed_call(
        body, name=name,
        in_specs=[ANY] * (2 * n), out_specs=[ANY] * n, out_shape=[jax.ShapeDtypeStruct(g.shape, g.dtype) for g in g0],
        scratch_shapes=[pltpu.SemaphoreType.DMA((n,)), pltpu.SemaphoreType.DMA((n,))],
    )(*g0, *g1)


def _all_reduce_small(v, name):
    _, rows, _ = v.shape

    def body(v_ref, o_ref, recv_ref, send_sems, recv_sems):
        x, y, c = _mesh_position()
        me = 4 * x + 2 * y + c
        peers = []
        for d in range(1, N_DEV):
            px, py, pc = x ^ ((d >> 2) & 1), y ^ ((d >> 1) & 1), c ^ (d & 1)
            peers.append(((px, py, pc), 4 * px + 2 * py + pc))

        def remote(k, src_ref, dst_ref, to):
            return pltpu.make_async_remote_copy(src_ref=src_ref, dst_ref=dst_ref, send_sem=send_sems.at[k],
                                                recv_sem=recv_sems.at[k], device_id=to, device_id_type=MESH)

        scatter = [remote(d, v_ref.at[pid], recv_ref.at[me], to) for d, (to, pid) in enumerate(peers)]
        for cp in scatter:
            cp.start()
        recv_ref[pl.ds(me, 1)] = v_ref[pl.ds(me, 1)]
        for d, (to, pid) in enumerate(peers):
            remote(d, v_ref.at[pid], recv_ref.at[pid], to).wait_recv()
        total = recv_ref[0]
        for s in range(1, N_DEV):
            total = total + recv_ref[s]
        o_ref[pl.ds(me, 1)] = total[None]
        gather = [remote(N_DEV - 1 + d, o_ref.at[me], o_ref.at[me], to) for d, (to, pid) in enumerate(peers)]
        for cp in gather:
            cp.start()
        for d, (to, pid) in enumerate(peers):
            remote(N_DEV - 1 + d, o_ref.at[pid], o_ref.at[pid], to).wait_recv()
        for cp in scatter + gather:
            cp.wait_send()

    vm = pl.BlockSpec(memory_space=pltpu.VMEM)
    return pl.pallas_call(
        body, name=name,
        in_specs=[vm], out_specs=vm, out_shape=jax.ShapeDtypeStruct(v.shape, F32),
        scratch_shapes=[pltpu.VMEM(v.shape, F32), pltpu.SemaphoreType.DMA((2 * (N_DEV - 1),)),
                        pltpu.SemaphoreType.DMA((2 * (N_DEV - 1),))],
        compiler_params=_cparams(4 * _nbytes(v.shape, F32)),
    )(v)


SHARDED_MATS = ("ffn1_w_gu", "ffn1_w_down", "mix_w_in", "conv_w_proj", "rnn_w_proj", "mix_w_out", "ffn2_w_gu", "ffn2_w_down")
COL_SHARDED = ("ffn1_w_gu", "ffn2_w_gu", "conv_dw_w")
SHARDED_VECS = ("conv_dw_w", "rnn_conv_w")
WEIGHT_NAMES = ("ffn1_w_gu", "ffn1_w_down", "ln1_g", "ln1_b", "mix_w_in", "mix_b_in", "conv_dw_w", "conv_dw_b", "conv_gn_g",
                "conv_gn_b", "conv_w_proj", "rnn_conv_w", "rnn_conv_b", "rnn_w_a", "rnn_b_a", "rnn_w_x", "rnn_b_x",
                "rnn_lambda", "rnn_w_proj", "mix_w_out", "ln2_g", "ln2_b", "ffn2_w_gu", "ffn2_w_down", "ln3_g", "ln3_b")
SMALL_NAMES = tuple(n for n in WEIGHT_NAMES if n not in SHARDED_MATS)
SECTION_NAMES = ("cv", "cg", "rx", "rg", "gc", "gr")


def _unshard_cols(gathered):
    k4, K, n = gathered.shape
    return jnp.transpose(gathered, (1, 0, 2)).reshape(K, k4 * n)


def _row(v):
    return v.reshape(1, -1)


def _layer_forward(x0, p, alpha, l, hooks):
    t = f"l{l}_"
    sv = {"x0": x0}
    x1, sv["z1"], sv["hg1"], sv["hu1"] = _ffn_fwd(x0, p["wgu1"], p["wd1"], p["ln1_g"], p["ln1_b"], alpha, t + "ffn1_fwd")
    sv["x1"] = x1
    hooks.get("after_ffn1", lambda v: None)(x1)
    sec = dict(zip(SECTION_NAMES, _mix_in(x1, p["win"], p["bin"], p["sections"], t + "mix_in")))
    sv.update(sec)
    sv["c2"], c4 = _conv_branch_fwd(sec["cv"], sec["cg"], p["conv_dw_w"], p["conv_dw_b"], p["conv_gn_g"], p["conv_gn_b"], t + "conv_fwd")
    sv["c4"] = c4
    r1 = _short_conv_fwd(sec["rx"], p["rnn_conv_w"], p["rnn_conv_b"], t + "rconv_fwd")
    sv["r1"] = r1
    sv["ra"], sv["ri"], a, uu = _gates_fwd(r1, p["wa"], p["wx"], p["rnn_b_a"], p["rnn_b_x"], p["rnn_lambda"], t + "gates_fwd")
    sv["a"] = a
    h = _scan_fwd(a, uu, t + "scan_fwd")
    sv["h"] = h
    hooks.get("after_scan", lambda v: None)(h)
    sv["yc"], sv["yr"], sv["z2"], x2 = _mixer_out_fwd(c4, h, sec["rg"], sec["gc"], sec["gr"], x1, p["wc"], p["wr"], p["wo"],
                                                      p["ln2_g"], p["ln2_b"], alpha, t + "mixout_fwd")
    sv["x2"] = x2
    hooks.get("after_mixer", lambda v: None)(x2)
    x3, sv["z3"], sv["hg2"], sv["hu2"] = _ffn_fwd(x2, p["wgu2"], p["wd2"], p["ln3_g"], p["ln3_b"], alpha, t + "ffn2_fwd")
    hooks.get("after_layer", lambda v: None)(x3)
    return x3, sv


def _layer_backward(dy, p, sv, alpha, l):
    t = f"l{l}_"
    g, gb = {}, {}
    dx2, df, a_act, dhg, dhu, g["ln3_g"], g["ln3_b"] = _ffn_bwd(dy, sv["z3"], sv["hg2"], sv["hu2"], p["wgu2"], p["wd2"],
                                                                 p["ln3_g"], alpha, t + "ffn2_bwd")
    g["ffn2_w_down"], gb["ffn2_w_down"] = _mm_tn(a_act, df, t + "dwd2")
    g["ffn2_w_gu"], gb["ffn2_w_gu"] = _mm_tn_pair(sv["x2"], dhg, dhu, t + "dwgu2")
    (dz2, dz2b, m_b, dyc, dyr, dgc, dgr, s_gc, s_gr, g["ln2_g"], g["ln2_b"]) = _mixer_out_bwd(
        dx2, sv["z2"], p["ln2_g"], p["wo"], sv["yc"], sv["yr"], sv["gc"], sv["gr"], t + "mixout_bwd")
    g["mix_w_out"], gb["mix_w_out"] = _mm_tn(m_b, dz2b, t + "dwo")
    dc4, dh, drg, q_b, s_rg = _branch_bwd(dyc, dyr, p["wc"], p["wr"], sv["h"], sv["rg"], t + "branch_bwd")
    g["conv_w_proj"], gb["conv_w_proj"] = _mm_tn(sv["c4"], dyc, t + "dwc")
    g["rnn_w_proj"], gb["rnn_w_proj"] = _mm_tn(q_b, dyr, t + "dwr")
    (dcv, dcg, g["conv_dw_w"], g["conv_dw_b"], g["conv_gn_g"], g["conv_gn_b"], s_cv, s_cg) = _conv_branch_bwd(
        dc4, sv["c2"], sv["cv"], sv["cg"], p["conv_dw_w"], p["conv_gn_g"], p["conv_gn_b"], t + "conv_bwd")
    guu, da = _scan_bwd(sv["a"], dh, sv["h"], t + "scan_bwd")
    dr1, dpa, dpx, g["rnn_b_a"], g["rnn_b_x"], g["rnn_lambda"] = _gates_bwd(
        guu, da, sv["ra"], sv["ri"], sv["r1"], p["wa"], p["wx"], p["rnn_lambda"], t + "gates_bwd")
    g["rnn_w_a"], g["rnn_w_x"] = _block_grads(sv["r1"], dpa, dpx, t + "dwax")
    drx, g["rnn_conv_w"], g["rnn_conv_b"], s_rx = _short_conv_bwd(dr1, sv["rx"], p["rnn_conv_w"], t + "rconv_bwd")
    du = {"cv": dcv, "cg": dcg, "rx": drx, "rg": drg, "gc": dgc, "gr": dgr}
    order = ("cv", "cg", "rx", "rg", "gc", "gr")
    pieces = [_mm_tn(du[s], sv["x1"], t + "dwin_" + s) for s in order]
    g["mix_w_in"] = jnp.concatenate([f for f, _ in pieces], axis=0)
    gb["mix_w_in"] = jnp.concatenate([h for _, h in pieces], axis=0)
    g["mix_b_in"] = jnp.concatenate([s_cv, s_cg, s_rx, s_rg, s_gc, s_gr], axis=1)
    dx1 = _mix_dx(dz2, [du[s] for s in order], p["win"], p["sections"], alpha, t + "mix_dx")
    dx0, df, a_act, dhg, dhu, g["ln1_g"], g["ln1_b"] = _ffn_bwd(dx1, sv["z1"], sv["hg1"], sv["hu1"], p["wgu1"], p["wd1"],
                                                                 p["ln1_g"], alpha, t + "ffn1_bwd")
    g["ffn1_w_down"], gb["ffn1_w_down"] = _mm_tn(a_act, df, t + "dwd1")
    g["ffn1_w_gu"], gb["ffn1_w_gu"] = _mm_tn_pair(sv["x0"], dhg, dhu, t + "dwgu1")
    return dx0, g, gb


def _pack_small(arrays, piece_rows):
    flat = jnp.concatenate([a.reshape(-1) for a in arrays])
    total = N_DEV * piece_rows * LANES
    return jnp.pad(flat, (0, total - flat.shape[0])).reshape(N_DEV, piece_rows, LANES)


def _unpack_small(packed, shapes):
    flat = packed.reshape(-1)
    out, off = [], 0
    for shp in shapes:
        n = 1
        for s in shp:
            n *= s
        out.append(flat[off:off + n].reshape(shp))
        off += n
    return out


def kernel(x, ffn1_w_gu, ffn1_w_down, ln1_g, ln1_b, mix_w_in, mix_b_in, conv_dw_w, conv_dw_b, conv_gn_g, conv_gn_b, conv_w_proj, rnn_conv_w, rnn_conv_b, rnn_w_a, rnn_b_a, rnn_w_x, rnn_b_x, rnn_lambda, rnn_w_proj, mix_w_out, ln2_g, ln2_b, ffn2_w_gu, ffn2_w_down, ln3_g, ln3_b, loss_target, m_ffn1_w_gu, m_ffn1_w_down, m_ln1_g, m_ln1_b, m_mix_w_in, m_mix_b_in, m_conv_dw_w, m_conv_dw_b, m_conv_gn_g, m_conv_gn_b, m_conv_w_proj, m_rnn_conv_w, m_rnn_conv_b, m_rnn_w_a, m_rnn_b_a, m_rnn_w_x, m_rnn_b_x, m_rnn_lambda, m_rnn_w_proj, m_mix_w_out, m_ln2_g, m_ln2_b, m_ffn2_w_gu, m_ffn2_w_down, m_ln3_g, m_ln3_b, v_ffn1_w_gu, v_ffn1_w_down, v_ln1_g, v_ln1_b, v_mix_w_in, v_mix_b_in, v_conv_dw_w, v_conv_dw_b, v_conv_gn_g, v_conv_gn_b, v_conv_w_proj, v_rnn_conv_w, v_rnn_conv_b, v_rnn_w_a, v_rnn_b_a, v_rnn_w_x, v_rnn_b_x, v_rnn_lambda, v_rnn_w_proj, v_mix_w_out, v_ln2_g, v_ln2_b, v_ffn2_w_gu, v_ffn2_w_down, v_ln3_g, v_ln3_b):
    args = locals()
    W = {n: args[n] for n in WEIGHT_NAMES}
    M = {n: args["m_" + n] for n in WEIGHT_NAMES}
    V = {n: args["v_" + n] for n in WEIGHT_NAMES}
    depth = ln1_g.shape[0]
    assert depth == 2, "each core of a chip moves one layer's weights and gradients"
    alpha = float((2 * depth) ** 0.25)
    S, D = x.shape[1], x.shape[2]
    F = ffn1_w_down.shape[1] * N_CHIPS
    R = rnn_w_proj.shape[1] * N_CHIPS
    chip = 2 * lax.axis_index("x") + lax.axis_index("y")

    for d in (W, M, V):
        d["mix_w_in"] = jnp.transpose(d["mix_w_in"], (0, 2, 1))

    names = SHARDED_MATS + SHARDED_VECS
    unit_groups = [[(0, "ffn1_w_gu"), (0, "ffn1_w_down")], [(0, "mix_w_in"), (0, "conv_dw_w"), (0, "rnn_conv_w")],
                   [(0, "conv_w_proj"), (0, "rnn_w_proj"), (0, "mix_w_out")], [(0, "ffn2_w_gu"), (0, "ffn2_w_down")],
                   [(1, n) for n in names]]
    order = [u for g in unit_groups for u in g]
    index = {u: i for i, u in enumerate(order)}
    groups = [[index[u] for u in g] for g in unit_groups]
    srcs = [W[n][l].astype(BF16) if n in SHARDED_MATS else W[n][l] for l, n in order]
    by_cols = [n in COL_SHARDED for _, n in order]
    send_sems, recv_sems, src_thru, land_thru = _gather_start(srcs, by_cols, groups, "gather_start")

    sections = ((0, D, F32), (D, D, F32), (2 * D, R, F32), (2 * D + R, R, BF16), (2 * D + 2 * R, D, BF16),
                (3 * D + 2 * R, D, BF16))
    keys = {"ffn1_w_gu": "wgu1", "ffn1_w_down": "wd1", "ffn2_w_gu": "wgu2", "ffn2_w_down": "wd2", "conv_w_proj": "wc",
            "rnn_w_proj": "wr", "mix_w_out": "wo"}
    params = []
    for l in range(depth):
        p = {"wa": _embed_blocks(rnn_w_a[l], f"l{l}_embed_wa"), "wx": _embed_blocks(rnn_w_x[l], f"l{l}_embed_wx")}
        for n in ("ln1_g", "ln1_b", "ln2_g", "ln2_b", "ln3_g", "ln3_b", "conv_dw_b", "conv_gn_g", "conv_gn_b", "rnn_conv_b",
                  "rnn_b_a", "rnn_b_x", "rnn_lambda"):
            p[n] = _row(W[n][l])
        p["bin"] = _row(mix_b_in[l])
        p["sections"] = sections
        params.append(p)

    def wait_group(g, after):
        ids = groups[g]
        landed = _gather_wait([src_thru[i] for i in ids], [land_thru[i] for i in ids], [by_cols[i] for i in ids],
                              send_sems[g], recv_sems[g], after, f"gather_wait{g}")
        for i, full in zip(ids, landed):
            l, n = order[i]
            p = params[l]
            if n not in COL_SHARDED:
                full = full.reshape((N_CHIPS * full.shape[1],) + full.shape[2:])
            if n == "mix_w_in":
                p["win"] = full
            elif n == "rnn_conv_w":
                p[n] = _unshard_cols(landed[ids.index(i)])
            elif n == "conv_dw_w":
                p[n] = full
            else:
                p[keys[n]] = (full[None], 0)

    h = x[0]
    wait_group(0, h)
    saved = []
    hooks = [{"after_ffn1": lambda v: wait_group(1, v), "after_scan": lambda v: wait_group(2, v),
              "after_mixer": lambda v: wait_group(3, v), "after_layer": lambda v: wait_group(4, v)}, {}]
    for l in range(depth):
        h, sv = _layer_forward(h, params[l], alpha, l, hooks[l])
        saved.append(sv)
    loss_part, dy = _loss_head(h, loss_target[0], "loss_head")
    loss = lax.psum(loss_part[0, 0], ("x", "y", "c"))
    grads, grads_bf16 = [None] * depth, [None] * depth
    for l in reversed(range(depth)):
        dy, grads[l], grads_bf16[l] = _layer_backward(dy, params[l], saved[l], alpha, l)
    grad_x = dy[None]

    g0 = [grads[0][n] for n in SHARDED_MATS]
    g1 = [grads[1][n] for n in SHARDED_MATS]
    theirs = _sibling_exchange([grads_bf16[0][n] for n in SHARDED_MATS], [grads_bf16[1][n] for n in SHARDED_MATS],
                               "pair_exchange")
    mat_cols = [n in COL_SHARDED for n in SHARDED_MATS]
    chip_sums = []
    for n, bc, a0, a1, b in zip(SHARDED_MATS, mat_cols, g0, g1, theirs):
        cs = _pair_sum_bf16(a0, a1, b, "pair_sum_" + n)
        chip_sums.append(cs if bc else cs.reshape((N_CHIPS, cs.shape[0] // N_CHIPS, cs.shape[1])))
    rem, sib = _scatter_grads(chip_sums, mat_cols, "scatter_grads")
    out_g, out_d, out_m, out_v = {}, {}, {}, {}
    for n, bc, cs, rm, sb in zip(SHARDED_MATS, mat_cols, chip_sums, rem, sib):
        if bc:
            width = cs.shape[1] // N_CHIPS
            own = lax.dynamic_slice_in_dim(cs, chip * width, width, axis=1)
        else:
            own = lax.dynamic_index_in_dim(cs, chip, axis=0, keepdims=False)
        out_g[n], out_d[n], out_m[n], out_v[n] = _adamw_sharded(W[n], M[n], V[n], own, sb, rm, "adamw_" + n)
    for d in (out_g, out_d, out_m, out_v):
        d["mix_w_in"] = jnp.transpose(d["mix_w_in"], (0, 2, 1))

    small_grads = [jnp.stack([grads[l][n].reshape(W[n].shape[1:] if n not in SHARDED_VECS else
                                                   (W[n].shape[1], W[n].shape[2] * N_CHIPS)) for l in range(depth)])
                   for n in SMALL_NAMES]
    n_small = sum(int(a.size) for a in small_grads)
    piece_rows = -(-n_small // (N_DEV * LANES * SUBLANES)) * SUBLANES
    reduced = _unpack_small(_all_reduce_small(_pack_small(small_grads, piece_rows), "all_reduce_small"),
                            [a.shape for a in small_grads])
    local_g = []
    for n, gr in zip(SMALL_NAMES, reduced):
        if n in SHARDED_VECS:
            width = W[n].shape[2]
            gr = lax.dynamic_slice_in_dim(gr, chip * width, width, axis=2)
        local_g.append(gr)
    n_local = sum(int(a.size) for a in local_g)
    flat_rows = -(-n_local // (N_DEV * LANES * SUBLANES)) * SUBLANES * N_DEV
    pack = lambda arrs: _pack_small(arrs, flat_rows // N_DEV).reshape(flat_rows, LANES)
    shapes = [a.shape for a in local_g]
    deltas, new_m, new_v = _adamw_flat(pack([W[n] for n in SMALL_NAMES]), pack([M[n] for n in SMALL_NAMES]),
                                       pack([V[n] for n in SMALL_NAMES]), pack(local_g), "adamw_small")
    for n, gr, d_, m_, v_ in zip(SMALL_NAMES, local_g, _unpack_small(deltas, shapes), _unpack_small(new_m, shapes),
                                 _unpack_small(new_v, shapes)):
        out_g[n], out_d[n], out_m[n], out_v[n] = gr, d_, m_, v_

    return (loss, grad_x, *[out_g[n] for n in WEIGHT_NAMES], *[out_d[n] for n in WEIGHT_NAMES],
            *[out_m[n] for n in WEIGHT_NAMES], *[out_v[n] for n in WEIGHT_NAMES])
```

```python
import functools

import jax
import jax.numpy as jnp
from jax import lax
from jax.experimental import pallas as pl
from jax.experimental.pallas import tpu as pltpu

F32 = jnp.float32
BF16 = jnp.bfloat16
MESH = pl.DeviceIdType.MESH

LN_EPS = 1e-5
CONV_GROUPS = 8
RNN_BLOCKS = 16
RG_LRU_C = 8.0
ADAM_LR = 0.001
ADAM_B1 = 0.9
ADAM_B2 = 0.999
ADAM_EPS = 1e-08
ADAM_WD = 0.01
ADAM_STEP = 10

LANES = 128
SUBLANES = 8
V7X_VMEM_BYTES = 64 << 20
VMEM_LIMIT_CAP = V7X_VMEM_BYTES - (6 << 20)
N_CHIPS = 4
N_DEV = 8
CONV_ROWS = 64
EW_ROWS = 1024
SCAN_SEGMENTS = 32
SCAN_UNROLL = 4


def _cparams(block_bytes):
    limit = min(VMEM_LIMIT_CAP, max(int(block_bytes) + (8 << 20), 24 << 20))
    return pltpu.CompilerParams(vmem_limit_bytes=limit)


def _nbytes(shape, dtype):
    n = 1
    for s in shape:
        n *= s
    return n * jnp.dtype(dtype).itemsize


def _divisor_tile(n, limit, quantum):
    if n <= limit:
        return n
    best = None
    for t in range(quantum, limit + 1, quantum):
        if n % t == 0:
            best = t
    assert best is not None, (n, limit, quantum)
    return best


def _bs(shape, imap, **kw):
    return pl.BlockSpec(shape, imap, **kw)


def _resident(shape):
    nd = len(shape)
    return pl.BlockSpec(shape, lambda *_: (0,) * nd, pipeline_mode=pl.Buffered(1))


def _streamed_call(body, **kw):
    call = pl.pallas_call(body, **kw)
    return lambda *operands: call(*[pltpu.with_memory_space_constraint(o, pltpu.HBM) for o in operands])


def _layer_block(w, block, imap, **kw):
    arr, layer = w
    return arr, pl.BlockSpec((None,) + block, lambda *ids: (layer,) + imap(*ids), **kw)


def _layer_resident(w):
    arr, _ = w
    return _layer_block(w, arr.shape[1:], lambda *_: (0, 0), pipeline_mode=pl.Buffered(1))


def _sigmoid(x):
    return jax.nn.sigmoid(x)


def _dot(a, b):
    return jnp.dot(a, b, preferred_element_type=F32)


def _dot_nt(a, b):
    return lax.dot_general(a, b, (((1,), (1,)), ((), ())), preferred_element_type=F32)


def _dot_tn(a, b):
    return lax.dot_general(a, b, (((0,), (0,)), ((), ())), preferred_element_type=F32)


def _row_mean(z):
    return jnp.mean(z, axis=-1, keepdims=True)


def _lane_mean(z):
    hi = z.astype(BF16)
    lo = (z - hi.astype(F32)).astype(BF16)
    ones = jnp.full((2 * LANES, LANES), 1.0 / LANES, BF16)
    return jnp.dot(jnp.concatenate([hi, lo], axis=-1), ones, preferred_element_type=F32)


def _norm_fwd(z, g, b, mean=_row_mean):
    mu = mean(z)
    xc = z - mu
    var = mean(xc * xc)
    return xc * lax.rsqrt(var + LN_EPS) * g + b


def _norm_bwd(z, g, dy, mean=_row_mean):
    mu = mean(z)
    xc = z - mu
    var = mean(xc * xc)
    rstd = lax.rsqrt(var + LN_EPS)
    xhat = xc * rstd
    dxh = dy * g
    m1 = mean(dxh)
    m2 = mean(dxh * xhat)
    return rstd * (dxh - m1 - xhat * m2), xhat


GELU_K = 0.7978845608028654
GELU_C = 0.044715


def _gelu(x):
    return 0.5 * x * (1.0 + jnp.tanh(GELU_K * (x + GELU_C * x * x * x)))


def _gelu_grad(x):
    t = jnp.tanh(GELU_K * (x + GELU_C * x * x * x))
    return 0.5 * (1.0 + t) + 0.5 * x * (1.0 - t * t) * GELU_K * (1.0 + 3.0 * GELU_C * x * x)


def _softplus(y):
    return jnp.maximum(y, 0.0) + jnp.log1p(jnp.exp(-jnp.abs(y)))


def _neg_expm1(y):
    series = -y * (1.0 + y * (0.5 + y * (1.0 / 6.0 + y * (1.0 / 24.0 + y * (1.0 / 120.0 + y * (1.0 / 720.0))))))
    return jnp.where(y > -0.25, series, 1.0 - jnp.exp(y))


def _colsum(x):
    return jnp.sum(x, axis=0, keepdims=True)


def _shifted_taps(src_ref, base, rows, taps):
    acc = None
    for o, coef in taps:
        term = coef() * src_ref[pl.ds(base + o, rows), :]
        acc = term if acc is None else acc + term
    return acc


def _shifted_corr(src_ref, base, rows, d, acc_ref, offs):
    for k, o in enumerate(offs):
        prod = d * src_ref[pl.ds(base + o, rows), :]
        part = jnp.sum(prod.reshape(rows // SUBLANES, SUBLANES, prod.shape[-1]), axis=0)
        acc_ref[SUBLANES * k:SUBLANES * (k + 1), :] += part


def _front_pad(ktaps):
    return SUBLANES * ((ktaps - 1 + SUBLANES - 1) // SUBLANES)


def _pad_rows(ktaps):
    return _front_pad(ktaps) + SUBLANES


def _ffn_tiles(S, F):
    tm = _divisor_tile(S, 1024, 16)
    tf = _divisor_tile(F, 256, LANES)
    return tm, tf


def _ffn_fwd(x, wgu, wd, g, b, alpha, name):
    S, D = x.shape
    F = wd[0].shape[1]
    tm, tf = _ffn_tiles(S, F)
    nf = F // tf
    wg_arr, wg_spec = _layer_block(wgu, (D, tf), lambda i, j: (0, j))
    wu_arr, wu_spec = _layer_block(wgu, (D, tf), lambda i, j: (0, nf + j))
    wd_arr, wd_spec = _layer_block(wd, (tf, D), lambda i, j: (j, 0))

    def body(x_ref, wg_ref, wu_ref, wd_ref, g_ref, b_ref, y_ref, z_ref, hg_ref, hu_ref, acc_ref, xb_ref):
        j = pl.program_id(1)

        @pl.when(j == 0)
        def _():
            xb_ref[...] = x_ref[...].astype(BF16)
            acc_ref[...] = jnp.zeros_like(acc_ref)

        xb = xb_ref[...]
        hg = _dot(xb, wg_ref[...])
        hu = _dot(xb, wu_ref[...])
        hg_ref[...] = hg
        hu_ref[...] = hu
        a = (hg * _sigmoid(hg) * hu).astype(BF16)
        acc_ref[...] += _dot(a, wd_ref[...])

        @pl.when(j == nf - 1)
        def _():
            z = alpha * x_ref[...] + 0.5 * acc_ref[...]
            z_ref[...] = z
            y_ref[...] = _norm_fwd(z, g_ref[...], b_ref[...])

    blk = 2 * (3 * tm * D * 4 + 2 * tm * tf * 4 + 3 * D * tf * 2) + tm * D * 6 + 3 * tm * tf * 4
    return _streamed_call(
        body, name=name, grid=(S // tm, nf),
        in_specs=[_bs((tm, D), lambda i, j: (i, 0)), wg_spec, wu_spec, wd_spec,
                  _bs((1, D), lambda i, j: (0, 0)), _bs((1, D), lambda i, j: (0, 0))],
        out_specs=[_bs((tm, D), lambda i, j: (i, 0)), _bs((tm, D), lambda i, j: (i, 0)),
                   _bs((tm, tf), lambda i, j: (i, j)), _bs((tm, tf), lambda i, j: (i, j))],
        out_shape=[jax.ShapeDtypeStruct((S, D), F32), jax.ShapeDtypeStruct((S, D), F32),
                   jax.ShapeDtypeStruct((S, F), F32), jax.ShapeDtypeStruct((S, F), F32)],
        scratch_shapes=[pltpu.VMEM((tm, D), F32), pltpu.VMEM((tm, D), BF16)],
        compiler_params=_cparams(blk),
    )(x, wg_arr, wu_arr, wd_arr, g, b)


def _ffn_bwd(dy, z, hg, hu, wgu, wd, g, alpha, name):
    S, D = dy.shape
    F = wd[0].shape[1]
    tm, tf = _ffn_tiles(S, F)
    nf = F // tf
    wg_arr, wg_spec = _layer_block(wgu, (D, tf), lambda i, j: (0, j))
    wu_arr, wu_spec = _layer_block(wgu, (D, tf), lambda i, j: (0, nf + j))
    wd_arr, wd_spec = _layer_block(wd, (tf, D), lambda i, j: (j, 0))

    def body(dy_ref, z_ref, hg_ref, hu_ref, wg_ref, wu_ref, wd_ref, g_ref,
             dx_ref, df_ref, a_ref, dhg_ref, dhu_ref, dg_ref, db_ref, acc_ref):
        i = pl.program_id(0)
        j = pl.program_id(1)

        @pl.when((i == 0) & (j == 0))
        def _():
            dg_ref[...] = jnp.zeros_like(dg_ref)
            db_ref[...] = jnp.zeros_like(db_ref)

        @pl.when(j == 0)
        def _():
            dy_ = dy_ref[...]
            dz, xhat = _norm_bwd(z_ref[...], g_ref[...], dy_)
            dg_ref[...] += _colsum(dy_ * xhat)
            db_ref[...] += _colsum(dy_)
            acc_ref[...] = alpha * dz
            df_ref[...] = (0.5 * dz).astype(BF16)

        da = _dot_nt(df_ref[...], wd_ref[...])
        hg_ = hg_ref[...]
        hu_ = hu_ref[...]
        s = _sigmoid(hg_)
        sl = hg_ * s
        dgate = (da * hu_ * (s * (1.0 + hg_ * (1.0 - s)))).astype(BF16)
        dup = (da * sl).astype(BF16)
        a_ref[...] = (sl * hu_).astype(BF16)
        dhg_ref[...] = dgate
        dhu_ref[...] = dup
        acc_ref[...] += _dot_nt(dgate, wg_ref[...]) + _dot_nt(dup, wu_ref[...])

        @pl.when(j == nf - 1)
        def _():
            dx_ref[...] = acc_ref[...]

    blk = 2 * (2 * tm * D * 4 + tm * D * 2 + 2 * tm * tf * 4 + 3 * tm * tf * 2 + 3 * D * tf * 2) + 3 * tm * D * 4 + 8 * tm * tf * 4
    once = dict(pipeline_mode=pl.Buffered(1))
    return _streamed_call(
        body, name=name, grid=(S // tm, nf),
        in_specs=[_bs((tm, D), lambda i, j: (i, 0), **once), _bs((tm, D), lambda i, j: (i, 0), **once),
                  _bs((tm, tf), lambda i, j: (i, j)), _bs((tm, tf), lambda i, j: (i, j)),
                  wg_spec, wu_spec, wd_spec, _bs((1, D), lambda i, j: (0, 0))],
        out_specs=[_bs((tm, D), lambda i, j: (i, 0)), _bs((tm, D), lambda i, j: (i, 0)),
                   _bs((tm, tf), lambda i, j: (i, j)), _bs((tm, tf), lambda i, j: (i, j)), _bs((tm, tf), lambda i, j: (i, j)),
                   _bs((1, D), lambda i, j: (0, 0)), _bs((1, D), lambda i, j: (0, 0))],
        out_shape=[jax.ShapeDtypeStruct((S, D), F32), jax.ShapeDtypeStruct((S, D), BF16),
                   jax.ShapeDtypeStruct((S, F), BF16), jax.ShapeDtypeStruct((S, F), BF16), jax.ShapeDtypeStruct((S, F), BF16),
                   jax.ShapeDtypeStruct((1, D), F32), jax.ShapeDtypeStruct((1, D), F32)],
        scratch_shapes=[pltpu.VMEM((tm, D), F32)],
        compiler_params=_cparams(blk),
    )(dy, z, hg, hu, wg_arr, wu_arr, wd_arr, g)


def _mm_tn(a, b, name):
    S, M = a.shape
    N = b.shape[1]
    bm = _divisor_tile(M, 1408, LANES)
    bn = _divisor_tile(N, 1408, LANES)
    tk = _divisor_tile(S, 512, 16)
    nk = S // tk

    def body(a_ref, b_ref, o_ref, ob_ref):
        k = pl.program_id(2)

        @pl.when(k == 0)
        def _():
            o_ref[...] = jnp.zeros_like(o_ref)

        o_ref[...] += _dot_tn(a_ref[...].astype(BF16), b_ref[...].astype(BF16))

        @pl.when(k == nk - 1)
        def _():
            ob_ref[...] = o_ref[...].astype(BF16)

    blk = 2 * (tk * bm * a.dtype.itemsize + tk * bn * b.dtype.itemsize + bm * bn * 6) + tk * bm * 4 + bm * bn * 4
    tile = _bs((bm, bn), lambda i, j, k: (i, j))
    return _streamed_call(
        body, name=name, grid=(M // bm, N // bn, nk),
        in_specs=[_bs((tk, bm), lambda i, j, k: (k, i)), _bs((tk, bn), lambda i, j, k: (k, j))],
        out_specs=[tile, tile],
        out_shape=[jax.ShapeDtypeStruct((M, N), F32), jax.ShapeDtypeStruct((M, N), BF16)],
        compiler_params=_cparams(blk),
    )(a, b)


def _mm_tn_pair(a, b0, b1, name):
    S, M = a.shape
    N = b0.shape[1]
    assert b1.shape == b0.shape
    bm = _divisor_tile(M, 1408, LANES)
    bn = _divisor_tile(N, 1408, LANES)
    tk = _divisor_tile(S, 512, 16)
    nb = N // bn
    nk = S // tk

    def body(a_ref, b0_ref, b1_ref, o_ref, ob_ref):
        j = pl.program_id(1)
        k = pl.program_id(2)

        @pl.when(k == 0)
        def _():
            o_ref[...] = jnp.zeros_like(o_ref)

        ab = a_ref[...].astype(BF16)

        @pl.when(j < nb)
        def _():
            o_ref[...] += _dot_tn(ab, b0_ref[...])

        @pl.when(j >= nb)
        def _():
            o_ref[...] += _dot_tn(ab, b1_ref[...])

        @pl.when(k == nk - 1)
        def _():
            ob_ref[...] = o_ref[...].astype(BF16)

    b0_map = lambda i, j, k: (jnp.where(j < nb, k, nk - 1), jnp.minimum(j, nb - 1))
    b1_map = lambda i, j, k: (jnp.where(j >= nb, k, 0), jnp.maximum(j - nb, 0))
    blk = 2 * (tk * bm * a.dtype.itemsize + 2 * tk * bn * 2 + bm * bn * 6) + tk * bm * 4 + bm * bn * 4
    tile = _bs((bm, bn), lambda i, j, k: (i, j))
    return _streamed_call(
        body, name=name, grid=(M // bm, 2 * nb, nk),
        in_specs=[_bs((tk, bm), lambda i, j, k: (k, i)), _bs((tk, bn), b0_map), _bs((tk, bn), b1_map)],
        out_specs=[tile, tile],
        out_shape=[jax.ShapeDtypeStruct((M, 2 * N), F32), jax.ShapeDtypeStruct((M, 2 * N), BF16)],
        compiler_params=_cparams(blk),
    )(a, b0, b1)


def _mix_in(x, wt, bias, sections, name):
    S, D = x.shape
    tm = _divisor_tile(S, 256, 16)
    n = len(sections)

    def body(x_ref, w_ref, b_ref, *o_refs):
        xb = x_ref[...].astype(BF16)
        for (off, width, dtype), o_ref in zip(sections, o_refs):
            o_ref[...] = (_dot_nt(xb, w_ref[off:off + width, :]) + b_ref[:, off:off + width]).astype(dtype)

    total = wt.shape[0]
    blk = 2 * (tm * D * 4 + sum(tm * w * jnp.dtype(dt).itemsize for _, w, dt in sections)) + total * D * 2 + 3 * tm * D * 4
    return _streamed_call(
        body, name=name, grid=(S // tm,),
        in_specs=[_bs((tm, D), lambda i: (i, 0)), _resident((total, D)), _resident((1, total))],
        out_specs=[_bs((tm, w), lambda i: (i, 0)) for _, w, _ in sections],
        out_shape=[jax.ShapeDtypeStruct((S, w), dt) for _, w, dt in sections],
        compiler_params=_cparams(blk),
    )(x, wt, bias)


def _mix_dx(dz, parts, wt, sections, alpha, name):
    S, D = dz.shape
    tm = _divisor_tile(S, 256, 16)
    n = len(parts)

    def body(*refs):
        dz_ref = refs[0]
        p_refs = refs[1:1 + n]
        w_ref = refs[1 + n]
        o_ref = refs[2 + n]
        acc = alpha * dz_ref[...]
        for p_ref, (off, width, _) in zip(p_refs, sections):
            acc = acc + _dot(p_ref[...], w_ref[off:off + width, :])
        o_ref[...] = acc

    widths = [p.shape[1] for p in parts]
    total = wt.shape[0]
    blk = 2 * (2 * tm * D * 4 + sum(tm * w * 2 for w in widths)) + total * D * 2 + 2 * tm * D * 4
    return _streamed_call(
        body, name=name, grid=(S // tm,),
        in_specs=[_bs((tm, D), lambda i: (i, 0))] + [_bs((tm, w), lambda i: (i, 0)) for w in widths]
                 + [_resident((total, D))],
        out_specs=_bs((tm, D), lambda i: (i, 0)),
        out_shape=jax.ShapeDtypeStruct((S, D), F32),
        compiler_params=_cparams(blk),
    )(dz, *parts, wt)


def _conv_branch_fwd(cv, cg, w, b, gg, gb, name):
    S, C = cv.shape
    K = w.shape[0]
    assert C // CONV_GROUPS == LANES
    padf = _front_pad(K)
    R = min(CONV_ROWS, S)
    E = min(EW_ROWS, S)

    def body(cv_ref, cg_ref, w_ref, b_ref, gg_ref, gb_ref, c2_ref, c4_ref, pad_ref):
        pad_ref[0:padf, :] = jnp.zeros((padf, LANES), F32)
        pad_ref[S + padf:S + padf + SUBLANES, :] = jnp.zeros((SUBLANES, LANES), F32)

        def fill(i, carry):
            r = pl.multiple_of(i * E, E)
            pad_ref[pl.ds(r + padf, E), :] = cv_ref[pl.ds(r, E), :] * _sigmoid(cg_ref[pl.ds(r, E), :])
            return carry

        lax.fori_loop(0, S // E, fill, 0)
        taps = [(padf - (K - 1) + k, functools.partial(lambda k: w_ref[k:k + 1, :], k)) for k in range(K)]

        def conv(i, carry):
            r = pl.multiple_of(i * R, R)
            c2_ref[pl.ds(r, R), :] = _shifted_taps(pad_ref, r, R, taps) + b_ref[...]
            return carry

        lax.fori_loop(0, S // R, conv, 0)

        def norm(i, carry):
            r = pl.multiple_of(i * E, E)
            c3 = _norm_fwd(c2_ref[pl.ds(r, E), :], gg_ref[...], gb_ref[...], _lane_mean)
            c4_ref[pl.ds(r, E), :] = (c3 * _sigmoid(c3)).astype(BF16)
            return carry

        lax.fori_loop(0, S // E, norm, 0)

    col = lambda i: (0, i)
    blk = 2 * (3 * S * LANES * 4 + S * LANES * 2) + (S + _pad_rows(K)) * LANES * 4
    return _streamed_call(
        body, name=name, grid=(C // LANES,),
        in_specs=[_bs((S, LANES), col), _bs((S, LANES), col), _bs((K, LANES), col),
                  _bs((1, LANES), col), _bs((1, LANES), col), _bs((1, LANES), col)],
        out_specs=[_bs((S, LANES), col), _bs((S, LANES), col)],
        out_shape=[jax.ShapeDtypeStruct((S, C), F32), jax.ShapeDtypeStruct((S, C), BF16)],
        scratch_shapes=[pltpu.VMEM((S + _pad_rows(K), LANES), F32)],
        compiler_params=_cparams(blk),
    )(cv, cg, w, b, gg, gb)


def _conv_branch_bwd(dc4, c2, cv, cg, w, gg, gb, name):
    S, C = cv.shape
    K = w.shape[0]
    padf = _front_pad(K)
    R = min(CONV_ROWS, S)
    E = min(EW_ROWS, S)

    def body(dc4_ref, c2_ref, cv_ref, cg_ref, w_ref, gg_ref, gb_ref,
             dcv_ref, dcg_ref, dw_ref, dwb_ref, dgg_ref, dgb_ref, scv_ref, scg_ref,
             dpad_ref, cpad_ref, dwacc_ref):
        cpad_ref[0:padf, :] = jnp.zeros((padf, LANES), F32)
        cpad_ref[S + padf:S + padf + SUBLANES, :] = jnp.zeros((SUBLANES, LANES), F32)
        dpad_ref[S:S + padf + SUBLANES, :] = jnp.zeros((padf + SUBLANES, LANES), F32)
        dwacc_ref[...] = jnp.zeros_like(dwacc_ref)
        for ref in (dwb_ref, dgg_ref, dgb_ref, scv_ref, scg_ref):
            ref[...] = jnp.zeros_like(ref)

        def norm_pass(i, carry):
            r = pl.multiple_of(i * E, E)
            g_ = gg_ref[...]
            c2 = c2_ref[pl.ds(r, E), :]
            xc = c2 - _lane_mean(c2)
            rstd = lax.rsqrt(_lane_mean(xc * xc) + LN_EPS)
            xhat = xc * rstd
            c3 = xhat * g_ + gb_ref[...]
            s = _sigmoid(c3)
            dc3 = dc4_ref[pl.ds(r, E), :].astype(F32) * (s * (1.0 + c3 * (1.0 - s)))
            dgg_ref[...] += _colsum(dc3 * xhat)
            dgb_ref[...] += _colsum(dc3)
            dxh = dc3 * g_
            dc2 = rstd * (dxh - _lane_mean(dxh) - xhat * _lane_mean(dxh * xhat))
            dpad_ref[pl.ds(r, E), :] = dc2
            dwb_ref[...] += _colsum(dc2)
            cpad_ref[pl.ds(r + padf, E), :] = cv_ref[pl.ds(r, E), :] * _sigmoid(cg_ref[pl.ds(r, E), :])
            return carry

        lax.fori_loop(0, S // E, norm_pass, 0)
        taps = [(K - 1 - k, functools.partial(lambda k: w_ref[k:k + 1, :], k)) for k in range(K)]
        offs = [padf - (K - 1) + k for k in range(K)]

        def conv_pass(i, carry):
            r = pl.multiple_of(i * R, R)
            dc1 = _shifted_taps(dpad_ref, r, R, taps)
            sg = _sigmoid(cg_ref[pl.ds(r, R), :])
            cv_ = cv_ref[pl.ds(r, R), :]
            dcv = dc1 * sg
            dcg = dc1 * cv_ * sg * (1.0 - sg)
            dcv_ref[pl.ds(r, R), :] = dcv.astype(BF16)
            dcg_ref[pl.ds(r, R), :] = dcg.astype(BF16)
            scv_ref[...] += _colsum(dcv)
            scg_ref[...] += _colsum(dcg)
            _shifted_corr(cpad_ref, r, R, dpad_ref[pl.ds(r, R), :], dwacc_ref, offs)
            return carry

        lax.fori_loop(0, S // R, conv_pass, 0)
        for k in range(K):
            dw_ref[k:k + 1, :] = _colsum(dwacc_ref[SUBLANES * k:SUBLANES * (k + 1), :])

    col = lambda i: (0, i)
    row = jax.ShapeDtypeStruct((1, C), F32)
    blk = 2 * (4 * S * LANES * 4 + 2 * S * LANES * 2) + 2 * (S + _pad_rows(K)) * LANES * 4
    return _streamed_call(
        body, name=name, grid=(C // LANES,),
        in_specs=[_bs((S, LANES), col)] * 4 + [_bs((K, LANES), col), _bs((1, LANES), col), _bs((1, LANES), col)],
        out_specs=[_bs((S, LANES), col), _bs((S, LANES), col), _bs((K, LANES), col)] + [_bs((1, LANES), col)] * 5,
        out_shape=[jax.ShapeDtypeStruct((S, C), BF16), jax.ShapeDtypeStruct((S, C), BF16),
                   jax.ShapeDtypeStruct((K, C), F32), row, row, row, row, row],
        scratch_shapes=[pltpu.VMEM((S + _pad_rows(K), LANES), F32), pltpu.VMEM((S + _pad_rows(K), LANES), F32),
                        pltpu.VMEM((SUBLANES * K, LANES), F32)],
        compiler_params=_cparams(blk),
    )(dc4, c2, cv, cg, w, gg, gb)


def _short_conv_fwd(xin, w, b, name):
    S, C = xin.shape
    K = w.shape[0]
    padf = _front_pad(K)
    R = min(CONV_ROWS, S)
    E = min(EW_ROWS, S)

    def body(x_ref, w_ref, b_ref, o_ref, pad_ref):
        pad_ref[0:padf, :] = jnp.zeros((padf, LANES), F32)
        pad_ref[S + padf:S + padf + SUBLANES, :] = jnp.zeros((SUBLANES, LANES), F32)

        def fill(i, carry):
            r = pl.multiple_of(i * E, E)
            pad_ref[pl.ds(r + padf, E), :] = x_ref[pl.ds(r, E), :]
            return carry

        lax.fori_loop(0, S // E, fill, 0)
        taps = [(padf - (K - 1) + k, functools.partial(lambda k: w_ref[k:k + 1, :], k)) for k in range(K)]

        def conv(i, carry):
            r = pl.multiple_of(i * R, R)
            o_ref[pl.ds(r, R), :] = _shifted_taps(pad_ref, r, R, taps) + b_ref[...]
            return carry

        lax.fori_loop(0, S // R, conv, 0)

    col = lambda i: (0, i)
    blk = 2 * (2 * S * LANES * 4) + (S + _pad_rows(K)) * LANES * 4
    return _streamed_call(
        body, name=name, grid=(C // LANES,),
        in_specs=[_bs((S, LANES), col), _bs((K, LANES), col), _bs((1, LANES), col)],
        out_specs=_bs((S, LANES), col),
        out_shape=jax.ShapeDtypeStruct((S, C), F32),
        scratch_shapes=[pltpu.VMEM((S + _pad_rows(K), LANES), F32)],
        compiler_params=_cparams(blk),
    )(xin, w, b)


def _short_conv_bwd(dy, xin, w, name):
    S, C = xin.shape
    K = w.shape[0]
    padf = _front_pad(K)
    R = min(CONV_ROWS, S)
    E = min(EW_ROWS, S)

    def body(dy_ref, x_ref, w_ref, dx_ref, dw_ref, db_ref, sx_ref, dpad_ref, xpad_ref, dwacc_ref):
        xpad_ref[0:padf, :] = jnp.zeros((padf, LANES), F32)
        xpad_ref[S + padf:S + padf + SUBLANES, :] = jnp.zeros((SUBLANES, LANES), F32)
        dpad_ref[S:S + padf + SUBLANES, :] = jnp.zeros((padf + SUBLANES, LANES), F32)
        dwacc_ref[...] = jnp.zeros_like(dwacc_ref)
        db_ref[...] = jnp.zeros_like(db_ref)
        sx_ref[...] = jnp.zeros_like(sx_ref)

        def fill(i, carry):
            r = pl.multiple_of(i * E, E)
            d = dy_ref[pl.ds(r, E), :]
            dpad_ref[pl.ds(r, E), :] = d
            db_ref[...] += _colsum(d)
            xpad_ref[pl.ds(r + padf, E), :] = x_ref[pl.ds(r, E), :]
            return carry

        lax.fori_loop(0, S // E, fill, 0)
        taps = [(K - 1 - k, functools.partial(lambda k: w_ref[k:k + 1, :], k)) for k in range(K)]
        offs = [padf - (K - 1) + k for k in range(K)]

        def conv_pass(i, carry):
            r = pl.multiple_of(i * R, R)
            dx = _shifted_taps(dpad_ref, r, R, taps)
            dx_ref[pl.ds(r, R), :] = dx.astype(BF16)
            sx_ref[...] += _colsum(dx)
            _shifted_corr(xpad_ref, r, R, dpad_ref[pl.ds(r, R), :], dwacc_ref, offs)
            return carry

        lax.fori_loop(0, S // R, conv_pass, 0)
        for k in range(K):
            dw_ref[k:k + 1, :] = _colsum(dwacc_ref[SUBLANES * k:SUBLANES * (k + 1), :])

    col = lambda i: (0, i)
    row = jax.ShapeDtypeStruct((1, C), F32)
    blk = 2 * (2 * S * LANES * 4 + S * LANES * 2) + 2 * (S + _pad_rows(K)) * LANES * 4
    return _streamed_call(
        body, name=name, grid=(C // LANES,),
        in_specs=[_bs((S, LANES), col), _bs((S, LANES), col), _bs((K, LANES), col)],
        out_specs=[_bs((S, LANES), col), _bs((K, LANES), col), _bs((1, LANES), col), _bs((1, LANES), col)],
        out_shape=[jax.ShapeDtypeStruct((S, C), BF16), jax.ShapeDtypeStruct((K, C), F32), row, row],
        scratch_shapes=[pltpu.VMEM((S + _pad_rows(K), LANES), F32), pltpu.VMEM((S + _pad_rows(K), LANES), F32),
                        pltpu.VMEM((SUBLANES * K, LANES), F32)],
        compiler_params=_cparams(blk),
    )(dy, xin, w)


def _band_panels(width, block):
    assert width % LANES == 0 and block <= LANES
    panels = []
    for c0 in range(0, width, 2 * LANES):
        c1 = min(width, c0 + 2 * LANES)
        r0 = (c0 // block) * block // LANES * LANES
        r1 = min(width, -(-(-(-c1 // block) * block) // LANES) * LANES)
        panels.append((r0, r1, c0, c1))
    return panels


def _gates_fwd(r1, wa, wx, ba, bx, lam, name):
    S, R = r1.shape
    tm = _divisor_tile(S, 256, 16)
    panels = _band_panels(R, R // RNN_BLOCKS)

    def body(r1_ref, wa_ref, wx_ref, ba_ref, bx_ref, lam_ref, ra_ref, ri_ref, a_ref, uu_ref):
        for r0, r1e, c0, c1 in panels:
            rb = r1_ref[:, r0:r1e].astype(BF16)
            ra = _sigmoid(_dot(rb, wa_ref[r0:r1e, c0:c1]) + ba_ref[:, c0:c1])
            ri = _sigmoid(_dot(rb, wx_ref[r0:r1e, c0:c1]) + bx_ref[:, c0:c1])
            log_a = -RG_LRU_C * ra * _softplus(-lam_ref[:, c0:c1])
            ra_ref[:, c0:c1] = ra
            ri_ref[:, c0:c1] = ri
            a_ref[:, c0:c1] = jnp.exp(log_a)
            uu_ref[:, c0:c1] = jnp.sqrt(_neg_expm1(2.0 * log_a)) * (ri * r1_ref[:, c0:c1])

    blk = 2 * (5 * tm * R * 4) + 2 * R * R * 2 + 6 * tm * R * 4
    tile = _bs((tm, R), lambda i: (i, 0))
    return _streamed_call(
        body, name=name, grid=(S // tm,),
        in_specs=[tile, _resident((R, R)), _resident((R, R)), _resident((1, R)), _resident((1, R)), _resident((1, R))],
        out_specs=[tile] * 4,
        out_shape=[jax.ShapeDtypeStruct((S, R), F32)] * 4,
        compiler_params=_cparams(blk),
    )(r1, wa, wx, ba, bx, lam)


def _gates_bwd(guu, da, ra, ri, r1, wa, wx, lam, name):
    S, R = r1.shape
    tm = _divisor_tile(S, 256, 16)
    nsteps = S // tm
    panels = _band_panels(R, R // RNN_BLOCKS)

    def body(g_ref, da_ref, ra_ref, ri_ref, r1_ref, wa_ref, wx_ref, lam_ref,
             dr1_ref, dpa_ref, dpx_ref, dba_ref, dbx_ref, dlam_ref):
        i = pl.program_id(0)

        @pl.when(i == 0)
        def _():
            dba_ref[...] = jnp.zeros_like(dba_ref)
            dbx_ref[...] = jnp.zeros_like(dbx_ref)
            dlam_ref[...] = jnp.zeros_like(dlam_ref)

        g = g_ref[...]
        ra = ra_ref[...]
        ri = ri_ref[...]
        r1_ = r1_ref[...]
        sp = _softplus(-lam_ref[...])
        log_a = -RG_LRU_C * ra * sp
        a = jnp.exp(log_a)
        mult = jnp.sqrt(_neg_expm1(2.0 * log_a))
        d_ri = g * mult * r1_
        dr1 = g * mult * ri
        dmult = g * ri * r1_
        dlog_a = da_ref[...] * a - dmult * (a * a) / mult
        dra = dlog_a * (-RG_LRU_C * sp)
        dlam_ref[...] += _colsum(dlog_a * (-RG_LRU_C * ra))
        dpa = dra * ra * (1.0 - ra)
        dpx = d_ri * ri * (1.0 - ri)
        dba_ref[...] += _colsum(dpa)
        dbx_ref[...] += _colsum(dpx)
        dpa_b = dpa.astype(BF16)
        dpx_b = dpx.astype(BF16)
        dpa_ref[...] = dpa_b
        dpx_ref[...] = dpx_b
        dr1_ref[...] = dr1
        for k0, k1, c0, c1 in panels:
            dr1_ref[:, c0:c1] += (_dot_nt(dpa_ref[:, k0:k1], wa_ref[c0:c1, k0:k1])
                                  + _dot_nt(dpx_ref[:, k0:k1], wx_ref[c0:c1, k0:k1]))

        @pl.when(i == nsteps - 1)
        def _():
            dlam_ref[...] = dlam_ref[...] * (-_sigmoid(-lam_ref[...]))

    blk = 2 * (6 * tm * R * 4 + 2 * tm * R * 2) + 2 * R * R * 2 + 10 * tm * R * 4
    tile = _bs((tm, R), lambda i: (i, 0))
    rowspec = _bs((1, R), lambda i: (0, 0))
    row = jax.ShapeDtypeStruct((1, R), F32)
    return _streamed_call(
        body, name=name, grid=(nsteps,),
        in_specs=[tile] * 5 + [_resident((R, R)), _resident((R, R)), _resident((1, R))],
        out_specs=[tile, tile, tile, rowspec, rowspec, rowspec],
        out_shape=[jax.ShapeDtypeStruct((S, R), F32), jax.ShapeDtypeStruct((S, R), BF16), jax.ShapeDtypeStruct((S, R), BF16),
                   row, row, row],
        compiler_params=_cparams(blk),
    )(guu, da, ra, ri, r1, wa, wx, lam)


def _embed_blocks(w, name):
    H, bk, _ = w.shape

    def body(w_ref, o_ref):
        o_ref[...] = jnp.zeros_like(o_ref)
        for h in range(H):
            o_ref[bk * h:bk * (h + 1), bk * h:bk * (h + 1)] = w_ref[h].astype(BF16)

    return pl.pallas_call(body, name=name, out_shape=jax.ShapeDtypeStruct((H * bk, H * bk), BF16),
                          compiler_params=_cparams(3 * H * bk * H * bk * 2))(w)


def _block_grads(r1, dpa, dpx, name):
    S, R = r1.shape
    bk = R // RNN_BLOCKS
    tk = _divisor_tile(S, 512, 16)
    nsteps = S // tk
    panels = _band_panels(R, bk)

    def body(r1_ref, dpa_ref, dpx_ref, ga_ref, gx_ref, acca_ref, accx_ref):
        k = pl.program_id(0)

        @pl.when(k == 0)
        def _():
            acca_ref[...] = jnp.zeros_like(acca_ref)
            accx_ref[...] = jnp.zeros_like(accx_ref)

        for k0, k1, c0, c1 in panels:
            rb = r1_ref[:, k0:k1].astype(BF16)
            acca_ref[k0:k1, c0:c1] += _dot_tn(rb, dpa_ref[:, c0:c1])
            accx_ref[k0:k1, c0:c1] += _dot_tn(rb, dpx_ref[:, c0:c1])

        @pl.when(k == nsteps - 1)
        def _():
            for h in range(RNN_BLOCKS):
                ga_ref[h] = acca_ref[bk * h:bk * (h + 1), bk * h:bk * (h + 1)]
                gx_ref[h] = accx_ref[bk * h:bk * (h + 1), bk * h:bk * (h + 1)]

    tile = lambda: _bs((tk, R), lambda k: (k, 0))
    out = _bs((RNN_BLOCKS, bk, bk), lambda k: (0, 0, 0))
    sds = jax.ShapeDtypeStruct((RNN_BLOCKS, bk, bk), F32)
    return _streamed_call(
        body, name=name, grid=(nsteps,),
        in_specs=[tile(), tile(), tile()], out_specs=[out, out], out_shape=[sds, sds],
        scratch_shapes=[pltpu.VMEM((R, R), F32), pltpu.VMEM((R, R), F32)],
        compiler_params=_cparams(2 * (tk * R * 8) + 2 * R * R * 4 + 4 * tk * R * 4),
    )(r1, dpa, dpx)


def _scan_geometry(S):
    nseg = SCAN_SEGMENTS if S % (SCAN_SEGMENTS * SUBLANES) == 0 else SUBLANES
    return nseg, S // nseg


def _steps(n, step, init):
    u = SCAN_UNROLL

    def trip(t, carry):
        for k in range(u):
            carry = step(t * u + k, carry)
        return carry

    carry = lax.fori_loop(0, n // u, trip, init)
    for j in range(n - n % u, n):
        carry = step(j, carry)
    return carry


def _scan_fwd(a, u, name):
    S, C = a.shape
    nseg, L = _scan_geometry(S)
    T = min(SUBLANES, L)

    def body(a3, u3, h3, ta_ref, tu_ref, e_ref, p_ref, init_ref):

        def to_steps(i, carry):
            j0 = pl.multiple_of(i * T, T)
            ta_ref[pl.ds(j0, T)] = jnp.swapaxes(a3[:, pl.ds(j0, T), :], 0, 1)
            tu_ref[pl.ds(j0, T)] = jnp.swapaxes(u3[:, pl.ds(j0, T), :], 0, 1)
            return carry

        lax.fori_loop(0, L // T, to_steps, 0)

        def run1(j, carry):
            hs, ps = carry
            aj = ta_ref[j]
            return aj * hs + tu_ref[j], aj * ps

        e_ref[...], p_ref[...] = _steps(L, run1, (jnp.zeros((nseg, LANES), F32), jnp.ones((nseg, LANES), F32)))
        init_ref[0:1, :] = jnp.zeros((1, LANES), F32)
        for s in range(1, nseg):
            init_ref[s:s + 1, :] = e_ref[s - 1:s, :] + p_ref[s - 1:s, :] * init_ref[s - 1:s, :]

        def run2(j, hs):
            hs = ta_ref[j] * hs + tu_ref[j]
            tu_ref[j] = hs
            return hs

        _steps(L, run2, init_ref[...])

        def from_steps(i, carry):
            j0 = pl.multiple_of(i * T, T)
            h3[:, pl.ds(j0, T), :] = jnp.swapaxes(tu_ref[pl.ds(j0, T)], 0, 1)
            return carry

        lax.fori_loop(0, L // T, from_steps, 0)

    seg_block = _bs((nseg, L, LANES), lambda i: (0, 0, i))
    blk = 2 * (3 * S * LANES * 4) + 2 * S * LANES * 4
    return _streamed_call(
        body, name=name, grid=(C // LANES,),
        in_specs=[seg_block, seg_block],
        out_specs=seg_block,
        out_shape=jax.ShapeDtypeStruct((nseg, L, C), F32),
        scratch_shapes=[pltpu.VMEM((L, nseg, LANES), F32)] * 2 + [pltpu.VMEM((nseg, LANES), F32)] * 3,
        compiler_params=_cparams(blk),
    )(a.reshape(nseg, L, C), u.reshape(nseg, L, C)).reshape(S, C)


def _scan_bwd(a, dh, h, name):
    S, C = a.shape
    nseg, L = _scan_geometry(S)
    T = min(SUBLANES, L)
    assert L >= 2

    def body(a3, d3, h3, g3, da3, ta_ref, td_ref, th_ref, e_ref, p_ref, init_ref):

        def to_steps(i, carry):
            j0 = pl.multiple_of(i * T, T)
            for src, dst in ((a3, ta_ref), (d3, td_ref), (h3, th_ref)):
                dst[pl.ds(j0, T)] = jnp.swapaxes(src[:, pl.ds(j0, T), :], 0, 1)
            return carry

        lax.fori_loop(0, L // T, to_steps, 0)
        seg = lax.broadcasted_iota(jnp.int32, (nseg, LANES), 0)
        b_last = jnp.where(seg == nseg - 1, 0.0, pltpu.roll(ta_ref[0], nseg - 1, axis=0))
        h_first = jnp.where(seg == 0, 0.0, pltpu.roll(th_ref[L - 1], 1, axis=0))

        def run1(jj, carry):
            gs, ps = carry
            j = L - 2 - jj
            bj = ta_ref[j + 1]
            return bj * gs + td_ref[j], bj * ps

        e_ref[...], p_ref[...] = _steps(L - 1, run1, (td_ref[L - 1], b_last))
        init_ref[nseg - 1:nseg, :] = jnp.zeros((1, LANES), F32)
        for s in range(nseg - 2, -1, -1):
            init_ref[s:s + 1, :] = e_ref[s + 1:s + 2, :] + p_ref[s + 1:s + 2, :] * init_ref[s + 1:s + 2, :]

        gs = b_last * init_ref[...] + td_ref[L - 1]
        td_ref[L - 1] = gs
        th_ref[L - 1] = gs * th_ref[L - 2]

        def run2(jj, gs):
            j = L - 2 - jj
            gs = ta_ref[j + 1] * gs + td_ref[j]
            td_ref[j] = gs
            th_ref[j] = gs * th_ref[j - 1]
            return gs

        gs = _steps(L - 2, run2, gs)
        gs = ta_ref[1] * gs + td_ref[0]
        td_ref[0] = gs
        th_ref[0] = gs * h_first

        def from_steps(i, carry):
            j0 = pl.multiple_of(i * T, T)
            g3[:, pl.ds(j0, T), :] = jnp.swapaxes(td_ref[pl.ds(j0, T)], 0, 1)
            da3[:, pl.ds(j0, T), :] = jnp.swapaxes(th_ref[pl.ds(j0, T)], 0, 1)
            return carry

        lax.fori_loop(0, L // T, from_steps, 0)

    seg_block = _bs((nseg, L, LANES), lambda i: (0, 0, i))
    blk = 2 * (5 * S * LANES * 4) + 3 * S * LANES * 4
    g, da = _streamed_call(
        body, name=name, grid=(C // LANES,),
        in_specs=[seg_block] * 3,
        out_specs=[seg_block] * 2,
        out_shape=[jax.ShapeDtypeStruct((nseg, L, C), F32)] * 2,
        scratch_shapes=[pltpu.VMEM((L, nseg, LANES), F32)] * 3 + [pltpu.VMEM((nseg, LANES), F32)] * 3,
        compiler_params=_cparams(blk),
    )(a.reshape(nseg, L, C), dh.reshape(nseg, L, C), h.reshape(nseg, L, C))
    return g.reshape(S, C), da.reshape(S, C)


def _mixer_out_fwd(c4, h, rg, gc, gr, x1, wc, wr, wo, g, b, alpha, name):
    S, D = x1.shape
    R = h.shape[1]
    tm = _divisor_tile(S, 256, 16)

    def body(c4_ref, h_ref, rg_ref, gc_ref, gr_ref, x_ref, wc_ref, wr_ref, wo_ref, g_ref, b_ref,
             yc_ref, yr_ref, z_ref, y_ref):
        yc = _dot(c4_ref[...], wc_ref[...])
        q = (h_ref[...] * _gelu(rg_ref[...].astype(F32))).astype(BF16)
        yr = _dot(q, wr_ref[...])
        yc_ref[...] = yc.astype(BF16)
        yr_ref[...] = yr.astype(BF16)
        m = (_sigmoid(gc_ref[...].astype(F32)) * yc + _sigmoid(gr_ref[...].astype(F32)) * yr).astype(BF16)
        z = alpha * x_ref[...] + _dot(m, wo_ref[...])
        z_ref[...] = z
        y_ref[...] = _norm_fwd(z, g_ref[...], b_ref[...])

    blk = 2 * (tm * D * 2 + 2 * tm * R * 4 + 7 * tm * D * 4) + (2 * D * D + R * D) * 2 + 6 * tm * D * 4
    td = _bs((tm, D), lambda i: (i, 0))
    tr = _bs((tm, R), lambda i: (i, 0))
    return _streamed_call(
        body, name=name, grid=(S // tm,),
        in_specs=[td, tr, tr, td, td, td, _layer_resident(wc)[1], _layer_resident(wr)[1], _layer_resident(wo)[1],
                  _resident((1, D)), _resident((1, D))],
        out_specs=[td] * 4,
        out_shape=[jax.ShapeDtypeStruct((S, D), BF16)] * 2 + [jax.ShapeDtypeStruct((S, D), F32)] * 2,
        compiler_params=_cparams(blk),
    )(c4, h, rg, gc, gr, x1, wc[0], wr[0], wo[0], g, b)


def _mixer_out_bwd(dy, z, g, wo, yc, yr, gc, gr, name):
    S, D = dy.shape
    tm = _divisor_tile(S, 256, 16)

    def body(dy_ref, z_ref, g_ref, wo_ref, yc_ref, yr_ref, gc_ref, gr_ref,
             dz_ref, dzb_ref, m_ref, dyc_ref, dyr_ref, dgc_ref, dgr_ref, sgc_ref, sgr_ref, dg_ref, db_ref):
        @pl.when(pl.program_id(0) == 0)
        def _():
            for ref in (sgc_ref, sgr_ref, dg_ref, db_ref):
                ref[...] = jnp.zeros_like(ref)

        dy_ = dy_ref[...]
        dz, xhat = _norm_bwd(z_ref[...], g_ref[...], dy_)
        dg_ref[...] += _colsum(dy_ * xhat)
        db_ref[...] += _colsum(dy_)
        dz_ref[...] = dz
        dzb = dz.astype(BF16)
        dzb_ref[...] = dzb
        dm = _dot_nt(dzb, wo_ref[...])
        yc = yc_ref[...].astype(F32)
        yr = yr_ref[...].astype(F32)
        sc = _sigmoid(gc_ref[...].astype(F32))
        sr = _sigmoid(gr_ref[...].astype(F32))
        m_ref[...] = (sc * yc + sr * yr).astype(BF16)
        dyc_ref[...] = (dm * sc).astype(BF16)
        dyr_ref[...] = (dm * sr).astype(BF16)
        dgc = dm * yc * sc * (1.0 - sc)
        dgr = dm * yr * sr * (1.0 - sr)
        dgc_ref[...] = dgc.astype(BF16)
        dgr_ref[...] = dgr.astype(BF16)
        sgc_ref[...] += _colsum(dgc)
        sgr_ref[...] += _colsum(dgr)

    blk = 2 * (7 * tm * D * 4 + 6 * tm * D * 2) + D * D * 2 + 8 * tm * D * 4
    td = _bs((tm, D), lambda i: (i, 0))
    rowspec = _bs((1, D), lambda i: (0, 0))
    row = jax.ShapeDtypeStruct((1, D), F32)
    bfd = jax.ShapeDtypeStruct((S, D), BF16)
    return _streamed_call(
        body, name=name, grid=(S // tm,),
        in_specs=[td, td, _resident((1, D)), _layer_resident(wo)[1], td, td, td, td],
        out_specs=[td] * 7 + [rowspec] * 4,
        out_shape=[jax.ShapeDtypeStruct((S, D), F32), bfd, bfd, bfd, bfd, bfd, bfd, row, row, row, row],
        compiler_params=_cparams(blk),
    )(dy, z, g, wo[0], yc, yr, gc, gr)


def _branch_bwd(dyc, dyr, wc, wr, h, rg, name):
    S, D = dyc.shape
    R = h.shape[1]
    tm = _divisor_tile(S, 256, 16)

    def body(dyc_ref, dyr_ref, wc_ref, wr_ref, h_ref, rg_ref, dc4_ref, dh_ref, drg_ref, q_ref, srg_ref):
        @pl.when(pl.program_id(0) == 0)
        def _():
            srg_ref[...] = jnp.zeros_like(srg_ref)

        dc4_ref[...] = _dot_nt(dyc_ref[...], wc_ref[...]).astype(BF16)
        dq = _dot_nt(dyr_ref[...], wr_ref[...])
        h_ = h_ref[...]
        rg_ = rg_ref[...].astype(F32)
        ge = _gelu(rg_)
        dh_ref[...] = dq * ge
        drg = dq * h_ * _gelu_grad(rg_)
        drg_ref[...] = drg.astype(BF16)
        srg_ref[...] += _colsum(drg)
        q_ref[...] = (h_ * ge).astype(BF16)

    blk = 2 * (2 * tm * D * 2 + tm * D * 4 + 3 * tm * R * 4 + 2 * tm * R * 2) + (D * D + R * D) * 2 + 6 * tm * R * 4
    td = _bs((tm, D), lambda i: (i, 0))
    tr = _bs((tm, R), lambda i: (i, 0))
    return _streamed_call(
        body, name=name, grid=(S // tm,),
        in_specs=[td, td, _layer_resident(wc)[1], _layer_resident(wr)[1], tr, tr],
        out_specs=[td, tr, tr, tr, _bs((1, R), lambda i: (0, 0))],
        out_shape=[jax.ShapeDtypeStruct((S, D), BF16), jax.ShapeDtypeStruct((S, R), F32), jax.ShapeDtypeStruct((S, R), BF16),
                   jax.ShapeDtypeStruct((S, R), BF16), jax.ShapeDtypeStruct((1, R), F32)],
        compiler_params=_cparams(blk),
    )(dyc, dyr, wc[0], wr[0], h, rg)


def _loss_head(y, target, name):
    S, D = y.shape
    tm = _divisor_tile(S, 512, 16)
    nsteps = S // tm

    def body(y_ref, t_ref, loss_ref, dy_ref, acc_ref):
        i = pl.program_id(0)

        @pl.when(i == 0)
        def _():
            acc_ref[...] = jnp.zeros_like(acc_ref)

        err = y_ref[...] - t_ref[...]
        dy_ref[...] = err * (1.0 / D)
        acc_ref[...] += _colsum(err * err)

        @pl.when(i == nsteps - 1)
        def _():
            loss_ref[...] = jnp.sum(acc_ref[...], axis=-1, keepdims=True) * (0.5 / D)

    td = _bs((tm, D), lambda i: (i, 0))
    return _streamed_call(
        body, name=name, grid=(nsteps,),
        in_specs=[td, td],
        out_specs=[_bs((1, 1), lambda i: (0, 0)), td],
        out_shape=[jax.ShapeDtypeStruct((1, 1), F32), jax.ShapeDtypeStruct((S, D), F32)],
        scratch_shapes=[pltpu.VMEM((1, D), F32)],
        compiler_params=_cparams(2 * 3 * tm * D * 4),
    )(y, target)


def _adamw_math(w, g, m, v):
    m = ADAM_B1 * m + (1.0 - ADAM_B1) * g
    v = ADAM_B2 * v + (1.0 - ADAM_B2) * (g * g)
    m_hat = m / (1.0 - ADAM_B1 ** ADAM_STEP)
    v_hat = v / (1.0 - ADAM_B2 ** ADAM_STEP)
    delta = -ADAM_LR * (m_hat / (jnp.sqrt(v_hat) + ADAM_EPS) + ADAM_WD * w)
    return delta, m, v


def _adamw_sharded(w, m, v, own, sib, rem, layer, filled, name):
    layers, r, c = w.shape
    r2 = r // 2
    tr = _divisor_tile(r2, max(16, (1 << 20) // (4 * c) // 16 * 16), 16)
    n_out = 4

    def body(w_ref, m_ref, v_ref, own_ref, sib_ref, rem_ref, *rest):
        g_ref, d_ref, nm_ref, nv_ref = rest[-n_out:]
        mine = pl.program_id(0) == lax.axis_index("c")
        g = jnp.where(mine, own_ref[...], sib_ref[...]).astype(F32)
        for j in range(N_CHIPS - 1):
            g = g + rem_ref[j].astype(F32)
        delta, nm, nv = _adamw_math(w_ref[...], g, m_ref[...], v_ref[...])
        g_ref[...] = g
        d_ref[...] = delta
        nm_ref[...] = nm
        nv_ref[...] = nv

    halves = lambda a: a.reshape(layers, 2, r2, c)
    tile = _bs((None, None, tr, c), lambda h, i: (layer, h, i, 0))
    flat = _bs((tr, c), lambda h, i: (i, 0))
    sds = jax.ShapeDtypeStruct((layers, 2, r2, c), F32)
    passed = [] if filled is None else [halves(a) for a in filled]
    outs = _streamed_call(
        body, name=name, grid=(2, r2 // tr),
        in_specs=[tile, tile, tile, flat, flat, _bs((N_CHIPS - 1, None, tr, c), lambda h, i: (0, h, i, 0))] + [ANY] * len(passed),
        out_specs=[tile] * n_out,
        out_shape=[sds] * n_out,
        input_output_aliases={6 + k: k for k in range(len(passed))},
        compiler_params=_cparams(2 * (7 * tr * c * 4 + (N_CHIPS + 1) * tr * c * 2) + 6 * tr * c * 4),
    )(halves(w), halves(m), halves(v), own, sib, rem, *passed)
    return [o.reshape(layers, r, c) for o in outs]


def _adamw_flat(w, m, v, g, name):
    rows = w.shape[0]
    tr = _divisor_tile(rows, 1024, SUBLANES)

    def body(w_ref, m_ref, v_ref, g_ref, d_ref, nm_ref, nv_ref):
        delta, nm, nv = _adamw_math(w_ref[...], g_ref[...], m_ref[...], v_ref[...])
        d_ref[...] = delta
        nm_ref[...] = nm
        nv_ref[...] = nv

    tile = _bs((tr, LANES), lambda i: (i, 0))
    sds = jax.ShapeDtypeStruct(w.shape, F32)
    return _streamed_call(
        body, name=name, grid=(rows // tr,),
        in_specs=[tile] * 4, out_specs=[tile] * 3, out_shape=[sds] * 3,
        compiler_params=_cparams(2 * 7 * tr * LANES * 4),
    )(w, m, v, g)


def _half_view(g, by_cols):
    rows, cols = g.shape
    if by_cols:
        return g.reshape(2, rows // 2, cols)
    return g.reshape(N_CHIPS, 2, rows // (2 * N_CHIPS), cols)


def _pair_sum_bf16(view, theirs, by_cols, name):
    rows, c = theirs.shape[-2:]
    tr = _divisor_tile(rows, max(16, (1 << 20) // (4 * c) // 16 * 16), 16)

    def body(g0_ref, g1_ref, t_ref, o_ref):
        mine = jnp.where(lax.axis_index("c") == 0, g0_ref[...], g1_ref[...])
        o_ref[...] = (mine + t_ref[...].astype(F32)).astype(BF16)

    if by_cols:
        grid = (rows // tr,)
        halves = [_bs((None, tr, c), functools.partial(lambda h, i: (h, i, 0), h)) for h in range(2)]
        tile = _bs((tr, c), lambda i: (i, 0))
    else:
        grid = (N_CHIPS, rows // tr)
        halves = [_bs((None, None, tr, c), functools.partial(lambda h, k, i: (k, h, i, 0), h)) for h in range(2)]
        tile = _bs((None, tr, c), lambda k, i: (k, i, 0))
    return _streamed_call(
        body, name=name, grid=grid,
        in_specs=halves + [tile], out_specs=tile, out_shape=jax.ShapeDtypeStruct(theirs.shape, BF16),
        compiler_params=_cparams(2 * 4 * tr * c * 4),
    )(view, view, theirs)


ANY = pl.BlockSpec(memory_space=pl.ANY)


def _mesh_position():
    return lax.axis_index("x"), lax.axis_index("y"), lax.axis_index("c")


def _other_chips():
    x, y, c = _mesh_position()
    chips = [(1 - x, y), (x, 1 - y), (1 - x, 1 - y)]
    return 2 * x + y, (x, y, 1 - c), chips, [2 * cx + cy for cx, cy in chips]


def _chip_slab(ref, k, width, by_cols):
    if by_cols:
        start = k * width if isinstance(k, int) else pl.multiple_of(k * width, LANES)
        return ref.at[:, pl.ds(start, width)]
    return ref.at[k]


HBM = pl.BlockSpec(memory_space=pltpu.HBM)
SEM = pl.BlockSpec(memory_space=pltpu.SEMAPHORE)
DATAFLOW = pltpu.SideEffectType.DATAFLOW_SIDE_EFFECTING
N_GATHER_COPIES = 4


def _land_shape(src, by_cols):
    return src.shape[:-1] + (N_CHIPS * src.shape[-1],) if by_cols else (N_CHIPS,) + src.shape


def _gather_copy(src_ref, land_ref, by_cols, send_sems, recv_sems, pos, j, slab, to):
    width = src_ref.shape[-1]
    return pltpu.make_async_remote_copy(src_ref=src_ref, dst_ref=_chip_slab(land_ref, slab, width, by_cols),
                                        send_sem=send_sems.at[N_GATHER_COPIES * pos + j],
                                        recv_sem=recv_sems.at[N_GATHER_COPIES * pos + j],
                                        device_id=to, device_id_type=MESH)


def _gather_start(srcs, by_cols, groups, name):
    U = len(srcs)
    G = len(groups)
    lands = [lax.empty(_land_shape(s, bc), s.dtype) for s, bc in zip(srcs, by_cols)]

    def body(*refs):
        src = refs[:U]
        land = refs[U:2 * U]
        send_sems = refs[2 * U:2 * U + G]
        recv_sems = refs[2 * U + G:2 * U + 2 * G]
        token = refs[-1]
        c = lax.axis_index("c")
        me, sibling, chips, _ = _other_chips()
        targets = [(*chip, c) for chip in chips] + [sibling]
        for g, members in enumerate(groups):
            for pos, u in enumerate(members):
                for j, to in enumerate(targets):
                    _gather_copy(src[u], land[u], by_cols[u], send_sems[g], recv_sems[g], pos, j, me, to).start()
        token[...] = jnp.zeros_like(token)

    sem_shapes = [pltpu.SemaphoreType.DMA((len(m) * N_GATHER_COPIES,)) for m in groups]
    outs = pl.pallas_call(
        body, name=name,
        out_shape=tuple(sem_shapes + sem_shapes + [pltpu.HBM(s.shape, s.dtype) for s in srcs]
                        + [pltpu.HBM(v.shape, v.dtype) for v in lands] + [jax.ShapeDtypeStruct((SUBLANES, LANES), F32)]),
        in_specs=[HBM] * (2 * U),
        out_specs=tuple([SEM] * (2 * G) + [HBM] * (2 * U) + [pl.BlockSpec(memory_space=pltpu.VMEM)]),
        input_output_aliases={i: 2 * G + i for i in range(2 * U)},
        compiler_params=pltpu.CompilerParams(has_side_effects=DATAFLOW),
    )(*[pltpu.with_memory_space_constraint(a, pltpu.HBM) for a in list(srcs) + lands])
    return outs[:G], outs[G:2 * G], outs[2 * G:2 * G + U], outs[2 * G + U:2 * G + 2 * U]


def _gather_wait(srcs, lands, by_cols, send_sems, recv_sems, after, name):
    n = len(srcs)

    def body(*refs):
        src = refs[:n]
        land = refs[n:2 * n]
        send_ref, recv_ref = refs[2 * n:2 * n + 2]
        _, sibling, _, _ = _other_chips()
        for pos in range(n):
            for j in range(N_GATHER_COPIES):
                cp = _gather_copy(src[pos], land[pos], by_cols[pos], send_ref, recv_ref, pos, j, 0, sibling)
                cp.wait_send()
                cp.wait_recv()

    outs = pl.pallas_call(
        body, name=name,
        out_shape=tuple([pltpu.HBM(s.shape, s.dtype) for s in srcs] + [pltpu.HBM(v.shape, v.dtype) for v in lands]),
        in_specs=[HBM] * (2 * n) + [SEM, SEM, pl.BlockSpec(memory_space=pl.ANY)],
        out_specs=tuple([HBM] * (2 * n)),
        input_output_aliases={i: i for i in range(2 * n)},
        compiler_params=pltpu.CompilerParams(has_side_effects=DATAFLOW),
    )(*srcs, *lands, send_sems, recv_sems, after)
    return outs[n:]


def _scatter_grads(csums, by_cols, name):
    n = len(csums)
    shard = [(s.shape[0], s.shape[1] // N_CHIPS) if bc else s.shape[1:] for s, bc in zip(csums, by_cols)]

    def body(*refs):
        src = refs[:n]
        rem = refs[n:2 * n]
        sib = refs[2 * n:3 * n]
        send_sems, recv_sems = refs[3 * n:]
        c = lax.axis_index("c")
        me, sibling, chips, chip_ids = _other_chips()

        def remote(i, k, src_ref, dst_ref, to):
            return pltpu.make_async_remote_copy(src_ref=src_ref, dst_ref=dst_ref, send_sem=send_sems.at[i, k],
                                                recv_sem=recv_sems.at[i, k], device_id=to, device_id_type=MESH)

        def part(i, k):
            return _chip_slab(src[i], k, shard[i][-1], by_cols[i])

        started = []
        for i in range(n):
            for j in range(3):
                started.append(remote(i, j, part(i, chip_ids[j]), rem[i].at[j, c], (*chips[j], c)))
            started.append(remote(i, 6, part(i, me), sib[i], sibling))
        for cp in started:
            cp.start()
        for i in range(n):
            for j in range(3):
                slot = rem[i].at[j, c]
                remote(i, j, slot, slot, sibling).wait_recv()
                fwd = remote(i, 3 + j, slot, slot, sibling)
                fwd.start()
                started.append(fwd)
        for i in range(n):
            for j in range(3):
                slot = rem[i].at[j, 1 - c]
                remote(i, 3 + j, slot, slot, sibling).wait_recv()
            remote(i, 6, sib[i], sib[i], sibling).wait_recv()
        for cp in started:
            cp.wait_send()

    out_shape = ([jax.ShapeDtypeStruct((N_CHIPS - 1, 2) + tuple(sh), s.dtype) for s, sh in zip(csums, shard)]
                 + [jax.ShapeDtypeStruct(tuple(sh), s.dtype) for s, sh in zip(csums, shard)])
    outs = _streamed_call(
        body, name=name,
        in_specs=[ANY] * n, out_specs=[ANY] * (2 * n), out_shape=out_shape,
        scratch_shapes=[pltpu.SemaphoreType.DMA((n, 7)), pltpu.SemaphoreType.DMA((n, 7))],
    )(*csums)
    return outs[:n], outs[n:]


def _sibling_exchange(views, by_cols, name):
    n = len(views)

    def body(*refs):
        src = refs[:n]
        theirs = refs[n:2 * n]
        send_sems, recv_sems = refs[2 * n:]
        x, y, c = _mesh_position()
        copies = []
        for i in range(n):
            half = src[i].at[1 - c] if by_cols[i] else src[i].at[:, 1 - c]
            copies.append(pltpu.make_async_remote_copy(src_ref=half, dst_ref=theirs[i], send_sem=send_sems.at[i],
                                                       recv_sem=recv_sems.at[i], device_id=(x, y, 1 - c), device_id_type=MESH))
        for cp in copies:
            cp.start()
        for cp in copies:
            cp.wait()

    out_shape = [jax.ShapeDtypeStruct(v.shape[1:] if bc else v.shape[:1] + v.shape[2:], v.dtype) for v, bc in zip(views, by_cols)]
    return _streamed_call(
        body, name=name,
        in_specs=[ANY] * n, out_specs=[ANY] * n, out_shape=out_shape,
        scratch_shapes=[pltpu.SemaphoreType.DMA((n,)), pltpu.SemaphoreType.DMA((n,))],
    )(*views)


N_SCATTER_COPIES = 7


def _scatter_start(csums, by_cols, name):
    n = len(csums)
    shard = [(s.shape[0], s.shape[1] // N_CHIPS) if bc else s.shape[1:] for s, bc in zip(csums, by_cols)]
    rems = [lax.empty((N_CHIPS - 1, 2) + tuple(sh), s.dtype) for s, sh in zip(csums, shard)]
    sibs = [lax.empty(tuple(sh), s.dtype) for s, sh in zip(csums, shard)]

    def body(*refs):
        src = refs[:n]
        rem = refs[n:2 * n]
        sib = refs[2 * n:3 * n]
        send_sems, recv_sems = refs[3 * n:3 * n + 2]
        token = refs[-1]
        c = lax.axis_index("c")
        me, sibling, chips, chip_ids = _other_chips()
        for i in range(n):
            base = N_SCATTER_COPIES * i
            for j in range(3):
                part = _chip_slab(src[i], chip_ids[j], shard[i][-1], by_cols[i])
                for core in range(2):
                    pltpu.make_async_remote_copy(src_ref=part, dst_ref=rem[i].at[j, c], send_sem=send_sems.at[base + 2 * j + core],
                                                 recv_sem=recv_sems.at[base + 2 * j + c], device_id=(*chips[j], core),
                                                 device_id_type=MESH).start()
            pltpu.make_async_remote_copy(src_ref=_chip_slab(src[i], me, shard[i][-1], by_cols[i]), dst_ref=sib[i],
                                         send_sem=send_sems.at[base + 6], recv_sem=recv_sems.at[base + 6], device_id=sibling,
                                         device_id_type=MESH).start()
        token[...] = jnp.zeros_like(token)

    sems = pltpu.SemaphoreType.DMA((N_SCATTER_COPIES * n,))
    operands = list(csums) + rems + sibs
    outs = pl.pallas_call(
        body, name=name,
        out_shape=tuple([sems, sems] + [pltpu.HBM(a.shape, a.dtype) for a in operands] + [jax.ShapeDtypeStruct((SUBLANES, LANES), F32)]),
        in_specs=[HBM] * (3 * n),
        out_specs=tuple([SEM, SEM] + [HBM] * (3 * n) + [pl.BlockSpec(memory_space=pltpu.VMEM)]),
        input_output_aliases={i: 2 + i for i in range(3 * n)},
        compiler_params=pltpu.CompilerParams(has_side_effects=DATAFLOW),
    )(*[pltpu.with_memory_space_constraint(a, pltpu.HBM) for a in operands])
    return outs[0], outs[1], outs[2:2 + n], outs[2 + n:2 + 2 * n], outs[2 + 2 * n:2 + 3 * n]


def _scatter_wait(send_sems, recv_sems, srcs, rems, sibs, by_cols, after, name):
    n = len(srcs)

    def body(*refs):
        src = refs[:n]
        rem = refs[n:2 * n]
        sib = refs[2 * n:3 * n]
        send_ref, recv_ref = refs[3 * n:3 * n + 2]
        _, sibling, _, _ = _other_chips()
        for i in range(n):
            base = N_SCATTER_COPIES * i
            width = sib[i].shape[-1]
            for j in range(3):
                for core in range(2):
                    cp = pltpu.make_async_remote_copy(src_ref=_chip_slab(src[i], 0, width, by_cols[i]), dst_ref=rem[i].at[j, core],
                                                      send_sem=send_ref.at[base + 2 * j + core],
                                                      recv_sem=recv_ref.at[base + 2 * j + core], device_id=sibling,
                                                      device_id_type=MESH)
                    cp.wait_send()
                    cp.wait_recv()
            cp = pltpu.make_async_remote_copy(src_ref=_chip_slab(src[i], 0, width, by_cols[i]), dst_ref=sib[i],
                                              send_sem=send_ref.at[base + 6], recv_sem=recv_ref.at[base + 6], device_id=sibling,
                                              device_id_type=MESH)
            cp.wait_send()
            cp.wait_recv()

    operands = list(srcs) + list(rems) + list(sibs)
    outs = pl.pallas_call(
        body, name=name,
        out_shape=tuple(pltpu.HBM(a.shape, a.dtype) for a in operands),
        in_specs=[HBM] * (3 * n) + [SEM, SEM, pl.BlockSpec(memory_space=pl.ANY)],
        out_specs=tuple([HBM] * (3 * n)),
        input_output_aliases={i: i for i in range(3 * n)},
        compiler_params=pltpu.CompilerParams(has_side_effects=DATAFLOW),
    )(*operands, send_sems, recv_sems, after)
    return outs[:n], outs[n:2 * n], outs[2 * n:3 * n]


def _all_reduce_small(v, name):
    _, rows, _ = v.shape

    def body(v_ref, o_ref, recv_ref, send_sems, recv_sems):
        x, y, c = _mesh_position()
        me = 4 * x + 2 * y + c
        peers = []
        for d in range(1, N_DEV):
            px, py, pc = x ^ ((d >> 2) & 1), y ^ ((d >> 1) & 1), c ^ (d & 1)
            peers.append(((px, py, pc), 4 * px + 2 * py + pc))

        def remote(k, src_ref, dst_ref, to):
            return pltpu.make_async_remote_copy(src_ref=src_ref, dst_ref=dst_ref, send_sem=send_sems.at[k],
                                                recv_sem=recv_sems.at[k], device_id=to, device_id_type=MESH)

        scatter = [remote(d, v_ref.at[pid], recv_ref.at[me], to) for d, (to, pid) in enumerate(peers)]
        for cp in scatter:
            cp.start()
        recv_ref[pl.ds(me, 1)] = v_ref[pl.ds(me, 1)]
        for d, (to, pid) in enumerate(peers):
            remote(d, v_ref.at[pid], recv_ref.at[pid], to).wait_recv()
        total = recv_ref[0]
        for s in range(1, N_DEV):
            total = total + recv_ref[s]
        o_ref[pl.ds(me, 1)] = total[None]
        gather = [remote(N_DEV - 1 + d, o_ref.at[me], o_ref.at[me], to) for d, (to, pid) in enumerate(peers)]
        for cp in gather:
            cp.start()
        for d, (to, pid) in enumerate(peers):
            remote(N_DEV - 1 + d, o_ref.at[pid], o_ref.at[pid], to).wait_recv()
        for cp in scatter + gather:
            cp.wait_send()

    vm = pl.BlockSpec(memory_space=pltpu.VMEM)
    return pl.pallas_call(
        body, name=name,
        in_specs=[vm], out_specs=vm, out_shape=jax.ShapeDtypeStruct(v.shape, F32),
        scratch_shapes=[pltpu.VMEM(v.shape, F32), pltpu.SemaphoreType.DMA((2 * (N_DEV - 1),)),
                        pltpu.SemaphoreType.DMA((2 * (N_DEV - 1),))],
        compiler_params=_cparams(4 * _nbytes(v.shape, F32)),
    )(v)


SHARDED_MATS = ("ffn1_w_gu", "ffn1_w_down", "mix_w_in", "conv_w_proj", "rnn_w_proj", "mix_w_out", "ffn2_w_gu", "ffn2_w_down")
COL_SHARDED = ("ffn1_w_gu", "ffn2_w_gu", "conv_dw_w")
SHARDED_VECS = ("conv_dw_w", "rnn_conv_w")
WEIGHT_NAMES = ("ffn1_w_gu", "ffn1_w_down", "ln1_g", "ln1_b", "mix_w_in", "mix_b_in", "conv_dw_w", "conv_dw_b", "conv_gn_g",
                "conv_gn_b", "conv_w_proj", "rnn_conv_w", "rnn_conv_b", "rnn_w_a", "rnn_b_a", "rnn_w_x", "rnn_b_x",
                "rnn_lambda", "rnn_w_proj", "mix_w_out", "ln2_g", "ln2_b", "ffn2_w_gu", "ffn2_w_down", "ln3_g", "ln3_b")
SMALL_NAMES = tuple(n for n in WEIGHT_NAMES if n not in SHARDED_MATS)
SECTION_NAMES = ("cv", "cg", "rx", "rg", "gc", "gr")


def _unshard_cols(gathered):
    k4, K, n = gathered.shape
    return jnp.transpose(gathered, (1, 0, 2)).reshape(K, k4 * n)


def _row(v):
    return v.reshape(1, -1)


def _layer_forward(x0, p, alpha, l, hooks):
    t = f"l{l}_"
    sv = {"x0": x0}
    x1, sv["z1"], sv["hg1"], sv["hu1"] = _ffn_fwd(x0, p["wgu1"], p["wd1"], p["ln1_g"], p["ln1_b"], alpha, t + "ffn1_fwd")
    sv["x1"] = x1
    hooks.get("after_ffn1", lambda v: None)(x1)
    sec = dict(zip(SECTION_NAMES, _mix_in(x1, p["win"], p["bin"], p["sections"], t + "mix_in")))
    sv.update(sec)
    sv["c2"], c4 = _conv_branch_fwd(sec["cv"], sec["cg"], p["conv_dw_w"], p["conv_dw_b"], p["conv_gn_g"], p["conv_gn_b"], t + "conv_fwd")
    sv["c4"] = c4
    r1 = _short_conv_fwd(sec["rx"], p["rnn_conv_w"], p["rnn_conv_b"], t + "rconv_fwd")
    sv["r1"] = r1
    sv["ra"], sv["ri"], a, uu = _gates_fwd(r1, p["wa"], p["wx"], p["rnn_b_a"], p["rnn_b_x"], p["rnn_lambda"], t + "gates_fwd")
    sv["a"] = a
    h = _scan_fwd(a, uu, t + "scan_fwd")
    sv["h"] = h
    hooks.get("after_scan", lambda v: None)(h)
    sv["yc"], sv["yr"], sv["z2"], x2 = _mixer_out_fwd(c4, h, sec["rg"], sec["gc"], sec["gr"], x1, p["wc"], p["wr"], p["wo"],
                                                      p["ln2_g"], p["ln2_b"], alpha, t + "mixout_fwd")
    sv["x2"] = x2
    hooks.get("after_mixer", lambda v: None)(x2)
    x3, sv["z3"], sv["hg2"], sv["hu2"] = _ffn_fwd(x2, p["wgu2"], p["wd2"], p["ln3_g"], p["ln3_b"], alpha, t + "ffn2_fwd")
    hooks.get("after_layer", lambda v: None)(x3)
    return x3, sv


def _layer_backward(dy, p, sv, alpha, l):
    t = f"l{l}_"
    g, gb = {}, {}
    dx2, df, a_act, dhg, dhu, g["ln3_g"], g["ln3_b"] = _ffn_bwd(dy, sv["z3"], sv["hg2"], sv["hu2"], p["wgu2"], p["wd2"],
                                                                 p["ln3_g"], alpha, t + "ffn2_bwd")
    g["ffn2_w_down"], gb["ffn2_w_down"] = _mm_tn(a_act, df, t + "dwd2")
    g["ffn2_w_gu"], gb["ffn2_w_gu"] = _mm_tn_pair(sv["x2"], dhg, dhu, t + "dwgu2")
    (dz2, dz2b, m_b, dyc, dyr, dgc, dgr, s_gc, s_gr, g["ln2_g"], g["ln2_b"]) = _mixer_out_bwd(
        dx2, sv["z2"], p["ln2_g"], p["wo"], sv["yc"], sv["yr"], sv["gc"], sv["gr"], t + "mixout_bwd")
    g["mix_w_out"], gb["mix_w_out"] = _mm_tn(m_b, dz2b, t + "dwo")
    dc4, dh, drg, q_b, s_rg = _branch_bwd(dyc, dyr, p["wc"], p["wr"], sv["h"], sv["rg"], t + "branch_bwd")
    g["conv_w_proj"], gb["conv_w_proj"] = _mm_tn(sv["c4"], dyc, t + "dwc")
    g["rnn_w_proj"], gb["rnn_w_proj"] = _mm_tn(q_b, dyr, t + "dwr")
    (dcv, dcg, g["conv_dw_w"], g["conv_dw_b"], g["conv_gn_g"], g["conv_gn_b"], s_cv, s_cg) = _conv_branch_bwd(
        dc4, sv["c2"], sv["cv"], sv["cg"], p["conv_dw_w"], p["conv_gn_g"], p["conv_gn_b"], t + "conv_bwd")
    guu, da = _scan_bwd(sv["a"], dh, sv["h"], t + "scan_bwd")
    dr1, dpa, dpx, g["rnn_b_a"], g["rnn_b_x"], g["rnn_lambda"] = _gates_bwd(
        guu, da, sv["ra"], sv["ri"], sv["r1"], p["wa"], p["wx"], p["rnn_lambda"], t + "gates_bwd")
    g["rnn_w_a"], g["rnn_w_x"] = _block_grads(sv["r1"], dpa, dpx, t + "dwax")
    drx, g["rnn_conv_w"], g["rnn_conv_b"], s_rx = _short_conv_bwd(dr1, sv["rx"], p["rnn_conv_w"], t + "rconv_bwd")
    du = {"cv": dcv, "cg": dcg, "rx": drx, "rg": drg, "gc": dgc, "gr": dgr}
    order = ("cv", "cg", "rx", "rg", "gc", "gr")
    pieces = [_mm_tn(du[s], sv["x1"], t + "dwin_" + s) for s in order]
    g["mix_w_in"] = jnp.concatenate([f for f, _ in pieces], axis=0)
    gb["mix_w_in"] = jnp.concatenate([h for _, h in pieces], axis=0)
    g["mix_b_in"] = jnp.concatenate([s_cv, s_cg, s_rx, s_rg, s_gc, s_gr], axis=1)
    dx1 = _mix_dx(dz2, [du[s] for s in order], p["win"], p["sections"], alpha, t + "mix_dx")
    dx0, df, a_act, dhg, dhu, g["ln1_g"], g["ln1_b"] = _ffn_bwd(dx1, sv["z1"], sv["hg1"], sv["hu1"], p["wgu1"], p["wd1"],
                                                                 p["ln1_g"], alpha, t + "ffn1_bwd")
    g["ffn1_w_down"], gb["ffn1_w_down"] = _mm_tn(a_act, df, t + "dwd1")
    g["ffn1_w_gu"], gb["ffn1_w_gu"] = _mm_tn_pair(sv["x0"], dhg, dhu, t + "dwgu1")
    return dx0, g, gb


def _pack_small(arrays, piece_rows):
    flat = jnp.concatenate([a.reshape(-1) for a in arrays])
    total = N_DEV * piece_rows * LANES
    return jnp.pad(flat, (0, total - flat.shape[0])).reshape(N_DEV, piece_rows, LANES)


def _unpack_small(packed, shapes):
    flat = packed.reshape(-1)
    out, off = [], 0
    for shp in shapes:
        n = 1
        for s in shp:
            n *= s
        out.append(flat[off:off + n].reshape(shp))
        off += n
    return out


def kernel(x, ffn1_w_gu, ffn1_w_down, ln1_g, ln1_b, mix_w_in, mix_b_in, conv_dw_w, conv_dw_b, conv_gn_g, conv_gn_b, conv_w_proj, rnn_conv_w, rnn_conv_b, rnn_w_a, rnn_b_a, rnn_w_x, rnn_b_x, rnn_lambda, rnn_w_proj, mix_w_out, ln2_g, ln2_b, ffn2_w_gu, ffn2_w_down, ln3_g, ln3_b, loss_target, m_ffn1_w_gu, m_ffn1_w_down, m_ln1_g, m_ln1_b, m_mix_w_in, m_mix_b_in, m_conv_dw_w, m_conv_dw_b, m_conv_gn_g, m_conv_gn_b, m_conv_w_proj, m_rnn_conv_w, m_rnn_conv_b, m_rnn_w_a, m_rnn_b_a, m_rnn_w_x, m_rnn_b_x, m_rnn_lambda, m_rnn_w_proj, m_mix_w_out, m_ln2_g, m_ln2_b, m_ffn2_w_gu, m_ffn2_w_down, m_ln3_g, m_ln3_b, v_ffn1_w_gu, v_ffn1_w_down, v_ln1_g, v_ln1_b, v_mix_w_in, v_mix_b_in, v_conv_dw_w, v_conv_dw_b, v_conv_gn_g, v_conv_gn_b, v_conv_w_proj, v_rnn_conv_w, v_rnn_conv_b, v_rnn_w_a, v_rnn_b_a, v_rnn_w_x, v_rnn_b_x, v_rnn_lambda, v_rnn_w_proj, v_mix_w_out, v_ln2_g, v_ln2_b, v_ffn2_w_gu, v_ffn2_w_down, v_ln3_g, v_ln3_b):
    args = locals()
    W = {n: args[n] for n in WEIGHT_NAMES}
    M = {n: args["m_" + n] for n in WEIGHT_NAMES}
    V = {n: args["v_" + n] for n in WEIGHT_NAMES}
    depth = ln1_g.shape[0]
    assert depth == 2, "each core of a chip moves one layer's weights and gradients"
    alpha = float((2 * depth) ** 0.25)
    S, D = x.shape[1], x.shape[2]
    F = ffn1_w_down.shape[1] * N_CHIPS
    R = rnn_w_proj.shape[1] * N_CHIPS
    chip = 2 * lax.axis_index("x") + lax.axis_index("y")

    for d in (W, M, V):
        d["mix_w_in"] = jnp.transpose(d["mix_w_in"], (0, 2, 1))

    names = SHARDED_MATS + SHARDED_VECS
    unit_groups = [[(0, "ffn1_w_gu"), (0, "ffn1_w_down")], [(0, "mix_w_in"), (0, "conv_dw_w"), (0, "rnn_conv_w")],
                   [(0, "conv_w_proj"), (0, "rnn_w_proj"), (0, "mix_w_out")], [(0, "ffn2_w_gu"), (0, "ffn2_w_down")],
                   [(1, n) for n in names]]
    order = [u for g in unit_groups for u in g]
    index = {u: i for i, u in enumerate(order)}
    groups = [[index[u] for u in g] for g in unit_groups]
    srcs = [W[n][l].astype(BF16) if n in SHARDED_MATS else W[n][l] for l, n in order]
    by_cols = [n in COL_SHARDED for _, n in order]
    send_sems, recv_sems, src_thru, land_thru = _gather_start(srcs, by_cols, groups, "gather_start")

    sections = ((0, D, F32), (D, D, F32), (2 * D, R, F32), (2 * D + R, R, BF16), (2 * D + 2 * R, D, BF16),
                (3 * D + 2 * R, D, BF16))
    keys = {"ffn1_w_gu": "wgu1", "ffn1_w_down": "wd1", "ffn2_w_gu": "wgu2", "ffn2_w_down": "wd2", "conv_w_proj": "wc",
            "rnn_w_proj": "wr", "mix_w_out": "wo"}
    params = []
    for l in range(depth):
        p = {"wa": _embed_blocks(rnn_w_a[l], f"l{l}_embed_wa"), "wx": _embed_blocks(rnn_w_x[l], f"l{l}_embed_wx")}
        for n in ("ln1_g", "ln1_b", "ln2_g", "ln2_b", "ln3_g", "ln3_b", "conv_dw_b", "conv_gn_g", "conv_gn_b", "rnn_conv_b",
                  "rnn_b_a", "rnn_b_x", "rnn_lambda"):
            p[n] = _row(W[n][l])
        p["bin"] = _row(mix_b_in[l])
        p["sections"] = sections
        params.append(p)

    def wait_group(g, after):
        ids = groups[g]
        landed = _gather_wait([src_thru[i] for i in ids], [land_thru[i] for i in ids], [by_cols[i] for i in ids],
                              send_sems[g], recv_sems[g], after, f"gather_wait{g}")
        for i, full in zip(ids, landed):
            l, n = order[i]
            p = params[l]
            if n not in COL_SHARDED:
                full = full.reshape((N_CHIPS * full.shape[1],) + full.shape[2:])
            if n == "mix_w_in":
                p["win"] = full
            elif n == "rnn_conv_w":
                p[n] = _unshard_cols(landed[ids.index(i)])
            elif n == "conv_dw_w":
                p[n] = full
            else:
                p[keys[n]] = (full[None], 0)

    h = x[0]
    wait_group(0, h)
    saved = []
    hooks = [{"after_ffn1": lambda v: wait_group(1, v), "after_scan": lambda v: wait_group(2, v),
              "after_mixer": lambda v: wait_group(3, v), "after_layer": lambda v: wait_group(4, v)}, {}]
    for l in range(depth):
        h, sv = _layer_forward(h, params[l], alpha, l, hooks[l])
        saved.append(sv)
    loss_part, dy = _loss_head(h, loss_target[0], "loss_head")
    loss = lax.psum(loss_part[0, 0], ("x", "y", "c"))
    mat_cols = [n in COL_SHARDED for n in SHARDED_MATS]

    def pair_sums(l):
        theirs = _sibling_exchange([_half_view(grads_bf16[l][n], bc) for n, bc in zip(SHARDED_MATS, mat_cols)], mat_cols,
                                   f"pair_exchange{l}")
        return [_pair_sum_bf16(_half_view(grads[l][n], bc), t, bc, f"pair_sum{l}_{n}")
                for n, bc, t in zip(SHARDED_MATS, mat_cols, theirs)]

    grads, grads_bf16 = [None] * depth, [None] * depth
    dy, grads[1], grads_bf16[1] = _layer_backward(dy, params[1], saved[1], alpha, 1)
    sums1 = pair_sums(1)
    in_flight = _scatter_start(sums1, mat_cols, "scatter_start1")
    dy, grads[0], grads_bf16[0] = _layer_backward(dy, params[0], saved[0], alpha, 0)
    grad_x = dy[None]
    sums0 = pair_sums(0)
    rem0, sib0 = _scatter_grads(sums0, mat_cols, "scatter_grads0")
    sums1, rem1, sib1 = _scatter_wait(*in_flight, mat_cols, rem0[0], "scatter_wait1")

    results = {}
    for l, sums, rem, sib in ((1, sums1, rem1, sib1), (0, sums0, rem0, sib0)):
        for n, bc, cs, rm, sb in zip(SHARDED_MATS, mat_cols, sums, rem, sib):
            if bc:
                width = cs.shape[1] // N_CHIPS
                own = lax.dynamic_slice_in_dim(cs, chip * width, width, axis=1)
            else:
                own = lax.dynamic_index_in_dim(cs, chip, axis=0, keepdims=False)
            results[n] = _adamw_sharded(W[n], M[n], V[n], own, sb, rm, l, results.get(n), f"adamw{l}_{n}")
    out_g, out_d, out_m, out_v = {}, {}, {}, {}
    for n in SHARDED_MATS:
        outs = results[n]
        if n == "mix_w_in":
            outs = [jnp.transpose(o, (0, 2, 1)) for o in outs]
        out_g[n], out_d[n], out_m[n], out_v[n] = outs

    small_grads = [jnp.stack([grads[l][n].reshape(W[n].shape[1:] if n not in SHARDED_VECS else
                                                   (W[n].shape[1], W[n].shape[2] * N_CHIPS)) for l in range(depth)])
                   for n in SMALL_NAMES]
    n_small = sum(int(a.size) for a in small_grads)
    piece_rows = -(-n_small // (N_DEV * LANES * SUBLANES)) * SUBLANES
    reduced = _unpack_small(_all_reduce_small(_pack_small(small_grads, piece_rows), "all_reduce_small"),
                            [a.shape for a in small_grads])
    local_g = []
    for n, gr in zip(SMALL_NAMES, reduced):
        if n in SHARDED_VECS:
            width = W[n].shape[2]
            gr = lax.dynamic_slice_in_dim(gr, chip * width, width, axis=2)
        local_g.append(gr)
    n_local = sum(int(a.size) for a in local_g)
    flat_rows = -(-n_local // (N_DEV * LANES * SUBLANES)) * SUBLANES * N_DEV
    pack = lambda arrs: _pack_small(arrs, flat_rows // N_DEV).reshape(flat_rows, LANES)
    shapes = [a.shape for a in local_g]
    deltas, new_m, new_v = _adamw_flat(pack([W[n] for n in SMALL_NAMES]), pack([M[n] for n in SMALL_NAMES]),
                                       pack([V[n] for n in SMALL_NAMES]), pack(local_g), "adamw_small")
    for n, gr, d_, m_, v_ in zip(SMALL_NAMES, local_g, _unpack_small(deltas, shapes), _unpack_small(new_m, shapes),
                                 _unpack_small(new_v, shapes)):
        out_g[n], out_d[n], out_m[n], out_v[n] = gr, d_, m_, v_

    return (loss, grad_x, *[out_g[n] for n in WEIGHT_NAMES], *[out_d[n] for n in WEIGHT_NAMES],
            *[out_m[n] for n in WEIGHT_NAMES], *[out_v[n] for n in WEIGHT_NAMES])
```

```python
import functools

import jax
import jax.numpy as jnp
from jax import lax
from jax.experimental import pallas as pl
from jax.experimental.pallas import tpu as pltpu

F32 = jnp.float32
BF16 = jnp.bfloat16
MESH = pl.DeviceIdType.MESH

LN_EPS = 1e-5
CONV_GROUPS = 8
RNN_BLOCKS = 16
RG_LRU_C = 8.0
ADAM_LR = 0.001
ADAM_B1 = 0.9
ADAM_B2 = 0.999
ADAM_EPS = 1e-08
ADAM_WD = 0.01
ADAM_STEP = 10

LANES = 128
SUBLANES = 8
V7X_VMEM_BYTES = 64 << 20
VMEM_LIMIT_CAP = V7X_VMEM_BYTES - (6 << 20)
N_CHIPS = 4
N_DEV = 8
CONV_ROWS = 64
EW_ROWS = 1024
SCAN_SEGMENTS = 32
SCAN_UNROLL = 4


def _cparams(block_bytes):
    limit = min(VMEM_LIMIT_CAP, max(int(block_bytes) + (8 << 20), 24 << 20))
    return pltpu.CompilerParams(vmem_limit_bytes=limit)


def _nbytes(shape, dtype):
    n = 1
    for s in shape:
        n *= s
    return n * jnp.dtype(dtype).itemsize


def _divisor_tile(n, limit, quantum):
    if n <= limit:
        return n
    best = None
    for t in range(quantum, limit + 1, quantum):
        if n % t == 0:
            best = t
    assert best is not None, (n, limit, quantum)
    return best


def _bs(shape, imap, **kw):
    return pl.BlockSpec(shape, imap, **kw)


def _resident(shape):
    nd = len(shape)
    return pl.BlockSpec(shape, lambda *_: (0,) * nd, pipeline_mode=pl.Buffered(1))


def _streamed_call(body, **kw):
    call = pl.pallas_call(body, **kw)
    return lambda *operands: call(*[pltpu.with_memory_space_constraint(o, pltpu.HBM) for o in operands])


def _layer_block(w, block, imap, **kw):
    arr, layer = w
    return arr, pl.BlockSpec((None,) + block, lambda *ids: (layer,) + imap(*ids), **kw)


def _layer_resident(w):
    arr, _ = w
    return _layer_block(w, arr.shape[1:], lambda *_: (0, 0), pipeline_mode=pl.Buffered(1))


def _sigmoid(x):
    return jax.nn.sigmoid(x)


def _dot(a, b):
    return jnp.dot(a, b, preferred_element_type=F32)


def _dot_nt(a, b):
    return lax.dot_general(a, b, (((1,), (1,)), ((), ())), preferred_element_type=F32)


def _dot_tn(a, b):
    return lax.dot_general(a, b, (((0,), (0,)), ((), ())), preferred_element_type=F32)


def _row_mean(z):
    return jnp.mean(z, axis=-1, keepdims=True)


def _lane_mean(z):
    hi = z.astype(BF16)
    lo = (z - hi.astype(F32)).astype(BF16)
    ones = jnp.full((2 * LANES, LANES), 1.0 / LANES, BF16)
    return jnp.dot(jnp.concatenate([hi, lo], axis=-1), ones, preferred_element_type=F32)


def _norm_fwd(z, g, b, mean=_row_mean):
    mu = mean(z)
    xc = z - mu
    var = mean(xc * xc)
    return xc * lax.rsqrt(var + LN_EPS) * g + b


def _norm_bwd(z, g, dy, mean=_row_mean):
    mu = mean(z)
    xc = z - mu
    var = mean(xc * xc)
    rstd = lax.rsqrt(var + LN_EPS)
    xhat = xc * rstd
    dxh = dy * g
    m1 = mean(dxh)
    m2 = mean(dxh * xhat)
    return rstd * (dxh - m1 - xhat * m2), xhat


GELU_K = 0.7978845608028654
GELU_C = 0.044715


def _gelu(x):
    return 0.5 * x * (1.0 + jnp.tanh(GELU_K * (x + GELU_C * x * x * x)))


def _gelu_grad(x):
    t = jnp.tanh(GELU_K * (x + GELU_C * x * x * x))
    return 0.5 * (1.0 + t) + 0.5 * x * (1.0 - t * t) * GELU_K * (1.0 + 3.0 * GELU_C * x * x)


def _softplus(y):
    return jnp.maximum(y, 0.0) + jnp.log1p(jnp.exp(-jnp.abs(y)))


def _neg_expm1(y):
    series = -y * (1.0 + y * (0.5 + y * (1.0 / 6.0 + y * (1.0 / 24.0 + y * (1.0 / 120.0 + y * (1.0 / 720.0))))))
    return jnp.where(y > -0.25, series, 1.0 - jnp.exp(y))


def _colsum(x):
    return jnp.sum(x, axis=0, keepdims=True)


def _shifted_taps(src_ref, base, rows, taps):
    acc = None
    for o, coef in taps:
        term = coef() * src_ref[pl.ds(base + o, rows), :]
        acc = term if acc is None else acc + term
    return acc


def _shifted_corr(src_ref, base, rows, d, acc_ref, offs):
    for k, o in enumerate(offs):
        prod = d * src_ref[pl.ds(base + o, rows), :]
        part = jnp.sum(prod.reshape(rows // SUBLANES, SUBLANES, prod.shape[-1]), axis=0)
        acc_ref[SUBLANES * k:SUBLANES * (k + 1), :] += part


def _front_pad(ktaps):
    return SUBLANES * ((ktaps - 1 + SUBLANES - 1) // SUBLANES)


def _pad_rows(ktaps):
    return _front_pad(ktaps) + SUBLANES


def _ffn_tiles(S, F):
    tm = _divisor_tile(S, 1024, 16)
    tf = _divisor_tile(F, 256, LANES)
    return tm, tf


def _ffn_fwd(x, wgu, wd, g, b, alpha, name):
    S, D = x.shape
    F = wd[0].shape[1]
    tm, tf = _ffn_tiles(S, F)
    nf = F // tf
    wg_arr, wg_spec = _layer_block(wgu, (D, tf), lambda i, j: (0, j))
    wu_arr, wu_spec = _layer_block(wgu, (D, tf), lambda i, j: (0, nf + j))
    wd_arr, wd_spec = _layer_block(wd, (tf, D), lambda i, j: (j, 0))

    def body(x_ref, wg_ref, wu_ref, wd_ref, g_ref, b_ref, y_ref, z_ref, hg_ref, hu_ref, acc_ref, xb_ref):
        j = pl.program_id(1)

        @pl.when(j == 0)
        def _():
            xb_ref[...] = x_ref[...].astype(BF16)
            acc_ref[...] = jnp.zeros_like(acc_ref)

        xb = xb_ref[...]
        hg = _dot(xb, wg_ref[...])
        hu = _dot(xb, wu_ref[...])
        hg_ref[...] = hg
        hu_ref[...] = hu
        a = (hg * _sigmoid(hg) * hu).astype(BF16)
        acc_ref[...] += _dot(a, wd_ref[...])

        @pl.when(j == nf - 1)
        def _():
            z = alpha * x_ref[...] + 0.5 * acc_ref[...]
            z_ref[...] = z
            y_ref[...] = _norm_fwd(z, g_ref[...], b_ref[...])

    blk = 2 * (3 * tm * D * 4 + 2 * tm * tf * 4 + 3 * D * tf * 2) + tm * D * 6 + 3 * tm * tf * 4
    return _streamed_call(
        body, name=name, grid=(S // tm, nf),
        in_specs=[_bs((tm, D), lambda i, j: (i, 0)), wg_spec, wu_spec, wd_spec,
                  _bs((1, D), lambda i, j: (0, 0)), _bs((1, D), lambda i, j: (0, 0))],
        out_specs=[_bs((tm, D), lambda i, j: (i, 0)), _bs((tm, D), lambda i, j: (i, 0)),
                   _bs((tm, tf), lambda i, j: (i, j)), _bs((tm, tf), lambda i, j: (i, j))],
        out_shape=[jax.ShapeDtypeStruct((S, D), F32), jax.ShapeDtypeStruct((S, D), F32),
                   jax.ShapeDtypeStruct((S, F), F32), jax.ShapeDtypeStruct((S, F), F32)],
        scratch_shapes=[pltpu.VMEM((tm, D), F32), pltpu.VMEM((tm, D), BF16)],
        compiler_params=_cparams(blk),
    )(x, wg_arr, wu_arr, wd_arr, g, b)


def _ffn_bwd(dy, z, hg, hu, wgu, wd, g, alpha, name):
    S, D = dy.shape
    F = wd[0].shape[1]
    tm, tf = _ffn_tiles(S, F)
    nf = F // tf
    wg_arr, wg_spec = _layer_block(wgu, (D, tf), lambda i, j: (0, j))
    wu_arr, wu_spec = _layer_block(wgu, (D, tf), lambda i, j: (0, nf + j))
    wd_arr, wd_spec = _layer_block(wd, (tf, D), lambda i, j: (j, 0))

    def body(dy_ref, z_ref, hg_ref, hu_ref, wg_ref, wu_ref, wd_ref, g_ref,
             dx_ref, df_ref, a_ref, dhg_ref, dhu_ref, dg_ref, db_ref, acc_ref):
        i = pl.program_id(0)
        j = pl.program_id(1)

        @pl.when((i == 0) & (j == 0))
        def _():
            dg_ref[...] = jnp.zeros_like(dg_ref)
            db_ref[...] = jnp.zeros_like(db_ref)

        @pl.when(j == 0)
        def _():
            dy_ = dy_ref[...]
            dz, xhat = _norm_bwd(z_ref[...], g_ref[...], dy_)
            dg_ref[...] += _colsum(dy_ * xhat)
            db_ref[...] += _colsum(dy_)
            acc_ref[...] = alpha * dz
            df_ref[...] = (0.5 * dz).astype(BF16)

        da = _dot_nt(df_ref[...], wd_ref[...])
        hg_ = hg_ref[...]
        hu_ = hu_ref[...]
        s = _sigmoid(hg_)
        sl = hg_ * s
        dgate = (da * hu_ * (s * (1.0 + hg_ * (1.0 - s)))).astype(BF16)
        dup = (da * sl).astype(BF16)
        a_ref[...] = (sl * hu_).astype(BF16)
        dhg_ref[...] = dgate
        dhu_ref[...] = dup
        acc_ref[...] += _dot_nt(dgate, wg_ref[...]) + _dot_nt(dup, wu_ref[...])

        @pl.when(j == nf - 1)
        def _():
            dx_ref[...] = acc_ref[...]

    blk = 2 * (2 * tm * D * 4 + tm * D * 2 + 2 * tm * tf * 4 + 3 * tm * tf * 2 + 3 * D * tf * 2) + 3 * tm * D * 4 + 8 * tm * tf * 4
    once = dict(pipeline_mode=pl.Buffered(1))
    return _streamed_call(
        body, name=name, grid=(S // tm, nf),
        in_specs=[_bs((tm, D), lambda i, j: (i, 0), **once), _bs((tm, D), lambda i, j: (i, 0), **once),
                  _bs((tm, tf), lambda i, j: (i, j)), _bs((tm, tf), lambda i, j: (i, j)),
                  wg_spec, wu_spec, wd_spec, _bs((1, D), lambda i, j: (0, 0))],
        out_specs=[_bs((tm, D), lambda i, j: (i, 0)), _bs((tm, D), lambda i, j: (i, 0)),
                   _bs((tm, tf), lambda i, j: (i, j)), _bs((tm, tf), lambda i, j: (i, j)), _bs((tm, tf), lambda i, j: (i, j)),
                   _bs((1, D), lambda i, j: (0, 0)), _bs((1, D), lambda i, j: (0, 0))],
        out_shape=[jax.ShapeDtypeStruct((S, D), F32), jax.ShapeDtypeStruct((S, D), BF16),
                   jax.ShapeDtypeStruct((S, F), BF16), jax.ShapeDtypeStruct((S, F), BF16), jax.ShapeDtypeStruct((S, F), BF16),
                   jax.ShapeDtypeStruct((1, D), F32), jax.ShapeDtypeStruct((1, D), F32)],
        scratch_shapes=[pltpu.VMEM((tm, D), F32)],
        compiler_params=_cparams(blk),
    )(dy, z, hg, hu, wg_arr, wu_arr, wd_arr, g)


def _mm_tn(a, b, name):
    S, M = a.shape
    N = b.shape[1]
    bm = _divisor_tile(M, 1408, LANES)
    bn = _divisor_tile(N, 1408, LANES)
    tk = _divisor_tile(S, 512, 16)
    nk = S // tk

    def body(a_ref, b_ref, o_ref, ob_ref):
        k = pl.program_id(2)

        @pl.when(k == 0)
        def _():
            o_ref[...] = jnp.zeros_like(o_ref)

        o_ref[...] += _dot_tn(a_ref[...].astype(BF16), b_ref[...].astype(BF16))

        @pl.when(k == nk - 1)
        def _():
            ob_ref[...] = o_ref[...].astype(BF16)

    blk = 2 * (tk * bm * a.dtype.itemsize + tk * bn * b.dtype.itemsize + bm * bn * 6) + tk * bm * 4 + bm * bn * 4
    tile = _bs((bm, bn), lambda i, j, k: (i, j))
    return _streamed_call(
        body, name=name, grid=(M // bm, N // bn, nk),
        in_specs=[_bs((tk, bm), lambda i, j, k: (k, i)), _bs((tk, bn), lambda i, j, k: (k, j))],
        out_specs=[tile, tile],
        out_shape=[jax.ShapeDtypeStruct((M, N), F32), jax.ShapeDtypeStruct((M, N), BF16)],
        compiler_params=_cparams(blk),
    )(a, b)


def _mm_tn_pair(a, b0, b1, name):
    S, M = a.shape
    N = b0.shape[1]
    assert b1.shape == b0.shape
    bm = _divisor_tile(M, 1408, LANES)
    bn = _divisor_tile(N, 1408, LANES)
    tk = _divisor_tile(S, 512, 16)
    nb = N // bn
    nk = S // tk

    def body(a_ref, b0_ref, b1_ref, o_ref, ob_ref):
        j = pl.program_id(1)
        k = pl.program_id(2)

        @pl.when(k == 0)
        def _():
            o_ref[...] = jnp.zeros_like(o_ref)

        ab = a_ref[...].astype(BF16)

        @pl.when(j < nb)
        def _():
            o_ref[...] += _dot_tn(ab, b0_ref[...])

        @pl.when(j >= nb)
        def _():
            o_ref[...] += _dot_tn(ab, b1_ref[...])

        @pl.when(k == nk - 1)
        def _():
            ob_ref[...] = o_ref[...].astype(BF16)

    b0_map = lambda i, j, k: (jnp.where(j < nb, k, nk - 1), jnp.minimum(j, nb - 1))
    b1_map = lambda i, j, k: (jnp.where(j >= nb, k, 0), jnp.maximum(j - nb, 0))
    blk = 2 * (tk * bm * a.dtype.itemsize + 2 * tk * bn * 2 + bm * bn * 6) + tk * bm * 4 + bm * bn * 4
    tile = _bs((bm, bn), lambda i, j, k: (i, j))
    return _streamed_call(
        body, name=name, grid=(M // bm, 2 * nb, nk),
        in_specs=[_bs((tk, bm), lambda i, j, k: (k, i)), _bs((tk, bn), b0_map), _bs((tk, bn), b1_map)],
        out_specs=[tile, tile],
        out_shape=[jax.ShapeDtypeStruct((M, 2 * N), F32), jax.ShapeDtypeStruct((M, 2 * N), BF16)],
        compiler_params=_cparams(blk),
    )(a, b0, b1)


def _mix_in(x, wt, bias, sections, name):
    S, D = x.shape
    tm = _divisor_tile(S, 256, 16)
    n = len(sections)

    def body(x_ref, w_ref, b_ref, *o_refs):
        xb = x_ref[...].astype(BF16)
        for (off, width, dtype), o_ref in zip(sections, o_refs):
            o_ref[...] = (_dot_nt(xb, w_ref[off:off + width, :]) + b_ref[:, off:off + width]).astype(dtype)

    total = wt.shape[0]
    blk = 2 * (tm * D * 4 + sum(tm * w * jnp.dtype(dt).itemsize for _, w, dt in sections)) + total * D * 2 + 3 * tm * D * 4
    return _streamed_call(
        body, name=name, grid=(S // tm,),
        in_specs=[_bs((tm, D), lambda i: (i, 0)), _resident((total, D)), _resident((1, total))],
        out_specs=[_bs((tm, w), lambda i: (i, 0)) for _, w, _ in sections],
        out_shape=[jax.ShapeDtypeStruct((S, w), dt) for _, w, dt in sections],
        compiler_params=_cparams(blk),
    )(x, wt, bias)


def _mix_dx(dz, parts, wt, sections, alpha, name):
    S, D = dz.shape
    tm = _divisor_tile(S, 256, 16)
    n = len(parts)

    def body(*refs):
        dz_ref = refs[0]
        p_refs = refs[1:1 + n]
        w_ref = refs[1 + n]
        o_ref = refs[2 + n]
        acc = alpha * dz_ref[...]
        for p_ref, (off, width, _) in zip(p_refs, sections):
            acc = acc + _dot(p_ref[...], w_ref[off:off + width, :])
        o_ref[...] = acc

    widths = [p.shape[1] for p in parts]
    total = wt.shape[0]
    blk = 2 * (2 * tm * D * 4 + sum(tm * w * 2 for w in widths)) + total * D * 2 + 2 * tm * D * 4
    return _streamed_call(
        body, name=name, grid=(S // tm,),
        in_specs=[_bs((tm, D), lambda i: (i, 0))] + [_bs((tm, w), lambda i: (i, 0)) for w in widths]
                 + [_resident((total, D))],
        out_specs=_bs((tm, D), lambda i: (i, 0)),
        out_shape=jax.ShapeDtypeStruct((S, D), F32),
        compiler_params=_cparams(blk),
    )(dz, *parts, wt)


def _conv_branch_fwd(cv, cg, w, b, gg, gb, name):
    S, C = cv.shape
    K = w.shape[0]
    assert C // CONV_GROUPS == LANES
    padf = _front_pad(K)
    R = min(CONV_ROWS, S)
    E = min(EW_ROWS, S)

    def body(cv_ref, cg_ref, w_ref, b_ref, gg_ref, gb_ref, c2_ref, c4_ref, pad_ref):
        pad_ref[0:padf, :] = jnp.zeros((padf, LANES), F32)
        pad_ref[S + padf:S + padf + SUBLANES, :] = jnp.zeros((SUBLANES, LANES), F32)

        def fill(i, carry):
            r = pl.multiple_of(i * E, E)
            pad_ref[pl.ds(r + padf, E), :] = cv_ref[pl.ds(r, E), :] * _sigmoid(cg_ref[pl.ds(r, E), :])
            return carry

        lax.fori_loop(0, S // E, fill, 0)
        taps = [(padf - (K - 1) + k, functools.partial(lambda k: w_ref[k:k + 1, :], k)) for k in range(K)]

        def conv(i, carry):
            r = pl.multiple_of(i * R, R)
            c2_ref[pl.ds(r, R), :] = _shifted_taps(pad_ref, r, R, taps) + b_ref[...]
            return carry

        lax.fori_loop(0, S // R, conv, 0)

        def norm(i, carry):
            r = pl.multiple_of(i * E, E)
            c3 = _norm_fwd(c2_ref[pl.ds(r, E), :], gg_ref[...], gb_ref[...], _lane_mean)
            c4_ref[pl.ds(r, E), :] = (c3 * _sigmoid(c3)).astype(BF16)
            return carry

        lax.fori_loop(0, S // E, norm, 0)

    col = lambda i: (0, i)
    blk = 2 * (3 * S * LANES * 4 + S * LANES * 2) + (S + _pad_rows(K)) * LANES * 4
    return _streamed_call(
        body, name=name, grid=(C // LANES,),
        in_specs=[_bs((S, LANES), col), _bs((S, LANES), col), _bs((K, LANES), col),
                  _bs((1, LANES), col), _bs((1, LANES), col), _bs((1, LANES), col)],
        out_specs=[_bs((S, LANES), col), _bs((S, LANES), col)],
        out_shape=[jax.ShapeDtypeStruct((S, C), F32), jax.ShapeDtypeStruct((S, C), BF16)],
        scratch_shapes=[pltpu.VMEM((S + _pad_rows(K), LANES), F32)],
        compiler_params=_cparams(blk),
    )(cv, cg, w, b, gg, gb)


def _conv_branch_bwd(dc4, c2, cv, cg, w, gg, gb, name):
    S, C = cv.shape
    K = w.shape[0]
    padf = _front_pad(K)
    R = min(CONV_ROWS, S)
    E = min(EW_ROWS, S)

    def body(dc4_ref, c2_ref, cv_ref, cg_ref, w_ref, gg_ref, gb_ref,
             dcv_ref, dcg_ref, dw_ref, dwb_ref, dgg_ref, dgb_ref, scv_ref, scg_ref,
             dpad_ref, cpad_ref, dwacc_ref):
        cpad_ref[0:padf, :] = jnp.zeros((padf, LANES), F32)
        cpad_ref[S + padf:S + padf + SUBLANES, :] = jnp.zeros((SUBLANES, LANES), F32)
        dpad_ref[S:S + padf + SUBLANES, :] = jnp.zeros((padf + SUBLANES, LANES), F32)
        dwacc_ref[...] = jnp.zeros_like(dwacc_ref)
        for ref in (dwb_ref, dgg_ref, dgb_ref, scv_ref, scg_ref):
            ref[...] = jnp.zeros_like(ref)

        def norm_pass(i, carry):
            r = pl.multiple_of(i * E, E)
            g_ = gg_ref[...]
            c2 = c2_ref[pl.ds(r, E), :]
            xc = c2 - _lane_mean(c2)
            rstd = lax.rsqrt(_lane_mean(xc * xc) + LN_EPS)
            xhat = xc * rstd
            c3 = xhat * g_ + gb_ref[...]
            s = _sigmoid(c3)
            dc3 = dc4_ref[pl.ds(r, E), :].astype(F32) * (s * (1.0 + c3 * (1.0 - s)))
            dgg_ref[...] += _colsum(dc3 * xhat)
            dgb_ref[...] += _colsum(dc3)
            dxh = dc3 * g_
            dc2 = rstd * (dxh - _lane_mean(dxh) - xhat * _lane_mean(dxh * xhat))
            dpad_ref[pl.ds(r, E), :] = dc2
            dwb_ref[...] += _colsum(dc2)
            cpad_ref[pl.ds(r + padf, E), :] = cv_ref[pl.ds(r, E), :] * _sigmoid(cg_ref[pl.ds(r, E), :])
            return carry

        lax.fori_loop(0, S // E, norm_pass, 0)
        taps = [(K - 1 - k, functools.partial(lambda k: w_ref[k:k + 1, :], k)) for k in range(K)]
        offs = [padf - (K - 1) + k for k in range(K)]

        def conv_pass(i, carry):
            r = pl.multiple_of(i * R, R)
            dc1 = _shifted_taps(dpad_ref, r, R, taps)
            sg = _sigmoid(cg_ref[pl.ds(r, R), :])
            cv_ = cv_ref[pl.ds(r, R), :]
            dcv = dc1 * sg
            dcg = dc1 * cv_ * sg * (1.0 - sg)
            dcv_ref[pl.ds(r, R), :] = dcv.astype(BF16)
            dcg_ref[pl.ds(r, R), :] = dcg.astype(BF16)
            scv_ref[...] += _colsum(dcv)
            scg_ref[...] += _colsum(dcg)
            _shifted_corr(cpad_ref, r, R, dpad_ref[pl.ds(r, R), :], dwacc_ref, offs)
            return carry

        lax.fori_loop(0, S // R, conv_pass, 0)
        for k in range(K):
            dw_ref[k:k + 1, :] = _colsum(dwacc_ref[SUBLANES * k:SUBLANES * (k + 1), :])

    col = lambda i: (0, i)
    row = jax.ShapeDtypeStruct((1, C), F32)
    blk = 2 * (4 * S * LANES * 4 + 2 * S * LANES * 2) + 2 * (S + _pad_rows(K)) * LANES * 4
    return _streamed_call(
        body, name=name, grid=(C // LANES,),
        in_specs=[_bs((S, LANES), col)] * 4 + [_bs((K, LANES), col), _bs((1, LANES), col), _bs((1, LANES), col)],
        out_specs=[_bs((S, LANES), col), _bs((S, LANES), col), _bs((K, LANES), col)] + [_bs((1, LANES), col)] * 5,
        out_shape=[jax.ShapeDtypeStruct((S, C), BF16), jax.ShapeDtypeStruct((S, C), BF16),
                   jax.ShapeDtypeStruct((K, C), F32), row, row, row, row, row],
        scratch_shapes=[pltpu.VMEM((S + _pad_rows(K), LANES), F32), pltpu.VMEM((S + _pad_rows(K), LANES), F32),
                        pltpu.VMEM((SUBLANES * K, LANES), F32)],
        compiler_params=_cparams(blk),
    )(dc4, c2, cv, cg, w, gg, gb)


def _short_conv_fwd(xin, w, b, name):
    S, C = xin.shape
    K = w.shape[0]
    padf = _front_pad(K)
    R = min(CONV_ROWS, S)
    E = min(EW_ROWS, S)

    def body(x_ref, w_ref, b_ref, o_ref, pad_ref):
        pad_ref[0:padf, :] = jnp.zeros((padf, LANES), F32)
        pad_ref[S + padf:S + padf + SUBLANES, :] = jnp.zeros((SUBLANES, LANES), F32)

        def fill(i, carry):
            r = pl.multiple_of(i * E, E)
            pad_ref[pl.ds(r + padf, E), :] = x_ref[pl.ds(r, E), :]
            return carry

        lax.fori_loop(0, S // E, fill, 0)
        taps = [(padf - (K - 1) + k, functools.partial(lambda k: w_ref[k:k + 1, :], k)) for k in range(K)]

        def conv(i, carry):
            r = pl.multiple_of(i * R, R)
            o_ref[pl.ds(r, R), :] = _shifted_taps(pad_ref, r, R, taps) + b_ref[...]
            return carry

        lax.fori_loop(0, S // R, conv, 0)

    col = lambda i: (0, i)
    blk = 2 * (2 * S * LANES * 4) + (S + _pad_rows(K)) * LANES * 4
    return _streamed_call(
        body, name=name, grid=(C // LANES,),
        in_specs=[_bs((S, LANES), col), _bs((K, LANES), col), _bs((1, LANES), col)],
        out_specs=_bs((S, LANES), col),
        out_shape=jax.ShapeDtypeStruct((S, C), F32),
        scratch_shapes=[pltpu.VMEM((S + _pad_rows(K), LANES), F32)],
        compiler_params=_cparams(blk),
    )(xin, w, b)


def _short_conv_bwd(dy, xin, w, name):
    S, C = xin.shape
    K = w.shape[0]
    padf = _front_pad(K)
    R = min(CONV_ROWS, S)
    E = min(EW_ROWS, S)

    def body(dy_ref, x_ref, w_ref, dx_ref, dw_ref, db_ref, sx_ref, dpad_ref, xpad_ref, dwacc_ref):
        xpad_ref[0:padf, :] = jnp.zeros((padf, LANES), F32)
        xpad_ref[S + padf:S + padf + SUBLANES, :] = jnp.zeros((SUBLANES, LANES), F32)
        dpad_ref[S:S + padf + SUBLANES, :] = jnp.zeros((padf + SUBLANES, LANES), F32)
        dwacc_ref[...] = jnp.zeros_like(dwacc_ref)
        db_ref[...] = jnp.zeros_like(db_ref)
        sx_ref[...] = jnp.zeros_like(sx_ref)

        def fill(i, carry):
            r = pl.multiple_of(i * E, E)
            d = dy_ref[pl.ds(r, E), :]
            dpad_ref[pl.ds(r, E), :] = d
            db_ref[...] += _colsum(d)
            xpad_ref[pl.ds(r + padf, E), :] = x_ref[pl.ds(r, E), :]
            return carry

        lax.fori_loop(0, S // E, fill, 0)
        taps = [(K - 1 - k, functools.partial(lambda k: w_ref[k:k + 1, :], k)) for k in range(K)]
        offs = [padf - (K - 1) + k for k in range(K)]

        def conv_pass(i, carry):
            r = pl.multiple_of(i * R, R)
            dx = _shifted_taps(dpad_ref, r, R, taps)
            dx_ref[pl.ds(r, R), :] = dx.astype(BF16)
            sx_ref[...] += _colsum(dx)
            _shifted_corr(xpad_ref, r, R, dpad_ref[pl.ds(r, R), :], dwacc_ref, offs)
            return carry

        lax.fori_loop(0, S // R, conv_pass, 0)
        for k in range(K):
            dw_ref[k:k + 1, :] = _colsum(dwacc_ref[SUBLANES * k:SUBLANES * (k + 1), :])

    col = lambda i: (0, i)
    row = jax.ShapeDtypeStruct((1, C), F32)
    blk = 2 * (2 * S * LANES * 4 + S * LANES * 2) + 2 * (S + _pad_rows(K)) * LANES * 4
    return _streamed_call(
        body, name=name, grid=(C // LANES,),
        in_specs=[_bs((S, LANES), col), _bs((S, LANES), col), _bs((K, LANES), col)],
        out_specs=[_bs((S, LANES), col), _bs((K, LANES), col), _bs((1, LANES), col), _bs((1, LANES), col)],
        out_shape=[jax.ShapeDtypeStruct((S, C), BF16), jax.ShapeDtypeStruct((K, C), F32), row, row],
        scratch_shapes=[pltpu.VMEM((S + _pad_rows(K), LANES), F32), pltpu.VMEM((S + _pad_rows(K), LANES), F32),
                        pltpu.VMEM((SUBLANES * K, LANES), F32)],
        compiler_params=_cparams(blk),
    )(dy, xin, w)


def _band_panels(width, block):
    assert width % LANES == 0 and block <= LANES
    panels = []
    for c0 in range(0, width, 2 * LANES):
        c1 = min(width, c0 + 2 * LANES)
        r0 = (c0 // block) * block // LANES * LANES
        r1 = min(width, -(-(-(-c1 // block) * block) // LANES) * LANES)
        panels.append((r0, r1, c0, c1))
    return panels


def _gates_fwd(r1, wa, wx, ba, bx, lam, name):
    S, R = r1.shape
    tm = _divisor_tile(S, 256, 16)
    panels = _band_panels(R, R // RNN_BLOCKS)

    def body(r1_ref, wa_ref, wx_ref, ba_ref, bx_ref, lam_ref, ra_ref, ri_ref, a_ref, uu_ref):
        for r0, r1e, c0, c1 in panels:
            rb = r1_ref[:, r0:r1e].astype(BF16)
            ra = _sigmoid(_dot(rb, wa_ref[r0:r1e, c0:c1]) + ba_ref[:, c0:c1])
            ri = _sigmoid(_dot(rb, wx_ref[r0:r1e, c0:c1]) + bx_ref[:, c0:c1])
            log_a = -RG_LRU_C * ra * _softplus(-lam_ref[:, c0:c1])
            ra_ref[:, c0:c1] = ra
            ri_ref[:, c0:c1] = ri
            a_ref[:, c0:c1] = jnp.exp(log_a)
            uu_ref[:, c0:c1] = jnp.sqrt(_neg_expm1(2.0 * log_a)) * (ri * r1_ref[:, c0:c1])

    blk = 2 * (5 * tm * R * 4) + 2 * R * R * 2 + 6 * tm * R * 4
    tile = _bs((tm, R), lambda i: (i, 0))
    return _streamed_call(
        body, name=name, grid=(S // tm,),
        in_specs=[tile, _resident((R, R)), _resident((R, R)), _resident((1, R)), _resident((1, R)), _resident((1, R))],
        out_specs=[tile] * 4,
        out_shape=[jax.ShapeDtypeStruct((S, R), F32)] * 4,
        compiler_params=_cparams(blk),
    )(r1, wa, wx, ba, bx, lam)


def _gates_bwd(guu, da, ra, ri, r1, wa, wx, lam, name):
    S, R = r1.shape
    tm = _divisor_tile(S, 256, 16)
    nsteps = S // tm
    panels = _band_panels(R, R // RNN_BLOCKS)

    def body(g_ref, da_ref, ra_ref, ri_ref, r1_ref, wa_ref, wx_ref, lam_ref,
             dr1_ref, dpa_ref, dpx_ref, dba_ref, dbx_ref, dlam_ref):
        i = pl.program_id(0)

        @pl.when(i == 0)
        def _():
            dba_ref[...] = jnp.zeros_like(dba_ref)
            dbx_ref[...] = jnp.zeros_like(dbx_ref)
            dlam_ref[...] = jnp.zeros_like(dlam_ref)

        g = g_ref[...]
        ra = ra_ref[...]
        ri = ri_ref[...]
        r1_ = r1_ref[...]
        sp = _softplus(-lam_ref[...])
        log_a = -RG_LRU_C * ra * sp
        a = jnp.exp(log_a)
        mult = jnp.sqrt(_neg_expm1(2.0 * log_a))
        d_ri = g * mult * r1_
        dr1 = g * mult * ri
        dmult = g * ri * r1_
        dlog_a = da_ref[...] * a - dmult * (a * a) / mult
        dra = dlog_a * (-RG_LRU_C * sp)
        dlam_ref[...] += _colsum(dlog_a * (-RG_LRU_C * ra))
        dpa = dra * ra * (1.0 - ra)
        dpx = d_ri * ri * (1.0 - ri)
        dba_ref[...] += _colsum(dpa)
        dbx_ref[...] += _colsum(dpx)
        dpa_b = dpa.astype(BF16)
        dpx_b = dpx.astype(BF16)
        dpa_ref[...] = dpa_b
        dpx_ref[...] = dpx_b
        dr1_ref[...] = dr1
        for k0, k1, c0, c1 in panels:
            dr1_ref[:, c0:c1] += (_dot_nt(dpa_ref[:, k0:k1], wa_ref[c0:c1, k0:k1])
                                  + _dot_nt(dpx_ref[:, k0:k1], wx_ref[c0:c1, k0:k1]))

        @pl.when(i == nsteps - 1)
        def _():
            dlam_ref[...] = dlam_ref[...] * (-_sigmoid(-lam_ref[...]))

    blk = 2 * (6 * tm * R * 4 + 2 * tm * R * 2) + 2 * R * R * 2 + 10 * tm * R * 4
    tile = _bs((tm, R), lambda i: (i, 0))
    rowspec = _bs((1, R), lambda i: (0, 0))
    row = jax.ShapeDtypeStruct((1, R), F32)
    return _streamed_call(
        body, name=name, grid=(nsteps,),
        in_specs=[tile] * 5 + [_resident((R, R)), _resident((R, R)), _resident((1, R))],
        out_specs=[tile, tile, tile, rowspec, rowspec, rowspec],
        out_shape=[jax.ShapeDtypeStruct((S, R), F32), jax.ShapeDtypeStruct((S, R), BF16), jax.ShapeDtypeStruct((S, R), BF16),
                   row, row, row],
        compiler_params=_cparams(blk),
    )(guu, da, ra, ri, r1, wa, wx, lam)


def _embed_blocks(w, name):
    H, bk, _ = w.shape

    def body(w_ref, o_ref):
        o_ref[...] = jnp.zeros_like(o_ref)
        for h in range(H):
            o_ref[bk * h:bk * (h + 1), bk * h:bk * (h + 1)] = w_ref[h].astype(BF16)

    return pl.pallas_call(body, name=name, out_shape=jax.ShapeDtypeStruct((H * bk, H * bk), BF16),
                          compiler_params=_cparams(3 * H * bk * H * bk * 2))(w)


def _block_grads(r1, dpa, dpx, name):
    S, R = r1.shape
    bk = R // RNN_BLOCKS
    tk = _divisor_tile(S, 512, 16)
    nsteps = S // tk
    panels = _band_panels(R, bk)

    def body(r1_ref, dpa_ref, dpx_ref, ga_ref, gx_ref, acca_ref, accx_ref):
        k = pl.program_id(0)

        @pl.when(k == 0)
        def _():
            acca_ref[...] = jnp.zeros_like(acca_ref)
            accx_ref[...] = jnp.zeros_like(accx_ref)

        for k0, k1, c0, c1 in panels:
            rb = r1_ref[:, k0:k1].astype(BF16)
            acca_ref[k0:k1, c0:c1] += _dot_tn(rb, dpa_ref[:, c0:c1])
            accx_ref[k0:k1, c0:c1] += _dot_tn(rb, dpx_ref[:, c0:c1])

        @pl.when(k == nsteps - 1)
        def _():
            for h in range(RNN_BLOCKS):
                ga_ref[h] = acca_ref[bk * h:bk * (h + 1), bk * h:bk * (h + 1)]
                gx_ref[h] = accx_ref[bk * h:bk * (h + 1), bk * h:bk * (h + 1)]

    tile = lambda: _bs((tk, R), lambda k: (k, 0))
    out = _bs((RNN_BLOCKS, bk, bk), lambda k: (0, 0, 0))
    sds = jax.ShapeDtypeStruct((RNN_BLOCKS, bk, bk), F32)
    return _streamed_call(
        body, name=name, grid=(nsteps,),
        in_specs=[tile(), tile(), tile()], out_specs=[out, out], out_shape=[sds, sds],
        scratch_shapes=[pltpu.VMEM((R, R), F32), pltpu.VMEM((R, R), F32)],
        compiler_params=_cparams(2 * (tk * R * 8) + 2 * R * R * 4 + 4 * tk * R * 4),
    )(r1, dpa, dpx)


def _scan_geometry(S):
    nseg = SCAN_SEGMENTS if S % (SCAN_SEGMENTS * SUBLANES) == 0 else SUBLANES
    return nseg, S // nseg


def _steps(n, step, init):
    u = SCAN_UNROLL

    def trip(t, carry):
        for k in range(u):
            carry = step(t * u + k, carry)
        return carry

    carry = lax.fori_loop(0, n // u, trip, init)
    for j in range(n - n % u, n):
        carry = step(j, carry)
    return carry


def _scan_fwd(a, u, name):
    S, C = a.shape
    nseg, L = _scan_geometry(S)
    T = min(SUBLANES, L)

    def body(a3, u3, h3, ta_ref, tu_ref, e_ref, p_ref, init_ref):

        def to_steps(i, carry):
            j0 = pl.multiple_of(i * T, T)
            ta_ref[pl.ds(j0, T)] = jnp.swapaxes(a3[:, pl.ds(j0, T), :], 0, 1)
            tu_ref[pl.ds(j0, T)] = jnp.swapaxes(u3[:, pl.ds(j0, T), :], 0, 1)
            return carry

        lax.fori_loop(0, L // T, to_steps, 0)

        def run1(j, carry):
            hs, ps = carry
            aj = ta_ref[j]
            return aj * hs + tu_ref[j], aj * ps

        e_ref[...], p_ref[...] = _steps(L, run1, (jnp.zeros((nseg, LANES), F32), jnp.ones((nseg, LANES), F32)))
        init_ref[0:1, :] = jnp.zeros((1, LANES), F32)
        for s in range(1, nseg):
            init_ref[s:s + 1, :] = e_ref[s - 1:s, :] + p_ref[s - 1:s, :] * init_ref[s - 1:s, :]

        def run2(j, hs):
            hs = ta_ref[j] * hs + tu_ref[j]
            tu_ref[j] = hs
            return hs

        _steps(L, run2, init_ref[...])

        def from_steps(i, carry):
            j0 = pl.multiple_of(i * T, T)
            h3[:, pl.ds(j0, T), :] = jnp.swapaxes(tu_ref[pl.ds(j0, T)], 0, 1)
            return carry

        lax.fori_loop(0, L // T, from_steps, 0)

    seg_block = _bs((nseg, L, LANES), lambda i: (0, 0, i))
    blk = 2 * (3 * S * LANES * 4) + 2 * S * LANES * 4
    return _streamed_call(
        body, name=name, grid=(C // LANES,),
        in_specs=[seg_block, seg_block],
        out_specs=seg_block,
        out_shape=jax.ShapeDtypeStruct((nseg, L, C), F32),
        scratch_shapes=[pltpu.VMEM((L, nseg, LANES), F32)] * 2 + [pltpu.VMEM((nseg, LANES), F32)] * 3,
        compiler_params=_cparams(blk),
    )(a.reshape(nseg, L, C), u.reshape(nseg, L, C)).reshape(S, C)


def _scan_bwd(a, dh, h, name):
    S, C = a.shape
    nseg, L = _scan_geometry(S)
    T = min(SUBLANES, L)
    assert L >= 2

    def body(a3, d3, h3, g3, da3, ta_ref, td_ref, th_ref, e_ref, p_ref, init_ref):

        def to_steps(i, carry):
            j0 = pl.multiple_of(i * T, T)
            for src, dst in ((a3, ta_ref), (d3, td_ref), (h3, th_ref)):
                dst[pl.ds(j0, T)] = jnp.swapaxes(src[:, pl.ds(j0, T), :], 0, 1)
            return carry

        lax.fori_loop(0, L // T, to_steps, 0)
        seg = lax.broadcasted_iota(jnp.int32, (nseg, LANES), 0)
        b_last = jnp.where(seg == nseg - 1, 0.0, pltpu.roll(ta_ref[0], nseg - 1, axis=0))
        h_first = jnp.where(seg == 0, 0.0, pltpu.roll(th_ref[L - 1], 1, axis=0))

        def run1(jj, carry):
            gs, ps = carry
            j = L - 2 - jj
            bj = ta_ref[j + 1]
            return bj * gs + td_ref[j], bj * ps

        e_ref[...], p_ref[...] = _steps(L - 1, run1, (td_ref[L - 1], b_last))
        init_ref[nseg - 1:nseg, :] = jnp.zeros((1, LANES), F32)
        for s in range(nseg - 2, -1, -1):
            init_ref[s:s + 1, :] = e_ref[s + 1:s + 2, :] + p_ref[s + 1:s + 2, :] * init_ref[s + 1:s + 2, :]

        gs = b_last * init_ref[...] + td_ref[L - 1]
        td_ref[L - 1] = gs
        th_ref[L - 1] = gs * th_ref[L - 2]

        def run2(jj, gs):
            j = L - 2 - jj
            gs = ta_ref[j + 1] * gs + td_ref[j]
            td_ref[j] = gs
            th_ref[j] = gs * th_ref[j - 1]
            return gs

        gs = _steps(L - 2, run2, gs)
        gs = ta_ref[1] * gs + td_ref[0]
        td_ref[0] = gs
        th_ref[0] = gs * h_first

        def from_steps(i, carry):
            j0 = pl.multiple_of(i * T, T)
            g3[:, pl.ds(j0, T), :] = jnp.swapaxes(td_ref[pl.ds(j0, T)], 0, 1)
            da3[:, pl.ds(j0, T), :] = jnp.swapaxes(th_ref[pl.ds(j0, T)], 0, 1)
            return carry

        lax.fori_loop(0, L // T, from_steps, 0)

    seg_block = _bs((nseg, L, LANES), lambda i: (0, 0, i))
    blk = 2 * (5 * S * LANES * 4) + 3 * S * LANES * 4
    g, da = _streamed_call(
        body, name=name, grid=(C // LANES,),
        in_specs=[seg_block] * 3,
        out_specs=[seg_block] * 2,
        out_shape=[jax.ShapeDtypeStruct((nseg, L, C), F32)] * 2,
        scratch_shapes=[pltpu.VMEM((L, nseg, LANES), F32)] * 3 + [pltpu.VMEM((nseg, LANES), F32)] * 3,
        compiler_params=_cparams(blk),
    )(a.reshape(nseg, L, C), dh.reshape(nseg, L, C), h.reshape(nseg, L, C))
    return g.reshape(S, C), da.reshape(S, C)


def _mixer_out_fwd(c4, h, rg, gc, gr, x1, wc, wr, wo, g, b, alpha, name):
    S, D = x1.shape
    R = h.shape[1]
    tm = _divisor_tile(S, 256, 16)

    def body(c4_ref, h_ref, rg_ref, gc_ref, gr_ref, x_ref, wc_ref, wr_ref, wo_ref, g_ref, b_ref,
             yc_ref, yr_ref, z_ref, y_ref):
        yc = _dot(c4_ref[...], wc_ref[...])
        q = (h_ref[...] * _gelu(rg_ref[...].astype(F32))).astype(BF16)
        yr = _dot(q, wr_ref[...])
        yc_ref[...] = yc.astype(BF16)
        yr_ref[...] = yr.astype(BF16)
        m = (_sigmoid(gc_ref[...].astype(F32)) * yc + _sigmoid(gr_ref[...].astype(F32)) * yr).astype(BF16)
        z = alpha * x_ref[...] + _dot(m, wo_ref[...])
        z_ref[...] = z
        y_ref[...] = _norm_fwd(z, g_ref[...], b_ref[...])

    blk = 2 * (tm * D * 2 + 2 * tm * R * 4 + 7 * tm * D * 4) + (2 * D * D + R * D) * 2 + 6 * tm * D * 4
    td = _bs((tm, D), lambda i: (i, 0))
    tr = _bs((tm, R), lambda i: (i, 0))
    return _streamed_call(
        body, name=name, grid=(S // tm,),
        in_specs=[td, tr, tr, td, td, td, _layer_resident(wc)[1], _layer_resident(wr)[1], _layer_resident(wo)[1],
                  _resident((1, D)), _resident((1, D))],
        out_specs=[td] * 4,
        out_shape=[jax.ShapeDtypeStruct((S, D), BF16)] * 2 + [jax.ShapeDtypeStruct((S, D), F32)] * 2,
        compiler_params=_cparams(blk),
    )(c4, h, rg, gc, gr, x1, wc[0], wr[0], wo[0], g, b)


def _mixer_out_bwd(dy, z, g, wo, yc, yr, gc, gr, name):
    S, D = dy.shape
    tm = _divisor_tile(S, 256, 16)

    def body(dy_ref, z_ref, g_ref, wo_ref, yc_ref, yr_ref, gc_ref, gr_ref,
             dz_ref, dzb_ref, m_ref, dyc_ref, dyr_ref, dgc_ref, dgr_ref, sgc_ref, sgr_ref, dg_ref, db_ref):
        @pl.when(pl.program_id(0) == 0)
        def _():
            for ref in (sgc_ref, sgr_ref, dg_ref, db_ref):
                ref[...] = jnp.zeros_like(ref)

        dy_ = dy_ref[...]
        dz, xhat = _norm_bwd(z_ref[...], g_ref[...], dy_)
        dg_ref[...] += _colsum(dy_ * xhat)
        db_ref[...] += _colsum(dy_)
        dz_ref[...] = dz
        dzb = dz.astype(BF16)
        dzb_ref[...] = dzb
        dm = _dot_nt(dzb, wo_ref[...])
        yc = yc_ref[...].astype(F32)
        yr = yr_ref[...].astype(F32)
        sc = _sigmoid(gc_ref[...].astype(F32))
        sr = _sigmoid(gr_ref[...].astype(F32))
        m_ref[...] = (sc * yc + sr * yr).astype(BF16)
        dyc_ref[...] = (dm * sc).astype(BF16)
        dyr_ref[...] = (dm * sr).astype(BF16)
        dgc = dm * yc * sc * (1.0 - sc)
        dgr = dm * yr * sr * (1.0 - sr)
        dgc_ref[...] = dgc.astype(BF16)
        dgr_ref[...] = dgr.astype(BF16)
        sgc_ref[...] += _colsum(dgc)
        sgr_ref[...] += _colsum(dgr)

    blk = 2 * (7 * tm * D * 4 + 6 * tm * D * 2) + D * D * 2 + 8 * tm * D * 4
    td = _bs((tm, D), lambda i: (i, 0))
    rowspec = _bs((1, D), lambda i: (0, 0))
    row = jax.ShapeDtypeStruct((1, D), F32)
    bfd = jax.ShapeDtypeStruct((S, D), BF16)
    return _streamed_call(
        body, name=name, grid=(S // tm,),
        in_specs=[td, td, _resident((1, D)), _layer_resident(wo)[1], td, td, td, td],
        out_specs=[td] * 7 + [rowspec] * 4,
        out_shape=[jax.ShapeDtypeStruct((S, D), F32), bfd, bfd, bfd, bfd, bfd, bfd, row, row, row, row],
        compiler_params=_cparams(blk),
    )(dy, z, g, wo[0], yc, yr, gc, gr)


def _branch_bwd(dyc, dyr, wc, wr, h, rg, name):
    S, D = dyc.shape
    R = h.shape[1]
    tm = _divisor_tile(S, 256, 16)

    def body(dyc_ref, dyr_ref, wc_ref, wr_ref, h_ref, rg_ref, dc4_ref, dh_ref, drg_ref, q_ref, srg_ref):
        @pl.when(pl.program_id(0) == 0)
        def _():
            srg_ref[...] = jnp.zeros_like(srg_ref)

        dc4_ref[...] = _dot_nt(dyc_ref[...], wc_ref[...]).astype(BF16)
        dq = _dot_nt(dyr_ref[...], wr_ref[...])
        h_ = h_ref[...]
        rg_ = rg_ref[...].astype(F32)
        ge = _gelu(rg_)
        dh_ref[...] = dq * ge
        drg = dq * h_ * _gelu_grad(rg_)
        drg_ref[...] = drg.astype(BF16)
        srg_ref[...] += _colsum(drg)
        q_ref[...] = (h_ * ge).astype(BF16)

    blk = 2 * (2 * tm * D * 2 + tm * D * 4 + 3 * tm * R * 4 + 2 * tm * R * 2) + (D * D + R * D) * 2 + 6 * tm * R * 4
    td = _bs((tm, D), lambda i: (i, 0))
    tr = _bs((tm, R), lambda i: (i, 0))
    return _streamed_call(
        body, name=name, grid=(S // tm,),
        in_specs=[td, td, _layer_resident(wc)[1], _layer_resident(wr)[1], tr, tr],
        out_specs=[td, tr, tr, tr, _bs((1, R), lambda i: (0, 0))],
        out_shape=[jax.ShapeDtypeStruct((S, D), BF16), jax.ShapeDtypeStruct((S, R), F32), jax.ShapeDtypeStruct((S, R), BF16),
                   jax.ShapeDtypeStruct((S, R), BF16), jax.ShapeDtypeStruct((1, R), F32)],
        compiler_params=_cparams(blk),
    )(dyc, dyr, wc[0], wr[0], h, rg)


def _loss_head(y, target, name):
    S, D = y.shape
    tm = _divisor_tile(S, 512, 16)
    nsteps = S // tm

    def body(y_ref, t_ref, loss_ref, dy_ref, acc_ref):
        i = pl.program_id(0)

        @pl.when(i == 0)
        def _():
            acc_ref[...] = jnp.zeros_like(acc_ref)

        err = y_ref[...] - t_ref[...]
        dy_ref[...] = err * (1.0 / D)
        acc_ref[...] += _colsum(err * err)

        @pl.when(i == nsteps - 1)
        def _():
            loss_ref[...] = jnp.sum(acc_ref[...], axis=-1, keepdims=True) * (0.5 / D)

    td = _bs((tm, D), lambda i: (i, 0))
    return _streamed_call(
        body, name=name, grid=(nsteps,),
        in_specs=[td, td],
        out_specs=[_bs((1, 1), lambda i: (0, 0)), td],
        out_shape=[jax.ShapeDtypeStruct((1, 1), F32), jax.ShapeDtypeStruct((S, D), F32)],
        scratch_shapes=[pltpu.VMEM((1, D), F32)],
        compiler_params=_cparams(2 * 3 * tm * D * 4),
    )(y, target)


def _adamw_math(w, g, m, v):
    m = ADAM_B1 * m + (1.0 - ADAM_B1) * g
    v = ADAM_B2 * v + (1.0 - ADAM_B2) * (g * g)
    m_hat = m / (1.0 - ADAM_B1 ** ADAM_STEP)
    v_hat = v / (1.0 - ADAM_B2 ** ADAM_STEP)
    delta = -ADAM_LR * (m_hat / (jnp.sqrt(v_hat) + ADAM_EPS) + ADAM_WD * w)
    return delta, m, v


def _adamw_sharded(w, m, v, own, sib, rem, layer, filled, name):
    layers, r, c = w.shape
    r2 = r // 2
    tr = _divisor_tile(r2, max(16, (1 << 20) // (4 * c) // 16 * 16), 16)
    n_out = 4

    def body(w_ref, m_ref, v_ref, own_ref, sib_ref, rem_ref, *rest):
        g_ref, d_ref, nm_ref, nv_ref = rest[-n_out:]
        mine = pl.program_id(0) == lax.axis_index("c")
        g = jnp.where(mine, own_ref[...], sib_ref[...]).astype(F32)
        for j in range(N_CHIPS - 1):
            g = g + rem_ref[j].astype(F32)
        delta, nm, nv = _adamw_math(w_ref[...], g, m_ref[...], v_ref[...])
        g_ref[...] = g
        d_ref[...] = delta
        nm_ref[...] = nm
        nv_ref[...] = nv

    halves = lambda a: a.reshape(layers, 2, r2, c)
    tile = _bs((None, None, tr, c), lambda h, i: (layer, h, i, 0))
    flat = _bs((tr, c), lambda h, i: (i, 0))
    sds = jax.ShapeDtypeStruct((layers, 2, r2, c), F32)
    passed = [] if filled is None else [halves(a) for a in filled]
    outs = _streamed_call(
        body, name=name, grid=(2, r2 // tr),
        in_specs=[tile, tile, tile, flat, flat, _bs((N_CHIPS - 1, None, tr, c), lambda h, i: (0, h, i, 0))] + [ANY] * len(passed),
        out_specs=[tile] * n_out,
        out_shape=[sds] * n_out,
        input_output_aliases={6 + k: k for k in range(len(passed))},
        compiler_params=_cparams(2 * (7 * tr * c * 4 + (N_CHIPS + 1) * tr * c * 2) + 6 * tr * c * 4),
    )(halves(w), halves(m), halves(v), own, sib, rem, *passed)
    return [o.reshape(layers, r, c) for o in outs]


def _adamw_flat(w, m, v, g, name):
    rows = w.shape[0]
    tr = _divisor_tile(rows, 1024, SUBLANES)

    def body(w_ref, m_ref, v_ref, g_ref, d_ref, nm_ref, nv_ref):
        delta, nm, nv = _adamw_math(w_ref[...], g_ref[...], m_ref[...], v_ref[...])
        d_ref[...] = delta
        nm_ref[...] = nm
        nv_ref[...] = nv

    tile = _bs((tr, LANES), lambda i: (i, 0))
    sds = jax.ShapeDtypeStruct(w.shape, F32)
    return _streamed_call(
        body, name=name, grid=(rows // tr,),
        in_specs=[tile] * 4, out_specs=[tile] * 3, out_shape=[sds] * 3,
        compiler_params=_cparams(2 * 7 * tr * LANES * 4),
    )(w, m, v, g)


def _half_view(g, by_cols):
    rows, cols = g.shape
    if by_cols:
        return g.reshape(2, rows // 2, cols)
    return g.reshape(N_CHIPS, 2, rows // (2 * N_CHIPS), cols)


def _pair_sum_bf16(view, theirs, by_cols, name):
    rows, c = theirs.shape[-2:]
    tr = _divisor_tile(rows, max(16, (1 << 20) // (4 * c) // 16 * 16), 16)

    def body(g0_ref, g1_ref, t_ref, o_ref):
        mine = jnp.where(lax.axis_index("c") == 0, g0_ref[...], g1_ref[...])
        o_ref[...] = (mine + t_ref[...].astype(F32)).astype(BF16)

    if by_cols:
        grid = (rows // tr,)
        halves = [_bs((None, tr, c), functools.partial(lambda h, i: (h, i, 0), h)) for h in range(2)]
        tile = _bs((tr, c), lambda i: (i, 0))
    else:
        grid = (N_CHIPS, rows // tr)
        halves = [_bs((None, None, tr, c), functools.partial(lambda h, k, i: (k, h, i, 0), h)) for h in range(2)]
        tile = _bs((None, tr, c), lambda k, i: (k, i, 0))
    return _streamed_call(
        body, name=name, grid=grid,
        in_specs=halves + [tile], out_specs=tile, out_shape=jax.ShapeDtypeStruct(theirs.shape, BF16),
        compiler_params=_cparams(2 * 4 * tr * c * 4),
    )(view, view, theirs)


ANY = pl.BlockSpec(memory_space=pl.ANY)


def _mesh_position():
    return lax.axis_index("x"), lax.axis_index("y"), lax.axis_index("c")


def _other_chips():
    x, y, c = _mesh_position()
    chips = [(1 - x, y), (x, 1 - y), (1 - x, 1 - y)]
    return 2 * x + y, (x, y, 1 - c), chips, [2 * cx + cy for cx, cy in chips]


def _chip_slab(ref, k, width, by_cols):
    if by_cols:
        start = k * width if isinstance(k, int) else pl.multiple_of(k * width, LANES)
        return ref.at[:, pl.ds(start, width)]
    return ref.at[k]


HBM = pl.BlockSpec(memory_space=pltpu.HBM)
SEM = pl.BlockSpec(memory_space=pltpu.SEMAPHORE)
DATAFLOW = pltpu.SideEffectType.DATAFLOW_SIDE_EFFECTING
N_GATHER_COPIES = 4


def _land_shape(src, by_cols):
    return src.shape[:-1] + (N_CHIPS * src.shape[-1],) if by_cols else (N_CHIPS,) + src.shape


def _gather_copy(src_ref, land_ref, by_cols, send_sems, recv_sems, pos, j, slab, to):
    width = src_ref.shape[-1]
    return pltpu.make_async_remote_copy(src_ref=src_ref, dst_ref=_chip_slab(land_ref, slab, width, by_cols),
                                        send_sem=send_sems.at[N_GATHER_COPIES * pos + j],
                                        recv_sem=recv_sems.at[N_GATHER_COPIES * pos + j],
                                        device_id=to, device_id_type=MESH)


def _gather_start(srcs, by_cols, groups, name):
    U = len(srcs)
    G = len(groups)
    lands = [lax.empty(_land_shape(s, bc), s.dtype) for s, bc in zip(srcs, by_cols)]

    def body(*refs):
        src = refs[:U]
        land = refs[U:2 * U]
        send_sems = refs[2 * U:2 * U + G]
        recv_sems = refs[2 * U + G:2 * U + 2 * G]
        token = refs[-1]
        c = lax.axis_index("c")
        me, sibling, chips, _ = _other_chips()
        targets = [(*chip, c) for chip in chips] + [sibling]
        for g, members in enumerate(groups):
            for pos, u in enumerate(members):
                for j, to in enumerate(targets):
                    _gather_copy(src[u], land[u], by_cols[u], send_sems[g], recv_sems[g], pos, j, me, to).start()
        token[...] = jnp.zeros_like(token)

    sem_shapes = [pltpu.SemaphoreType.DMA((len(m) * N_GATHER_COPIES,)) for m in groups]
    outs = pl.pallas_call(
        body, name=name,
        out_shape=tuple(sem_shapes + sem_shapes + [pltpu.HBM(s.shape, s.dtype) for s in srcs]
                        + [pltpu.HBM(v.shape, v.dtype) for v in lands] + [jax.ShapeDtypeStruct((SUBLANES, LANES), F32)]),
        in_specs=[HBM] * (2 * U),
        out_specs=tuple([SEM] * (2 * G) + [HBM] * (2 * U) + [pl.BlockSpec(memory_space=pltpu.VMEM)]),
        input_output_aliases={i: 2 * G + i for i in range(2 * U)},
        compiler_params=pltpu.CompilerParams(has_side_effects=DATAFLOW),
    )(*[pltpu.with_memory_space_constraint(a, pltpu.HBM) for a in list(srcs) + lands])
    return outs[:G], outs[G:2 * G], outs[2 * G:2 * G + U], outs[2 * G + U:2 * G + 2 * U]


def _gather_wait(srcs, lands, by_cols, send_sems, recv_sems, after, name):
    n = len(srcs)

    def body(*refs):
        src = refs[:n]
        land = refs[n:2 * n]
        send_ref, recv_ref = refs[2 * n:2 * n + 2]
        _, sibling, _, _ = _other_chips()
        for pos in range(n):
            for j in range(N_GATHER_COPIES):
                cp = _gather_copy(src[pos], land[pos], by_cols[pos], send_ref, recv_ref, pos, j, 0, sibling)
                cp.wait_send()
                cp.wait_recv()

    outs = pl.pallas_call(
        body, name=name,
        out_shape=tuple([pltpu.HBM(s.shape, s.dtype) for s in srcs] + [pltpu.HBM(v.shape, v.dtype) for v in lands]),
        in_specs=[HBM] * (2 * n) + [SEM, SEM, pl.BlockSpec(memory_space=pl.ANY)],
        out_specs=tuple([HBM] * (2 * n)),
        input_output_aliases={i: i for i in range(2 * n)},
        compiler_params=pltpu.CompilerParams(has_side_effects=DATAFLOW),
    )(*srcs, *lands, send_sems, recv_sems, after)
    return outs[n:]


def _scatter_grads(csums, by_cols, name):
    n = len(csums)
    shard = [(s.shape[0], s.shape[1] // N_CHIPS) if bc else s.shape[1:] for s, bc in zip(csums, by_cols)]

    def body(*refs):
        src = refs[:n]
        rem = refs[n:2 * n]
        sib = refs[2 * n:3 * n]
        send_sems, recv_sems = refs[3 * n:]
        c = lax.axis_index("c")
        me, sibling, chips, chip_ids = _other_chips()

        def remote(i, k, src_ref, dst_ref, to):
            return pltpu.make_async_remote_copy(src_ref=src_ref, dst_ref=dst_ref, send_sem=send_sems.at[i, k],
                                                recv_sem=recv_sems.at[i, k], device_id=to, device_id_type=MESH)

        def part(i, k):
            return _chip_slab(src[i], k, shard[i][-1], by_cols[i])

        started = []
        for i in range(n):
            for j in range(3):
                started.append(remote(i, j, part(i, chip_ids[j]), rem[i].at[j, c], (*chips[j], c)))
            started.append(remote(i, 6, part(i, me), sib[i], sibling))
        for cp in started:
            cp.start()
        for i in range(n):
            for j in range(3):
                slot = rem[i].at[j, c]
                remote(i, j, slot, slot, sibling).wait_recv()
                fwd = remote(i, 3 + j, slot, slot, sibling)
                fwd.start()
                started.append(fwd)
        for i in range(n):
            for j in range(3):
                slot = rem[i].at[j, 1 - c]
                remote(i, 3 + j, slot, slot, sibling).wait_recv()
            remote(i, 6, sib[i], sib[i], sibling).wait_recv()
        for cp in started:
            cp.wait_send()

    out_shape = ([jax.ShapeDtypeStruct((N_CHIPS - 1, 2) + tuple(sh), s.dtype) for s, sh in zip(csums, shard)]
                 + [jax.ShapeDtypeStruct(tuple(sh), s.dtype) for s, sh in zip(csums, shard)])
    outs = _streamed_call(
        body, name=name,
        in_specs=[ANY] * n, out_specs=[ANY] * (2 * n), out_shape=out_shape,
        scratch_shapes=[pltpu.SemaphoreType.DMA((n, 7)), pltpu.SemaphoreType.DMA((n, 7))],
    )(*csums)
    return outs[:n], outs[n:]


def _sibling_exchange(views, by_cols, name):
    n = len(views)

    def body(*refs):
        src = refs[:n]
        theirs = refs[n:2 * n]
        send_sems, recv_sems = refs[2 * n:]
        x, y, c = _mesh_position()
        copies = []
        for i in range(n):
            half = src[i].at[1 - c] if by_cols[i] else src[i].at[:, 1 - c]
            copies.append(pltpu.make_async_remote_copy(src_ref=half, dst_ref=theirs[i], send_sem=send_sems.at[i],
                                                       recv_sem=recv_sems.at[i], device_id=(x, y, 1 - c), device_id_type=MESH))
        for cp in copies:
            cp.start()
        for cp in copies:
            cp.wait()

    out_shape = [jax.ShapeDtypeStruct(v.shape[1:] if bc else v.shape[:1] + v.shape[2:], v.dtype) for v, bc in zip(views, by_cols)]
    return _streamed_call(
        body, name=name,
        in_specs=[ANY] * n, out_specs=[ANY] * n, out_shape=out_shape,
        scratch_shapes=[pltpu.SemaphoreType.DMA((n,)), pltpu.SemaphoreType.DMA((n,))],
    )(*views)


N_SCATTER_COPIES = 7


def _scatter_start(csums, by_cols, name):
    n = len(csums)
    shard = [(s.shape[0], s.shape[1] // N_CHIPS) if bc else s.shape[1:] for s, bc in zip(csums, by_cols)]
    rems = [lax.empty((N_CHIPS - 1, 2) + tuple(sh), s.dtype) for s, sh in zip(csums, shard)]
    sibs = [lax.empty(tuple(sh), s.dtype) for s, sh in zip(csums, shard)]

    def body(*refs):
        src = refs[:n]
        rem = refs[n:2 * n]
        sib = refs[2 * n:3 * n]
        send_sems, recv_sems = refs[3 * n:3 * n + 2]
        token = refs[-1]
        c = lax.axis_index("c")
        me, sibling, chips, chip_ids = _other_chips()
        for i in range(n):
            base = N_SCATTER_COPIES * i
            for j in range(3):
                part = _chip_slab(src[i], chip_ids[j], shard[i][-1], by_cols[i])
                for core in range(2):
                    pltpu.make_async_remote_copy(src_ref=part, dst_ref=rem[i].at[j, c], send_sem=send_sems.at[base + 2 * j + core],
                                                 recv_sem=recv_sems.at[base + 2 * j + c], device_id=(*chips[j], core),
                                                 device_id_type=MESH).start()
            pltpu.make_async_remote_copy(src_ref=_chip_slab(src[i], me, shard[i][-1], by_cols[i]), dst_ref=sib[i],
                                         send_sem=send_sems.at[base + 6], recv_sem=recv_sems.at[base + 6], device_id=sibling,
                                         device_id_type=MESH).start()
        token[...] = jnp.zeros_like(token)

    sems = pltpu.SemaphoreType.DMA((N_SCATTER_COPIES * n,))
    operands = list(csums) + rems + sibs
    outs = pl.pallas_call(
        body, name=name,
        out_shape=tuple([sems, sems] + [pltpu.HBM(a.shape, a.dtype) for a in operands] + [jax.ShapeDtypeStruct((SUBLANES, LANES), F32)]),
        in_specs=[HBM] * (3 * n),
        out_specs=tuple([SEM, SEM] + [HBM] * (3 * n) + [pl.BlockSpec(memory_space=pltpu.VMEM)]),
        input_output_aliases={i: 2 + i for i in range(3 * n)},
        compiler_params=pltpu.CompilerParams(has_side_effects=DATAFLOW),
    )(*[pltpu.with_memory_space_constraint(a, pltpu.HBM) for a in operands])
    return (outs[0], outs[1], outs[2:2 + n], outs[2 + n:2 + 2 * n], outs[2 + 2 * n:2 + 3 * n]), outs[-1]


def _scatter_wait(send_sems, recv_sems, srcs, rems, sibs, by_cols, after, name):
    n = len(srcs)

    def body(*refs):
        src = refs[:n]
        rem = refs[n:2 * n]
        sib = refs[2 * n:3 * n]
        send_ref, recv_ref = refs[3 * n:3 * n + 2]
        _, sibling, _, _ = _other_chips()
        for i in range(n):
            base = N_SCATTER_COPIES * i
            width = sib[i].shape[-1]
            for j in range(3):
                for core in range(2):
                    cp = pltpu.make_async_remote_copy(src_ref=_chip_slab(src[i], 0, width, by_cols[i]), dst_ref=rem[i].at[j, core],
                                                      send_sem=send_ref.at[base + 2 * j + core],
                                                      recv_sem=recv_ref.at[base + 2 * j + core], device_id=sibling,
                                                      device_id_type=MESH)
                    cp.wait_send()
                    cp.wait_recv()
            cp = pltpu.make_async_remote_copy(src_ref=_chip_slab(src[i], 0, width, by_cols[i]), dst_ref=sib[i],
                                              send_sem=send_ref.at[base + 6], recv_sem=recv_ref.at[base + 6], device_id=sibling,
                                              device_id_type=MESH)
            cp.wait_send()
            cp.wait_recv()

    operands = list(srcs) + list(rems) + list(sibs)
    outs = pl.pallas_call(
        body, name=name,
        out_shape=tuple(pltpu.HBM(a.shape, a.dtype) for a in operands),
        in_specs=[HBM] * (3 * n) + [SEM, SEM, pl.BlockSpec(memory_space=pl.ANY)],
        out_specs=tuple([HBM] * (3 * n)),
        input_output_aliases={i: i for i in range(3 * n)},
        compiler_params=pltpu.CompilerParams(has_side_effects=DATAFLOW),
    )(*operands, send_sems, recv_sems, after)
    return outs[:n], outs[n:2 * n], outs[2 * n:3 * n]


def _all_reduce_small(v, name):
    _, rows, _ = v.shape

    def body(v_ref, o_ref, recv_ref, send_sems, recv_sems):
        x, y, c = _mesh_position()
        me = 4 * x + 2 * y + c
        peers = []
        for d in range(1, N_DEV):
            px, py, pc = x ^ ((d >> 2) & 1), y ^ ((d >> 1) & 1), c ^ (d & 1)
            peers.append(((px, py, pc), 4 * px + 2 * py + pc))

        def remote(k, src_ref, dst_ref, to):
            return pltpu.make_async_remote_copy(src_ref=src_ref, dst_ref=dst_ref, send_sem=send_sems.at[k],
                                                recv_sem=recv_sems.at[k], device_id=to, device_id_type=MESH)

        scatter = [remote(d, v_ref.at[pid], recv_ref.at[me], to) for d, (to, pid) in enumerate(peers)]
        for cp in scatter:
            cp.start()
        recv_ref[pl.ds(me, 1)] = v_ref[pl.ds(me, 1)]
        for d, (to, pid) in enumerate(peers):
            remote(d, v_ref.at[pid], recv_ref.at[pid], to).wait_recv()
        total = recv_ref[0]
        for s in range(1, N_DEV):
            total = total + recv_ref[s]
        o_ref[pl.ds(me, 1)] = total[None]
        gather = [remote(N_DEV - 1 + d, o_ref.at[me], o_ref.at[me], to) for d, (to, pid) in enumerate(peers)]
        for cp in gather:
            cp.start()
        for d, (to, pid) in enumerate(peers):
            remote(N_DEV - 1 + d, o_ref.at[pid], o_ref.at[pid], to).wait_recv()
        for cp in scatter + gather:
            cp.wait_send()

    vm = pl.BlockSpec(memory_space=pltpu.VMEM)
    return pl.pallas_call(
        body, name=name,
        in_specs=[vm], out_specs=vm, out_shape=jax.ShapeDtypeStruct(v.shape, F32),
        scratch_shapes=[pltpu.VMEM(v.shape, F32), pltpu.SemaphoreType.DMA((2 * (N_DEV - 1),)),
                        pltpu.SemaphoreType.DMA((2 * (N_DEV - 1),))],
        compiler_params=_cparams(4 * _nbytes(v.shape, F32)),
    )(v)


SHARDED_MATS = ("ffn1_w_gu", "ffn1_w_down", "mix_w_in", "conv_w_proj", "rnn_w_proj", "mix_w_out", "ffn2_w_gu", "ffn2_w_down")
COL_SHARDED = ("ffn1_w_gu", "ffn2_w_gu", "conv_dw_w")
SHARDED_VECS = ("conv_dw_w", "rnn_conv_w")
WEIGHT_NAMES = ("ffn1_w_gu", "ffn1_w_down", "ln1_g", "ln1_b", "mix_w_in", "mix_b_in", "conv_dw_w", "conv_dw_b", "conv_gn_g",
                "conv_gn_b", "conv_w_proj", "rnn_conv_w", "rnn_conv_b", "rnn_w_a", "rnn_b_a", "rnn_w_x", "rnn_b_x",
                "rnn_lambda", "rnn_w_proj", "mix_w_out", "ln2_g", "ln2_b", "ffn2_w_gu", "ffn2_w_down", "ln3_g", "ln3_b")
SMALL_NAMES = tuple(n for n in WEIGHT_NAMES if n not in SHARDED_MATS)
SECTION_NAMES = ("cv", "cg", "rx", "rg", "gc", "gr")


def _unshard_cols(gathered):
    k4, K, n = gathered.shape
    return jnp.transpose(gathered, (1, 0, 2)).reshape(K, k4 * n)


def _row(v):
    return v.reshape(1, -1)


def _layer_forward(x0, p, alpha, l, hooks):
    t = f"l{l}_"
    sv = {"x0": x0}
    x1, sv["z1"], sv["hg1"], sv["hu1"] = _ffn_fwd(x0, p["wgu1"], p["wd1"], p["ln1_g"], p["ln1_b"], alpha, t + "ffn1_fwd")
    sv["x1"] = x1
    hooks.get("after_ffn1", lambda v: None)(x1)
    sec = dict(zip(SECTION_NAMES, _mix_in(x1, p["win"], p["bin"], p["sections"], t + "mix_in")))
    sv.update(sec)
    sv["c2"], c4 = _conv_branch_fwd(sec["cv"], sec["cg"], p["conv_dw_w"], p["conv_dw_b"], p["conv_gn_g"], p["conv_gn_b"], t + "conv_fwd")
    sv["c4"] = c4
    r1 = _short_conv_fwd(sec["rx"], p["rnn_conv_w"], p["rnn_conv_b"], t + "rconv_fwd")
    sv["r1"] = r1
    sv["ra"], sv["ri"], a, uu = _gates_fwd(r1, p["wa"], p["wx"], p["rnn_b_a"], p["rnn_b_x"], p["rnn_lambda"], t + "gates_fwd")
    sv["a"] = a
    h = _scan_fwd(a, uu, t + "scan_fwd")
    sv["h"] = h
    hooks.get("after_scan", lambda v: None)(h)
    sv["yc"], sv["yr"], sv["z2"], x2 = _mixer_out_fwd(c4, h, sec["rg"], sec["gc"], sec["gr"], x1, p["wc"], p["wr"], p["wo"],
                                                      p["ln2_g"], p["ln2_b"], alpha, t + "mixout_fwd")
    sv["x2"] = x2
    hooks.get("after_mixer", lambda v: None)(x2)
    x3, sv["z3"], sv["hg2"], sv["hu2"] = _ffn_fwd(x2, p["wgu2"], p["wd2"], p["ln3_g"], p["ln3_b"], alpha, t + "ffn2_fwd")
    hooks.get("after_layer", lambda v: None)(x3)
    return x3, sv


def _layer_backward(dy, p, sv, alpha, l):
    t = f"l{l}_"
    g, gb = {}, {}
    dx2, df, a_act, dhg, dhu, g["ln3_g"], g["ln3_b"] = _ffn_bwd(dy, sv["z3"], sv["hg2"], sv["hu2"], p["wgu2"], p["wd2"],
                                                                 p["ln3_g"], alpha, t + "ffn2_bwd")
    g["ffn2_w_down"], gb["ffn2_w_down"] = _mm_tn(a_act, df, t + "dwd2")
    g["ffn2_w_gu"], gb["ffn2_w_gu"] = _mm_tn_pair(sv["x2"], dhg, dhu, t + "dwgu2")
    (dz2, dz2b, m_b, dyc, dyr, dgc, dgr, s_gc, s_gr, g["ln2_g"], g["ln2_b"]) = _mixer_out_bwd(
        dx2, sv["z2"], p["ln2_g"], p["wo"], sv["yc"], sv["yr"], sv["gc"], sv["gr"], t + "mixout_bwd")
    g["mix_w_out"], gb["mix_w_out"] = _mm_tn(m_b, dz2b, t + "dwo")
    dc4, dh, drg, q_b, s_rg = _branch_bwd(dyc, dyr, p["wc"], p["wr"], sv["h"], sv["rg"], t + "branch_bwd")
    g["conv_w_proj"], gb["conv_w_proj"] = _mm_tn(sv["c4"], dyc, t + "dwc")
    g["rnn_w_proj"], gb["rnn_w_proj"] = _mm_tn(q_b, dyr, t + "dwr")
    (dcv, dcg, g["conv_dw_w"], g["conv_dw_b"], g["conv_gn_g"], g["conv_gn_b"], s_cv, s_cg) = _conv_branch_bwd(
        dc4, sv["c2"], sv["cv"], sv["cg"], p["conv_dw_w"], p["conv_gn_g"], p["conv_gn_b"], t + "conv_bwd")
    guu, da = _scan_bwd(sv["a"], dh, sv["h"], t + "scan_bwd")
    dr1, dpa, dpx, g["rnn_b_a"], g["rnn_b_x"], g["rnn_lambda"] = _gates_bwd(
        guu, da, sv["ra"], sv["ri"], sv["r1"], p["wa"], p["wx"], p["rnn_lambda"], t + "gates_bwd")
    g["rnn_w_a"], g["rnn_w_x"] = _block_grads(sv["r1"], dpa, dpx, t + "dwax")
    drx, g["rnn_conv_w"], g["rnn_conv_b"], s_rx = _short_conv_bwd(dr1, sv["rx"], p["rnn_conv_w"], t + "rconv_bwd")
    du = {"cv": dcv, "cg": dcg, "rx": drx, "rg": drg, "gc": dgc, "gr": dgr}
    order = ("cv", "cg", "rx", "rg", "gc", "gr")
    pieces = [_mm_tn(du[s], sv["x1"], t + "dwin_" + s) for s in order]
    g["mix_w_in"] = jnp.concatenate([f for f, _ in pieces], axis=0)
    gb["mix_w_in"] = jnp.concatenate([h for _, h in pieces], axis=0)
    g["mix_b_in"] = jnp.concatenate([s_cv, s_cg, s_rx, s_rg, s_gc, s_gr], axis=1)
    dx1 = _mix_dx(dz2, [du[s] for s in order], p["win"], p["sections"], alpha, t + "mix_dx")
    dx0, df, a_act, dhg, dhu, g["ln1_g"], g["ln1_b"] = _ffn_bwd(dx1, sv["z1"], sv["hg1"], sv["hu1"], p["wgu1"], p["wd1"],
                                                                 p["ln1_g"], alpha, t + "ffn1_bwd")
    g["ffn1_w_down"], gb["ffn1_w_down"] = _mm_tn(a_act, df, t + "dwd1")
    g["ffn1_w_gu"], gb["ffn1_w_gu"] = _mm_tn_pair(sv["x0"], dhg, dhu, t + "dwgu1")
    return dx0, g, gb


def _pack_small(arrays, piece_rows):
    flat = jnp.concatenate([a.reshape(-1) for a in arrays])
    total = N_DEV * piece_rows * LANES
    return jnp.pad(flat, (0, total - flat.shape[0])).reshape(N_DEV, piece_rows, LANES)


def _unpack_small(packed, shapes):
    flat = packed.reshape(-1)
    out, off = [], 0
    for shp in shapes:
        n = 1
        for s in shp:
            n *= s
        out.append(flat[off:off + n].reshape(shp))
        off += n
    return out


def kernel(x, ffn1_w_gu, ffn1_w_down, ln1_g, ln1_b, mix_w_in, mix_b_in, conv_dw_w, conv_dw_b, conv_gn_g, conv_gn_b, conv_w_proj, rnn_conv_w, rnn_conv_b, rnn_w_a, rnn_b_a, rnn_w_x, rnn_b_x, rnn_lambda, rnn_w_proj, mix_w_out, ln2_g, ln2_b, ffn2_w_gu, ffn2_w_down, ln3_g, ln3_b, loss_target, m_ffn1_w_gu, m_ffn1_w_down, m_ln1_g, m_ln1_b, m_mix_w_in, m_mix_b_in, m_conv_dw_w, m_conv_dw_b, m_conv_gn_g, m_conv_gn_b, m_conv_w_proj, m_rnn_conv_w, m_rnn_conv_b, m_rnn_w_a, m_rnn_b_a, m_rnn_w_x, m_rnn_b_x, m_rnn_lambda, m_rnn_w_proj, m_mix_w_out, m_ln2_g, m_ln2_b, m_ffn2_w_gu, m_ffn2_w_down, m_ln3_g, m_ln3_b, v_ffn1_w_gu, v_ffn1_w_down, v_ln1_g, v_ln1_b, v_mix_w_in, v_mix_b_in, v_conv_dw_w, v_conv_dw_b, v_conv_gn_g, v_conv_gn_b, v_conv_w_proj, v_rnn_conv_w, v_rnn_conv_b, v_rnn_w_a, v_rnn_b_a, v_rnn_w_x, v_rnn_b_x, v_rnn_lambda, v_rnn_w_proj, v_mix_w_out, v_ln2_g, v_ln2_b, v_ffn2_w_gu, v_ffn2_w_down, v_ln3_g, v_ln3_b):
    args = locals()
    W = {n: args[n] for n in WEIGHT_NAMES}
    M = {n: args["m_" + n] for n in WEIGHT_NAMES}
    V = {n: args["v_" + n] for n in WEIGHT_NAMES}
    depth = ln1_g.shape[0]
    assert depth == 2, "each core of a chip moves one layer's weights and gradients"
    alpha = float((2 * depth) ** 0.25)
    S, D = x.shape[1], x.shape[2]
    F = ffn1_w_down.shape[1] * N_CHIPS
    R = rnn_w_proj.shape[1] * N_CHIPS
    chip = 2 * lax.axis_index("x") + lax.axis_index("y")

    for d in (W, M, V):
        d["mix_w_in"] = jnp.transpose(d["mix_w_in"], (0, 2, 1))

    names = SHARDED_MATS + SHARDED_VECS
    unit_groups = [[(0, "ffn1_w_gu"), (0, "ffn1_w_down")], [(0, "mix_w_in"), (0, "conv_dw_w"), (0, "rnn_conv_w")],
                   [(0, "conv_w_proj"), (0, "rnn_w_proj"), (0, "mix_w_out")], [(0, "ffn2_w_gu"), (0, "ffn2_w_down")],
                   [(1, n) for n in names]]
    order = [u for g in unit_groups for u in g]
    index = {u: i for i, u in enumerate(order)}
    groups = [[index[u] for u in g] for g in unit_groups]
    srcs = [W[n][l].astype(BF16) if n in SHARDED_MATS else W[n][l] for l, n in order]
    by_cols = [n in COL_SHARDED for _, n in order]
    send_sems, recv_sems, src_thru, land_thru = _gather_start(srcs, by_cols, groups, "gather_start")

    sections = ((0, D, F32), (D, D, F32), (2 * D, R, F32), (2 * D + R, R, BF16), (2 * D + 2 * R, D, BF16),
                (3 * D + 2 * R, D, BF16))
    keys = {"ffn1_w_gu": "wgu1", "ffn1_w_down": "wd1", "ffn2_w_gu": "wgu2", "ffn2_w_down": "wd2", "conv_w_proj": "wc",
            "rnn_w_proj": "wr", "mix_w_out": "wo"}
    params = []
    for l in range(depth):
        p = {"wa": _embed_blocks(rnn_w_a[l], f"l{l}_embed_wa"), "wx": _embed_blocks(rnn_w_x[l], f"l{l}_embed_wx")}
        for n in ("ln1_g", "ln1_b", "ln2_g", "ln2_b", "ln3_g", "ln3_b", "conv_dw_b", "conv_gn_g", "conv_gn_b", "rnn_conv_b",
                  "rnn_b_a", "rnn_b_x", "rnn_lambda"):
            p[n] = _row(W[n][l])
        p["bin"] = _row(mix_b_in[l])
        p["sections"] = sections
        params.append(p)

    def wait_group(g, after):
        ids = groups[g]
        landed = _gather_wait([src_thru[i] for i in ids], [land_thru[i] for i in ids], [by_cols[i] for i in ids],
                              send_sems[g], recv_sems[g], after, f"gather_wait{g}")
        for i, full in zip(ids, landed):
            l, n = order[i]
            p = params[l]
            if n not in COL_SHARDED:
                full = full.reshape((N_CHIPS * full.shape[1],) + full.shape[2:])
            if n == "mix_w_in":
                p["win"] = full
            elif n == "rnn_conv_w":
                p[n] = _unshard_cols(landed[ids.index(i)])
            elif n == "conv_dw_w":
                p[n] = full
            else:
                p[keys[n]] = (full[None], 0)

    h = x[0]
    wait_group(0, h)
    saved = []
    hooks = [{"after_ffn1": lambda v: wait_group(1, v), "after_scan": lambda v: wait_group(2, v),
              "after_mixer": lambda v: wait_group(3, v), "after_layer": lambda v: wait_group(4, v)}, {}]
    for l in range(depth):
        h, sv = _layer_forward(h, params[l], alpha, l, hooks[l])
        saved.append(sv)
    loss_part, dy = _loss_head(h, loss_target[0], "loss_head")
    loss = lax.psum(loss_part[0, 0], ("x", "y", "c"))
    mat_cols = [n in COL_SHARDED for n in SHARDED_MATS]

    def pair_sums(l):
        theirs = _sibling_exchange([_half_view(grads_bf16[l][n], bc) for n, bc in zip(SHARDED_MATS, mat_cols)], mat_cols,
                                   f"pair_exchange{l}")
        return [_pair_sum_bf16(_half_view(grads[l][n], bc), t, bc, f"pair_sum{l}_{n}")
                for n, bc, t in zip(SHARDED_MATS, mat_cols, theirs)]

    grads, grads_bf16 = [None] * depth, [None] * depth
    dy, grads[1], grads_bf16[1] = _layer_backward(dy, params[1], saved[1], alpha, 1)
    sums1 = pair_sums(1)
    in_flight, token = _scatter_start(sums1, mat_cols, "scatter_start1")
    first = dict(params[0], ln3_g=params[0]["ln3_g"] + token[0:1, 0:1])
    dy, grads[0], grads_bf16[0] = _layer_backward(dy, first, saved[0], alpha, 0)
    grad_x = dy[None]
    sums0 = pair_sums(0)
    rem0, sib0 = _scatter_grads(sums0, mat_cols, "scatter_grads0")
    sums1, rem1, sib1 = _scatter_wait(*in_flight, mat_cols, rem0[0], "scatter_wait1")

    results = {}
    for l, sums, rem, sib in ((1, sums1, rem1, sib1), (0, sums0, rem0, sib0)):
        for n, bc, cs, rm, sb in zip(SHARDED_MATS, mat_cols, sums, rem, sib):
            if bc:
                width = cs.shape[1] // N_CHIPS
                own = lax.dynamic_slice_in_dim(cs, chip * width, width, axis=1)
            else:
                own = lax.dynamic_index_in_dim(cs, chip, axis=0, keepdims=False)
            results[n] = _adamw_sharded(W[n], M[n], V[n], own, sb, rm, l, results.get(n), f"adamw{l}_{n}")
    out_g, out_d, out_m, out_v = {}, {}, {}, {}
    for n in SHARDED_MATS:
        outs = results[n]
        if n == "mix_w_in":
            outs = [jnp.transpose(o, (0, 2, 1)) for o in outs]
        out_g[n], out_d[n], out_m[n], out_v[n] = outs

    small_grads = [jnp.stack([grads[l][n].reshape(W[n].shape[1:] if n not in SHARDED_VECS else
                                                   (W[n].shape[1], W[n].shape[2] * N_CHIPS)) for l in range(depth)])
                   for n in SMALL_NAMES]
    n_small = sum(int(a.size) for a in small_grads)
    piece_rows = -(-n_small // (N_DEV * LANES * SUBLANES)) * SUBLANES
    reduced = _unpack_small(_all_reduce_small(_pack_small(small_grads, piece_rows), "all_reduce_small"),
                            [a.shape for a in small_grads])
    local_g = []
    for n, gr in zip(SMALL_NAMES, reduced):
        if n in SHARDED_VECS:
            width = W[n].shape[2]
            gr = lax.dynamic_slice_in_dim(gr, chip * width, width, axis=2)
        local_g.append(gr)
    n_local = sum(int(a.size) for a in local_g)
    flat_rows = -(-n_local // (N_DEV * LANES * SUBLANES)) * SUBLANES * N_DEV
    pack = lambda arrs: _pack_small(arrs, flat_rows // N_DEV).reshape(flat_rows, LANES)
    shapes = [a.shape for a in local_g]
    deltas, new_m, new_v = _adamw_flat(pack([W[n] for n in SMALL_NAMES]), pack([M[n] for n in SMALL_NAMES]),
                                       pack([V[n] for n in SMALL_NAMES]), pack(local_g), "adamw_small")
    for n, gr, d_, m_, v_ in zip(SMALL_NAMES, local_g, _unpack_small(deltas, shapes), _unpack_small(new_m, shapes),
                                 _unpack_small(new_v, shapes)):
        out_g[n], out_d[n], out_m[n], out_v[n] = gr, d_, m_, v_

    return (loss, grad_x, *[out_g[n] for n in WEIGHT_NAMES], *[out_d[n] for n in WEIGHT_NAMES],
            *[out_m[n] for n in WEIGHT_NAMES], *[out_v[n] for n in WEIGHT_NAMES])
```

```python
import functools

import jax
import jax.numpy as jnp
from jax import lax
from jax.experimental import pallas as pl
from jax.experimental.pallas import tpu as pltpu

F32 = jnp.float32
BF16 = jnp.bfloat16
MESH = pl.DeviceIdType.MESH

LN_EPS = 1e-5
CONV_GROUPS = 8
RNN_BLOCKS = 16
RG_LRU_C = 8.0
ADAM_LR = 0.001
ADAM_B1 = 0.9
ADAM_B2 = 0.999
ADAM_EPS = 1e-08
ADAM_WD = 0.01
ADAM_STEP = 10

LANES = 128
SUBLANES = 8
V7X_VMEM_BYTES = 64 << 20
VMEM_LIMIT_CAP = V7X_VMEM_BYTES - (6 << 20)
N_CHIPS = 4
N_DEV = 8
CONV_ROWS = 64
EW_ROWS = 1024
SCAN_SEGMENTS = 32
SCAN_UNROLL = 4


def _cparams(block_bytes):
    limit = min(VMEM_LIMIT_CAP, max(int(block_bytes) + (8 << 20), 24 << 20))
    return pltpu.CompilerParams(vmem_limit_bytes=limit)


def _nbytes(shape, dtype):
    n = 1
    for s in shape:
        n *= s
    return n * jnp.dtype(dtype).itemsize


def _divisor_tile(n, limit, quantum):
    if n <= limit:
        return n
    best = None
    for t in range(quantum, limit + 1, quantum):
        if n % t == 0:
            best = t
    assert best is not None, (n, limit, quantum)
    return best


def _bs(shape, imap, **kw):
    return pl.BlockSpec(shape, imap, **kw)


def _resident(shape):
    nd = len(shape)
    return pl.BlockSpec(shape, lambda *_: (0,) * nd, pipeline_mode=pl.Buffered(1))


def _streamed_call(body, **kw):
    call = pl.pallas_call(body, **kw)
    return lambda *operands: call(*[pltpu.with_memory_space_constraint(o, pltpu.HBM) for o in operands])


def _layer_block(w, block, imap, **kw):
    arr, layer = w
    return arr, pl.BlockSpec((None,) + block, lambda *ids: (layer,) + imap(*ids), **kw)


def _layer_resident(w):
    arr, _ = w
    return _layer_block(w, arr.shape[1:], lambda *_: (0, 0), pipeline_mode=pl.Buffered(1))


def _sigmoid(x):
    return jax.nn.sigmoid(x)


def _dot(a, b):
    return jnp.dot(a, b, preferred_element_type=F32)


def _dot_nt(a, b):
    return lax.dot_general(a, b, (((1,), (1,)), ((), ())), preferred_element_type=F32)


def _dot_tn(a, b):
    return lax.dot_general(a, b, (((0,), (0,)), ((), ())), preferred_element_type=F32)


def _row_mean(z):
    return jnp.mean(z, axis=-1, keepdims=True)


def _lane_mean(z):
    hi = z.astype(BF16)
    lo = (z - hi.astype(F32)).astype(BF16)
    ones = jnp.full((2 * LANES, LANES), 1.0 / LANES, BF16)
    return jnp.dot(jnp.concatenate([hi, lo], axis=-1), ones, preferred_element_type=F32)


def _norm_fwd(z, g, b, mean=_row_mean):
    mu = mean(z)
    xc = z - mu
    var = mean(xc * xc)
    return xc * lax.rsqrt(var + LN_EPS) * g + b


def _norm_bwd(z, g, dy, mean=_row_mean):
    mu = mean(z)
    xc = z - mu
    var = mean(xc * xc)
    rstd = lax.rsqrt(var + LN_EPS)
    xhat = xc * rstd
    dxh = dy * g
    m1 = mean(dxh)
    m2 = mean(dxh * xhat)
    return rstd * (dxh - m1 - xhat * m2), xhat


GELU_K = 0.7978845608028654
GELU_C = 0.044715


def _gelu(x):
    return 0.5 * x * (1.0 + jnp.tanh(GELU_K * (x + GELU_C * x * x * x)))


def _gelu_grad(x):
    t = jnp.tanh(GELU_K * (x + GELU_C * x * x * x))
    return 0.5 * (1.0 + t) + 0.5 * x * (1.0 - t * t) * GELU_K * (1.0 + 3.0 * GELU_C * x * x)


def _softplus(y):
    return jnp.maximum(y, 0.0) + jnp.log1p(jnp.exp(-jnp.abs(y)))


def _neg_expm1(y):
    series = -y * (1.0 + y * (0.5 + y * (1.0 / 6.0 + y * (1.0 / 24.0 + y * (1.0 / 120.0 + y * (1.0 / 720.0))))))
    return jnp.where(y > -0.25, series, 1.0 - jnp.exp(y))


def _colsum(x):
    return jnp.sum(x, axis=0, keepdims=True)


def _shifted_taps(src_ref, base, rows, taps):
    acc = None
    for o, coef in taps:
        term = coef() * src_ref[pl.ds(base + o, rows), :]
        acc = term if acc is None else acc + term
    return acc


def _shifted_corr(src_ref, base, rows, d, acc_ref, offs):
    for k, o in enumerate(offs):
        prod = d * src_ref[pl.ds(base + o, rows), :]
        part = jnp.sum(prod.reshape(rows // SUBLANES, SUBLANES, prod.shape[-1]), axis=0)
        acc_ref[SUBLANES * k:SUBLANES * (k + 1), :] += part


def _front_pad(ktaps):
    return SUBLANES * ((ktaps - 1 + SUBLANES - 1) // SUBLANES)


def _pad_rows(ktaps):
    return _front_pad(ktaps) + SUBLANES


def _ffn_tiles(S, F):
    tm = _divisor_tile(S, 1024, 16)
    tf = _divisor_tile(F, 256, LANES)
    return tm, tf


def _ffn_fwd(x, wgu, wd, g, b, alpha, name):
    S, D = x.shape
    F = wd[0].shape[1]
    tm, tf = _ffn_tiles(S, F)
    nf = F // tf
    wg_arr, wg_spec = _layer_block(wgu, (D, tf), lambda i, j: (0, j))
    wu_arr, wu_spec = _layer_block(wgu, (D, tf), lambda i, j: (0, nf + j))
    wd_arr, wd_spec = _layer_block(wd, (tf, D), lambda i, j: (j, 0))

    def body(x_ref, wg_ref, wu_ref, wd_ref, g_ref, b_ref, y_ref, z_ref, hg_ref, hu_ref, acc_ref, xb_ref):
        j = pl.program_id(1)

        @pl.when(j == 0)
        def _():
            xb_ref[...] = x_ref[...].astype(BF16)
            acc_ref[...] = jnp.zeros_like(acc_ref)

        xb = xb_ref[...]
        hg = _dot(xb, wg_ref[...])
        hu = _dot(xb, wu_ref[...])
        hg_ref[...] = hg
        hu_ref[...] = hu
        a = (hg * _sigmoid(hg) * hu).astype(BF16)
        acc_ref[...] += _dot(a, wd_ref[...])

        @pl.when(j == nf - 1)
        def _():
            z = alpha * x_ref[...] + 0.5 * acc_ref[...]
            z_ref[...] = z
            y_ref[...] = _norm_fwd(z, g_ref[...], b_ref[...])

    blk = 2 * (3 * tm * D * 4 + 2 * tm * tf * 4 + 3 * D * tf * 2) + tm * D * 6 + 3 * tm * tf * 4
    return _streamed_call(
        body, name=name, grid=(S // tm, nf),
        in_specs=[_bs((tm, D), lambda i, j: (i, 0)), wg_spec, wu_spec, wd_spec,
                  _bs((1, D), lambda i, j: (0, 0)), _bs((1, D), lambda i, j: (0, 0))],
        out_specs=[_bs((tm, D), lambda i, j: (i, 0)), _bs((tm, D), lambda i, j: (i, 0)),
                   _bs((tm, tf), lambda i, j: (i, j)), _bs((tm, tf), lambda i, j: (i, j))],
        out_shape=[jax.ShapeDtypeStruct((S, D), F32), jax.ShapeDtypeStruct((S, D), F32),
                   jax.ShapeDtypeStruct((S, F), F32), jax.ShapeDtypeStruct((S, F), F32)],
        scratch_shapes=[pltpu.VMEM((tm, D), F32), pltpu.VMEM((tm, D), BF16)],
        compiler_params=_cparams(blk),
    )(x, wg_arr, wu_arr, wd_arr, g, b)


def _ffn_bwd(dy, z, hg, hu, wgu, wd, g, alpha, name):
    S, D = dy.shape
    F = wd[0].shape[1]
    tm, tf = _ffn_tiles(S, F)
    nf = F // tf
    wg_arr, wg_spec = _layer_block(wgu, (D, tf), lambda i, j: (0, j))
    wu_arr, wu_spec = _layer_block(wgu, (D, tf), lambda i, j: (0, nf + j))
    wd_arr, wd_spec = _layer_block(wd, (tf, D), lambda i, j: (j, 0))

    def body(dy_ref, z_ref, hg_ref, hu_ref, wg_ref, wu_ref, wd_ref, g_ref,
             dx_ref, df_ref, a_ref, dhg_ref, dhu_ref, dg_ref, db_ref, acc_ref):
        i = pl.program_id(0)
        j = pl.program_id(1)

        @pl.when((i == 0) & (j == 0))
        def _():
            dg_ref[...] = jnp.zeros_like(dg_ref)
            db_ref[...] = jnp.zeros_like(db_ref)

        @pl.when(j == 0)
        def _():
            dy_ = dy_ref[...]
            dz, xhat = _norm_bwd(z_ref[...], g_ref[...], dy_)
            dg_ref[...] += _colsum(dy_ * xhat)
            db_ref[...] += _colsum(dy_)
            acc_ref[...] = alpha * dz
            df_ref[...] = (0.5 * dz).astype(BF16)

        da = _dot_nt(df_ref[...], wd_ref[...])
        hg_ = hg_ref[...]
        hu_ = hu_ref[...]
        s = _sigmoid(hg_)
        sl = hg_ * s
        dgate = (da * hu_ * (s * (1.0 + hg_ * (1.0 - s)))).astype(BF16)
        dup = (da * sl).astype(BF16)
        a_ref[...] = (sl * hu_).astype(BF16)
        dhg_ref[...] = dgate
        dhu_ref[...] = dup
        acc_ref[...] += _dot_nt(dgate, wg_ref[...]) + _dot_nt(dup, wu_ref[...])

        @pl.when(j == nf - 1)
        def _():
            dx_ref[...] = acc_ref[...]

    blk = 2 * (2 * tm * D * 4 + tm * D * 2 + 2 * tm * tf * 4 + 3 * tm * tf * 2 + 3 * D * tf * 2) + 3 * tm * D * 4 + 8 * tm * tf * 4
    once = dict(pipeline_mode=pl.Buffered(1))
    return _streamed_call(
        body, name=name, grid=(S // tm, nf),
        in_specs=[_bs((tm, D), lambda i, j: (i, 0), **once), _bs((tm, D), lambda i, j: (i, 0), **once),
                  _bs((tm, tf), lambda i, j: (i, j)), _bs((tm, tf), lambda i, j: (i, j)),
                  wg_spec, wu_spec, wd_spec, _bs((1, D), lambda i, j: (0, 0))],
        out_specs=[_bs((tm, D), lambda i, j: (i, 0)), _bs((tm, D), lambda i, j: (i, 0)),
                   _bs((tm, tf), lambda i, j: (i, j)), _bs((tm, tf), lambda i, j: (i, j)), _bs((tm, tf), lambda i, j: (i, j)),
                   _bs((1, D), lambda i, j: (0, 0)), _bs((1, D), lambda i, j: (0, 0))],
        out_shape=[jax.ShapeDtypeStruct((S, D), F32), jax.ShapeDtypeStruct((S, D), BF16),
                   jax.ShapeDtypeStruct((S, F), BF16), jax.ShapeDtypeStruct((S, F), BF16), jax.ShapeDtypeStruct((S, F), BF16),
                   jax.ShapeDtypeStruct((1, D), F32), jax.ShapeDtypeStruct((1, D), F32)],
        scratch_shapes=[pltpu.VMEM((tm, D), F32)],
        compiler_params=_cparams(blk),
    )(dy, z, hg, hu, wg_arr, wu_arr, wd_arr, g)


def _mm_tn(a, b, name):
    S, M = a.shape
    N = b.shape[1]
    bm = _divisor_tile(M, 1408, LANES)
    bn = _divisor_tile(N, 1408, LANES)
    tk = _divisor_tile(S, 512, 16)
    nk = S // tk

    def body(a_ref, b_ref, o_ref, ob_ref):
        k = pl.program_id(2)

        @pl.when(k == 0)
        def _():
            o_ref[...] = jnp.zeros_like(o_ref)

        o_ref[...] += _dot_tn(a_ref[...].astype(BF16), b_ref[...].astype(BF16))

        @pl.when(k == nk - 1)
        def _():
            ob_ref[...] = o_ref[...].astype(BF16)

    blk = 2 * (tk * bm * a.dtype.itemsize + tk * bn * b.dtype.itemsize + bm * bn * 6) + tk * bm * 4 + bm * bn * 4
    tile = _bs((bm, bn), lambda i, j, k: (i, j))
    return _streamed_call(
        body, name=name, grid=(M // bm, N // bn, nk),
        in_specs=[_bs((tk, bm), lambda i, j, k: (k, i)), _bs((tk, bn), lambda i, j, k: (k, j))],
        out_specs=[tile, tile],
        out_shape=[jax.ShapeDtypeStruct((M, N), F32), jax.ShapeDtypeStruct((M, N), BF16)],
        compiler_params=_cparams(blk),
    )(a, b)


def _mm_tn_pair(a, b0, b1, name):
    S, M = a.shape
    N = b0.shape[1]
    assert b1.shape == b0.shape
    bm = _divisor_tile(M, 1408, LANES)
    bn = _divisor_tile(N, 1408, LANES)
    tk = _divisor_tile(S, 512, 16)
    nb = N // bn
    nk = S // tk

    def body(a_ref, b0_ref, b1_ref, o_ref, ob_ref):
        j = pl.program_id(1)
        k = pl.program_id(2)

        @pl.when(k == 0)
        def _():
            o_ref[...] = jnp.zeros_like(o_ref)

        ab = a_ref[...].astype(BF16)

        @pl.when(j < nb)
        def _():
            o_ref[...] += _dot_tn(ab, b0_ref[...])

        @pl.when(j >= nb)
        def _():
            o_ref[...] += _dot_tn(ab, b1_ref[...])

        @pl.when(k == nk - 1)
        def _():
            ob_ref[...] = o_ref[...].astype(BF16)

    b0_map = lambda i, j, k: (jnp.where(j < nb, k, nk - 1), jnp.minimum(j, nb - 1))
    b1_map = lambda i, j, k: (jnp.where(j >= nb, k, 0), jnp.maximum(j - nb, 0))
    blk = 2 * (tk * bm * a.dtype.itemsize + 2 * tk * bn * 2 + bm * bn * 6) + tk * bm * 4 + bm * bn * 4
    tile = _bs((bm, bn), lambda i, j, k: (i, j))
    return _streamed_call(
        body, name=name, grid=(M // bm, 2 * nb, nk),
        in_specs=[_bs((tk, bm), lambda i, j, k: (k, i)), _bs((tk, bn), b0_map), _bs((tk, bn), b1_map)],
        out_specs=[tile, tile],
        out_shape=[jax.ShapeDtypeStruct((M, 2 * N), F32), jax.ShapeDtypeStruct((M, 2 * N), BF16)],
        compiler_params=_cparams(blk),
    )(a, b0, b1)


def _mix_in(x, wt, bias, sections, name):
    S, D = x.shape
    tm = _divisor_tile(S, 256, 16)
    n = len(sections)

    def body(x_ref, w_ref, b_ref, *o_refs):
        xb = x_ref[...].astype(BF16)
        for (off, width, dtype), o_ref in zip(sections, o_refs):
            o_ref[...] = (_dot_nt(xb, w_ref[off:off + width, :]) + b_ref[:, off:off + width]).astype(dtype)

    total = wt.shape[0]
    blk = 2 * (tm * D * 4 + sum(tm * w * jnp.dtype(dt).itemsize for _, w, dt in sections)) + total * D * 2 + 3 * tm * D * 4
    return _streamed_call(
        body, name=name, grid=(S // tm,),
        in_specs=[_bs((tm, D), lambda i: (i, 0)), _resident((total, D)), _resident((1, total))],
        out_specs=[_bs((tm, w), lambda i: (i, 0)) for _, w, _ in sections],
        out_shape=[jax.ShapeDtypeStruct((S, w), dt) for _, w, dt in sections],
        compiler_params=_cparams(blk),
    )(x, wt, bias)


def _mix_dx(dz, parts, wt, sections, alpha, name):
    S, D = dz.shape
    tm = _divisor_tile(S, 256, 16)
    n = len(parts)

    def body(*refs):
        dz_ref = refs[0]
        p_refs = refs[1:1 + n]
        w_ref = refs[1 + n]
        o_ref = refs[2 + n]
        acc = alpha * dz_ref[...]
        for p_ref, (off, width, _) in zip(p_refs, sections):
            acc = acc + _dot(p_ref[...], w_ref[off:off + width, :])
        o_ref[...] = acc

    widths = [p.shape[1] for p in parts]
    total = wt.shape[0]
    blk = 2 * (2 * tm * D * 4 + sum(tm * w * 2 for w in widths)) + total * D * 2 + 2 * tm * D * 4
    return _streamed_call(
        body, name=name, grid=(S // tm,),
        in_specs=[_bs((tm, D), lambda i: (i, 0))] + [_bs((tm, w), lambda i: (i, 0)) for w in widths]
                 + [_resident((total, D))],
        out_specs=_bs((tm, D), lambda i: (i, 0)),
        out_shape=jax.ShapeDtypeStruct((S, D), F32),
        compiler_params=_cparams(blk),
    )(dz, *parts, wt)


def _conv_branch_fwd(cv, cg, w, b, gg, gb, name):
    S, C = cv.shape
    K = w.shape[0]
    assert C // CONV_GROUPS == LANES
    padf = _front_pad(K)
    R = min(CONV_ROWS, S)
    E = min(EW_ROWS, S)

    def body(cv_ref, cg_ref, w_ref, b_ref, gg_ref, gb_ref, c2_ref, c4_ref, pad_ref):
        pad_ref[0:padf, :] = jnp.zeros((padf, LANES), F32)
        pad_ref[S + padf:S + padf + SUBLANES, :] = jnp.zeros((SUBLANES, LANES), F32)

        def fill(i, carry):
            r = pl.multiple_of(i * E, E)
            pad_ref[pl.ds(r + padf, E), :] = cv_ref[pl.ds(r, E), :] * _sigmoid(cg_ref[pl.ds(r, E), :])
            return carry

        lax.fori_loop(0, S // E, fill, 0)
        taps = [(padf - (K - 1) + k, functools.partial(lambda k: w_ref[k:k + 1, :], k)) for k in range(K)]

        def conv(i, carry):
            r = pl.multiple_of(i * R, R)
            c2_ref[pl.ds(r, R), :] = _shifted_taps(pad_ref, r, R, taps) + b_ref[...]
            return carry

        lax.fori_loop(0, S // R, conv, 0)

        def norm(i, carry):
            r = pl.multiple_of(i * E, E)
            c3 = _norm_fwd(c2_ref[pl.ds(r, E), :], gg_ref[...], gb_ref[...], _lane_mean)
            c4_ref[pl.ds(r, E), :] = (c3 * _sigmoid(c3)).astype(BF16)
            return carry

        lax.fori_loop(0, S // E, norm, 0)

    col = lambda i: (0, i)
    blk = 2 * (3 * S * LANES * 4 + S * LANES * 2) + (S + _pad_rows(K)) * LANES * 4
    return _streamed_call(
        body, name=name, grid=(C // LANES,),
        in_specs=[_bs((S, LANES), col), _bs((S, LANES), col), _bs((K, LANES), col),
                  _bs((1, LANES), col), _bs((1, LANES), col), _bs((1, LANES), col)],
        out_specs=[_bs((S, LANES), col), _bs((S, LANES), col)],
        out_shape=[jax.ShapeDtypeStruct((S, C), F32), jax.ShapeDtypeStruct((S, C), BF16)],
        scratch_shapes=[pltpu.VMEM((S + _pad_rows(K), LANES), F32)],
        compiler_params=_cparams(blk),
    )(cv, cg, w, b, gg, gb)


def _conv_branch_bwd(dc4, c2, cv, cg, w, gg, gb, name):
    S, C = cv.shape
    K = w.shape[0]
    padf = _front_pad(K)
    R = min(CONV_ROWS, S)
    E = min(EW_ROWS, S)

    def body(dc4_ref, c2_ref, cv_ref, cg_ref, w_ref, gg_ref, gb_ref,
             dcv_ref, dcg_ref, dw_ref, dwb_ref, dgg_ref, dgb_ref, scv_ref, scg_ref,
             dpad_ref, cpad_ref, dwacc_ref):
        cpad_ref[0:padf, :] = jnp.zeros((padf, LANES), F32)
        cpad_ref[S + padf:S + padf + SUBLANES, :] = jnp.zeros((SUBLANES, LANES), F32)
        dpad_ref[S:S + padf + SUBLANES, :] = jnp.zeros((padf + SUBLANES, LANES), F32)
        dwacc_ref[...] = jnp.zeros_like(dwacc_ref)
        for ref in (dwb_ref, dgg_ref, dgb_ref, scv_ref, scg_ref):
            ref[...] = jnp.zeros_like(ref)

        def norm_pass(i, carry):
            r = pl.multiple_of(i * E, E)
            g_ = gg_ref[...]
            c2 = c2_ref[pl.ds(r, E), :]
            xc = c2 - _lane_mean(c2)
            rstd = lax.rsqrt(_lane_mean(xc * xc) + LN_EPS)
            xhat = xc * rstd
            c3 = xhat * g_ + gb_ref[...]
            s = _sigmoid(c3)
            dc3 = dc4_ref[pl.ds(r, E), :].astype(F32) * (s * (1.0 + c3 * (1.0 - s)))
            dgg_ref[...] += _colsum(dc3 * xhat)
            dgb_ref[...] += _colsum(dc3)
            dxh = dc3 * g_
            dc2 = rstd * (dxh - _lane_mean(dxh) - xhat * _lane_mean(dxh * xhat))
            dpad_ref[pl.ds(r, E), :] = dc2
            dwb_ref[...] += _colsum(dc2)
            cpad_ref[pl.ds(r + padf, E), :] = cv_ref[pl.ds(r, E), :] * _sigmoid(cg_ref[pl.ds(r, E), :])
            return carry

        lax.fori_loop(0, S // E, norm_pass, 0)
        taps = [(K - 1 - k, functools.partial(lambda k: w_ref[k:k + 1, :], k)) for k in range(K)]
        offs = [padf - (K - 1) + k for k in range(K)]

        def conv_pass(i, carry):
            r = pl.multiple_of(i * R, R)
            dc1 = _shifted_taps(dpad_ref, r, R, taps)
            sg = _sigmoid(cg_ref[pl.ds(r, R), :])
            cv_ = cv_ref[pl.ds(r, R), :]
            dcv = dc1 * sg
            dcg = dc1 * cv_ * sg * (1.0 - sg)
            dcv_ref[pl.ds(r, R), :] = dcv.astype(BF16)
            dcg_ref[pl.ds(r, R), :] = dcg.astype(BF16)
            scv_ref[...] += _colsum(dcv)
            scg_ref[...] += _colsum(dcg)
            _shifted_corr(cpad_ref, r, R, dpad_ref[pl.ds(r, R), :], dwacc_ref, offs)
            return carry

        lax.fori_loop(0, S // R, conv_pass, 0)
        for k in range(K):
            dw_ref[k:k + 1, :] = _colsum(dwacc_ref[SUBLANES * k:SUBLANES * (k + 1), :])

    col = lambda i: (0, i)
    row = jax.ShapeDtypeStruct((1, C), F32)
    blk = 2 * (4 * S * LANES * 4 + 2 * S * LANES * 2) + 2 * (S + _pad_rows(K)) * LANES * 4
    return _streamed_call(
        body, name=name, grid=(C // LANES,),
        in_specs=[_bs((S, LANES), col)] * 4 + [_bs((K, LANES), col), _bs((1, LANES), col), _bs((1, LANES), col)],
        out_specs=[_bs((S, LANES), col), _bs((S, LANES), col), _bs((K, LANES), col)] + [_bs((1, LANES), col)] * 5,
        out_shape=[jax.ShapeDtypeStruct((S, C), BF16), jax.ShapeDtypeStruct((S, C), BF16),
                   jax.ShapeDtypeStruct((K, C), F32), row, row, row, row, row],
        scratch_shapes=[pltpu.VMEM((S + _pad_rows(K), LANES), F32), pltpu.VMEM((S + _pad_rows(K), LANES), F32),
                        pltpu.VMEM((SUBLANES * K, LANES), F32)],
        compiler_params=_cparams(blk),
    )(dc4, c2, cv, cg, w, gg, gb)


def _short_conv_fwd(xin, w, b, name):
    S, C = xin.shape
    K = w.shape[0]
    padf = _front_pad(K)
    R = min(CONV_ROWS, S)
    E = min(EW_ROWS, S)

    def body(x_ref, w_ref, b_ref, o_ref, pad_ref):
        pad_ref[0:padf, :] = jnp.zeros((padf, LANES), F32)
        pad_ref[S + padf:S + padf + SUBLANES, :] = jnp.zeros((SUBLANES, LANES), F32)

        def fill(i, carry):
            r = pl.multiple_of(i * E, E)
            pad_ref[pl.ds(r + padf, E), :] = x_ref[pl.ds(r, E), :]
            return carry

        lax.fori_loop(0, S // E, fill, 0)
        taps = [(padf - (K - 1) + k, functools.partial(lambda k: w_ref[k:k + 1, :], k)) for k in range(K)]

        def conv(i, carry):
            r = pl.multiple_of(i * R, R)
            o_ref[pl.ds(r, R), :] = _shifted_taps(pad_ref, r, R, taps) + b_ref[...]
            return carry

        lax.fori_loop(0, S // R, conv, 0)

    col = lambda i: (0, i)
    blk = 2 * (2 * S * LANES * 4) + (S + _pad_rows(K)) * LANES * 4
    return _streamed_call(
        body, name=name, grid=(C // LANES,),
        in_specs=[_bs((S, LANES), col), _bs((K, LANES), col), _bs((1, LANES), col)],
        out_specs=_bs((S, LANES), col),
        out_shape=jax.ShapeDtypeStruct((S, C), F32),
        scratch_shapes=[pltpu.VMEM((S + _pad_rows(K), LANES), F32)],
        compiler_params=_cparams(blk),
    )(xin, w, b)


def _short_conv_bwd(dy, xin, w, name):
    S, C = xin.shape
    K = w.shape[0]
    padf = _front_pad(K)
    R = min(CONV_ROWS, S)
    E = min(EW_ROWS, S)

    def body(dy_ref, x_ref, w_ref, dx_ref, dw_ref, db_ref, sx_ref, dpad_ref, xpad_ref, dwacc_ref):
        xpad_ref[0:padf, :] = jnp.zeros((padf, LANES), F32)
        xpad_ref[S + padf:S + padf + SUBLANES, :] = jnp.zeros((SUBLANES, LANES), F32)
        dpad_ref[S:S + padf + SUBLANES, :] = jnp.zeros((padf + SUBLANES, LANES), F32)
        dwacc_ref[...] = jnp.zeros_like(dwacc_ref)
        db_ref[...] = jnp.zeros_like(db_ref)
        sx_ref[...] = jnp.zeros_like(sx_ref)

        def fill(i, carry):
            r = pl.multiple_of(i * E, E)
            d = dy_ref[pl.ds(r, E), :]
            dpad_ref[pl.ds(r, E), :] = d
            db_ref[...] += _colsum(d)
            xpad_ref[pl.ds(r + padf, E), :] = x_ref[pl.ds(r, E), :]
            return carry

        lax.fori_loop(0, S // E, fill, 0)
        taps = [(K - 1 - k, functools.partial(lambda k: w_ref[k:k + 1, :], k)) for k in range(K)]
        offs = [padf - (K - 1) + k for k in range(K)]

        def conv_pass(i, carry):
            r = pl.multiple_of(i * R, R)
            dx = _shifted_taps(dpad_ref, r, R, taps)
            dx_ref[pl.ds(r, R), :] = dx.astype(BF16)
            sx_ref[...] += _colsum(dx)
            _shifted_corr(xpad_ref, r, R, dpad_ref[pl.ds(r, R), :], dwacc_ref, offs)
            return carry

        lax.fori_loop(0, S // R, conv_pass, 0)
        for k in range(K):
            dw_ref[k:k + 1, :] = _colsum(dwacc_ref[SUBLANES * k:SUBLANES * (k + 1), :])

    col = lambda i: (0, i)
    row = jax.ShapeDtypeStruct((1, C), F32)
    blk = 2 * (2 * S * LANES * 4 + S * LANES * 2) + 2 * (S + _pad_rows(K)) * LANES * 4
    return _streamed_call(
        body, name=name, grid=(C // LANES,),
        in_specs=[_bs((S, LANES), col), _bs((S, LANES), col), _bs((K, LANES), col)],
        out_specs=[_bs((S, LANES), col), _bs((K, LANES), col), _bs((1, LANES), col), _bs((1, LANES), col)],
        out_shape=[jax.ShapeDtypeStruct((S, C), BF16), jax.ShapeDtypeStruct((K, C), F32), row, row],
        scratch_shapes=[pltpu.VMEM((S + _pad_rows(K), LANES), F32), pltpu.VMEM((S + _pad_rows(K), LANES), F32),
                        pltpu.VMEM((SUBLANES * K, LANES), F32)],
        compiler_params=_cparams(blk),
    )(dy, xin, w)


def _band_panels(width, block):
    assert width % LANES == 0 and block <= LANES
    panels = []
    for c0 in range(0, width, 2 * LANES):
        c1 = min(width, c0 + 2 * LANES)
        r0 = (c0 // block) * block // LANES * LANES
        r1 = min(width, -(-(-(-c1 // block) * block) // LANES) * LANES)
        panels.append((r0, r1, c0, c1))
    return panels


def _gates_fwd(r1, wa, wx, ba, bx, lam, name):
    S, R = r1.shape
    tm = _divisor_tile(S, 256, 16)
    panels = _band_panels(R, R // RNN_BLOCKS)

    def body(r1_ref, wa_ref, wx_ref, ba_ref, bx_ref, lam_ref, ra_ref, ri_ref, a_ref, uu_ref):
        for r0, r1e, c0, c1 in panels:
            rb = r1_ref[:, r0:r1e].astype(BF16)
            ra = _sigmoid(_dot(rb, wa_ref[r0:r1e, c0:c1]) + ba_ref[:, c0:c1])
            ri = _sigmoid(_dot(rb, wx_ref[r0:r1e, c0:c1]) + bx_ref[:, c0:c1])
            log_a = -RG_LRU_C * ra * _softplus(-lam_ref[:, c0:c1])
            ra_ref[:, c0:c1] = ra
            ri_ref[:, c0:c1] = ri
            a_ref[:, c0:c1] = jnp.exp(log_a)
            uu_ref[:, c0:c1] = jnp.sqrt(_neg_expm1(2.0 * log_a)) * (ri * r1_ref[:, c0:c1])

    blk = 2 * (5 * tm * R * 4) + 2 * R * R * 2 + 6 * tm * R * 4
    tile = _bs((tm, R), lambda i: (i, 0))
    return _streamed_call(
        body, name=name, grid=(S // tm,),
        in_specs=[tile, _resident((R, R)), _resident((R, R)), _resident((1, R)), _resident((1, R)), _resident((1, R))],
        out_specs=[tile] * 4,
        out_shape=[jax.ShapeDtypeStruct((S, R), F32)] * 4,
        compiler_params=_cparams(blk),
    )(r1, wa, wx, ba, bx, lam)


def _gates_bwd(guu, da, ra, ri, r1, wa, wx, lam, name):
    S, R = r1.shape
    tm = _divisor_tile(S, 256, 16)
    nsteps = S // tm
    panels = _band_panels(R, R // RNN_BLOCKS)

    def body(g_ref, da_ref, ra_ref, ri_ref, r1_ref, wa_ref, wx_ref, lam_ref,
             dr1_ref, dpa_ref, dpx_ref, dba_ref, dbx_ref, dlam_ref):
        i = pl.program_id(0)

        @pl.when(i == 0)
        def _():
            dba_ref[...] = jnp.zeros_like(dba_ref)
            dbx_ref[...] = jnp.zeros_like(dbx_ref)
            dlam_ref[...] = jnp.zeros_like(dlam_ref)

        g = g_ref[...]
        ra = ra_ref[...]
        ri = ri_ref[...]
        r1_ = r1_ref[...]
        sp = _softplus(-lam_ref[...])
        log_a = -RG_LRU_C * ra * sp
        a = jnp.exp(log_a)
        mult = jnp.sqrt(_neg_expm1(2.0 * log_a))
        d_ri = g * mult * r1_
        dr1 = g * mult * ri
        dmult = g * ri * r1_
        dlog_a = da_ref[...] * a - dmult * (a * a) / mult
        dra = dlog_a * (-RG_LRU_C * sp)
        dlam_ref[...] += _colsum(dlog_a * (-RG_LRU_C * ra))
        dpa = dra * ra * (1.0 - ra)
        dpx = d_ri * ri * (1.0 - ri)
        dba_ref[...] += _colsum(dpa)
        dbx_ref[...] += _colsum(dpx)
        dpa_b = dpa.astype(BF16)
        dpx_b = dpx.astype(BF16)
        dpa_ref[...] = dpa_b
        dpx_ref[...] = dpx_b
        dr1_ref[...] = dr1
        for k0, k1, c0, c1 in panels:
            dr1_ref[:, c0:c1] += (_dot_nt(dpa_ref[:, k0:k1], wa_ref[c0:c1, k0:k1])
                                  + _dot_nt(dpx_ref[:, k0:k1], wx_ref[c0:c1, k0:k1]))

        @pl.when(i == nsteps - 1)
        def _():
            dlam_ref[...] = dlam_ref[...] * (-_sigmoid(-lam_ref[...]))

    blk = 2 * (6 * tm * R * 4 + 2 * tm * R * 2) + 2 * R * R * 2 + 10 * tm * R * 4
    tile = _bs((tm, R), lambda i: (i, 0))
    rowspec = _bs((1, R), lambda i: (0, 0))
    row = jax.ShapeDtypeStruct((1, R), F32)
    return _streamed_call(
        body, name=name, grid=(nsteps,),
        in_specs=[tile] * 5 + [_resident((R, R)), _resident((R, R)), _resident((1, R))],
        out_specs=[tile, tile, tile, rowspec, rowspec, rowspec],
        out_shape=[jax.ShapeDtypeStruct((S, R), F32), jax.ShapeDtypeStruct((S, R), BF16), jax.ShapeDtypeStruct((S, R), BF16),
                   row, row, row],
        compiler_params=_cparams(blk),
    )(guu, da, ra, ri, r1, wa, wx, lam)


def _embed_blocks(w, name):
    H, bk, _ = w.shape

    def body(w_ref, o_ref):
        o_ref[...] = jnp.zeros_like(o_ref)
        for h in range(H):
            o_ref[bk * h:bk * (h + 1), bk * h:bk * (h + 1)] = w_ref[h].astype(BF16)

    return pl.pallas_call(body, name=name, out_shape=jax.ShapeDtypeStruct((H * bk, H * bk), BF16),
                          compiler_params=_cparams(3 * H * bk * H * bk * 2))(w)


def _block_grads(r1, dpa, dpx, name):
    S, R = r1.shape
    bk = R // RNN_BLOCKS
    tk = _divisor_tile(S, 512, 16)
    nsteps = S // tk
    panels = _band_panels(R, bk)

    def body(r1_ref, dpa_ref, dpx_ref, ga_ref, gx_ref, acca_ref, accx_ref):
        k = pl.program_id(0)

        @pl.when(k == 0)
        def _():
            acca_ref[...] = jnp.zeros_like(acca_ref)
            accx_ref[...] = jnp.zeros_like(accx_ref)

        for k0, k1, c0, c1 in panels:
            rb = r1_ref[:, k0:k1].astype(BF16)
            acca_ref[k0:k1, c0:c1] += _dot_tn(rb, dpa_ref[:, c0:c1])
            accx_ref[k0:k1, c0:c1] += _dot_tn(rb, dpx_ref[:, c0:c1])

        @pl.when(k == nsteps - 1)
        def _():
            for h in range(RNN_BLOCKS):
                ga_ref[h] = acca_ref[bk * h:bk * (h + 1), bk * h:bk * (h + 1)]
                gx_ref[h] = accx_ref[bk * h:bk * (h + 1), bk * h:bk * (h + 1)]

    tile = lambda: _bs((tk, R), lambda k: (k, 0))
    out = _bs((RNN_BLOCKS, bk, bk), lambda k: (0, 0, 0))
    sds = jax.ShapeDtypeStruct((RNN_BLOCKS, bk, bk), F32)
    return _streamed_call(
        body, name=name, grid=(nsteps,),
        in_specs=[tile(), tile(), tile()], out_specs=[out, out], out_shape=[sds, sds],
        scratch_shapes=[pltpu.VMEM((R, R), F32), pltpu.VMEM((R, R), F32)],
        compiler_params=_cparams(2 * (tk * R * 8) + 2 * R * R * 4 + 4 * tk * R * 4),
    )(r1, dpa, dpx)


def _scan_geometry(S):
    nseg = SCAN_SEGMENTS if S % (SCAN_SEGMENTS * SUBLANES) == 0 else SUBLANES
    return nseg, S // nseg


def _steps(n, step, init):
    u = SCAN_UNROLL

    def trip(t, carry):
        for k in range(u):
            carry = step(t * u + k, carry)
        return carry

    carry = lax.fori_loop(0, n // u, trip, init)
    for j in range(n - n % u, n):
        carry = step(j, carry)
    return carry


def _scan_fwd(a, u, name):
    S, C = a.shape
    nseg, L = _scan_geometry(S)
    T = min(SUBLANES, L)

    def body(a3, u3, h3, ta_ref, tu_ref, e_ref, p_ref, init_ref):

        def to_steps(i, carry):
            j0 = pl.multiple_of(i * T, T)
            ta_ref[pl.ds(j0, T)] = jnp.swapaxes(a3[:, pl.ds(j0, T), :], 0, 1)
            tu_ref[pl.ds(j0, T)] = jnp.swapaxes(u3[:, pl.ds(j0, T), :], 0, 1)
            return carry

        lax.fori_loop(0, L // T, to_steps, 0)

        def run1(j, carry):
            hs, ps = carry
            aj = ta_ref[j]
            return aj * hs + tu_ref[j], aj * ps

        e_ref[...], p_ref[...] = _steps(L, run1, (jnp.zeros((nseg, LANES), F32), jnp.ones((nseg, LANES), F32)))
        init_ref[0:1, :] = jnp.zeros((1, LANES), F32)
        for s in range(1, nseg):
            init_ref[s:s + 1, :] = e_ref[s - 1:s, :] + p_ref[s - 1:s, :] * init_ref[s - 1:s, :]

        def run2(j, hs):
            hs = ta_ref[j] * hs + tu_ref[j]
            tu_ref[j] = hs
            return hs

        _steps(L, run2, init_ref[...])

        def from_steps(i, carry):
            j0 = pl.multiple_of(i * T, T)
            h3[:, pl.ds(j0, T), :] = jnp.swapaxes(tu_ref[pl.ds(j0, T)], 0, 1)
            return carry

        lax.fori_loop(0, L // T, from_steps, 0)

    seg_block = _bs((nseg, L, LANES), lambda i: (0, 0, i))
    blk = 2 * (3 * S * LANES * 4) + 2 * S * LANES * 4
    return _streamed_call(
        body, name=name, grid=(C // LANES,),
        in_specs=[seg_block, seg_block],
        out_specs=seg_block,
        out_shape=jax.ShapeDtypeStruct((nseg, L, C), F32),
        scratch_shapes=[pltpu.VMEM((L, nseg, LANES), F32)] * 2 + [pltpu.VMEM((nseg, LANES), F32)] * 3,
        compiler_params=_cparams(blk),
    )(a.reshape(nseg, L, C), u.reshape(nseg, L, C)).reshape(S, C)


def _scan_bwd(a, dh, h, name):
    S, C = a.shape
    nseg, L = _scan_geometry(S)
    T = min(SUBLANES, L)
    assert L >= 2

    def body(a3, d3, h3, g3, da3, ta_ref, td_ref, th_ref, e_ref, p_ref, init_ref):

        def to_steps(i, carry):
            j0 = pl.multiple_of(i * T, T)
            for src, dst in ((a3, ta_ref), (d3, td_ref), (h3, th_ref)):
                dst[pl.ds(j0, T)] = jnp.swapaxes(src[:, pl.ds(j0, T), :], 0, 1)
            return carry

        lax.fori_loop(0, L // T, to_steps, 0)
        seg = lax.broadcasted_iota(jnp.int32, (nseg, LANES), 0)
        b_last = jnp.where(seg == nseg - 1, 0.0, pltpu.roll(ta_ref[0], nseg - 1, axis=0))
        h_first = jnp.where(seg == 0, 0.0, pltpu.roll(th_ref[L - 1], 1, axis=0))

        def run1(jj, carry):
            gs, ps = carry
            j = L - 2 - jj
            bj = ta_ref[j + 1]
            return bj * gs + td_ref[j], bj * ps

        e_ref[...], p_ref[...] = _steps(L - 1, run1, (td_ref[L - 1], b_last))
        init_ref[nseg - 1:nseg, :] = jnp.zeros((1, LANES), F32)
        for s in range(nseg - 2, -1, -1):
            init_ref[s:s + 1, :] = e_ref[s + 1:s + 2, :] + p_ref[s + 1:s + 2, :] * init_ref[s + 1:s + 2, :]

        gs = b_last * init_ref[...] + td_ref[L - 1]
        td_ref[L - 1] = gs
        th_ref[L - 1] = gs * th_ref[L - 2]

        def run2(jj, gs):
            j = L - 2 - jj
            gs = ta_ref[j + 1] * gs + td_ref[j]
            td_ref[j] = gs
            th_ref[j] = gs * th_ref[j - 1]
            return gs

        gs = _steps(L - 2, run2, gs)
        gs = ta_ref[1] * gs + td_ref[0]
        td_ref[0] = gs
        th_ref[0] = gs * h_first

        def from_steps(i, carry):
            j0 = pl.multiple_of(i * T, T)
            g3[:, pl.ds(j0, T), :] = jnp.swapaxes(td_ref[pl.ds(j0, T)], 0, 1)
            da3[:, pl.ds(j0, T), :] = jnp.swapaxes(th_ref[pl.ds(j0, T)], 0, 1)
            return carry

        lax.fori_loop(0, L // T, from_steps, 0)

    seg_block = _bs((nseg, L, LANES), lambda i: (0, 0, i))
    blk = 2 * (5 * S * LANES * 4) + 3 * S * LANES * 4
    g, da = _streamed_call(
        body, name=name, grid=(C // LANES,),
        in_specs=[seg_block] * 3,
        out_specs=[seg_block] * 2,
        out_shape=[jax.ShapeDtypeStruct((nseg, L, C), F32)] * 2,
        scratch_shapes=[pltpu.VMEM((L, nseg, LANES), F32)] * 3 + [pltpu.VMEM((nseg, LANES), F32)] * 3,
        compiler_params=_cparams(blk),
    )(a.reshape(nseg, L, C), dh.reshape(nseg, L, C), h.reshape(nseg, L, C))
    return g.reshape(S, C), da.reshape(S, C)


def _mixer_out_fwd(c4, h, rg, gc, gr, x1, wc, wr, wo, g, b, alpha, name):
    S, D = x1.shape
    R = h.shape[1]
    tm = _divisor_tile(S, 256, 16)

    def body(c4_ref, h_ref, rg_ref, gc_ref, gr_ref, x_ref, wc_ref, wr_ref, wo_ref, g_ref, b_ref,
             yc_ref, yr_ref, z_ref, y_ref):
        yc = _dot(c4_ref[...], wc_ref[...])
        q = (h_ref[...] * _gelu(rg_ref[...].astype(F32))).astype(BF16)
        yr = _dot(q, wr_ref[...])
        yc_ref[...] = yc.astype(BF16)
        yr_ref[...] = yr.astype(BF16)
        m = (_sigmoid(gc_ref[...].astype(F32)) * yc + _sigmoid(gr_ref[...].astype(F32)) * yr).astype(BF16)
        z = alpha * x_ref[...] + _dot(m, wo_ref[...])
        z_ref[...] = z
        y_ref[...] = _norm_fwd(z, g_ref[...], b_ref[...])

    blk = 2 * (tm * D * 2 + 2 * tm * R * 4 + 7 * tm * D * 4) + (2 * D * D + R * D) * 2 + 6 * tm * D * 4
    td = _bs((tm, D), lambda i: (i, 0))
    tr = _bs((tm, R), lambda i: (i, 0))
    return _streamed_call(
        body, name=name, grid=(S // tm,),
        in_specs=[td, tr, tr, td, td, td, _layer_resident(wc)[1], _layer_resident(wr)[1], _layer_resident(wo)[1],
                  _resident((1, D)), _resident((1, D))],
        out_specs=[td] * 4,
        out_shape=[jax.ShapeDtypeStruct((S, D), BF16)] * 2 + [jax.ShapeDtypeStruct((S, D), F32)] * 2,
        compiler_params=_cparams(blk),
    )(c4, h, rg, gc, gr, x1, wc[0], wr[0], wo[0], g, b)


def _mixer_out_bwd(dy, z, g, wo, yc, yr, gc, gr, name):
    S, D = dy.shape
    tm = _divisor_tile(S, 256, 16)

    def body(dy_ref, z_ref, g_ref, wo_ref, yc_ref, yr_ref, gc_ref, gr_ref,
             dz_ref, dzb_ref, m_ref, dyc_ref, dyr_ref, dgc_ref, dgr_ref, sgc_ref, sgr_ref, dg_ref, db_ref):
        @pl.when(pl.program_id(0) == 0)
        def _():
            for ref in (sgc_ref, sgr_ref, dg_ref, db_ref):
                ref[...] = jnp.zeros_like(ref)

        dy_ = dy_ref[...]
        dz, xhat = _norm_bwd(z_ref[...], g_ref[...], dy_)
        dg_ref[...] += _colsum(dy_ * xhat)
        db_ref[...] += _colsum(dy_)
        dz_ref[...] = dz
        dzb = dz.astype(BF16)
        dzb_ref[...] = dzb
        dm = _dot_nt(dzb, wo_ref[...])
        yc = yc_ref[...].astype(F32)
        yr = yr_ref[...].astype(F32)
        sc = _sigmoid(gc_ref[...].astype(F32))
        sr = _sigmoid(gr_ref[...].astype(F32))
        m_ref[...] = (sc * yc + sr * yr).astype(BF16)
        dyc_ref[...] = (dm * sc).astype(BF16)
        dyr_ref[...] = (dm * sr).astype(BF16)
        dgc = dm * yc * sc * (1.0 - sc)
        dgr = dm * yr * sr * (1.0 - sr)
        dgc_ref[...] = dgc.astype(BF16)
        dgr_ref[...] = dgr.astype(BF16)
        sgc_ref[...] += _colsum(dgc)
        sgr_ref[...] += _colsum(dgr)

    blk = 2 * (7 * tm * D * 4 + 6 * tm * D * 2) + D * D * 2 + 8 * tm * D * 4
    td = _bs((tm, D), lambda i: (i, 0))
    rowspec = _bs((1, D), lambda i: (0, 0))
    row = jax.ShapeDtypeStruct((1, D), F32)
    bfd = jax.ShapeDtypeStruct((S, D), BF16)
    return _streamed_call(
        body, name=name, grid=(S // tm,),
        in_specs=[td, td, _resident((1, D)), _layer_resident(wo)[1], td, td, td, td],
        out_specs=[td] * 7 + [rowspec] * 4,
        out_shape=[jax.ShapeDtypeStruct((S, D), F32), bfd, bfd, bfd, bfd, bfd, bfd, row, row, row, row],
        compiler_params=_cparams(blk),
    )(dy, z, g, wo[0], yc, yr, gc, gr)


def _branch_bwd(dyc, dyr, wc, wr, h, rg, name):
    S, D = dyc.shape
    R = h.shape[1]
    tm = _divisor_tile(S, 256, 16)

    def body(dyc_ref, dyr_ref, wc_ref, wr_ref, h_ref, rg_ref, dc4_ref, dh_ref, drg_ref, q_ref, srg_ref):
        @pl.when(pl.program_id(0) == 0)
        def _():
            srg_ref[...] = jnp.zeros_like(srg_ref)

        dc4_ref[...] = _dot_nt(dyc_ref[...], wc_ref[...]).astype(BF16)
        dq = _dot_nt(dyr_ref[...], wr_ref[...])
        h_ = h_ref[...]
        rg_ = rg_ref[...].astype(F32)
        ge = _gelu(rg_)
        dh_ref[...] = dq * ge
        drg = dq * h_ * _gelu_grad(rg_)
        drg_ref[...] = drg.astype(BF16)
        srg_ref[...] += _colsum(drg)
        q_ref[...] = (h_ * ge).astype(BF16)

    blk = 2 * (2 * tm * D * 2 + tm * D * 4 + 3 * tm * R * 4 + 2 * tm * R * 2) + (D * D + R * D) * 2 + 6 * tm * R * 4
    td = _bs((tm, D), lambda i: (i, 0))
    tr = _bs((tm, R), lambda i: (i, 0))
    return _streamed_call(
        body, name=name, grid=(S // tm,),
        in_specs=[td, td, _layer_resident(wc)[1], _layer_resident(wr)[1], tr, tr],
        out_specs=[td, tr, tr, tr, _bs((1, R), lambda i: (0, 0))],
        out_shape=[jax.ShapeDtypeStruct((S, D), BF16), jax.ShapeDtypeStruct((S, R), F32), jax.ShapeDtypeStruct((S, R), BF16),
                   jax.ShapeDtypeStruct((S, R), BF16), jax.ShapeDtypeStruct((1, R), F32)],
        compiler_params=_cparams(blk),
    )(dyc, dyr, wc[0], wr[0], h, rg)


def _loss_head(y, target, name):
    S, D = y.shape
    tm = _divisor_tile(S, 512, 16)
    nsteps = S // tm

    def body(y_ref, t_ref, loss_ref, dy_ref, acc_ref):
        i = pl.program_id(0)

        @pl.when(i == 0)
        def _():
            acc_ref[...] = jnp.zeros_like(acc_ref)

        err = y_ref[...] - t_ref[...]
        dy_ref[...] = err * (1.0 / D)
        acc_ref[...] += _colsum(err * err)

        @pl.when(i == nsteps - 1)
        def _():
            loss_ref[...] = jnp.sum(acc_ref[...], axis=-1, keepdims=True) * (0.5 / D)

    td = _bs((tm, D), lambda i: (i, 0))
    return _streamed_call(
        body, name=name, grid=(nsteps,),
        in_specs=[td, td],
        out_specs=[_bs((1, 1), lambda i: (0, 0)), td],
        out_shape=[jax.ShapeDtypeStruct((1, 1), F32), jax.ShapeDtypeStruct((S, D), F32)],
        scratch_shapes=[pltpu.VMEM((1, D), F32)],
        compiler_params=_cparams(2 * 3 * tm * D * 4),
    )(y, target)


def _adamw_math(w, g, m, v):
    m = ADAM_B1 * m + (1.0 - ADAM_B1) * g
    v = ADAM_B2 * v + (1.0 - ADAM_B2) * (g * g)
    m_hat = m / (1.0 - ADAM_B1 ** ADAM_STEP)
    v_hat = v / (1.0 - ADAM_B2 ** ADAM_STEP)
    delta = -ADAM_LR * (m_hat / (jnp.sqrt(v_hat) + ADAM_EPS) + ADAM_WD * w)
    return delta, m, v


def _adamw_sharded(w, m, v, own, sib, rem, layer, filled, name):
    layers, r, c = w.shape
    r2 = r // 2
    tr = _divisor_tile(r2, max(16, (1 << 20) // (4 * c) // 16 * 16), 16)
    n_out = 4

    def body(w_ref, m_ref, v_ref, own_ref, sib_ref, rem_ref, *rest):
        g_ref, d_ref, nm_ref, nv_ref = rest[-n_out:]
        mine = pl.program_id(0) == lax.axis_index("c")
        g = jnp.where(mine, own_ref[...], sib_ref[...]).astype(F32)
        for j in range(N_CHIPS - 1):
            g = g + rem_ref[j].astype(F32)
        delta, nm, nv = _adamw_math(w_ref[...], g, m_ref[...], v_ref[...])
        g_ref[...] = g
        d_ref[...] = delta
        nm_ref[...] = nm
        nv_ref[...] = nv

    halves = lambda a: a.reshape(layers, 2, r2, c)
    tile = _bs((None, None, tr, c), lambda h, i: (layer, h, i, 0))
    flat = _bs((tr, c), lambda h, i: (i, 0))
    sds = jax.ShapeDtypeStruct((layers, 2, r2, c), F32)
    passed = [] if filled is None else [halves(a) for a in filled]
    outs = _streamed_call(
        body, name=name, grid=(2, r2 // tr),
        in_specs=[tile, tile, tile, flat, flat, _bs((N_CHIPS - 1, None, tr, c), lambda h, i: (0, h, i, 0))] + [ANY] * len(passed),
        out_specs=[tile] * n_out,
        out_shape=[sds] * n_out,
        input_output_aliases={6 + k: k for k in range(len(passed))},
        compiler_params=_cparams(2 * (7 * tr * c * 4 + (N_CHIPS + 1) * tr * c * 2) + 6 * tr * c * 4),
    )(halves(w), halves(m), halves(v), own, sib, rem, *passed)
    return [o.reshape(layers, r, c) for o in outs]


def _adamw_flat(w, m, v, g, name):
    rows = w.shape[0]
    tr = _divisor_tile(rows, 1024, SUBLANES)

    def body(w_ref, m_ref, v_ref, g_ref, d_ref, nm_ref, nv_ref):
        delta, nm, nv = _adamw_math(w_ref[...], g_ref[...], m_ref[...], v_ref[...])
        d_ref[...] = delta
        nm_ref[...] = nm
        nv_ref[...] = nv

    tile = _bs((tr, LANES), lambda i: (i, 0))
    sds = jax.ShapeDtypeStruct(w.shape, F32)
    return _streamed_call(
        body, name=name, grid=(rows // tr,),
        in_specs=[tile] * 4, out_specs=[tile] * 3, out_shape=[sds] * 3,
        compiler_params=_cparams(2 * 7 * tr * LANES * 4),
    )(w, m, v, g)


def _half_view(g, by_cols):
    rows, cols = g.shape
    if by_cols:
        return g.reshape(2, rows // 2, cols)
    return g.reshape(N_CHIPS, 2, rows // (2 * N_CHIPS), cols)


def _pair_sum_bf16(view, theirs, by_cols, name):
    rows, c = theirs.shape[-2:]
    tr = _divisor_tile(rows, max(16, (1 << 20) // (4 * c) // 16 * 16), 16)

    def body(g0_ref, g1_ref, t_ref, o_ref):
        mine = jnp.where(lax.axis_index("c") == 0, g0_ref[...], g1_ref[...])
        o_ref[...] = (mine + t_ref[...].astype(F32)).astype(BF16)

    if by_cols:
        grid = (rows // tr,)
        halves = [_bs((None, tr, c), functools.partial(lambda h, i: (h, i, 0), h)) for h in range(2)]
        tile = _bs((tr, c), lambda i: (i, 0))
    else:
        grid = (N_CHIPS, rows // tr)
        halves = [_bs((None, None, tr, c), functools.partial(lambda h, k, i: (k, h, i, 0), h)) for h in range(2)]
        tile = _bs((None, tr, c), lambda k, i: (k, i, 0))
    return _streamed_call(
        body, name=name, grid=grid,
        in_specs=halves + [tile], out_specs=tile, out_shape=jax.ShapeDtypeStruct(theirs.shape, BF16),
        compiler_params=_cparams(2 * 4 * tr * c * 4),
    )(view, view, theirs)


ANY = pl.BlockSpec(memory_space=pl.ANY)


def _mesh_position():
    return lax.axis_index("x"), lax.axis_index("y"), lax.axis_index("c")


def _other_chips():
    x, y, c = _mesh_position()
    chips = [(1 - x, y), (x, 1 - y), (1 - x, 1 - y)]
    return 2 * x + y, (x, y, 1 - c), chips, [2 * cx + cy for cx, cy in chips]


def _chip_slab(ref, k, width, by_cols):
    if by_cols:
        start = k * width if isinstance(k, int) else pl.multiple_of(k * width, LANES)
        return ref.at[:, pl.ds(start, width)]
    return ref.at[k]


HBM = pl.BlockSpec(memory_space=pltpu.HBM)
SEM = pl.BlockSpec(memory_space=pltpu.SEMAPHORE)
DATAFLOW = pltpu.SideEffectType.DATAFLOW_SIDE_EFFECTING
N_GATHER_COPIES = 4


def _land_shape(src, by_cols):
    return src.shape[:-1] + (N_CHIPS * src.shape[-1],) if by_cols else (N_CHIPS,) + src.shape


def _gather_copy(src_ref, land_ref, by_cols, send_sems, recv_sems, pos, j, slab, to):
    width = src_ref.shape[-1]
    return pltpu.make_async_remote_copy(src_ref=src_ref, dst_ref=_chip_slab(land_ref, slab, width, by_cols),
                                        send_sem=send_sems.at[N_GATHER_COPIES * pos + j],
                                        recv_sem=recv_sems.at[N_GATHER_COPIES * pos + j],
                                        device_id=to, device_id_type=MESH)


def _gather_start(srcs, by_cols, groups, name):
    U = len(srcs)
    G = len(groups)
    lands = [lax.empty(_land_shape(s, bc), s.dtype) for s, bc in zip(srcs, by_cols)]

    def body(*refs):
        src = refs[:U]
        land = refs[U:2 * U]
        send_sems = refs[2 * U:2 * U + G]
        recv_sems = refs[2 * U + G:2 * U + 2 * G]
        token = refs[-1]
        c = lax.axis_index("c")
        me, sibling, chips, _ = _other_chips()
        targets = [(*chip, c) for chip in chips] + [sibling]
        for g, members in enumerate(groups):
            for pos, u in enumerate(members):
                for j, to in enumerate(targets):
                    _gather_copy(src[u], land[u], by_cols[u], send_sems[g], recv_sems[g], pos, j, me, to).start()
        token[...] = jnp.zeros_like(token)

    sem_shapes = [pltpu.SemaphoreType.DMA((len(m) * N_GATHER_COPIES,)) for m in groups]
    outs = pl.pallas_call(
        body, name=name,
        out_shape=tuple(sem_shapes + sem_shapes + [pltpu.HBM(s.shape, s.dtype) for s in srcs]
                        + [pltpu.HBM(v.shape, v.dtype) for v in lands] + [jax.ShapeDtypeStruct((SUBLANES, LANES), F32)]),
        in_specs=[HBM] * (2 * U),
        out_specs=tuple([SEM] * (2 * G) + [HBM] * (2 * U) + [pl.BlockSpec(memory_space=pltpu.VMEM)]),
        input_output_aliases={i: 2 * G + i for i in range(2 * U)},
        compiler_params=pltpu.CompilerParams(has_side_effects=DATAFLOW),
    )(*[pltpu.with_memory_space_constraint(a, pltpu.HBM) for a in list(srcs) + lands])
    return outs[:G], outs[G:2 * G], outs[2 * G:2 * G + U], outs[2 * G + U:2 * G + 2 * U]


def _gather_wait(srcs, lands, by_cols, send_sems, recv_sems, after, name):
    n = len(srcs)

    def body(*refs):
        src = refs[:n]
        land = refs[n:2 * n]
        send_ref, recv_ref = refs[2 * n:2 * n + 2]
        _, sibling, _, _ = _other_chips()
        for pos in range(n):
            for j in range(N_GATHER_COPIES):
                cp = _gather_copy(src[pos], land[pos], by_cols[pos], send_ref, recv_ref, pos, j, 0, sibling)
                cp.wait_send()
                cp.wait_recv()

    outs = pl.pallas_call(
        body, name=name,
        out_shape=tuple([pltpu.HBM(s.shape, s.dtype) for s in srcs] + [pltpu.HBM(v.shape, v.dtype) for v in lands]),
        in_specs=[HBM] * (2 * n) + [SEM, SEM, pl.BlockSpec(memory_space=pl.ANY)],
        out_specs=tuple([HBM] * (2 * n)),
        input_output_aliases={i: i for i in range(2 * n)},
        compiler_params=pltpu.CompilerParams(has_side_effects=DATAFLOW),
    )(*srcs, *lands, send_sems, recv_sems, after)
    return outs[n:]


def _scatter_grads(csums, by_cols, name):
    n = len(csums)
    shard = [(s.shape[0], s.shape[1] // N_CHIPS) if bc else s.shape[1:] for s, bc in zip(csums, by_cols)]

    def body(*refs):
        src = refs[:n]
        rem = refs[n:2 * n]
        sib = refs[2 * n:3 * n]
        send_sems, recv_sems = refs[3 * n:]
        c = lax.axis_index("c")
        me, sibling, chips, chip_ids = _other_chips()

        def remote(i, k, src_ref, dst_ref, to):
            return pltpu.make_async_remote_copy(src_ref=src_ref, dst_ref=dst_ref, send_sem=send_sems.at[i, k],
                                                recv_sem=recv_sems.at[i, k], device_id=to, device_id_type=MESH)

        def part(i, k):
            return _chip_slab(src[i], k, shard[i][-1], by_cols[i])

        started = []
        for i in range(n):
            for j in range(3):
                started.append(remote(i, j, part(i, chip_ids[j]), rem[i].at[j, c], (*chips[j], c)))
            started.append(remote(i, 6, part(i, me), sib[i], sibling))
        for cp in started:
            cp.start()
        for i in range(n):
            for j in range(3):
                slot = rem[i].at[j, c]
                remote(i, j, slot, slot, sibling).wait_recv()
                fwd = remote(i, 3 + j, slot, slot, sibling)
                fwd.start()
                started.append(fwd)
        for i in range(n):
            for j in range(3):
                slot = rem[i].at[j, 1 - c]
                remote(i, 3 + j, slot, slot, sibling).wait_recv()
            remote(i, 6, sib[i], sib[i], sibling).wait_recv()
        for cp in started:
            cp.wait_send()

    out_shape = ([jax.ShapeDtypeStruct((N_CHIPS - 1, 2) + tuple(sh), s.dtype) for s, sh in zip(csums, shard)]
                 + [jax.ShapeDtypeStruct(tuple(sh), s.dtype) for s, sh in zip(csums, shard)])
    outs = _streamed_call(
        body, name=name,
        in_specs=[ANY] * n, out_specs=[ANY] * (2 * n), out_shape=out_shape,
        scratch_shapes=[pltpu.SemaphoreType.DMA((n, 7)), pltpu.SemaphoreType.DMA((n, 7))],
    )(*csums)
    return outs[:n], outs[n:]


def _sibling_exchange(views, by_cols, name):
    n = len(views)

    def body(*refs):
        src = refs[:n]
        theirs = refs[n:2 * n]
        send_sems, recv_sems = refs[2 * n:]
        x, y, c = _mesh_position()
        copies = []
        for i in range(n):
            half = src[i].at[1 - c] if by_cols[i] else src[i].at[:, 1 - c]
            copies.append(pltpu.make_async_remote_copy(src_ref=half, dst_ref=theirs[i], send_sem=send_sems.at[i],
                                                       recv_sem=recv_sems.at[i], device_id=(x, y, 1 - c), device_id_type=MESH))
        for cp in copies:
            cp.start()
        for cp in copies:
            cp.wait()

    out_shape = [jax.ShapeDtypeStruct(v.shape[1:] if bc else v.shape[:1] + v.shape[2:], v.dtype) for v, bc in zip(views, by_cols)]
    return _streamed_call(
        body, name=name,
        in_specs=[ANY] * n, out_specs=[ANY] * n, out_shape=out_shape,
        scratch_shapes=[pltpu.SemaphoreType.DMA((n,)), pltpu.SemaphoreType.DMA((n,))],
    )(*views)


N_SCATTER_COPIES = 7


def _scatter_start(csums, by_cols, name):
    n = len(csums)
    shard = [(s.shape[0], s.shape[1] // N_CHIPS) if bc else s.shape[1:] for s, bc in zip(csums, by_cols)]
    rems = [lax.empty((N_CHIPS - 1, 2) + tuple(sh), s.dtype) for s, sh in zip(csums, shard)]
    sibs = [lax.empty(tuple(sh), s.dtype) for s, sh in zip(csums, shard)]

    def body(*refs):
        src = refs[:n]
        rem = refs[n:2 * n]
        sib = refs[2 * n:3 * n]
        send_sems, recv_sems = refs[3 * n:3 * n + 2]
        token = refs[-1]
        c = lax.axis_index("c")
        me, sibling, chips, chip_ids = _other_chips()
        for i in range(n):
            base = N_SCATTER_COPIES * i
            for j in range(3):
                part = _chip_slab(src[i], chip_ids[j], shard[i][-1], by_cols[i])
                for core in range(2):
                    pltpu.make_async_remote_copy(src_ref=part, dst_ref=rem[i].at[j, c], send_sem=send_sems.at[base + 2 * j + core],
                                                 recv_sem=recv_sems.at[base + 2 * j + c], device_id=(*chips[j], core),
                                                 device_id_type=MESH).start()
            pltpu.make_async_remote_copy(src_ref=_chip_slab(src[i], me, shard[i][-1], by_cols[i]), dst_ref=sib[i],
                                         send_sem=send_sems.at[base + 6], recv_sem=recv_sems.at[base + 6], device_id=sibling,
                                         device_id_type=MESH).start()
        token[...] = jnp.zeros_like(token)

    sems = pltpu.SemaphoreType.DMA((N_SCATTER_COPIES * n,))
    operands = list(csums) + rems + sibs
    outs = pl.pallas_call(
        body, name=name,
        out_shape=tuple([sems, sems] + [pltpu.HBM(a.shape, a.dtype) for a in operands] + [jax.ShapeDtypeStruct((SUBLANES, LANES), F32)]),
        in_specs=[HBM] * (3 * n),
        out_specs=tuple([SEM, SEM] + [HBM] * (3 * n) + [pl.BlockSpec(memory_space=pltpu.VMEM)]),
        input_output_aliases={i: 2 + i for i in range(3 * n)},
        compiler_params=pltpu.CompilerParams(has_side_effects=DATAFLOW),
    )(*[pltpu.with_memory_space_constraint(a, pltpu.HBM) for a in operands])
    return (outs[0], outs[1], outs[2:2 + n], outs[2 + n:2 + 2 * n], outs[2 + 2 * n:2 + 3 * n]), outs[-1]


def _scatter_wait(send_sems, recv_sems, srcs, rems, sibs, by_cols, after, name):
    n = len(srcs)

    def body(*refs):
        src = refs[:n]
        rem = refs[n:2 * n]
        sib = refs[2 * n:3 * n]
        send_ref, recv_ref = refs[3 * n:3 * n + 2]
        _, sibling, _, _ = _other_chips()
        for i in range(n):
            base = N_SCATTER_COPIES * i
            width = sib[i].shape[-1]
            for j in range(3):
                for core in range(2):
                    cp = pltpu.make_async_remote_copy(src_ref=_chip_slab(src[i], 0, width, by_cols[i]), dst_ref=rem[i].at[j, core],
                                                      send_sem=send_ref.at[base + 2 * j + core],
                                                      recv_sem=recv_ref.at[base + 2 * j + core], device_id=sibling,
                                                      device_id_type=MESH)
                    cp.wait_send()
                    cp.wait_recv()
            cp = pltpu.make_async_remote_copy(src_ref=_chip_slab(src[i], 0, width, by_cols[i]), dst_ref=sib[i],
                                              send_sem=send_ref.at[base + 6], recv_sem=recv_ref.at[base + 6], device_id=sibling,
                                              device_id_type=MESH)
            cp.wait_send()
            cp.wait_recv()

    operands = list(srcs) + list(rems) + list(sibs)
    outs = pl.pallas_call(
        body, name=name,
        out_shape=tuple(pltpu.HBM(a.shape, a.dtype) for a in operands),
        in_specs=[HBM] * (3 * n) + [SEM, SEM, pl.BlockSpec(memory_space=pl.ANY)],
        out_specs=tuple([HBM] * (3 * n)),
        input_output_aliases={i: i for i in range(3 * n)},
        compiler_params=pltpu.CompilerParams(has_side_effects=DATAFLOW),
    )(*operands, send_sems, recv_sems, after)
    return outs[:n], outs[n:2 * n], outs[2 * n:3 * n]


def _all_reduce_small(v, name):
    _, rows, _ = v.shape

    def body(v_ref, o_ref, recv_ref, send_sems, recv_sems):
        x, y, c = _mesh_position()
        me = 4 * x + 2 * y + c
        peers = []
        for d in range(1, N_DEV):
            px, py, pc = x ^ ((d >> 2) & 1), y ^ ((d >> 1) & 1), c ^ (d & 1)
            peers.append(((px, py, pc), 4 * px + 2 * py + pc))

        def remote(k, src_ref, dst_ref, to):
            return pltpu.make_async_remote_copy(src_ref=src_ref, dst_ref=dst_ref, send_sem=send_sems.at[k],
                                                recv_sem=recv_sems.at[k], device_id=to, device_id_type=MESH)

        scatter = [remote(d, v_ref.at[pid], recv_ref.at[me], to) for d, (to, pid) in enumerate(peers)]
        for cp in scatter:
            cp.start()
        recv_ref[pl.ds(me, 1)] = v_ref[pl.ds(me, 1)]
        for d, (to, pid) in enumerate(peers):
            remote(d, v_ref.at[pid], recv_ref.at[pid], to).wait_recv()
        total = recv_ref[0]
        for s in range(1, N_DEV):
            total = total + recv_ref[s]
        o_ref[pl.ds(me, 1)] = total[None]
        gather = [remote(N_DEV - 1 + d, o_ref.at[me], o_ref.at[me], to) for d, (to, pid) in enumerate(peers)]
        for cp in gather:
            cp.start()
        for d, (to, pid) in enumerate(peers):
            remote(N_DEV - 1 + d, o_ref.at[pid], o_ref.at[pid], to).wait_recv()
        for cp in scatter + gather:
            cp.wait_send()

    vm = pl.BlockSpec(memory_space=pltpu.VMEM)
    return pl.pallas_call(
        body, name=name,
        in_specs=[vm], out_specs=vm, out_shape=jax.ShapeDtypeStruct(v.shape, F32),
        scratch_shapes=[pltpu.VMEM(v.shape, F32), pltpu.SemaphoreType.DMA((2 * (N_DEV - 1),)),
                        pltpu.SemaphoreType.DMA((2 * (N_DEV - 1),))],
        compiler_params=_cparams(4 * _nbytes(v.shape, F32)),
    )(v)


SHARDED_MATS = ("ffn1_w_gu", "ffn1_w_down", "mix_w_in", "conv_w_proj", "rnn_w_proj", "mix_w_out", "ffn2_w_gu", "ffn2_w_down")
COL_SHARDED = ("ffn1_w_gu", "ffn2_w_gu", "conv_dw_w")
SHARDED_VECS = ("conv_dw_w", "rnn_conv_w")
WEIGHT_NAMES = ("ffn1_w_gu", "ffn1_w_down", "ln1_g", "ln1_b", "mix_w_in", "mix_b_in", "conv_dw_w", "conv_dw_b", "conv_gn_g",
                "conv_gn_b", "conv_w_proj", "rnn_conv_w", "rnn_conv_b", "rnn_w_a", "rnn_b_a", "rnn_w_x", "rnn_b_x",
                "rnn_lambda", "rnn_w_proj", "mix_w_out", "ln2_g", "ln2_b", "ffn2_w_gu", "ffn2_w_down", "ln3_g", "ln3_b")
SMALL_NAMES = tuple(n for n in WEIGHT_NAMES if n not in SHARDED_MATS)
SECTION_NAMES = ("cv", "cg", "rx", "rg", "gc", "gr")


def _unshard_cols(gathered):
    k4, K, n = gathered.shape
    return jnp.transpose(gathered, (1, 0, 2)).reshape(K, k4 * n)


def _row(v):
    return v.reshape(1, -1)


def _layer_forward(x0, p, alpha, l, hooks):
    t = f"l{l}_"
    sv = {"x0": x0}
    x1, sv["z1"], sv["hg1"], sv["hu1"] = _ffn_fwd(x0, p["wgu1"], p["wd1"], p["ln1_g"], p["ln1_b"], alpha, t + "ffn1_fwd")
    sv["x1"] = x1
    hooks.get("after_ffn1", lambda v: None)(x1)
    sec = dict(zip(SECTION_NAMES, _mix_in(x1, p["win"], p["bin"], p["sections"], t + "mix_in")))
    sv.update(sec)
    sv["c2"], c4 = _conv_branch_fwd(sec["cv"], sec["cg"], p["conv_dw_w"], p["conv_dw_b"], p["conv_gn_g"], p["conv_gn_b"], t + "conv_fwd")
    sv["c4"] = c4
    r1 = _short_conv_fwd(sec["rx"], p["rnn_conv_w"], p["rnn_conv_b"], t + "rconv_fwd")
    sv["r1"] = r1
    sv["ra"], sv["ri"], a, uu = _gates_fwd(r1, p["wa"], p["wx"], p["rnn_b_a"], p["rnn_b_x"], p["rnn_lambda"], t + "gates_fwd")
    sv["a"] = a
    h = _scan_fwd(a, uu, t + "scan_fwd")
    sv["h"] = h
    hooks.get("after_scan", lambda v: None)(h)
    sv["yc"], sv["yr"], sv["z2"], x2 = _mixer_out_fwd(c4, h, sec["rg"], sec["gc"], sec["gr"], x1, p["wc"], p["wr"], p["wo"],
                                                      p["ln2_g"], p["ln2_b"], alpha, t + "mixout_fwd")
    sv["x2"] = x2
    hooks.get("after_mixer", lambda v: None)(x2)
    x3, sv["z3"], sv["hg2"], sv["hu2"] = _ffn_fwd(x2, p["wgu2"], p["wd2"], p["ln3_g"], p["ln3_b"], alpha, t + "ffn2_fwd")
    hooks.get("after_layer", lambda v: None)(x3)
    return x3, sv


def _layer_backward(dy, p, sv, alpha, l, before_ffn1=None):
    t = f"l{l}_"
    g, gb = {}, {}
    dx2, df, a_act, dhg, dhu, g["ln3_g"], g["ln3_b"] = _ffn_bwd(dy, sv["z3"], sv["hg2"], sv["hu2"], p["wgu2"], p["wd2"],
                                                                 p["ln3_g"], alpha, t + "ffn2_bwd")
    g["ffn2_w_down"], gb["ffn2_w_down"] = _mm_tn(a_act, df, t + "dwd2")
    g["ffn2_w_gu"], gb["ffn2_w_gu"] = _mm_tn_pair(sv["x2"], dhg, dhu, t + "dwgu2")
    (dz2, dz2b, m_b, dyc, dyr, dgc, dgr, s_gc, s_gr, g["ln2_g"], g["ln2_b"]) = _mixer_out_bwd(
        dx2, sv["z2"], p["ln2_g"], p["wo"], sv["yc"], sv["yr"], sv["gc"], sv["gr"], t + "mixout_bwd")
    g["mix_w_out"], gb["mix_w_out"] = _mm_tn(m_b, dz2b, t + "dwo")
    dc4, dh, drg, q_b, s_rg = _branch_bwd(dyc, dyr, p["wc"], p["wr"], sv["h"], sv["rg"], t + "branch_bwd")
    g["conv_w_proj"], gb["conv_w_proj"] = _mm_tn(sv["c4"], dyc, t + "dwc")
    g["rnn_w_proj"], gb["rnn_w_proj"] = _mm_tn(q_b, dyr, t + "dwr")
    (dcv, dcg, g["conv_dw_w"], g["conv_dw_b"], g["conv_gn_g"], g["conv_gn_b"], s_cv, s_cg) = _conv_branch_bwd(
        dc4, sv["c2"], sv["cv"], sv["cg"], p["conv_dw_w"], p["conv_gn_g"], p["conv_gn_b"], t + "conv_bwd")
    guu, da = _scan_bwd(sv["a"], dh, sv["h"], t + "scan_bwd")
    dr1, dpa, dpx, g["rnn_b_a"], g["rnn_b_x"], g["rnn_lambda"] = _gates_bwd(
        guu, da, sv["ra"], sv["ri"], sv["r1"], p["wa"], p["wx"], p["rnn_lambda"], t + "gates_bwd")
    g["rnn_w_a"], g["rnn_w_x"] = _block_grads(sv["r1"], dpa, dpx, t + "dwax")
    drx, g["rnn_conv_w"], g["rnn_conv_b"], s_rx = _short_conv_bwd(dr1, sv["rx"], p["rnn_conv_w"], t + "rconv_bwd")
    du = {"cv": dcv, "cg": dcg, "rx": drx, "rg": drg, "gc": dgc, "gr": dgr}
    order = ("cv", "cg", "rx", "rg", "gc", "gr")
    pieces = [_mm_tn(du[s], sv["x1"], t + "dwin_" + s) for s in order]
    g["mix_w_in"] = jnp.concatenate([f for f, _ in pieces], axis=0)
    gb["mix_w_in"] = jnp.concatenate([h for _, h in pieces], axis=0)
    g["mix_b_in"] = jnp.concatenate([s_cv, s_cg, s_rx, s_rg, s_gc, s_gr], axis=1)
    dx1 = _mix_dx(dz2, [du[s] for s in order], p["win"], p["sections"], alpha, t + "mix_dx")
    ln1_g = p["ln1_g"] if before_ffn1 is None else p["ln1_g"] + before_ffn1(g, gb)[0:1, 0:1]
    dx0, df, a_act, dhg, dhu, g["ln1_g"], g["ln1_b"] = _ffn_bwd(dx1, sv["z1"], sv["hg1"], sv["hu1"], p["wgu1"], p["wd1"],
                                                                 ln1_g, alpha, t + "ffn1_bwd")
    g["ffn1_w_down"], gb["ffn1_w_down"] = _mm_tn(a_act, df, t + "dwd1")
    g["ffn1_w_gu"], gb["ffn1_w_gu"] = _mm_tn_pair(sv["x0"], dhg, dhu, t + "dwgu1")
    return dx0, g, gb


def _pack_small(arrays, piece_rows):
    flat = jnp.concatenate([a.reshape(-1) for a in arrays])
    total = N_DEV * piece_rows * LANES
    return jnp.pad(flat, (0, total - flat.shape[0])).reshape(N_DEV, piece_rows, LANES)


def _unpack_small(packed, shapes):
    flat = packed.reshape(-1)
    out, off = [], 0
    for shp in shapes:
        n = 1
        for s in shp:
            n *= s
        out.append(flat[off:off + n].reshape(shp))
        off += n
    return out


def kernel(x, ffn1_w_gu, ffn1_w_down, ln1_g, ln1_b, mix_w_in, mix_b_in, conv_dw_w, conv_dw_b, conv_gn_g, conv_gn_b, conv_w_proj, rnn_conv_w, rnn_conv_b, rnn_w_a, rnn_b_a, rnn_w_x, rnn_b_x, rnn_lambda, rnn_w_proj, mix_w_out, ln2_g, ln2_b, ffn2_w_gu, ffn2_w_down, ln3_g, ln3_b, loss_target, m_ffn1_w_gu, m_ffn1_w_down, m_ln1_g, m_ln1_b, m_mix_w_in, m_mix_b_in, m_conv_dw_w, m_conv_dw_b, m_conv_gn_g, m_conv_gn_b, m_conv_w_proj, m_rnn_conv_w, m_rnn_conv_b, m_rnn_w_a, m_rnn_b_a, m_rnn_w_x, m_rnn_b_x, m_rnn_lambda, m_rnn_w_proj, m_mix_w_out, m_ln2_g, m_ln2_b, m_ffn2_w_gu, m_ffn2_w_down, m_ln3_g, m_ln3_b, v_ffn1_w_gu, v_ffn1_w_down, v_ln1_g, v_ln1_b, v_mix_w_in, v_mix_b_in, v_conv_dw_w, v_conv_dw_b, v_conv_gn_g, v_conv_gn_b, v_conv_w_proj, v_rnn_conv_w, v_rnn_conv_b, v_rnn_w_a, v_rnn_b_a, v_rnn_w_x, v_rnn_b_x, v_rnn_lambda, v_rnn_w_proj, v_mix_w_out, v_ln2_g, v_ln2_b, v_ffn2_w_gu, v_ffn2_w_down, v_ln3_g, v_ln3_b):
    args = locals()
    W = {n: args[n] for n in WEIGHT_NAMES}
    M = {n: args["m_" + n] for n in WEIGHT_NAMES}
    V = {n: args["v_" + n] for n in WEIGHT_NAMES}
    depth = ln1_g.shape[0]
    assert depth == 2, "each core of a chip moves one layer's weights and gradients"
    alpha = float((2 * depth) ** 0.25)
    S, D = x.shape[1], x.shape[2]
    F = ffn1_w_down.shape[1] * N_CHIPS
    R = rnn_w_proj.shape[1] * N_CHIPS
    chip = 2 * lax.axis_index("x") + lax.axis_index("y")

    for d in (W, M, V):
        d["mix_w_in"] = jnp.transpose(d["mix_w_in"], (0, 2, 1))

    names = SHARDED_MATS + SHARDED_VECS
    unit_groups = [[(0, "ffn1_w_gu"), (0, "ffn1_w_down")], [(0, "mix_w_in"), (0, "conv_dw_w"), (0, "rnn_conv_w")],
                   [(0, "conv_w_proj"), (0, "rnn_w_proj"), (0, "mix_w_out")], [(0, "ffn2_w_gu"), (0, "ffn2_w_down")],
                   [(1, n) for n in names]]
    order = [u for g in unit_groups for u in g]
    index = {u: i for i, u in enumerate(order)}
    groups = [[index[u] for u in g] for g in unit_groups]
    srcs = [W[n][l].astype(BF16) if n in SHARDED_MATS else W[n][l] for l, n in order]
    by_cols = [n in COL_SHARDED for _, n in order]
    send_sems, recv_sems, src_thru, land_thru = _gather_start(srcs, by_cols, groups, "gather_start")

    sections = ((0, D, F32), (D, D, F32), (2 * D, R, F32), (2 * D + R, R, BF16), (2 * D + 2 * R, D, BF16),
                (3 * D + 2 * R, D, BF16))
    keys = {"ffn1_w_gu": "wgu1", "ffn1_w_down": "wd1", "ffn2_w_gu": "wgu2", "ffn2_w_down": "wd2", "conv_w_proj": "wc",
            "rnn_w_proj": "wr", "mix_w_out": "wo"}
    params = []
    for l in range(depth):
        p = {"wa": _embed_blocks(rnn_w_a[l], f"l{l}_embed_wa"), "wx": _embed_blocks(rnn_w_x[l], f"l{l}_embed_wx")}
        for n in ("ln1_g", "ln1_b", "ln2_g", "ln2_b", "ln3_g", "ln3_b", "conv_dw_b", "conv_gn_g", "conv_gn_b", "rnn_conv_b",
                  "rnn_b_a", "rnn_b_x", "rnn_lambda"):
            p[n] = _row(W[n][l])
        p["bin"] = _row(mix_b_in[l])
        p["sections"] = sections
        params.append(p)

    def wait_group(g, after):
        ids = groups[g]
        landed = _gather_wait([src_thru[i] for i in ids], [land_thru[i] for i in ids], [by_cols[i] for i in ids],
                              send_sems[g], recv_sems[g], after, f"gather_wait{g}")
        for i, full in zip(ids, landed):
            l, n = order[i]
            p = params[l]
            if n not in COL_SHARDED:
                full = full.reshape((N_CHIPS * full.shape[1],) + full.shape[2:])
            if n == "mix_w_in":
                p["win"] = full
            elif n == "rnn_conv_w":
                p[n] = _unshard_cols(landed[ids.index(i)])
            elif n == "conv_dw_w":
                p[n] = full
            else:
                p[keys[n]] = (full[None], 0)

    h = x[0]
    wait_group(0, h)
    saved = []
    hooks = [{"after_ffn1": lambda v: wait_group(1, v), "after_scan": lambda v: wait_group(2, v),
              "after_mixer": lambda v: wait_group(3, v), "after_layer": lambda v: wait_group(4, v)}, {}]
    for l in range(depth):
        h, sv = _layer_forward(h, params[l], alpha, l, hooks[l])
        saved.append(sv)
    loss_part, dy = _loss_head(h, loss_target[0], "loss_head")
    loss = lax.psum(loss_part[0, 0], ("x", "y", "c"))
    def pair_sums(names, g, gb, tag):
        cols = [n in COL_SHARDED for n in names]
        theirs = _sibling_exchange([_half_view(gb[n], bc) for n, bc in zip(names, cols)], cols, "pair_exchange" + tag)
        return cols, [_pair_sum_bf16(_half_view(g[n], bc), t, bc, f"pair_sum{tag}_{n}") for n, bc, t in zip(names, cols, theirs)]

    pending = []

    def start_scatter(layer, names, g, gb, tag):
        cols, sums = pair_sums(names, g, gb, tag)
        in_flight, token = _scatter_start(sums, cols, "scatter_start" + tag)
        pending.append((layer, names, cols, in_flight, tag))
        return token

    early = [n for n in SHARDED_MATS if not n.startswith("ffn1_")]
    late = [n for n in SHARDED_MATS if n.startswith("ffn1_")]
    grads, grads_bf16 = [None] * depth, [None] * depth
    dy, grads[1], grads_bf16[1] = _layer_backward(dy, params[1], saved[1], alpha, 1)
    token = start_scatter(1, SHARDED_MATS, grads[1], grads_bf16[1], "1")
    first = dict(params[0], ln3_g=params[0]["ln3_g"] + token[0:1, 0:1])
    dy, grads[0], grads_bf16[0] = _layer_backward(dy, first, saved[0], alpha, 0,
                                                  before_ffn1=lambda g, gb: start_scatter(0, early, g, gb, "0a"))
    grad_x = dy[None]
    late_cols, late_sums = pair_sums(late, grads[0], grads_bf16[0], "0b")
    late_rem, late_sib = _scatter_grads(late_sums, late_cols, "scatter_grads0b")
    partial = {(0, n): part for n, part in zip(late, zip(late_sums, late_rem, late_sib))}
    for layer, names, cols, in_flight, tag in pending:
        for n, part in zip(names, zip(*_scatter_wait(*in_flight, cols, late_rem[0], "scatter_wait" + tag))):
            partial[(layer, n)] = part

    results = {}
    for l in (1, 0):
        for n in SHARDED_MATS:
            cs, rm, sb = partial[(l, n)]
            if n in COL_SHARDED:
                width = cs.shape[1] // N_CHIPS
                own = lax.dynamic_slice_in_dim(cs, chip * width, width, axis=1)
            else:
                own = lax.dynamic_index_in_dim(cs, chip, axis=0, keepdims=False)
            results[n] = _adamw_sharded(W[n], M[n], V[n], own, sb, rm, l, results.get(n), f"adamw{l}_{n}")
    out_g, out_d, out_m, out_v = {}, {}, {}, {}
    for n in SHARDED_MATS:
        outs = results[n]
        if n == "mix_w_in":
            outs = [jnp.transpose(o, (0, 2, 1)) for o in outs]
        out_g[n], out_d[n], out_m[n], out_v[n] = outs

    small_grads = [jnp.stack([grads[l][n].reshape(W[n].shape[1:] if n not in SHARDED_VECS else
                                                   (W[n].shape[1], W[n].shape[2] * N_CHIPS)) for l in range(depth)])
                   for n in SMALL_NAMES]
    n_small = sum(int(a.size) for a in small_grads)
    piece_rows = -(-n_small // (N_DEV * LANES * SUBLANES)) * SUBLANES
    reduced = _unpack_small(_all_reduce_small(_pack_small(small_grads, piece_rows), "all_reduce_small"),
                            [a.shape for a in small_grads])
    local_g = []
    for n, gr in zip(SMALL_NAMES, reduced):
        if n in SHARDED_VECS:
            width = W[n].shape[2]
            gr = lax.dynamic_slice_in_dim(gr, chip * width, width, axis=2)
        local_g.append(gr)
    n_local = sum(int(a.size) for a in local_g)
    flat_rows = -(-n_local // (N_DEV * LANES * SUBLANES)) * SUBLANES * N_DEV
    pack = lambda arrs: _pack_small(arrs, flat_rows // N_DEV).reshape(flat_rows, LANES)
    shapes = [a.shape for a in local_g]
    deltas, new_m, new_v = _adamw_flat(pack([W[n] for n in SMALL_NAMES]), pack([M[n] for n in SMALL_NAMES]),
                                       pack([V[n] for n in SMALL_NAMES]), pack(local_g), "adamw_small")
    for n, gr, d_, m_, v_ in zip(SMALL_NAMES, local_g, _unpack_small(deltas, shapes), _unpack_small(new_m, shapes),
                                 _unpack_small(new_v, shapes)):
        out_g[n], out_d[n], out_m[n], out_v[n] = gr, d_, m_, v_

    return (loss, grad_x, *[out_g[n] for n in WEIGHT_NAMES], *[out_d[n] for n in WEIGHT_NAMES],
            *[out_m[n] for n in WEIGHT_NAMES], *[out_v[n] for n in WEIGHT_NAMES])
```

```python
import functools

import jax
import jax.numpy as jnp
from jax import lax
from jax.experimental import pallas as pl
from jax.experimental.pallas import tpu as pltpu

F32 = jnp.float32
BF16 = jnp.bfloat16
MESH = pl.DeviceIdType.MESH

LN_EPS = 1e-5
CONV_GROUPS = 8
RNN_BLOCKS = 16
RG_LRU_C = 8.0
ADAM_LR = 0.001
ADAM_B1 = 0.9
ADAM_B2 = 0.999
ADAM_EPS = 1e-08
ADAM_WD = 0.01
ADAM_STEP = 10

LANES = 128
SUBLANES = 8
V7X_VMEM_BYTES = 64 << 20
VMEM_LIMIT_CAP = V7X_VMEM_BYTES - (6 << 20)
N_CHIPS = 4
N_DEV = 8
CONV_ROWS = 64
EW_ROWS = 1024
SCAN_SEGMENTS = 32
SCAN_UNROLL = 4


def _cparams(block_bytes):
    limit = min(VMEM_LIMIT_CAP, max(int(block_bytes) + (8 << 20), 24 << 20))
    return pltpu.CompilerParams(vmem_limit_bytes=limit)


def _nbytes(shape, dtype):
    n = 1
    for s in shape:
        n *= s
    return n * jnp.dtype(dtype).itemsize


def _divisor_tile(n, limit, quantum):
    if n <= limit:
        return n
    best = None
    for t in range(quantum, limit + 1, quantum):
        if n % t == 0:
            best = t
    assert best is not None, (n, limit, quantum)
    return best


def _bs(shape, imap, **kw):
    return pl.BlockSpec(shape, imap, **kw)


def _resident(shape):
    nd = len(shape)
    return pl.BlockSpec(shape, lambda *_: (0,) * nd, pipeline_mode=pl.Buffered(1))


def _streamed_call(body, **kw):
    call = pl.pallas_call(body, **kw)
    return lambda *operands: call(*[pltpu.with_memory_space_constraint(o, pltpu.HBM) for o in operands])


def _layer_block(w, block, imap, **kw):
    arr, layer = w
    return arr, pl.BlockSpec((None,) + block, lambda *ids: (layer,) + imap(*ids), **kw)


def _layer_resident(w):
    arr, _ = w
    return _layer_block(w, arr.shape[1:], lambda *_: (0, 0), pipeline_mode=pl.Buffered(1))


def _sigmoid(x):
    return jax.nn.sigmoid(x)


def _dot(a, b):
    return jnp.dot(a, b, preferred_element_type=F32)


def _dot_nt(a, b):
    return lax.dot_general(a, b, (((1,), (1,)), ((), ())), preferred_element_type=F32)


def _dot_tn(a, b):
    return lax.dot_general(a, b, (((0,), (0,)), ((), ())), preferred_element_type=F32)


def _row_mean(z):
    return jnp.mean(z, axis=-1, keepdims=True)


def _lane_mean(z):
    hi = z.astype(BF16)
    lo = (z - hi.astype(F32)).astype(BF16)
    ones = jnp.full((2 * LANES, LANES), 1.0 / LANES, BF16)
    return jnp.dot(jnp.concatenate([hi, lo], axis=-1), ones, preferred_element_type=F32)


def _norm_fwd(z, g, b, mean=_row_mean):
    mu = mean(z)
    xc = z - mu
    var = mean(xc * xc)
    return xc * lax.rsqrt(var + LN_EPS) * g + b


def _norm_bwd(z, g, dy, mean=_row_mean):
    mu = mean(z)
    xc = z - mu
    var = mean(xc * xc)
    rstd = lax.rsqrt(var + LN_EPS)
    xhat = xc * rstd
    dxh = dy * g
    m1 = mean(dxh)
    m2 = mean(dxh * xhat)
    return rstd * (dxh - m1 - xhat * m2), xhat


GELU_K = 0.7978845608028654
GELU_C = 0.044715


def _gelu(x):
    return 0.5 * x * (1.0 + jnp.tanh(GELU_K * (x + GELU_C * x * x * x)))


def _gelu_grad(x):
    t = jnp.tanh(GELU_K * (x + GELU_C * x * x * x))
    return 0.5 * (1.0 + t) + 0.5 * x * (1.0 - t * t) * GELU_K * (1.0 + 3.0 * GELU_C * x * x)


def _softplus(y):
    return jnp.maximum(y, 0.0) + jnp.log1p(jnp.exp(-jnp.abs(y)))


def _neg_expm1(y):
    series = -y * (1.0 + y * (0.5 + y * (1.0 / 6.0 + y * (1.0 / 24.0 + y * (1.0 / 120.0 + y * (1.0 / 720.0))))))
    return jnp.where(y > -0.25, series, 1.0 - jnp.exp(y))


def _colsum(x):
    return jnp.sum(x, axis=0, keepdims=True)


def _shifted_taps(src_ref, base, rows, taps):
    acc = None
    for o, coef in taps:
        term = coef() * src_ref[pl.ds(base + o, rows), :]
        acc = term if acc is None else acc + term
    return acc


def _shifted_corr(src_ref, base, rows, d, acc_ref, offs):
    for k, o in enumerate(offs):
        prod = d * src_ref[pl.ds(base + o, rows), :]
        part = jnp.sum(prod.reshape(rows // SUBLANES, SUBLANES, prod.shape[-1]), axis=0)
        acc_ref[SUBLANES * k:SUBLANES * (k + 1), :] += part


def _front_pad(ktaps):
    return SUBLANES * ((ktaps - 1 + SUBLANES - 1) // SUBLANES)


def _pad_rows(ktaps):
    return _front_pad(ktaps) + SUBLANES


def _ffn_tiles(S, F):
    tm = _divisor_tile(S, 1024, 16)
    tf = _divisor_tile(F, 256, LANES)
    return tm, tf


def _ffn_fwd(x, wgu, wd, g, b, alpha, name):
    S, D = x.shape
    F = wd[0].shape[1]
    tm, tf = _ffn_tiles(S, F)
    nf = F // tf
    wg_arr, wg_spec = _layer_block(wgu, (D, tf), lambda i, j: (0, j))
    wu_arr, wu_spec = _layer_block(wgu, (D, tf), lambda i, j: (0, nf + j))
    wd_arr, wd_spec = _layer_block(wd, (tf, D), lambda i, j: (j, 0))

    def body(x_ref, wg_ref, wu_ref, wd_ref, g_ref, b_ref, y_ref, z_ref, hg_ref, hu_ref, acc_ref, xb_ref):
        j = pl.program_id(1)

        @pl.when(j == 0)
        def _():
            xb_ref[...] = x_ref[...].astype(BF16)
            acc_ref[...] = jnp.zeros_like(acc_ref)

        xb = xb_ref[...]
        hg = _dot(xb, wg_ref[...])
        hu = _dot(xb, wu_ref[...])
        hg_ref[...] = hg
        hu_ref[...] = hu
        a = (hg * _sigmoid(hg) * hu).astype(BF16)
        acc_ref[...] += _dot(a, wd_ref[...])

        @pl.when(j == nf - 1)
        def _():
            z = alpha * x_ref[...] + 0.5 * acc_ref[...]
            z_ref[...] = z
            y_ref[...] = _norm_fwd(z, g_ref[...], b_ref[...])

    blk = 2 * (3 * tm * D * 4 + 2 * tm * tf * 4 + 3 * D * tf * 2) + tm * D * 6 + 3 * tm * tf * 4
    return _streamed_call(
        body, name=name, grid=(S // tm, nf),
        in_specs=[_bs((tm, D), lambda i, j: (i, 0)), wg_spec, wu_spec, wd_spec,
                  _bs((1, D), lambda i, j: (0, 0)), _bs((1, D), lambda i, j: (0, 0))],
        out_specs=[_bs((tm, D), lambda i, j: (i, 0)), _bs((tm, D), lambda i, j: (i, 0)),
                   _bs((tm, tf), lambda i, j: (i, j)), _bs((tm, tf), lambda i, j: (i, j))],
        out_shape=[jax.ShapeDtypeStruct((S, D), F32), jax.ShapeDtypeStruct((S, D), F32),
                   jax.ShapeDtypeStruct((S, F), F32), jax.ShapeDtypeStruct((S, F), F32)],
        scratch_shapes=[pltpu.VMEM((tm, D), F32), pltpu.VMEM((tm, D), BF16)],
        compiler_params=_cparams(blk),
    )(x, wg_arr, wu_arr, wd_arr, g, b)


def _ffn_bwd(dy, z, hg, hu, wgu, wd, g, alpha, name):
    S, D = dy.shape
    F = wd[0].shape[1]
    tm, tf = _ffn_tiles(S, F)
    nf = F // tf
    wg_arr, wg_spec = _layer_block(wgu, (D, tf), lambda i, j: (0, j))
    wu_arr, wu_spec = _layer_block(wgu, (D, tf), lambda i, j: (0, nf + j))
    wd_arr, wd_spec = _layer_block(wd, (tf, D), lambda i, j: (j, 0))

    def body(dy_ref, z_ref, hg_ref, hu_ref, wg_ref, wu_ref, wd_ref, g_ref,
             dx_ref, df_ref, a_ref, dhg_ref, dhu_ref, dg_ref, db_ref, acc_ref):
        i = pl.program_id(0)
        j = pl.program_id(1)

        @pl.when((i == 0) & (j == 0))
        def _():
            dg_ref[...] = jnp.zeros_like(dg_ref)
            db_ref[...] = jnp.zeros_like(db_ref)

        @pl.when(j == 0)
        def _():
            dy_ = dy_ref[...]
            dz, xhat = _norm_bwd(z_ref[...], g_ref[...], dy_)
            dg_ref[...] += _colsum(dy_ * xhat)
            db_ref[...] += _colsum(dy_)
            acc_ref[...] = alpha * dz
            df_ref[...] = (0.5 * dz).astype(BF16)

        da = _dot_nt(df_ref[...], wd_ref[...])
        hg_ = hg_ref[...]
        hu_ = hu_ref[...]
        s = _sigmoid(hg_)
        sl = hg_ * s
        dgate = (da * hu_ * (s * (1.0 + hg_ * (1.0 - s)))).astype(BF16)
        dup = (da * sl).astype(BF16)
        a_ref[...] = (sl * hu_).astype(BF16)
        dhg_ref[...] = dgate
        dhu_ref[...] = dup
        acc_ref[...] += _dot_nt(dgate, wg_ref[...]) + _dot_nt(dup, wu_ref[...])

        @pl.when(j == nf - 1)
        def _():
            dx_ref[...] = acc_ref[...]

    blk = 2 * (2 * tm * D * 4 + tm * D * 2 + 2 * tm * tf * 4 + 3 * tm * tf * 2 + 3 * D * tf * 2) + 3 * tm * D * 4 + 8 * tm * tf * 4
    once = dict(pipeline_mode=pl.Buffered(1))
    return _streamed_call(
        body, name=name, grid=(S // tm, nf),
        in_specs=[_bs((tm, D), lambda i, j: (i, 0), **once), _bs((tm, D), lambda i, j: (i, 0), **once),
                  _bs((tm, tf), lambda i, j: (i, j)), _bs((tm, tf), lambda i, j: (i, j)),
                  wg_spec, wu_spec, wd_spec, _bs((1, D), lambda i, j: (0, 0))],
        out_specs=[_bs((tm, D), lambda i, j: (i, 0)), _bs((tm, D), lambda i, j: (i, 0)),
                   _bs((tm, tf), lambda i, j: (i, j)), _bs((tm, tf), lambda i, j: (i, j)), _bs((tm, tf), lambda i, j: (i, j)),
                   _bs((1, D), lambda i, j: (0, 0)), _bs((1, D), lambda i, j: (0, 0))],
        out_shape=[jax.ShapeDtypeStruct((S, D), F32), jax.ShapeDtypeStruct((S, D), BF16),
                   jax.ShapeDtypeStruct((S, F), BF16), jax.ShapeDtypeStruct((S, F), BF16), jax.ShapeDtypeStruct((S, F), BF16),
                   jax.ShapeDtypeStruct((1, D), F32), jax.ShapeDtypeStruct((1, D), F32)],
        scratch_shapes=[pltpu.VMEM((tm, D), F32)],
        compiler_params=_cparams(blk),
    )(dy, z, hg, hu, wg_arr, wu_arr, wd_arr, g)


def _mm_tn(a, b, name):
    S, M = a.shape
    N = b.shape[1]
    bm = _divisor_tile(M, 1408, LANES)
    bn = _divisor_tile(N, 1408, LANES)
    tk = _divisor_tile(S, 512, 16)
    nk = S // tk

    def body(a_ref, b_ref, o_ref, ob_ref):
        k = pl.program_id(2)

        @pl.when(k == 0)
        def _():
            o_ref[...] = jnp.zeros_like(o_ref)

        o_ref[...] += _dot_tn(a_ref[...].astype(BF16), b_ref[...].astype(BF16))

        @pl.when(k == nk - 1)
        def _():
            ob_ref[...] = o_ref[...].astype(BF16)

    blk = 2 * (tk * bm * a.dtype.itemsize + tk * bn * b.dtype.itemsize + bm * bn * 6) + tk * bm * 4 + bm * bn * 4
    tile = _bs((bm, bn), lambda i, j, k: (i, j))
    return _streamed_call(
        body, name=name, grid=(M // bm, N // bn, nk),
        in_specs=[_bs((tk, bm), lambda i, j, k: (k, i)), _bs((tk, bn), lambda i, j, k: (k, j))],
        out_specs=[tile, tile],
        out_shape=[jax.ShapeDtypeStruct((M, N), F32), jax.ShapeDtypeStruct((M, N), BF16)],
        compiler_params=_cparams(blk),
    )(a, b)


def _mm_tn_pair(a, b0, b1, name):
    S, M = a.shape
    N = b0.shape[1]
    assert b1.shape == b0.shape
    bm = _divisor_tile(M, 1408, LANES)
    bn = _divisor_tile(N, 1408, LANES)
    tk = _divisor_tile(S, 512, 16)
    nb = N // bn
    nk = S // tk

    def body(a_ref, b0_ref, b1_ref, o_ref, ob_ref):
        j = pl.program_id(1)
        k = pl.program_id(2)

        @pl.when(k == 0)
        def _():
            o_ref[...] = jnp.zeros_like(o_ref)

        ab = a_ref[...].astype(BF16)

        @pl.when(j < nb)
        def _():
            o_ref[...] += _dot_tn(ab, b0_ref[...])

        @pl.when(j >= nb)
        def _():
            o_ref[...] += _dot_tn(ab, b1_ref[...])

        @pl.when(k == nk - 1)
        def _():
            ob_ref[...] = o_ref[...].astype(BF16)

    b0_map = lambda i, j, k: (jnp.where(j < nb, k, nk - 1), jnp.minimum(j, nb - 1))
    b1_map = lambda i, j, k: (jnp.where(j >= nb, k, 0), jnp.maximum(j - nb, 0))
    blk = 2 * (tk * bm * a.dtype.itemsize + 2 * tk * bn * 2 + bm * bn * 6) + tk * bm * 4 + bm * bn * 4
    tile = _bs((bm, bn), lambda i, j, k: (i, j))
    return _streamed_call(
        body, name=name, grid=(M // bm, 2 * nb, nk),
        in_specs=[_bs((tk, bm), lambda i, j, k: (k, i)), _bs((tk, bn), b0_map), _bs((tk, bn), b1_map)],
        out_specs=[tile, tile],
        out_shape=[jax.ShapeDtypeStruct((M, 2 * N), F32), jax.ShapeDtypeStruct((M, 2 * N), BF16)],
        compiler_params=_cparams(blk),
    )(a, b0, b1)


def _mix_in(x, wt, bias, sections, name):
    S, D = x.shape
    tm = _divisor_tile(S, 256, 16)
    n = len(sections)

    def body(x_ref, w_ref, b_ref, *o_refs):
        xb = x_ref[...].astype(BF16)
        for (off, width, dtype), o_ref in zip(sections, o_refs):
            o_ref[...] = (_dot_nt(xb, w_ref[off:off + width, :]) + b_ref[:, off:off + width]).astype(dtype)

    total = wt.shape[0]
    blk = 2 * (tm * D * 4 + sum(tm * w * jnp.dtype(dt).itemsize for _, w, dt in sections)) + total * D * 2 + 3 * tm * D * 4
    return _streamed_call(
        body, name=name, grid=(S // tm,),
        in_specs=[_bs((tm, D), lambda i: (i, 0)), _resident((total, D)), _resident((1, total))],
        out_specs=[_bs((tm, w), lambda i: (i, 0)) for _, w, _ in sections],
        out_shape=[jax.ShapeDtypeStruct((S, w), dt) for _, w, dt in sections],
        compiler_params=_cparams(blk),
    )(x, wt, bias)


def _mix_dx(dz, parts, wt, sections, alpha, name):
    S, D = dz.shape
    tm = _divisor_tile(S, 256, 16)
    n = len(parts)

    def body(*refs):
        dz_ref = refs[0]
        p_refs = refs[1:1 + n]
        w_ref = refs[1 + n]
        o_ref = refs[2 + n]
        acc = alpha * dz_ref[...]
        for p_ref, (off, width, _) in zip(p_refs, sections):
            acc = acc + _dot(p_ref[...], w_ref[off:off + width, :])
        o_ref[...] = acc

    widths = [p.shape[1] for p in parts]
    total = wt.shape[0]
    blk = 2 * (2 * tm * D * 4 + sum(tm * w * 2 for w in widths)) + total * D * 2 + 2 * tm * D * 4
    return _streamed_call(
        body, name=name, grid=(S // tm,),
        in_specs=[_bs((tm, D), lambda i: (i, 0))] + [_bs((tm, w), lambda i: (i, 0)) for w in widths]
                 + [_resident((total, D))],
        out_specs=_bs((tm, D), lambda i: (i, 0)),
        out_shape=jax.ShapeDtypeStruct((S, D), F32),
        compiler_params=_cparams(blk),
    )(dz, *parts, wt)


def _conv_branch_fwd(cv, cg, w, b, gg, gb, name):
    S, C = cv.shape
    K = w.shape[0]
    assert C // CONV_GROUPS == LANES
    padf = _front_pad(K)
    R = min(CONV_ROWS, S)
    E = min(EW_ROWS, S)

    def body(cv_ref, cg_ref, w_ref, b_ref, gg_ref, gb_ref, c2_ref, c4_ref, pad_ref):
        pad_ref[0:padf, :] = jnp.zeros((padf, LANES), F32)
        pad_ref[S + padf:S + padf + SUBLANES, :] = jnp.zeros((SUBLANES, LANES), F32)

        def fill(i, carry):
            r = pl.multiple_of(i * E, E)
            pad_ref[pl.ds(r + padf, E), :] = cv_ref[pl.ds(r, E), :] * _sigmoid(cg_ref[pl.ds(r, E), :])
            return carry

        lax.fori_loop(0, S // E, fill, 0)
        taps = [(padf - (K - 1) + k, functools.partial(lambda k: w_ref[k:k + 1, :], k)) for k in range(K)]

        def conv(i, carry):
            r = pl.multiple_of(i * R, R)
            c2_ref[pl.ds(r, R), :] = _shifted_taps(pad_ref, r, R, taps) + b_ref[...]
            return carry

        lax.fori_loop(0, S // R, conv, 0)

        def norm(i, carry):
            r = pl.multiple_of(i * E, E)
            c3 = _norm_fwd(c2_ref[pl.ds(r, E), :], gg_ref[...], gb_ref[...], _lane_mean)
            c4_ref[pl.ds(r, E), :] = (c3 * _sigmoid(c3)).astype(BF16)
            return carry

        lax.fori_loop(0, S // E, norm, 0)

    col = lambda i: (0, i)
    blk = 2 * (3 * S * LANES * 4 + S * LANES * 2) + (S + _pad_rows(K)) * LANES * 4
    return _streamed_call(
        body, name=name, grid=(C // LANES,),
        in_specs=[_bs((S, LANES), col), _bs((S, LANES), col), _bs((K, LANES), col),
                  _bs((1, LANES), col), _bs((1, LANES), col), _bs((1, LANES), col)],
        out_specs=[_bs((S, LANES), col), _bs((S, LANES), col)],
        out_shape=[jax.ShapeDtypeStruct((S, C), F32), jax.ShapeDtypeStruct((S, C), BF16)],
        scratch_shapes=[pltpu.VMEM((S + _pad_rows(K), LANES), F32)],
        compiler_params=_cparams(blk),
    )(cv, cg, w, b, gg, gb)


def _conv_branch_bwd(dc4, c2, cv, cg, w, gg, gb, name):
    S, C = cv.shape
    K = w.shape[0]
    padf = _front_pad(K)
    R = min(CONV_ROWS, S)
    E = min(EW_ROWS, S)

    def body(dc4_ref, c2_ref, cv_ref, cg_ref, w_ref, gg_ref, gb_ref,
             dcv_ref, dcg_ref, dw_ref, dwb_ref, dgg_ref, dgb_ref, scv_ref, scg_ref,
             dpad_ref, cpad_ref, dwacc_ref):
        cpad_ref[0:padf, :] = jnp.zeros((padf, LANES), F32)
        cpad_ref[S + padf:S + padf + SUBLANES, :] = jnp.zeros((SUBLANES, LANES), F32)
        dpad_ref[S:S + padf + SUBLANES, :] = jnp.zeros((padf + SUBLANES, LANES), F32)
        dwacc_ref[...] = jnp.zeros_like(dwacc_ref)
        for ref in (dwb_ref, dgg_ref, dgb_ref, scv_ref, scg_ref):
            ref[...] = jnp.zeros_like(ref)

        def norm_pass(i, carry):
            r = pl.multiple_of(i * E, E)
            g_ = gg_ref[...]
            c2 = c2_ref[pl.ds(r, E), :]
            xc = c2 - _lane_mean(c2)
            rstd = lax.rsqrt(_lane_mean(xc * xc) + LN_EPS)
            xhat = xc * rstd
            c3 = xhat * g_ + gb_ref[...]
            s = _sigmoid(c3)
            dc3 = dc4_ref[pl.ds(r, E), :].astype(F32) * (s * (1.0 + c3 * (1.0 - s)))
            dgg_ref[...] += _colsum(dc3 * xhat)
            dgb_ref[...] += _colsum(dc3)
            dxh = dc3 * g_
            dc2 = rstd * (dxh - _lane_mean(dxh) - xhat * _lane_mean(dxh * xhat))
            dpad_ref[pl.ds(r, E), :] = dc2
            dwb_ref[...] += _colsum(dc2)
            cpad_ref[pl.ds(r + padf, E), :] = cv_ref[pl.ds(r, E), :] * _sigmoid(cg_ref[pl.ds(r, E), :])
            return carry

        lax.fori_loop(0, S // E, norm_pass, 0)
        taps = [(K - 1 - k, functools.partial(lambda k: w_ref[k:k + 1, :], k)) for k in range(K)]
        offs = [padf - (K - 1) + k for k in range(K)]

        def conv_pass(i, carry):
            r = pl.multiple_of(i * R, R)
            dc1 = _shifted_taps(dpad_ref, r, R, taps)
            sg = _sigmoid(cg_ref[pl.ds(r, R), :])
            cv_ = cv_ref[pl.ds(r, R), :]
            dcv = dc1 * sg
            dcg = dc1 * cv_ * sg * (1.0 - sg)
            dcv_ref[pl.ds(r, R), :] = dcv.astype(BF16)
            dcg_ref[pl.ds(r, R), :] = dcg.astype(BF16)
            scv_ref[...] += _colsum(dcv)
            scg_ref[...] += _colsum(dcg)
            _shifted_corr(cpad_ref, r, R, dpad_ref[pl.ds(r, R), :], dwacc_ref, offs)
            return carry

        lax.fori_loop(0, S // R, conv_pass, 0)
        for k in range(K):
            dw_ref[k:k + 1, :] = _colsum(dwacc_ref[SUBLANES * k:SUBLANES * (k + 1), :])

    col = lambda i: (0, i)
    row = jax.ShapeDtypeStruct((1, C), F32)
    blk = 2 * (4 * S * LANES * 4 + 2 * S * LANES * 2) + 2 * (S + _pad_rows(K)) * LANES * 4
    return _streamed_call(
        body, name=name, grid=(C // LANES,),
        in_specs=[_bs((S, LANES), col)] * 4 + [_bs((K, LANES), col), _bs((1, LANES), col), _bs((1, LANES), col)],
        out_specs=[_bs((S, LANES), col), _bs((S, LANES), col), _bs((K, LANES), col)] + [_bs((1, LANES), col)] * 5,
        out_shape=[jax.ShapeDtypeStruct((S, C), BF16), jax.ShapeDtypeStruct((S, C), BF16),
                   jax.ShapeDtypeStruct((K, C), F32), row, row, row, row, row],
        scratch_shapes=[pltpu.VMEM((S + _pad_rows(K), LANES), F32), pltpu.VMEM((S + _pad_rows(K), LANES), F32),
                        pltpu.VMEM((SUBLANES * K, LANES), F32)],
        compiler_params=_cparams(blk),
    )(dc4, c2, cv, cg, w, gg, gb)


def _short_conv_fwd(xin, w, b, name):
    S, C = xin.shape
    K = w.shape[0]
    padf = _front_pad(K)
    R = min(CONV_ROWS, S)
    E = min(EW_ROWS, S)

    def body(x_ref, w_ref, b_ref, o_ref, pad_ref):
        pad_ref[0:padf, :] = jnp.zeros((padf, LANES), F32)
        pad_ref[S + padf:S + padf + SUBLANES, :] = jnp.zeros((SUBLANES, LANES), F32)

        def fill(i, carry):
            r = pl.multiple_of(i * E, E)
            pad_ref[pl.ds(r + padf, E), :] = x_ref[pl.ds(r, E), :]
            return carry

        lax.fori_loop(0, S // E, fill, 0)
        taps = [(padf - (K - 1) + k, functools.partial(lambda k: w_ref[k:k + 1, :], k)) for k in range(K)]

        def conv(i, carry):
            r = pl.multiple_of(i * R, R)
            o_ref[pl.ds(r, R), :] = _shifted_taps(pad_ref, r, R, taps) + b_ref[...]
            return carry

        lax.fori_loop(0, S // R, conv, 0)

    col = lambda i: (0, i)
    blk = 2 * (2 * S * LANES * 4) + (S + _pad_rows(K)) * LANES * 4
    return _streamed_call(
        body, name=name, grid=(C // LANES,),
        in_specs=[_bs((S, LANES), col), _bs((K, LANES), col), _bs((1, LANES), col)],
        out_specs=_bs((S, LANES), col),
        out_shape=jax.ShapeDtypeStruct((S, C), F32),
        scratch_shapes=[pltpu.VMEM((S + _pad_rows(K), LANES), F32)],
        compiler_params=_cparams(blk),
    )(xin, w, b)


def _short_conv_bwd(dy, xin, w, name):
    S, C = xin.shape
    K = w.shape[0]
    padf = _front_pad(K)
    R = min(CONV_ROWS, S)
    E = min(EW_ROWS, S)

    def body(dy_ref, x_ref, w_ref, dx_ref, dw_ref, db_ref, sx_ref, dpad_ref, xpad_ref, dwacc_ref):
        xpad_ref[0:padf, :] = jnp.zeros((padf, LANES), F32)
        xpad_ref[S + padf:S + padf + SUBLANES, :] = jnp.zeros((SUBLANES, LANES), F32)
        dpad_ref[S:S + padf + SUBLANES, :] = jnp.zeros((padf + SUBLANES, LANES), F32)
        dwacc_ref[...] = jnp.zeros_like(dwacc_ref)
        db_ref[...] = jnp.zeros_like(db_ref)
        sx_ref[...] = jnp.zeros_like(sx_ref)

        def fill(i, carry):
            r = pl.multiple_of(i * E, E)
            d = dy_ref[pl.ds(r, E), :]
            dpad_ref[pl.ds(r, E), :] = d
            db_ref[...] += _colsum(d)
            xpad_ref[pl.ds(r + padf, E), :] = x_ref[pl.ds(r, E), :]
            return carry

        lax.fori_loop(0, S // E, fill, 0)
        taps = [(K - 1 - k, functools.partial(lambda k: w_ref[k:k + 1, :], k)) for k in range(K)]
        offs = [padf - (K - 1) + k for k in range(K)]

        def conv_pass(i, carry):
            r = pl.multiple_of(i * R, R)
            dx = _shifted_taps(dpad_ref, r, R, taps)
            dx_ref[pl.ds(r, R), :] = dx.astype(BF16)
            sx_ref[...] += _colsum(dx)
            _shifted_corr(xpad_ref, r, R, dpad_ref[pl.ds(r, R), :], dwacc_ref, offs)
            return carry

        lax.fori_loop(0, S // R, conv_pass, 0)
        for k in range(K):
            dw_ref[k:k + 1, :] = _colsum(dwacc_ref[SUBLANES * k:SUBLANES * (k + 1), :])

    col = lambda i: (0, i)
    row = jax.ShapeDtypeStruct((1, C), F32)
    blk = 2 * (2 * S * LANES * 4 + S * LANES * 2) + 2 * (S + _pad_rows(K)) * LANES * 4
    return _streamed_call(
        body, name=name, grid=(C // LANES,),
        in_specs=[_bs((S, LANES), col), _bs((S, LANES), col), _bs((K, LANES), col)],
        out_specs=[_bs((S, LANES), col), _bs((K, LANES), col), _bs((1, LANES), col), _bs((1, LANES), col)],
        out_shape=[jax.ShapeDtypeStruct((S, C), BF16), jax.ShapeDtypeStruct((K, C), F32), row, row],
        scratch_shapes=[pltpu.VMEM((S + _pad_rows(K), LANES), F32), pltpu.VMEM((S + _pad_rows(K), LANES), F32),
                        pltpu.VMEM((SUBLANES * K, LANES), F32)],
        compiler_params=_cparams(blk),
    )(dy, xin, w)


def _band_panels(width, block):
    assert width % LANES == 0 and block <= LANES
    panels = []
    for c0 in range(0, width, 2 * LANES):
        c1 = min(width, c0 + 2 * LANES)
        r0 = (c0 // block) * block // LANES * LANES
        r1 = min(width, -(-(-(-c1 // block) * block) // LANES) * LANES)
        panels.append((r0, r1, c0, c1))
    return panels


def _gates_fwd(r1, wa, wx, ba, bx, lam, name):
    S, R = r1.shape
    tm = _divisor_tile(S, 256, 16)
    panels = _band_panels(R, R // RNN_BLOCKS)

    def body(r1_ref, wa_ref, wx_ref, ba_ref, bx_ref, lam_ref, ra_ref, ri_ref, a_ref, uu_ref):
        for r0, r1e, c0, c1 in panels:
            rb = r1_ref[:, r0:r1e].astype(BF16)
            ra = _sigmoid(_dot(rb, wa_ref[r0:r1e, c0:c1]) + ba_ref[:, c0:c1])
            ri = _sigmoid(_dot(rb, wx_ref[r0:r1e, c0:c1]) + bx_ref[:, c0:c1])
            log_a = -RG_LRU_C * ra * _softplus(-lam_ref[:, c0:c1])
            ra_ref[:, c0:c1] = ra
            ri_ref[:, c0:c1] = ri
            a_ref[:, c0:c1] = jnp.exp(log_a)
            uu_ref[:, c0:c1] = jnp.sqrt(_neg_expm1(2.0 * log_a)) * (ri * r1_ref[:, c0:c1])

    blk = 2 * (5 * tm * R * 4) + 2 * R * R * 2 + 6 * tm * R * 4
    tile = _bs((tm, R), lambda i: (i, 0))
    return _streamed_call(
        body, name=name, grid=(S // tm,),
        in_specs=[tile, _resident((R, R)), _resident((R, R)), _resident((1, R)), _resident((1, R)), _resident((1, R))],
        out_specs=[tile] * 4,
        out_shape=[jax.ShapeDtypeStruct((S, R), F32)] * 4,
        compiler_params=_cparams(blk),
    )(r1, wa, wx, ba, bx, lam)


def _gates_bwd(guu, da, ra, ri, r1, wa, wx, lam, name):
    S, R = r1.shape
    tm = _divisor_tile(S, 256, 16)
    nsteps = S // tm
    panels = _band_panels(R, R // RNN_BLOCKS)

    def body(g_ref, da_ref, ra_ref, ri_ref, r1_ref, wa_ref, wx_ref, lam_ref,
             dr1_ref, dpa_ref, dpx_ref, dba_ref, dbx_ref, dlam_ref):
        i = pl.program_id(0)

        @pl.when(i == 0)
        def _():
            dba_ref[...] = jnp.zeros_like(dba_ref)
            dbx_ref[...] = jnp.zeros_like(dbx_ref)
            dlam_ref[...] = jnp.zeros_like(dlam_ref)

        g = g_ref[...]
        ra = ra_ref[...]
        ri = ri_ref[...]
        r1_ = r1_ref[...]
        sp = _softplus(-lam_ref[...])
        log_a = -RG_LRU_C * ra * sp
        a = jnp.exp(log_a)
        mult = jnp.sqrt(_neg_expm1(2.0 * log_a))
        d_ri = g * mult * r1_
        dr1 = g * mult * ri
        dmult = g * ri * r1_
        dlog_a = da_ref[...] * a - dmult * (a * a) / mult
        dra = dlog_a * (-RG_LRU_C * sp)
        dlam_ref[...] += _colsum(dlog_a * (-RG_LRU_C * ra))
        dpa = dra * ra * (1.0 - ra)
        dpx = d_ri * ri * (1.0 - ri)
        dba_ref[...] += _colsum(dpa)
        dbx_ref[...] += _colsum(dpx)
        dpa_b = dpa.astype(BF16)
        dpx_b = dpx.astype(BF16)
        dpa_ref[...] = dpa_b
        dpx_ref[...] = dpx_b
        dr1_ref[...] = dr1
        for k0, k1, c0, c1 in panels:
            dr1_ref[:, c0:c1] += (_dot_nt(dpa_ref[:, k0:k1], wa_ref[c0:c1, k0:k1])
                                  + _dot_nt(dpx_ref[:, k0:k1], wx_ref[c0:c1, k0:k1]))

        @pl.when(i == nsteps - 1)
        def _():
            dlam_ref[...] = dlam_ref[...] * (-_sigmoid(-lam_ref[...]))

    blk = 2 * (6 * tm * R * 4 + 2 * tm * R * 2) + 2 * R * R * 2 + 10 * tm * R * 4
    tile = _bs((tm, R), lambda i: (i, 0))
    rowspec = _bs((1, R), lambda i: (0, 0))
    row = jax.ShapeDtypeStruct((1, R), F32)
    return _streamed_call(
        body, name=name, grid=(nsteps,),
        in_specs=[tile] * 5 + [_resident((R, R)), _resident((R, R)), _resident((1, R))],
        out_specs=[tile, tile, tile, rowspec, rowspec, rowspec],
        out_shape=[jax.ShapeDtypeStruct((S, R), F32), jax.ShapeDtypeStruct((S, R), BF16), jax.ShapeDtypeStruct((S, R), BF16),
                   row, row, row],
        compiler_params=_cparams(blk),
    )(guu, da, ra, ri, r1, wa, wx, lam)


def _embed_blocks(w, name):
    H, bk, _ = w.shape

    def body(w_ref, o_ref):
        o_ref[...] = jnp.zeros_like(o_ref)
        for h in range(H):
            o_ref[bk * h:bk * (h + 1), bk * h:bk * (h + 1)] = w_ref[h].astype(BF16)

    return pl.pallas_call(body, name=name, out_shape=jax.ShapeDtypeStruct((H * bk, H * bk), BF16),
                          compiler_params=_cparams(3 * H * bk * H * bk * 2))(w)


def _block_grads(r1, dpa, dpx, name):
    S, R = r1.shape
    bk = R // RNN_BLOCKS
    tk = _divisor_tile(S, 512, 16)
    nsteps = S // tk
    panels = _band_panels(R, bk)

    def body(r1_ref, dpa_ref, dpx_ref, ga_ref, gx_ref, acca_ref, accx_ref):
        k = pl.program_id(0)

        @pl.when(k == 0)
        def _():
            acca_ref[...] = jnp.zeros_like(acca_ref)
            accx_ref[...] = jnp.zeros_like(accx_ref)

        for k0, k1, c0, c1 in panels:
            rb = r1_ref[:, k0:k1].astype(BF16)
            acca_ref[k0:k1, c0:c1] += _dot_tn(rb, dpa_ref[:, c0:c1])
            accx_ref[k0:k1, c0:c1] += _dot_tn(rb, dpx_ref[:, c0:c1])

        @pl.when(k == nsteps - 1)
        def _():
            for h in range(RNN_BLOCKS):
                ga_ref[h] = acca_ref[bk * h:bk * (h + 1), bk * h:bk * (h + 1)]
                gx_ref[h] = accx_ref[bk * h:bk * (h + 1), bk * h:bk * (h + 1)]

    tile = lambda: _bs((tk, R), lambda k: (k, 0))
    out = _bs((RNN_BLOCKS, bk, bk), lambda k: (0, 0, 0))
    sds = jax.ShapeDtypeStruct((RNN_BLOCKS, bk, bk), F32)
    return _streamed_call(
        body, name=name, grid=(nsteps,),
        in_specs=[tile(), tile(), tile()], out_specs=[out, out], out_shape=[sds, sds],
        scratch_shapes=[pltpu.VMEM((R, R), F32), pltpu.VMEM((R, R), F32)],
        compiler_params=_cparams(2 * (tk * R * 8) + 2 * R * R * 4 + 4 * tk * R * 4),
    )(r1, dpa, dpx)


def _scan_geometry(S):
    nseg = SCAN_SEGMENTS if S % (SCAN_SEGMENTS * SUBLANES) == 0 else SUBLANES
    return nseg, S // nseg


def _steps(n, step, init):
    u = SCAN_UNROLL

    def trip(t, carry):
        for k in range(u):
            carry = step(t * u + k, carry)
        return carry

    carry = lax.fori_loop(0, n // u, trip, init)
    for j in range(n - n % u, n):
        carry = step(j, carry)
    return carry


def _scan_fwd(a, u, name):
    S, C = a.shape
    nseg, L = _scan_geometry(S)
    T = min(SUBLANES, L)

    def body(a3, u3, h3, ta_ref, tu_ref, e_ref, p_ref, init_ref):

        def to_steps(i, carry):
            j0 = pl.multiple_of(i * T, T)
            ta_ref[pl.ds(j0, T)] = jnp.swapaxes(a3[:, pl.ds(j0, T), :], 0, 1)
            tu_ref[pl.ds(j0, T)] = jnp.swapaxes(u3[:, pl.ds(j0, T), :], 0, 1)
            return carry

        lax.fori_loop(0, L // T, to_steps, 0)

        def run1(j, carry):
            hs, ps = carry
            aj = ta_ref[j]
            return aj * hs + tu_ref[j], aj * ps

        e_ref[...], p_ref[...] = _steps(L, run1, (jnp.zeros((nseg, LANES), F32), jnp.ones((nseg, LANES), F32)))
        init_ref[0:1, :] = jnp.zeros((1, LANES), F32)
        for s in range(1, nseg):
            init_ref[s:s + 1, :] = e_ref[s - 1:s, :] + p_ref[s - 1:s, :] * init_ref[s - 1:s, :]

        def run2(j, hs):
            hs = ta_ref[j] * hs + tu_ref[j]
            tu_ref[j] = hs
            return hs

        _steps(L, run2, init_ref[...])

        def from_steps(i, carry):
            j0 = pl.multiple_of(i * T, T)
            h3[:, pl.ds(j0, T), :] = jnp.swapaxes(tu_ref[pl.ds(j0, T)], 0, 1)
            return carry

        lax.fori_loop(0, L // T, from_steps, 0)

    seg_block = _bs((nseg, L, LANES), lambda i: (0, 0, i))
    blk = 2 * (3 * S * LANES * 4) + 2 * S * LANES * 4
    return _streamed_call(
        body, name=name, grid=(C // LANES,),
        in_specs=[seg_block, seg_block],
        out_specs=seg_block,
        out_shape=jax.ShapeDtypeStruct((nseg, L, C), F32),
        scratch_shapes=[pltpu.VMEM((L, nseg, LANES), F32)] * 2 + [pltpu.VMEM((nseg, LANES), F32)] * 3,
        compiler_params=_cparams(blk),
    )(a.reshape(nseg, L, C), u.reshape(nseg, L, C)).reshape(S, C)


def _scan_bwd(a, dh, h, name):
    S, C = a.shape
    nseg, L = _scan_geometry(S)
    T = min(SUBLANES, L)
    assert L >= 2

    def body(a3, d3, h3, g3, da3, ta_ref, td_ref, th_ref, e_ref, p_ref, init_ref):

        def to_steps(i, carry):
            j0 = pl.multiple_of(i * T, T)
            for src, dst in ((a3, ta_ref), (d3, td_ref), (h3, th_ref)):
                dst[pl.ds(j0, T)] = jnp.swapaxes(src[:, pl.ds(j0, T), :], 0, 1)
            return carry

        lax.fori_loop(0, L // T, to_steps, 0)
        seg = lax.broadcasted_iota(jnp.int32, (nseg, LANES), 0)
        b_last = jnp.where(seg == nseg - 1, 0.0, pltpu.roll(ta_ref[0], nseg - 1, axis=0))
        h_first = jnp.where(seg == 0, 0.0, pltpu.roll(th_ref[L - 1], 1, axis=0))

        def run1(jj, carry):
            gs, ps = carry
            j = L - 2 - jj
            bj = ta_ref[j + 1]
            return bj * gs + td_ref[j], bj * ps

        e_ref[...], p_ref[...] = _steps(L - 1, run1, (td_ref[L - 1], b_last))
        init_ref[nseg - 1:nseg, :] = jnp.zeros((1, LANES), F32)
        for s in range(nseg - 2, -1, -1):
            init_ref[s:s + 1, :] = e_ref[s + 1:s + 2, :] + p_ref[s + 1:s + 2, :] * init_ref[s + 1:s + 2, :]

        gs = b_last * init_ref[...] + td_ref[L - 1]
        td_ref[L - 1] = gs
        th_ref[L - 1] = gs * th_ref[L - 2]

        def run2(jj, gs):
            j = L - 2 - jj
            gs = ta_ref[j + 1] * gs + td_ref[j]
            td_ref[j] = gs
            th_ref[j] = gs * th_ref[j - 1]
            return gs

        gs = _steps(L - 2, run2, gs)
        gs = ta_ref[1] * gs + td_ref[0]
        td_ref[0] = gs
        th_ref[0] = gs * h_first

        def from_steps(i, carry):
            j0 = pl.multiple_of(i * T, T)
            g3[:, pl.ds(j0, T), :] = jnp.swapaxes(td_ref[pl.ds(j0, T)], 0, 1)
            da3[:, pl.ds(j0, T), :] = jnp.swapaxes(th_ref[pl.ds(j0, T)], 0, 1)
            return carry

        lax.fori_loop(0, L // T, from_steps, 0)

    seg_block = _bs((nseg, L, LANES), lambda i: (0, 0, i))
    blk = 2 * (5 * S * LANES * 4) + 3 * S * LANES * 4
    g, da = _streamed_call(
        body, name=name, grid=(C // LANES,),
        in_specs=[seg_block] * 3,
        out_specs=[seg_block] * 2,
        out_shape=[jax.ShapeDtypeStruct((nseg, L, C), F32)] * 2,
        scratch_shapes=[pltpu.VMEM((L, nseg, LANES), F32)] * 3 + [pltpu.VMEM((nseg, LANES), F32)] * 3,
        compiler_params=_cparams(blk),
    )(a.reshape(nseg, L, C), dh.reshape(nseg, L, C), h.reshape(nseg, L, C))
    return g.reshape(S, C), da.reshape(S, C)


def _mixer_out_fwd(c4, h, rg, gc, gr, x1, wc, wr, wo, g, b, alpha, name):
    S, D = x1.shape
    R = h.shape[1]
    tm = _divisor_tile(S, 256, 16)

    def body(c4_ref, h_ref, rg_ref, gc_ref, gr_ref, x_ref, wc_ref, wr_ref, wo_ref, g_ref, b_ref,
             yc_ref, yr_ref, z_ref, y_ref):
        yc = _dot(c4_ref[...], wc_ref[...])
        q = (h_ref[...] * _gelu(rg_ref[...].astype(F32))).astype(BF16)
        yr = _dot(q, wr_ref[...])
        yc_ref[...] = yc.astype(BF16)
        yr_ref[...] = yr.astype(BF16)
        m = (_sigmoid(gc_ref[...].astype(F32)) * yc + _sigmoid(gr_ref[...].astype(F32)) * yr).astype(BF16)
        z = alpha * x_ref[...] + _dot(m, wo_ref[...])
        z_ref[...] = z
        y_ref[...] = _norm_fwd(z, g_ref[...], b_ref[...])

    blk = 2 * (tm * D * 2 + 2 * tm * R * 4 + 7 * tm * D * 4) + (2 * D * D + R * D) * 2 + 6 * tm * D * 4
    td = _bs((tm, D), lambda i: (i, 0))
    tr = _bs((tm, R), lambda i: (i, 0))
    return _streamed_call(
        body, name=name, grid=(S // tm,),
        in_specs=[td, tr, tr, td, td, td, _layer_resident(wc)[1], _layer_resident(wr)[1], _layer_resident(wo)[1],
                  _resident((1, D)), _resident((1, D))],
        out_specs=[td] * 4,
        out_shape=[jax.ShapeDtypeStruct((S, D), BF16)] * 2 + [jax.ShapeDtypeStruct((S, D), F32)] * 2,
        compiler_params=_cparams(blk),
    )(c4, h, rg, gc, gr, x1, wc[0], wr[0], wo[0], g, b)


def _mixer_out_bwd(dy, z, g, wo, yc, yr, gc, gr, name):
    S, D = dy.shape
    tm = _divisor_tile(S, 256, 16)

    def body(dy_ref, z_ref, g_ref, wo_ref, yc_ref, yr_ref, gc_ref, gr_ref,
             dz_ref, dzb_ref, m_ref, dyc_ref, dyr_ref, dgc_ref, dgr_ref, sgc_ref, sgr_ref, dg_ref, db_ref):
        @pl.when(pl.program_id(0) == 0)
        def _():
            for ref in (sgc_ref, sgr_ref, dg_ref, db_ref):
                ref[...] = jnp.zeros_like(ref)

        dy_ = dy_ref[...]
        dz, xhat = _norm_bwd(z_ref[...], g_ref[...], dy_)
        dg_ref[...] += _colsum(dy_ * xhat)
        db_ref[...] += _colsum(dy_)
        dz_ref[...] = dz
        dzb = dz.astype(BF16)
        dzb_ref[...] = dzb
        dm = _dot_nt(dzb, wo_ref[...])
        yc = yc_ref[...].astype(F32)
        yr = yr_ref[...].astype(F32)
        sc = _sigmoid(gc_ref[...].astype(F32))
        sr = _sigmoid(gr_ref[...].astype(F32))
        m_ref[...] = (sc * yc + sr * yr).astype(BF16)
        dyc_ref[...] = (dm * sc).astype(BF16)
        dyr_ref[...] = (dm * sr).astype(BF16)
        dgc = dm * yc * sc * (1.0 - sc)
        dgr = dm * yr * sr * (1.0 - sr)
        dgc_ref[...] = dgc.astype(BF16)
        dgr_ref[...] = dgr.astype(BF16)
        sgc_ref[...] += _colsum(dgc)
        sgr_ref[...] += _colsum(dgr)

    blk = 2 * (7 * tm * D * 4 + 6 * tm * D * 2) + D * D * 2 + 8 * tm * D * 4
    td = _bs((tm, D), lambda i: (i, 0))
    rowspec = _bs((1, D), lambda i: (0, 0))
    row = jax.ShapeDtypeStruct((1, D), F32)
    bfd = jax.ShapeDtypeStruct((S, D), BF16)
    return _streamed_call(
        body, name=name, grid=(S // tm,),
        in_specs=[td, td, _resident((1, D)), _layer_resident(wo)[1], td, td, td, td],
        out_specs=[td] * 7 + [rowspec] * 4,
        out_shape=[jax.ShapeDtypeStruct((S, D), F32), bfd, bfd, bfd, bfd, bfd, bfd, row, row, row, row],
        compiler_params=_cparams(blk),
    )(dy, z, g, wo[0], yc, yr, gc, gr)


def _branch_bwd(dyc, dyr, wc, wr, h, rg, name):
    S, D = dyc.shape
    R = h.shape[1]
    tm = _divisor_tile(S, 256, 16)

    def body(dyc_ref, dyr_ref, wc_ref, wr_ref, h_ref, rg_ref, dc4_ref, dh_ref, drg_ref, q_ref, srg_ref):
        @pl.when(pl.program_id(0) == 0)
        def _():
            srg_ref[...] = jnp.zeros_like(srg_ref)

        dc4_ref[...] = _dot_nt(dyc_ref[...], wc_ref[...]).astype(BF16)
        dq = _dot_nt(dyr_ref[...], wr_ref[...])
        h_ = h_ref[...]
        rg_ = rg_ref[...].astype(F32)
        ge = _gelu(rg_)
        dh_ref[...] = dq * ge
        drg = dq * h_ * _gelu_grad(rg_)
        drg_ref[...] = drg.astype(BF16)
        srg_ref[...] += _colsum(drg)
        q_ref[...] = (h_ * ge).astype(BF16)

    blk = 2 * (2 * tm * D * 2 + tm * D * 4 + 3 * tm * R * 4 + 2 * tm * R * 2) + (D * D + R * D) * 2 + 6 * tm * R * 4
    td = _bs((tm, D), lambda i: (i, 0))
    tr = _bs((tm, R), lambda i: (i, 0))
    return _streamed_call(
        body, name=name, grid=(S // tm,),
        in_specs=[td, td, _layer_resident(wc)[1], _layer_resident(wr)[1], tr, tr],
        out_specs=[td, tr, tr, tr, _bs((1, R), lambda i: (0, 0))],
        out_shape=[jax.ShapeDtypeStruct((S, D), BF16), jax.ShapeDtypeStruct((S, R), F32), jax.ShapeDtypeStruct((S, R), BF16),
                   jax.ShapeDtypeStruct((S, R), BF16), jax.ShapeDtypeStruct((1, R), F32)],
        compiler_params=_cparams(blk),
    )(dyc, dyr, wc[0], wr[0], h, rg)


def _loss_head(y, target, name):
    S, D = y.shape
    tm = _divisor_tile(S, 512, 16)
    nsteps = S // tm

    def body(y_ref, t_ref, loss_ref, dy_ref, acc_ref):
        i = pl.program_id(0)

        @pl.when(i == 0)
        def _():
            acc_ref[...] = jnp.zeros_like(acc_ref)

        err = y_ref[...] - t_ref[...]
        dy_ref[...] = err * (1.0 / D)
        acc_ref[...] += _colsum(err * err)

        @pl.when(i == nsteps - 1)
        def _():
            loss_ref[...] = jnp.sum(acc_ref[...], axis=-1, keepdims=True) * (0.5 / D)

    td = _bs((tm, D), lambda i: (i, 0))
    return _streamed_call(
        body, name=name, grid=(nsteps,),
        in_specs=[td, td],
        out_specs=[_bs((1, 1), lambda i: (0, 0)), td],
        out_shape=[jax.ShapeDtypeStruct((1, 1), F32), jax.ShapeDtypeStruct((S, D), F32)],
        scratch_shapes=[pltpu.VMEM((1, D), F32)],
        compiler_params=_cparams(2 * 3 * tm * D * 4),
    )(y, target)


def _adamw_math(w, g, m, v):
    m = ADAM_B1 * m + (1.0 - ADAM_B1) * g
    v = ADAM_B2 * v + (1.0 - ADAM_B2) * (g * g)
    m_hat = m / (1.0 - ADAM_B1 ** ADAM_STEP)
    v_hat = v / (1.0 - ADAM_B2 ** ADAM_STEP)
    delta = -ADAM_LR * (m_hat / (jnp.sqrt(v_hat) + ADAM_EPS) + ADAM_WD * w)
    return delta, m, v


def _adamw_sharded(w, m, v, own, sib, rem, layer, filled, name):
    layers, r, c = w.shape
    r2 = r // 2
    tr = _divisor_tile(r2, max(16, (1 << 20) // (4 * c) // 16 * 16), 16)
    n_out = 4

    def body(w_ref, m_ref, v_ref, own_ref, sib_ref, rem_ref, *rest):
        g_ref, d_ref, nm_ref, nv_ref = rest[-n_out:]
        mine = pl.program_id(0) == lax.axis_index("c")
        g = jnp.where(mine, own_ref[...], sib_ref[...]).astype(F32)
        for j in range(N_CHIPS - 1):
            g = g + rem_ref[j].astype(F32)
        delta, nm, nv = _adamw_math(w_ref[...], g, m_ref[...], v_ref[...])
        g_ref[...] = g
        d_ref[...] = delta
        nm_ref[...] = nm
        nv_ref[...] = nv

    halves = lambda a: a.reshape(layers, 2, r2, c)
    tile = _bs((None, None, tr, c), lambda h, i: (layer, h, i, 0))
    flat = _bs((tr, c), lambda h, i: (i, 0))
    sds = jax.ShapeDtypeStruct((layers, 2, r2, c), F32)
    passed = [] if filled is None else [halves(a) for a in filled]
    outs = _streamed_call(
        body, name=name, grid=(2, r2 // tr),
        in_specs=[tile, tile, tile, flat, flat, _bs((N_CHIPS - 1, None, tr, c), lambda h, i: (0, h, i, 0))] + [ANY] * len(passed),
        out_specs=[tile] * n_out,
        out_shape=[sds] * n_out,
        input_output_aliases={6 + k: k for k in range(len(passed))},
        compiler_params=_cparams(2 * (7 * tr * c * 4 + (N_CHIPS + 1) * tr * c * 2) + 6 * tr * c * 4),
    )(halves(w), halves(m), halves(v), own, sib, rem, *passed)
    return [o.reshape(layers, r, c) for o in outs]


def _adamw_flat(w, m, v, g, name):
    rows = w.shape[0]
    tr = _divisor_tile(rows, 1024, SUBLANES)

    def body(w_ref, m_ref, v_ref, g_ref, d_ref, nm_ref, nv_ref):
        delta, nm, nv = _adamw_math(w_ref[...], g_ref[...], m_ref[...], v_ref[...])
        d_ref[...] = delta
        nm_ref[...] = nm
        nv_ref[...] = nv

    tile = _bs((tr, LANES), lambda i: (i, 0))
    sds = jax.ShapeDtypeStruct(w.shape, F32)
    return _streamed_call(
        body, name=name, grid=(rows // tr,),
        in_specs=[tile] * 4, out_specs=[tile] * 3, out_shape=[sds] * 3,
        compiler_params=_cparams(2 * 7 * tr * LANES * 4),
    )(w, m, v, g)


def _half_view(g, by_cols):
    rows, cols = g.shape
    if by_cols:
        return g.reshape(2, rows // 2, cols)
    return g.reshape(N_CHIPS, 2, rows // (2 * N_CHIPS), cols)


def _pair_sum_bf16(view, theirs, by_cols, name):
    rows, c = theirs.shape[-2:]
    tr = _divisor_tile(rows, max(16, (1 << 20) // (4 * c) // 16 * 16), 16)

    def body(g0_ref, g1_ref, t_ref, o_ref):
        mine = jnp.where(lax.axis_index("c") == 0, g0_ref[...], g1_ref[...])
        o_ref[...] = (mine + t_ref[...].astype(F32)).astype(BF16)

    if by_cols:
        grid = (rows // tr,)
        halves = [_bs((None, tr, c), functools.partial(lambda h, i: (h, i, 0), h)) for h in range(2)]
        tile = _bs((tr, c), lambda i: (i, 0))
    else:
        grid = (N_CHIPS, rows // tr)
        halves = [_bs((None, None, tr, c), functools.partial(lambda h, k, i: (k, h, i, 0), h)) for h in range(2)]
        tile = _bs((None, tr, c), lambda k, i: (k, i, 0))
    return _streamed_call(
        body, name=name, grid=grid,
        in_specs=halves + [tile], out_specs=tile, out_shape=jax.ShapeDtypeStruct(theirs.shape, BF16),
        compiler_params=_cparams(2 * 4 * tr * c * 4),
    )(view, view, theirs)


ANY = pl.BlockSpec(memory_space=pl.ANY)


def _mesh_position():
    return lax.axis_index("x"), lax.axis_index("y"), lax.axis_index("c")


def _other_chips():
    x, y, c = _mesh_position()
    chips = [(1 - x, y), (x, 1 - y), (1 - x, 1 - y)]
    return 2 * x + y, (x, y, 1 - c), chips, [2 * cx + cy for cx, cy in chips]


def _chip_slab(ref, k, width, by_cols):
    if by_cols:
        start = k * width if isinstance(k, int) else pl.multiple_of(k * width, LANES)
        return ref.at[:, pl.ds(start, width)]
    return ref.at[k]


HBM = pl.BlockSpec(memory_space=pltpu.HBM)
SEM = pl.BlockSpec(memory_space=pltpu.SEMAPHORE)
DATAFLOW = pltpu.SideEffectType.DATAFLOW_SIDE_EFFECTING
N_GATHER_COPIES = 4


def _land_shape(src, by_cols):
    return src.shape[:-1] + (N_CHIPS * src.shape[-1],) if by_cols else (N_CHIPS,) + src.shape


def _gather_copy(src_ref, land_ref, by_cols, send_sems, recv_sems, pos, j, slab, to):
    width = src_ref.shape[-1]
    return pltpu.make_async_remote_copy(src_ref=src_ref, dst_ref=_chip_slab(land_ref, slab, width, by_cols),
                                        send_sem=send_sems.at[N_GATHER_COPIES * pos + j],
                                        recv_sem=recv_sems.at[N_GATHER_COPIES * pos + j],
                                        device_id=to, device_id_type=MESH)


def _gather_start(srcs, by_cols, groups, name):
    U = len(srcs)
    G = len(groups)
    lands = [lax.empty(_land_shape(s, bc), s.dtype) for s, bc in zip(srcs, by_cols)]

    def body(*refs):
        src = refs[:U]
        land = refs[U:2 * U]
        send_sems = refs[2 * U:2 * U + G]
        recv_sems = refs[2 * U + G:2 * U + 2 * G]
        token = refs[-1]
        c = lax.axis_index("c")
        me, sibling, chips, _ = _other_chips()
        targets = [(*chip, c) for chip in chips] + [sibling]
        for g, members in enumerate(groups):
            for pos, u in enumerate(members):
                for j, to in enumerate(targets):
                    _gather_copy(src[u], land[u], by_cols[u], send_sems[g], recv_sems[g], pos, j, me, to).start()
        token[...] = jnp.zeros_like(token)

    sem_shapes = [pltpu.SemaphoreType.DMA((len(m) * N_GATHER_COPIES,)) for m in groups]
    outs = pl.pallas_call(
        body, name=name,
        out_shape=tuple(sem_shapes + sem_shapes + [pltpu.HBM(s.shape, s.dtype) for s in srcs]
                        + [pltpu.HBM(v.shape, v.dtype) for v in lands] + [jax.ShapeDtypeStruct((SUBLANES, LANES), F32)]),
        in_specs=[HBM] * (2 * U),
        out_specs=tuple([SEM] * (2 * G) + [HBM] * (2 * U) + [pl.BlockSpec(memory_space=pltpu.VMEM)]),
        input_output_aliases={i: 2 * G + i for i in range(2 * U)},
        compiler_params=pltpu.CompilerParams(has_side_effects=DATAFLOW),
    )(*[pltpu.with_memory_space_constraint(a, pltpu.HBM) for a in list(srcs) + lands])
    return outs[:G], outs[G:2 * G], outs[2 * G:2 * G + U], outs[2 * G + U:2 * G + 2 * U], outs[-1]


def _gather_wait(srcs, lands, by_cols, send_sems, recv_sems, after, name):
    n = len(srcs)

    def body(*refs):
        src = refs[:n]
        land = refs[n:2 * n]
        send_ref, recv_ref = refs[2 * n:2 * n + 2]
        _, sibling, _, _ = _other_chips()
        for pos in range(n):
            for j in range(N_GATHER_COPIES):
                cp = _gather_copy(src[pos], land[pos], by_cols[pos], send_ref, recv_ref, pos, j, 0, sibling)
                cp.wait_send()
                cp.wait_recv()

    outs = pl.pallas_call(
        body, name=name,
        out_shape=tuple([pltpu.HBM(s.shape, s.dtype) for s in srcs] + [pltpu.HBM(v.shape, v.dtype) for v in lands]),
        in_specs=[HBM] * (2 * n) + [SEM, SEM, pl.BlockSpec(memory_space=pl.ANY)],
        out_specs=tuple([HBM] * (2 * n)),
        input_output_aliases={i: i for i in range(2 * n)},
        compiler_params=pltpu.CompilerParams(has_side_effects=DATAFLOW),
    )(*srcs, *lands, send_sems, recv_sems, after)
    return outs[n:]


def _scatter_grads(csums, by_cols, name):
    n = len(csums)
    shard = [(s.shape[0], s.shape[1] // N_CHIPS) if bc else s.shape[1:] for s, bc in zip(csums, by_cols)]

    def body(*refs):
        src = refs[:n]
        rem = refs[n:2 * n]
        sib = refs[2 * n:3 * n]
        send_sems, recv_sems = refs[3 * n:]
        c = lax.axis_index("c")
        me, sibling, chips, chip_ids = _other_chips()

        def remote(i, k, src_ref, dst_ref, to):
            return pltpu.make_async_remote_copy(src_ref=src_ref, dst_ref=dst_ref, send_sem=send_sems.at[i, k],
                                                recv_sem=recv_sems.at[i, k], device_id=to, device_id_type=MESH)

        def part(i, k):
            return _chip_slab(src[i], k, shard[i][-1], by_cols[i])

        started = []
        for i in range(n):
            for j in range(3):
                started.append(remote(i, j, part(i, chip_ids[j]), rem[i].at[j, c], (*chips[j], c)))
            started.append(remote(i, 6, part(i, me), sib[i], sibling))
        for cp in started:
            cp.start()
        for i in range(n):
            for j in range(3):
                slot = rem[i].at[j, c]
                remote(i, j, slot, slot, sibling).wait_recv()
                fwd = remote(i, 3 + j, slot, slot, sibling)
                fwd.start()
                started.append(fwd)
        for i in range(n):
            for j in range(3):
                slot = rem[i].at[j, 1 - c]
                remote(i, 3 + j, slot, slot, sibling).wait_recv()
            remote(i, 6, sib[i], sib[i], sibling).wait_recv()
        for cp in started:
            cp.wait_send()

    out_shape = ([jax.ShapeDtypeStruct((N_CHIPS - 1, 2) + tuple(sh), s.dtype) for s, sh in zip(csums, shard)]
                 + [jax.ShapeDtypeStruct(tuple(sh), s.dtype) for s, sh in zip(csums, shard)])
    outs = _streamed_call(
        body, name=name,
        in_specs=[ANY] * n, out_specs=[ANY] * (2 * n), out_shape=out_shape,
        scratch_shapes=[pltpu.SemaphoreType.DMA((n, 7)), pltpu.SemaphoreType.DMA((n, 7))],
    )(*csums)
    return outs[:n], outs[n:]


def _sibling_exchange(views, by_cols, name):
    n = len(views)

    def body(*refs):
        src = refs[:n]
        theirs = refs[n:2 * n]
        send_sems, recv_sems = refs[2 * n:]
        x, y, c = _mesh_position()
        copies = []
        for i in range(n):
            half = src[i].at[1 - c] if by_cols[i] else src[i].at[:, 1 - c]
            copies.append(pltpu.make_async_remote_copy(src_ref=half, dst_ref=theirs[i], send_sem=send_sems.at[i],
                                                       recv_sem=recv_sems.at[i], device_id=(x, y, 1 - c), device_id_type=MESH))
        for cp in copies:
            cp.start()
        for cp in copies:
            cp.wait()

    out_shape = [jax.ShapeDtypeStruct(v.shape[1:] if bc else v.shape[:1] + v.shape[2:], v.dtype) for v, bc in zip(views, by_cols)]
    return _streamed_call(
        body, name=name,
        in_specs=[ANY] * n, out_specs=[ANY] * n, out_shape=out_shape,
        scratch_shapes=[pltpu.SemaphoreType.DMA((n,)), pltpu.SemaphoreType.DMA((n,))],
    )(*views)


N_SCATTER_COPIES = 7


def _scatter_start(csums, by_cols, name):
    n = len(csums)
    shard = [(s.shape[0], s.shape[1] // N_CHIPS) if bc else s.shape[1:] for s, bc in zip(csums, by_cols)]
    rems = [lax.empty((N_CHIPS - 1, 2) + tuple(sh), s.dtype) for s, sh in zip(csums, shard)]
    sibs = [lax.empty(tuple(sh), s.dtype) for s, sh in zip(csums, shard)]

    def body(*refs):
        src = refs[:n]
        rem = refs[n:2 * n]
        sib = refs[2 * n:3 * n]
        send_sems, recv_sems = refs[3 * n:3 * n + 2]
        token = refs[-1]
        c = lax.axis_index("c")
        me, sibling, chips, chip_ids = _other_chips()
        for i in range(n):
            base = N_SCATTER_COPIES * i
            for j in range(3):
                part = _chip_slab(src[i], chip_ids[j], shard[i][-1], by_cols[i])
                for core in range(2):
                    pltpu.make_async_remote_copy(src_ref=part, dst_ref=rem[i].at[j, c], send_sem=send_sems.at[base + 2 * j + core],
                                                 recv_sem=recv_sems.at[base + 2 * j + c], device_id=(*chips[j], core),
                                                 device_id_type=MESH).start()
            pltpu.make_async_remote_copy(src_ref=_chip_slab(src[i], me, shard[i][-1], by_cols[i]), dst_ref=sib[i],
                                         send_sem=send_sems.at[base + 6], recv_sem=recv_sems.at[base + 6], device_id=sibling,
                                         device_id_type=MESH).start()
        token[...] = jnp.zeros_like(token)

    sems = pltpu.SemaphoreType.DMA((N_SCATTER_COPIES * n,))
    operands = list(csums) + rems + sibs
    outs = pl.pallas_call(
        body, name=name,
        out_shape=tuple([sems, sems] + [pltpu.HBM(a.shape, a.dtype) for a in operands] + [jax.ShapeDtypeStruct((SUBLANES, LANES), F32)]),
        in_specs=[HBM] * (3 * n),
        out_specs=tuple([SEM, SEM] + [HBM] * (3 * n) + [pl.BlockSpec(memory_space=pltpu.VMEM)]),
        input_output_aliases={i: 2 + i for i in range(3 * n)},
        compiler_params=pltpu.CompilerParams(has_side_effects=DATAFLOW),
    )(*[pltpu.with_memory_space_constraint(a, pltpu.HBM) for a in operands])
    return (outs[0], outs[1], outs[2:2 + n], outs[2 + n:2 + 2 * n], outs[2 + 2 * n:2 + 3 * n]), outs[-1]


def _scatter_wait(send_sems, recv_sems, srcs, rems, sibs, by_cols, after, name):
    n = len(srcs)

    def body(*refs):
        src = refs[:n]
        rem = refs[n:2 * n]
        sib = refs[2 * n:3 * n]
        send_ref, recv_ref = refs[3 * n:3 * n + 2]
        _, sibling, _, _ = _other_chips()
        for i in range(n):
            base = N_SCATTER_COPIES * i
            width = sib[i].shape[-1]
            for j in range(3):
                for core in range(2):
                    cp = pltpu.make_async_remote_copy(src_ref=_chip_slab(src[i], 0, width, by_cols[i]), dst_ref=rem[i].at[j, core],
                                                      send_sem=send_ref.at[base + 2 * j + core],
                                                      recv_sem=recv_ref.at[base + 2 * j + core], device_id=sibling,
                                                      device_id_type=MESH)
                    cp.wait_send()
                    cp.wait_recv()
            cp = pltpu.make_async_remote_copy(src_ref=_chip_slab(src[i], 0, width, by_cols[i]), dst_ref=sib[i],
                                              send_sem=send_ref.at[base + 6], recv_sem=recv_ref.at[base + 6], device_id=sibling,
                                              device_id_type=MESH)
            cp.wait_send()
            cp.wait_recv()

    operands = list(srcs) + list(rems) + list(sibs)
    outs = pl.pallas_call(
        body, name=name,
        out_shape=tuple(pltpu.HBM(a.shape, a.dtype) for a in operands),
        in_specs=[HBM] * (3 * n) + [SEM, SEM, pl.BlockSpec(memory_space=pl.ANY)],
        out_specs=tuple([HBM] * (3 * n)),
        input_output_aliases={i: i for i in range(3 * n)},
        compiler_params=pltpu.CompilerParams(has_side_effects=DATAFLOW),
    )(*operands, send_sems, recv_sems, after)
    return outs[:n], outs[n:2 * n], outs[2 * n:3 * n]


def _all_reduce_small(v, name):
    _, rows, _ = v.shape

    def body(v_ref, o_ref, recv_ref, send_sems, recv_sems):
        x, y, c = _mesh_position()
        me = 4 * x + 2 * y + c
        peers = []
        for d in range(1, N_DEV):
            px, py, pc = x ^ ((d >> 2) & 1), y ^ ((d >> 1) & 1), c ^ (d & 1)
            peers.append(((px, py, pc), 4 * px + 2 * py + pc))

        def remote(k, src_ref, dst_ref, to):
            return pltpu.make_async_remote_copy(src_ref=src_ref, dst_ref=dst_ref, send_sem=send_sems.at[k],
                                                recv_sem=recv_sems.at[k], device_id=to, device_id_type=MESH)

        scatter = [remote(d, v_ref.at[pid], recv_ref.at[me], to) for d, (to, pid) in enumerate(peers)]
        for cp in scatter:
            cp.start()
        recv_ref[pl.ds(me, 1)] = v_ref[pl.ds(me, 1)]
        for d, (to, pid) in enumerate(peers):
            remote(d, v_ref.at[pid], recv_ref.at[pid], to).wait_recv()
        total = recv_ref[0]
        for s in range(1, N_DEV):
            total = total + recv_ref[s]
        o_ref[pl.ds(me, 1)] = total[None]
        gather = [remote(N_DEV - 1 + d, o_ref.at[me], o_ref.at[me], to) for d, (to, pid) in enumerate(peers)]
        for cp in gather:
            cp.start()
        for d, (to, pid) in enumerate(peers):
            remote(N_DEV - 1 + d, o_ref.at[pid], o_ref.at[pid], to).wait_recv()
        for cp in scatter + gather:
            cp.wait_send()

    vm = pl.BlockSpec(memory_space=pltpu.VMEM)
    return pl.pallas_call(
        body, name=name,
        in_specs=[vm], out_specs=vm, out_shape=jax.ShapeDtypeStruct(v.shape, F32),
        scratch_shapes=[pltpu.VMEM(v.shape, F32), pltpu.SemaphoreType.DMA((2 * (N_DEV - 1),)),
                        pltpu.SemaphoreType.DMA((2 * (N_DEV - 1),))],
        compiler_params=_cparams(4 * _nbytes(v.shape, F32)),
    )(v)


SHARDED_MATS = ("ffn1_w_gu", "ffn1_w_down", "mix_w_in", "conv_w_proj", "rnn_w_proj", "mix_w_out", "ffn2_w_gu", "ffn2_w_down")
COL_SHARDED = ("ffn1_w_gu", "ffn2_w_gu", "conv_dw_w")
SHARDED_VECS = ("conv_dw_w", "rnn_conv_w")
WEIGHT_NAMES = ("ffn1_w_gu", "ffn1_w_down", "ln1_g", "ln1_b", "mix_w_in", "mix_b_in", "conv_dw_w", "conv_dw_b", "conv_gn_g",
                "conv_gn_b", "conv_w_proj", "rnn_conv_w", "rnn_conv_b", "rnn_w_a", "rnn_b_a", "rnn_w_x", "rnn_b_x",
                "rnn_lambda", "rnn_w_proj", "mix_w_out", "ln2_g", "ln2_b", "ffn2_w_gu", "ffn2_w_down", "ln3_g", "ln3_b")
SMALL_NAMES = tuple(n for n in WEIGHT_NAMES if n not in SHARDED_MATS)
SECTION_NAMES = ("cv", "cg", "rx", "rg", "gc", "gr")


def _unshard_cols(gathered):
    k4, K, n = gathered.shape
    return jnp.transpose(gathered, (1, 0, 2)).reshape(K, k4 * n)


def _row(v):
    return v.reshape(1, -1)


def _layer_forward(x0, p, alpha, l, hooks):
    t = f"l{l}_"
    sv = {"x0": x0}
    x1, sv["z1"], sv["hg1"], sv["hu1"] = _ffn_fwd(x0, p["wgu1"], p["wd1"], p["ln1_g"], p["ln1_b"], alpha, t + "ffn1_fwd")
    sv["x1"] = x1
    hooks.get("after_ffn1", lambda v: None)(x1)
    sec = dict(zip(SECTION_NAMES, _mix_in(x1, p["win"], p["bin"], p["sections"], t + "mix_in")))
    sv.update(sec)
    sv["c2"], c4 = _conv_branch_fwd(sec["cv"], sec["cg"], p["conv_dw_w"], p["conv_dw_b"], p["conv_gn_g"], p["conv_gn_b"], t + "conv_fwd")
    sv["c4"] = c4
    r1 = _short_conv_fwd(sec["rx"], p["rnn_conv_w"], p["rnn_conv_b"], t + "rconv_fwd")
    sv["r1"] = r1
    sv["ra"], sv["ri"], a, uu = _gates_fwd(r1, p["wa"], p["wx"], p["rnn_b_a"], p["rnn_b_x"], p["rnn_lambda"], t + "gates_fwd")
    sv["a"] = a
    h = _scan_fwd(a, uu, t + "scan_fwd")
    sv["h"] = h
    hooks.get("after_scan", lambda v: None)(h)
    sv["yc"], sv["yr"], sv["z2"], x2 = _mixer_out_fwd(c4, h, sec["rg"], sec["gc"], sec["gr"], x1, p["wc"], p["wr"], p["wo"],
                                                      p["ln2_g"], p["ln2_b"], alpha, t + "mixout_fwd")
    sv["x2"] = x2
    hooks.get("after_mixer", lambda v: None)(x2)
    x3, sv["z3"], sv["hg2"], sv["hu2"] = _ffn_fwd(x2, p["wgu2"], p["wd2"], p["ln3_g"], p["ln3_b"], alpha, t + "ffn2_fwd")
    hooks.get("after_layer", lambda v: None)(x3)
    return x3, sv


def _layer_backward(dy, p, sv, alpha, l, before_ffn1=None):
    t = f"l{l}_"
    g, gb = {}, {}
    dx2, df, a_act, dhg, dhu, g["ln3_g"], g["ln3_b"] = _ffn_bwd(dy, sv["z3"], sv["hg2"], sv["hu2"], p["wgu2"], p["wd2"],
                                                                 p["ln3_g"], alpha, t + "ffn2_bwd")
    g["ffn2_w_down"], gb["ffn2_w_down"] = _mm_tn(a_act, df, t + "dwd2")
    g["ffn2_w_gu"], gb["ffn2_w_gu"] = _mm_tn_pair(sv["x2"], dhg, dhu, t + "dwgu2")
    (dz2, dz2b, m_b, dyc, dyr, dgc, dgr, s_gc, s_gr, g["ln2_g"], g["ln2_b"]) = _mixer_out_bwd(
        dx2, sv["z2"], p["ln2_g"], p["wo"], sv["yc"], sv["yr"], sv["gc"], sv["gr"], t + "mixout_bwd")
    g["mix_w_out"], gb["mix_w_out"] = _mm_tn(m_b, dz2b, t + "dwo")
    dc4, dh, drg, q_b, s_rg = _branch_bwd(dyc, dyr, p["wc"], p["wr"], sv["h"], sv["rg"], t + "branch_bwd")
    g["conv_w_proj"], gb["conv_w_proj"] = _mm_tn(sv["c4"], dyc, t + "dwc")
    g["rnn_w_proj"], gb["rnn_w_proj"] = _mm_tn(q_b, dyr, t + "dwr")
    (dcv, dcg, g["conv_dw_w"], g["conv_dw_b"], g["conv_gn_g"], g["conv_gn_b"], s_cv, s_cg) = _conv_branch_bwd(
        dc4, sv["c2"], sv["cv"], sv["cg"], p["conv_dw_w"], p["conv_gn_g"], p["conv_gn_b"], t + "conv_bwd")
    guu, da = _scan_bwd(sv["a"], dh, sv["h"], t + "scan_bwd")
    dr1, dpa, dpx, g["rnn_b_a"], g["rnn_b_x"], g["rnn_lambda"] = _gates_bwd(
        guu, da, sv["ra"], sv["ri"], sv["r1"], p["wa"], p["wx"], p["rnn_lambda"], t + "gates_bwd")
    g["rnn_w_a"], g["rnn_w_x"] = _block_grads(sv["r1"], dpa, dpx, t + "dwax")
    drx, g["rnn_conv_w"], g["rnn_conv_b"], s_rx = _short_conv_bwd(dr1, sv["rx"], p["rnn_conv_w"], t + "rconv_bwd")
    du = {"cv": dcv, "cg": dcg, "rx": drx, "rg": drg, "gc": dgc, "gr": dgr}
    order = ("cv", "cg", "rx", "rg", "gc", "gr")
    pieces = [_mm_tn(du[s], sv["x1"], t + "dwin_" + s) for s in order]
    g["mix_w_in"] = jnp.concatenate([f for f, _ in pieces], axis=0)
    gb["mix_w_in"] = jnp.concatenate([h for _, h in pieces], axis=0)
    g["mix_b_in"] = jnp.concatenate([s_cv, s_cg, s_rx, s_rg, s_gc, s_gr], axis=1)
    dx1 = _mix_dx(dz2, [du[s] for s in order], p["win"], p["sections"], alpha, t + "mix_dx")
    ln1_g = p["ln1_g"] if before_ffn1 is None else p["ln1_g"] + before_ffn1(g, gb)[0:1, 0:1]
    dx0, df, a_act, dhg, dhu, g["ln1_g"], g["ln1_b"] = _ffn_bwd(dx1, sv["z1"], sv["hg1"], sv["hu1"], p["wgu1"], p["wd1"],
                                                                 ln1_g, alpha, t + "ffn1_bwd")
    g["ffn1_w_down"], gb["ffn1_w_down"] = _mm_tn(a_act, df, t + "dwd1")
    g["ffn1_w_gu"], gb["ffn1_w_gu"] = _mm_tn_pair(sv["x0"], dhg, dhu, t + "dwgu1")
    return dx0, g, gb


def _pack_small(arrays, piece_rows):
    flat = jnp.concatenate([a.reshape(-1) for a in arrays])
    total = N_DEV * piece_rows * LANES
    return jnp.pad(flat, (0, total - flat.shape[0])).reshape(N_DEV, piece_rows, LANES)


def _unpack_small(packed, shapes):
    flat = packed.reshape(-1)
    out, off = [], 0
    for shp in shapes:
        n = 1
        for s in shp:
            n *= s
        out.append(flat[off:off + n].reshape(shp))
        off += n
    return out


def kernel(x, ffn1_w_gu, ffn1_w_down, ln1_g, ln1_b, mix_w_in, mix_b_in, conv_dw_w, conv_dw_b, conv_gn_g, conv_gn_b, conv_w_proj, rnn_conv_w, rnn_conv_b, rnn_w_a, rnn_b_a, rnn_w_x, rnn_b_x, rnn_lambda, rnn_w_proj, mix_w_out, ln2_g, ln2_b, ffn2_w_gu, ffn2_w_down, ln3_g, ln3_b, loss_target, m_ffn1_w_gu, m_ffn1_w_down, m_ln1_g, m_ln1_b, m_mix_w_in, m_mix_b_in, m_conv_dw_w, m_conv_dw_b, m_conv_gn_g, m_conv_gn_b, m_conv_w_proj, m_rnn_conv_w, m_rnn_conv_b, m_rnn_w_a, m_rnn_b_a, m_rnn_w_x, m_rnn_b_x, m_rnn_lambda, m_rnn_w_proj, m_mix_w_out, m_ln2_g, m_ln2_b, m_ffn2_w_gu, m_ffn2_w_down, m_ln3_g, m_ln3_b, v_ffn1_w_gu, v_ffn1_w_down, v_ln1_g, v_ln1_b, v_mix_w_in, v_mix_b_in, v_conv_dw_w, v_conv_dw_b, v_conv_gn_g, v_conv_gn_b, v_conv_w_proj, v_rnn_conv_w, v_rnn_conv_b, v_rnn_w_a, v_rnn_b_a, v_rnn_w_x, v_rnn_b_x, v_rnn_lambda, v_rnn_w_proj, v_mix_w_out, v_ln2_g, v_ln2_b, v_ffn2_w_gu, v_ffn2_w_down, v_ln3_g, v_ln3_b):
    args = locals()
    W = {n: args[n] for n in WEIGHT_NAMES}
    M = {n: args["m_" + n] for n in WEIGHT_NAMES}
    V = {n: args["v_" + n] for n in WEIGHT_NAMES}
    depth = ln1_g.shape[0]
    assert depth == 2, "each core of a chip moves one layer's weights and gradients"
    alpha = float((2 * depth) ** 0.25)
    S, D = x.shape[1], x.shape[2]
    F = ffn1_w_down.shape[1] * N_CHIPS
    R = rnn_w_proj.shape[1] * N_CHIPS
    chip = 2 * lax.axis_index("x") + lax.axis_index("y")

    for d in (W, M, V):
        d["mix_w_in"] = jnp.transpose(d["mix_w_in"], (0, 2, 1))

    names = SHARDED_MATS + SHARDED_VECS
    unit_groups = [[(0, "ffn1_w_gu"), (0, "ffn1_w_down")], [(0, "mix_w_in"), (0, "conv_dw_w"), (0, "rnn_conv_w")],
                   [(0, "conv_w_proj"), (0, "rnn_w_proj"), (0, "mix_w_out")], [(0, "ffn2_w_gu"), (0, "ffn2_w_down")],
                   [(1, n) for n in names]]
    order = [u for g in unit_groups for u in g]
    index = {u: i for i, u in enumerate(order)}
    groups = [[index[u] for u in g] for g in unit_groups]
    by_cols = [n in COL_SHARDED for _, n in order]

    def start_gather(ids, id_groups, token, name):
        srcs = [(W[n][l] + token).astype(BF16) if n in SHARDED_MATS else W[n][l] for l, n in (order[i] for i in ids)]
        local = {i: k for k, i in enumerate(ids)}
        send, recv, thru, land, new_token = _gather_start(srcs, [by_cols[i] for i in ids],
                                                          [[local[i] for i in g] for g in id_groups], name)
        return list(send), list(recv), {i: thru[local[i]] for i in ids}, {i: land[local[i]] for i in ids}, new_token

    send_sems, recv_sems, src_thru, land_thru, token = start_gather(groups[0], groups[:1], 0.0, "gather_start_first")
    rest = start_gather([i for g in groups[1:] for i in g], groups[1:], token[0, 0], "gather_start_rest")
    send_sems += rest[0]
    recv_sems += rest[1]
    src_thru.update(rest[2])
    land_thru.update(rest[3])

    sections = ((0, D, F32), (D, D, F32), (2 * D, R, F32), (2 * D + R, R, BF16), (2 * D + 2 * R, D, BF16),
                (3 * D + 2 * R, D, BF16))
    keys = {"ffn1_w_gu": "wgu1", "ffn1_w_down": "wd1", "ffn2_w_gu": "wgu2", "ffn2_w_down": "wd2", "conv_w_proj": "wc",
            "rnn_w_proj": "wr", "mix_w_out": "wo"}
    params = []
    for l in range(depth):
        p = {"wa": _embed_blocks(rnn_w_a[l], f"l{l}_embed_wa"), "wx": _embed_blocks(rnn_w_x[l], f"l{l}_embed_wx")}
        for n in ("ln1_g", "ln1_b", "ln2_g", "ln2_b", "ln3_g", "ln3_b", "conv_dw_b", "conv_gn_g", "conv_gn_b", "rnn_conv_b",
                  "rnn_b_a", "rnn_b_x", "rnn_lambda"):
            p[n] = _row(W[n][l])
        p["bin"] = _row(mix_b_in[l])
        p["sections"] = sections
        params.append(p)

    def wait_group(g, after):
        ids = groups[g]
        landed = _gather_wait([src_thru[i] for i in ids], [land_thru[i] for i in ids], [by_cols[i] for i in ids],
                              send_sems[g], recv_sems[g], after, f"gather_wait{g}")
        for i, full in zip(ids, landed):
            l, n = order[i]
            p = params[l]
            if n not in COL_SHARDED:
                full = full.reshape((N_CHIPS * full.shape[1],) + full.shape[2:])
            if n == "mix_w_in":
                p["win"] = full
            elif n == "rnn_conv_w":
                p[n] = _unshard_cols(landed[ids.index(i)])
            elif n == "conv_dw_w":
                p[n] = full
            else:
                p[keys[n]] = (full[None], 0)

    h = x[0]
    wait_group(0, h)
    saved = []
    hooks = [{"after_ffn1": lambda v: wait_group(1, v), "after_scan": lambda v: wait_group(2, v),
              "after_mixer": lambda v: wait_group(3, v), "after_layer": lambda v: wait_group(4, v)}, {}]
    for l in range(depth):
        h, sv = _layer_forward(h, params[l], alpha, l, hooks[l])
        saved.append(sv)
    loss_part, dy = _loss_head(h, loss_target[0], "loss_head")
    loss = lax.psum(loss_part[0, 0], ("x", "y", "c"))
    def pair_sums(names, g, gb, tag):
        cols = [n in COL_SHARDED for n in names]
        theirs = _sibling_exchange([_half_view(gb[n], bc) for n, bc in zip(names, cols)], cols, "pair_exchange" + tag)
        return cols, [_pair_sum_bf16(_half_view(g[n], bc), t, bc, f"pair_sum{tag}_{n}") for n, bc, t in zip(names, cols, theirs)]

    pending = []

    def start_scatter(layer, names, g, gb, tag):
        cols, sums = pair_sums(names, g, gb, tag)
        in_flight, token = _scatter_start(sums, cols, "scatter_start" + tag)
        pending.append((layer, names, cols, in_flight, tag))
        return token

    early = [n for n in SHARDED_MATS if not n.startswith("ffn1_")]
    late = [n for n in SHARDED_MATS if n.startswith("ffn1_")]
    grads, grads_bf16 = [None] * depth, [None] * depth
    dy, grads[1], grads_bf16[1] = _layer_backward(dy, params[1], saved[1], alpha, 1)
    token = start_scatter(1, SHARDED_MATS, grads[1], grads_bf16[1], "1")
    first = dict(params[0], ln3_g=params[0]["ln3_g"] + token[0:1, 0:1])
    dy, grads[0], grads_bf16[0] = _layer_backward(dy, first, saved[0], alpha, 0,
                                                  before_ffn1=lambda g, gb: start_scatter(0, early, g, gb, "0a"))
    grad_x = dy[None]
    late_cols, late_sums = pair_sums(late, grads[0], grads_bf16[0], "0b")
    late_rem, late_sib = _scatter_grads(late_sums, late_cols, "scatter_grads0b")
    partial = {(0, n): part for n, part in zip(late, zip(late_sums, late_rem, late_sib))}
    for layer, names, cols, in_flight, tag in pending:
        for n, part in zip(names, zip(*_scatter_wait(*in_flight, cols, late_rem[0], "scatter_wait" + tag))):
            partial[(layer, n)] = part

    results = {}
    for l in (1, 0):
        for n in SHARDED_MATS:
            cs, rm, sb = partial[(l, n)]
            if n in COL_SHARDED:
                width = cs.shape[1] // N_CHIPS
                own = lax.dynamic_slice_in_dim(cs, chip * width, width, axis=1)
            else:
                own = lax.dynamic_index_in_dim(cs, chip, axis=0, keepdims=False)
            results[n] = _adamw_sharded(W[n], M[n], V[n], own, sb, rm, l, results.get(n), f"adamw{l}_{n}")
    out_g, out_d, out_m, out_v = {}, {}, {}, {}
    for n in SHARDED_MATS:
        outs = results[n]
        if n == "mix_w_in":
            outs = [jnp.transpose(o, (0, 2, 1)) for o in outs]
        out_g[n], out_d[n], out_m[n], out_v[n] = outs

    small_grads = [jnp.stack([grads[l][n].reshape(W[n].shape[1:] if n not in SHARDED_VECS else
                                                   (W[n].shape[1], W[n].shape[2] * N_CHIPS)) for l in range(depth)])
                   for n in SMALL_NAMES]
    n_small = sum(int(a.size) for a in small_grads)
    piece_rows = -(-n_small // (N_DEV * LANES * SUBLANES)) * SUBLANES
    reduced = _unpack_small(_all_reduce_small(_pack_small(small_grads, piece_rows), "all_reduce_small"),
                            [a.shape for a in small_grads])
    local_g = []
    for n, gr in zip(SMALL_NAMES, reduced):
        if n in SHARDED_VECS:
            width = W[n].shape[2]
            gr = lax.dynamic_slice_in_dim(gr, chip * width, width, axis=2)
        local_g.append(gr)
    n_local = sum(int(a.size) for a in local_g)
    flat_rows = -(-n_local // (N_DEV * LANES * SUBLANES)) * SUBLANES * N_DEV
    pack = lambda arrs: _pack_small(arrs, flat_rows // N_DEV).reshape(flat_rows, LANES)
    shapes = [a.shape for a in local_g]
    deltas, new_m, new_v = _adamw_flat(pack([W[n] for n in SMALL_NAMES]), pack([M[n] for n in SMALL_NAMES]),
                                       pack([V[n] for n in SMALL_NAMES]), pack(local_g), "adamw_small")
    for n, gr, d_, m_, v_ in zip(SMALL_NAMES, local_g, _unpack_small(deltas, shapes), _unpack_small(new_m, shapes),
                                 _unpack_small(new_v, shapes)):
        out_g[n], out_d[n], out_m[n], out_v[n] = gr, d_, m_, v_

    return (loss, grad_x, *[out_g[n] for n in WEIGHT_NAMES], *[out_d[n] for n in WEIGHT_NAMES],
            *[out_m[n] for n in WEIGHT_NAMES], *[out_v[n] for n in WEIGHT_NAMES])
```

```python
import functools

import jax
import jax.numpy as jnp
from jax import lax
from jax.experimental import pallas as pl
from jax.experimental.pallas import tpu as pltpu

F32 = jnp.float32
BF16 = jnp.bfloat16
MESH = pl.DeviceIdType.MESH

LN_EPS = 1e-5
CONV_GROUPS = 8
RNN_BLOCKS = 16
RG_LRU_C = 8.0
ADAM_LR = 0.001
ADAM_B1 = 0.9
ADAM_B2 = 0.999
ADAM_EPS = 1e-08
ADAM_WD = 0.01
ADAM_STEP = 10

LANES = 128
SUBLANES = 8
V7X_VMEM_BYTES = 64 << 20
VMEM_LIMIT_CAP = V7X_VMEM_BYTES - (6 << 20)
N_CHIPS = 4
N_DEV = 8
CONV_ROWS = 64
EW_ROWS = 1024
SCAN_SEGMENTS = 32
SCAN_UNROLL = 4


def _cparams(block_bytes):
    limit = min(VMEM_LIMIT_CAP, max(int(block_bytes) + (8 << 20), 24 << 20))
    return pltpu.CompilerParams(vmem_limit_bytes=limit)


def _nbytes(shape, dtype):
    n = 1
    for s in shape:
        n *= s
    return n * jnp.dtype(dtype).itemsize


def _divisor_tile(n, limit, quantum):
    if n <= limit:
        return n
    best = None
    for t in range(quantum, limit + 1, quantum):
        if n % t == 0:
            best = t
    assert best is not None, (n, limit, quantum)
    return best


def _bs(shape, imap, **kw):
    return pl.BlockSpec(shape, imap, **kw)


def _resident(shape):
    nd = len(shape)
    return pl.BlockSpec(shape, lambda *_: (0,) * nd, pipeline_mode=pl.Buffered(1))


def _streamed_call(body, **kw):
    call = pl.pallas_call(body, **kw)
    return lambda *operands: call(*[pltpu.with_memory_space_constraint(o, pltpu.HBM) for o in operands])


def _layer_block(w, block, imap, **kw):
    arr, layer = w
    return arr, pl.BlockSpec((None,) + block, lambda *ids: (layer,) + imap(*ids), **kw)


def _layer_resident(w):
    arr, _ = w
    return _layer_block(w, arr.shape[1:], lambda *_: (0, 0), pipeline_mode=pl.Buffered(1))


def _sigmoid(x):
    return jax.nn.sigmoid(x)


def _dot(a, b):
    return jnp.dot(a, b, preferred_element_type=F32)


def _dot_nt(a, b):
    return lax.dot_general(a, b, (((1,), (1,)), ((), ())), preferred_element_type=F32)


def _dot_tn(a, b):
    return lax.dot_general(a, b, (((0,), (0,)), ((), ())), preferred_element_type=F32)


def _row_mean(z):
    return jnp.mean(z, axis=-1, keepdims=True)


def _lane_mean(z):
    hi = z.astype(BF16)
    lo = (z - hi.astype(F32)).astype(BF16)
    ones = jnp.full((2 * LANES, LANES), 1.0 / LANES, BF16)
    return jnp.dot(jnp.concatenate([hi, lo], axis=-1), ones, preferred_element_type=F32)


def _norm_fwd(z, g, b, mean=_row_mean):
    mu = mean(z)
    xc = z - mu
    var = mean(xc * xc)
    return xc * lax.rsqrt(var + LN_EPS) * g + b


def _norm_bwd(z, g, dy, mean=_row_mean):
    mu = mean(z)
    xc = z - mu
    var = mean(xc * xc)
    rstd = lax.rsqrt(var + LN_EPS)
    xhat = xc * rstd
    dxh = dy * g
    m1 = mean(dxh)
    m2 = mean(dxh * xhat)
    return rstd * (dxh - m1 - xhat * m2), xhat


GELU_K = 0.7978845608028654
GELU_C = 0.044715


def _gelu(x):
    return 0.5 * x * (1.0 + jnp.tanh(GELU_K * (x + GELU_C * x * x * x)))


def _gelu_grad(x):
    t = jnp.tanh(GELU_K * (x + GELU_C * x * x * x))
    return 0.5 * (1.0 + t) + 0.5 * x * (1.0 - t * t) * GELU_K * (1.0 + 3.0 * GELU_C * x * x)


def _softplus(y):
    return jnp.maximum(y, 0.0) + jnp.log1p(jnp.exp(-jnp.abs(y)))


def _neg_expm1(y):
    series = -y * (1.0 + y * (0.5 + y * (1.0 / 6.0 + y * (1.0 / 24.0 + y * (1.0 / 120.0 + y * (1.0 / 720.0))))))
    return jnp.where(y > -0.25, series, 1.0 - jnp.exp(y))


def _colsum(x):
    return jnp.sum(x, axis=0, keepdims=True)


def _shifted_taps(src_ref, base, rows, taps):
    acc = None
    for o, coef in taps:
        term = coef() * src_ref[pl.ds(base + o, rows), :]
        acc = term if acc is None else acc + term
    return acc


def _shifted_corr(src_ref, base, rows, d, acc_ref, offs):
    for k, o in enumerate(offs):
        prod = d * src_ref[pl.ds(base + o, rows), :]
        part = jnp.sum(prod.reshape(rows // SUBLANES, SUBLANES, prod.shape[-1]), axis=0)
        acc_ref[SUBLANES * k:SUBLANES * (k + 1), :] += part


def _front_pad(ktaps):
    return SUBLANES * ((ktaps - 1 + SUBLANES - 1) // SUBLANES)


def _pad_rows(ktaps):
    return _front_pad(ktaps) + SUBLANES


def _ffn_tiles(S, F):
    tm = _divisor_tile(S, 1024, 16)
    tf = _divisor_tile(F, 256, LANES)
    return tm, tf


def _ffn_fwd(x, wgu, wd, g, b, alpha, name):
    S, D = x.shape
    F = wd[0].shape[1]
    tm, tf = _ffn_tiles(S, F)
    nf = F // tf
    wg_arr, wg_spec = _layer_block(wgu, (D, tf), lambda i, j: (0, j))
    wu_arr, wu_spec = _layer_block(wgu, (D, tf), lambda i, j: (0, nf + j))
    wd_arr, wd_spec = _layer_block(wd, (tf, D), lambda i, j: (j, 0))

    def body(x_ref, wg_ref, wu_ref, wd_ref, g_ref, b_ref, y_ref, z_ref, hg_ref, hu_ref, acc_ref, xb_ref):
        j = pl.program_id(1)

        @pl.when(j == 0)
        def _():
            xb_ref[...] = x_ref[...].astype(BF16)
            acc_ref[...] = jnp.zeros_like(acc_ref)

        xb = xb_ref[...]
        hg = _dot(xb, wg_ref[...])
        hu = _dot(xb, wu_ref[...])
        hg_ref[...] = hg
        hu_ref[...] = hu
        a = (hg * _sigmoid(hg) * hu).astype(BF16)
        acc_ref[...] += _dot(a, wd_ref[...])

        @pl.when(j == nf - 1)
        def _():
            z = alpha * x_ref[...] + 0.5 * acc_ref[...]
            z_ref[...] = z
            y_ref[...] = _norm_fwd(z, g_ref[...], b_ref[...])

    blk = 2 * (3 * tm * D * 4 + 2 * tm * tf * 4 + 3 * D * tf * 2) + tm * D * 6 + 3 * tm * tf * 4
    return _streamed_call(
        body, name=name, grid=(S // tm, nf),
        in_specs=[_bs((tm, D), lambda i, j: (i, 0)), wg_spec, wu_spec, wd_spec,
                  _bs((1, D), lambda i, j: (0, 0)), _bs((1, D), lambda i, j: (0, 0))],
        out_specs=[_bs((tm, D), lambda i, j: (i, 0)), _bs((tm, D), lambda i, j: (i, 0)),
                   _bs((tm, tf), lambda i, j: (i, j)), _bs((tm, tf), lambda i, j: (i, j))],
        out_shape=[jax.ShapeDtypeStruct((S, D), F32), jax.ShapeDtypeStruct((S, D), F32),
                   jax.ShapeDtypeStruct((S, F), F32), jax.ShapeDtypeStruct((S, F), F32)],
        scratch_shapes=[pltpu.VMEM((tm, D), F32), pltpu.VMEM((tm, D), BF16)],
        compiler_params=_cparams(blk),
    )(x, wg_arr, wu_arr, wd_arr, g, b)


def _ffn_bwd(dy, z, hg, hu, wgu, wd, g, alpha, name):
    S, D = dy.shape
    F = wd[0].shape[1]
    tm, tf = _ffn_tiles(S, F)
    nf = F // tf
    wg_arr, wg_spec = _layer_block(wgu, (D, tf), lambda i, j: (0, j))
    wu_arr, wu_spec = _layer_block(wgu, (D, tf), lambda i, j: (0, nf + j))
    wd_arr, wd_spec = _layer_block(wd, (tf, D), lambda i, j: (j, 0))

    def body(dy_ref, z_ref, hg_ref, hu_ref, wg_ref, wu_ref, wd_ref, g_ref,
             dx_ref, df_ref, a_ref, dhg_ref, dhu_ref, dg_ref, db_ref, acc_ref):
        i = pl.program_id(0)
        j = pl.program_id(1)

        @pl.when((i == 0) & (j == 0))
        def _():
            dg_ref[...] = jnp.zeros_like(dg_ref)
            db_ref[...] = jnp.zeros_like(db_ref)

        @pl.when(j == 0)
        def _():
            dy_ = dy_ref[...]
            dz, xhat = _norm_bwd(z_ref[...], g_ref[...], dy_)
            dg_ref[...] += _colsum(dy_ * xhat)
            db_ref[...] += _colsum(dy_)
            acc_ref[...] = alpha * dz
            df_ref[...] = (0.5 * dz).astype(BF16)

        da = _dot_nt(df_ref[...], wd_ref[...])
        hg_ = hg_ref[...]
        hu_ = hu_ref[...]
        s = _sigmoid(hg_)
        sl = hg_ * s
        dgate = (da * hu_ * (s * (1.0 + hg_ * (1.0 - s)))).astype(BF16)
        dup = (da * sl).astype(BF16)
        a_ref[...] = (sl * hu_).astype(BF16)
        dhg_ref[...] = dgate
        dhu_ref[...] = dup
        acc_ref[...] += _dot_nt(dgate, wg_ref[...]) + _dot_nt(dup, wu_ref[...])

        @pl.when(j == nf - 1)
        def _():
            dx_ref[...] = acc_ref[...]

    blk = 2 * (2 * tm * D * 4 + tm * D * 2 + 2 * tm * tf * 4 + 3 * tm * tf * 2 + 3 * D * tf * 2) + 3 * tm * D * 4 + 8 * tm * tf * 4
    once = dict(pipeline_mode=pl.Buffered(1))
    return _streamed_call(
        body, name=name, grid=(S // tm, nf),
        in_specs=[_bs((tm, D), lambda i, j: (i, 0), **once), _bs((tm, D), lambda i, j: (i, 0), **once),
                  _bs((tm, tf), lambda i, j: (i, j)), _bs((tm, tf), lambda i, j: (i, j)),
                  wg_spec, wu_spec, wd_spec, _bs((1, D), lambda i, j: (0, 0))],
        out_specs=[_bs((tm, D), lambda i, j: (i, 0)), _bs((tm, D), lambda i, j: (i, 0)),
                   _bs((tm, tf), lambda i, j: (i, j)), _bs((tm, tf), lambda i, j: (i, j)), _bs((tm, tf), lambda i, j: (i, j)),
                   _bs((1, D), lambda i, j: (0, 0)), _bs((1, D), lambda i, j: (0, 0))],
        out_shape=[jax.ShapeDtypeStruct((S, D), F32), jax.ShapeDtypeStruct((S, D), BF16),
                   jax.ShapeDtypeStruct((S, F), BF16), jax.ShapeDtypeStruct((S, F), BF16), jax.ShapeDtypeStruct((S, F), BF16),
                   jax.ShapeDtypeStruct((1, D), F32), jax.ShapeDtypeStruct((1, D), F32)],
        scratch_shapes=[pltpu.VMEM((tm, D), F32)],
        compiler_params=_cparams(blk),
    )(dy, z, hg, hu, wg_arr, wu_arr, wd_arr, g)


def _mm_tn(a, b, name):
    S, M = a.shape
    N = b.shape[1]
    bm = _divisor_tile(M, 1408, LANES)
    bn = _divisor_tile(N, 1408, LANES)
    tk = _divisor_tile(S, 512, 16)
    nk = S // tk

    def body(a_ref, b_ref, o_ref, ob_ref):
        k = pl.program_id(2)

        @pl.when(k == 0)
        def _():
            o_ref[...] = jnp.zeros_like(o_ref)

        o_ref[...] += _dot_tn(a_ref[...].astype(BF16), b_ref[...].astype(BF16))

        @pl.when(k == nk - 1)
        def _():
            ob_ref[...] = o_ref[...].astype(BF16)

    blk = 2 * (tk * bm * a.dtype.itemsize + tk * bn * b.dtype.itemsize + bm * bn * 6) + tk * bm * 4 + bm * bn * 4
    tile = _bs((bm, bn), lambda i, j, k: (i, j))
    return _streamed_call(
        body, name=name, grid=(M // bm, N // bn, nk),
        in_specs=[_bs((tk, bm), lambda i, j, k: (k, i)), _bs((tk, bn), lambda i, j, k: (k, j))],
        out_specs=[tile, tile],
        out_shape=[jax.ShapeDtypeStruct((M, N), F32), jax.ShapeDtypeStruct((M, N), BF16)],
        compiler_params=_cparams(blk),
    )(a, b)


def _mm_tn_pair(a, b0, b1, name):
    S, M = a.shape
    N = b0.shape[1]
    assert b1.shape == b0.shape
    bm = _divisor_tile(M, 1408, LANES)
    bn = _divisor_tile(N, 1408, LANES)
    tk = _divisor_tile(S, 512, 16)
    nb = N // bn
    nk = S // tk

    def body(a_ref, b0_ref, b1_ref, o_ref, ob_ref):
        j = pl.program_id(1)
        k = pl.program_id(2)

        @pl.when(k == 0)
        def _():
            o_ref[...] = jnp.zeros_like(o_ref)

        ab = a_ref[...].astype(BF16)

        @pl.when(j < nb)
        def _():
            o_ref[...] += _dot_tn(ab, b0_ref[...])

        @pl.when(j >= nb)
        def _():
            o_ref[...] += _dot_tn(ab, b1_ref[...])

        @pl.when(k == nk - 1)
        def _():
            ob_ref[...] = o_ref[...].astype(BF16)

    b0_map = lambda i, j, k: (jnp.where(j < nb, k, nk - 1), jnp.minimum(j, nb - 1))
    b1_map = lambda i, j, k: (jnp.where(j >= nb, k, 0), jnp.maximum(j - nb, 0))
    blk = 2 * (tk * bm * a.dtype.itemsize + 2 * tk * bn * 2 + bm * bn * 6) + tk * bm * 4 + bm * bn * 4
    tile = _bs((bm, bn), lambda i, j, k: (i, j))
    return _streamed_call(
        body, name=name, grid=(M // bm, 2 * nb, nk),
        in_specs=[_bs((tk, bm), lambda i, j, k: (k, i)), _bs((tk, bn), b0_map), _bs((tk, bn), b1_map)],
        out_specs=[tile, tile],
        out_shape=[jax.ShapeDtypeStruct((M, 2 * N), F32), jax.ShapeDtypeStruct((M, 2 * N), BF16)],
        compiler_params=_cparams(blk),
    )(a, b0, b1)


def _mix_in(x, wt, bias, sections, name):
    S, D = x.shape
    tm = _divisor_tile(S, 256, 16)
    n = len(sections)

    def body(x_ref, w_ref, b_ref, *o_refs):
        xb = x_ref[...].astype(BF16)
        for (off, width, dtype), o_ref in zip(sections, o_refs):
            o_ref[...] = (_dot_nt(xb, w_ref[off:off + width, :]) + b_ref[:, off:off + width]).astype(dtype)

    total = wt.shape[0]
    blk = 2 * (tm * D * 4 + sum(tm * w * jnp.dtype(dt).itemsize for _, w, dt in sections)) + total * D * 2 + 3 * tm * D * 4
    return _streamed_call(
        body, name=name, grid=(S // tm,),
        in_specs=[_bs((tm, D), lambda i: (i, 0)), _resident((total, D)), _resident((1, total))],
        out_specs=[_bs((tm, w), lambda i: (i, 0)) for _, w, _ in sections],
        out_shape=[jax.ShapeDtypeStruct((S, w), dt) for _, w, dt in sections],
        compiler_params=_cparams(blk),
    )(x, wt, bias)


def _mix_dx(dz, parts, wt, sections, alpha, name):
    S, D = dz.shape
    tm = _divisor_tile(S, 256, 16)
    n = len(parts)

    def body(*refs):
        dz_ref = refs[0]
        p_refs = refs[1:1 + n]
        w_ref = refs[1 + n]
        o_ref = refs[2 + n]
        acc = alpha * dz_ref[...]
        for p_ref, (off, width, _) in zip(p_refs, sections):
            acc = acc + _dot(p_ref[...], w_ref[off:off + width, :])
        o_ref[...] = acc

    widths = [p.shape[1] for p in parts]
    total = wt.shape[0]
    blk = 2 * (2 * tm * D * 4 + sum(tm * w * 2 for w in widths)) + total * D * 2 + 2 * tm * D * 4
    return _streamed_call(
        body, name=name, grid=(S // tm,),
        in_specs=[_bs((tm, D), lambda i: (i, 0))] + [_bs((tm, w), lambda i: (i, 0)) for w in widths]
                 + [_resident((total, D))],
        out_specs=_bs((tm, D), lambda i: (i, 0)),
        out_shape=jax.ShapeDtypeStruct((S, D), F32),
        compiler_params=_cparams(blk),
    )(dz, *parts, wt)


def _conv_branch_fwd(cv, cg, w, b, gg, gb, name):
    S, C = cv.shape
    K = w.shape[0]
    assert C // CONV_GROUPS == LANES
    padf = _front_pad(K)
    R = min(CONV_ROWS, S)
    E = min(EW_ROWS, S)

    def body(cv_ref, cg_ref, w_ref, b_ref, gg_ref, gb_ref, c2_ref, c4_ref, pad_ref):
        pad_ref[0:padf, :] = jnp.zeros((padf, LANES), F32)
        pad_ref[S + padf:S + padf + SUBLANES, :] = jnp.zeros((SUBLANES, LANES), F32)

        def fill(i, carry):
            r = pl.multiple_of(i * E, E)
            pad_ref[pl.ds(r + padf, E), :] = cv_ref[pl.ds(r, E), :] * _sigmoid(cg_ref[pl.ds(r, E), :])
            return carry

        lax.fori_loop(0, S // E, fill, 0)
        taps = [(padf - (K - 1) + k, functools.partial(lambda k: w_ref[k:k + 1, :], k)) for k in range(K)]

        def conv(i, carry):
            r = pl.multiple_of(i * R, R)
            c2_ref[pl.ds(r, R), :] = _shifted_taps(pad_ref, r, R, taps) + b_ref[...]
            return carry

        lax.fori_loop(0, S // R, conv, 0)

        def norm(i, carry):
            r = pl.multiple_of(i * E, E)
            c3 = _norm_fwd(c2_ref[pl.ds(r, E), :], gg_ref[...], gb_ref[...], _lane_mean)
            c4_ref[pl.ds(r, E), :] = (c3 * _sigmoid(c3)).astype(BF16)
            return carry

        lax.fori_loop(0, S // E, norm, 0)

    col = lambda i: (0, i)
    blk = 2 * (3 * S * LANES * 4 + S * LANES * 2) + (S + _pad_rows(K)) * LANES * 4
    return _streamed_call(
        body, name=name, grid=(C // LANES,),
        in_specs=[_bs((S, LANES), col), _bs((S, LANES), col), _bs((K, LANES), col),
                  _bs((1, LANES), col), _bs((1, LANES), col), _bs((1, LANES), col)],
        out_specs=[_bs((S, LANES), col), _bs((S, LANES), col)],
        out_shape=[jax.ShapeDtypeStruct((S, C), F32), jax.ShapeDtypeStruct((S, C), BF16)],
        scratch_shapes=[pltpu.VMEM((S + _pad_rows(K), LANES), F32)],
        compiler_params=_cparams(blk),
    )(cv, cg, w, b, gg, gb)


def _conv_branch_bwd(dc4, c2, cv, cg, w, gg, gb, name):
    S, C = cv.shape
    K = w.shape[0]
    padf = _front_pad(K)
    R = min(CONV_ROWS, S)
    E = min(EW_ROWS, S)

    def body(dc4_ref, c2_ref, cv_ref, cg_ref, w_ref, gg_ref, gb_ref,
             dcv_ref, dcg_ref, dw_ref, dwb_ref, dgg_ref, dgb_ref, scv_ref, scg_ref,
             dpad_ref, cpad_ref, dwacc_ref):
        cpad_ref[0:padf, :] = jnp.zeros((padf, LANES), F32)
        cpad_ref[S + padf:S + padf + SUBLANES, :] = jnp.zeros((SUBLANES, LANES), F32)
        dpad_ref[S:S + padf + SUBLANES, :] = jnp.zeros((padf + SUBLANES, LANES), F32)
        dwacc_ref[...] = jnp.zeros_like(dwacc_ref)
        for ref in (dwb_ref, dgg_ref, dgb_ref, scv_ref, scg_ref):
            ref[...] = jnp.zeros_like(ref)

        def norm_pass(i, carry):
            r = pl.multiple_of(i * E, E)
            g_ = gg_ref[...]
            c2 = c2_ref[pl.ds(r, E), :]
            xc = c2 - _lane_mean(c2)
            rstd = lax.rsqrt(_lane_mean(xc * xc) + LN_EPS)
            xhat = xc * rstd
            c3 = xhat * g_ + gb_ref[...]
            s = _sigmoid(c3)
            dc3 = dc4_ref[pl.ds(r, E), :].astype(F32) * (s * (1.0 + c3 * (1.0 - s)))
            dgg_ref[...] += _colsum(dc3 * xhat)
            dgb_ref[...] += _colsum(dc3)
            dxh = dc3 * g_
            dc2 = rstd * (dxh - _lane_mean(dxh) - xhat * _lane_mean(dxh * xhat))
            dpad_ref[pl.ds(r, E), :] = dc2
            dwb_ref[...] += _colsum(dc2)
            cpad_ref[pl.ds(r + padf, E), :] = cv_ref[pl.ds(r, E), :] * _sigmoid(cg_ref[pl.ds(r, E), :])
            return carry

        lax.fori_loop(0, S // E, norm_pass, 0)
        taps = [(K - 1 - k, functools.partial(lambda k: w_ref[k:k + 1, :], k)) for k in range(K)]
        offs = [padf - (K - 1) + k for k in range(K)]

        def conv_pass(i, carry):
            r = pl.multiple_of(i * R, R)
            dc1 = _shifted_taps(dpad_ref, r, R, taps)
            sg = _sigmoid(cg_ref[pl.ds(r, R), :])
            cv_ = cv_ref[pl.ds(r, R), :]
            dcv = dc1 * sg
            dcg = dc1 * cv_ * sg * (1.0 - sg)
            dcv_ref[pl.ds(r, R), :] = dcv.astype(BF16)
            dcg_ref[pl.ds(r, R), :] = dcg.astype(BF16)
            scv_ref[...] += _colsum(dcv)
            scg_ref[...] += _colsum(dcg)
            _shifted_corr(cpad_ref, r, R, dpad_ref[pl.ds(r, R), :], dwacc_ref, offs)
            return carry

        lax.fori_loop(0, S // R, conv_pass, 0)
        for k in range(K):
            dw_ref[k:k + 1, :] = _colsum(dwacc_ref[SUBLANES * k:SUBLANES * (k + 1), :])

    col = lambda i: (0, i)
    row = jax.ShapeDtypeStruct((1, C), F32)
    blk = 2 * (4 * S * LANES * 4 + 2 * S * LANES * 2) + 2 * (S + _pad_rows(K)) * LANES * 4
    return _streamed_call(
        body, name=name, grid=(C // LANES,),
        in_specs=[_bs((S, LANES), col)] * 4 + [_bs((K, LANES), col), _bs((1, LANES), col), _bs((1, LANES), col)],
        out_specs=[_bs((S, LANES), col), _bs((S, LANES), col), _bs((K, LANES), col)] + [_bs((1, LANES), col)] * 5,
        out_shape=[jax.ShapeDtypeStruct((S, C), BF16), jax.ShapeDtypeStruct((S, C), BF16),
                   jax.ShapeDtypeStruct((K, C), F32), row, row, row, row, row],
        scratch_shapes=[pltpu.VMEM((S + _pad_rows(K), LANES), F32), pltpu.VMEM((S + _pad_rows(K), LANES), F32),
                        pltpu.VMEM((SUBLANES * K, LANES), F32)],
        compiler_params=_cparams(blk),
    )(dc4, c2, cv, cg, w, gg, gb)


def _short_conv_fwd(xin, w, b, name):
    S, C = xin.shape
    K = w.shape[0]
    padf = _front_pad(K)
    R = min(CONV_ROWS, S)
    E = min(EW_ROWS, S)

    def body(x_ref, w_ref, b_ref, o_ref, pad_ref):
        pad_ref[0:padf, :] = jnp.zeros((padf, LANES), F32)
        pad_ref[S + padf:S + padf + SUBLANES, :] = jnp.zeros((SUBLANES, LANES), F32)

        def fill(i, carry):
            r = pl.multiple_of(i * E, E)
            pad_ref[pl.ds(r + padf, E), :] = x_ref[pl.ds(r, E), :]
            return carry

        lax.fori_loop(0, S // E, fill, 0)
        taps = [(padf - (K - 1) + k, functools.partial(lambda k: w_ref[k:k + 1, :], k)) for k in range(K)]

        def conv(i, carry):
            r = pl.multiple_of(i * R, R)
            o_ref[pl.ds(r, R), :] = _shifted_taps(pad_ref, r, R, taps) + b_ref[...]
            return carry

        lax.fori_loop(0, S // R, conv, 0)

    col = lambda i: (0, i)
    blk = 2 * (2 * S * LANES * 4) + (S + _pad_rows(K)) * LANES * 4
    return _streamed_call(
        body, name=name, grid=(C // LANES,),
        in_specs=[_bs((S, LANES), col), _bs((K, LANES), col), _bs((1, LANES), col)],
        out_specs=_bs((S, LANES), col),
        out_shape=jax.ShapeDtypeStruct((S, C), F32),
        scratch_shapes=[pltpu.VMEM((S + _pad_rows(K), LANES), F32)],
        compiler_params=_cparams(blk),
    )(xin, w, b)


def _short_conv_bwd(dy, xin, w, name):
    S, C = xin.shape
    K = w.shape[0]
    padf = _front_pad(K)
    R = min(CONV_ROWS, S)
    E = min(EW_ROWS, S)

    def body(dy_ref, x_ref, w_ref, dx_ref, dw_ref, db_ref, sx_ref, dpad_ref, xpad_ref, dwacc_ref):
        xpad_ref[0:padf, :] = jnp.zeros((padf, LANES), F32)
        xpad_ref[S + padf:S + padf + SUBLANES, :] = jnp.zeros((SUBLANES, LANES), F32)
        dpad_ref[S:S + padf + SUBLANES, :] = jnp.zeros((padf + SUBLANES, LANES), F32)
        dwacc_ref[...] = jnp.zeros_like(dwacc_ref)
        db_ref[...] = jnp.zeros_like(db_ref)
        sx_ref[...] = jnp.zeros_like(sx_ref)

        def fill(i, carry):
            r = pl.multiple_of(i * E, E)
            d = dy_ref[pl.ds(r, E), :]
            dpad_ref[pl.ds(r, E), :] = d
            db_ref[...] += _colsum(d)
            xpad_ref[pl.ds(r + padf, E), :] = x_ref[pl.ds(r, E), :]
            return carry

        lax.fori_loop(0, S // E, fill, 0)
        taps = [(K - 1 - k, functools.partial(lambda k: w_ref[k:k + 1, :], k)) for k in range(K)]
        offs = [padf - (K - 1) + k for k in range(K)]

        def conv_pass(i, carry):
            r = pl.multiple_of(i * R, R)
            dx = _shifted_taps(dpad_ref, r, R, taps)
            dx_ref[pl.ds(r, R), :] = dx.astype(BF16)
            sx_ref[...] += _colsum(dx)
            _shifted_corr(xpad_ref, r, R, dpad_ref[pl.ds(r, R), :], dwacc_ref, offs)
            return carry

        lax.fori_loop(0, S // R, conv_pass, 0)
        for k in range(K):
            dw_ref[k:k + 1, :] = _colsum(dwacc_ref[SUBLANES * k:SUBLANES * (k + 1), :])

    col = lambda i: (0, i)
    row = jax.ShapeDtypeStruct((1, C), F32)
    blk = 2 * (2 * S * LANES * 4 + S * LANES * 2) + 2 * (S + _pad_rows(K)) * LANES * 4
    return _streamed_call(
        body, name=name, grid=(C // LANES,),
        in_specs=[_bs((S, LANES), col), _bs((S, LANES), col), _bs((K, LANES), col)],
        out_specs=[_bs((S, LANES), col), _bs((K, LANES), col), _bs((1, LANES), col), _bs((1, LANES), col)],
        out_shape=[jax.ShapeDtypeStruct((S, C), BF16), jax.ShapeDtypeStruct((K, C), F32), row, row],
        scratch_shapes=[pltpu.VMEM((S + _pad_rows(K), LANES), F32), pltpu.VMEM((S + _pad_rows(K), LANES), F32),
                        pltpu.VMEM((SUBLANES * K, LANES), F32)],
        compiler_params=_cparams(blk),
    )(dy, xin, w)


def _band_panels(width, block):
    assert width % LANES == 0 and block <= LANES
    panels = []
    for c0 in range(0, width, 2 * LANES):
        c1 = min(width, c0 + 2 * LANES)
        r0 = (c0 // block) * block // LANES * LANES
        r1 = min(width, -(-(-(-c1 // block) * block) // LANES) * LANES)
        panels.append((r0, r1, c0, c1))
    return panels


def _gates_fwd(r1, wa, wx, ba, bx, lam, name):
    S, R = r1.shape
    tm = _divisor_tile(S, 256, 16)
    panels = _band_panels(R, R // RNN_BLOCKS)

    def body(r1_ref, wa_ref, wx_ref, ba_ref, bx_ref, lam_ref, ra_ref, ri_ref, a_ref, uu_ref):
        for r0, r1e, c0, c1 in panels:
            rb = r1_ref[:, r0:r1e].astype(BF16)
            ra = _sigmoid(_dot(rb, wa_ref[r0:r1e, c0:c1]) + ba_ref[:, c0:c1])
            ri = _sigmoid(_dot(rb, wx_ref[r0:r1e, c0:c1]) + bx_ref[:, c0:c1])
            log_a = -RG_LRU_C * ra * _softplus(-lam_ref[:, c0:c1])
            ra_ref[:, c0:c1] = ra.astype(BF16)
            ri_ref[:, c0:c1] = ri.astype(BF16)
            a_ref[:, c0:c1] = jnp.exp(log_a)
            uu_ref[:, c0:c1] = jnp.sqrt(_neg_expm1(2.0 * log_a)) * (ri * r1_ref[:, c0:c1])

    blk = 2 * (5 * tm * R * 4) + 2 * R * R * 2 + 6 * tm * R * 4
    tile = _bs((tm, R), lambda i: (i, 0))
    return _streamed_call(
        body, name=name, grid=(S // tm,),
        in_specs=[tile, _resident((R, R)), _resident((R, R)), _resident((1, R)), _resident((1, R)), _resident((1, R))],
        out_specs=[tile] * 4,
        out_shape=[jax.ShapeDtypeStruct((S, R), BF16)] * 2 + [jax.ShapeDtypeStruct((S, R), F32)] * 2,
        compiler_params=_cparams(blk),
    )(r1, wa, wx, ba, bx, lam)


def _gates_bwd(guu, da, ra, ri, r1, wa, wx, lam, name):
    S, R = r1.shape
    tm = _divisor_tile(S, 256, 16)
    nsteps = S // tm
    panels = _band_panels(R, R // RNN_BLOCKS)

    def body(g_ref, da_ref, ra_ref, ri_ref, r1_ref, wa_ref, wx_ref, lam_ref,
             dr1_ref, dpa_ref, dpx_ref, dba_ref, dbx_ref, dlam_ref):
        i = pl.program_id(0)

        @pl.when(i == 0)
        def _():
            dba_ref[...] = jnp.zeros_like(dba_ref)
            dbx_ref[...] = jnp.zeros_like(dbx_ref)
            dlam_ref[...] = jnp.zeros_like(dlam_ref)

        g = g_ref[...]
        ra = ra_ref[...].astype(F32)
        ri = ri_ref[...].astype(F32)
        r1_ = r1_ref[...]
        sp = _softplus(-lam_ref[...])
        log_a = -RG_LRU_C * ra * sp
        a = jnp.exp(log_a)
        mult = jnp.sqrt(_neg_expm1(2.0 * log_a))
        d_ri = g * mult * r1_
        dr1 = g * mult * ri
        dmult = g * ri * r1_
        dlog_a = da_ref[...] * a - dmult * (a * a) / mult
        dra = dlog_a * (-RG_LRU_C * sp)
        dlam_ref[...] += _colsum(dlog_a * (-RG_LRU_C * ra))
        dpa = dra * ra * (1.0 - ra)
        dpx = d_ri * ri * (1.0 - ri)
        dba_ref[...] += _colsum(dpa)
        dbx_ref[...] += _colsum(dpx)
        dpa_b = dpa.astype(BF16)
        dpx_b = dpx.astype(BF16)
        dpa_ref[...] = dpa_b
        dpx_ref[...] = dpx_b
        dr1_ref[...] = dr1
        for k0, k1, c0, c1 in panels:
            dr1_ref[:, c0:c1] += (_dot_nt(dpa_ref[:, k0:k1], wa_ref[c0:c1, k0:k1])
                                  + _dot_nt(dpx_ref[:, k0:k1], wx_ref[c0:c1, k0:k1]))

        @pl.when(i == nsteps - 1)
        def _():
            dlam_ref[...] = dlam_ref[...] * (-_sigmoid(-lam_ref[...]))

    blk = 2 * (6 * tm * R * 4 + 2 * tm * R * 2) + 2 * R * R * 2 + 10 * tm * R * 4
    tile = _bs((tm, R), lambda i: (i, 0))
    rowspec = _bs((1, R), lambda i: (0, 0))
    row = jax.ShapeDtypeStruct((1, R), F32)
    return _streamed_call(
        body, name=name, grid=(nsteps,),
        in_specs=[tile] * 5 + [_resident((R, R)), _resident((R, R)), _resident((1, R))],
        out_specs=[tile, tile, tile, rowspec, rowspec, rowspec],
        out_shape=[jax.ShapeDtypeStruct((S, R), F32), jax.ShapeDtypeStruct((S, R), BF16), jax.ShapeDtypeStruct((S, R), BF16),
                   row, row, row],
        compiler_params=_cparams(blk),
    )(guu, da, ra, ri, r1, wa, wx, lam)


def _embed_blocks(w, name):
    H, bk, _ = w.shape

    def body(w_ref, o_ref):
        o_ref[...] = jnp.zeros_like(o_ref)
        for h in range(H):
            o_ref[bk * h:bk * (h + 1), bk * h:bk * (h + 1)] = w_ref[h].astype(BF16)

    return pl.pallas_call(body, name=name, out_shape=jax.ShapeDtypeStruct((H * bk, H * bk), BF16),
                          compiler_params=_cparams(3 * H * bk * H * bk * 2))(w)


def _block_grads(r1, dpa, dpx, name):
    S, R = r1.shape
    bk = R // RNN_BLOCKS
    tk = _divisor_tile(S, 512, 16)
    nsteps = S // tk
    panels = _band_panels(R, bk)

    def body(r1_ref, dpa_ref, dpx_ref, ga_ref, gx_ref, acca_ref, accx_ref):
        k = pl.program_id(0)

        @pl.when(k == 0)
        def _():
            acca_ref[...] = jnp.zeros_like(acca_ref)
            accx_ref[...] = jnp.zeros_like(accx_ref)

        for k0, k1, c0, c1 in panels:
            rb = r1_ref[:, k0:k1].astype(BF16)
            acca_ref[k0:k1, c0:c1] += _dot_tn(rb, dpa_ref[:, c0:c1])
            accx_ref[k0:k1, c0:c1] += _dot_tn(rb, dpx_ref[:, c0:c1])

        @pl.when(k == nsteps - 1)
        def _():
            for h in range(RNN_BLOCKS):
                ga_ref[h] = acca_ref[bk * h:bk * (h + 1), bk * h:bk * (h + 1)]
                gx_ref[h] = accx_ref[bk * h:bk * (h + 1), bk * h:bk * (h + 1)]

    tile = lambda: _bs((tk, R), lambda k: (k, 0))
    out = _bs((RNN_BLOCKS, bk, bk), lambda k: (0, 0, 0))
    sds = jax.ShapeDtypeStruct((RNN_BLOCKS, bk, bk), F32)
    return _streamed_call(
        body, name=name, grid=(nsteps,),
        in_specs=[tile(), tile(), tile()], out_specs=[out, out], out_shape=[sds, sds],
        scratch_shapes=[pltpu.VMEM((R, R), F32), pltpu.VMEM((R, R), F32)],
        compiler_params=_cparams(2 * (tk * R * 8) + 2 * R * R * 4 + 4 * tk * R * 4),
    )(r1, dpa, dpx)


def _scan_geometry(S):
    nseg = SCAN_SEGMENTS if S % (SCAN_SEGMENTS * SUBLANES) == 0 else SUBLANES
    return nseg, S // nseg


def _steps(n, step, init):
    u = SCAN_UNROLL

    def trip(t, carry):
        for k in range(u):
            carry = step(t * u + k, carry)
        return carry

    carry = lax.fori_loop(0, n // u, trip, init)
    for j in range(n - n % u, n):
        carry = step(j, carry)
    return carry


def _scan_fwd(a, u, name):
    S, C = a.shape
    nseg, L = _scan_geometry(S)
    T = min(SUBLANES, L)

    def body(a3, u3, h3, ta_ref, tu_ref, e_ref, p_ref, init_ref):

        def to_steps(i, carry):
            j0 = pl.multiple_of(i * T, T)
            ta_ref[pl.ds(j0, T)] = jnp.swapaxes(a3[:, pl.ds(j0, T), :], 0, 1)
            tu_ref[pl.ds(j0, T)] = jnp.swapaxes(u3[:, pl.ds(j0, T), :], 0, 1)
            return carry

        lax.fori_loop(0, L // T, to_steps, 0)

        def run1(j, carry):
            hs, ps = carry
            aj = ta_ref[j]
            return aj * hs + tu_ref[j], aj * ps

        e_ref[...], p_ref[...] = _steps(L, run1, (jnp.zeros((nseg, LANES), F32), jnp.ones((nseg, LANES), F32)))
        init_ref[0:1, :] = jnp.zeros((1, LANES), F32)
        for s in range(1, nseg):
            init_ref[s:s + 1, :] = e_ref[s - 1:s, :] + p_ref[s - 1:s, :] * init_ref[s - 1:s, :]

        def run2(j, hs):
            hs = ta_ref[j] * hs + tu_ref[j]
            tu_ref[j] = hs
            return hs

        _steps(L, run2, init_ref[...])

        def from_steps(i, carry):
            j0 = pl.multiple_of(i * T, T)
            h3[:, pl.ds(j0, T), :] = jnp.swapaxes(tu_ref[pl.ds(j0, T)], 0, 1)
            return carry

        lax.fori_loop(0, L // T, from_steps, 0)

    seg_block = _bs((nseg, L, LANES), lambda i: (0, 0, i))
    blk = 2 * (3 * S * LANES * 4) + 2 * S * LANES * 4
    return _streamed_call(
        body, name=name, grid=(C // LANES,),
        in_specs=[seg_block, seg_block],
        out_specs=seg_block,
        out_shape=jax.ShapeDtypeStruct((nseg, L, C), F32),
        scratch_shapes=[pltpu.VMEM((L, nseg, LANES), F32)] * 2 + [pltpu.VMEM((nseg, LANES), F32)] * 3,
        compiler_params=_cparams(blk),
    )(a.reshape(nseg, L, C), u.reshape(nseg, L, C)).reshape(S, C)


def _scan_bwd(a, dh, h, name):
    S, C = a.shape
    nseg, L = _scan_geometry(S)
    T = min(SUBLANES, L)
    assert L >= 2

    def body(a3, d3, h3, g3, da3, ta_ref, td_ref, th_ref, e_ref, p_ref, init_ref):

        def to_steps(i, carry):
            j0 = pl.multiple_of(i * T, T)
            for src, dst in ((a3, ta_ref), (d3, td_ref), (h3, th_ref)):
                dst[pl.ds(j0, T)] = jnp.swapaxes(src[:, pl.ds(j0, T), :], 0, 1)
            return carry

        lax.fori_loop(0, L // T, to_steps, 0)
        seg = lax.broadcasted_iota(jnp.int32, (nseg, LANES), 0)
        b_last = jnp.where(seg == nseg - 1, 0.0, pltpu.roll(ta_ref[0], nseg - 1, axis=0))
        h_first = jnp.where(seg == 0, 0.0, pltpu.roll(th_ref[L - 1], 1, axis=0))

        def run1(jj, carry):
            gs, ps = carry
            j = L - 2 - jj
            bj = ta_ref[j + 1]
            return bj * gs + td_ref[j], bj * ps

        e_ref[...], p_ref[...] = _steps(L - 1, run1, (td_ref[L - 1], b_last))
        init_ref[nseg - 1:nseg, :] = jnp.zeros((1, LANES), F32)
        for s in range(nseg - 2, -1, -1):
            init_ref[s:s + 1, :] = e_ref[s + 1:s + 2, :] + p_ref[s + 1:s + 2, :] * init_ref[s + 1:s + 2, :]

        gs = b_last * init_ref[...] + td_ref[L - 1]
        td_ref[L - 1] = gs
        th_ref[L - 1] = gs * th_ref[L - 2]

        def run2(jj, gs):
            j = L - 2 - jj
            gs = ta_ref[j + 1] * gs + td_ref[j]
            td_ref[j] = gs
            th_ref[j] = gs * th_ref[j - 1]
            return gs

        gs = _steps(L - 2, run2, gs)
        gs = ta_ref[1] * gs + td_ref[0]
        td_ref[0] = gs
        th_ref[0] = gs * h_first

        def from_steps(i, carry):
            j0 = pl.multiple_of(i * T, T)
            g3[:, pl.ds(j0, T), :] = jnp.swapaxes(td_ref[pl.ds(j0, T)], 0, 1)
            da3[:, pl.ds(j0, T), :] = jnp.swapaxes(th_ref[pl.ds(j0, T)], 0, 1)
            return carry

        lax.fori_loop(0, L // T, from_steps, 0)

    seg_block = _bs((nseg, L, LANES), lambda i: (0, 0, i))
    blk = 2 * (5 * S * LANES * 4) + 3 * S * LANES * 4
    g, da = _streamed_call(
        body, name=name, grid=(C // LANES,),
        in_specs=[seg_block] * 3,
        out_specs=[seg_block] * 2,
        out_shape=[jax.ShapeDtypeStruct((nseg, L, C), F32)] * 2,
        scratch_shapes=[pltpu.VMEM((L, nseg, LANES), F32)] * 3 + [pltpu.VMEM((nseg, LANES), F32)] * 3,
        compiler_params=_cparams(blk),
    )(a.reshape(nseg, L, C), dh.reshape(nseg, L, C), h.reshape(nseg, L, C))
    return g.reshape(S, C), da.reshape(S, C)


def _mixer_out_fwd(c4, h, rg, gc, gr, x1, wc, wr, wo, g, b, alpha, name):
    S, D = x1.shape
    R = h.shape[1]
    tm = _divisor_tile(S, 256, 16)

    def body(c4_ref, h_ref, rg_ref, gc_ref, gr_ref, x_ref, wc_ref, wr_ref, wo_ref, g_ref, b_ref,
             yc_ref, yr_ref, z_ref, y_ref):
        yc = _dot(c4_ref[...], wc_ref[...])
        q = (h_ref[...] * _gelu(rg_ref[...].astype(F32))).astype(BF16)
        yr = _dot(q, wr_ref[...])
        yc_ref[...] = yc.astype(BF16)
        yr_ref[...] = yr.astype(BF16)
        m = (_sigmoid(gc_ref[...].astype(F32)) * yc + _sigmoid(gr_ref[...].astype(F32)) * yr).astype(BF16)
        z = alpha * x_ref[...] + _dot(m, wo_ref[...])
        z_ref[...] = z
        y_ref[...] = _norm_fwd(z, g_ref[...], b_ref[...])

    blk = 2 * (tm * D * 2 + 2 * tm * R * 4 + 7 * tm * D * 4) + (2 * D * D + R * D) * 2 + 6 * tm * D * 4
    td = _bs((tm, D), lambda i: (i, 0))
    tr = _bs((tm, R), lambda i: (i, 0))
    return _streamed_call(
        body, name=name, grid=(S // tm,),
        in_specs=[td, tr, tr, td, td, td, _layer_resident(wc)[1], _layer_resident(wr)[1], _layer_resident(wo)[1],
                  _resident((1, D)), _resident((1, D))],
        out_specs=[td] * 4,
        out_shape=[jax.ShapeDtypeStruct((S, D), BF16)] * 2 + [jax.ShapeDtypeStruct((S, D), F32)] * 2,
        compiler_params=_cparams(blk),
    )(c4, h, rg, gc, gr, x1, wc[0], wr[0], wo[0], g, b)


def _mixer_out_bwd(dy, z, g, wo, yc, yr, gc, gr, name):
    S, D = dy.shape
    tm = _divisor_tile(S, 256, 16)

    def body(dy_ref, z_ref, g_ref, wo_ref, yc_ref, yr_ref, gc_ref, gr_ref,
             dz_ref, dzb_ref, m_ref, dyc_ref, dyr_ref, dgc_ref, dgr_ref, sgc_ref, sgr_ref, dg_ref, db_ref):
        @pl.when(pl.program_id(0) == 0)
        def _():
            for ref in (sgc_ref, sgr_ref, dg_ref, db_ref):
                ref[...] = jnp.zeros_like(ref)

        dy_ = dy_ref[...]
        dz, xhat = _norm_bwd(z_ref[...], g_ref[...], dy_)
        dg_ref[...] += _colsum(dy_ * xhat)
        db_ref[...] += _colsum(dy_)
        dz_ref[...] = dz
        dzb = dz.astype(BF16)
        dzb_ref[...] = dzb
        dm = _dot_nt(dzb, wo_ref[...])
        yc = yc_ref[...].astype(F32)
        yr = yr_ref[...].astype(F32)
        sc = _sigmoid(gc_ref[...].astype(F32))
        sr = _sigmoid(gr_ref[...].astype(F32))
        m_ref[...] = (sc * yc + sr * yr).astype(BF16)
        dyc_ref[...] = (dm * sc).astype(BF16)
        dyr_ref[...] = (dm * sr).astype(BF16)
        dgc = dm * yc * sc * (1.0 - sc)
        dgr = dm * yr * sr * (1.0 - sr)
        dgc_ref[...] = dgc.astype(BF16)
        dgr_ref[...] = dgr.astype(BF16)
        sgc_ref[...] += _colsum(dgc)
        sgr_ref[...] += _colsum(dgr)

    blk = 2 * (7 * tm * D * 4 + 6 * tm * D * 2) + D * D * 2 + 8 * tm * D * 4
    td = _bs((tm, D), lambda i: (i, 0))
    rowspec = _bs((1, D), lambda i: (0, 0))
    row = jax.ShapeDtypeStruct((1, D), F32)
    bfd = jax.ShapeDtypeStruct((S, D), BF16)
    return _streamed_call(
        body, name=name, grid=(S // tm,),
        in_specs=[td, td, _resident((1, D)), _layer_resident(wo)[1], td, td, td, td],
        out_specs=[td] * 7 + [rowspec] * 4,
        out_shape=[jax.ShapeDtypeStruct((S, D), F32), bfd, bfd, bfd, bfd, bfd, bfd, row, row, row, row],
        compiler_params=_cparams(blk),
    )(dy, z, g, wo[0], yc, yr, gc, gr)


def _branch_bwd(dyc, dyr, wc, wr, h, rg, name):
    S, D = dyc.shape
    R = h.shape[1]
    tm = _divisor_tile(S, 256, 16)

    def body(dyc_ref, dyr_ref, wc_ref, wr_ref, h_ref, rg_ref, dc4_ref, dh_ref, drg_ref, q_ref, srg_ref):
        @pl.when(pl.program_id(0) == 0)
        def _():
            srg_ref[...] = jnp.zeros_like(srg_ref)

        dc4_ref[...] = _dot_nt(dyc_ref[...], wc_ref[...]).astype(BF16)
        dq = _dot_nt(dyr_ref[...], wr_ref[...])
        h_ = h_ref[...]
        rg_ = rg_ref[...].astype(F32)
        ge = _gelu(rg_)
        dh_ref[...] = dq * ge
        drg = dq * h_ * _gelu_grad(rg_)
        drg_ref[...] = drg.astype(BF16)
        srg_ref[...] += _colsum(drg)
        q_ref[...] = (h_ * ge).astype(BF16)

    blk = 2 * (2 * tm * D * 2 + tm * D * 4 + 3 * tm * R * 4 + 2 * tm * R * 2) + (D * D + R * D) * 2 + 6 * tm * R * 4
    td = _bs((tm, D), lambda i: (i, 0))
    tr = _bs((tm, R), lambda i: (i, 0))
    return _streamed_call(
        body, name=name, grid=(S // tm,),
        in_specs=[td, td, _layer_resident(wc)[1], _layer_resident(wr)[1], tr, tr],
        out_specs=[td, tr, tr, tr, _bs((1, R), lambda i: (0, 0))],
        out_shape=[jax.ShapeDtypeStruct((S, D), BF16), jax.ShapeDtypeStruct((S, R), F32), jax.ShapeDtypeStruct((S, R), BF16),
                   jax.ShapeDtypeStruct((S, R), BF16), jax.ShapeDtypeStruct((1, R), F32)],
        compiler_params=_cparams(blk),
    )(dyc, dyr, wc[0], wr[0], h, rg)


def _loss_head(y, target, name):
    S, D = y.shape
    tm = _divisor_tile(S, 512, 16)
    nsteps = S // tm

    def body(y_ref, t_ref, loss_ref, dy_ref, acc_ref):
        i = pl.program_id(0)

        @pl.when(i == 0)
        def _():
            acc_ref[...] = jnp.zeros_like(acc_ref)

        err = y_ref[...] - t_ref[...]
        dy_ref[...] = err * (1.0 / D)
        acc_ref[...] += _colsum(err * err)

        @pl.when(i == nsteps - 1)
        def _():
            loss_ref[...] = jnp.sum(acc_ref[...], axis=-1, keepdims=True) * (0.5 / D)

    td = _bs((tm, D), lambda i: (i, 0))
    return _streamed_call(
        body, name=name, grid=(nsteps,),
        in_specs=[td, td],
        out_specs=[_bs((1, 1), lambda i: (0, 0)), td],
        out_shape=[jax.ShapeDtypeStruct((1, 1), F32), jax.ShapeDtypeStruct((S, D), F32)],
        scratch_shapes=[pltpu.VMEM((1, D), F32)],
        compiler_params=_cparams(2 * 3 * tm * D * 4),
    )(y, target)


def _adamw_math(w, g, m, v):
    m = ADAM_B1 * m + (1.0 - ADAM_B1) * g
    v = ADAM_B2 * v + (1.0 - ADAM_B2) * (g * g)
    m_hat = m / (1.0 - ADAM_B1 ** ADAM_STEP)
    v_hat = v / (1.0 - ADAM_B2 ** ADAM_STEP)
    delta = -ADAM_LR * (m_hat / (jnp.sqrt(v_hat) + ADAM_EPS) + ADAM_WD * w)
    return delta, m, v


def _adamw_sharded(w, m, v, own, sib, rem, layer, filled, name):
    layers, r, c = w.shape
    r2 = r // 2
    tr = _divisor_tile(r2, max(16, (1 << 20) // (4 * c) // 16 * 16), 16)
    n_out = 4

    def body(w_ref, m_ref, v_ref, own_ref, sib_ref, rem_ref, *rest):
        g_ref, d_ref, nm_ref, nv_ref = rest[-n_out:]
        mine = pl.program_id(0) == lax.axis_index("c")
        g = jnp.where(mine, own_ref[...], sib_ref[...]).astype(F32)
        for j in range(N_CHIPS - 1):
            g = g + rem_ref[j].astype(F32)
        delta, nm, nv = _adamw_math(w_ref[...], g, m_ref[...], v_ref[...])
        g_ref[...] = g
        d_ref[...] = delta
        nm_ref[...] = nm
        nv_ref[...] = nv

    halves = lambda a: a.reshape(layers, 2, r2, c)
    tile = _bs((None, None, tr, c), lambda h, i: (layer, h, i, 0))
    flat = _bs((tr, c), lambda h, i: (i, 0))
    sds = jax.ShapeDtypeStruct((layers, 2, r2, c), F32)
    passed = [] if filled is None else [halves(a) for a in filled]
    outs = _streamed_call(
        body, name=name, grid=(2, r2 // tr),
        in_specs=[tile, tile, tile, flat, flat, _bs((N_CHIPS - 1, None, tr, c), lambda h, i: (0, h, i, 0))] + [ANY] * len(passed),
        out_specs=[tile] * n_out,
        out_shape=[sds] * n_out,
        input_output_aliases={6 + k: k for k in range(len(passed))},
        compiler_params=_cparams(2 * (7 * tr * c * 4 + (N_CHIPS + 1) * tr * c * 2) + 6 * tr * c * 4),
    )(halves(w), halves(m), halves(v), own, sib, rem, *passed)
    return [o.reshape(layers, r, c) for o in outs]


def _adamw_flat(w, m, v, g, name):
    rows = w.shape[0]
    tr = _divisor_tile(rows, 1024, SUBLANES)

    def body(w_ref, m_ref, v_ref, g_ref, d_ref, nm_ref, nv_ref):
        delta, nm, nv = _adamw_math(w_ref[...], g_ref[...], m_ref[...], v_ref[...])
        d_ref[...] = delta
        nm_ref[...] = nm
        nv_ref[...] = nv

    tile = _bs((tr, LANES), lambda i: (i, 0))
    sds = jax.ShapeDtypeStruct(w.shape, F32)
    return _streamed_call(
        body, name=name, grid=(rows // tr,),
        in_specs=[tile] * 4, out_specs=[tile] * 3, out_shape=[sds] * 3,
        compiler_params=_cparams(2 * 7 * tr * LANES * 4),
    )(w, m, v, g)


def _half_view(g, by_cols):
    rows, cols = g.shape
    if by_cols:
        return g.reshape(2, rows // 2, cols)
    return g.reshape(N_CHIPS, 2, rows // (2 * N_CHIPS), cols)


def _pair_sum_bf16(view, theirs, by_cols, name):
    rows, c = theirs.shape[-2:]
    tr = _divisor_tile(rows, max(16, (1 << 20) // (4 * c) // 16 * 16), 16)

    def body(g0_ref, g1_ref, t_ref, o_ref):
        mine = jnp.where(lax.axis_index("c") == 0, g0_ref[...], g1_ref[...])
        o_ref[...] = (mine + t_ref[...].astype(F32)).astype(BF16)

    if by_cols:
        grid = (rows // tr,)
        halves = [_bs((None, tr, c), functools.partial(lambda h, i: (h, i, 0), h)) for h in range(2)]
        tile = _bs((tr, c), lambda i: (i, 0))
    else:
        grid = (N_CHIPS, rows // tr)
        halves = [_bs((None, None, tr, c), functools.partial(lambda h, k, i: (k, h, i, 0), h)) for h in range(2)]
        tile = _bs((None, tr, c), lambda k, i: (k, i, 0))
    return _streamed_call(
        body, name=name, grid=grid,
        in_specs=halves + [tile], out_specs=tile, out_shape=jax.ShapeDtypeStruct(theirs.shape, BF16),
        compiler_params=_cparams(2 * 4 * tr * c * 4),
    )(view, view, theirs)


ANY = pl.BlockSpec(memory_space=pl.ANY)


def _mesh_position():
    return lax.axis_index("x"), lax.axis_index("y"), lax.axis_index("c")


def _other_chips():
    x, y, c = _mesh_position()
    chips = [(1 - x, y), (x, 1 - y), (1 - x, 1 - y)]
    return 2 * x + y, (x, y, 1 - c), chips, [2 * cx + cy for cx, cy in chips]


def _chip_slab(ref, k, width, by_cols):
    if by_cols:
        start = k * width if isinstance(k, int) else pl.multiple_of(k * width, LANES)
        return ref.at[:, pl.ds(start, width)]
    return ref.at[k]


HBM = pl.BlockSpec(memory_space=pltpu.HBM)
SEM = pl.BlockSpec(memory_space=pltpu.SEMAPHORE)
DATAFLOW = pltpu.SideEffectType.DATAFLOW_SIDE_EFFECTING
N_GATHER_COPIES = 4


def _land_shape(src, by_cols):
    return src.shape[:-1] + (N_CHIPS * src.shape[-1],) if by_cols else (N_CHIPS,) + src.shape


def _gather_copy(src_ref, land_ref, by_cols, send_sems, recv_sems, pos, j, slab, to):
    width = src_ref.shape[-1]
    return pltpu.make_async_remote_copy(src_ref=src_ref, dst_ref=_chip_slab(land_ref, slab, width, by_cols),
                                        send_sem=send_sems.at[N_GATHER_COPIES * pos + j],
                                        recv_sem=recv_sems.at[N_GATHER_COPIES * pos + j],
                                        device_id=to, device_id_type=MESH)


def _gather_start(srcs, by_cols, groups, name):
    U = len(srcs)
    G = len(groups)
    lands = [lax.empty(_land_shape(s, bc), s.dtype) for s, bc in zip(srcs, by_cols)]

    def body(*refs):
        src = refs[:U]
        land = refs[U:2 * U]
        send_sems = refs[2 * U:2 * U + G]
        recv_sems = refs[2 * U + G:2 * U + 2 * G]
        token = refs[-1]
        c = lax.axis_index("c")
        me, sibling, chips, _ = _other_chips()
        targets = [(*chip, c) for chip in chips] + [sibling]
        for g, members in enumerate(groups):
            for pos, u in enumerate(members):
                for j, to in enumerate(targets):
                    _gather_copy(src[u], land[u], by_cols[u], send_sems[g], recv_sems[g], pos, j, me, to).start()
        token[...] = jnp.zeros_like(token)

    sem_shapes = [pltpu.SemaphoreType.DMA((len(m) * N_GATHER_COPIES,)) for m in groups]
    outs = pl.pallas_call(
        body, name=name,
        out_shape=tuple(sem_shapes + sem_shapes + [pltpu.HBM(s.shape, s.dtype) for s in srcs]
                        + [pltpu.HBM(v.shape, v.dtype) for v in lands] + [jax.ShapeDtypeStruct((SUBLANES, LANES), F32)]),
        in_specs=[HBM] * (2 * U),
        out_specs=tuple([SEM] * (2 * G) + [HBM] * (2 * U) + [pl.BlockSpec(memory_space=pltpu.VMEM)]),
        input_output_aliases={i: 2 * G + i for i in range(2 * U)},
        compiler_params=pltpu.CompilerParams(has_side_effects=DATAFLOW),
    )(*[pltpu.with_memory_space_constraint(a, pltpu.HBM) for a in list(srcs) + lands])
    return outs[:G], outs[G:2 * G], outs[2 * G:2 * G + U], outs[2 * G + U:2 * G + 2 * U]


def _gather_wait(srcs, lands, by_cols, send_sems, recv_sems, after, name):
    n = len(srcs)

    def body(*refs):
        src = refs[:n]
        land = refs[n:2 * n]
        send_ref, recv_ref = refs[2 * n:2 * n + 2]
        _, sibling, _, _ = _other_chips()
        for pos in range(n):
            for j in range(N_GATHER_COPIES):
                cp = _gather_copy(src[pos], land[pos], by_cols[pos], send_ref, recv_ref, pos, j, 0, sibling)
                cp.wait_send()
                cp.wait_recv()

    outs = pl.pallas_call(
        body, name=name,
        out_shape=tuple([pltpu.HBM(s.shape, s.dtype) for s in srcs] + [pltpu.HBM(v.shape, v.dtype) for v in lands]),
        in_specs=[HBM] * (2 * n) + [SEM, SEM, pl.BlockSpec(memory_space=pl.ANY)],
        out_specs=tuple([HBM] * (2 * n)),
        input_output_aliases={i: i for i in range(2 * n)},
        compiler_params=pltpu.CompilerParams(has_side_effects=DATAFLOW),
    )(*srcs, *lands, send_sems, recv_sems, after)
    return outs[n:]


def _scatter_grads(csums, by_cols, name):
    n = len(csums)
    shard = [(s.shape[0], s.shape[1] // N_CHIPS) if bc else s.shape[1:] for s, bc in zip(csums, by_cols)]

    def body(*refs):
        src = refs[:n]
        rem = refs[n:2 * n]
        sib = refs[2 * n:3 * n]
        send_sems, recv_sems = refs[3 * n:]
        c = lax.axis_index("c")
        me, sibling, chips, chip_ids = _other_chips()

        def remote(i, k, src_ref, dst_ref, to):
            return pltpu.make_async_remote_copy(src_ref=src_ref, dst_ref=dst_ref, send_sem=send_sems.at[i, k],
                                                recv_sem=recv_sems.at[i, k], device_id=to, device_id_type=MESH)

        def part(i, k):
            return _chip_slab(src[i], k, shard[i][-1], by_cols[i])

        started = []
        for i in range(n):
            for j in range(3):
                started.append(remote(i, j, part(i, chip_ids[j]), rem[i].at[j, c], (*chips[j], c)))
            started.append(remote(i, 6, part(i, me), sib[i], sibling))
        for cp in started:
            cp.start()
        for i in range(n):
            for j in range(3):
                slot = rem[i].at[j, c]
                remote(i, j, slot, slot, sibling).wait_recv()
                fwd = remote(i, 3 + j, slot, slot, sibling)
                fwd.start()
                started.append(fwd)
        for i in range(n):
            for j in range(3):
                slot = rem[i].at[j, 1 - c]
                remote(i, 3 + j, slot, slot, sibling).wait_recv()
            remote(i, 6, sib[i], sib[i], sibling).wait_recv()
        for cp in started:
            cp.wait_send()

    out_shape = ([jax.ShapeDtypeStruct((N_CHIPS - 1, 2) + tuple(sh), s.dtype) for s, sh in zip(csums, shard)]
                 + [jax.ShapeDtypeStruct(tuple(sh), s.dtype) for s, sh in zip(csums, shard)])
    outs = _streamed_call(
        body, name=name,
        in_specs=[ANY] * n, out_specs=[ANY] * (2 * n), out_shape=out_shape,
        scratch_shapes=[pltpu.SemaphoreType.DMA((n, 7)), pltpu.SemaphoreType.DMA((n, 7))],
    )(*csums)
    return outs[:n], outs[n:]


def _sibling_exchange(views, by_cols, name):
    n = len(views)

    def body(*refs):
        src = refs[:n]
        theirs = refs[n:2 * n]
        send_sems, recv_sems = refs[2 * n:]
        x, y, c = _mesh_position()
        copies = []
        for i in range(n):
            half = src[i].at[1 - c] if by_cols[i] else src[i].at[:, 1 - c]
            copies.append(pltpu.make_async_remote_copy(src_ref=half, dst_ref=theirs[i], send_sem=send_sems.at[i],
                                                       recv_sem=recv_sems.at[i], device_id=(x, y, 1 - c), device_id_type=MESH))
        for cp in copies:
            cp.start()
        for cp in copies:
            cp.wait()

    out_shape = [jax.ShapeDtypeStruct(v.shape[1:] if bc else v.shape[:1] + v.shape[2:], v.dtype) for v, bc in zip(views, by_cols)]
    return _streamed_call(
        body, name=name,
        in_specs=[ANY] * n, out_specs=[ANY] * n, out_shape=out_shape,
        scratch_shapes=[pltpu.SemaphoreType.DMA((n,)), pltpu.SemaphoreType.DMA((n,))],
    )(*views)


N_SCATTER_COPIES = 7


def _scatter_start(csums, by_cols, name):
    n = len(csums)
    shard = [(s.shape[0], s.shape[1] // N_CHIPS) if bc else s.shape[1:] for s, bc in zip(csums, by_cols)]
    rems = [lax.empty((N_CHIPS - 1, 2) + tuple(sh), s.dtype) for s, sh in zip(csums, shard)]
    sibs = [lax.empty(tuple(sh), s.dtype) for s, sh in zip(csums, shard)]

    def body(*refs):
        src = refs[:n]
        rem = refs[n:2 * n]
        sib = refs[2 * n:3 * n]
        send_sems, recv_sems = refs[3 * n:3 * n + 2]
        token = refs[-1]
        c = lax.axis_index("c")
        me, sibling, chips, chip_ids = _other_chips()
        for i in range(n):
            base = N_SCATTER_COPIES * i
            for j in range(3):
                part = _chip_slab(src[i], chip_ids[j], shard[i][-1], by_cols[i])
                for core in range(2):
                    pltpu.make_async_remote_copy(src_ref=part, dst_ref=rem[i].at[j, c], send_sem=send_sems.at[base + 2 * j + core],
                                                 recv_sem=recv_sems.at[base + 2 * j + c], device_id=(*chips[j], core),
                                                 device_id_type=MESH).start()
            pltpu.make_async_remote_copy(src_ref=_chip_slab(src[i], me, shard[i][-1], by_cols[i]), dst_ref=sib[i],
                                         send_sem=send_sems.at[base + 6], recv_sem=recv_sems.at[base + 6], device_id=sibling,
                                         device_id_type=MESH).start()
        token[...] = jnp.zeros_like(token)

    sems = pltpu.SemaphoreType.DMA((N_SCATTER_COPIES * n,))
    operands = list(csums) + rems + sibs
    outs = pl.pallas_call(
        body, name=name,
        out_shape=tuple([sems, sems] + [pltpu.HBM(a.shape, a.dtype) for a in operands] + [jax.ShapeDtypeStruct((SUBLANES, LANES), F32)]),
        in_specs=[HBM] * (3 * n),
        out_specs=tuple([SEM, SEM] + [HBM] * (3 * n) + [pl.BlockSpec(memory_space=pltpu.VMEM)]),
        input_output_aliases={i: 2 + i for i in range(3 * n)},
        compiler_params=pltpu.CompilerParams(has_side_effects=DATAFLOW),
    )(*[pltpu.with_memory_space_constraint(a, pltpu.HBM) for a in operands])
    return (outs[0], outs[1], outs[2:2 + n], outs[2 + n:2 + 2 * n], outs[2 + 2 * n:2 + 3 * n]), outs[-1]


def _scatter_wait(send_sems, recv_sems, srcs, rems, sibs, by_cols, after, name):
    n = len(srcs)

    def body(*refs):
        src = refs[:n]
        rem = refs[n:2 * n]
        sib = refs[2 * n:3 * n]
        send_ref, recv_ref = refs[3 * n:3 * n + 2]
        _, sibling, _, _ = _other_chips()
        for i in range(n):
            base = N_SCATTER_COPIES * i
            width = sib[i].shape[-1]
            for j in range(3):
                for core in range(2):
                    cp = pltpu.make_async_remote_copy(src_ref=_chip_slab(src[i], 0, width, by_cols[i]), dst_ref=rem[i].at[j, core],
                                                      send_sem=send_ref.at[base + 2 * j + core],
                                                      recv_sem=recv_ref.at[base + 2 * j + core], device_id=sibling,
                                                      device_id_type=MESH)
                    cp.wait_send()
                    cp.wait_recv()
            cp = pltpu.make_async_remote_copy(src_ref=_chip_slab(src[i], 0, width, by_cols[i]), dst_ref=sib[i],
                                              send_sem=send_ref.at[base + 6], recv_sem=recv_ref.at[base + 6], device_id=sibling,
                                              device_id_type=MESH)
            cp.wait_send()
            cp.wait_recv()

    operands = list(srcs) + list(rems) + list(sibs)
    outs = pl.pallas_call(
        body, name=name,
        out_shape=tuple(pltpu.HBM(a.shape, a.dtype) for a in operands),
        in_specs=[HBM] * (3 * n) + [SEM, SEM, pl.BlockSpec(memory_space=pl.ANY)],
        out_specs=tuple([HBM] * (3 * n)),
        input_output_aliases={i: i for i in range(3 * n)},
        compiler_params=pltpu.CompilerParams(has_side_effects=DATAFLOW),
    )(*operands, send_sems, recv_sems, after)
    return outs[:n], outs[n:2 * n], outs[2 * n:3 * n]


def _all_reduce_small(v, name):
    _, rows, _ = v.shape

    def body(v_ref, o_ref, recv_ref, send_sems, recv_sems):
        x, y, c = _mesh_position()
        me = 4 * x + 2 * y + c
        peers = []
        for d in range(1, N_DEV):
            px, py, pc = x ^ ((d >> 2) & 1), y ^ ((d >> 1) & 1), c ^ (d & 1)
            peers.append(((px, py, pc), 4 * px + 2 * py + pc))

        def remote(k, src_ref, dst_ref, to):
            return pltpu.make_async_remote_copy(src_ref=src_ref, dst_ref=dst_ref, send_sem=send_sems.at[k],
                                                recv_sem=recv_sems.at[k], device_id=to, device_id_type=MESH)

        scatter = [remote(d, v_ref.at[pid], recv_ref.at[me], to) for d, (to, pid) in enumerate(peers)]
        for cp in scatter:
            cp.start()
        recv_ref[pl.ds(me, 1)] = v_ref[pl.ds(me, 1)]
        for d, (to, pid) in enumerate(peers):
            remote(d, v_ref.at[pid], recv_ref.at[pid], to).wait_recv()
        total = recv_ref[0]
        for s in range(1, N_DEV):
            total = total + recv_ref[s]
        o_ref[pl.ds(me, 1)] = total[None]
        gather = [remote(N_DEV - 1 + d, o_ref.at[me], o_ref.at[me], to) for d, (to, pid) in enumerate(peers)]
        for cp in gather:
            cp.start()
        for d, (to, pid) in enumerate(peers):
            remote(N_DEV - 1 + d, o_ref.at[pid], o_ref.at[pid], to).wait_recv()
        for cp in scatter + gather:
            cp.wait_send()

    vm = pl.BlockSpec(memory_space=pltpu.VMEM)
    return pl.pallas_call(
        body, name=name,
        in_specs=[vm], out_specs=vm, out_shape=jax.ShapeDtypeStruct(v.shape, F32),
        scratch_shapes=[pltpu.VMEM(v.shape, F32), pltpu.SemaphoreType.DMA((2 * (N_DEV - 1),)),
                        pltpu.SemaphoreType.DMA((2 * (N_DEV - 1),))],
        compiler_params=_cparams(4 * _nbytes(v.shape, F32)),
    )(v)


SHARDED_MATS = ("ffn1_w_gu", "ffn1_w_down", "mix_w_in", "conv_w_proj", "rnn_w_proj", "mix_w_out", "ffn2_w_gu", "ffn2_w_down")
COL_SHARDED = ("ffn1_w_gu", "ffn2_w_gu", "conv_dw_w")
SHARDED_VECS = ("conv_dw_w", "rnn_conv_w")
WEIGHT_NAMES = ("ffn1_w_gu", "ffn1_w_down", "ln1_g", "ln1_b", "mix_w_in", "mix_b_in", "conv_dw_w", "conv_dw_b", "conv_gn_g",
                "conv_gn_b", "conv_w_proj", "rnn_conv_w", "rnn_conv_b", "rnn_w_a", "rnn_b_a", "rnn_w_x", "rnn_b_x",
                "rnn_lambda", "rnn_w_proj", "mix_w_out", "ln2_g", "ln2_b", "ffn2_w_gu", "ffn2_w_down", "ln3_g", "ln3_b")
SMALL_NAMES = tuple(n for n in WEIGHT_NAMES if n not in SHARDED_MATS)
SECTION_NAMES = ("cv", "cg", "rx", "rg", "gc", "gr")


def _unshard_cols(gathered):
    k4, K, n = gathered.shape
    return jnp.transpose(gathered, (1, 0, 2)).reshape(K, k4 * n)


def _row(v):
    return v.reshape(1, -1)


def _layer_forward(x0, p, alpha, l, hooks):
    t = f"l{l}_"
    sv = {"x0": x0}
    x1, sv["z1"], sv["hg1"], sv["hu1"] = _ffn_fwd(x0, p["wgu1"], p["wd1"], p["ln1_g"], p["ln1_b"], alpha, t + "ffn1_fwd")
    sv["x1"] = x1
    hooks.get("after_ffn1", lambda v: None)(x1)
    sec = dict(zip(SECTION_NAMES, _mix_in(x1, p["win"], p["bin"], p["sections"], t + "mix_in")))
    sv.update(sec)
    sv["c2"], c4 = _conv_branch_fwd(sec["cv"], sec["cg"], p["conv_dw_w"], p["conv_dw_b"], p["conv_gn_g"], p["conv_gn_b"], t + "conv_fwd")
    sv["c4"] = c4
    r1 = _short_conv_fwd(sec["rx"], p["rnn_conv_w"], p["rnn_conv_b"], t + "rconv_fwd")
    sv["r1"] = r1
    sv["ra"], sv["ri"], a, uu = _gates_fwd(r1, p["wa"], p["wx"], p["rnn_b_a"], p["rnn_b_x"], p["rnn_lambda"], t + "gates_fwd")
    sv["a"] = a
    h = _scan_fwd(a, uu, t + "scan_fwd")
    sv["h"] = h
    hooks.get("after_scan", lambda v: None)(h)
    sv["yc"], sv["yr"], sv["z2"], x2 = _mixer_out_fwd(c4, h, sec["rg"], sec["gc"], sec["gr"], x1, p["wc"], p["wr"], p["wo"],
                                                      p["ln2_g"], p["ln2_b"], alpha, t + "mixout_fwd")
    sv["x2"] = x2
    hooks.get("after_mixer", lambda v: None)(x2)
    x3, sv["z3"], sv["hg2"], sv["hu2"] = _ffn_fwd(x2, p["wgu2"], p["wd2"], p["ln3_g"], p["ln3_b"], alpha, t + "ffn2_fwd")
    hooks.get("after_layer", lambda v: None)(x3)
    return x3, sv


def _layer_backward(dy, p, sv, alpha, l, before_ffn1=None):
    t = f"l{l}_"
    g, gb = {}, {}
    dx2, df, a_act, dhg, dhu, g["ln3_g"], g["ln3_b"] = _ffn_bwd(dy, sv["z3"], sv["hg2"], sv["hu2"], p["wgu2"], p["wd2"],
                                                                 p["ln3_g"], alpha, t + "ffn2_bwd")
    g["ffn2_w_down"], gb["ffn2_w_down"] = _mm_tn(a_act, df, t + "dwd2")
    g["ffn2_w_gu"], gb["ffn2_w_gu"] = _mm_tn_pair(sv["x2"], dhg, dhu, t + "dwgu2")
    (dz2, dz2b, m_b, dyc, dyr, dgc, dgr, s_gc, s_gr, g["ln2_g"], g["ln2_b"]) = _mixer_out_bwd(
        dx2, sv["z2"], p["ln2_g"], p["wo"], sv["yc"], sv["yr"], sv["gc"], sv["gr"], t + "mixout_bwd")
    g["mix_w_out"], gb["mix_w_out"] = _mm_tn(m_b, dz2b, t + "dwo")
    dc4, dh, drg, q_b, s_rg = _branch_bwd(dyc, dyr, p["wc"], p["wr"], sv["h"], sv["rg"], t + "branch_bwd")
    g["conv_w_proj"], gb["conv_w_proj"] = _mm_tn(sv["c4"], dyc, t + "dwc")
    g["rnn_w_proj"], gb["rnn_w_proj"] = _mm_tn(q_b, dyr, t + "dwr")
    (dcv, dcg, g["conv_dw_w"], g["conv_dw_b"], g["conv_gn_g"], g["conv_gn_b"], s_cv, s_cg) = _conv_branch_bwd(
        dc4, sv["c2"], sv["cv"], sv["cg"], p["conv_dw_w"], p["conv_gn_g"], p["conv_gn_b"], t + "conv_bwd")
    guu, da = _scan_bwd(sv["a"], dh, sv["h"], t + "scan_bwd")
    dr1, dpa, dpx, g["rnn_b_a"], g["rnn_b_x"], g["rnn_lambda"] = _gates_bwd(
        guu, da, sv["ra"], sv["ri"], sv["r1"], p["wa"], p["wx"], p["rnn_lambda"], t + "gates_bwd")
    g["rnn_w_a"], g["rnn_w_x"] = _block_grads(sv["r1"], dpa, dpx, t + "dwax")
    drx, g["rnn_conv_w"], g["rnn_conv_b"], s_rx = _short_conv_bwd(dr1, sv["rx"], p["rnn_conv_w"], t + "rconv_bwd")
    du = {"cv": dcv, "cg": dcg, "rx": drx, "rg": drg, "gc": dgc, "gr": dgr}
    order = ("cv", "cg", "rx", "rg", "gc", "gr")
    pieces = [_mm_tn(du[s], sv["x1"], t + "dwin_" + s) for s in order]
    g["mix_w_in"] = jnp.concatenate([f for f, _ in pieces], axis=0)
    gb["mix_w_in"] = jnp.concatenate([h for _, h in pieces], axis=0)
    g["mix_b_in"] = jnp.concatenate([s_cv, s_cg, s_rx, s_rg, s_gc, s_gr], axis=1)
    dx1 = _mix_dx(dz2, [du[s] for s in order], p["win"], p["sections"], alpha, t + "mix_dx")
    ln1_g = p["ln1_g"] if before_ffn1 is None else p["ln1_g"] + before_ffn1(g, gb)[0:1, 0:1]
    dx0, df, a_act, dhg, dhu, g["ln1_g"], g["ln1_b"] = _ffn_bwd(dx1, sv["z1"], sv["hg1"], sv["hu1"], p["wgu1"], p["wd1"],
                                                                 ln1_g, alpha, t + "ffn1_bwd")
    g["ffn1_w_down"], gb["ffn1_w_down"] = _mm_tn(a_act, df, t + "dwd1")
    g["ffn1_w_gu"], gb["ffn1_w_gu"] = _mm_tn_pair(sv["x0"], dhg, dhu, t + "dwgu1")
    return dx0, g, gb


def _pack_small(arrays, piece_rows):
    flat = jnp.concatenate([a.reshape(-1) for a in arrays])
    total = N_DEV * piece_rows * LANES
    return jnp.pad(flat, (0, total - flat.shape[0])).reshape(N_DEV, piece_rows, LANES)


def _unpack_small(packed, shapes):
    flat = packed.reshape(-1)
    out, off = [], 0
    for shp in shapes:
        n = 1
        for s in shp:
            n *= s
        out.append(flat[off:off + n].reshape(shp))
        off += n
    return out


def kernel(x, ffn1_w_gu, ffn1_w_down, ln1_g, ln1_b, mix_w_in, mix_b_in, conv_dw_w, conv_dw_b, conv_gn_g, conv_gn_b, conv_w_proj, rnn_conv_w, rnn_conv_b, rnn_w_a, rnn_b_a, rnn_w_x, rnn_b_x, rnn_lambda, rnn_w_proj, mix_w_out, ln2_g, ln2_b, ffn2_w_gu, ffn2_w_down, ln3_g, ln3_b, loss_target, m_ffn1_w_gu, m_ffn1_w_down, m_ln1_g, m_ln1_b, m_mix_w_in, m_mix_b_in, m_conv_dw_w, m_conv_dw_b, m_conv_gn_g, m_conv_gn_b, m_conv_w_proj, m_rnn_conv_w, m_rnn_conv_b, m_rnn_w_a, m_rnn_b_a, m_rnn_w_x, m_rnn_b_x, m_rnn_lambda, m_rnn_w_proj, m_mix_w_out, m_ln2_g, m_ln2_b, m_ffn2_w_gu, m_ffn2_w_down, m_ln3_g, m_ln3_b, v_ffn1_w_gu, v_ffn1_w_down, v_ln1_g, v_ln1_b, v_mix_w_in, v_mix_b_in, v_conv_dw_w, v_conv_dw_b, v_conv_gn_g, v_conv_gn_b, v_conv_w_proj, v_rnn_conv_w, v_rnn_conv_b, v_rnn_w_a, v_rnn_b_a, v_rnn_w_x, v_rnn_b_x, v_rnn_lambda, v_rnn_w_proj, v_mix_w_out, v_ln2_g, v_ln2_b, v_ffn2_w_gu, v_ffn2_w_down, v_ln3_g, v_ln3_b):
    args = locals()
    W = {n: args[n] for n in WEIGHT_NAMES}
    M = {n: args["m_" + n] for n in WEIGHT_NAMES}
    V = {n: args["v_" + n] for n in WEIGHT_NAMES}
    depth = ln1_g.shape[0]
    assert depth == 2, "each core of a chip moves one layer's weights and gradients"
    alpha = float((2 * depth) ** 0.25)
    S, D = x.shape[1], x.shape[2]
    F = ffn1_w_down.shape[1] * N_CHIPS
    R = rnn_w_proj.shape[1] * N_CHIPS
    chip = 2 * lax.axis_index("x") + lax.axis_index("y")

    for d in (W, M, V):
        d["mix_w_in"] = jnp.transpose(d["mix_w_in"], (0, 2, 1))

    names = SHARDED_MATS + SHARDED_VECS
    unit_groups = [[(0, "ffn1_w_gu"), (0, "ffn1_w_down")], [(0, "mix_w_in"), (0, "conv_dw_w"), (0, "rnn_conv_w")],
                   [(0, "conv_w_proj"), (0, "rnn_w_proj"), (0, "mix_w_out")], [(0, "ffn2_w_gu"), (0, "ffn2_w_down")],
                   [(1, n) for n in names]]
    order = [u for g in unit_groups for u in g]
    index = {u: i for i, u in enumerate(order)}
    groups = [[index[u] for u in g] for g in unit_groups]
    srcs = [W[n][l].astype(BF16) if n in SHARDED_MATS else W[n][l] for l, n in order]
    by_cols = [n in COL_SHARDED for _, n in order]
    send_sems, recv_sems, src_thru, land_thru = _gather_start(srcs, by_cols, groups, "gather_start")

    sections = ((0, D, F32), (D, D, F32), (2 * D, R, F32), (2 * D + R, R, BF16), (2 * D + 2 * R, D, BF16),
                (3 * D + 2 * R, D, BF16))
    keys = {"ffn1_w_gu": "wgu1", "ffn1_w_down": "wd1", "ffn2_w_gu": "wgu2", "ffn2_w_down": "wd2", "conv_w_proj": "wc",
            "rnn_w_proj": "wr", "mix_w_out": "wo"}
    params = []
    for l in range(depth):
        p = {"wa": _embed_blocks(rnn_w_a[l], f"l{l}_embed_wa"), "wx": _embed_blocks(rnn_w_x[l], f"l{l}_embed_wx")}
        for n in ("ln1_g", "ln1_b", "ln2_g", "ln2_b", "ln3_g", "ln3_b", "conv_dw_b", "conv_gn_g", "conv_gn_b", "rnn_conv_b",
                  "rnn_b_a", "rnn_b_x", "rnn_lambda"):
            p[n] = _row(W[n][l])
        p["bin"] = _row(mix_b_in[l])
        p["sections"] = sections
        params.append(p)

    def wait_group(g, after):
        ids = groups[g]
        landed = _gather_wait([src_thru[i] for i in ids], [land_thru[i] for i in ids], [by_cols[i] for i in ids],
                              send_sems[g], recv_sems[g], after, f"gather_wait{g}")
        for i, full in zip(ids, landed):
            l, n = order[i]
            p = params[l]
            if n not in COL_SHARDED:
                full = full.reshape((N_CHIPS * full.shape[1],) + full.shape[2:])
            if n == "mix_w_in":
                p["win"] = full
            elif n == "rnn_conv_w":
                p[n] = _unshard_cols(landed[ids.index(i)])
            elif n == "conv_dw_w":
                p[n] = full
            else:
                p[keys[n]] = (full[None], 0)

    h = x[0]
    wait_group(0, h)
    saved = []
    hooks = [{"after_ffn1": lambda v: wait_group(1, v), "after_scan": lambda v: wait_group(2, v),
              "after_mixer": lambda v: wait_group(3, v), "after_layer": lambda v: wait_group(4, v)}, {}]
    for l in range(depth):
        h, sv = _layer_forward(h, params[l], alpha, l, hooks[l])
        saved.append(sv)
    loss_part, dy = _loss_head(h, loss_target[0], "loss_head")
    loss = lax.psum(loss_part[0, 0], ("x", "y", "c"))
    def pair_sums(names, g, gb, tag):
        cols = [n in COL_SHARDED for n in names]
        theirs = _sibling_exchange([_half_view(gb[n], bc) for n, bc in zip(names, cols)], cols, "pair_exchange" + tag)
        return cols, [_pair_sum_bf16(_half_view(g[n], bc), t, bc, f"pair_sum{tag}_{n}") for n, bc, t in zip(names, cols, theirs)]

    pending = []

    def start_scatter(layer, names, g, gb, tag):
        cols, sums = pair_sums(names, g, gb, tag)
        in_flight, token = _scatter_start(sums, cols, "scatter_start" + tag)
        pending.append((layer, names, cols, in_flight, tag))
        return token

    early = [n for n in SHARDED_MATS if not n.startswith("ffn1_")]
    late = [n for n in SHARDED_MATS if n.startswith("ffn1_")]
    grads, grads_bf16 = [None] * depth, [None] * depth
    dy, grads[1], grads_bf16[1] = _layer_backward(dy, params[1], saved[1], alpha, 1)
    token = start_scatter(1, SHARDED_MATS, grads[1], grads_bf16[1], "1")
    first = dict(params[0], ln3_g=params[0]["ln3_g"] + token[0:1, 0:1])
    dy, grads[0], grads_bf16[0] = _layer_backward(dy, first, saved[0], alpha, 0,
                                                  before_ffn1=lambda g, gb: start_scatter(0, early, g, gb, "0a"))
    grad_x = dy[None]
    late_cols, late_sums = pair_sums(late, grads[0], grads_bf16[0], "0b")
    late_rem, late_sib = _scatter_grads(late_sums, late_cols, "scatter_grads0b")
    partial = {(0, n): part for n, part in zip(late, zip(late_sums, late_rem, late_sib))}
    for layer, names, cols, in_flight, tag in pending:
        for n, part in zip(names, zip(*_scatter_wait(*in_flight, cols, late_rem[0], "scatter_wait" + tag))):
            partial[(layer, n)] = part

    results = {}
    for l in (1, 0):
        for n in SHARDED_MATS:
            cs, rm, sb = partial[(l, n)]
            if n in COL_SHARDED:
                width = cs.shape[1] // N_CHIPS
                own = lax.dynamic_slice_in_dim(cs, chip * width, width, axis=1)
            else:
                own = lax.dynamic_index_in_dim(cs, chip, axis=0, keepdims=False)
            results[n] = _adamw_sharded(W[n], M[n], V[n], own, sb, rm, l, results.get(n), f"adamw{l}_{n}")
    out_g, out_d, out_m, out_v = {}, {}, {}, {}
    for n in SHARDED_MATS:
        outs = results[n]
        if n == "mix_w_in":
            outs = [jnp.transpose(o, (0, 2, 1)) for o in outs]
        out_g[n], out_d[n], out_m[n], out_v[n] = outs

    small_grads = [jnp.stack([grads[l][n].reshape(W[n].shape[1:] if n not in SHARDED_VECS else
                                                   (W[n].shape[1], W[n].shape[2] * N_CHIPS)) for l in range(depth)])
                   for n in SMALL_NAMES]
    n_small = sum(int(a.size) for a in small_grads)
    piece_rows = -(-n_small // (N_DEV * LANES * SUBLANES)) * SUBLANES
    reduced = _unpack_small(_all_reduce_small(_pack_small(small_grads, piece_rows), "all_reduce_small"),
                            [a.shape for a in small_grads])
    local_g = []
    for n, gr in zip(SMALL_NAMES, reduced):
        if n in SHARDED_VECS:
            width = W[n].shape[2]
            gr = lax.dynamic_slice_in_dim(gr, chip * width, width, axis=2)
        local_g.append(gr)
    n_local = sum(int(a.size) for a in local_g)
    flat_rows = -(-n_local // (N_DEV * LANES * SUBLANES)) * SUBLANES * N_DEV
    pack = lambda arrs: _pack_small(arrs, flat_rows // N_DEV).reshape(flat_rows, LANES)
    shapes = [a.shape for a in local_g]
    deltas, new_m, new_v = _adamw_flat(pack([W[n] for n in SMALL_NAMES]), pack([M[n] for n in SMALL_NAMES]),
                                       pack([V[n] for n in SMALL_NAMES]), pack(local_g), "adamw_small")
    for n, gr, d_, m_, v_ in zip(SMALL_NAMES, local_g, _unpack_small(deltas, shapes), _unpack_small(new_m, shapes),
                                 _unpack_small(new_v, shapes)):
        out_g[n], out_d[n], out_m[n], out_v[n] = gr, d_, m_, v_

    return (loss, grad_x, *[out_g[n] for n in WEIGHT_NAMES], *[out_d[n] for n in WEIGHT_NAMES],
            *[out_m[n] for n in WEIGHT_NAMES], *[out_v[n] for n in WEIGHT_NAMES])
```

```python
import functools

import jax
import jax.numpy as jnp
from jax import lax
from jax.experimental import pallas as pl
from jax.experimental.pallas import tpu as pltpu

F32 = jnp.float32
BF16 = jnp.bfloat16
MESH = pl.DeviceIdType.MESH

LN_EPS = 1e-5
CONV_GROUPS = 8
RNN_BLOCKS = 16
RG_LRU_C = 8.0
ADAM_LR = 0.001
ADAM_B1 = 0.9
ADAM_B2 = 0.999
ADAM_EPS = 1e-08
ADAM_WD = 0.01
ADAM_STEP = 10

LANES = 128
SUBLANES = 8
V7X_VMEM_BYTES = 64 << 20
VMEM_LIMIT_CAP = V7X_VMEM_BYTES - (6 << 20)
N_CHIPS = 4
N_DEV = 8
CONV_ROWS = 64
EW_ROWS = 1024
SCAN_SEGMENTS = 32
SCAN_UNROLL = 4


def _cparams(block_bytes):
    limit = min(VMEM_LIMIT_CAP, max(int(block_bytes) + (8 << 20), 24 << 20))
    return pltpu.CompilerParams(vmem_limit_bytes=limit)


def _nbytes(shape, dtype):
    n = 1
    for s in shape:
        n *= s
    return n * jnp.dtype(dtype).itemsize


def _divisor_tile(n, limit, quantum):
    if n <= limit:
        return n
    best = None
    for t in range(quantum, limit + 1, quantum):
        if n % t == 0:
            best = t
    assert best is not None, (n, limit, quantum)
    return best


def _bs(shape, imap, **kw):
    return pl.BlockSpec(shape, imap, **kw)


def _resident(shape):
    nd = len(shape)
    return pl.BlockSpec(shape, lambda *_: (0,) * nd, pipeline_mode=pl.Buffered(1))


def _streamed_call(body, **kw):
    call = pl.pallas_call(body, **kw)
    return lambda *operands: call(*[pltpu.with_memory_space_constraint(o, pltpu.HBM) for o in operands])


def _layer_block(w, block, imap, **kw):
    arr, layer = w
    return arr, pl.BlockSpec((None,) + block, lambda *ids: (layer,) + imap(*ids), **kw)


def _layer_resident(w):
    arr, _ = w
    return _layer_block(w, arr.shape[1:], lambda *_: (0, 0), pipeline_mode=pl.Buffered(1))


def _sigmoid(x):
    return jax.nn.sigmoid(x)


def _dot(a, b):
    return jnp.dot(a, b, preferred_element_type=F32)


def _dot_nt(a, b):
    return lax.dot_general(a, b, (((1,), (1,)), ((), ())), preferred_element_type=F32)


def _dot_tn(a, b):
    return lax.dot_general(a, b, (((0,), (0,)), ((), ())), preferred_element_type=F32)


def _row_mean(z):
    return jnp.mean(z, axis=-1, keepdims=True)


def _lane_mean(z):
    hi = z.astype(BF16)
    lo = (z - hi.astype(F32)).astype(BF16)
    ones = jnp.full((2 * LANES, LANES), 1.0 / LANES, BF16)
    return jnp.dot(jnp.concatenate([hi, lo], axis=-1), ones, preferred_element_type=F32)


def _norm_fwd(z, g, b, mean=_row_mean):
    mu = mean(z)
    xc = z - mu
    var = mean(xc * xc)
    return xc * lax.rsqrt(var + LN_EPS) * g + b


def _norm_bwd(z, g, dy, mean=_row_mean):
    mu = mean(z)
    xc = z - mu
    var = mean(xc * xc)
    rstd = lax.rsqrt(var + LN_EPS)
    xhat = xc * rstd
    dxh = dy * g
    m1 = mean(dxh)
    m2 = mean(dxh * xhat)
    return rstd * (dxh - m1 - xhat * m2), xhat


GELU_K = 0.7978845608028654
GELU_C = 0.044715


def _gelu(x):
    return 0.5 * x * (1.0 + jnp.tanh(GELU_K * (x + GELU_C * x * x * x)))


def _gelu_grad(x):
    t = jnp.tanh(GELU_K * (x + GELU_C * x * x * x))
    return 0.5 * (1.0 + t) + 0.5 * x * (1.0 - t * t) * GELU_K * (1.0 + 3.0 * GELU_C * x * x)


def _softplus(y):
    return jnp.maximum(y, 0.0) + jnp.log1p(jnp.exp(-jnp.abs(y)))


def _neg_expm1(y):
    series = -y * (1.0 + y * (0.5 + y * (1.0 / 6.0 + y * (1.0 / 24.0 + y * (1.0 / 120.0 + y * (1.0 / 720.0))))))
    return jnp.where(y > -0.25, series, 1.0 - jnp.exp(y))


def _colsum(x):
    return jnp.sum(x, axis=0, keepdims=True)


def _shifted_taps(src_ref, base, rows, taps):
    acc = None
    for o, coef in taps:
        term = coef() * src_ref[pl.ds(base + o, rows), :]
        acc = term if acc is None else acc + term
    return acc


def _shifted_corr(src_ref, base, rows, d, acc_ref, offs):
    for k, o in enumerate(offs):
        prod = d * src_ref[pl.ds(base + o, rows), :]
        part = jnp.sum(prod.reshape(rows // SUBLANES, SUBLANES, prod.shape[-1]), axis=0)
        acc_ref[SUBLANES * k:SUBLANES * (k + 1), :] += part


def _front_pad(ktaps):
    return SUBLANES * ((ktaps - 1 + SUBLANES - 1) // SUBLANES)


def _pad_rows(ktaps):
    return _front_pad(ktaps) + SUBLANES


def _ffn_tiles(S, F):
    tm = _divisor_tile(S, 1024, 16)
    tf = _divisor_tile(F, 256, LANES)
    return tm, tf


def _ffn_fwd(x, wgu, wd, g, b, alpha, name):
    S, D = x.shape
    F = wd[0].shape[1]
    tm, tf = _ffn_tiles(S, F)
    nf = F // tf
    wg_arr, wg_spec = _layer_block(wgu, (D, tf), lambda i, j: (0, j))
    wu_arr, wu_spec = _layer_block(wgu, (D, tf), lambda i, j: (0, nf + j))
    wd_arr, wd_spec = _layer_block(wd, (tf, D), lambda i, j: (j, 0))

    def body(x_ref, wg_ref, wu_ref, wd_ref, g_ref, b_ref, y_ref, z_ref, hg_ref, hu_ref, acc_ref, xb_ref):
        j = pl.program_id(1)

        @pl.when(j == 0)
        def _():
            xb_ref[...] = x_ref[...].astype(BF16)
            acc_ref[...] = jnp.zeros_like(acc_ref)

        xb = xb_ref[...]
        hg = _dot(xb, wg_ref[...])
        hu = _dot(xb, wu_ref[...])
        hg_ref[...] = hg
        hu_ref[...] = hu
        a = (hg * _sigmoid(hg) * hu).astype(BF16)
        acc_ref[...] += _dot(a, wd_ref[...])

        @pl.when(j == nf - 1)
        def _():
            z = alpha * x_ref[...] + 0.5 * acc_ref[...]
            z_ref[...] = z
            y_ref[...] = _norm_fwd(z, g_ref[...], b_ref[...])

    blk = 2 * (3 * tm * D * 4 + 2 * tm * tf * 4 + 3 * D * tf * 2) + tm * D * 6 + 3 * tm * tf * 4
    return _streamed_call(
        body, name=name, grid=(S // tm, nf),
        in_specs=[_bs((tm, D), lambda i, j: (i, 0)), wg_spec, wu_spec, wd_spec,
                  _bs((1, D), lambda i, j: (0, 0)), _bs((1, D), lambda i, j: (0, 0))],
        out_specs=[_bs((tm, D), lambda i, j: (i, 0)), _bs((tm, D), lambda i, j: (i, 0)),
                   _bs((tm, tf), lambda i, j: (i, j)), _bs((tm, tf), lambda i, j: (i, j))],
        out_shape=[jax.ShapeDtypeStruct((S, D), F32), jax.ShapeDtypeStruct((S, D), F32),
                   jax.ShapeDtypeStruct((S, F), F32), jax.ShapeDtypeStruct((S, F), F32)],
        scratch_shapes=[pltpu.VMEM((tm, D), F32), pltpu.VMEM((tm, D), BF16)],
        compiler_params=_cparams(blk),
    )(x, wg_arr, wu_arr, wd_arr, g, b)


def _ffn_bwd(dy, z, hg, hu, wgu, wd, g, alpha, name):
    S, D = dy.shape
    F = wd[0].shape[1]
    tm, tf = _ffn_tiles(S, F)
    nf = F // tf
    wg_arr, wg_spec = _layer_block(wgu, (D, tf), lambda i, j: (0, j))
    wu_arr, wu_spec = _layer_block(wgu, (D, tf), lambda i, j: (0, nf + j))
    wd_arr, wd_spec = _layer_block(wd, (tf, D), lambda i, j: (j, 0))

    def body(dy_ref, z_ref, hg_ref, hu_ref, wg_ref, wu_ref, wd_ref, g_ref,
             dx_ref, df_ref, a_ref, dhg_ref, dhu_ref, dg_ref, db_ref, acc_ref):
        i = pl.program_id(0)
        j = pl.program_id(1)

        @pl.when((i == 0) & (j == 0))
        def _():
            dg_ref[...] = jnp.zeros_like(dg_ref)
            db_ref[...] = jnp.zeros_like(db_ref)

        @pl.when(j == 0)
        def _():
            dy_ = dy_ref[...]
            dz, xhat = _norm_bwd(z_ref[...], g_ref[...], dy_)
            dg_ref[...] += _colsum(dy_ * xhat)
            db_ref[...] += _colsum(dy_)
            acc_ref[...] = alpha * dz
            df_ref[...] = (0.5 * dz).astype(BF16)

        da = _dot_nt(df_ref[...], wd_ref[...])
        hg_ = hg_ref[...]
        hu_ = hu_ref[...]
        s = _sigmoid(hg_)
        sl = hg_ * s
        dgate = (da * hu_ * (s * (1.0 + hg_ * (1.0 - s)))).astype(BF16)
        dup = (da * sl).astype(BF16)
        a_ref[...] = (sl * hu_).astype(BF16)
        dhg_ref[...] = dgate
        dhu_ref[...] = dup
        acc_ref[...] += _dot_nt(dgate, wg_ref[...]) + _dot_nt(dup, wu_ref[...])

        @pl.when(j == nf - 1)
        def _():
            dx_ref[...] = acc_ref[...]

    blk = 2 * (2 * tm * D * 4 + tm * D * 2 + 2 * tm * tf * 4 + 3 * tm * tf * 2 + 3 * D * tf * 2) + 3 * tm * D * 4 + 8 * tm * tf * 4
    once = dict(pipeline_mode=pl.Buffered(1))
    return _streamed_call(
        body, name=name, grid=(S // tm, nf),
        in_specs=[_bs((tm, D), lambda i, j: (i, 0), **once), _bs((tm, D), lambda i, j: (i, 0), **once),
                  _bs((tm, tf), lambda i, j: (i, j)), _bs((tm, tf), lambda i, j: (i, j)),
                  wg_spec, wu_spec, wd_spec, _bs((1, D), lambda i, j: (0, 0))],
        out_specs=[_bs((tm, D), lambda i, j: (i, 0)), _bs((tm, D), lambda i, j: (i, 0)),
                   _bs((tm, tf), lambda i, j: (i, j)), _bs((tm, tf), lambda i, j: (i, j)), _bs((tm, tf), lambda i, j: (i, j)),
                   _bs((1, D), lambda i, j: (0, 0)), _bs((1, D), lambda i, j: (0, 0))],
        out_shape=[jax.ShapeDtypeStruct((S, D), F32), jax.ShapeDtypeStruct((S, D), BF16),
                   jax.ShapeDtypeStruct((S, F), BF16), jax.ShapeDtypeStruct((S, F), BF16), jax.ShapeDtypeStruct((S, F), BF16),
                   jax.ShapeDtypeStruct((1, D), F32), jax.ShapeDtypeStruct((1, D), F32)],
        scratch_shapes=[pltpu.VMEM((tm, D), F32)],
        compiler_params=_cparams(blk),
    )(dy, z, hg, hu, wg_arr, wu_arr, wd_arr, g)


def _mm_tn(a, b, name):
    S, M = a.shape
    N = b.shape[1]
    bm = _divisor_tile(M, 1408, LANES)
    bn = _divisor_tile(N, 1408, LANES)
    tk = _divisor_tile(S, 512, 16)
    nk = S // tk

    def body(a_ref, b_ref, o_ref, ob_ref):
        k = pl.program_id(2)

        @pl.when(k == 0)
        def _():
            o_ref[...] = jnp.zeros_like(o_ref)

        o_ref[...] += _dot_tn(a_ref[...].astype(BF16), b_ref[...].astype(BF16))

        @pl.when(k == nk - 1)
        def _():
            ob_ref[...] = o_ref[...].astype(BF16)

    blk = 2 * (tk * bm * a.dtype.itemsize + tk * bn * b.dtype.itemsize + bm * bn * 6) + tk * bm * 4 + bm * bn * 4
    tile = _bs((bm, bn), lambda i, j, k: (i, j))
    return _streamed_call(
        body, name=name, grid=(M // bm, N // bn, nk),
        in_specs=[_bs((tk, bm), lambda i, j, k: (k, i)), _bs((tk, bn), lambda i, j, k: (k, j))],
        out_specs=[tile, tile],
        out_shape=[jax.ShapeDtypeStruct((M, N), F32), jax.ShapeDtypeStruct((M, N), BF16)],
        compiler_params=_cparams(blk),
    )(a, b)


def _mm_tn_pair(a, b0, b1, name):
    S, M = a.shape
    N = b0.shape[1]
    assert b1.shape == b0.shape
    bm = _divisor_tile(M, 1408, LANES)
    bn = _divisor_tile(N, 1408, LANES)
    tk = _divisor_tile(S, 512, 16)
    nb = N // bn
    nk = S // tk

    def body(a_ref, b0_ref, b1_ref, o_ref, ob_ref):
        j = pl.program_id(1)
        k = pl.program_id(2)

        @pl.when(k == 0)
        def _():
            o_ref[...] = jnp.zeros_like(o_ref)

        ab = a_ref[...].astype(BF16)

        @pl.when(j < nb)
        def _():
            o_ref[...] += _dot_tn(ab, b0_ref[...])

        @pl.when(j >= nb)
        def _():
            o_ref[...] += _dot_tn(ab, b1_ref[...])

        @pl.when(k == nk - 1)
        def _():
            ob_ref[...] = o_ref[...].astype(BF16)

    b0_map = lambda i, j, k: (jnp.where(j < nb, k, nk - 1), jnp.minimum(j, nb - 1))
    b1_map = lambda i, j, k: (jnp.where(j >= nb, k, 0), jnp.maximum(j - nb, 0))
    blk = 2 * (tk * bm * a.dtype.itemsize + 2 * tk * bn * 2 + bm * bn * 6) + tk * bm * 4 + bm * bn * 4
    tile = _bs((bm, bn), lambda i, j, k: (i, j))
    return _streamed_call(
        body, name=name, grid=(M // bm, 2 * nb, nk),
        in_specs=[_bs((tk, bm), lambda i, j, k: (k, i)), _bs((tk, bn), b0_map), _bs((tk, bn), b1_map)],
        out_specs=[tile, tile],
        out_shape=[jax.ShapeDtypeStruct((M, 2 * N), F32), jax.ShapeDtypeStruct((M, 2 * N), BF16)],
        compiler_params=_cparams(blk),
    )(a, b0, b1)


def _mix_in(x, wt, bias, sections, name):
    S, D = x.shape
    tm = _divisor_tile(S, 256, 16)
    n = len(sections)

    def body(x_ref, w_ref, b_ref, *o_refs):
        xb = x_ref[...].astype(BF16)
        for (off, width, dtype), o_ref in zip(sections, o_refs):
            o_ref[...] = (_dot_nt(xb, w_ref[off:off + width, :]) + b_ref[:, off:off + width]).astype(dtype)

    total = wt.shape[0]
    blk = 2 * (tm * D * 4 + sum(tm * w * jnp.dtype(dt).itemsize for _, w, dt in sections)) + total * D * 2 + 3 * tm * D * 4
    return _streamed_call(
        body, name=name, grid=(S // tm,),
        in_specs=[_bs((tm, D), lambda i: (i, 0)), _resident((total, D)), _resident((1, total))],
        out_specs=[_bs((tm, w), lambda i: (i, 0)) for _, w, _ in sections],
        out_shape=[jax.ShapeDtypeStruct((S, w), dt) for _, w, dt in sections],
        compiler_params=_cparams(blk),
    )(x, wt, bias)


def _mix_dx(dz, parts, wt, sections, alpha, name):
    S, D = dz.shape
    tm = _divisor_tile(S, 256, 16)
    n = len(parts)

    def body(*refs):
        dz_ref = refs[0]
        p_refs = refs[1:1 + n]
        w_ref = refs[1 + n]
        o_ref = refs[2 + n]
        acc = alpha * dz_ref[...]
        for p_ref, (off, width, _) in zip(p_refs, sections):
            acc = acc + _dot(p_ref[...], w_ref[off:off + width, :])
        o_ref[...] = acc

    widths = [p.shape[1] for p in parts]
    total = wt.shape[0]
    blk = 2 * (2 * tm * D * 4 + sum(tm * w * 2 for w in widths)) + total * D * 2 + 2 * tm * D * 4
    return _streamed_call(
        body, name=name, grid=(S // tm,),
        in_specs=[_bs((tm, D), lambda i: (i, 0))] + [_bs((tm, w), lambda i: (i, 0)) for w in widths]
                 + [_resident((total, D))],
        out_specs=_bs((tm, D), lambda i: (i, 0)),
        out_shape=jax.ShapeDtypeStruct((S, D), F32),
        compiler_params=_cparams(blk),
    )(dz, *parts, wt)


def _conv_branch_fwd(cv, cg, w, b, gg, gb, name):
    S, C = cv.shape
    K = w.shape[0]
    assert C // CONV_GROUPS == LANES
    padf = _front_pad(K)
    R = min(CONV_ROWS, S)
    E = min(EW_ROWS, S)

    def body(cv_ref, cg_ref, w_ref, b_ref, gg_ref, gb_ref, c2_ref, c4_ref, pad_ref):
        pad_ref[0:padf, :] = jnp.zeros((padf, LANES), F32)
        pad_ref[S + padf:S + padf + SUBLANES, :] = jnp.zeros((SUBLANES, LANES), F32)

        def fill(i, carry):
            r = pl.multiple_of(i * E, E)
            pad_ref[pl.ds(r + padf, E), :] = cv_ref[pl.ds(r, E), :] * _sigmoid(cg_ref[pl.ds(r, E), :])
            return carry

        lax.fori_loop(0, S // E, fill, 0)
        taps = [(padf - (K - 1) + k, functools.partial(lambda k: w_ref[k:k + 1, :], k)) for k in range(K)]

        def conv(i, carry):
            r = pl.multiple_of(i * R, R)
            c2_ref[pl.ds(r, R), :] = _shifted_taps(pad_ref, r, R, taps) + b_ref[...]
            return carry

        lax.fori_loop(0, S // R, conv, 0)

        def norm(i, carry):
            r = pl.multiple_of(i * E, E)
            c3 = _norm_fwd(c2_ref[pl.ds(r, E), :], gg_ref[...], gb_ref[...], _lane_mean)
            c4_ref[pl.ds(r, E), :] = (c3 * _sigmoid(c3)).astype(BF16)
            return carry

        lax.fori_loop(0, S // E, norm, 0)

    col = lambda i: (0, i)
    blk = 2 * (3 * S * LANES * 4 + S * LANES * 2) + (S + _pad_rows(K)) * LANES * 4
    return _streamed_call(
        body, name=name, grid=(C // LANES,),
        in_specs=[_bs((S, LANES), col), _bs((S, LANES), col), _bs((K, LANES), col),
                  _bs((1, LANES), col), _bs((1, LANES), col), _bs((1, LANES), col)],
        out_specs=[_bs((S, LANES), col), _bs((S, LANES), col)],
        out_shape=[jax.ShapeDtypeStruct((S, C), F32), jax.ShapeDtypeStruct((S, C), BF16)],
        scratch_shapes=[pltpu.VMEM((S + _pad_rows(K), LANES), F32)],
        compiler_params=_cparams(blk),
    )(cv, cg, w, b, gg, gb)


def _conv_branch_bwd(dc4, c2, cv, cg, w, gg, gb, name):
    S, C = cv.shape
    K = w.shape[0]
    padf = _front_pad(K)
    R = min(CONV_ROWS, S)
    E = min(EW_ROWS, S)

    def body(dc4_ref, c2_ref, cv_ref, cg_ref, w_ref, gg_ref, gb_ref,
             dcv_ref, dcg_ref, dw_ref, dwb_ref, dgg_ref, dgb_ref, scv_ref, scg_ref,
             dpad_ref, cpad_ref, dwacc_ref):
        cpad_ref[0:padf, :] = jnp.zeros((padf, LANES), F32)
        cpad_ref[S + padf:S + padf + SUBLANES, :] = jnp.zeros((SUBLANES, LANES), F32)
        dpad_ref[S:S + padf + SUBLANES, :] = jnp.zeros((padf + SUBLANES, LANES), F32)
        dwacc_ref[...] = jnp.zeros_like(dwacc_ref)
        for ref in (dwb_ref, dgg_ref, dgb_ref, scv_ref, scg_ref):
            ref[...] = jnp.zeros_like(ref)

        def norm_pass(i, carry):
            r = pl.multiple_of(i * E, E)
            g_ = gg_ref[...]
            c2 = c2_ref[pl.ds(r, E), :]
            xc = c2 - _lane_mean(c2)
            rstd = lax.rsqrt(_lane_mean(xc * xc) + LN_EPS)
            xhat = xc * rstd
            c3 = xhat * g_ + gb_ref[...]
            s = _sigmoid(c3)
            dc3 = dc4_ref[pl.ds(r, E), :].astype(F32) * (s * (1.0 + c3 * (1.0 - s)))
            dgg_ref[...] += _colsum(dc3 * xhat)
            dgb_ref[...] += _colsum(dc3)
            dxh = dc3 * g_
            dc2 = rstd * (dxh - _lane_mean(dxh) - xhat * _lane_mean(dxh * xhat))
            dpad_ref[pl.ds(r, E), :] = dc2
            dwb_ref[...] += _colsum(dc2)
            cpad_ref[pl.ds(r + padf, E), :] = cv_ref[pl.ds(r, E), :] * _sigmoid(cg_ref[pl.ds(r, E), :])
            return carry

        lax.fori_loop(0, S // E, norm_pass, 0)
        taps = [(K - 1 - k, functools.partial(lambda k: w_ref[k:k + 1, :], k)) for k in range(K)]
        offs = [padf - (K - 1) + k for k in range(K)]

        def conv_pass(i, carry):
            r = pl.multiple_of(i * R, R)
            dc1 = _shifted_taps(dpad_ref, r, R, taps)
            sg = _sigmoid(cg_ref[pl.ds(r, R), :])
            cv_ = cv_ref[pl.ds(r, R), :]
            dcv = dc1 * sg
            dcg = dc1 * cv_ * sg * (1.0 - sg)
            dcv_ref[pl.ds(r, R), :] = dcv.astype(BF16)
            dcg_ref[pl.ds(r, R), :] = dcg.astype(BF16)
            scv_ref[...] += _colsum(dcv)
            scg_ref[...] += _colsum(dcg)
            _shifted_corr(cpad_ref, r, R, dpad_ref[pl.ds(r, R), :], dwacc_ref, offs)
            return carry

        lax.fori_loop(0, S // R, conv_pass, 0)
        for k in range(K):
            dw_ref[k:k + 1, :] = _colsum(dwacc_ref[SUBLANES * k:SUBLANES * (k + 1), :])

    col = lambda i: (0, i)
    row = jax.ShapeDtypeStruct((1, C), F32)
    blk = 2 * (4 * S * LANES * 4 + 2 * S * LANES * 2) + 2 * (S + _pad_rows(K)) * LANES * 4
    return _streamed_call(
        body, name=name, grid=(C // LANES,),
        in_specs=[_bs((S, LANES), col)] * 4 + [_bs((K, LANES), col), _bs((1, LANES), col), _bs((1, LANES), col)],
        out_specs=[_bs((S, LANES), col), _bs((S, LANES), col), _bs((K, LANES), col)] + [_bs((1, LANES), col)] * 5,
        out_shape=[jax.ShapeDtypeStruct((S, C), BF16), jax.ShapeDtypeStruct((S, C), BF16),
                   jax.ShapeDtypeStruct((K, C), F32), row, row, row, row, row],
        scratch_shapes=[pltpu.VMEM((S + _pad_rows(K), LANES), F32), pltpu.VMEM((S + _pad_rows(K), LANES), F32),
                        pltpu.VMEM((SUBLANES * K, LANES), F32)],
        compiler_params=_cparams(blk),
    )(dc4, c2, cv, cg, w, gg, gb)


def _short_conv_fwd(xin, w, b, name):
    S, C = xin.shape
    K = w.shape[0]
    padf = _front_pad(K)
    R = min(CONV_ROWS, S)
    E = min(EW_ROWS, S)

    def body(x_ref, w_ref, b_ref, o_ref, pad_ref):
        pad_ref[0:padf, :] = jnp.zeros((padf, LANES), F32)
        pad_ref[S + padf:S + padf + SUBLANES, :] = jnp.zeros((SUBLANES, LANES), F32)

        def fill(i, carry):
            r = pl.multiple_of(i * E, E)
            pad_ref[pl.ds(r + padf, E), :] = x_ref[pl.ds(r, E), :]
            return carry

        lax.fori_loop(0, S // E, fill, 0)
        taps = [(padf - (K - 1) + k, functools.partial(lambda k: w_ref[k:k + 1, :], k)) for k in range(K)]

        def conv(i, carry):
            r = pl.multiple_of(i * R, R)
            o_ref[pl.ds(r, R), :] = _shifted_taps(pad_ref, r, R, taps) + b_ref[...]
            return carry

        lax.fori_loop(0, S // R, conv, 0)

    col = lambda i: (0, i)
    blk = 2 * (2 * S * LANES * 4) + (S + _pad_rows(K)) * LANES * 4
    return _streamed_call(
        body, name=name, grid=(C // LANES,),
        in_specs=[_bs((S, LANES), col), _bs((K, LANES), col), _bs((1, LANES), col)],
        out_specs=_bs((S, LANES), col),
        out_shape=jax.ShapeDtypeStruct((S, C), F32),
        scratch_shapes=[pltpu.VMEM((S + _pad_rows(K), LANES), F32)],
        compiler_params=_cparams(blk),
    )(xin, w, b)


def _short_conv_bwd(dy, xin, w, name):
    S, C = xin.shape
    K = w.shape[0]
    padf = _front_pad(K)
    R = min(CONV_ROWS, S)
    E = min(EW_ROWS, S)

    def body(dy_ref, x_ref, w_ref, dx_ref, dw_ref, db_ref, sx_ref, dpad_ref, xpad_ref, dwacc_ref):
        xpad_ref[0:padf, :] = jnp.zeros((padf, LANES), F32)
        xpad_ref[S + padf:S + padf + SUBLANES, :] = jnp.zeros((SUBLANES, LANES), F32)
        dpad_ref[S:S + padf + SUBLANES, :] = jnp.zeros((padf + SUBLANES, LANES), F32)
        dwacc_ref[...] = jnp.zeros_like(dwacc_ref)
        db_ref[...] = jnp.zeros_like(db_ref)
        sx_ref[...] = jnp.zeros_like(sx_ref)

        def fill(i, carry):
            r = pl.multiple_of(i * E, E)
            d = dy_ref[pl.ds(r, E), :]
            dpad_ref[pl.ds(r, E), :] = d
            db_ref[...] += _colsum(d)
            xpad_ref[pl.ds(r + padf, E), :] = x_ref[pl.ds(r, E), :]
            return carry

        lax.fori_loop(0, S // E, fill, 0)
        taps = [(K - 1 - k, functools.partial(lambda k: w_ref[k:k + 1, :], k)) for k in range(K)]
        offs = [padf - (K - 1) + k for k in range(K)]

        def conv_pass(i, carry):
            r = pl.multiple_of(i * R, R)
            dx = _shifted_taps(dpad_ref, r, R, taps)
            dx_ref[pl.ds(r, R), :] = dx.astype(BF16)
            sx_ref[...] += _colsum(dx)
            _shifted_corr(xpad_ref, r, R, dpad_ref[pl.ds(r, R), :], dwacc_ref, offs)
            return carry

        lax.fori_loop(0, S // R, conv_pass, 0)
        for k in range(K):
            dw_ref[k:k + 1, :] = _colsum(dwacc_ref[SUBLANES * k:SUBLANES * (k + 1), :])

    col = lambda i: (0, i)
    row = jax.ShapeDtypeStruct((1, C), F32)
    blk = 2 * (2 * S * LANES * 4 + S * LANES * 2) + 2 * (S + _pad_rows(K)) * LANES * 4
    return _streamed_call(
        body, name=name, grid=(C // LANES,),
        in_specs=[_bs((S, LANES), col), _bs((S, LANES), col), _bs((K, LANES), col)],
        out_specs=[_bs((S, LANES), col), _bs((K, LANES), col), _bs((1, LANES), col), _bs((1, LANES), col)],
        out_shape=[jax.ShapeDtypeStruct((S, C), BF16), jax.ShapeDtypeStruct((K, C), F32), row, row],
        scratch_shapes=[pltpu.VMEM((S + _pad_rows(K), LANES), F32), pltpu.VMEM((S + _pad_rows(K), LANES), F32),
                        pltpu.VMEM((SUBLANES * K, LANES), F32)],
        compiler_params=_cparams(blk),
    )(dy, xin, w)


def _band_panels(width, block):
    assert width % LANES == 0 and block <= LANES
    panels = []
    for c0 in range(0, width, 2 * LANES):
        c1 = min(width, c0 + 2 * LANES)
        r0 = (c0 // block) * block // LANES * LANES
        r1 = min(width, -(-(-(-c1 // block) * block) // LANES) * LANES)
        panels.append((r0, r1, c0, c1))
    return panels


def _gates_fwd(r1, wa, wx, ba, bx, lam, name):
    S, R = r1.shape
    tm = _divisor_tile(S, 256, 16)
    panels = _band_panels(R, R // RNN_BLOCKS)

    def body(r1_ref, wa_ref, wx_ref, ba_ref, bx_ref, lam_ref, ra_ref, ri_ref, a_ref, uu_ref):
        for r0, r1e, c0, c1 in panels:
            rb = r1_ref[:, r0:r1e].astype(BF16)
            ra = _sigmoid(_dot(rb, wa_ref[r0:r1e, c0:c1]) + ba_ref[:, c0:c1])
            ri = _sigmoid(_dot(rb, wx_ref[r0:r1e, c0:c1]) + bx_ref[:, c0:c1])
            log_a = -RG_LRU_C * ra * _softplus(-lam_ref[:, c0:c1])
            ra_ref[:, c0:c1] = ra
            ri_ref[:, c0:c1] = ri
            a_ref[:, c0:c1] = jnp.exp(log_a)
            uu_ref[:, c0:c1] = jnp.sqrt(_neg_expm1(2.0 * log_a)) * (ri * r1_ref[:, c0:c1])

    blk = 2 * (5 * tm * R * 4) + 2 * R * R * 2 + 6 * tm * R * 4
    tile = _bs((tm, R), lambda i: (i, 0))
    return _streamed_call(
        body, name=name, grid=(S // tm,),
        in_specs=[tile, _resident((R, R)), _resident((R, R)), _resident((1, R)), _resident((1, R)), _resident((1, R))],
        out_specs=[tile] * 4,
        out_shape=[jax.ShapeDtypeStruct((S, R), F32)] * 4,
        compiler_params=_cparams(blk),
    )(r1, wa, wx, ba, bx, lam)


def _gates_bwd(guu, da, ra, ri, r1, wa, wx, lam, name):
    S, R = r1.shape
    tm = _divisor_tile(S, 256, 16)
    nsteps = S // tm
    panels = _band_panels(R, R // RNN_BLOCKS)

    def body(g_ref, da_ref, ra_ref, ri_ref, r1_ref, wa_ref, wx_ref, lam_ref,
             dr1_ref, dpa_ref, dpx_ref, dba_ref, dbx_ref, dlam_ref):
        i = pl.program_id(0)

        @pl.when(i == 0)
        def _():
            dba_ref[...] = jnp.zeros_like(dba_ref)
            dbx_ref[...] = jnp.zeros_like(dbx_ref)
            dlam_ref[...] = jnp.zeros_like(dlam_ref)

        g = g_ref[...]
        ra = ra_ref[...]
        ri = ri_ref[...]
        r1_ = r1_ref[...]
        sp = _softplus(-lam_ref[...])
        log_a = -RG_LRU_C * ra * sp
        a = jnp.exp(log_a)
        mult = jnp.sqrt(_neg_expm1(2.0 * log_a))
        d_ri = g * mult * r1_
        dr1 = g * mult * ri
        dmult = g * ri * r1_
        dlog_a = da_ref[...] * a - dmult * (a * a) / mult
        dra = dlog_a * (-RG_LRU_C * sp)
        dlam_ref[...] += _colsum(dlog_a * (-RG_LRU_C * ra))
        dpa = dra * ra * (1.0 - ra)
        dpx = d_ri * ri * (1.0 - ri)
        dba_ref[...] += _colsum(dpa)
        dbx_ref[...] += _colsum(dpx)
        dpa_b = dpa.astype(BF16)
        dpx_b = dpx.astype(BF16)
        dpa_ref[...] = dpa_b
        dpx_ref[...] = dpx_b
        dr1_ref[...] = dr1
        for k0, k1, c0, c1 in panels:
            dr1_ref[:, c0:c1] += (_dot_nt(dpa_ref[:, k0:k1], wa_ref[c0:c1, k0:k1])
                                  + _dot_nt(dpx_ref[:, k0:k1], wx_ref[c0:c1, k0:k1]))

        @pl.when(i == nsteps - 1)
        def _():
            dlam_ref[...] = dlam_ref[...] * (-_sigmoid(-lam_ref[...]))

    blk = 2 * (6 * tm * R * 4 + 2 * tm * R * 2) + 2 * R * R * 2 + 10 * tm * R * 4
    tile = _bs((tm, R), lambda i: (i, 0))
    rowspec = _bs((1, R), lambda i: (0, 0))
    row = jax.ShapeDtypeStruct((1, R), F32)
    return _streamed_call(
        body, name=name, grid=(nsteps,),
        in_specs=[tile] * 5 + [_resident((R, R)), _resident((R, R)), _resident((1, R))],
        out_specs=[tile, tile, tile, rowspec, rowspec, rowspec],
        out_shape=[jax.ShapeDtypeStruct((S, R), F32), jax.ShapeDtypeStruct((S, R), BF16), jax.ShapeDtypeStruct((S, R), BF16),
                   row, row, row],
        compiler_params=_cparams(blk),
    )(guu, da, ra, ri, r1, wa, wx, lam)


def _embed_blocks(w, name):
    H, bk, _ = w.shape

    def body(w_ref, o_ref):
        o_ref[...] = jnp.zeros_like(o_ref)
        for h in range(H):
            o_ref[bk * h:bk * (h + 1), bk * h:bk * (h + 1)] = w_ref[h].astype(BF16)

    return pl.pallas_call(body, name=name, out_shape=jax.ShapeDtypeStruct((H * bk, H * bk), BF16),
                          compiler_params=_cparams(3 * H * bk * H * bk * 2))(w)


def _block_grads(r1, dpa, dpx, name):
    S, R = r1.shape
    bk = R // RNN_BLOCKS
    tk = _divisor_tile(S, 512, 16)
    nsteps = S // tk
    panels = _band_panels(R, bk)

    def body(r1_ref, dpa_ref, dpx_ref, ga_ref, gx_ref, acca_ref, accx_ref):
        k = pl.program_id(0)

        @pl.when(k == 0)
        def _():
            acca_ref[...] = jnp.zeros_like(acca_ref)
            accx_ref[...] = jnp.zeros_like(accx_ref)

        for k0, k1, c0, c1 in panels:
            rb = r1_ref[:, k0:k1].astype(BF16)
            acca_ref[k0:k1, c0:c1] += _dot_tn(rb, dpa_ref[:, c0:c1])
            accx_ref[k0:k1, c0:c1] += _dot_tn(rb, dpx_ref[:, c0:c1])

        @pl.when(k == nsteps - 1)
        def _():
            for h in range(RNN_BLOCKS):
                ga_ref[h] = acca_ref[bk * h:bk * (h + 1), bk * h:bk * (h + 1)]
                gx_ref[h] = accx_ref[bk * h:bk * (h + 1), bk * h:bk * (h + 1)]

    tile = lambda: _bs((tk, R), lambda k: (k, 0))
    out = _bs((RNN_BLOCKS, bk, bk), lambda k: (0, 0, 0))
    sds = jax.ShapeDtypeStruct((RNN_BLOCKS, bk, bk), F32)
    return _streamed_call(
        body, name=name, grid=(nsteps,),
        in_specs=[tile(), tile(), tile()], out_specs=[out, out], out_shape=[sds, sds],
        scratch_shapes=[pltpu.VMEM((R, R), F32), pltpu.VMEM((R, R), F32)],
        compiler_params=_cparams(2 * (tk * R * 8) + 2 * R * R * 4 + 4 * tk * R * 4),
    )(r1, dpa, dpx)


def _scan_geometry(S):
    nseg = SCAN_SEGMENTS if S % (SCAN_SEGMENTS * SUBLANES) == 0 else SUBLANES
    return nseg, S // nseg


def _steps(n, step, init):
    u = SCAN_UNROLL

    def trip(t, carry):
        for k in range(u):
            carry = step(t * u + k, carry)
        return carry

    carry = lax.fori_loop(0, n // u, trip, init)
    for j in range(n - n % u, n):
        carry = step(j, carry)
    return carry


def _scan_fwd(a, u, name):
    S, C = a.shape
    nseg, L = _scan_geometry(S)
    T = min(SUBLANES, L)

    def body(a3, u3, h3, ta_ref, tu_ref, e_ref, p_ref, init_ref):

        def to_steps(i, carry):
            j0 = pl.multiple_of(i * T, T)
            ta_ref[pl.ds(j0, T)] = jnp.swapaxes(a3[:, pl.ds(j0, T), :], 0, 1)
            tu_ref[pl.ds(j0, T)] = jnp.swapaxes(u3[:, pl.ds(j0, T), :], 0, 1)
            return carry

        lax.fori_loop(0, L // T, to_steps, 0)

        def run1(j, carry):
            hs, ps = carry
            aj = ta_ref[j]
            return aj * hs + tu_ref[j], aj * ps

        e_ref[...], p_ref[...] = _steps(L, run1, (jnp.zeros((nseg, LANES), F32), jnp.ones((nseg, LANES), F32)))
        init_ref[0:1, :] = jnp.zeros((1, LANES), F32)
        for s in range(1, nseg):
            init_ref[s:s + 1, :] = e_ref[s - 1:s, :] + p_ref[s - 1:s, :] * init_ref[s - 1:s, :]

        def run2(j, hs):
            hs = ta_ref[j] * hs + tu_ref[j]
            tu_ref[j] = hs
            return hs

        _steps(L, run2, init_ref[...])

        def from_steps(i, carry):
            j0 = pl.multiple_of(i * T, T)
            h3[:, pl.ds(j0, T), :] = jnp.swapaxes(tu_ref[pl.ds(j0, T)], 0, 1)
            return carry

        lax.fori_loop(0, L // T, from_steps, 0)

    seg_block = _bs((nseg, L, LANES), lambda i: (0, 0, i))
    blk = 2 * (3 * S * LANES * 4) + 2 * S * LANES * 4
    return _streamed_call(
        body, name=name, grid=(C // LANES,),
        in_specs=[seg_block, seg_block],
        out_specs=seg_block,
        out_shape=jax.ShapeDtypeStruct((nseg, L, C), F32),
        scratch_shapes=[pltpu.VMEM((L, nseg, LANES), F32)] * 2 + [pltpu.VMEM((nseg, LANES), F32)] * 3,
        compiler_params=_cparams(blk),
    )(a.reshape(nseg, L, C), u.reshape(nseg, L, C)).reshape(S, C)


def _scan_bwd(a, dh, h, name):
    S, C = a.shape
    nseg, L = _scan_geometry(S)
    T = min(SUBLANES, L)
    assert L >= 2

    def body(a3, d3, h3, g3, da3, ta_ref, td_ref, th_ref, e_ref, p_ref, init_ref):

        def to_steps(i, carry):
            j0 = pl.multiple_of(i * T, T)
            for src, dst in ((a3, ta_ref), (d3, td_ref), (h3, th_ref)):
                dst[pl.ds(j0, T)] = jnp.swapaxes(src[:, pl.ds(j0, T), :], 0, 1)
            return carry

        lax.fori_loop(0, L // T, to_steps, 0)
        seg = lax.broadcasted_iota(jnp.int32, (nseg, LANES), 0)
        b_last = jnp.where(seg == nseg - 1, 0.0, pltpu.roll(ta_ref[0], nseg - 1, axis=0))
        h_first = jnp.where(seg == 0, 0.0, pltpu.roll(th_ref[L - 1], 1, axis=0))

        def run1(jj, carry):
            gs, ps = carry
            j = L - 2 - jj
            bj = ta_ref[j + 1]
            return bj * gs + td_ref[j], bj * ps

        e_ref[...], p_ref[...] = _steps(L - 1, run1, (td_ref[L - 1], b_last))
        init_ref[nseg - 1:nseg, :] = jnp.zeros((1, LANES), F32)
        for s in range(nseg - 2, -1, -1):
            init_ref[s:s + 1, :] = e_ref[s + 1:s + 2, :] + p_ref[s + 1:s + 2, :] * init_ref[s + 1:s + 2, :]

        gs = b_last * init_ref[...] + td_ref[L - 1]
        td_ref[L - 1] = gs
        th_ref[L - 1] = gs * th_ref[L - 2]

        def run2(jj, gs):
            j = L - 2 - jj
            gs = ta_ref[j + 1] * gs + td_ref[j]
            td_ref[j] = gs
            th_ref[j] = gs * th_ref[j - 1]
            return gs

        gs = _steps(L - 2, run2, gs)
        gs = ta_ref[1] * gs + td_ref[0]
        td_ref[0] = gs
        th_ref[0] = gs * h_first

        def from_steps(i, carry):
            j0 = pl.multiple_of(i * T, T)
            g3[:, pl.ds(j0, T), :] = jnp.swapaxes(td_ref[pl.ds(j0, T)], 0, 1)
            da3[:, pl.ds(j0, T), :] = jnp.swapaxes(th_ref[pl.ds(j0, T)], 0, 1)
            return carry

        lax.fori_loop(0, L // T, from_steps, 0)

    seg_block = _bs((nseg, L, LANES), lambda i: (0, 0, i))
    blk = 2 * (5 * S * LANES * 4) + 3 * S * LANES * 4
    g, da = _streamed_call(
        body, name=name, grid=(C // LANES,),
        in_specs=[seg_block] * 3,
        out_specs=[seg_block] * 2,
        out_shape=[jax.ShapeDtypeStruct((nseg, L, C), F32)] * 2,
        scratch_shapes=[pltpu.VMEM((L, nseg, LANES), F32)] * 3 + [pltpu.VMEM((nseg, LANES), F32)] * 3,
        compiler_params=_cparams(blk),
    )(a.reshape(nseg, L, C), dh.reshape(nseg, L, C), h.reshape(nseg, L, C))
    return g.reshape(S, C), da.reshape(S, C)


def _mixer_out_fwd(c4, h, rg, gc, gr, x1, wc, wr, wo, g, b, alpha, name):
    S, D = x1.shape
    R = h.shape[1]
    tm = _divisor_tile(S, 256, 16)

    def body(c4_ref, h_ref, rg_ref, gc_ref, gr_ref, x_ref, wc_ref, wr_ref, wo_ref, g_ref, b_ref,
             yc_ref, yr_ref, z_ref, y_ref):
        yc = _dot(c4_ref[...], wc_ref[...])
        q = (h_ref[...] * _gelu(rg_ref[...].astype(F32))).astype(BF16)
        yr = _dot(q, wr_ref[...])
        yc_ref[...] = yc.astype(BF16)
        yr_ref[...] = yr.astype(BF16)
        m = (_sigmoid(gc_ref[...].astype(F32)) * yc + _sigmoid(gr_ref[...].astype(F32)) * yr).astype(BF16)
        z = alpha * x_ref[...] + _dot(m, wo_ref[...])
        z_ref[...] = z
        y_ref[...] = _norm_fwd(z, g_ref[...], b_ref[...])

    blk = 2 * (tm * D * 2 + 2 * tm * R * 4 + 7 * tm * D * 4) + (2 * D * D + R * D) * 2 + 6 * tm * D * 4
    td = _bs((tm, D), lambda i: (i, 0))
    tr = _bs((tm, R), lambda i: (i, 0))
    return _streamed_call(
        body, name=name, grid=(S // tm,),
        in_specs=[td, tr, tr, td, td, td, _layer_resident(wc)[1], _layer_resident(wr)[1], _layer_resident(wo)[1],
                  _resident((1, D)), _resident((1, D))],
        out_specs=[td] * 4,
        out_shape=[jax.ShapeDtypeStruct((S, D), BF16)] * 2 + [jax.ShapeDtypeStruct((S, D), F32)] * 2,
        compiler_params=_cparams(blk),
    )(c4, h, rg, gc, gr, x1, wc[0], wr[0], wo[0], g, b)


def _mixer_out_bwd(dy, z, g, wo, yc, yr, gc, gr, name):
    S, D = dy.shape
    tm = _divisor_tile(S, 256, 16)

    def body(dy_ref, z_ref, g_ref, wo_ref, yc_ref, yr_ref, gc_ref, gr_ref,
             dz_ref, dzb_ref, m_ref, dyc_ref, dyr_ref, dgc_ref, dgr_ref, sgc_ref, sgr_ref, dg_ref, db_ref):
        @pl.when(pl.program_id(0) == 0)
        def _():
            for ref in (sgc_ref, sgr_ref, dg_ref, db_ref):
                ref[...] = jnp.zeros_like(ref)

        dy_ = dy_ref[...]
        dz, xhat = _norm_bwd(z_ref[...], g_ref[...], dy_)
        dg_ref[...] += _colsum(dy_ * xhat)
        db_ref[...] += _colsum(dy_)
        dz_ref[...] = dz
        dzb = dz.astype(BF16)
        dzb_ref[...] = dzb
        dm = _dot_nt(dzb, wo_ref[...])
        yc = yc_ref[...].astype(F32)
        yr = yr_ref[...].astype(F32)
        sc = _sigmoid(gc_ref[...].astype(F32))
        sr = _sigmoid(gr_ref[...].astype(F32))
        m_ref[...] = (sc * yc + sr * yr).astype(BF16)
        dyc_ref[...] = (dm * sc).astype(BF16)
        dyr_ref[...] = (dm * sr).astype(BF16)
        dgc = dm * yc * sc * (1.0 - sc)
        dgr = dm * yr * sr * (1.0 - sr)
        dgc_ref[...] = dgc.astype(BF16)
        dgr_ref[...] = dgr.astype(BF16)
        sgc_ref[...] += _colsum(dgc)
        sgr_ref[...] += _colsum(dgr)

    blk = 2 * (7 * tm * D * 4 + 6 * tm * D * 2) + D * D * 2 + 8 * tm * D * 4
    td = _bs((tm, D), lambda i: (i, 0))
    rowspec = _bs((1, D), lambda i: (0, 0))
    row = jax.ShapeDtypeStruct((1, D), F32)
    bfd = jax.ShapeDtypeStruct((S, D), BF16)
    return _streamed_call(
        body, name=name, grid=(S // tm,),
        in_specs=[td, td, _resident((1, D)), _layer_resident(wo)[1], td, td, td, td],
        out_specs=[td] * 7 + [rowspec] * 4,
        out_shape=[jax.ShapeDtypeStruct((S, D), F32), bfd, bfd, bfd, bfd, bfd, bfd, row, row, row, row],
        compiler_params=_cparams(blk),
    )(dy, z, g, wo[0], yc, yr, gc, gr)


def _branch_bwd(dyc, dyr, wc, wr, h, rg, name):
    S, D = dyc.shape
    R = h.shape[1]
    tm = _divisor_tile(S, 256, 16)

    def body(dyc_ref, dyr_ref, wc_ref, wr_ref, h_ref, rg_ref, dc4_ref, dh_ref, drg_ref, q_ref, srg_ref):
        @pl.when(pl.program_id(0) == 0)
        def _():
            srg_ref[...] = jnp.zeros_like(srg_ref)

        dc4_ref[...] = _dot_nt(dyc_ref[...], wc_ref[...]).astype(BF16)
        dq = _dot_nt(dyr_ref[...], wr_ref[...])
        h_ = h_ref[...]
        rg_ = rg_ref[...].astype(F32)
        ge = _gelu(rg_)
        dh_ref[...] = dq * ge
        drg = dq * h_ * _gelu_grad(rg_)
        drg_ref[...] = drg.astype(BF16)
        srg_ref[...] += _colsum(drg)
        q_ref[...] = (h_ * ge).astype(BF16)

    blk = 2 * (2 * tm * D * 2 + tm * D * 4 + 3 * tm * R * 4 + 2 * tm * R * 2) + (D * D + R * D) * 2 + 6 * tm * R * 4
    td = _bs((tm, D), lambda i: (i, 0))
    tr = _bs((tm, R), lambda i: (i, 0))
    return _streamed_call(
        body, name=name, grid=(S // tm,),
        in_specs=[td, td, _layer_resident(wc)[1], _layer_resident(wr)[1], tr, tr],
        out_specs=[td, tr, tr, tr, _bs((1, R), lambda i: (0, 0))],
        out_shape=[jax.ShapeDtypeStruct((S, D), BF16), jax.ShapeDtypeStruct((S, R), F32), jax.ShapeDtypeStruct((S, R), BF16),
                   jax.ShapeDtypeStruct((S, R), BF16), jax.ShapeDtypeStruct((1, R), F32)],
        compiler_params=_cparams(blk),
    )(dyc, dyr, wc[0], wr[0], h, rg)


def _loss_head(y, target, name):
    S, D = y.shape
    tm = _divisor_tile(S, 512, 16)
    nsteps = S // tm

    def body(y_ref, t_ref, loss_ref, dy_ref, acc_ref):
        i = pl.program_id(0)

        @pl.when(i == 0)
        def _():
            acc_ref[...] = jnp.zeros_like(acc_ref)

        err = y_ref[...] - t_ref[...]
        dy_ref[...] = err * (1.0 / D)
        acc_ref[...] += _colsum(err * err)

        @pl.when(i == nsteps - 1)
        def _():
            loss_ref[...] = jnp.sum(acc_ref[...], axis=-1, keepdims=True) * (0.5 / D)

    td = _bs((tm, D), lambda i: (i, 0))
    return _streamed_call(
        body, name=name, grid=(nsteps,),
        in_specs=[td, td],
        out_specs=[_bs((1, 1), lambda i: (0, 0)), td],
        out_shape=[jax.ShapeDtypeStruct((1, 1), F32), jax.ShapeDtypeStruct((S, D), F32)],
        scratch_shapes=[pltpu.VMEM((1, D), F32)],
        compiler_params=_cparams(2 * 3 * tm * D * 4),
    )(y, target)


def _adamw_math(w, g, m, v):
    m = ADAM_B1 * m + (1.0 - ADAM_B1) * g
    v = ADAM_B2 * v + (1.0 - ADAM_B2) * (g * g)
    m_hat = m / (1.0 - ADAM_B1 ** ADAM_STEP)
    v_hat = v / (1.0 - ADAM_B2 ** ADAM_STEP)
    delta = -ADAM_LR * (m_hat / (jnp.sqrt(v_hat) + ADAM_EPS) + ADAM_WD * w)
    return delta, m, v


def _adamw_sharded(w, m, v, own, sib, rem, layer, filled, name):
    layers, r, c = w.shape
    r2 = r // 2
    tr = _divisor_tile(r2, max(16, (1 << 20) // (4 * c) // 16 * 16), 16)
    n_out = 4

    def body(w_ref, m_ref, v_ref, own_ref, sib_ref, rem_ref, *rest):
        g_ref, d_ref, nm_ref, nv_ref = rest[-n_out:]
        mine = pl.program_id(0) == lax.axis_index("c")
        g = jnp.where(mine, own_ref[...], sib_ref[...]).astype(F32)
        for j in range(N_CHIPS - 1):
            g = g + rem_ref[j].astype(F32)
        delta, nm, nv = _adamw_math(w_ref[...], g, m_ref[...], v_ref[...])
        g_ref[...] = g
        d_ref[...] = delta
        nm_ref[...] = nm
        nv_ref[...] = nv

    halves = lambda a: a.reshape(layers, 2, r2, c)
    tile = _bs((None, None, tr, c), lambda h, i: (layer, h, i, 0))
    flat = _bs((tr, c), lambda h, i: (i, 0))
    sds = jax.ShapeDtypeStruct((layers, 2, r2, c), F32)
    passed = [] if filled is None else [halves(a) for a in filled]
    outs = _streamed_call(
        body, name=name, grid=(2, r2 // tr),
        in_specs=[tile, tile, tile, flat, flat, _bs((N_CHIPS - 1, None, tr, c), lambda h, i: (0, h, i, 0))] + [ANY] * len(passed),
        out_specs=[tile] * n_out,
        out_shape=[sds] * n_out,
        input_output_aliases={6 + k: k for k in range(len(passed))},
        compiler_params=_cparams(2 * (7 * tr * c * 4 + (N_CHIPS + 1) * tr * c * 2) + 6 * tr * c * 4),
    )(halves(w), halves(m), halves(v), own, sib, rem, *passed)
    return [o.reshape(layers, r, c) for o in outs]


def _adamw_small(ws, ms, vs, gs, name):
    n = len(ws)

    def body(*refs):
        w_refs, m_refs, v_refs, g_refs = (refs[k * n:(k + 1) * n] for k in range(4))
        d_refs, nm_refs, nv_refs = (refs[(4 + k) * n:(5 + k) * n] for k in range(3))
        for i in range(n):
            delta, nm, nv = _adamw_math(w_refs[i][...], g_refs[i][...], m_refs[i][...], v_refs[i][...])
            d_refs[i][...] = delta
            nm_refs[i][...] = nm
            nv_refs[i][...] = nv

    sds = [jax.ShapeDtypeStruct(w.shape, F32) for w in ws]
    total = sum(_nbytes(w.shape, F32) for w in ws)
    outs = pl.pallas_call(body, name=name, out_shape=sds * 3, compiler_params=_cparams(16 * total))(*ws, *ms, *vs, *gs)
    return outs[:n], outs[n:2 * n], outs[2 * n:]


def _half_view(g, by_cols):
    rows, cols = g.shape
    if by_cols:
        return g.reshape(2, rows // 2, cols)
    return g.reshape(N_CHIPS, 2, rows // (2 * N_CHIPS), cols)


def _pair_sum_bf16(view, theirs, by_cols, name):
    rows, c = theirs.shape[-2:]
    tr = _divisor_tile(rows, max(16, (1 << 20) // (4 * c) // 16 * 16), 16)

    def body(g0_ref, g1_ref, t_ref, o_ref):
        mine = jnp.where(lax.axis_index("c") == 0, g0_ref[...], g1_ref[...])
        o_ref[...] = (mine + t_ref[...].astype(F32)).astype(BF16)

    if by_cols:
        grid = (rows // tr,)
        halves = [_bs((None, tr, c), functools.partial(lambda h, i: (h, i, 0), h)) for h in range(2)]
        tile = _bs((tr, c), lambda i: (i, 0))
    else:
        grid = (N_CHIPS, rows // tr)
        halves = [_bs((None, None, tr, c), functools.partial(lambda h, k, i: (k, h, i, 0), h)) for h in range(2)]
        tile = _bs((None, tr, c), lambda k, i: (k, i, 0))
    return _streamed_call(
        body, name=name, grid=grid,
        in_specs=halves + [tile], out_specs=tile, out_shape=jax.ShapeDtypeStruct(theirs.shape, BF16),
        compiler_params=_cparams(2 * 4 * tr * c * 4),
    )(view, view, theirs)


ANY = pl.BlockSpec(memory_space=pl.ANY)


def _mesh_position():
    return lax.axis_index("x"), lax.axis_index("y"), lax.axis_index("c")


def _other_chips():
    x, y, c = _mesh_position()
    chips = [(1 - x, y), (x, 1 - y), (1 - x, 1 - y)]
    return 2 * x + y, (x, y, 1 - c), chips, [2 * cx + cy for cx, cy in chips]


def _chip_slab(ref, k, width, by_cols):
    if by_cols:
        start = k * width if isinstance(k, int) else pl.multiple_of(k * width, LANES)
        return ref.at[:, pl.ds(start, width)]
    return ref.at[k]


HBM = pl.BlockSpec(memory_space=pltpu.HBM)
SEM = pl.BlockSpec(memory_space=pltpu.SEMAPHORE)
DATAFLOW = pltpu.SideEffectType.DATAFLOW_SIDE_EFFECTING
N_GATHER_COPIES = 4


def _land_shape(src, by_cols):
    return src.shape[:-1] + (N_CHIPS * src.shape[-1],) if by_cols else (N_CHIPS,) + src.shape


def _gather_copy(src_ref, land_ref, by_cols, send_sems, recv_sems, pos, j, slab, to):
    width = src_ref.shape[-1]
    return pltpu.make_async_remote_copy(src_ref=src_ref, dst_ref=_chip_slab(land_ref, slab, width, by_cols),
                                        send_sem=send_sems.at[N_GATHER_COPIES * pos + j],
                                        recv_sem=recv_sems.at[N_GATHER_COPIES * pos + j],
                                        device_id=to, device_id_type=MESH)


def _gather_start(srcs, by_cols, groups, name):
    U = len(srcs)
    G = len(groups)
    lands = [lax.empty(_land_shape(s, bc), s.dtype) for s, bc in zip(srcs, by_cols)]

    def body(*refs):
        src = refs[:U]
        land = refs[U:2 * U]
        send_sems = refs[2 * U:2 * U + G]
        recv_sems = refs[2 * U + G:2 * U + 2 * G]
        token = refs[-1]
        c = lax.axis_index("c")
        me, sibling, chips, _ = _other_chips()
        targets = [(*chip, c) for chip in chips] + [sibling]
        for g, members in enumerate(groups):
            for pos, u in enumerate(members):
                for j, to in enumerate(targets):
                    _gather_copy(src[u], land[u], by_cols[u], send_sems[g], recv_sems[g], pos, j, me, to).start()
        token[...] = jnp.zeros_like(token)

    sem_shapes = [pltpu.SemaphoreType.DMA((len(m) * N_GATHER_COPIES,)) for m in groups]
    outs = pl.pallas_call(
        body, name=name,
        out_shape=tuple(sem_shapes + sem_shapes + [pltpu.HBM(s.shape, s.dtype) for s in srcs]
                        + [pltpu.HBM(v.shape, v.dtype) for v in lands] + [jax.ShapeDtypeStruct((SUBLANES, LANES), F32)]),
        in_specs=[HBM] * (2 * U),
        out_specs=tuple([SEM] * (2 * G) + [HBM] * (2 * U) + [pl.BlockSpec(memory_space=pltpu.VMEM)]),
        input_output_aliases={i: 2 * G + i for i in range(2 * U)},
        compiler_params=pltpu.CompilerParams(has_side_effects=DATAFLOW),
    )(*[pltpu.with_memory_space_constraint(a, pltpu.HBM) for a in list(srcs) + lands])
    return outs[:G], outs[G:2 * G], outs[2 * G:2 * G + U], outs[2 * G + U:2 * G + 2 * U]


def _gather_wait(srcs, lands, by_cols, send_sems, recv_sems, after, name):
    n = len(srcs)

    def body(*refs):
        src = refs[:n]
        land = refs[n:2 * n]
        send_ref, recv_ref = refs[2 * n:2 * n + 2]
        _, sibling, _, _ = _other_chips()
        for pos in range(n):
            for j in range(N_GATHER_COPIES):
                cp = _gather_copy(src[pos], land[pos], by_cols[pos], send_ref, recv_ref, pos, j, 0, sibling)
                cp.wait_send()
                cp.wait_recv()

    outs = pl.pallas_call(
        body, name=name,
        out_shape=tuple([pltpu.HBM(s.shape, s.dtype) for s in srcs] + [pltpu.HBM(v.shape, v.dtype) for v in lands]),
        in_specs=[HBM] * (2 * n) + [SEM, SEM, pl.BlockSpec(memory_space=pl.ANY)],
        out_specs=tuple([HBM] * (2 * n)),
        input_output_aliases={i: i for i in range(2 * n)},
        compiler_params=pltpu.CompilerParams(has_side_effects=DATAFLOW),
    )(*srcs, *lands, send_sems, recv_sems, after)
    return outs[n:]


def _scatter_grads(csums, by_cols, name):
    n = len(csums)
    shard = [(s.shape[0], s.shape[1] // N_CHIPS) if bc else s.shape[1:] for s, bc in zip(csums, by_cols)]

    def body(*refs):
        src = refs[:n]
        rem = refs[n:2 * n]
        sib = refs[2 * n:3 * n]
        send_sems, recv_sems = refs[3 * n:]
        c = lax.axis_index("c")
        me, sibling, chips, chip_ids = _other_chips()

        def remote(i, k, src_ref, dst_ref, to):
            return pltpu.make_async_remote_copy(src_ref=src_ref, dst_ref=dst_ref, send_sem=send_sems.at[i, k],
                                                recv_sem=recv_sems.at[i, k], device_id=to, device_id_type=MESH)

        def part(i, k):
            return _chip_slab(src[i], k, shard[i][-1], by_cols[i])

        started = []
        for i in range(n):
            for j in range(3):
                started.append(remote(i, j, part(i, chip_ids[j]), rem[i].at[j, c], (*chips[j], c)))
            started.append(remote(i, 6, part(i, me), sib[i], sibling))
        for cp in started:
            cp.start()
        for i in range(n):
            for j in range(3):
                slot = rem[i].at[j, c]
                remote(i, j, slot, slot, sibling).wait_recv()
                fwd = remote(i, 3 + j, slot, slot, sibling)
                fwd.start()
                started.append(fwd)
        for i in range(n):
            for j in range(3):
                slot = rem[i].at[j, 1 - c]
                remote(i, 3 + j, slot, slot, sibling).wait_recv()
            remote(i, 6, sib[i], sib[i], sibling).wait_recv()
        for cp in started:
            cp.wait_send()

    out_shape = ([jax.ShapeDtypeStruct((N_CHIPS - 1, 2) + tuple(sh), s.dtype) for s, sh in zip(csums, shard)]
                 + [jax.ShapeDtypeStruct(tuple(sh), s.dtype) for s, sh in zip(csums, shard)])
    outs = _streamed_call(
        body, name=name,
        in_specs=[ANY] * n, out_specs=[ANY] * (2 * n), out_shape=out_shape,
        scratch_shapes=[pltpu.SemaphoreType.DMA((n, 7)), pltpu.SemaphoreType.DMA((n, 7))],
    )(*csums)
    return outs[:n], outs[n:]


def _sibling_exchange(views, by_cols, name):
    n = len(views)

    def body(*refs):
        src = refs[:n]
        theirs = refs[n:2 * n]
        send_sems, recv_sems = refs[2 * n:]
        x, y, c = _mesh_position()
        copies = []
        for i in range(n):
            half = src[i].at[1 - c] if by_cols[i] else src[i].at[:, 1 - c]
            copies.append(pltpu.make_async_remote_copy(src_ref=half, dst_ref=theirs[i], send_sem=send_sems.at[i],
                                                       recv_sem=recv_sems.at[i], device_id=(x, y, 1 - c), device_id_type=MESH))
        for cp in copies:
            cp.start()
        for cp in copies:
            cp.wait()

    out_shape = [jax.ShapeDtypeStruct(v.shape[1:] if bc else v.shape[:1] + v.shape[2:], v.dtype) for v, bc in zip(views, by_cols)]
    return _streamed_call(
        body, name=name,
        in_specs=[ANY] * n, out_specs=[ANY] * n, out_shape=out_shape,
        scratch_shapes=[pltpu.SemaphoreType.DMA((n,)), pltpu.SemaphoreType.DMA((n,))],
    )(*views)


N_SCATTER_COPIES = 7


def _scatter_start(csums, by_cols, name):
    n = len(csums)
    shard = [(s.shape[0], s.shape[1] // N_CHIPS) if bc else s.shape[1:] for s, bc in zip(csums, by_cols)]
    rems = [lax.empty((N_CHIPS - 1, 2) + tuple(sh), s.dtype) for s, sh in zip(csums, shard)]
    sibs = [lax.empty(tuple(sh), s.dtype) for s, sh in zip(csums, shard)]

    def body(*refs):
        src = refs[:n]
        rem = refs[n:2 * n]
        sib = refs[2 * n:3 * n]
        send_sems, recv_sems = refs[3 * n:3 * n + 2]
        token = refs[-1]
        c = lax.axis_index("c")
        me, sibling, chips, chip_ids = _other_chips()
        for i in range(n):
            base = N_SCATTER_COPIES * i
            for j in range(3):
                part = _chip_slab(src[i], chip_ids[j], shard[i][-1], by_cols[i])
                for core in range(2):
                    pltpu.make_async_remote_copy(src_ref=part, dst_ref=rem[i].at[j, c], send_sem=send_sems.at[base + 2 * j + core],
                                                 recv_sem=recv_sems.at[base + 2 * j + c], device_id=(*chips[j], core),
                                                 device_id_type=MESH).start()
            pltpu.make_async_remote_copy(src_ref=_chip_slab(src[i], me, shard[i][-1], by_cols[i]), dst_ref=sib[i],
                                         send_sem=send_sems.at[base + 6], recv_sem=recv_sems.at[base + 6], device_id=sibling,
                                         device_id_type=MESH).start()
        token[...] = jnp.zeros_like(token)

    sems = pltpu.SemaphoreType.DMA((N_SCATTER_COPIES * n,))
    operands = list(csums) + rems + sibs
    outs = pl.pallas_call(
        body, name=name,
        out_shape=tuple([sems, sems] + [pltpu.HBM(a.shape, a.dtype) for a in operands] + [jax.ShapeDtypeStruct((SUBLANES, LANES), F32)]),
        in_specs=[HBM] * (3 * n),
        out_specs=tuple([SEM, SEM] + [HBM] * (3 * n) + [pl.BlockSpec(memory_space=pltpu.VMEM)]),
        input_output_aliases={i: 2 + i for i in range(3 * n)},
        compiler_params=pltpu.CompilerParams(has_side_effects=DATAFLOW),
    )(*[pltpu.with_memory_space_constraint(a, pltpu.HBM) for a in operands])
    return (outs[0], outs[1], outs[2:2 + n], outs[2 + n:2 + 2 * n], outs[2 + 2 * n:2 + 3 * n]), outs[-1]


def _scatter_wait(send_sems, recv_sems, srcs, rems, sibs, by_cols, after, name):
    n = len(srcs)

    def body(*refs):
        src = refs[:n]
        rem = refs[n:2 * n]
        sib = refs[2 * n:3 * n]
        send_ref, recv_ref = refs[3 * n:3 * n + 2]
        _, sibling, _, _ = _other_chips()
        for i in range(n):
            base = N_SCATTER_COPIES * i
            width = sib[i].shape[-1]
            for j in range(3):
                for core in range(2):
                    cp = pltpu.make_async_remote_copy(src_ref=_chip_slab(src[i], 0, width, by_cols[i]), dst_ref=rem[i].at[j, core],
                                                      send_sem=send_ref.at[base + 2 * j + core],
                                                      recv_sem=recv_ref.at[base + 2 * j + core], device_id=sibling,
                                                      device_id_type=MESH)
                    cp.wait_send()
                    cp.wait_recv()
            cp = pltpu.make_async_remote_copy(src_ref=_chip_slab(src[i], 0, width, by_cols[i]), dst_ref=sib[i],
                                              send_sem=send_ref.at[base + 6], recv_sem=recv_ref.at[base + 6], device_id=sibling,
                                              device_id_type=MESH)
            cp.wait_send()
            cp.wait_recv()

    operands = list(srcs) + list(rems) + list(sibs)
    outs = pl.pallas_call(
        body, name=name,
        out_shape=tuple(pltpu.HBM(a.shape, a.dtype) for a in operands),
        in_specs=[HBM] * (3 * n) + [SEM, SEM, pl.BlockSpec(memory_space=pl.ANY)],
        out_specs=tuple([HBM] * (3 * n)),
        input_output_aliases={i: i for i in range(3 * n)},
        compiler_params=pltpu.CompilerParams(has_side_effects=DATAFLOW),
    )(*operands, send_sems, recv_sems, after)
    return outs[:n], outs[n:2 * n], outs[2 * n:3 * n]


def _all_reduce_small(v, name):
    _, rows, _ = v.shape

    def body(v_ref, o_ref, recv_ref, send_sems, recv_sems):
        x, y, c = _mesh_position()
        me = 4 * x + 2 * y + c
        peers = []
        for d in range(1, N_DEV):
            px, py, pc = x ^ ((d >> 2) & 1), y ^ ((d >> 1) & 1), c ^ (d & 1)
            peers.append(((px, py, pc), 4 * px + 2 * py + pc))

        def remote(k, src_ref, dst_ref, to):
            return pltpu.make_async_remote_copy(src_ref=src_ref, dst_ref=dst_ref, send_sem=send_sems.at[k],
                                                recv_sem=recv_sems.at[k], device_id=to, device_id_type=MESH)

        scatter = [remote(d, v_ref.at[pid], recv_ref.at[me], to) for d, (to, pid) in enumerate(peers)]
        for cp in scatter:
            cp.start()
        recv_ref[pl.ds(me, 1)] = v_ref[pl.ds(me, 1)]
        for d, (to, pid) in enumerate(peers):
            remote(d, v_ref.at[pid], recv_ref.at[pid], to).wait_recv()
        total = recv_ref[0]
        for s in range(1, N_DEV):
            total = total + recv_ref[s]
        o_ref[pl.ds(me, 1)] = total[None]
        gather = [remote(N_DEV - 1 + d, o_ref.at[me], o_ref.at[me], to) for d, (to, pid) in enumerate(peers)]
        for cp in gather:
            cp.start()
        for d, (to, pid) in enumerate(peers):
            remote(N_DEV - 1 + d, o_ref.at[pid], o_ref.at[pid], to).wait_recv()
        for cp in scatter + gather:
            cp.wait_send()

    vm = pl.BlockSpec(memory_space=pltpu.VMEM)
    return pl.pallas_call(
        body, name=name,
        in_specs=[vm], out_specs=vm, out_shape=jax.ShapeDtypeStruct(v.shape, F32),
        scratch_shapes=[pltpu.VMEM(v.shape, F32), pltpu.SemaphoreType.DMA((2 * (N_DEV - 1),)),
                        pltpu.SemaphoreType.DMA((2 * (N_DEV - 1),))],
        compiler_params=_cparams(4 * _nbytes(v.shape, F32)),
    )(v)


SHARDED_MATS = ("ffn1_w_gu", "ffn1_w_down", "mix_w_in", "conv_w_proj", "rnn_w_proj", "mix_w_out", "ffn2_w_gu", "ffn2_w_down")
COL_SHARDED = ("ffn1_w_gu", "ffn2_w_gu", "conv_dw_w")
SHARDED_VECS = ("conv_dw_w", "rnn_conv_w")
WEIGHT_NAMES = ("ffn1_w_gu", "ffn1_w_down", "ln1_g", "ln1_b", "mix_w_in", "mix_b_in", "conv_dw_w", "conv_dw_b", "conv_gn_g",
                "conv_gn_b", "conv_w_proj", "rnn_conv_w", "rnn_conv_b", "rnn_w_a", "rnn_b_a", "rnn_w_x", "rnn_b_x",
                "rnn_lambda", "rnn_w_proj", "mix_w_out", "ln2_g", "ln2_b", "ffn2_w_gu", "ffn2_w_down", "ln3_g", "ln3_b")
SMALL_NAMES = tuple(n for n in WEIGHT_NAMES if n not in SHARDED_MATS)
SECTION_NAMES = ("cv", "cg", "rx", "rg", "gc", "gr")


def _unshard_cols(gathered):
    k4, K, n = gathered.shape
    return jnp.transpose(gathered, (1, 0, 2)).reshape(K, k4 * n)


def _row(v):
    return v.reshape(1, -1)


def _layer_forward(x0, p, alpha, l, hooks):
    t = f"l{l}_"
    sv = {"x0": x0}
    x1, sv["z1"], sv["hg1"], sv["hu1"] = _ffn_fwd(x0, p["wgu1"], p["wd1"], p["ln1_g"], p["ln1_b"], alpha, t + "ffn1_fwd")
    sv["x1"] = x1
    hooks.get("after_ffn1", lambda v: None)(x1)
    sec = dict(zip(SECTION_NAMES, _mix_in(x1, p["win"], p["bin"], p["sections"], t + "mix_in")))
    sv.update(sec)
    sv["c2"], c4 = _conv_branch_fwd(sec["cv"], sec["cg"], p["conv_dw_w"], p["conv_dw_b"], p["conv_gn_g"], p["conv_gn_b"], t + "conv_fwd")
    sv["c4"] = c4
    r1 = _short_conv_fwd(sec["rx"], p["rnn_conv_w"], p["rnn_conv_b"], t + "rconv_fwd")
    sv["r1"] = r1
    sv["ra"], sv["ri"], a, uu = _gates_fwd(r1, p["wa"], p["wx"], p["rnn_b_a"], p["rnn_b_x"], p["rnn_lambda"], t + "gates_fwd")
    sv["a"] = a
    h = _scan_fwd(a, uu, t + "scan_fwd")
    sv["h"] = h
    hooks.get("after_scan", lambda v: None)(h)
    sv["yc"], sv["yr"], sv["z2"], x2 = _mixer_out_fwd(c4, h, sec["rg"], sec["gc"], sec["gr"], x1, p["wc"], p["wr"], p["wo"],
                                                      p["ln2_g"], p["ln2_b"], alpha, t + "mixout_fwd")
    sv["x2"] = x2
    hooks.get("after_mixer", lambda v: None)(x2)
    x3, sv["z3"], sv["hg2"], sv["hu2"] = _ffn_fwd(x2, p["wgu2"], p["wd2"], p["ln3_g"], p["ln3_b"], alpha, t + "ffn2_fwd")
    hooks.get("after_layer", lambda v: None)(x3)
    return x3, sv


def _layer_backward(dy, p, sv, alpha, l, before_ffn1=None):
    t = f"l{l}_"
    g, gb = {}, {}
    dx2, df, a_act, dhg, dhu, g["ln3_g"], g["ln3_b"] = _ffn_bwd(dy, sv["z3"], sv["hg2"], sv["hu2"], p["wgu2"], p["wd2"],
                                                                 p["ln3_g"], alpha, t + "ffn2_bwd")
    g["ffn2_w_down"], gb["ffn2_w_down"] = _mm_tn(a_act, df, t + "dwd2")
    g["ffn2_w_gu"], gb["ffn2_w_gu"] = _mm_tn_pair(sv["x2"], dhg, dhu, t + "dwgu2")
    (dz2, dz2b, m_b, dyc, dyr, dgc, dgr, s_gc, s_gr, g["ln2_g"], g["ln2_b"]) = _mixer_out_bwd(
        dx2, sv["z2"], p["ln2_g"], p["wo"], sv["yc"], sv["yr"], sv["gc"], sv["gr"], t + "mixout_bwd")
    g["mix_w_out"], gb["mix_w_out"] = _mm_tn(m_b, dz2b, t + "dwo")
    dc4, dh, drg, q_b, s_rg = _branch_bwd(dyc, dyr, p["wc"], p["wr"], sv["h"], sv["rg"], t + "branch_bwd")
    g["conv_w_proj"], gb["conv_w_proj"] = _mm_tn(sv["c4"], dyc, t + "dwc")
    g["rnn_w_proj"], gb["rnn_w_proj"] = _mm_tn(q_b, dyr, t + "dwr")
    (dcv, dcg, g["conv_dw_w"], g["conv_dw_b"], g["conv_gn_g"], g["conv_gn_b"], s_cv, s_cg) = _conv_branch_bwd(
        dc4, sv["c2"], sv["cv"], sv["cg"], p["conv_dw_w"], p["conv_gn_g"], p["conv_gn_b"], t + "conv_bwd")
    guu, da = _scan_bwd(sv["a"], dh, sv["h"], t + "scan_bwd")
    dr1, dpa, dpx, g["rnn_b_a"], g["rnn_b_x"], g["rnn_lambda"] = _gates_bwd(
        guu, da, sv["ra"], sv["ri"], sv["r1"], p["wa"], p["wx"], p["rnn_lambda"], t + "gates_bwd")
    g["rnn_w_a"], g["rnn_w_x"] = _block_grads(sv["r1"], dpa, dpx, t + "dwax")
    drx, g["rnn_conv_w"], g["rnn_conv_b"], s_rx = _short_conv_bwd(dr1, sv["rx"], p["rnn_conv_w"], t + "rconv_bwd")
    du = {"cv": dcv, "cg": dcg, "rx": drx, "rg": drg, "gc": dgc, "gr": dgr}
    order = ("cv", "cg", "rx", "rg", "gc", "gr")
    pieces = [_mm_tn(du[s], sv["x1"], t + "dwin_" + s) for s in order]
    g["mix_w_in"] = jnp.concatenate([f for f, _ in pieces], axis=0)
    gb["mix_w_in"] = jnp.concatenate([h for _, h in pieces], axis=0)
    g["mix_b_in"] = jnp.concatenate([s_cv, s_cg, s_rx, s_rg, s_gc, s_gr], axis=1)
    dx1 = _mix_dx(dz2, [du[s] for s in order], p["win"], p["sections"], alpha, t + "mix_dx")
    ln1_g = p["ln1_g"] if before_ffn1 is None else p["ln1_g"] + before_ffn1(g, gb)[0:1, 0:1]
    dx0, df, a_act, dhg, dhu, g["ln1_g"], g["ln1_b"] = _ffn_bwd(dx1, sv["z1"], sv["hg1"], sv["hu1"], p["wgu1"], p["wd1"],
                                                                 ln1_g, alpha, t + "ffn1_bwd")
    g["ffn1_w_down"], gb["ffn1_w_down"] = _mm_tn(a_act, df, t + "dwd1")
    g["ffn1_w_gu"], gb["ffn1_w_gu"] = _mm_tn_pair(sv["x0"], dhg, dhu, t + "dwgu1")
    return dx0, g, gb


def _pack_small(arrays, piece_rows):
    flat = jnp.concatenate([a.reshape(-1) for a in arrays])
    total = N_DEV * piece_rows * LANES
    return jnp.pad(flat, (0, total - flat.shape[0])).reshape(N_DEV, piece_rows, LANES)


def _unpack_small(packed, shapes):
    flat = packed.reshape(-1)
    out, off = [], 0
    for shp in shapes:
        n = 1
        for s in shp:
            n *= s
        out.append(flat[off:off + n].reshape(shp))
        off += n
    return out


def kernel(x, ffn1_w_gu, ffn1_w_down, ln1_g, ln1_b, mix_w_in, mix_b_in, conv_dw_w, conv_dw_b, conv_gn_g, conv_gn_b, conv_w_proj, rnn_conv_w, rnn_conv_b, rnn_w_a, rnn_b_a, rnn_w_x, rnn_b_x, rnn_lambda, rnn_w_proj, mix_w_out, ln2_g, ln2_b, ffn2_w_gu, ffn2_w_down, ln3_g, ln3_b, loss_target, m_ffn1_w_gu, m_ffn1_w_down, m_ln1_g, m_ln1_b, m_mix_w_in, m_mix_b_in, m_conv_dw_w, m_conv_dw_b, m_conv_gn_g, m_conv_gn_b, m_conv_w_proj, m_rnn_conv_w, m_rnn_conv_b, m_rnn_w_a, m_rnn_b_a, m_rnn_w_x, m_rnn_b_x, m_rnn_lambda, m_rnn_w_proj, m_mix_w_out, m_ln2_g, m_ln2_b, m_ffn2_w_gu, m_ffn2_w_down, m_ln3_g, m_ln3_b, v_ffn1_w_gu, v_ffn1_w_down, v_ln1_g, v_ln1_b, v_mix_w_in, v_mix_b_in, v_conv_dw_w, v_conv_dw_b, v_conv_gn_g, v_conv_gn_b, v_conv_w_proj, v_rnn_conv_w, v_rnn_conv_b, v_rnn_w_a, v_rnn_b_a, v_rnn_w_x, v_rnn_b_x, v_rnn_lambda, v_rnn_w_proj, v_mix_w_out, v_ln2_g, v_ln2_b, v_ffn2_w_gu, v_ffn2_w_down, v_ln3_g, v_ln3_b):
    args = locals()
    W = {n: args[n] for n in WEIGHT_NAMES}
    M = {n: args["m_" + n] for n in WEIGHT_NAMES}
    V = {n: args["v_" + n] for n in WEIGHT_NAMES}
    depth = ln1_g.shape[0]
    assert depth == 2, "each core of a chip moves one layer's weights and gradients"
    alpha = float((2 * depth) ** 0.25)
    S, D = x.shape[1], x.shape[2]
    F = ffn1_w_down.shape[1] * N_CHIPS
    R = rnn_w_proj.shape[1] * N_CHIPS
    chip = 2 * lax.axis_index("x") + lax.axis_index("y")

    for d in (W, M, V):
        d["mix_w_in"] = jnp.transpose(d["mix_w_in"], (0, 2, 1))

    names = SHARDED_MATS + SHARDED_VECS
    unit_groups = [[(0, "ffn1_w_gu"), (0, "ffn1_w_down")], [(0, "mix_w_in"), (0, "conv_dw_w"), (0, "rnn_conv_w")],
                   [(0, "conv_w_proj"), (0, "rnn_w_proj"), (0, "mix_w_out")], [(0, "ffn2_w_gu"), (0, "ffn2_w_down")],
                   [(1, n) for n in names]]
    order = [u for g in unit_groups for u in g]
    index = {u: i for i, u in enumerate(order)}
    groups = [[index[u] for u in g] for g in unit_groups]
    srcs = [W[n][l].astype(BF16) if n in SHARDED_MATS else W[n][l] for l, n in order]
    by_cols = [n in COL_SHARDED for _, n in order]
    send_sems, recv_sems, src_thru, land_thru = _gather_start(srcs, by_cols, groups, "gather_start")

    sections = ((0, D, F32), (D, D, F32), (2 * D, R, F32), (2 * D + R, R, BF16), (2 * D + 2 * R, D, BF16),
                (3 * D + 2 * R, D, BF16))
    keys = {"ffn1_w_gu": "wgu1", "ffn1_w_down": "wd1", "ffn2_w_gu": "wgu2", "ffn2_w_down": "wd2", "conv_w_proj": "wc",
            "rnn_w_proj": "wr", "mix_w_out": "wo"}
    params = []
    for l in range(depth):
        p = {"wa": _embed_blocks(rnn_w_a[l], f"l{l}_embed_wa"), "wx": _embed_blocks(rnn_w_x[l], f"l{l}_embed_wx")}
        for n in ("ln1_g", "ln1_b", "ln2_g", "ln2_b", "ln3_g", "ln3_b", "conv_dw_b", "conv_gn_g", "conv_gn_b", "rnn_conv_b",
                  "rnn_b_a", "rnn_b_x", "rnn_lambda"):
            p[n] = _row(W[n][l])
        p["bin"] = _row(mix_b_in[l])
        p["sections"] = sections
        params.append(p)

    def wait_group(g, after):
        ids = groups[g]
        landed = _gather_wait([src_thru[i] for i in ids], [land_thru[i] for i in ids], [by_cols[i] for i in ids],
                              send_sems[g], recv_sems[g], after, f"gather_wait{g}")
        for i, full in zip(ids, landed):
            l, n = order[i]
            p = params[l]
            if n not in COL_SHARDED:
                full = full.reshape((N_CHIPS * full.shape[1],) + full.shape[2:])
            if n == "mix_w_in":
                p["win"] = full
            elif n == "rnn_conv_w":
                p[n] = _unshard_cols(landed[ids.index(i)])
            elif n == "conv_dw_w":
                p[n] = full
            else:
                p[keys[n]] = (full[None], 0)

    h = x[0]
    wait_group(0, h)
    saved = []
    hooks = [{"after_ffn1": lambda v: wait_group(1, v), "after_scan": lambda v: wait_group(2, v),
              "after_mixer": lambda v: wait_group(3, v), "after_layer": lambda v: wait_group(4, v)}, {}]
    for l in range(depth):
        h, sv = _layer_forward(h, params[l], alpha, l, hooks[l])
        saved.append(sv)
    loss_part, dy = _loss_head(h, loss_target[0], "loss_head")
    loss = lax.psum(loss_part[0, 0], ("x", "y", "c"))
    def pair_sums(names, g, gb, tag):
        cols = [n in COL_SHARDED for n in names]
        theirs = _sibling_exchange([_half_view(gb[n], bc) for n, bc in zip(names, cols)], cols, "pair_exchange" + tag)
        return cols, [_pair_sum_bf16(_half_view(g[n], bc), t, bc, f"pair_sum{tag}_{n}") for n, bc, t in zip(names, cols, theirs)]

    pending = []

    def start_scatter(layer, names, g, gb, tag):
        cols, sums = pair_sums(names, g, gb, tag)
        in_flight, token = _scatter_start(sums, cols, "scatter_start" + tag)
        pending.append((layer, names, cols, in_flight, tag))
        return token

    early = [n for n in SHARDED_MATS if not n.startswith("ffn1_")]
    late = [n for n in SHARDED_MATS if n.startswith("ffn1_")]
    grads, grads_bf16 = [None] * depth, [None] * depth
    dy, grads[1], grads_bf16[1] = _layer_backward(dy, params[1], saved[1], alpha, 1)
    token = start_scatter(1, SHARDED_MATS, grads[1], grads_bf16[1], "1")
    first = dict(params[0], ln3_g=params[0]["ln3_g"] + token[0:1, 0:1])
    dy, grads[0], grads_bf16[0] = _layer_backward(dy, first, saved[0], alpha, 0,
                                                  before_ffn1=lambda g, gb: start_scatter(0, early, g, gb, "0a"))
    grad_x = dy[None]
    late_cols, late_sums = pair_sums(late, grads[0], grads_bf16[0], "0b")
    late_rem, late_sib = _scatter_grads(late_sums, late_cols, "scatter_grads0b")
    partial = {(0, n): part for n, part in zip(late, zip(late_sums, late_rem, late_sib))}
    for layer, names, cols, in_flight, tag in pending:
        for n, part in zip(names, zip(*_scatter_wait(*in_flight, cols, late_rem[0], "scatter_wait" + tag))):
            partial[(layer, n)] = part

    results = {}
    for l in (1, 0):
        for n in SHARDED_MATS:
            cs, rm, sb = partial[(l, n)]
            if n in COL_SHARDED:
                width = cs.shape[1] // N_CHIPS
                own = lax.dynamic_slice_in_dim(cs, chip * width, width, axis=1)
            else:
                own = lax.dynamic_index_in_dim(cs, chip, axis=0, keepdims=False)
            results[n] = _adamw_sharded(W[n], M[n], V[n], own, sb, rm, l, results.get(n), f"adamw{l}_{n}")
    out_g, out_d, out_m, out_v = {}, {}, {}, {}
    for n in SHARDED_MATS:
        outs = results[n]
        if n == "mix_w_in":
            outs = [jnp.transpose(o, (0, 2, 1)) for o in outs]
        out_g[n], out_d[n], out_m[n], out_v[n] = outs

    small_grads = [jnp.stack([grads[l][n].reshape(W[n].shape[1:] if n not in SHARDED_VECS else
                                                   (W[n].shape[1], W[n].shape[2] * N_CHIPS)) for l in range(depth)])
                   for n in SMALL_NAMES]
    n_small = sum(int(a.size) for a in small_grads)
    piece_rows = -(-n_small // (N_DEV * LANES * SUBLANES)) * SUBLANES
    reduced = _unpack_small(_all_reduce_small(_pack_small(small_grads, piece_rows), "all_reduce_small"),
                            [a.shape for a in small_grads])
    local_g = []
    for n, gr in zip(SMALL_NAMES, reduced):
        if n in SHARDED_VECS:
            width = W[n].shape[2]
            gr = lax.dynamic_slice_in_dim(gr, chip * width, width, axis=2)
        local_g.append(gr)
    deltas, new_m, new_v = _adamw_small([W[n] for n in SMALL_NAMES], [M[n] for n in SMALL_NAMES],
                                        [V[n] for n in SMALL_NAMES], local_g, "adamw_small")
    for n, gr, d_, m_, v_ in zip(SMALL_NAMES, local_g, deltas, new_m, new_v):
        out_g[n], out_d[n], out_m[n], out_v[n] = gr, d_, m_, v_

    return (loss, grad_x, *[out_g[n] for n in WEIGHT_NAMES], *[out_d[n] for n in WEIGHT_NAMES],
            *[out_m[n] for n in WEIGHT_NAMES], *[out_v[n] for n in WEIGHT_NAMES])
```

```python
import functools

import jax
import jax.numpy as jnp
from jax import lax
from jax.experimental import pallas as pl
from jax.experimental.pallas import tpu as pltpu

F32 = jnp.float32
BF16 = jnp.bfloat16
MESH = pl.DeviceIdType.MESH

LN_EPS = 1e-5
CONV_GROUPS = 8
RNN_BLOCKS = 16
RG_LRU_C = 8.0
ADAM_LR = 0.001
ADAM_B1 = 0.9
ADAM_B2 = 0.999
ADAM_EPS = 1e-08
ADAM_WD = 0.01
ADAM_STEP = 10

LANES = 128
SUBLANES = 8
V7X_VMEM_BYTES = 64 << 20
VMEM_LIMIT_CAP = V7X_VMEM_BYTES - (6 << 20)
N_CHIPS = 4
N_DEV = 8
CONV_ROWS = 64
EW_ROWS = 1024
SCAN_SEGMENTS = 32
SCAN_UNROLL = 4


def _cparams(block_bytes):
    limit = min(VMEM_LIMIT_CAP, max(int(block_bytes) + (8 << 20), 24 << 20))
    return pltpu.CompilerParams(vmem_limit_bytes=limit)


def _nbytes(shape, dtype):
    n = 1
    for s in shape:
        n *= s
    return n * jnp.dtype(dtype).itemsize


def _divisor_tile(n, limit, quantum):
    if n <= limit:
        return n
    best = None
    for t in range(quantum, limit + 1, quantum):
        if n % t == 0:
            best = t
    assert best is not None, (n, limit, quantum)
    return best


def _bs(shape, imap, **kw):
    return pl.BlockSpec(shape, imap, **kw)


def _resident(shape):
    nd = len(shape)
    return pl.BlockSpec(shape, lambda *_: (0,) * nd, pipeline_mode=pl.Buffered(1))


def _streamed_call(body, **kw):
    call = pl.pallas_call(body, **kw)
    return lambda *operands: call(*[pltpu.with_memory_space_constraint(o, pltpu.HBM) for o in operands])


def _layer_block(w, block, imap, **kw):
    arr, layer = w
    return arr, pl.BlockSpec((None,) + block, lambda *ids: (layer,) + imap(*ids), **kw)


def _layer_resident(w):
    arr, _ = w
    return _layer_block(w, arr.shape[1:], lambda *_: (0, 0), pipeline_mode=pl.Buffered(1))


def _sigmoid(x):
    return jax.nn.sigmoid(x)


def _dot(a, b):
    return jnp.dot(a, b, preferred_element_type=F32)


def _dot_nt(a, b):
    return lax.dot_general(a, b, (((1,), (1,)), ((), ())), preferred_element_type=F32)


def _dot_tn(a, b):
    return lax.dot_general(a, b, (((0,), (0,)), ((), ())), preferred_element_type=F32)


def _row_mean(z):
    return jnp.mean(z, axis=-1, keepdims=True)


def _lane_mean(z):
    hi = z.astype(BF16)
    lo = (z - hi.astype(F32)).astype(BF16)
    ones = jnp.full((2 * LANES, LANES), 1.0 / LANES, BF16)
    return jnp.dot(jnp.concatenate([hi, lo], axis=-1), ones, preferred_element_type=F32)


def _norm_fwd(z, g, b, mean=_row_mean):
    mu = mean(z)
    xc = z - mu
    var = mean(xc * xc)
    return xc * lax.rsqrt(var + LN_EPS) * g + b


def _norm_bwd(z, g, dy, mean=_row_mean):
    mu = mean(z)
    xc = z - mu
    var = mean(xc * xc)
    rstd = lax.rsqrt(var + LN_EPS)
    xhat = xc * rstd
    dxh = dy * g
    m1 = mean(dxh)
    m2 = mean(dxh * xhat)
    return rstd * (dxh - m1 - xhat * m2), xhat


GELU_K = 0.7978845608028654
GELU_C = 0.044715


def _gelu(x):
    return 0.5 * x * (1.0 + jnp.tanh(GELU_K * (x + GELU_C * x * x * x)))


def _gelu_grad(x):
    t = jnp.tanh(GELU_K * (x + GELU_C * x * x * x))
    return 0.5 * (1.0 + t) + 0.5 * x * (1.0 - t * t) * GELU_K * (1.0 + 3.0 * GELU_C * x * x)


def _softplus(y):
    return jnp.maximum(y, 0.0) + jnp.log1p(jnp.exp(-jnp.abs(y)))


def _neg_expm1(y):
    series = -y * (1.0 + y * (0.5 + y * (1.0 / 6.0 + y * (1.0 / 24.0 + y * (1.0 / 120.0 + y * (1.0 / 720.0))))))
    return jnp.where(y > -0.25, series, 1.0 - jnp.exp(y))


def _colsum(x):
    return jnp.sum(x, axis=0, keepdims=True)


def _shifted_taps(src_ref, base, rows, taps):
    acc = None
    for o, coef in taps:
        term = coef() * src_ref[pl.ds(base + o, rows), :]
        acc = term if acc is None else acc + term
    return acc


def _shifted_corr(src_ref, base, rows, d, acc_ref, offs):
    for k, o in enumerate(offs):
        prod = d * src_ref[pl.ds(base + o, rows), :]
        part = jnp.sum(prod.reshape(rows // SUBLANES, SUBLANES, prod.shape[-1]), axis=0)
        acc_ref[SUBLANES * k:SUBLANES * (k + 1), :] += part


def _front_pad(ktaps):
    return SUBLANES * ((ktaps - 1 + SUBLANES - 1) // SUBLANES)


def _pad_rows(ktaps):
    return _front_pad(ktaps) + SUBLANES


def _ffn_tiles(S, F):
    tm = _divisor_tile(S, 1024, 16)
    tf = _divisor_tile(F, 256, LANES)
    return tm, tf


def _ffn_fwd(x, wgu, wd, g, b, alpha, name):
    S, D = x.shape
    F = wd[0].shape[1]
    tm, tf = _ffn_tiles(S, F)
    nf = F // tf
    wg_arr, wg_spec = _layer_block(wgu, (D, tf), lambda i, j: (0, j))
    wu_arr, wu_spec = _layer_block(wgu, (D, tf), lambda i, j: (0, nf + j))
    wd_arr, wd_spec = _layer_block(wd, (tf, D), lambda i, j: (j, 0))

    def body(x_ref, wg_ref, wu_ref, wd_ref, g_ref, b_ref, y_ref, z_ref, hg_ref, hu_ref, acc_ref, xb_ref):
        j = pl.program_id(1)

        @pl.when(j == 0)
        def _():
            xb_ref[...] = x_ref[...].astype(BF16)
            acc_ref[...] = jnp.zeros_like(acc_ref)

        xb = xb_ref[...]
        hg = _dot(xb, wg_ref[...])
        hu = _dot(xb, wu_ref[...])
        hg_ref[...] = hg
        hu_ref[...] = hu
        a = (hg * _sigmoid(hg) * hu).astype(BF16)
        acc_ref[...] += _dot(a, wd_ref[...])

        @pl.when(j == nf - 1)
        def _():
            z = alpha * x_ref[...] + 0.5 * acc_ref[...]
            z_ref[...] = z
            y_ref[...] = _norm_fwd(z, g_ref[...], b_ref[...])

    blk = 2 * (3 * tm * D * 4 + 2 * tm * tf * 4 + 3 * D * tf * 2) + tm * D * 6 + 3 * tm * tf * 4
    return _streamed_call(
        body, name=name, grid=(S // tm, nf),
        in_specs=[_bs((tm, D), lambda i, j: (i, 0)), wg_spec, wu_spec, wd_spec,
                  _bs((1, D), lambda i, j: (0, 0)), _bs((1, D), lambda i, j: (0, 0))],
        out_specs=[_bs((tm, D), lambda i, j: (i, 0)), _bs((tm, D), lambda i, j: (i, 0)),
                   _bs((tm, tf), lambda i, j: (i, j)), _bs((tm, tf), lambda i, j: (i, j))],
        out_shape=[jax.ShapeDtypeStruct((S, D), F32), jax.ShapeDtypeStruct((S, D), F32),
                   jax.ShapeDtypeStruct((S, F), F32), jax.ShapeDtypeStruct((S, F), F32)],
        scratch_shapes=[pltpu.VMEM((tm, D), F32), pltpu.VMEM((tm, D), BF16)],
        compiler_params=_cparams(blk),
    )(x, wg_arr, wu_arr, wd_arr, g, b)


def _ffn_bwd(dy, z, hg, hu, wgu, wd, g, alpha, name):
    S, D = dy.shape
    F = wd[0].shape[1]
    tm, tf = _ffn_tiles(S, F)
    nf = F // tf
    wg_arr, wg_spec = _layer_block(wgu, (D, tf), lambda i, j: (0, j))
    wu_arr, wu_spec = _layer_block(wgu, (D, tf), lambda i, j: (0, nf + j))
    wd_arr, wd_spec = _layer_block(wd, (tf, D), lambda i, j: (j, 0))

    def body(dy_ref, z_ref, hg_ref, hu_ref, wg_ref, wu_ref, wd_ref, g_ref,
             dx_ref, df_ref, a_ref, dhg_ref, dhu_ref, dg_ref, db_ref, acc_ref):
        i = pl.program_id(0)
        j = pl.program_id(1)

        @pl.when((i == 0) & (j == 0))
        def _():
            dg_ref[...] = jnp.zeros_like(dg_ref)
            db_ref[...] = jnp.zeros_like(db_ref)

        @pl.when(j == 0)
        def _():
            dy_ = dy_ref[...]
            dz, xhat = _norm_bwd(z_ref[...], g_ref[...], dy_)
            dg_ref[...] += _colsum(dy_ * xhat)
            db_ref[...] += _colsum(dy_)
            acc_ref[...] = alpha * dz
            df_ref[...] = (0.5 * dz).astype(BF16)

        da = _dot_nt(df_ref[...], wd_ref[...])
        hg_ = hg_ref[...]
        hu_ = hu_ref[...]
        s = _sigmoid(hg_)
        sl = hg_ * s
        dgate = (da * hu_ * (s * (1.0 + hg_ * (1.0 - s)))).astype(BF16)
        dup = (da * sl).astype(BF16)
        a_ref[...] = (sl * hu_).astype(BF16)
        dhg_ref[...] = dgate
        dhu_ref[...] = dup
        acc_ref[...] += _dot_nt(dgate, wg_ref[...]) + _dot_nt(dup, wu_ref[...])

        @pl.when(j == nf - 1)
        def _():
            dx_ref[...] = acc_ref[...]

    blk = 2 * (2 * tm * D * 4 + tm * D * 2 + 2 * tm * tf * 4 + 3 * tm * tf * 2 + 3 * D * tf * 2) + 3 * tm * D * 4 + 8 * tm * tf * 4
    once = dict(pipeline_mode=pl.Buffered(1))
    return _streamed_call(
        body, name=name, grid=(S // tm, nf),
        in_specs=[_bs((tm, D), lambda i, j: (i, 0), **once), _bs((tm, D), lambda i, j: (i, 0), **once),
                  _bs((tm, tf), lambda i, j: (i, j)), _bs((tm, tf), lambda i, j: (i, j)),
                  wg_spec, wu_spec, wd_spec, _bs((1, D), lambda i, j: (0, 0))],
        out_specs=[_bs((tm, D), lambda i, j: (i, 0)), _bs((tm, D), lambda i, j: (i, 0)),
                   _bs((tm, tf), lambda i, j: (i, j)), _bs((tm, tf), lambda i, j: (i, j)), _bs((tm, tf), lambda i, j: (i, j)),
                   _bs((1, D), lambda i, j: (0, 0)), _bs((1, D), lambda i, j: (0, 0))],
        out_shape=[jax.ShapeDtypeStruct((S, D), F32), jax.ShapeDtypeStruct((S, D), BF16),
                   jax.ShapeDtypeStruct((S, F), BF16), jax.ShapeDtypeStruct((S, F), BF16), jax.ShapeDtypeStruct((S, F), BF16),
                   jax.ShapeDtypeStruct((1, D), F32), jax.ShapeDtypeStruct((1, D), F32)],
        scratch_shapes=[pltpu.VMEM((tm, D), F32)],
        compiler_params=_cparams(blk),
    )(dy, z, hg, hu, wg_arr, wu_arr, wd_arr, g)


def _mm_tn(a, b, name):
    S, M = a.shape
    N = b.shape[1]
    bm = _divisor_tile(M, 1408, LANES)
    bn = _divisor_tile(N, 1408, LANES)
    tk = _divisor_tile(S, 512, 16)
    nk = S // tk

    def body(a_ref, b_ref, o_ref, ob_ref):
        k = pl.program_id(2)

        @pl.when(k == 0)
        def _():
            o_ref[...] = jnp.zeros_like(o_ref)

        o_ref[...] += _dot_tn(a_ref[...].astype(BF16), b_ref[...].astype(BF16))

        @pl.when(k == nk - 1)
        def _():
            ob_ref[...] = o_ref[...].astype(BF16)

    blk = 2 * (tk * bm * a.dtype.itemsize + tk * bn * b.dtype.itemsize + bm * bn * 6) + tk * bm * 4 + bm * bn * 4
    tile = _bs((bm, bn), lambda i, j, k: (i, j))
    return _streamed_call(
        body, name=name, grid=(M // bm, N // bn, nk),
        in_specs=[_bs((tk, bm), lambda i, j, k: (k, i)), _bs((tk, bn), lambda i, j, k: (k, j))],
        out_specs=[tile, tile],
        out_shape=[jax.ShapeDtypeStruct((M, N), F32), jax.ShapeDtypeStruct((M, N), BF16)],
        compiler_params=_cparams(blk),
    )(a, b)


def _mm_tn_pair(a, b0, b1, name):
    S, M = a.shape
    N = b0.shape[1]
    assert b1.shape == b0.shape
    bm = _divisor_tile(M, 1408, LANES)
    bn = _divisor_tile(N, 1408, LANES)
    tk = _divisor_tile(S, 512, 16)
    nb = N // bn
    nk = S // tk

    def body(a_ref, b0_ref, b1_ref, o_ref, ob_ref):
        j = pl.program_id(1)
        k = pl.program_id(2)

        @pl.when(k == 0)
        def _():
            o_ref[...] = jnp.zeros_like(o_ref)

        ab = a_ref[...].astype(BF16)

        @pl.when(j < nb)
        def _():
            o_ref[...] += _dot_tn(ab, b0_ref[...])

        @pl.when(j >= nb)
        def _():
            o_ref[...] += _dot_tn(ab, b1_ref[...])

        @pl.when(k == nk - 1)
        def _():
            ob_ref[...] = o_ref[...].astype(BF16)

    b0_map = lambda i, j, k: (jnp.where(j < nb, k, nk - 1), jnp.minimum(j, nb - 1))
    b1_map = lambda i, j, k: (jnp.where(j >= nb, k, 0), jnp.maximum(j - nb, 0))
    blk = 2 * (tk * bm * a.dtype.itemsize + 2 * tk * bn * 2 + bm * bn * 6) + tk * bm * 4 + bm * bn * 4
    tile = _bs((bm, bn), lambda i, j, k: (i, j))
    return _streamed_call(
        body, name=name, grid=(M // bm, 2 * nb, nk),
        in_specs=[_bs((tk, bm), lambda i, j, k: (k, i)), _bs((tk, bn), b0_map), _bs((tk, bn), b1_map)],
        out_specs=[tile, tile],
        out_shape=[jax.ShapeDtypeStruct((M, 2 * N), F32), jax.ShapeDtypeStruct((M, 2 * N), BF16)],
        compiler_params=_cparams(blk),
    )(a, b0, b1)


def _mix_in(x, wt, bias, sections, name):
    S, D = x.shape
    tm = _divisor_tile(S, 256, 16)
    n = len(sections)

    def body(x_ref, w_ref, b_ref, *o_refs):
        xb = x_ref[...].astype(BF16)
        for (off, width, dtype), o_ref in zip(sections, o_refs):
            o_ref[...] = (_dot_nt(xb, w_ref[off:off + width, :]) + b_ref[:, off:off + width]).astype(dtype)

    total = wt.shape[0]
    blk = 2 * (tm * D * 4 + sum(tm * w * jnp.dtype(dt).itemsize for _, w, dt in sections)) + total * D * 2 + 3 * tm * D * 4
    return _streamed_call(
        body, name=name, grid=(S // tm,),
        in_specs=[_bs((tm, D), lambda i: (i, 0)), _resident((total, D)), _resident((1, total))],
        out_specs=[_bs((tm, w), lambda i: (i, 0)) for _, w, _ in sections],
        out_shape=[jax.ShapeDtypeStruct((S, w), dt) for _, w, dt in sections],
        compiler_params=_cparams(blk),
    )(x, wt, bias)


def _mix_dx(dz, parts, wt, sections, alpha, name):
    S, D = dz.shape
    tm = _divisor_tile(S, 256, 16)
    n = len(parts)

    def body(*refs):
        dz_ref = refs[0]
        p_refs = refs[1:1 + n]
        w_ref = refs[1 + n]
        o_ref = refs[2 + n]
        acc = alpha * dz_ref[...]
        for p_ref, (off, width, _) in zip(p_refs, sections):
            acc = acc + _dot(p_ref[...], w_ref[off:off + width, :])
        o_ref[...] = acc

    widths = [p.shape[1] for p in parts]
    total = wt.shape[0]
    blk = 2 * (2 * tm * D * 4 + sum(tm * w * 2 for w in widths)) + total * D * 2 + 2 * tm * D * 4
    return _streamed_call(
        body, name=name, grid=(S // tm,),
        in_specs=[_bs((tm, D), lambda i: (i, 0))] + [_bs((tm, w), lambda i: (i, 0)) for w in widths]
                 + [_resident((total, D))],
        out_specs=_bs((tm, D), lambda i: (i, 0)),
        out_shape=jax.ShapeDtypeStruct((S, D), F32),
        compiler_params=_cparams(blk),
    )(dz, *parts, wt)


def _conv_branch_fwd(cv, cg, w, b, gg, gb, name):
    S, C = cv.shape
    K = w.shape[0]
    assert C // CONV_GROUPS == LANES
    padf = _front_pad(K)
    R = min(CONV_ROWS, S)
    E = min(EW_ROWS, S)

    def body(cv_ref, cg_ref, w_ref, b_ref, gg_ref, gb_ref, c2_ref, c4_ref, pad_ref):
        pad_ref[0:padf, :] = jnp.zeros((padf, LANES), F32)
        pad_ref[S + padf:S + padf + SUBLANES, :] = jnp.zeros((SUBLANES, LANES), F32)

        def fill(i, carry):
            r = pl.multiple_of(i * E, E)
            pad_ref[pl.ds(r + padf, E), :] = cv_ref[pl.ds(r, E), :] * _sigmoid(cg_ref[pl.ds(r, E), :])
            return carry

        lax.fori_loop(0, S // E, fill, 0)
        taps = [(padf - (K - 1) + k, functools.partial(lambda k: w_ref[k:k + 1, :], k)) for k in range(K)]

        def conv(i, carry):
            r = pl.multiple_of(i * R, R)
            c2_ref[pl.ds(r, R), :] = _shifted_taps(pad_ref, r, R, taps) + b_ref[...]
            return carry

        lax.fori_loop(0, S // R, conv, 0)

        def norm(i, carry):
            r = pl.multiple_of(i * E, E)
            c3 = _norm_fwd(c2_ref[pl.ds(r, E), :], gg_ref[...], gb_ref[...], _lane_mean)
            c4_ref[pl.ds(r, E), :] = (c3 * _sigmoid(c3)).astype(BF16)
            return carry

        lax.fori_loop(0, S // E, norm, 0)

    col = lambda i: (0, i)
    blk = 2 * (3 * S * LANES * 4 + S * LANES * 2) + (S + _pad_rows(K)) * LANES * 4
    return _streamed_call(
        body, name=name, grid=(C // LANES,),
        in_specs=[_bs((S, LANES), col), _bs((S, LANES), col), _bs((K, LANES), col),
                  _bs((1, LANES), col), _bs((1, LANES), col), _bs((1, LANES), col)],
        out_specs=[_bs((S, LANES), col), _bs((S, LANES), col)],
        out_shape=[jax.ShapeDtypeStruct((S, C), F32), jax.ShapeDtypeStruct((S, C), BF16)],
        scratch_shapes=[pltpu.VMEM((S + _pad_rows(K), LANES), F32)],
        compiler_params=_cparams(blk),
    )(cv, cg, w, b, gg, gb)


def _conv_branch_bwd(dc4, c2, cv, cg, w, gg, gb, name):
    S, C = cv.shape
    K = w.shape[0]
    padf = _front_pad(K)
    R = min(CONV_ROWS, S)
    E = min(EW_ROWS, S)

    def body(dc4_ref, c2_ref, cv_ref, cg_ref, w_ref, gg_ref, gb_ref,
             dcv_ref, dcg_ref, dw_ref, dwb_ref, dgg_ref, dgb_ref, scv_ref, scg_ref,
             dpad_ref, cpad_ref, dwacc_ref):
        cpad_ref[0:padf, :] = jnp.zeros((padf, LANES), F32)
        cpad_ref[S + padf:S + padf + SUBLANES, :] = jnp.zeros((SUBLANES, LANES), F32)
        dpad_ref[S:S + padf + SUBLANES, :] = jnp.zeros((padf + SUBLANES, LANES), F32)
        dwacc_ref[...] = jnp.zeros_like(dwacc_ref)
        for ref in (dwb_ref, dgg_ref, dgb_ref, scv_ref, scg_ref):
            ref[...] = jnp.zeros_like(ref)

        def norm_pass(i, carry):
            r = pl.multiple_of(i * E, E)
            g_ = gg_ref[...]
            c2 = c2_ref[pl.ds(r, E), :]
            xc = c2 - _lane_mean(c2)
            rstd = lax.rsqrt(_lane_mean(xc * xc) + LN_EPS)
            xhat = xc * rstd
            c3 = xhat * g_ + gb_ref[...]
            s = _sigmoid(c3)
            dc3 = dc4_ref[pl.ds(r, E), :].astype(F32) * (s * (1.0 + c3 * (1.0 - s)))
            dgg_ref[...] += _colsum(dc3 * xhat)
            dgb_ref[...] += _colsum(dc3)
            dxh = dc3 * g_
            dc2 = rstd * (dxh - _lane_mean(dxh) - xhat * _lane_mean(dxh * xhat))
            dpad_ref[pl.ds(r, E), :] = dc2
            dwb_ref[...] += _colsum(dc2)
            cpad_ref[pl.ds(r + padf, E), :] = cv_ref[pl.ds(r, E), :] * _sigmoid(cg_ref[pl.ds(r, E), :])
            return carry

        lax.fori_loop(0, S // E, norm_pass, 0)
        taps = [(K - 1 - k, functools.partial(lambda k: w_ref[k:k + 1, :], k)) for k in range(K)]
        offs = [padf - (K - 1) + k for k in range(K)]

        def conv_pass(i, carry):
            r = pl.multiple_of(i * R, R)
            dc1 = _shifted_taps(dpad_ref, r, R, taps)
            sg = _sigmoid(cg_ref[pl.ds(r, R), :])
            cv_ = cv_ref[pl.ds(r, R), :]
            dcv = dc1 * sg
            dcg = dc1 * cv_ * sg * (1.0 - sg)
            dcv_ref[pl.ds(r, R), :] = dcv.astype(BF16)
            dcg_ref[pl.ds(r, R), :] = dcg.astype(BF16)
            scv_ref[...] += _colsum(dcv)
            scg_ref[...] += _colsum(dcg)
            _shifted_corr(cpad_ref, r, R, dpad_ref[pl.ds(r, R), :], dwacc_ref, offs)
            return carry

        lax.fori_loop(0, S // R, conv_pass, 0)
        for k in range(K):
            dw_ref[k:k + 1, :] = _colsum(dwacc_ref[SUBLANES * k:SUBLANES * (k + 1), :])

    col = lambda i: (0, i)
    row = jax.ShapeDtypeStruct((1, C), F32)
    blk = 2 * (4 * S * LANES * 4 + 2 * S * LANES * 2) + 2 * (S + _pad_rows(K)) * LANES * 4
    return _streamed_call(
        body, name=name, grid=(C // LANES,),
        in_specs=[_bs((S, LANES), col)] * 4 + [_bs((K, LANES), col), _bs((1, LANES), col), _bs((1, LANES), col)],
        out_specs=[_bs((S, LANES), col), _bs((S, LANES), col), _bs((K, LANES), col)] + [_bs((1, LANES), col)] * 5,
        out_shape=[jax.ShapeDtypeStruct((S, C), BF16), jax.ShapeDtypeStruct((S, C), BF16),
                   jax.ShapeDtypeStruct((K, C), F32), row, row, row, row, row],
        scratch_shapes=[pltpu.VMEM((S + _pad_rows(K), LANES), F32), pltpu.VMEM((S + _pad_rows(K), LANES), F32),
                        pltpu.VMEM((SUBLANES * K, LANES), F32)],
        compiler_params=_cparams(blk),
    )(dc4, c2, cv, cg, w, gg, gb)


def _short_conv_fwd(xin, w, b, name):
    S, C = xin.shape
    K = w.shape[0]
    padf = _front_pad(K)
    R = min(CONV_ROWS, S)
    E = min(EW_ROWS, S)

    def body(x_ref, w_ref, b_ref, o_ref, pad_ref):
        pad_ref[0:padf, :] = jnp.zeros((padf, LANES), F32)
        pad_ref[S + padf:S + padf + SUBLANES, :] = jnp.zeros((SUBLANES, LANES), F32)

        def fill(i, carry):
            r = pl.multiple_of(i * E, E)
            pad_ref[pl.ds(r + padf, E), :] = x_ref[pl.ds(r, E), :]
            return carry

        lax.fori_loop(0, S // E, fill, 0)
        taps = [(padf - (K - 1) + k, functools.partial(lambda k: w_ref[k:k + 1, :], k)) for k in range(K)]

        def conv(i, carry):
            r = pl.multiple_of(i * R, R)
            o_ref[pl.ds(r, R), :] = _shifted_taps(pad_ref, r, R, taps) + b_ref[...]
            return carry

        lax.fori_loop(0, S // R, conv, 0)

    col = lambda i: (0, i)
    blk = 2 * (2 * S * LANES * 4) + (S + _pad_rows(K)) * LANES * 4
    return _streamed_call(
        body, name=name, grid=(C // LANES,),
        in_specs=[_bs((S, LANES), col), _bs((K, LANES), col), _bs((1, LANES), col)],
        out_specs=_bs((S, LANES), col),
        out_shape=jax.ShapeDtypeStruct((S, C), F32),
        scratch_shapes=[pltpu.VMEM((S + _pad_rows(K), LANES), F32)],
        compiler_params=_cparams(blk),
    )(xin, w, b)


def _short_conv_bwd(dy, xin, w, name):
    S, C = xin.shape
    K = w.shape[0]
    padf = _front_pad(K)
    R = min(CONV_ROWS, S)
    E = min(EW_ROWS, S)

    def body(dy_ref, x_ref, w_ref, dx_ref, dw_ref, db_ref, sx_ref, dpad_ref, xpad_ref, dwacc_ref):
        xpad_ref[0:padf, :] = jnp.zeros((padf, LANES), F32)
        xpad_ref[S + padf:S + padf + SUBLANES, :] = jnp.zeros((SUBLANES, LANES), F32)
        dpad_ref[S:S + padf + SUBLANES, :] = jnp.zeros((padf + SUBLANES, LANES), F32)
        dwacc_ref[...] = jnp.zeros_like(dwacc_ref)
        db_ref[...] = jnp.zeros_like(db_ref)
        sx_ref[...] = jnp.zeros_like(sx_ref)

        def fill(i, carry):
            r = pl.multiple_of(i * E, E)
            d = dy_ref[pl.ds(r, E), :]
            dpad_ref[pl.ds(r, E), :] = d
            db_ref[...] += _colsum(d)
            xpad_ref[pl.ds(r + padf, E), :] = x_ref[pl.ds(r, E), :]
            return carry

        lax.fori_loop(0, S // E, fill, 0)
        taps = [(K - 1 - k, functools.partial(lambda k: w_ref[k:k + 1, :], k)) for k in range(K)]
        offs = [padf - (K - 1) + k for k in range(K)]

        def conv_pass(i, carry):
            r = pl.multiple_of(i * R, R)
            dx = _shifted_taps(dpad_ref, r, R, taps)
            dx_ref[pl.ds(r, R), :] = dx.astype(BF16)
            sx_ref[...] += _colsum(dx)
            _shifted_corr(xpad_ref, r, R, dpad_ref[pl.ds(r, R), :], dwacc_ref, offs)
            return carry

        lax.fori_loop(0, S // R, conv_pass, 0)
        for k in range(K):
            dw_ref[k:k + 1, :] = _colsum(dwacc_ref[SUBLANES * k:SUBLANES * (k + 1), :])

    col = lambda i: (0, i)
    row = jax.ShapeDtypeStruct((1, C), F32)
    blk = 2 * (2 * S * LANES * 4 + S * LANES * 2) + 2 * (S + _pad_rows(K)) * LANES * 4
    return _streamed_call(
        body, name=name, grid=(C // LANES,),
        in_specs=[_bs((S, LANES), col), _bs((S, LANES), col), _bs((K, LANES), col)],
        out_specs=[_bs((S, LANES), col), _bs((K, LANES), col), _bs((1, LANES), col), _bs((1, LANES), col)],
        out_shape=[jax.ShapeDtypeStruct((S, C), BF16), jax.ShapeDtypeStruct((K, C), F32), row, row],
        scratch_shapes=[pltpu.VMEM((S + _pad_rows(K), LANES), F32), pltpu.VMEM((S + _pad_rows(K), LANES), F32),
                        pltpu.VMEM((SUBLANES * K, LANES), F32)],
        compiler_params=_cparams(blk),
    )(dy, xin, w)


def _band_panels(width, block):
    assert width % LANES == 0 and block <= LANES
    panels = []
    for c0 in range(0, width, 2 * LANES):
        c1 = min(width, c0 + 2 * LANES)
        r0 = (c0 // block) * block // LANES * LANES
        r1 = min(width, -(-(-(-c1 // block) * block) // LANES) * LANES)
        panels.append((r0, r1, c0, c1))
    return panels


def _gates_fwd(r1, wa, wx, ba, bx, lam, name):
    S, R = r1.shape
    tm = _divisor_tile(S, 256, 16)
    panels = _band_panels(R, R // RNN_BLOCKS)

    def body(r1_ref, wa_ref, wx_ref, ba_ref, bx_ref, lam_ref, ra_ref, ri_ref, a_ref, uu_ref):
        for r0, r1e, c0, c1 in panels:
            rb = r1_ref[:, r0:r1e].astype(BF16)
            ra = _sigmoid(_dot(rb, wa_ref[r0:r1e, c0:c1]) + ba_ref[:, c0:c1])
            ri = _sigmoid(_dot(rb, wx_ref[r0:r1e, c0:c1]) + bx_ref[:, c0:c1])
            log_a = -RG_LRU_C * ra * _softplus(-lam_ref[:, c0:c1])
            ra_ref[:, c0:c1] = ra
            ri_ref[:, c0:c1] = ri
            a_ref[:, c0:c1] = jnp.exp(log_a)
            uu_ref[:, c0:c1] = jnp.sqrt(_neg_expm1(2.0 * log_a)) * (ri * r1_ref[:, c0:c1])

    blk = 2 * (5 * tm * R * 4) + 2 * R * R * 2 + 6 * tm * R * 4
    tile = _bs((tm, R), lambda i: (i, 0))
    return _streamed_call(
        body, name=name, grid=(S // tm,),
        in_specs=[tile, _resident((R, R)), _resident((R, R)), _resident((1, R)), _resident((1, R)), _resident((1, R))],
        out_specs=[tile] * 4,
        out_shape=[jax.ShapeDtypeStruct((S, R), F32)] * 4,
        compiler_params=_cparams(blk),
    )(r1, wa, wx, ba, bx, lam)


def _gates_bwd(guu, da, ra, ri, r1, wa, wx, lam, name):
    S, R = r1.shape
    tm = _divisor_tile(S, 256, 16)
    nsteps = S // tm
    panels = _band_panels(R, R // RNN_BLOCKS)

    def body(g_ref, da_ref, ra_ref, ri_ref, r1_ref, wa_ref, wx_ref, lam_ref,
             dr1_ref, dpa_ref, dpx_ref, dba_ref, dbx_ref, dlam_ref):
        i = pl.program_id(0)

        @pl.when(i == 0)
        def _():
            dba_ref[...] = jnp.zeros_like(dba_ref)
            dbx_ref[...] = jnp.zeros_like(dbx_ref)
            dlam_ref[...] = jnp.zeros_like(dlam_ref)

        g = g_ref[...]
        ra = ra_ref[...]
        ri = ri_ref[...]
        r1_ = r1_ref[...]
        sp = _softplus(-lam_ref[...])
        log_a = -RG_LRU_C * ra * sp
        a = jnp.exp(log_a)
        mult = jnp.sqrt(_neg_expm1(2.0 * log_a))
        d_ri = g * mult * r1_
        dr1 = g * mult * ri
        dmult = g * ri * r1_
        dlog_a = da_ref[...] * a - dmult * (a * a) / mult
        dra = dlog_a * (-RG_LRU_C * sp)
        dlam_ref[...] += _colsum(dlog_a * (-RG_LRU_C * ra))
        dpa = dra * ra * (1.0 - ra)
        dpx = d_ri * ri * (1.0 - ri)
        dba_ref[...] += _colsum(dpa)
        dbx_ref[...] += _colsum(dpx)
        dpa_b = dpa.astype(BF16)
        dpx_b = dpx.astype(BF16)
        dpa_ref[...] = dpa_b
        dpx_ref[...] = dpx_b
        dr1_ref[...] = dr1
        for k0, k1, c0, c1 in panels:
            dr1_ref[:, c0:c1] += (_dot_nt(dpa_ref[:, k0:k1], wa_ref[c0:c1, k0:k1])
                                  + _dot_nt(dpx_ref[:, k0:k1], wx_ref[c0:c1, k0:k1]))

        @pl.when(i == nsteps - 1)
        def _():
            dlam_ref[...] = dlam_ref[...] * (-_sigmoid(-lam_ref[...]))

    blk = 2 * (6 * tm * R * 4 + 2 * tm * R * 2) + 2 * R * R * 2 + 10 * tm * R * 4
    tile = _bs((tm, R), lambda i: (i, 0))
    rowspec = _bs((1, R), lambda i: (0, 0))
    row = jax.ShapeDtypeStruct((1, R), F32)
    return _streamed_call(
        body, name=name, grid=(nsteps,),
        in_specs=[tile] * 5 + [_resident((R, R)), _resident((R, R)), _resident((1, R))],
        out_specs=[tile, tile, tile, rowspec, rowspec, rowspec],
        out_shape=[jax.ShapeDtypeStruct((S, R), F32), jax.ShapeDtypeStruct((S, R), BF16), jax.ShapeDtypeStruct((S, R), BF16),
                   row, row, row],
        compiler_params=_cparams(blk),
    )(guu, da, ra, ri, r1, wa, wx, lam)


def _embed_blocks(w, name):
    H, bk, _ = w.shape

    def body(w_ref, o_ref):
        o_ref[...] = jnp.zeros_like(o_ref)
        for h in range(H):
            o_ref[bk * h:bk * (h + 1), bk * h:bk * (h + 1)] = w_ref[h].astype(BF16)

    return pl.pallas_call(body, name=name, out_shape=jax.ShapeDtypeStruct((H * bk, H * bk), BF16),
                          compiler_params=_cparams(3 * H * bk * H * bk * 2))(w)


def _block_grads(r1, dpa, dpx, name):
    S, R = r1.shape
    bk = R // RNN_BLOCKS
    tk = _divisor_tile(S, 512, 16)
    nsteps = S // tk
    panels = _band_panels(R, bk)

    def body(r1_ref, dpa_ref, dpx_ref, ga_ref, gx_ref, acca_ref, accx_ref):
        k = pl.program_id(0)

        @pl.when(k == 0)
        def _():
            acca_ref[...] = jnp.zeros_like(acca_ref)
            accx_ref[...] = jnp.zeros_like(accx_ref)

        for k0, k1, c0, c1 in panels:
            rb = r1_ref[:, k0:k1].astype(BF16)
            acca_ref[k0:k1, c0:c1] += _dot_tn(rb, dpa_ref[:, c0:c1])
            accx_ref[k0:k1, c0:c1] += _dot_tn(rb, dpx_ref[:, c0:c1])

        @pl.when(k == nsteps - 1)
        def _():
            for h in range(RNN_BLOCKS):
                ga_ref[h] = acca_ref[bk * h:bk * (h + 1), bk * h:bk * (h + 1)]
                gx_ref[h] = accx_ref[bk * h:bk * (h + 1), bk * h:bk * (h + 1)]

    tile = lambda: _bs((tk, R), lambda k: (k, 0))
    out = _bs((RNN_BLOCKS, bk, bk), lambda k: (0, 0, 0))
    sds = jax.ShapeDtypeStruct((RNN_BLOCKS, bk, bk), F32)
    return _streamed_call(
        body, name=name, grid=(nsteps,),
        in_specs=[tile(), tile(), tile()], out_specs=[out, out], out_shape=[sds, sds],
        scratch_shapes=[pltpu.VMEM((R, R), F32), pltpu.VMEM((R, R), F32)],
        compiler_params=_cparams(2 * (tk * R * 8) + 2 * R * R * 4 + 4 * tk * R * 4),
    )(r1, dpa, dpx)


def _scan_geometry(S):
    nseg = SCAN_SEGMENTS if S % (SCAN_SEGMENTS * SUBLANES) == 0 else SUBLANES
    return nseg, S // nseg


def _steps(n, step, init):
    u = SCAN_UNROLL

    def trip(t, carry):
        for k in range(u):
            carry = step(t * u + k, carry)
        return carry

    carry = lax.fori_loop(0, n // u, trip, init)
    for j in range(n - n % u, n):
        carry = step(j, carry)
    return carry


def _scan_fwd(a, u, name):
    S, C = a.shape
    nseg, L = _scan_geometry(S)
    T = min(SUBLANES, L)

    def body(a3, u3, h3, ta_ref, tu_ref, e_ref, p_ref, init_ref):

        def to_steps(i, carry):
            j0 = pl.multiple_of(i * T, T)
            ta_ref[pl.ds(j0, T)] = jnp.swapaxes(a3[:, pl.ds(j0, T), :], 0, 1)
            tu_ref[pl.ds(j0, T)] = jnp.swapaxes(u3[:, pl.ds(j0, T), :], 0, 1)
            return carry

        lax.fori_loop(0, L // T, to_steps, 0)

        def run1(j, carry):
            hs, ps = carry
            aj = ta_ref[j]
            return aj * hs + tu_ref[j], aj * ps

        e_ref[...], p_ref[...] = _steps(L, run1, (jnp.zeros((nseg, LANES), F32), jnp.ones((nseg, LANES), F32)))
        init_ref[0:1, :] = jnp.zeros((1, LANES), F32)
        for s in range(1, nseg):
            init_ref[s:s + 1, :] = e_ref[s - 1:s, :] + p_ref[s - 1:s, :] * init_ref[s - 1:s, :]

        def run2(j, hs):
            hs = ta_ref[j] * hs + tu_ref[j]
            tu_ref[j] = hs
            return hs

        _steps(L, run2, init_ref[...])

        def from_steps(i, carry):
            j0 = pl.multiple_of(i * T, T)
            h3[:, pl.ds(j0, T), :] = jnp.swapaxes(tu_ref[pl.ds(j0, T)], 0, 1)
            return carry

        lax.fori_loop(0, L // T, from_steps, 0)

    seg_block = _bs((nseg, L, LANES), lambda i: (0, 0, i))
    blk = 2 * (3 * S * LANES * 4) + 2 * S * LANES * 4
    return _streamed_call(
        body, name=name, grid=(C // LANES,),
        in_specs=[seg_block, seg_block],
        out_specs=seg_block,
        out_shape=jax.ShapeDtypeStruct((nseg, L, C), F32),
        scratch_shapes=[pltpu.VMEM((L, nseg, LANES), F32)] * 2 + [pltpu.VMEM((nseg, LANES), F32)] * 3,
        compiler_params=_cparams(blk),
    )(a.reshape(nseg, L, C), u.reshape(nseg, L, C)).reshape(S, C)


def _scan_bwd(a, dh, h, name):
    S, C = a.shape
    nseg, L = _scan_geometry(S)
    T = min(SUBLANES, L)
    assert L >= 2

    def body(a3, d3, h3, g3, da3, ta_ref, td_ref, th_ref, e_ref, p_ref, init_ref):

        def to_steps(i, carry):
            j0 = pl.multiple_of(i * T, T)
            for src, dst in ((a3, ta_ref), (d3, td_ref), (h3, th_ref)):
                dst[pl.ds(j0, T)] = jnp.swapaxes(src[:, pl.ds(j0, T), :], 0, 1)
            return carry

        lax.fori_loop(0, L // T, to_steps, 0)
        seg = lax.broadcasted_iota(jnp.int32, (nseg, LANES), 0)
        b_last = jnp.where(seg == nseg - 1, 0.0, pltpu.roll(ta_ref[0], nseg - 1, axis=0))
        h_first = jnp.where(seg == 0, 0.0, pltpu.roll(th_ref[L - 1], 1, axis=0))

        def run1(jj, carry):
            gs, ps = carry
            j = L - 2 - jj
            bj = ta_ref[j + 1]
            return bj * gs + td_ref[j], bj * ps

        e_ref[...], p_ref[...] = _steps(L - 1, run1, (td_ref[L - 1], b_last))
        init_ref[nseg - 1:nseg, :] = jnp.zeros((1, LANES), F32)
        for s in range(nseg - 2, -1, -1):
            init_ref[s:s + 1, :] = e_ref[s + 1:s + 2, :] + p_ref[s + 1:s + 2, :] * init_ref[s + 1:s + 2, :]

        gs = b_last * init_ref[...] + td_ref[L - 1]
        td_ref[L - 1] = gs
        th_ref[L - 1] = gs * th_ref[L - 2]

        def run2(jj, gs):
            j = L - 2 - jj
            gs = ta_ref[j + 1] * gs + td_ref[j]
            td_ref[j] = gs
            th_ref[j] = gs * th_ref[j - 1]
            return gs

        gs = _steps(L - 2, run2, gs)
        gs = ta_ref[1] * gs + td_ref[0]
        td_ref[0] = gs
        th_ref[0] = gs * h_first

        def from_steps(i, carry):
            j0 = pl.multiple_of(i * T, T)
            g3[:, pl.ds(j0, T), :] = jnp.swapaxes(td_ref[pl.ds(j0, T)], 0, 1)
            da3[:, pl.ds(j0, T), :] = jnp.swapaxes(th_ref[pl.ds(j0, T)], 0, 1)
            return carry

        lax.fori_loop(0, L // T, from_steps, 0)

    seg_block = _bs((nseg, L, LANES), lambda i: (0, 0, i))
    blk = 2 * (5 * S * LANES * 4) + 3 * S * LANES * 4
    g, da = _streamed_call(
        body, name=name, grid=(C // LANES,),
        in_specs=[seg_block] * 3,
        out_specs=[seg_block] * 2,
        out_shape=[jax.ShapeDtypeStruct((nseg, L, C), F32)] * 2,
        scratch_shapes=[pltpu.VMEM((L, nseg, LANES), F32)] * 3 + [pltpu.VMEM((nseg, LANES), F32)] * 3,
        compiler_params=_cparams(blk),
    )(a.reshape(nseg, L, C), dh.reshape(nseg, L, C), h.reshape(nseg, L, C))
    return g.reshape(S, C), da.reshape(S, C)


def _mixer_out_fwd(c4, h, rg, gc, gr, x1, wc, wr, wo, g, b, alpha, name):
    S, D = x1.shape
    R = h.shape[1]
    tm = _divisor_tile(S, 256, 16)

    def body(c4_ref, h_ref, rg_ref, gc_ref, gr_ref, x_ref, wc_ref, wr_ref, wo_ref, g_ref, b_ref,
             yc_ref, yr_ref, z_ref, y_ref):
        yc = _dot(c4_ref[...], wc_ref[...])
        q = (h_ref[...] * _gelu(rg_ref[...].astype(F32))).astype(BF16)
        yr = _dot(q, wr_ref[...])
        yc_ref[...] = yc.astype(BF16)
        yr_ref[...] = yr.astype(BF16)
        m = (_sigmoid(gc_ref[...].astype(F32)) * yc + _sigmoid(gr_ref[...].astype(F32)) * yr).astype(BF16)
        z = alpha * x_ref[...] + _dot(m, wo_ref[...])
        z_ref[...] = z
        y_ref[...] = _norm_fwd(z, g_ref[...], b_ref[...])

    blk = 2 * (tm * D * 2 + 2 * tm * R * 4 + 7 * tm * D * 4) + (2 * D * D + R * D) * 2 + 6 * tm * D * 4
    td = _bs((tm, D), lambda i: (i, 0))
    tr = _bs((tm, R), lambda i: (i, 0))
    return _streamed_call(
        body, name=name, grid=(S // tm,),
        in_specs=[td, tr, tr, td, td, td, _layer_resident(wc)[1], _layer_resident(wr)[1], _layer_resident(wo)[1],
                  _resident((1, D)), _resident((1, D))],
        out_specs=[td] * 4,
        out_shape=[jax.ShapeDtypeStruct((S, D), BF16)] * 2 + [jax.ShapeDtypeStruct((S, D), F32)] * 2,
        compiler_params=_cparams(blk),
    )(c4, h, rg, gc, gr, x1, wc[0], wr[0], wo[0], g, b)


def _mixer_out_bwd(dy, z, g, wo, yc, yr, gc, gr, name):
    S, D = dy.shape
    tm = _divisor_tile(S, 256, 16)

    def body(dy_ref, z_ref, g_ref, wo_ref, yc_ref, yr_ref, gc_ref, gr_ref,
             dz_ref, dzb_ref, m_ref, dyc_ref, dyr_ref, dgc_ref, dgr_ref, sgc_ref, sgr_ref, dg_ref, db_ref):
        @pl.when(pl.program_id(0) == 0)
        def _():
            for ref in (sgc_ref, sgr_ref, dg_ref, db_ref):
                ref[...] = jnp.zeros_like(ref)

        dy_ = dy_ref[...]
        dz, xhat = _norm_bwd(z_ref[...], g_ref[...], dy_)
        dg_ref[...] += _colsum(dy_ * xhat)
        db_ref[...] += _colsum(dy_)
        dz_ref[...] = dz
        dzb = dz.astype(BF16)
        dzb_ref[...] = dzb
        dm = _dot_nt(dzb, wo_ref[...])
        yc = yc_ref[...].astype(F32)
        yr = yr_ref[...].astype(F32)
        sc = _sigmoid(gc_ref[...].astype(F32))
        sr = _sigmoid(gr_ref[...].astype(F32))
        m_ref[...] = (sc * yc + sr * yr).astype(BF16)
        dyc_ref[...] = (dm * sc).astype(BF16)
        dyr_ref[...] = (dm * sr).astype(BF16)
        dgc = dm * yc * sc * (1.0 - sc)
        dgr = dm * yr * sr * (1.0 - sr)
        dgc_ref[...] = dgc.astype(BF16)
        dgr_ref[...] = dgr.astype(BF16)
        sgc_ref[...] += _colsum(dgc)
        sgr_ref[...] += _colsum(dgr)

    blk = 2 * (7 * tm * D * 4 + 6 * tm * D * 2) + D * D * 2 + 8 * tm * D * 4
    td = _bs((tm, D), lambda i: (i, 0))
    rowspec = _bs((1, D), lambda i: (0, 0))
    row = jax.ShapeDtypeStruct((1, D), F32)
    bfd = jax.ShapeDtypeStruct((S, D), BF16)
    return _streamed_call(
        body, name=name, grid=(S // tm,),
        in_specs=[td, td, _resident((1, D)), _layer_resident(wo)[1], td, td, td, td],
        out_specs=[td] * 7 + [rowspec] * 4,
        out_shape=[jax.ShapeDtypeStruct((S, D), F32), bfd, bfd, bfd, bfd, bfd, bfd, row, row, row, row],
        compiler_params=_cparams(blk),
    )(dy, z, g, wo[0], yc, yr, gc, gr)


def _branch_bwd(dyc, dyr, wc, wr, h, rg, name):
    S, D = dyc.shape
    R = h.shape[1]
    tm = _divisor_tile(S, 256, 16)

    def body(dyc_ref, dyr_ref, wc_ref, wr_ref, h_ref, rg_ref, dc4_ref, dh_ref, drg_ref, q_ref, srg_ref):
        @pl.when(pl.program_id(0) == 0)
        def _():
            srg_ref[...] = jnp.zeros_like(srg_ref)

        dc4_ref[...] = _dot_nt(dyc_ref[...], wc_ref[...]).astype(BF16)
        dq = _dot_nt(dyr_ref[...], wr_ref[...])
        h_ = h_ref[...]
        rg_ = rg_ref[...].astype(F32)
        ge = _gelu(rg_)
        dh_ref[...] = dq * ge
        drg = dq * h_ * _gelu_grad(rg_)
        drg_ref[...] = drg.astype(BF16)
        srg_ref[...] += _colsum(drg)
        q_ref[...] = (h_ * ge).astype(BF16)

    blk = 2 * (2 * tm * D * 2 + tm * D * 4 + 3 * tm * R * 4 + 2 * tm * R * 2) + (D * D + R * D) * 2 + 6 * tm * R * 4
    td = _bs((tm, D), lambda i: (i, 0))
    tr = _bs((tm, R), lambda i: (i, 0))
    return _streamed_call(
        body, name=name, grid=(S // tm,),
        in_specs=[td, td, _layer_resident(wc)[1], _layer_resident(wr)[1], tr, tr],
        out_specs=[td, tr, tr, tr, _bs((1, R), lambda i: (0, 0))],
        out_shape=[jax.ShapeDtypeStruct((S, D), BF16), jax.ShapeDtypeStruct((S, R), F32), jax.ShapeDtypeStruct((S, R), BF16),
                   jax.ShapeDtypeStruct((S, R), BF16), jax.ShapeDtypeStruct((1, R), F32)],
        compiler_params=_cparams(blk),
    )(dyc, dyr, wc[0], wr[0], h, rg)


def _loss_head(y, target, name):
    S, D = y.shape
    tm = _divisor_tile(S, 512, 16)
    nsteps = S // tm

    def body(y_ref, t_ref, loss_ref, dy_ref, acc_ref):
        i = pl.program_id(0)

        @pl.when(i == 0)
        def _():
            acc_ref[...] = jnp.zeros_like(acc_ref)

        err = y_ref[...] - t_ref[...]
        dy_ref[...] = err * (1.0 / D)
        acc_ref[...] += _colsum(err * err)

        @pl.when(i == nsteps - 1)
        def _():
            loss_ref[...] = jnp.sum(acc_ref[...], axis=-1, keepdims=True) * (0.5 / D)

    td = _bs((tm, D), lambda i: (i, 0))
    return _streamed_call(
        body, name=name, grid=(nsteps,),
        in_specs=[td, td],
        out_specs=[_bs((1, 1), lambda i: (0, 0)), td],
        out_shape=[jax.ShapeDtypeStruct((1, 1), F32), jax.ShapeDtypeStruct((S, D), F32)],
        scratch_shapes=[pltpu.VMEM((1, D), F32)],
        compiler_params=_cparams(2 * 3 * tm * D * 4),
    )(y, target)


def _adamw_math(w, g, m, v):
    m = ADAM_B1 * m + (1.0 - ADAM_B1) * g
    v = ADAM_B2 * v + (1.0 - ADAM_B2) * (g * g)
    m_hat = m / (1.0 - ADAM_B1 ** ADAM_STEP)
    v_hat = v / (1.0 - ADAM_B2 ** ADAM_STEP)
    delta = -ADAM_LR * (m_hat / (jnp.sqrt(v_hat) + ADAM_EPS) + ADAM_WD * w)
    return delta, m, v


def _adamw_sharded(w, m, v, own, sib, rem, layer, filled, name):
    layers, r, c = w.shape
    r2 = r // 2
    tr = _divisor_tile(r2, max(16, (1 << 20) // (4 * c) // 16 * 16), 16)
    n_out = 4

    def body(w_ref, m_ref, v_ref, own_ref, sib_ref, rem_ref, *rest):
        g_ref, d_ref, nm_ref, nv_ref = rest[-n_out:]
        mine = pl.program_id(0) == lax.axis_index("c")
        g = jnp.where(mine, own_ref[...], sib_ref[...]).astype(F32)
        for j in range(N_CHIPS - 1):
            g = g + rem_ref[j].astype(F32)
        delta, nm, nv = _adamw_math(w_ref[...], g, m_ref[...], v_ref[...])
        g_ref[...] = g
        d_ref[...] = delta
        nm_ref[...] = nm
        nv_ref[...] = nv

    halves = lambda a: a.reshape(layers, 2, r2, c)
    tile = _bs((None, None, tr, c), lambda h, i: (layer, h, i, 0))
    flat = _bs((tr, c), lambda h, i: (i, 0))
    sds = jax.ShapeDtypeStruct((layers, 2, r2, c), F32)
    passed = [] if filled is None else [halves(a) for a in filled]
    outs = _streamed_call(
        body, name=name, grid=(2, r2 // tr),
        in_specs=[tile, tile, tile, flat, flat, _bs((N_CHIPS - 1, None, tr, c), lambda h, i: (0, h, i, 0))] + [ANY] * len(passed),
        out_specs=[tile] * n_out,
        out_shape=[sds] * n_out,
        input_output_aliases={6 + k: k for k in range(len(passed))},
        compiler_params=_cparams(2 * (7 * tr * c * 4 + (N_CHIPS + 1) * tr * c * 2) + 6 * tr * c * 4),
    )(halves(w), halves(m), halves(v), own, sib, rem, *passed)
    return [o.reshape(layers, r, c) for o in outs]


def _adamw_small(ws, ms, vs, gs, name):
    n = len(ws)

    def body(*refs):
        w_refs, m_refs, v_refs, g_refs = (refs[k * n:(k + 1) * n] for k in range(4))
        d_refs, nm_refs, nv_refs = (refs[(4 + k) * n:(5 + k) * n] for k in range(3))
        for i in range(n):
            delta, nm, nv = _adamw_math(w_refs[i][...], g_refs[i][...], m_refs[i][...], v_refs[i][...])
            d_refs[i][...] = delta
            nm_refs[i][...] = nm
            nv_refs[i][...] = nv

    sds = [jax.ShapeDtypeStruct(w.shape, F32) for w in ws]
    total = sum(_nbytes(w.shape, F32) for w in ws)
    outs = pl.pallas_call(body, name=name, out_shape=sds * 3, compiler_params=_cparams(16 * total))(*ws, *ms, *vs, *gs)
    return outs[:n], outs[n:2 * n], outs[2 * n:]


def _half_view(g, by_cols):
    rows, cols = g.shape
    if by_cols:
        return g.reshape(2, rows // 2, cols)
    return g.reshape(N_CHIPS, 2, rows // (2 * N_CHIPS), cols)


def _pair_sum_bf16(view, theirs, by_cols, name):
    rows, c = theirs.shape[-2:]
    tr = _divisor_tile(rows, max(16, (1 << 20) // (4 * c) // 16 * 16), 16)

    def body(g0_ref, g1_ref, t_ref, o_ref):
        mine = jnp.where(lax.axis_index("c") == 0, g0_ref[...], g1_ref[...])
        o_ref[...] = (mine + t_ref[...].astype(F32)).astype(BF16)

    if by_cols:
        grid = (rows // tr,)
        halves = [_bs((None, tr, c), functools.partial(lambda h, i: (h, i, 0), h)) for h in range(2)]
        tile = _bs((tr, c), lambda i: (i, 0))
    else:
        grid = (N_CHIPS, rows // tr)
        halves = [_bs((None, None, tr, c), functools.partial(lambda h, k, i: (k, h, i, 0), h)) for h in range(2)]
        tile = _bs((None, tr, c), lambda k, i: (k, i, 0))
    return _streamed_call(
        body, name=name, grid=grid,
        in_specs=halves + [tile], out_specs=tile, out_shape=jax.ShapeDtypeStruct(theirs.shape, BF16),
        compiler_params=_cparams(2 * 4 * tr * c * 4),
    )(view, view, theirs)


ANY = pl.BlockSpec(memory_space=pl.ANY)


def _mesh_position():
    return lax.axis_index("x"), lax.axis_index("y"), lax.axis_index("c")


def _other_chips():
    x, y, c = _mesh_position()
    chips = [(1 - x, y), (x, 1 - y), (1 - x, 1 - y)]
    return 2 * x + y, (x, y, 1 - c), chips, [2 * cx + cy for cx, cy in chips]


def _chip_slab(ref, k, width, by_cols):
    if by_cols:
        start = k * width if isinstance(k, int) else pl.multiple_of(k * width, LANES)
        return ref.at[:, pl.ds(start, width)]
    return ref.at[k]


HBM = pl.BlockSpec(memory_space=pltpu.HBM)
SEM = pl.BlockSpec(memory_space=pltpu.SEMAPHORE)
DATAFLOW = pltpu.SideEffectType.DATAFLOW_SIDE_EFFECTING
N_GATHER_COPIES = 4


def _land_shape(src, by_cols):
    return src.shape[:-1] + (N_CHIPS * src.shape[-1],) if by_cols else (N_CHIPS,) + src.shape


def _gather_copy(src_ref, land_ref, by_cols, send_sems, recv_sems, pos, j, slab, to):
    width = src_ref.shape[-1]
    return pltpu.make_async_remote_copy(src_ref=src_ref, dst_ref=_chip_slab(land_ref, slab, width, by_cols),
                                        send_sem=send_sems.at[N_GATHER_COPIES * pos + j],
                                        recv_sem=recv_sems.at[N_GATHER_COPIES * pos + j],
                                        device_id=to, device_id_type=MESH)


def _gather_start(srcs, by_cols, groups, name):
    U = len(srcs)
    G = len(groups)
    lands = [lax.empty(_land_shape(s, bc), s.dtype) for s, bc in zip(srcs, by_cols)]

    def body(*refs):
        src = refs[:U]
        land = refs[U:2 * U]
        send_sems = refs[2 * U:2 * U + G]
        recv_sems = refs[2 * U + G:2 * U + 2 * G]
        token = refs[-1]
        c = lax.axis_index("c")
        me, sibling, chips, _ = _other_chips()
        targets = [(*chip, c) for chip in chips] + [sibling]
        for g, members in enumerate(groups):
            for pos, u in enumerate(members):
                for j, to in enumerate(targets):
                    _gather_copy(src[u], land[u], by_cols[u], send_sems[g], recv_sems[g], pos, j, me, to).start()
        token[...] = jnp.zeros_like(token)

    sem_shapes = [pltpu.SemaphoreType.DMA((len(m) * N_GATHER_COPIES,)) for m in groups]
    outs = pl.pallas_call(
        body, name=name,
        out_shape=tuple(sem_shapes + sem_shapes + [pltpu.HBM(s.shape, s.dtype) for s in srcs]
                        + [pltpu.HBM(v.shape, v.dtype) for v in lands] + [jax.ShapeDtypeStruct((SUBLANES, LANES), F32)]),
        in_specs=[HBM] * (2 * U),
        out_specs=tuple([SEM] * (2 * G) + [HBM] * (2 * U) + [pl.BlockSpec(memory_space=pltpu.VMEM)]),
        input_output_aliases={i: 2 * G + i for i in range(2 * U)},
        compiler_params=pltpu.CompilerParams(has_side_effects=DATAFLOW),
    )(*[pltpu.with_memory_space_constraint(a, pltpu.HBM) for a in list(srcs) + lands])
    return outs[:G], outs[G:2 * G], outs[2 * G:2 * G + U], outs[2 * G + U:2 * G + 2 * U]


def _gather_wait(srcs, lands, by_cols, send_sems, recv_sems, after, name):
    n = len(srcs)

    def body(*refs):
        src = refs[:n]
        land = refs[n:2 * n]
        send_ref, recv_ref = refs[2 * n:2 * n + 2]
        _, sibling, _, _ = _other_chips()
        for pos in range(n):
            for j in range(N_GATHER_COPIES):
                cp = _gather_copy(src[pos], land[pos], by_cols[pos], send_ref, recv_ref, pos, j, 0, sibling)
                cp.wait_send()
                cp.wait_recv()

    outs = pl.pallas_call(
        body, name=name,
        out_shape=tuple([pltpu.HBM(s.shape, s.dtype) for s in srcs] + [pltpu.HBM(v.shape, v.dtype) for v in lands]),
        in_specs=[HBM] * (2 * n) + [SEM, SEM, pl.BlockSpec(memory_space=pl.ANY)],
        out_specs=tuple([HBM] * (2 * n)),
        input_output_aliases={i: i for i in range(2 * n)},
        compiler_params=pltpu.CompilerParams(has_side_effects=DATAFLOW),
    )(*srcs, *lands, send_sems, recv_sems, after)
    return outs[n:]


def _scatter_grads(csums, by_cols, name):
    n = len(csums)
    shard = [(s.shape[0], s.shape[1] // N_CHIPS) if bc else s.shape[1:] for s, bc in zip(csums, by_cols)]

    def body(*refs):
        src = refs[:n]
        rem = refs[n:2 * n]
        sib = refs[2 * n:3 * n]
        send_sems, recv_sems = refs[3 * n:]
        c = lax.axis_index("c")
        me, sibling, chips, chip_ids = _other_chips()

        def remote(i, k, src_ref, dst_ref, to):
            return pltpu.make_async_remote_copy(src_ref=src_ref, dst_ref=dst_ref, send_sem=send_sems.at[i, k],
                                                recv_sem=recv_sems.at[i, k], device_id=to, device_id_type=MESH)

        def part(i, k):
            return _chip_slab(src[i], k, shard[i][-1], by_cols[i])

        started = []
        for i in range(n):
            for j in range(3):
                started.append(remote(i, j, part(i, chip_ids[j]), rem[i].at[j, c], (*chips[j], c)))
            started.append(remote(i, 6, part(i, me), sib[i], sibling))
        for cp in started:
            cp.start()
        for i in range(n):
            for j in range(3):
                slot = rem[i].at[j, c]
                remote(i, j, slot, slot, sibling).wait_recv()
                fwd = remote(i, 3 + j, slot, slot, sibling)
                fwd.start()
                started.append(fwd)
        for i in range(n):
            for j in range(3):
                slot = rem[i].at[j, 1 - c]
                remote(i, 3 + j, slot, slot, sibling).wait_recv()
            remote(i, 6, sib[i], sib[i], sibling).wait_recv()
        for cp in started:
            cp.wait_send()

    out_shape = ([jax.ShapeDtypeStruct((N_CHIPS - 1, 2) + tuple(sh), s.dtype) for s, sh in zip(csums, shard)]
                 + [jax.ShapeDtypeStruct(tuple(sh), s.dtype) for s, sh in zip(csums, shard)])
    outs = _streamed_call(
        body, name=name,
        in_specs=[ANY] * n, out_specs=[ANY] * (2 * n), out_shape=out_shape,
        scratch_shapes=[pltpu.SemaphoreType.DMA((n, 7)), pltpu.SemaphoreType.DMA((n, 7))],
    )(*csums)
    return outs[:n], outs[n:]


def _sibling_exchange(views, by_cols, name):
    n = len(views)

    def body(*refs):
        src = refs[:n]
        theirs = refs[n:2 * n]
        send_sems, recv_sems = refs[2 * n:]
        x, y, c = _mesh_position()
        copies = []
        for i in range(n):
            half = src[i].at[1 - c] if by_cols[i] else src[i].at[:, 1 - c]
            copies.append(pltpu.make_async_remote_copy(src_ref=half, dst_ref=theirs[i], send_sem=send_sems.at[i],
                                                       recv_sem=recv_sems.at[i], device_id=(x, y, 1 - c), device_id_type=MESH))
        for cp in copies:
            cp.start()
        for cp in copies:
            cp.wait()

    out_shape = [jax.ShapeDtypeStruct(v.shape[1:] if bc else v.shape[:1] + v.shape[2:], v.dtype) for v, bc in zip(views, by_cols)]
    return _streamed_call(
        body, name=name,
        in_specs=[ANY] * n, out_specs=[ANY] * n, out_shape=out_shape,
        scratch_shapes=[pltpu.SemaphoreType.DMA((n,)), pltpu.SemaphoreType.DMA((n,))],
    )(*views)


N_SCATTER_COPIES = 7


def _scatter_start(csums, by_cols, name):
    n = len(csums)
    shard = [(s.shape[0], s.shape[1] // N_CHIPS) if bc else s.shape[1:] for s, bc in zip(csums, by_cols)]
    rems = [lax.empty((N_CHIPS - 1, 2) + tuple(sh), s.dtype) for s, sh in zip(csums, shard)]
    sibs = [lax.empty(tuple(sh), s.dtype) for s, sh in zip(csums, shard)]

    def body(*refs):
        src = refs[:n]
        rem = refs[n:2 * n]
        sib = refs[2 * n:3 * n]
        send_sems, recv_sems = refs[3 * n:3 * n + 2]
        token = refs[-1]
        c = lax.axis_index("c")
        me, sibling, chips, chip_ids = _other_chips()
        for i in range(n):
            base = N_SCATTER_COPIES * i
            for j in range(3):
                part = _chip_slab(src[i], chip_ids[j], shard[i][-1], by_cols[i])
                for core in range(2):
                    pltpu.make_async_remote_copy(src_ref=part, dst_ref=rem[i].at[j, c], send_sem=send_sems.at[base + 2 * j + core],
                                                 recv_sem=recv_sems.at[base + 2 * j + c], device_id=(*chips[j], core),
                                                 device_id_type=MESH).start()
            pltpu.make_async_remote_copy(src_ref=_chip_slab(src[i], me, shard[i][-1], by_cols[i]), dst_ref=sib[i],
                                         send_sem=send_sems.at[base + 6], recv_sem=recv_sems.at[base + 6], device_id=sibling,
                                         device_id_type=MESH).start()
        token[...] = jnp.zeros_like(token)

    sems = pltpu.SemaphoreType.DMA((N_SCATTER_COPIES * n,))
    operands = list(csums) + rems + sibs
    outs = pl.pallas_call(
        body, name=name,
        out_shape=tuple([sems, sems] + [pltpu.HBM(a.shape, a.dtype) for a in operands] + [jax.ShapeDtypeStruct((SUBLANES, LANES), F32)]),
        in_specs=[HBM] * (3 * n),
        out_specs=tuple([SEM, SEM] + [HBM] * (3 * n) + [pl.BlockSpec(memory_space=pltpu.VMEM)]),
        input_output_aliases={i: 2 + i for i in range(3 * n)},
        compiler_params=pltpu.CompilerParams(has_side_effects=DATAFLOW),
    )(*[pltpu.with_memory_space_constraint(a, pltpu.HBM) for a in operands])
    return (outs[0], outs[1], outs[2:2 + n], outs[2 + n:2 + 2 * n], outs[2 + 2 * n:2 + 3 * n]), outs[-1]


def _scatter_wait(send_sems, recv_sems, srcs, rems, sibs, by_cols, after, name):
    n = len(srcs)

    def body(*refs):
        src = refs[:n]
        rem = refs[n:2 * n]
        sib = refs[2 * n:3 * n]
        send_ref, recv_ref = refs[3 * n:3 * n + 2]
        _, sibling, _, _ = _other_chips()
        for i in range(n):
            base = N_SCATTER_COPIES * i
            width = sib[i].shape[-1]
            for j in range(3):
                for core in range(2):
                    cp = pltpu.make_async_remote_copy(src_ref=_chip_slab(src[i], 0, width, by_cols[i]), dst_ref=rem[i].at[j, core],
                                                      send_sem=send_ref.at[base + 2 * j + core],
                                                      recv_sem=recv_ref.at[base + 2 * j + core], device_id=sibling,
                                                      device_id_type=MESH)
                    cp.wait_send()
                    cp.wait_recv()
            cp = pltpu.make_async_remote_copy(src_ref=_chip_slab(src[i], 0, width, by_cols[i]), dst_ref=sib[i],
                                              send_sem=send_ref.at[base + 6], recv_sem=recv_ref.at[base + 6], device_id=sibling,
                                              device_id_type=MESH)
            cp.wait_send()
            cp.wait_recv()

    operands = list(srcs) + list(rems) + list(sibs)
    outs = pl.pallas_call(
        body, name=name,
        out_shape=tuple(pltpu.HBM(a.shape, a.dtype) for a in operands),
        in_specs=[HBM] * (3 * n) + [SEM, SEM, pl.BlockSpec(memory_space=pl.ANY)],
        out_specs=tuple([HBM] * (3 * n)),
        input_output_aliases={i: i for i in range(3 * n)},
        compiler_params=pltpu.CompilerParams(has_side_effects=DATAFLOW),
    )(*operands, send_sems, recv_sems, after)
    return outs[:n], outs[n:2 * n], outs[2 * n:3 * n]


def _all_reduce_small(v, name):
    _, rows, _ = v.shape

    def body(v_ref, o_ref, recv_ref, send_sems, recv_sems):
        x, y, c = _mesh_position()
        me = 4 * x + 2 * y + c
        peers = []
        for d in range(1, N_DEV):
            px, py, pc = x ^ ((d >> 2) & 1), y ^ ((d >> 1) & 1), c ^ (d & 1)
            peers.append(((px, py, pc), 4 * px + 2 * py + pc))

        def remote(k, src_ref, dst_ref, to):
            return pltpu.make_async_remote_copy(src_ref=src_ref, dst_ref=dst_ref, send_sem=send_sems.at[k],
                                                recv_sem=recv_sems.at[k], device_id=to, device_id_type=MESH)

        scatter = [remote(d, v_ref.at[pid], recv_ref.at[me], to) for d, (to, pid) in enumerate(peers)]
        for cp in scatter:
            cp.start()
        recv_ref[pl.ds(me, 1)] = v_ref[pl.ds(me, 1)]
        for d, (to, pid) in enumerate(peers):
            remote(d, v_ref.at[pid], recv_ref.at[pid], to).wait_recv()
        total = recv_ref[0]
        for s in range(1, N_DEV):
            total = total + recv_ref[s]
        o_ref[pl.ds(me, 1)] = total[None]
        gather = [remote(N_DEV - 1 + d, o_ref.at[me], o_ref.at[me], to) for d, (to, pid) in enumerate(peers)]
        for cp in gather:
            cp.start()
        for d, (to, pid) in enumerate(peers):
            remote(N_DEV - 1 + d, o_ref.at[pid], o_ref.at[pid], to).wait_recv()
        for cp in scatter + gather:
            cp.wait_send()

    vm = pl.BlockSpec(memory_space=pltpu.VMEM)
    return pl.pallas_call(
        body, name=name,
        in_specs=[vm], out_specs=vm, out_shape=jax.ShapeDtypeStruct(v.shape, F32),
        scratch_shapes=[pltpu.VMEM(v.shape, F32), pltpu.SemaphoreType.DMA((2 * (N_DEV - 1),)),
                        pltpu.SemaphoreType.DMA((2 * (N_DEV - 1),))],
        compiler_params=_cparams(4 * _nbytes(v.shape, F32)),
    )(v)


SHARDED_MATS = ("ffn1_w_gu", "ffn1_w_down", "mix_w_in", "conv_w_proj", "rnn_w_proj", "mix_w_out", "ffn2_w_gu", "ffn2_w_down")
COL_SHARDED = ("ffn1_w_gu", "ffn2_w_gu", "conv_dw_w")
SHARDED_VECS = ("conv_dw_w", "rnn_conv_w")
WEIGHT_NAMES = ("ffn1_w_gu", "ffn1_w_down", "ln1_g", "ln1_b", "mix_w_in", "mix_b_in", "conv_dw_w", "conv_dw_b", "conv_gn_g",
                "conv_gn_b", "conv_w_proj", "rnn_conv_w", "rnn_conv_b", "rnn_w_a", "rnn_b_a", "rnn_w_x", "rnn_b_x",
                "rnn_lambda", "rnn_w_proj", "mix_w_out", "ln2_g", "ln2_b", "ffn2_w_gu", "ffn2_w_down", "ln3_g", "ln3_b")
SMALL_NAMES = tuple(n for n in WEIGHT_NAMES if n not in SHARDED_MATS)
SECTION_NAMES = ("cv", "cg", "rx", "rg", "gc", "gr")


def _unshard_cols(gathered):
    k4, K, n = gathered.shape
    return jnp.transpose(gathered, (1, 0, 2)).reshape(K, k4 * n)


def _row(v):
    return v.reshape(1, -1)


def _layer_forward(x0, p, alpha, l, hooks):
    t = f"l{l}_"
    sv = {"x0": x0}
    x1, sv["z1"], sv["hg1"], sv["hu1"] = _ffn_fwd(x0, p["wgu1"], p["wd1"], p["ln1_g"], p["ln1_b"], alpha, t + "ffn1_fwd")
    sv["x1"] = x1
    hooks.get("after_ffn1", lambda v: None)(x1)
    sec = dict(zip(SECTION_NAMES, _mix_in(x1, p["win"], p["bin"], p["sections"], t + "mix_in")))
    sv.update(sec)
    sv["c2"], c4 = _conv_branch_fwd(sec["cv"], sec["cg"], p["conv_dw_w"], p["conv_dw_b"], p["conv_gn_g"], p["conv_gn_b"], t + "conv_fwd")
    sv["c4"] = c4
    r1 = _short_conv_fwd(sec["rx"], p["rnn_conv_w"], p["rnn_conv_b"], t + "rconv_fwd")
    sv["r1"] = r1
    sv["ra"], sv["ri"], a, uu = _gates_fwd(r1, p["wa"], p["wx"], p["rnn_b_a"], p["rnn_b_x"], p["rnn_lambda"], t + "gates_fwd")
    sv["a"] = a
    h = _scan_fwd(a, uu, t + "scan_fwd")
    sv["h"] = h
    hooks.get("after_scan", lambda v: None)(h)
    sv["yc"], sv["yr"], sv["z2"], x2 = _mixer_out_fwd(c4, h, sec["rg"], sec["gc"], sec["gr"], x1, p["wc"], p["wr"], p["wo"],
                                                      p["ln2_g"], p["ln2_b"], alpha, t + "mixout_fwd")
    sv["x2"] = x2
    hooks.get("after_mixer", lambda v: None)(x2)
    x3, sv["z3"], sv["hg2"], sv["hu2"] = _ffn_fwd(x2, p["wgu2"], p["wd2"], p["ln3_g"], p["ln3_b"], alpha, t + "ffn2_fwd")
    hooks.get("after_layer", lambda v: None)(x3)
    return x3, sv


def _layer_backward(dy, p, sv, alpha, l, before_ffn1=None):
    t = f"l{l}_"
    g, gb = {}, {}
    dx2, df, a_act, dhg, dhu, g["ln3_g"], g["ln3_b"] = _ffn_bwd(dy, sv["z3"], sv["hg2"], sv["hu2"], p["wgu2"], p["wd2"],
                                                                 p["ln3_g"], alpha, t + "ffn2_bwd")
    g["ffn2_w_down"], gb["ffn2_w_down"] = _mm_tn(a_act, df, t + "dwd2")
    g["ffn2_w_gu"], gb["ffn2_w_gu"] = _mm_tn_pair(sv["x2"], dhg, dhu, t + "dwgu2")
    (dz2, dz2b, m_b, dyc, dyr, dgc, dgr, s_gc, s_gr, g["ln2_g"], g["ln2_b"]) = _mixer_out_bwd(
        dx2, sv["z2"], p["ln2_g"], p["wo"], sv["yc"], sv["yr"], sv["gc"], sv["gr"], t + "mixout_bwd")
    g["mix_w_out"], gb["mix_w_out"] = _mm_tn(m_b, dz2b, t + "dwo")
    dc4, dh, drg, q_b, s_rg = _branch_bwd(dyc, dyr, p["wc"], p["wr"], sv["h"], sv["rg"], t + "branch_bwd")
    g["conv_w_proj"], gb["conv_w_proj"] = _mm_tn(sv["c4"], dyc, t + "dwc")
    g["rnn_w_proj"], gb["rnn_w_proj"] = _mm_tn(q_b, dyr, t + "dwr")
    (dcv, dcg, g["conv_dw_w"], g["conv_dw_b"], g["conv_gn_g"], g["conv_gn_b"], s_cv, s_cg) = _conv_branch_bwd(
        dc4, sv["c2"], sv["cv"], sv["cg"], p["conv_dw_w"], p["conv_gn_g"], p["conv_gn_b"], t + "conv_bwd")
    guu, da = _scan_bwd(sv["a"], dh, sv["h"], t + "scan_bwd")
    dr1, dpa, dpx, g["rnn_b_a"], g["rnn_b_x"], g["rnn_lambda"] = _gates_bwd(
        guu, da, sv["ra"], sv["ri"], sv["r1"], p["wa"], p["wx"], p["rnn_lambda"], t + "gates_bwd")
    g["rnn_w_a"], g["rnn_w_x"] = _block_grads(sv["r1"], dpa, dpx, t + "dwax")
    drx, g["rnn_conv_w"], g["rnn_conv_b"], s_rx = _short_conv_bwd(dr1, sv["rx"], p["rnn_conv_w"], t + "rconv_bwd")
    du = {"cv": dcv, "cg": dcg, "rx": drx, "rg": drg, "gc": dgc, "gr": dgr}
    order = ("cv", "cg", "rx", "rg", "gc", "gr")
    pieces = [_mm_tn(du[s], sv["x1"], t + "dwin_" + s) for s in order]
    g["mix_w_in"] = jnp.concatenate([f for f, _ in pieces], axis=0)
    gb["mix_w_in"] = jnp.concatenate([h for _, h in pieces], axis=0)
    g["mix_b_in"] = jnp.concatenate([s_cv, s_cg, s_rx, s_rg, s_gc, s_gr], axis=1)
    dx1 = _mix_dx(dz2, [du[s] for s in order], p["win"], p["sections"], alpha, t + "mix_dx")
    ln1_g = p["ln1_g"] if before_ffn1 is None else p["ln1_g"] + before_ffn1(g, gb)[0:1, 0:1]
    dx0, df, a_act, dhg, dhu, g["ln1_g"], g["ln1_b"] = _ffn_bwd(dx1, sv["z1"], sv["hg1"], sv["hu1"], p["wgu1"], p["wd1"],
                                                                 ln1_g, alpha, t + "ffn1_bwd")
    g["ffn1_w_down"], gb["ffn1_w_down"] = _mm_tn(a_act, df, t + "dwd1")
    g["ffn1_w_gu"], gb["ffn1_w_gu"] = _mm_tn_pair(sv["x0"], dhg, dhu, t + "dwgu1")
    return dx0, g, gb


def _pack_small(arrays, piece_rows):
    flat = jnp.concatenate([a.reshape(-1) for a in arrays])
    total = N_DEV * piece_rows * LANES
    return jnp.pad(flat, (0, total - flat.shape[0])).reshape(N_DEV, piece_rows, LANES)


def _unpack_small(packed, shapes):
    flat = packed.reshape(-1)
    out, off = [], 0
    for shp in shapes:
        n = 1
        for s in shp:
            n *= s
        out.append(flat[off:off + n].reshape(shp))
        off += n
    return out


def kernel(x, ffn1_w_gu, ffn1_w_down, ln1_g, ln1_b, mix_w_in, mix_b_in, conv_dw_w, conv_dw_b, conv_gn_g, conv_gn_b, conv_w_proj, rnn_conv_w, rnn_conv_b, rnn_w_a, rnn_b_a, rnn_w_x, rnn_b_x, rnn_lambda, rnn_w_proj, mix_w_out, ln2_g, ln2_b, ffn2_w_gu, ffn2_w_down, ln3_g, ln3_b, loss_target, m_ffn1_w_gu, m_ffn1_w_down, m_ln1_g, m_ln1_b, m_mix_w_in, m_mix_b_in, m_conv_dw_w, m_conv_dw_b, m_conv_gn_g, m_conv_gn_b, m_conv_w_proj, m_rnn_conv_w, m_rnn_conv_b, m_rnn_w_a, m_rnn_b_a, m_rnn_w_x, m_rnn_b_x, m_rnn_lambda, m_rnn_w_proj, m_mix_w_out, m_ln2_g, m_ln2_b, m_ffn2_w_gu, m_ffn2_w_down, m_ln3_g, m_ln3_b, v_ffn1_w_gu, v_ffn1_w_down, v_ln1_g, v_ln1_b, v_mix_w_in, v_mix_b_in, v_conv_dw_w, v_conv_dw_b, v_conv_gn_g, v_conv_gn_b, v_conv_w_proj, v_rnn_conv_w, v_rnn_conv_b, v_rnn_w_a, v_rnn_b_a, v_rnn_w_x, v_rnn_b_x, v_rnn_lambda, v_rnn_w_proj, v_mix_w_out, v_ln2_g, v_ln2_b, v_ffn2_w_gu, v_ffn2_w_down, v_ln3_g, v_ln3_b):
    args = locals()
    W = {n: args[n] for n in WEIGHT_NAMES}
    M = {n: args["m_" + n] for n in WEIGHT_NAMES}
    V = {n: args["v_" + n] for n in WEIGHT_NAMES}
    depth = ln1_g.shape[0]
    assert depth == 2, "each core of a chip moves one layer's weights and gradients"
    alpha = float((2 * depth) ** 0.25)
    S, D = x.shape[1], x.shape[2]
    F = ffn1_w_down.shape[1] * N_CHIPS
    R = rnn_w_proj.shape[1] * N_CHIPS
    chip = 2 * lax.axis_index("x") + lax.axis_index("y")

    for d in (W, M, V):
        d["mix_w_in"] = jnp.transpose(d["mix_w_in"], (0, 2, 1))

    names = SHARDED_MATS + SHARDED_VECS
    unit_groups = [[(0, "ffn1_w_gu"), (0, "ffn1_w_down")], [(0, "mix_w_in"), (0, "conv_dw_w"), (0, "rnn_conv_w")],
                   [(0, "conv_w_proj"), (0, "rnn_w_proj"), (0, "mix_w_out")], [(0, "ffn2_w_gu"), (0, "ffn2_w_down")],
                   [(1, n) for n in names]]
    order = [u for g in unit_groups for u in g]
    index = {u: i for i, u in enumerate(order)}
    groups = [[index[u] for u in g] for g in unit_groups]
    srcs = [W[n][l].astype(BF16) if n in SHARDED_MATS else W[n][l] for l, n in order]
    by_cols = [n in COL_SHARDED for _, n in order]
    send_sems, recv_sems, src_thru, land_thru = _gather_start(srcs, by_cols, groups, "gather_start")

    sections = ((0, D, F32), (D, D, F32), (2 * D, R, F32), (2 * D + R, R, BF16), (2 * D + 2 * R, D, BF16),
                (3 * D + 2 * R, D, BF16))
    keys = {"ffn1_w_gu": "wgu1", "ffn1_w_down": "wd1", "ffn2_w_gu": "wgu2", "ffn2_w_down": "wd2", "conv_w_proj": "wc",
            "rnn_w_proj": "wr", "mix_w_out": "wo"}
    params = []
    for l in range(depth):
        p = {"wa": _embed_blocks(rnn_w_a[l], f"l{l}_embed_wa"), "wx": _embed_blocks(rnn_w_x[l], f"l{l}_embed_wx")}
        for n in ("ln1_g", "ln1_b", "ln2_g", "ln2_b", "ln3_g", "ln3_b", "conv_dw_b", "conv_gn_g", "conv_gn_b", "rnn_conv_b",
                  "rnn_b_a", "rnn_b_x", "rnn_lambda"):
            p[n] = _row(W[n][l])
        p["bin"] = _row(mix_b_in[l])
        p["sections"] = sections
        params.append(p)

    def wait_group(g, after):
        ids = groups[g]
        landed = _gather_wait([src_thru[i] for i in ids], [land_thru[i] for i in ids], [by_cols[i] for i in ids],
                              send_sems[g], recv_sems[g], after, f"gather_wait{g}")
        for i, full in zip(ids, landed):
            l, n = order[i]
            p = params[l]
            if n not in COL_SHARDED:
                full = full.reshape((N_CHIPS * full.shape[1],) + full.shape[2:])
            if n == "mix_w_in":
                p["win"] = full
            elif n == "rnn_conv_w":
                p[n] = _unshard_cols(landed[ids.index(i)])
            elif n == "conv_dw_w":
                p[n] = full
            else:
                p[keys[n]] = (full[None], 0)

    h = x[0]
    wait_group(0, h)
    saved = []
    hooks = [{"after_ffn1": lambda v: wait_group(1, v), "after_scan": lambda v: wait_group(2, v),
              "after_mixer": lambda v: wait_group(3, v), "after_layer": lambda v: wait_group(4, v)}, {}]
    for l in range(depth):
        h, sv = _layer_forward(h, params[l], alpha, l, hooks[l])
        saved.append(sv)
    loss_part, dy = _loss_head(h, loss_target[0], "loss_head")
    loss = lax.psum(loss_part[0, 0], ("x", "y", "c"))
    def pair_sums(names, g, gb, tag):
        cols = [n in COL_SHARDED for n in names]
        theirs = _sibling_exchange([_half_view(gb[n], bc) for n, bc in zip(names, cols)], cols, "pair_exchange" + tag)
        return cols, [_pair_sum_bf16(_half_view(g[n], bc), t, bc, f"pair_sum{tag}_{n}") for n, bc, t in zip(names, cols, theirs)]

    pending = []

    def start_scatter(layer, names, g, gb, tag):
        cols, sums = pair_sums(names, g, gb, tag)
        in_flight, token = _scatter_start(sums, cols, "scatter_start" + tag)
        pending.append((layer, names, cols, in_flight, tag))
        return token

    early = [n for n in SHARDED_MATS if not n.startswith("ffn1_")]
    late = [n for n in SHARDED_MATS if n.startswith("ffn1_")]
    grads, grads_bf16 = [None] * depth, [None] * depth
    dy, grads[1], grads_bf16[1] = _layer_backward(dy, params[1], saved[1], alpha, 1)
    token = start_scatter(1, SHARDED_MATS, grads[1], grads_bf16[1], "1")
    first = dict(params[0], ln3_g=params[0]["ln3_g"] + token[0:1, 0:1])
    dy, grads[0], grads_bf16[0] = _layer_backward(dy, first, saved[0], alpha, 0,
                                                  before_ffn1=lambda g, gb: start_scatter(0, early, g, gb, "0a"))
    grad_x = dy[None]
    last_token = start_scatter(0, late, grads[0], grads_bf16[0], "0b")
    partial, results = {}, {}

    def collect(entry, after):
        layer, names, cols, in_flight, tag = entry
        for n, part in zip(names, zip(*_scatter_wait(*in_flight, cols, after, "scatter_wait" + tag))):
            partial[(layer, n)] = part

    def update(l, n):
        cs, rm, sb = partial[(l, n)]
        if n in COL_SHARDED:
            width = cs.shape[1] // N_CHIPS
            own = lax.dynamic_slice_in_dim(cs, chip * width, width, axis=1)
        else:
            own = lax.dynamic_index_in_dim(cs, chip, axis=0, keepdims=False)
        results[n] = _adamw_sharded(W[n], M[n], V[n], own, sb, rm, l, results.get(n), f"adamw{l}_{n}")

    for entry in pending[:-1]:
        collect(entry, last_token)
    for l in (1, 0):
        for n in SHARDED_MATS:
            if (l, n) in partial:
                update(l, n)
    collect(pending[-1], results[early[-1]][0])
    for n in late:
        update(0, n)
    out_g, out_d, out_m, out_v = {}, {}, {}, {}
    for n in SHARDED_MATS:
        outs = results[n]
        if n == "mix_w_in":
            outs = [jnp.transpose(o, (0, 2, 1)) for o in outs]
        out_g[n], out_d[n], out_m[n], out_v[n] = outs

    small_grads = [jnp.stack([grads[l][n].reshape(W[n].shape[1:] if n not in SHARDED_VECS else
                                                   (W[n].shape[1], W[n].shape[2] * N_CHIPS)) for l in range(depth)])
                   for n in SMALL_NAMES]
    n_small = sum(int(a.size) for a in small_grads)
    piece_rows = -(-n_small // (N_DEV * LANES * SUBLANES)) * SUBLANES
    reduced = _unpack_small(_all_reduce_small(_pack_small(small_grads, piece_rows), "all_reduce_small"),
                            [a.shape for a in small_grads])
    local_g = []
    for n, gr in zip(SMALL_NAMES, reduced):
        if n in SHARDED_VECS:
            width = W[n].shape[2]
            gr = lax.dynamic_slice_in_dim(gr, chip * width, width, axis=2)
        local_g.append(gr)
    deltas, new_m, new_v = _adamw_small([W[n] for n in SMALL_NAMES], [M[n] for n in SMALL_NAMES],
                                        [V[n] for n in SMALL_NAMES], local_g, "adamw_small")
    for n, gr, d_, m_, v_ in zip(SMALL_NAMES, local_g, deltas, new_m, new_v):
        out_g[n], out_d[n], out_m[n], out_v[n] = gr, d_, m_, v_

    return (loss, grad_x, *[out_g[n] for n in WEIGHT_NAMES], *[out_d[n] for n in WEIGHT_NAMES],
            *[out_m[n] for n in WEIGHT_NAMES], *[out_v[n] for n in WEIGHT_NAMES])
```

```python
import functools

import jax
import jax.numpy as jnp
from jax import lax
from jax.experimental import pallas as pl
from jax.experimental.pallas import tpu as pltpu

F32 = jnp.float32
BF16 = jnp.bfloat16
MESH = pl.DeviceIdType.MESH

LN_EPS = 1e-5
CONV_GROUPS = 8
RNN_BLOCKS = 16
RG_LRU_C = 8.0
ADAM_LR = 0.001
ADAM_B1 = 0.9
ADAM_B2 = 0.999
ADAM_EPS = 1e-08
ADAM_WD = 0.01
ADAM_STEP = 10

LANES = 128
SUBLANES = 8
V7X_VMEM_BYTES = 64 << 20
VMEM_LIMIT_CAP = V7X_VMEM_BYTES - (6 << 20)
N_CHIPS = 4
N_DEV = 8
CONV_ROWS = 64
EW_ROWS = 1024
SCAN_SEGMENTS = 32
SCAN_UNROLL = 4


def _cparams(block_bytes):
    limit = min(VMEM_LIMIT_CAP, max(int(block_bytes) + (8 << 20), 24 << 20))
    return pltpu.CompilerParams(vmem_limit_bytes=limit)


def _nbytes(shape, dtype):
    n = 1
    for s in shape:
        n *= s
    return n * jnp.dtype(dtype).itemsize


def _divisor_tile(n, limit, quantum):
    if n <= limit:
        return n
    best = None
    for t in range(quantum, limit + 1, quantum):
        if n % t == 0:
            best = t
    assert best is not None, (n, limit, quantum)
    return best


def _bs(shape, imap, **kw):
    return pl.BlockSpec(shape, imap, **kw)


def _resident(shape):
    nd = len(shape)
    return pl.BlockSpec(shape, lambda *_: (0,) * nd, pipeline_mode=pl.Buffered(1))


def _streamed_call(body, **kw):
    call = pl.pallas_call(body, **kw)
    return lambda *operands: call(*[pltpu.with_memory_space_constraint(o, pltpu.HBM) for o in operands])


def _layer_block(w, block, imap, **kw):
    arr, layer = w
    return arr, pl.BlockSpec((None,) + block, lambda *ids: (layer,) + imap(*ids), **kw)


def _layer_resident(w):
    arr, _ = w
    return _layer_block(w, arr.shape[1:], lambda *_: (0, 0), pipeline_mode=pl.Buffered(1))


def _sigmoid(x):
    return jax.nn.sigmoid(x)


def _dot(a, b):
    return jnp.dot(a, b, preferred_element_type=F32)


def _dot_nt(a, b):
    return lax.dot_general(a, b, (((1,), (1,)), ((), ())), preferred_element_type=F32)


def _dot_tn(a, b):
    return lax.dot_general(a, b, (((0,), (0,)), ((), ())), preferred_element_type=F32)


def _row_mean(z):
    return jnp.mean(z, axis=-1, keepdims=True)


def _lane_mean(z):
    hi = z.astype(BF16)
    lo = (z - hi.astype(F32)).astype(BF16)
    ones = jnp.full((2 * LANES, LANES), 1.0 / LANES, BF16)
    return jnp.dot(jnp.concatenate([hi, lo], axis=-1), ones, preferred_element_type=F32)


def _norm_fwd(z, g, b, mean=_row_mean):
    mu = mean(z)
    xc = z - mu
    var = mean(xc * xc)
    return xc * lax.rsqrt(var + LN_EPS) * g + b


def _norm_bwd(z, g, dy, mean=_row_mean):
    mu = mean(z)
    xc = z - mu
    var = mean(xc * xc)
    rstd = lax.rsqrt(var + LN_EPS)
    xhat = xc * rstd
    dxh = dy * g
    m1 = mean(dxh)
    m2 = mean(dxh * xhat)
    return rstd * (dxh - m1 - xhat * m2), xhat


GELU_K = 0.7978845608028654
GELU_C = 0.044715


def _gelu(x):
    return 0.5 * x * (1.0 + jnp.tanh(GELU_K * (x + GELU_C * x * x * x)))


def _gelu_grad(x):
    t = jnp.tanh(GELU_K * (x + GELU_C * x * x * x))
    return 0.5 * (1.0 + t) + 0.5 * x * (1.0 - t * t) * GELU_K * (1.0 + 3.0 * GELU_C * x * x)


def _softplus(y):
    return jnp.maximum(y, 0.0) + jnp.log1p(jnp.exp(-jnp.abs(y)))


def _neg_expm1(y):
    series = -y * (1.0 + y * (0.5 + y * (1.0 / 6.0 + y * (1.0 / 24.0 + y * (1.0 / 120.0 + y * (1.0 / 720.0))))))
    return jnp.where(y > -0.25, series, 1.0 - jnp.exp(y))


def _colsum(x):
    return jnp.sum(x, axis=0, keepdims=True)


def _shifted_taps(src_ref, base, rows, taps):
    acc = None
    for o, coef in taps:
        term = coef() * src_ref[pl.ds(base + o, rows), :]
        acc = term if acc is None else acc + term
    return acc


def _shifted_corr(src_ref, base, rows, d, acc_ref, offs):
    for k, o in enumerate(offs):
        prod = d * src_ref[pl.ds(base + o, rows), :]
        part = jnp.sum(prod.reshape(rows // SUBLANES, SUBLANES, prod.shape[-1]), axis=0)
        acc_ref[SUBLANES * k:SUBLANES * (k + 1), :] += part


def _front_pad(ktaps):
    return SUBLANES * ((ktaps - 1 + SUBLANES - 1) // SUBLANES)


def _pad_rows(ktaps):
    return _front_pad(ktaps) + SUBLANES


def _ffn_tiles(S, F):
    tm = _divisor_tile(S, 1024, 16)
    tf = _divisor_tile(F, 256, LANES)
    return tm, tf


def _ffn_fwd(x, wgu, wd, g, b, alpha, name):
    S, D = x.shape
    F = wd[0].shape[1]
    tm, tf = _ffn_tiles(S, F)
    nf = F // tf
    wg_arr, wg_spec = _layer_block(wgu, (D, tf), lambda i, j: (0, j))
    wu_arr, wu_spec = _layer_block(wgu, (D, tf), lambda i, j: (0, nf + j))
    wd_arr, wd_spec = _layer_block(wd, (tf, D), lambda i, j: (j, 0))

    def body(x_ref, wg_ref, wu_ref, wd_ref, g_ref, b_ref, y_ref, z_ref, hg_ref, hu_ref, acc_ref, xb_ref):
        j = pl.program_id(1)

        @pl.when(j == 0)
        def _():
            xb_ref[...] = x_ref[...].astype(BF16)
            acc_ref[...] = jnp.zeros_like(acc_ref)

        xb = xb_ref[...]
        hg = _dot(xb, wg_ref[...])
        hu = _dot(xb, wu_ref[...])
        hg_ref[...] = hg
        hu_ref[...] = hu
        a = (hg * _sigmoid(hg) * hu).astype(BF16)
        acc_ref[...] += _dot(a, wd_ref[...])

        @pl.when(j == nf - 1)
        def _():
            z = alpha * x_ref[...] + 0.5 * acc_ref[...]
            z_ref[...] = z
            y_ref[...] = _norm_fwd(z, g_ref[...], b_ref[...])

    blk = 2 * (3 * tm * D * 4 + 2 * tm * tf * 4 + 3 * D * tf * 2) + tm * D * 6 + 3 * tm * tf * 4
    return _streamed_call(
        body, name=name, grid=(S // tm, nf),
        in_specs=[_bs((tm, D), lambda i, j: (i, 0)), wg_spec, wu_spec, wd_spec,
                  _bs((1, D), lambda i, j: (0, 0)), _bs((1, D), lambda i, j: (0, 0))],
        out_specs=[_bs((tm, D), lambda i, j: (i, 0)), _bs((tm, D), lambda i, j: (i, 0)),
                   _bs((tm, tf), lambda i, j: (i, j)), _bs((tm, tf), lambda i, j: (i, j))],
        out_shape=[jax.ShapeDtypeStruct((S, D), F32), jax.ShapeDtypeStruct((S, D), F32),
                   jax.ShapeDtypeStruct((S, F), F32), jax.ShapeDtypeStruct((S, F), F32)],
        scratch_shapes=[pltpu.VMEM((tm, D), F32), pltpu.VMEM((tm, D), BF16)],
        compiler_params=_cparams(blk),
    )(x, wg_arr, wu_arr, wd_arr, g, b)


def _ffn_bwd(dy, z, hg, hu, wgu, wd, g, alpha, name):
    S, D = dy.shape
    F = wd[0].shape[1]
    tm, tf = _ffn_tiles(S, F)
    nf = F // tf
    wg_arr, wg_spec = _layer_block(wgu, (D, tf), lambda i, j: (0, j))
    wu_arr, wu_spec = _layer_block(wgu, (D, tf), lambda i, j: (0, nf + j))
    wd_arr, wd_spec = _layer_block(wd, (tf, D), lambda i, j: (j, 0))

    def body(dy_ref, z_ref, hg_ref, hu_ref, wg_ref, wu_ref, wd_ref, g_ref,
             dx_ref, df_ref, a_ref, dhg_ref, dhu_ref, dg_ref, db_ref, acc_ref):
        i = pl.program_id(0)
        j = pl.program_id(1)

        @pl.when((i == 0) & (j == 0))
        def _():
            dg_ref[...] = jnp.zeros_like(dg_ref)
            db_ref[...] = jnp.zeros_like(db_ref)

        @pl.when(j == 0)
        def _():
            dy_ = dy_ref[...]
            dz, xhat = _norm_bwd(z_ref[...], g_ref[...], dy_)
            dg_ref[...] += _colsum(dy_ * xhat)
            db_ref[...] += _colsum(dy_)
            acc_ref[...] = alpha * dz
            df_ref[...] = (0.5 * dz).astype(BF16)

        da = _dot_nt(df_ref[...], wd_ref[...])
        hg_ = hg_ref[...]
        hu_ = hu_ref[...]
        s = _sigmoid(hg_)
        sl = hg_ * s
        dgate = (da * hu_ * (s * (1.0 + hg_ * (1.0 - s)))).astype(BF16)
        dup = (da * sl).astype(BF16)
        a_ref[...] = (sl * hu_).astype(BF16)
        dhg_ref[...] = dgate
        dhu_ref[...] = dup
        acc_ref[...] += _dot_nt(dgate, wg_ref[...]) + _dot_nt(dup, wu_ref[...])

        @pl.when(j == nf - 1)
        def _():
            dx_ref[...] = acc_ref[...]

    blk = 2 * (2 * tm * D * 4 + tm * D * 2 + 2 * tm * tf * 4 + 3 * tm * tf * 2 + 3 * D * tf * 2) + 3 * tm * D * 4 + 8 * tm * tf * 4
    once = dict(pipeline_mode=pl.Buffered(1))
    return _streamed_call(
        body, name=name, grid=(S // tm, nf),
        in_specs=[_bs((tm, D), lambda i, j: (i, 0), **once), _bs((tm, D), lambda i, j: (i, 0), **once),
                  _bs((tm, tf), lambda i, j: (i, j)), _bs((tm, tf), lambda i, j: (i, j)),
                  wg_spec, wu_spec, wd_spec, _bs((1, D), lambda i, j: (0, 0))],
        out_specs=[_bs((tm, D), lambda i, j: (i, 0)), _bs((tm, D), lambda i, j: (i, 0)),
                   _bs((tm, tf), lambda i, j: (i, j)), _bs((tm, tf), lambda i, j: (i, j)), _bs((tm, tf), lambda i, j: (i, j)),
                   _bs((1, D), lambda i, j: (0, 0)), _bs((1, D), lambda i, j: (0, 0))],
        out_shape=[jax.ShapeDtypeStruct((S, D), F32), jax.ShapeDtypeStruct((S, D), BF16),
                   jax.ShapeDtypeStruct((S, F), BF16), jax.ShapeDtypeStruct((S, F), BF16), jax.ShapeDtypeStruct((S, F), BF16),
                   jax.ShapeDtypeStruct((1, D), F32), jax.ShapeDtypeStruct((1, D), F32)],
        scratch_shapes=[pltpu.VMEM((tm, D), F32)],
        compiler_params=_cparams(blk),
    )(dy, z, hg, hu, wg_arr, wu_arr, wd_arr, g)


def _mm_tn(a, b, name):
    S, M = a.shape
    N = b.shape[1]
    bm = _divisor_tile(M, 1408, LANES)
    bn = _divisor_tile(N, 1408, LANES)
    tk = _divisor_tile(S, 512, 16)
    nk = S // tk

    def body(a_ref, b_ref, o_ref, ob_ref):
        k = pl.program_id(2)

        @pl.when(k == 0)
        def _():
            o_ref[...] = jnp.zeros_like(o_ref)

        o_ref[...] += _dot_tn(a_ref[...].astype(BF16), b_ref[...].astype(BF16))

        @pl.when(k == nk - 1)
        def _():
            ob_ref[...] = o_ref[...].astype(BF16)

    blk = 2 * (tk * bm * a.dtype.itemsize + tk * bn * b.dtype.itemsize + bm * bn * 6) + tk * bm * 4 + bm * bn * 4
    tile = _bs((bm, bn), lambda i, j, k: (i, j))
    return _streamed_call(
        body, name=name, grid=(M // bm, N // bn, nk),
        in_specs=[_bs((tk, bm), lambda i, j, k: (k, i)), _bs((tk, bn), lambda i, j, k: (k, j))],
        out_specs=[tile, tile],
        out_shape=[jax.ShapeDtypeStruct((M, N), F32), jax.ShapeDtypeStruct((M, N), BF16)],
        compiler_params=_cparams(blk),
    )(a, b)


def _mm_tn_pair(a, b0, b1, name):
    S, M = a.shape
    N = b0.shape[1]
    assert b1.shape == b0.shape
    bm = _divisor_tile(M, 1408, LANES)
    bn = _divisor_tile(N, 1408, LANES)
    tk = _divisor_tile(S, 512, 16)
    nb = N // bn
    nk = S // tk

    def body(a_ref, b0_ref, b1_ref, o_ref, ob_ref):
        j = pl.program_id(1)
        k = pl.program_id(2)

        @pl.when(k == 0)
        def _():
            o_ref[...] = jnp.zeros_like(o_ref)

        ab = a_ref[...].astype(BF16)

        @pl.when(j < nb)
        def _():
            o_ref[...] += _dot_tn(ab, b0_ref[...])

        @pl.when(j >= nb)
        def _():
            o_ref[...] += _dot_tn(ab, b1_ref[...])

        @pl.when(k == nk - 1)
        def _():
            ob_ref[...] = o_ref[...].astype(BF16)

    b0_map = lambda i, j, k: (jnp.where(j < nb, k, nk - 1), jnp.minimum(j, nb - 1))
    b1_map = lambda i, j, k: (jnp.where(j >= nb, k, 0), jnp.maximum(j - nb, 0))
    blk = 2 * (tk * bm * a.dtype.itemsize + 2 * tk * bn * 2 + bm * bn * 6) + tk * bm * 4 + bm * bn * 4
    tile = _bs((bm, bn), lambda i, j, k: (i, j))
    return _streamed_call(
        body, name=name, grid=(M // bm, 2 * nb, nk),
        in_specs=[_bs((tk, bm), lambda i, j, k: (k, i)), _bs((tk, bn), b0_map), _bs((tk, bn), b1_map)],
        out_specs=[tile, tile],
        out_shape=[jax.ShapeDtypeStruct((M, 2 * N), F32), jax.ShapeDtypeStruct((M, 2 * N), BF16)],
        compiler_params=_cparams(blk),
    )(a, b0, b1)


def _mix_in(x, wt, bias, sections, name):
    S, D = x.shape
    tm = _divisor_tile(S, 256, 16)
    n = len(sections)

    def body(x_ref, w_ref, b_ref, *o_refs):
        xb = x_ref[...].astype(BF16)
        for (off, width, dtype), o_ref in zip(sections, o_refs):
            o_ref[...] = (_dot_nt(xb, w_ref[off:off + width, :]) + b_ref[:, off:off + width]).astype(dtype)

    total = wt.shape[0]
    blk = 2 * (tm * D * 4 + sum(tm * w * jnp.dtype(dt).itemsize for _, w, dt in sections)) + total * D * 2 + 3 * tm * D * 4
    return _streamed_call(
        body, name=name, grid=(S // tm,),
        in_specs=[_bs((tm, D), lambda i: (i, 0)), _resident((total, D)), _resident((1, total))],
        out_specs=[_bs((tm, w), lambda i: (i, 0)) for _, w, _ in sections],
        out_shape=[jax.ShapeDtypeStruct((S, w), dt) for _, w, dt in sections],
        compiler_params=_cparams(blk),
    )(x, wt, bias)


def _mix_dx(dz, parts, wt, sections, alpha, name):
    S, D = dz.shape
    tm = _divisor_tile(S, 256, 16)
    n = len(parts)

    def body(*refs):
        dz_ref = refs[0]
        p_refs = refs[1:1 + n]
        w_ref = refs[1 + n]
        o_ref = refs[2 + n]
        acc = alpha * dz_ref[...]
        for p_ref, (off, width, _) in zip(p_refs, sections):
            acc = acc + _dot(p_ref[...], w_ref[off:off + width, :])
        o_ref[...] = acc

    widths = [p.shape[1] for p in parts]
    total = wt.shape[0]
    blk = 2 * (2 * tm * D * 4 + sum(tm * w * 2 for w in widths)) + total * D * 2 + 2 * tm * D * 4
    return _streamed_call(
        body, name=name, grid=(S // tm,),
        in_specs=[_bs((tm, D), lambda i: (i, 0))] + [_bs((tm, w), lambda i: (i, 0)) for w in widths]
                 + [_resident((total, D))],
        out_specs=_bs((tm, D), lambda i: (i, 0)),
        out_shape=jax.ShapeDtypeStruct((S, D), F32),
        compiler_params=_cparams(blk),
    )(dz, *parts, wt)


def _conv_branch_fwd(cv, cg, w, b, gg, gb, name):
    S, C = cv.shape
    K = w.shape[0]
    assert C // CONV_GROUPS == LANES
    padf = _front_pad(K)
    R = min(CONV_ROWS, S)
    E = min(EW_ROWS, S)

    def body(cv_ref, cg_ref, w_ref, b_ref, gg_ref, gb_ref, c2_ref, c4_ref, pad_ref):
        pad_ref[0:padf, :] = jnp.zeros((padf, LANES), F32)
        pad_ref[S + padf:S + padf + SUBLANES, :] = jnp.zeros((SUBLANES, LANES), F32)

        def fill(i, carry):
            r = pl.multiple_of(i * E, E)
            pad_ref[pl.ds(r + padf, E), :] = cv_ref[pl.ds(r, E), :] * _sigmoid(cg_ref[pl.ds(r, E), :])
            return carry

        lax.fori_loop(0, S // E, fill, 0)
        taps = [(padf - (K - 1) + k, functools.partial(lambda k: w_ref[k:k + 1, :], k)) for k in range(K)]

        def conv(i, carry):
            r = pl.multiple_of(i * R, R)
            c2_ref[pl.ds(r, R), :] = _shifted_taps(pad_ref, r, R, taps) + b_ref[...]
            return carry

        lax.fori_loop(0, S // R, conv, 0)

        def norm(i, carry):
            r = pl.multiple_of(i * E, E)
            c3 = _norm_fwd(c2_ref[pl.ds(r, E), :], gg_ref[...], gb_ref[...], _lane_mean)
            c4_ref[pl.ds(r, E), :] = (c3 * _sigmoid(c3)).astype(BF16)
            return carry

        lax.fori_loop(0, S // E, norm, 0)

    col = lambda i: (0, i)
    blk = 2 * (3 * S * LANES * 4 + S * LANES * 2) + (S + _pad_rows(K)) * LANES * 4
    return _streamed_call(
        body, name=name, grid=(C // LANES,),
        in_specs=[_bs((S, LANES), col), _bs((S, LANES), col), _bs((K, LANES), col),
                  _bs((1, LANES), col), _bs((1, LANES), col), _bs((1, LANES), col)],
        out_specs=[_bs((S, LANES), col), _bs((S, LANES), col)],
        out_shape=[jax.ShapeDtypeStruct((S, C), F32), jax.ShapeDtypeStruct((S, C), BF16)],
        scratch_shapes=[pltpu.VMEM((S + _pad_rows(K), LANES), F32)],
        compiler_params=_cparams(blk),
    )(cv, cg, w, b, gg, gb)


def _conv_branch_bwd(dc4, c2, cv, cg, w, gg, gb, name):
    S, C = cv.shape
    K = w.shape[0]
    padf = _front_pad(K)
    R = min(CONV_ROWS, S)
    E = min(EW_ROWS, S)

    def body(dc4_ref, c2_ref, cv_ref, cg_ref, w_ref, gg_ref, gb_ref,
             dcv_ref, dcg_ref, dw_ref, dwb_ref, dgg_ref, dgb_ref, scv_ref, scg_ref,
             dpad_ref, cpad_ref, dwacc_ref):
        cpad_ref[0:padf, :] = jnp.zeros((padf, LANES), F32)
        cpad_ref[S + padf:S + padf + SUBLANES, :] = jnp.zeros((SUBLANES, LANES), F32)
        dpad_ref[S:S + padf + SUBLANES, :] = jnp.zeros((padf + SUBLANES, LANES), F32)
        dwacc_ref[...] = jnp.zeros_like(dwacc_ref)
        for ref in (dwb_ref, dgg_ref, dgb_ref, scv_ref, scg_ref):
            ref[...] = jnp.zeros_like(ref)

        def norm_pass(i, carry):
            r = pl.multiple_of(i * E, E)
            g_ = gg_ref[...]
            c2 = c2_ref[pl.ds(r, E), :]
            xc = c2 - _lane_mean(c2)
            rstd = lax.rsqrt(_lane_mean(xc * xc) + LN_EPS)
            xhat = xc * rstd
            c3 = xhat * g_ + gb_ref[...]
            s = _sigmoid(c3)
            dc3 = dc4_ref[pl.ds(r, E), :].astype(F32) * (s * (1.0 + c3 * (1.0 - s)))
            dgg_ref[...] += _colsum(dc3 * xhat)
            dgb_ref[...] += _colsum(dc3)
            dxh = dc3 * g_
            dc2 = rstd * (dxh - _lane_mean(dxh) - xhat * _lane_mean(dxh * xhat))
            dpad_ref[pl.ds(r, E), :] = dc2
            dwb_ref[...] += _colsum(dc2)
            cpad_ref[pl.ds(r + padf, E), :] = cv_ref[pl.ds(r, E), :] * _sigmoid(cg_ref[pl.ds(r, E), :])
            return carry

        lax.fori_loop(0, S // E, norm_pass, 0)
        taps = [(K - 1 - k, functools.partial(lambda k: w_ref[k:k + 1, :], k)) for k in range(K)]
        offs = [padf - (K - 1) + k for k in range(K)]

        def conv_pass(i, carry):
            r = pl.multiple_of(i * R, R)
            dc1 = _shifted_taps(dpad_ref, r, R, taps)
            sg = _sigmoid(cg_ref[pl.ds(r, R), :])
            cv_ = cv_ref[pl.ds(r, R), :]
            dcv = dc1 * sg
            dcg = dc1 * cv_ * sg * (1.0 - sg)
            dcv_ref[pl.ds(r, R), :] = dcv.astype(BF16)
            dcg_ref[pl.ds(r, R), :] = dcg.astype(BF16)
            scv_ref[...] += _colsum(dcv)
            scg_ref[...] += _colsum(dcg)
            _shifted_corr(cpad_ref, r, R, dpad_ref[pl.ds(r, R), :], dwacc_ref, offs)
            return carry

        lax.fori_loop(0, S // R, conv_pass, 0)
        for k in range(K):
            dw_ref[k:k + 1, :] = _colsum(dwacc_ref[SUBLANES * k:SUBLANES * (k + 1), :])

    col = lambda i: (0, i)
    row = jax.ShapeDtypeStruct((1, C), F32)
    blk = 2 * (4 * S * LANES * 4 + 2 * S * LANES * 2) + 2 * (S + _pad_rows(K)) * LANES * 4
    return _streamed_call(
        body, name=name, grid=(C // LANES,),
        in_specs=[_bs((S, LANES), col)] * 4 + [_bs((K, LANES), col), _bs((1, LANES), col), _bs((1, LANES), col)],
        out_specs=[_bs((S, LANES), col), _bs((S, LANES), col), _bs((K, LANES), col)] + [_bs((1, LANES), col)] * 5,
        out_shape=[jax.ShapeDtypeStruct((S, C), BF16), jax.ShapeDtypeStruct((S, C), BF16),
                   jax.ShapeDtypeStruct((K, C), F32), row, row, row, row, row],
        scratch_shapes=[pltpu.VMEM((S + _pad_rows(K), LANES), F32), pltpu.VMEM((S + _pad_rows(K), LANES), F32),
                        pltpu.VMEM((SUBLANES * K, LANES), F32)],
        compiler_params=_cparams(blk),
    )(dc4, c2, cv, cg, w, gg, gb)


def _short_conv_fwd(xin, w, b, name):
    S, C = xin.shape
    K = w.shape[0]
    padf = _front_pad(K)
    R = min(CONV_ROWS, S)
    E = min(EW_ROWS, S)

    def body(x_ref, w_ref, b_ref, o_ref, pad_ref):
        pad_ref[0:padf, :] = jnp.zeros((padf, LANES), F32)
        pad_ref[S + padf:S + padf + SUBLANES, :] = jnp.zeros((SUBLANES, LANES), F32)

        def fill(i, carry):
            r = pl.multiple_of(i * E, E)
            pad_ref[pl.ds(r + padf, E), :] = x_ref[pl.ds(r, E), :]
            return carry

        lax.fori_loop(0, S // E, fill, 0)
        taps = [(padf - (K - 1) + k, functools.partial(lambda k: w_ref[k:k + 1, :], k)) for k in range(K)]

        def conv(i, carry):
            r = pl.multiple_of(i * R, R)
            o_ref[pl.ds(r, R), :] = _shifted_taps(pad_ref, r, R, taps) + b_ref[...]
            return carry

        lax.fori_loop(0, S // R, conv, 0)

    col = lambda i: (0, i)
    blk = 2 * (2 * S * LANES * 4) + (S + _pad_rows(K)) * LANES * 4
    return _streamed_call(
        body, name=name, grid=(C // LANES,),
        in_specs=[_bs((S, LANES), col), _bs((K, LANES), col), _bs((1, LANES), col)],
        out_specs=_bs((S, LANES), col),
        out_shape=jax.ShapeDtypeStruct((S, C), F32),
        scratch_shapes=[pltpu.VMEM((S + _pad_rows(K), LANES), F32)],
        compiler_params=_cparams(blk),
    )(xin, w, b)


def _short_conv_bwd(dy, xin, w, name):
    S, C = xin.shape
    K = w.shape[0]
    padf = _front_pad(K)
    R = min(CONV_ROWS, S)
    E = min(EW_ROWS, S)

    def body(dy_ref, x_ref, w_ref, dx_ref, dw_ref, db_ref, sx_ref, dpad_ref, xpad_ref, dwacc_ref):
        xpad_ref[0:padf, :] = jnp.zeros((padf, LANES), F32)
        xpad_ref[S + padf:S + padf + SUBLANES, :] = jnp.zeros((SUBLANES, LANES), F32)
        dpad_ref[S:S + padf + SUBLANES, :] = jnp.zeros((padf + SUBLANES, LANES), F32)
        dwacc_ref[...] = jnp.zeros_like(dwacc_ref)
        db_ref[...] = jnp.zeros_like(db_ref)
        sx_ref[...] = jnp.zeros_like(sx_ref)

        def fill(i, carry):
            r = pl.multiple_of(i * E, E)
            d = dy_ref[pl.ds(r, E), :]
            dpad_ref[pl.ds(r, E), :] = d
            db_ref[...] += _colsum(d)
            xpad_ref[pl.ds(r + padf, E), :] = x_ref[pl.ds(r, E), :]
            return carry

        lax.fori_loop(0, S // E, fill, 0)
        taps = [(K - 1 - k, functools.partial(lambda k: w_ref[k:k + 1, :], k)) for k in range(K)]
        offs = [padf - (K - 1) + k for k in range(K)]

        def conv_pass(i, carry):
            r = pl.multiple_of(i * R, R)
            dx = _shifted_taps(dpad_ref, r, R, taps)
            dx_ref[pl.ds(r, R), :] = dx.astype(BF16)
            sx_ref[...] += _colsum(dx)
            _shifted_corr(xpad_ref, r, R, dpad_ref[pl.ds(r, R), :], dwacc_ref, offs)
            return carry

        lax.fori_loop(0, S // R, conv_pass, 0)
        for k in range(K):
            dw_ref[k:k + 1, :] = _colsum(dwacc_ref[SUBLANES * k:SUBLANES * (k + 1), :])

    col = lambda i: (0, i)
    row = jax.ShapeDtypeStruct((1, C), F32)
    blk = 2 * (2 * S * LANES * 4 + S * LANES * 2) + 2 * (S + _pad_rows(K)) * LANES * 4
    return _streamed_call(
        body, name=name, grid=(C // LANES,),
        in_specs=[_bs((S, LANES), col), _bs((S, LANES), col), _bs((K, LANES), col)],
        out_specs=[_bs((S, LANES), col), _bs((K, LANES), col), _bs((1, LANES), col), _bs((1, LANES), col)],
        out_shape=[jax.ShapeDtypeStruct((S, C), BF16), jax.ShapeDtypeStruct((K, C), F32), row, row],
        scratch_shapes=[pltpu.VMEM((S + _pad_rows(K), LANES), F32), pltpu.VMEM((S + _pad_rows(K), LANES), F32),
                        pltpu.VMEM((SUBLANES * K, LANES), F32)],
        compiler_params=_cparams(blk),
    )(dy, xin, w)


def _band_panels(width, block):
    assert width % LANES == 0 and block <= LANES
    panels = []
    for c0 in range(0, width, 2 * LANES):
        c1 = min(width, c0 + 2 * LANES)
        r0 = (c0 // block) * block // LANES * LANES
        r1 = min(width, -(-(-(-c1 // block) * block) // LANES) * LANES)
        panels.append((r0, r1, c0, c1))
    return panels


def _gates_fwd(r1, wa, wx, ba, bx, lam, name):
    S, R = r1.shape
    tm = _divisor_tile(S, 256, 16)
    panels = _band_panels(R, R // RNN_BLOCKS)

    def body(r1_ref, wa_ref, wx_ref, ba_ref, bx_ref, lam_ref, ra_ref, ri_ref, a_ref, uu_ref):
        for r0, r1e, c0, c1 in panels:
            rb = r1_ref[:, r0:r1e].astype(BF16)
            ra = _sigmoid(_dot(rb, wa_ref[r0:r1e, c0:c1]) + ba_ref[:, c0:c1])
            ri = _sigmoid(_dot(rb, wx_ref[r0:r1e, c0:c1]) + bx_ref[:, c0:c1])
            log_a = -RG_LRU_C * ra * _softplus(-lam_ref[:, c0:c1])
            ra_ref[:, c0:c1] = ra
            ri_ref[:, c0:c1] = ri
            a_ref[:, c0:c1] = jnp.exp(log_a)
            uu_ref[:, c0:c1] = jnp.sqrt(_neg_expm1(2.0 * log_a)) * (ri * r1_ref[:, c0:c1])

    blk = 2 * (5 * tm * R * 4) + 2 * R * R * 2 + 6 * tm * R * 4
    tile = _bs((tm, R), lambda i: (i, 0))
    return _streamed_call(
        body, name=name, grid=(S // tm,),
        in_specs=[tile, _resident((R, R)), _resident((R, R)), _resident((1, R)), _resident((1, R)), _resident((1, R))],
        out_specs=[tile] * 4,
        out_shape=[jax.ShapeDtypeStruct((S, R), F32)] * 4,
        compiler_params=_cparams(blk),
    )(r1, wa, wx, ba, bx, lam)


def _gates_bwd(guu, da, ra, ri, r1, wa, wx, lam, name):
    S, R = r1.shape
    tm = _divisor_tile(S, 256, 16)
    nsteps = S // tm
    panels = _band_panels(R, R // RNN_BLOCKS)

    def body(g_ref, da_ref, ra_ref, ri_ref, r1_ref, wa_ref, wx_ref, lam_ref,
             dr1_ref, dpa_ref, dpx_ref, dba_ref, dbx_ref, dlam_ref):
        i = pl.program_id(0)

        @pl.when(i == 0)
        def _():
            dba_ref[...] = jnp.zeros_like(dba_ref)
            dbx_ref[...] = jnp.zeros_like(dbx_ref)
            dlam_ref[...] = jnp.zeros_like(dlam_ref)

        g = g_ref[...]
        ra = ra_ref[...]
        ri = ri_ref[...]
        r1_ = r1_ref[...]
        sp = _softplus(-lam_ref[...])
        log_a = -RG_LRU_C * ra * sp
        a = jnp.exp(log_a)
        mult = jnp.sqrt(_neg_expm1(2.0 * log_a))
        d_ri = g * mult * r1_
        dr1 = g * mult * ri
        dmult = g * ri * r1_
        dlog_a = da_ref[...] * a - dmult * (a * a) / mult
        dra = dlog_a * (-RG_LRU_C * sp)
        dlam_ref[...] += _colsum(dlog_a * (-RG_LRU_C * ra))
        dpa = dra * ra * (1.0 - ra)
        dpx = d_ri * ri * (1.0 - ri)
        dba_ref[...] += _colsum(dpa)
        dbx_ref[...] += _colsum(dpx)
        dpa_b = dpa.astype(BF16)
        dpx_b = dpx.astype(BF16)
        dpa_ref[...] = dpa_b
        dpx_ref[...] = dpx_b
        dr1_ref[...] = dr1
        for k0, k1, c0, c1 in panels:
            dr1_ref[:, c0:c1] += (_dot_nt(dpa_ref[:, k0:k1], wa_ref[c0:c1, k0:k1])
                                  + _dot_nt(dpx_ref[:, k0:k1], wx_ref[c0:c1, k0:k1]))

        @pl.when(i == nsteps - 1)
        def _():
            dlam_ref[...] = dlam_ref[...] * (-_sigmoid(-lam_ref[...]))

    blk = 2 * (6 * tm * R * 4 + 2 * tm * R * 2) + 2 * R * R * 2 + 10 * tm * R * 4
    tile = _bs((tm, R), lambda i: (i, 0))
    rowspec = _bs((1, R), lambda i: (0, 0))
    row = jax.ShapeDtypeStruct((1, R), F32)
    return _streamed_call(
        body, name=name, grid=(nsteps,),
        in_specs=[tile] * 5 + [_resident((R, R)), _resident((R, R)), _resident((1, R))],
        out_specs=[tile, tile, tile, rowspec, rowspec, rowspec],
        out_shape=[jax.ShapeDtypeStruct((S, R), F32), jax.ShapeDtypeStruct((S, R), BF16), jax.ShapeDtypeStruct((S, R), BF16),
                   row, row, row],
        compiler_params=_cparams(blk),
    )(guu, da, ra, ri, r1, wa, wx, lam)


def _embed_blocks(w, name):
    H, bk, _ = w.shape

    def body(w_ref, o_ref):
        o_ref[...] = jnp.zeros_like(o_ref)
        for h in range(H):
            o_ref[bk * h:bk * (h + 1), bk * h:bk * (h + 1)] = w_ref[h].astype(BF16)

    return pl.pallas_call(body, name=name, out_shape=jax.ShapeDtypeStruct((H * bk, H * bk), BF16),
                          compiler_params=_cparams(3 * H * bk * H * bk * 2))(w)


def _block_grads(r1, dpa, dpx, name):
    S, R = r1.shape
    bk = R // RNN_BLOCKS
    tk = _divisor_tile(S, 512, 16)
    nsteps = S // tk
    panels = _band_panels(R, bk)

    def body(r1_ref, dpa_ref, dpx_ref, ga_ref, gx_ref, acca_ref, accx_ref):
        k = pl.program_id(0)

        @pl.when(k == 0)
        def _():
            acca_ref[...] = jnp.zeros_like(acca_ref)
            accx_ref[...] = jnp.zeros_like(accx_ref)

        for k0, k1, c0, c1 in panels:
            rb = r1_ref[:, k0:k1].astype(BF16)
            acca_ref[k0:k1, c0:c1] += _dot_tn(rb, dpa_ref[:, c0:c1])
            accx_ref[k0:k1, c0:c1] += _dot_tn(rb, dpx_ref[:, c0:c1])

        @pl.when(k == nsteps - 1)
        def _():
            for h in range(RNN_BLOCKS):
                ga_ref[h] = acca_ref[bk * h:bk * (h + 1), bk * h:bk * (h + 1)]
                gx_ref[h] = accx_ref[bk * h:bk * (h + 1), bk * h:bk * (h + 1)]

    tile = lambda: _bs((tk, R), lambda k: (k, 0))
    out = _bs((RNN_BLOCKS, bk, bk), lambda k: (0, 0, 0))
    sds = jax.ShapeDtypeStruct((RNN_BLOCKS, bk, bk), F32)
    return _streamed_call(
        body, name=name, grid=(nsteps,),
        in_specs=[tile(), tile(), tile()], out_specs=[out, out], out_shape=[sds, sds],
        scratch_shapes=[pltpu.VMEM((R, R), F32), pltpu.VMEM((R, R), F32)],
        compiler_params=_cparams(2 * (tk * R * 8) + 2 * R * R * 4 + 4 * tk * R * 4),
    )(r1, dpa, dpx)


def _scan_geometry(S):
    nseg = SCAN_SEGMENTS if S % (SCAN_SEGMENTS * SUBLANES) == 0 else SUBLANES
    return nseg, S // nseg


def _steps(n, step, init):
    u = SCAN_UNROLL

    def trip(t, carry):
        for k in range(u):
            carry = step(t * u + k, carry)
        return carry

    carry = lax.fori_loop(0, n // u, trip, init)
    for j in range(n - n % u, n):
        carry = step(j, carry)
    return carry


def _scan_fwd(a, u, name):
    S, C = a.shape
    nseg, L = _scan_geometry(S)
    T = min(SUBLANES, L)

    def body(a3, u3, h3, ta_ref, tu_ref, e_ref, p_ref, init_ref):

        def to_steps(i, carry):
            j0 = pl.multiple_of(i * T, T)
            ta_ref[pl.ds(j0, T)] = jnp.swapaxes(a3[:, pl.ds(j0, T), :], 0, 1)
            tu_ref[pl.ds(j0, T)] = jnp.swapaxes(u3[:, pl.ds(j0, T), :], 0, 1)
            return carry

        lax.fori_loop(0, L // T, to_steps, 0)

        def run1(j, carry):
            hs, ps = carry
            aj = ta_ref[j]
            return aj * hs + tu_ref[j], aj * ps

        e_ref[...], p_ref[...] = _steps(L, run1, (jnp.zeros((nseg, LANES), F32), jnp.ones((nseg, LANES), F32)))
        init_ref[0:1, :] = jnp.zeros((1, LANES), F32)
        for s in range(1, nseg):
            init_ref[s:s + 1, :] = e_ref[s - 1:s, :] + p_ref[s - 1:s, :] * init_ref[s - 1:s, :]

        def run2(j, hs):
            hs = ta_ref[j] * hs + tu_ref[j]
            tu_ref[j] = hs
            return hs

        _steps(L, run2, init_ref[...])

        def from_steps(i, carry):
            j0 = pl.multiple_of(i * T, T)
            h3[:, pl.ds(j0, T), :] = jnp.swapaxes(tu_ref[pl.ds(j0, T)], 0, 1)
            return carry

        lax.fori_loop(0, L // T, from_steps, 0)

    seg_block = _bs((nseg, L, LANES), lambda i: (0, 0, i))
    blk = 2 * (3 * S * LANES * 4) + 2 * S * LANES * 4
    return _streamed_call(
        body, name=name, grid=(C // LANES,),
        in_specs=[seg_block, seg_block],
        out_specs=seg_block,
        out_shape=jax.ShapeDtypeStruct((nseg, L, C), F32),
        scratch_shapes=[pltpu.VMEM((L, nseg, LANES), F32)] * 2 + [pltpu.VMEM((nseg, LANES), F32)] * 3,
        compiler_params=_cparams(blk),
    )(a.reshape(nseg, L, C), u.reshape(nseg, L, C)).reshape(S, C)


def _scan_bwd(a, dh, h, name):
    S, C = a.shape
    nseg, L = _scan_geometry(S)
    T = min(SUBLANES, L)
    assert L >= 2

    def body(a3, d3, h3, g3, da3, ta_ref, td_ref, th_ref, e_ref, p_ref, init_ref):

        def to_steps(i, carry):
            j0 = pl.multiple_of(i * T, T)
            for src, dst in ((a3, ta_ref), (d3, td_ref), (h3, th_ref)):
                dst[pl.ds(j0, T)] = jnp.swapaxes(src[:, pl.ds(j0, T), :], 0, 1)
            return carry

        lax.fori_loop(0, L // T, to_steps, 0)
        seg = lax.broadcasted_iota(jnp.int32, (nseg, LANES), 0)
        b_last = jnp.where(seg == nseg - 1, 0.0, pltpu.roll(ta_ref[0], nseg - 1, axis=0))
        h_first = jnp.where(seg == 0, 0.0, pltpu.roll(th_ref[L - 1], 1, axis=0))

        def run1(jj, carry):
            gs, ps = carry
            j = L - 2 - jj
            bj = ta_ref[j + 1]
            return bj * gs + td_ref[j], bj * ps

        e_ref[...], p_ref[...] = _steps(L - 1, run1, (td_ref[L - 1], b_last))
        init_ref[nseg - 1:nseg, :] = jnp.zeros((1, LANES), F32)
        for s in range(nseg - 2, -1, -1):
            init_ref[s:s + 1, :] = e_ref[s + 1:s + 2, :] + p_ref[s + 1:s + 2, :] * init_ref[s + 1:s + 2, :]

        gs = b_last * init_ref[...] + td_ref[L - 1]
        td_ref[L - 1] = gs
        th_ref[L - 1] = gs * th_ref[L - 2]

        def run2(jj, gs):
            j = L - 2 - jj
            gs = ta_ref[j + 1] * gs + td_ref[j]
            td_ref[j] = gs
            th_ref[j] = gs * th_ref[j - 1]
            return gs

        gs = _steps(L - 2, run2, gs)
        gs = ta_ref[1] * gs + td_ref[0]
        td_ref[0] = gs
        th_ref[0] = gs * h_first

        def from_steps(i, carry):
            j0 = pl.multiple_of(i * T, T)
            g3[:, pl.ds(j0, T), :] = jnp.swapaxes(td_ref[pl.ds(j0, T)], 0, 1)
            da3[:, pl.ds(j0, T), :] = jnp.swapaxes(th_ref[pl.ds(j0, T)], 0, 1)
            return carry

        lax.fori_loop(0, L // T, from_steps, 0)

    seg_block = _bs((nseg, L, LANES), lambda i: (0, 0, i))
    blk = 2 * (5 * S * LANES * 4) + 3 * S * LANES * 4
    g, da = _streamed_call(
        body, name=name, grid=(C // LANES,),
        in_specs=[seg_block] * 3,
        out_specs=[seg_block] * 2,
        out_shape=[jax.ShapeDtypeStruct((nseg, L, C), F32)] * 2,
        scratch_shapes=[pltpu.VMEM((L, nseg, LANES), F32)] * 3 + [pltpu.VMEM((nseg, LANES), F32)] * 3,
        compiler_params=_cparams(blk),
    )(a.reshape(nseg, L, C), dh.reshape(nseg, L, C), h.reshape(nseg, L, C))
    return g.reshape(S, C), da.reshape(S, C)


def _mixer_out_fwd(c4, h, rg, gc, gr, x1, wc, wr, wo, g, b, alpha, name):
    S, D = x1.shape
    R = h.shape[1]
    tm = _divisor_tile(S, 256, 16)

    def body(c4_ref, h_ref, rg_ref, gc_ref, gr_ref, x_ref, wc_ref, wr_ref, wo_ref, g_ref, b_ref,
             yc_ref, yr_ref, z_ref, y_ref):
        yc = _dot(c4_ref[...], wc_ref[...])
        q = (h_ref[...] * _gelu(rg_ref[...].astype(F32))).astype(BF16)
        yr = _dot(q, wr_ref[...])
        yc_ref[...] = yc.astype(BF16)
        yr_ref[...] = yr.astype(BF16)
        m = (_sigmoid(gc_ref[...].astype(F32)) * yc + _sigmoid(gr_ref[...].astype(F32)) * yr).astype(BF16)
        z = alpha * x_ref[...] + _dot(m, wo_ref[...])
        z_ref[...] = z
        y_ref[...] = _norm_fwd(z, g_ref[...], b_ref[...])

    blk = 2 * (tm * D * 2 + 2 * tm * R * 4 + 7 * tm * D * 4) + (2 * D * D + R * D) * 2 + 6 * tm * D * 4
    td = _bs((tm, D), lambda i: (i, 0))
    tr = _bs((tm, R), lambda i: (i, 0))
    return _streamed_call(
        body, name=name, grid=(S // tm,),
        in_specs=[td, tr, tr, td, td, td, _layer_resident(wc)[1], _layer_resident(wr)[1], _layer_resident(wo)[1],
                  _resident((1, D)), _resident((1, D))],
        out_specs=[td] * 4,
        out_shape=[jax.ShapeDtypeStruct((S, D), BF16)] * 2 + [jax.ShapeDtypeStruct((S, D), F32)] * 2,
        compiler_params=_cparams(blk),
    )(c4, h, rg, gc, gr, x1, wc[0], wr[0], wo[0], g, b)


def _mixer_out_bwd(dy, z, g, wo, yc, yr, gc, gr, name):
    S, D = dy.shape
    tm = _divisor_tile(S, 256, 16)

    def body(dy_ref, z_ref, g_ref, wo_ref, yc_ref, yr_ref, gc_ref, gr_ref,
             dz_ref, dzb_ref, m_ref, dyc_ref, dyr_ref, dgc_ref, dgr_ref, sgc_ref, sgr_ref, dg_ref, db_ref):
        @pl.when(pl.program_id(0) == 0)
        def _():
            for ref in (sgc_ref, sgr_ref, dg_ref, db_ref):
                ref[...] = jnp.zeros_like(ref)

        dy_ = dy_ref[...]
        dz, xhat = _norm_bwd(z_ref[...], g_ref[...], dy_)
        dg_ref[...] += _colsum(dy_ * xhat)
        db_ref[...] += _colsum(dy_)
        dz_ref[...] = dz
        dzb = dz.astype(BF16)
        dzb_ref[...] = dzb
        dm = _dot_nt(dzb, wo_ref[...])
        yc = yc_ref[...].astype(F32)
        yr = yr_ref[...].astype(F32)
        sc = _sigmoid(gc_ref[...].astype(F32))
        sr = _sigmoid(gr_ref[...].astype(F32))
        m_ref[...] = (sc * yc + sr * yr).astype(BF16)
        dyc_ref[...] = (dm * sc).astype(BF16)
        dyr_ref[...] = (dm * sr).astype(BF16)
        dgc = dm * yc * sc * (1.0 - sc)
        dgr = dm * yr * sr * (1.0 - sr)
        dgc_ref[...] = dgc.astype(BF16)
        dgr_ref[...] = dgr.astype(BF16)
        sgc_ref[...] += _colsum(dgc)
        sgr_ref[...] += _colsum(dgr)

    blk = 2 * (7 * tm * D * 4 + 6 * tm * D * 2) + D * D * 2 + 8 * tm * D * 4
    td = _bs((tm, D), lambda i: (i, 0))
    rowspec = _bs((1, D), lambda i: (0, 0))
    row = jax.ShapeDtypeStruct((1, D), F32)
    bfd = jax.ShapeDtypeStruct((S, D), BF16)
    return _streamed_call(
        body, name=name, grid=(S // tm,),
        in_specs=[td, td, _resident((1, D)), _layer_resident(wo)[1], td, td, td, td],
        out_specs=[td] * 7 + [rowspec] * 4,
        out_shape=[jax.ShapeDtypeStruct((S, D), F32), bfd, bfd, bfd, bfd, bfd, bfd, row, row, row, row],
        compiler_params=_cparams(blk),
    )(dy, z, g, wo[0], yc, yr, gc, gr)


def _branch_bwd(dyc, dyr, wc, wr, h, rg, name):
    S, D = dyc.shape
    R = h.shape[1]
    tm = _divisor_tile(S, 256, 16)

    def body(dyc_ref, dyr_ref, wc_ref, wr_ref, h_ref, rg_ref, dc4_ref, dh_ref, drg_ref, q_ref, srg_ref):
        @pl.when(pl.program_id(0) == 0)
        def _():
            srg_ref[...] = jnp.zeros_like(srg_ref)

        dc4_ref[...] = _dot_nt(dyc_ref[...], wc_ref[...]).astype(BF16)
        dq = _dot_nt(dyr_ref[...], wr_ref[...])
        h_ = h_ref[...]
        rg_ = rg_ref[...].astype(F32)
        ge = _gelu(rg_)
        dh_ref[...] = dq * ge
        drg = dq * h_ * _gelu_grad(rg_)
        drg_ref[...] = drg.astype(BF16)
        srg_ref[...] += _colsum(drg)
        q_ref[...] = (h_ * ge).astype(BF16)

    blk = 2 * (2 * tm * D * 2 + tm * D * 4 + 3 * tm * R * 4 + 2 * tm * R * 2) + (D * D + R * D) * 2 + 6 * tm * R * 4
    td = _bs((tm, D), lambda i: (i, 0))
    tr = _bs((tm, R), lambda i: (i, 0))
    return _streamed_call(
        body, name=name, grid=(S // tm,),
        in_specs=[td, td, _layer_resident(wc)[1], _layer_resident(wr)[1], tr, tr],
        out_specs=[td, tr, tr, tr, _bs((1, R), lambda i: (0, 0))],
        out_shape=[jax.ShapeDtypeStruct((S, D), BF16), jax.ShapeDtypeStruct((S, R), F32), jax.ShapeDtypeStruct((S, R), BF16),
                   jax.ShapeDtypeStruct((S, R), BF16), jax.ShapeDtypeStruct((1, R), F32)],
        compiler_params=_cparams(blk),
    )(dyc, dyr, wc[0], wr[0], h, rg)


def _loss_head(y, target, name):
    S, D = y.shape
    tm = _divisor_tile(S, 512, 16)
    nsteps = S // tm

    def body(y_ref, t_ref, loss_ref, dy_ref, acc_ref):
        i = pl.program_id(0)

        @pl.when(i == 0)
        def _():
            acc_ref[...] = jnp.zeros_like(acc_ref)

        err = y_ref[...] - t_ref[...]
        dy_ref[...] = err * (1.0 / D)
        acc_ref[...] += _colsum(err * err)

        @pl.when(i == nsteps - 1)
        def _():
            loss_ref[...] = jnp.sum(acc_ref[...], axis=-1, keepdims=True) * (0.5 / D)

    td = _bs((tm, D), lambda i: (i, 0))
    return _streamed_call(
        body, name=name, grid=(nsteps,),
        in_specs=[td, td],
        out_specs=[_bs((1, 1), lambda i: (0, 0)), td],
        out_shape=[jax.ShapeDtypeStruct((1, 1), F32), jax.ShapeDtypeStruct((S, D), F32)],
        scratch_shapes=[pltpu.VMEM((1, D), F32)],
        compiler_params=_cparams(2 * 3 * tm * D * 4),
    )(y, target)


def _adamw_math(w, g, m, v):
    m = ADAM_B1 * m + (1.0 - ADAM_B1) * g
    v = ADAM_B2 * v + (1.0 - ADAM_B2) * (g * g)
    m_hat = m / (1.0 - ADAM_B1 ** ADAM_STEP)
    v_hat = v / (1.0 - ADAM_B2 ** ADAM_STEP)
    delta = -ADAM_LR * (m_hat / (jnp.sqrt(v_hat) + ADAM_EPS) + ADAM_WD * w)
    return delta, m, v


def _adamw_sharded(w, m, v, own, sib, rem, layer, filled, name):
    layers, r, c = w.shape
    r2 = r // 2
    tr = _divisor_tile(r2, max(16, (1 << 20) // (4 * c) // 16 * 16), 16)
    n_out = 4

    def body(w_ref, m_ref, v_ref, own_ref, sib_ref, rem_ref, *rest):
        g_ref, d_ref, nm_ref, nv_ref = rest[-n_out:]
        mine = pl.program_id(0) == lax.axis_index("c")
        g = jnp.where(mine, own_ref[...], sib_ref[...]).astype(F32)
        for j in range(N_CHIPS - 1):
            g = g + rem_ref[j].astype(F32)
        delta, nm, nv = _adamw_math(w_ref[...], g, m_ref[...], v_ref[...])
        g_ref[...] = g
        d_ref[...] = delta
        nm_ref[...] = nm
        nv_ref[...] = nv

    halves = lambda a: a.reshape(layers, 2, r2, c)
    tile = _bs((None, None, tr, c), lambda h, i: (layer, h, i, 0))
    flat = _bs((tr, c), lambda h, i: (i, 0))
    sds = jax.ShapeDtypeStruct((layers, 2, r2, c), F32)
    passed = [] if filled is None else [halves(a) for a in filled]
    outs = _streamed_call(
        body, name=name, grid=(2, r2 // tr),
        in_specs=[tile, tile, tile, flat, flat, _bs((N_CHIPS - 1, None, tr, c), lambda h, i: (0, h, i, 0))] + [ANY] * len(passed),
        out_specs=[tile] * n_out,
        out_shape=[sds] * n_out,
        input_output_aliases={6 + k: k for k in range(len(passed))},
        compiler_params=_cparams(2 * (7 * tr * c * 4 + (N_CHIPS + 1) * tr * c * 2) + 6 * tr * c * 4),
    )(halves(w), halves(m), halves(v), own, sib, rem, *passed)
    return [o.reshape(layers, r, c) for o in outs]


def _adamw_small(ws, ms, vs, gs, name):
    n = len(ws)

    def body(*refs):
        w_refs, m_refs, v_refs, g_refs = (refs[k * n:(k + 1) * n] for k in range(4))
        d_refs, nm_refs, nv_refs = (refs[(4 + k) * n:(5 + k) * n] for k in range(3))
        for i in range(n):
            delta, nm, nv = _adamw_math(w_refs[i][...], g_refs[i][...], m_refs[i][...], v_refs[i][...])
            d_refs[i][...] = delta
            nm_refs[i][...] = nm
            nv_refs[i][...] = nv

    sds = [jax.ShapeDtypeStruct(w.shape, F32) for w in ws]
    total = sum(_nbytes(w.shape, F32) for w in ws)
    outs = pl.pallas_call(body, name=name, out_shape=sds * 3, compiler_params=_cparams(16 * total))(*ws, *ms, *vs, *gs)
    return outs[:n], outs[n:2 * n], outs[2 * n:]


def _half_view(g, by_cols):
    rows, cols = g.shape
    if by_cols:
        return g.reshape(2, rows // 2, cols)
    return g.reshape(N_CHIPS, 2, rows // (2 * N_CHIPS), cols)


def _pair_sum_bf16(view, theirs, by_cols, name):
    rows, c = theirs.shape[-2:]
    tr = _divisor_tile(rows, max(16, (1 << 20) // (4 * c) // 16 * 16), 16)

    def body(half_ref, g_ref, t_ref, o_ref):
        o_ref[...] = (g_ref[...] + t_ref[...].astype(F32)).astype(BF16)

    if by_cols:
        grid = (rows // tr,)
        mine = _bs((None, tr, c), lambda i, half: (half[0], i, 0))
        tile = _bs((tr, c), lambda i, half: (i, 0))
    else:
        grid = (N_CHIPS, rows // tr)
        mine = _bs((None, None, tr, c), lambda k, i, half: (k, half[0], i, 0))
        tile = _bs((None, tr, c), lambda k, i, half: (k, i, 0))
    return pl.pallas_call(
        body, name=name,
        grid_spec=pltpu.PrefetchScalarGridSpec(num_scalar_prefetch=1, grid=grid, in_specs=[mine, tile], out_specs=tile),
        out_shape=jax.ShapeDtypeStruct(theirs.shape, BF16),
        compiler_params=_cparams(2 * 3 * tr * c * 4),
    )(lax.axis_index("c").astype(jnp.int32).reshape(1), pltpu.with_memory_space_constraint(view, pltpu.HBM),
      pltpu.with_memory_space_constraint(theirs, pltpu.HBM))


ANY = pl.BlockSpec(memory_space=pl.ANY)


def _mesh_position():
    return lax.axis_index("x"), lax.axis_index("y"), lax.axis_index("c")


def _other_chips():
    x, y, c = _mesh_position()
    chips = [(1 - x, y), (x, 1 - y), (1 - x, 1 - y)]
    return 2 * x + y, (x, y, 1 - c), chips, [2 * cx + cy for cx, cy in chips]


def _chip_slab(ref, k, width, by_cols):
    if by_cols:
        start = k * width if isinstance(k, int) else pl.multiple_of(k * width, LANES)
        return ref.at[:, pl.ds(start, width)]
    return ref.at[k]


HBM = pl.BlockSpec(memory_space=pltpu.HBM)
SEM = pl.BlockSpec(memory_space=pltpu.SEMAPHORE)
DATAFLOW = pltpu.SideEffectType.DATAFLOW_SIDE_EFFECTING
N_GATHER_COPIES = 4


def _land_shape(src, by_cols):
    return src.shape[:-1] + (N_CHIPS * src.shape[-1],) if by_cols else (N_CHIPS,) + src.shape


def _gather_copy(src_ref, land_ref, by_cols, send_sems, recv_sems, pos, j, slab, to):
    width = src_ref.shape[-1]
    return pltpu.make_async_remote_copy(src_ref=src_ref, dst_ref=_chip_slab(land_ref, slab, width, by_cols),
                                        send_sem=send_sems.at[N_GATHER_COPIES * pos + j],
                                        recv_sem=recv_sems.at[N_GATHER_COPIES * pos + j],
                                        device_id=to, device_id_type=MESH)


def _gather_start(srcs, by_cols, groups, name):
    U = len(srcs)
    G = len(groups)
    lands = [lax.empty(_land_shape(s, bc), s.dtype) for s, bc in zip(srcs, by_cols)]

    def body(*refs):
        src = refs[:U]
        land = refs[U:2 * U]
        send_sems = refs[2 * U:2 * U + G]
        recv_sems = refs[2 * U + G:2 * U + 2 * G]
        token = refs[-1]
        c = lax.axis_index("c")
        me, sibling, chips, _ = _other_chips()
        targets = [(*chip, c) for chip in chips] + [sibling]
        for g, members in enumerate(groups):
            for pos, u in enumerate(members):
                for j, to in enumerate(targets):
                    _gather_copy(src[u], land[u], by_cols[u], send_sems[g], recv_sems[g], pos, j, me, to).start()
        token[...] = jnp.zeros_like(token)

    sem_shapes = [pltpu.SemaphoreType.DMA((len(m) * N_GATHER_COPIES,)) for m in groups]
    outs = pl.pallas_call(
        body, name=name,
        out_shape=tuple(sem_shapes + sem_shapes + [pltpu.HBM(s.shape, s.dtype) for s in srcs]
                        + [pltpu.HBM(v.shape, v.dtype) for v in lands] + [jax.ShapeDtypeStruct((SUBLANES, LANES), F32)]),
        in_specs=[HBM] * (2 * U),
        out_specs=tuple([SEM] * (2 * G) + [HBM] * (2 * U) + [pl.BlockSpec(memory_space=pltpu.VMEM)]),
        input_output_aliases={i: 2 * G + i for i in range(2 * U)},
        compiler_params=pltpu.CompilerParams(has_side_effects=DATAFLOW),
    )(*[pltpu.with_memory_space_constraint(a, pltpu.HBM) for a in list(srcs) + lands])
    return outs[:G], outs[G:2 * G], outs[2 * G:2 * G + U], outs[2 * G + U:2 * G + 2 * U]


def _gather_wait(srcs, lands, by_cols, send_sems, recv_sems, after, name):
    n = len(srcs)

    def body(*refs):
        src = refs[:n]
        land = refs[n:2 * n]
        send_ref, recv_ref = refs[2 * n:2 * n + 2]
        _, sibling, _, _ = _other_chips()
        for pos in range(n):
            for j in range(N_GATHER_COPIES):
                cp = _gather_copy(src[pos], land[pos], by_cols[pos], send_ref, recv_ref, pos, j, 0, sibling)
                cp.wait_send()
                cp.wait_recv()

    outs = pl.pallas_call(
        body, name=name,
        out_shape=tuple([pltpu.HBM(s.shape, s.dtype) for s in srcs] + [pltpu.HBM(v.shape, v.dtype) for v in lands]),
        in_specs=[HBM] * (2 * n) + [SEM, SEM, pl.BlockSpec(memory_space=pl.ANY)],
        out_specs=tuple([HBM] * (2 * n)),
        input_output_aliases={i: i for i in range(2 * n)},
        compiler_params=pltpu.CompilerParams(has_side_effects=DATAFLOW),
    )(*srcs, *lands, send_sems, recv_sems, after)
    return outs[n:]


def _scatter_grads(csums, by_cols, name):
    n = len(csums)
    shard = [(s.shape[0], s.shape[1] // N_CHIPS) if bc else s.shape[1:] for s, bc in zip(csums, by_cols)]

    def body(*refs):
        src = refs[:n]
        rem = refs[n:2 * n]
        sib = refs[2 * n:3 * n]
        send_sems, recv_sems = refs[3 * n:]
        c = lax.axis_index("c")
        me, sibling, chips, chip_ids = _other_chips()

        def remote(i, k, src_ref, dst_ref, to):
            return pltpu.make_async_remote_copy(src_ref=src_ref, dst_ref=dst_ref, send_sem=send_sems.at[i, k],
                                                recv_sem=recv_sems.at[i, k], device_id=to, device_id_type=MESH)

        def part(i, k):
            return _chip_slab(src[i], k, shard[i][-1], by_cols[i])

        started = []
        for i in range(n):
            for j in range(3):
                started.append(remote(i, j, part(i, chip_ids[j]), rem[i].at[j, c], (*chips[j], c)))
            started.append(remote(i, 6, part(i, me), sib[i], sibling))
        for cp in started:
            cp.start()
        for i in range(n):
            for j in range(3):
                slot = rem[i].at[j, c]
                remote(i, j, slot, slot, sibling).wait_recv()
                fwd = remote(i, 3 + j, slot, slot, sibling)
                fwd.start()
                started.append(fwd)
        for i in range(n):
            for j in range(3):
                slot = rem[i].at[j, 1 - c]
                remote(i, 3 + j, slot, slot, sibling).wait_recv()
            remote(i, 6, sib[i], sib[i], sibling).wait_recv()
        for cp in started:
            cp.wait_send()

    out_shape = ([jax.ShapeDtypeStruct((N_CHIPS - 1, 2) + tuple(sh), s.dtype) for s, sh in zip(csums, shard)]
                 + [jax.ShapeDtypeStruct(tuple(sh), s.dtype) for s, sh in zip(csums, shard)])
    outs = _streamed_call(
        body, name=name,
        in_specs=[ANY] * n, out_specs=[ANY] * (2 * n), out_shape=out_shape,
        scratch_shapes=[pltpu.SemaphoreType.DMA((n, 7)), pltpu.SemaphoreType.DMA((n, 7))],
    )(*csums)
    return outs[:n], outs[n:]


def _sibling_exchange(views, by_cols, name):
    n = len(views)

    def body(*refs):
        src = refs[:n]
        theirs = refs[n:2 * n]
        send_sems, recv_sems = refs[2 * n:]
        x, y, c = _mesh_position()
        copies = []
        for i in range(n):
            half = src[i].at[1 - c] if by_cols[i] else src[i].at[:, 1 - c]
            copies.append(pltpu.make_async_remote_copy(src_ref=half, dst_ref=theirs[i], send_sem=send_sems.at[i],
                                                       recv_sem=recv_sems.at[i], device_id=(x, y, 1 - c), device_id_type=MESH))
        for cp in copies:
            cp.start()
        for cp in copies:
            cp.wait()

    out_shape = [jax.ShapeDtypeStruct(v.shape[1:] if bc else v.shape[:1] + v.shape[2:], v.dtype) for v, bc in zip(views, by_cols)]
    return _streamed_call(
        body, name=name,
        in_specs=[ANY] * n, out_specs=[ANY] * n, out_shape=out_shape,
        scratch_shapes=[pltpu.SemaphoreType.DMA((n,)), pltpu.SemaphoreType.DMA((n,))],
    )(*views)


N_SCATTER_COPIES = 7


def _scatter_start(csums, by_cols, name):
    n = len(csums)
    shard = [(s.shape[0], s.shape[1] // N_CHIPS) if bc else s.shape[1:] for s, bc in zip(csums, by_cols)]
    rems = [lax.empty((N_CHIPS - 1, 2) + tuple(sh), s.dtype) for s, sh in zip(csums, shard)]
    sibs = [lax.empty(tuple(sh), s.dtype) for s, sh in zip(csums, shard)]

    def body(*refs):
        src = refs[:n]
        rem = refs[n:2 * n]
        sib = refs[2 * n:3 * n]
        send_sems, recv_sems = refs[3 * n:3 * n + 2]
        token = refs[-1]
        c = lax.axis_index("c")
        me, sibling, chips, chip_ids = _other_chips()
        for i in range(n):
            base = N_SCATTER_COPIES * i
            for j in range(3):
                part = _chip_slab(src[i], chip_ids[j], shard[i][-1], by_cols[i])
                for core in range(2):
                    pltpu.make_async_remote_copy(src_ref=part, dst_ref=rem[i].at[j, c], send_sem=send_sems.at[base + 2 * j + core],
                                                 recv_sem=recv_sems.at[base + 2 * j + c], device_id=(*chips[j], core),
                                                 device_id_type=MESH).start()
            pltpu.make_async_remote_copy(src_ref=_chip_slab(src[i], me, shard[i][-1], by_cols[i]), dst_ref=sib[i],
                                         send_sem=send_sems.at[base + 6], recv_sem=recv_sems.at[base + 6], device_id=sibling,
                                         device_id_type=MESH).start()
        token[...] = jnp.zeros_like(token)

    sems = pltpu.SemaphoreType.DMA((N_SCATTER_COPIES * n,))
    operands = list(csums) + rems + sibs
    outs = pl.pallas_call(
        body, name=name,
        out_shape=tuple([sems, sems] + [pltpu.HBM(a.shape, a.dtype) for a in operands] + [jax.ShapeDtypeStruct((SUBLANES, LANES), F32)]),
        in_specs=[HBM] * (3 * n),
        out_specs=tuple([SEM, SEM] + [HBM] * (3 * n) + [pl.BlockSpec(memory_space=pltpu.VMEM)]),
        input_output_aliases={i: 2 + i for i in range(3 * n)},
        compiler_params=pltpu.CompilerParams(has_side_effects=DATAFLOW),
    )(*[pltpu.with_memory_space_constraint(a, pltpu.HBM) for a in operands])
    return (outs[0], outs[1], outs[2:2 + n], outs[2 + n:2 + 2 * n], outs[2 + 2 * n:2 + 3 * n]), outs[-1]


def _scatter_wait(send_sems, recv_sems, srcs, rems, sibs, by_cols, after, name):
    n = len(srcs)

    def body(*refs):
        src = refs[:n]
        rem = refs[n:2 * n]
        sib = refs[2 * n:3 * n]
        send_ref, recv_ref = refs[3 * n:3 * n + 2]
        _, sibling, _, _ = _other_chips()
        for i in range(n):
            base = N_SCATTER_COPIES * i
            width = sib[i].shape[-1]
            for j in range(3):
                for core in range(2):
                    cp = pltpu.make_async_remote_copy(src_ref=_chip_slab(src[i], 0, width, by_cols[i]), dst_ref=rem[i].at[j, core],
                                                      send_sem=send_ref.at[base + 2 * j + core],
                                                      recv_sem=recv_ref.at[base + 2 * j + core], device_id=sibling,
                                                      device_id_type=MESH)
                    cp.wait_send()
                    cp.wait_recv()
            cp = pltpu.make_async_remote_copy(src_ref=_chip_slab(src[i], 0, width, by_cols[i]), dst_ref=sib[i],
                                              send_sem=send_ref.at[base + 6], recv_sem=recv_ref.at[base + 6], device_id=sibling,
                                              device_id_type=MESH)
            cp.wait_send()
            cp.wait_recv()

    operands = list(srcs) + list(rems) + list(sibs)
    outs = pl.pallas_call(
        body, name=name,
        out_shape=tuple(pltpu.HBM(a.shape, a.dtype) for a in operands),
        in_specs=[HBM] * (3 * n) + [SEM, SEM, pl.BlockSpec(memory_space=pl.ANY)],
        out_specs=tuple([HBM] * (3 * n)),
        input_output_aliases={i: i for i in range(3 * n)},
        compiler_params=pltpu.CompilerParams(has_side_effects=DATAFLOW),
    )(*operands, send_sems, recv_sems, after)
    return outs[:n], outs[n:2 * n], outs[2 * n:3 * n]


def _all_reduce_small(v, name):
    _, rows, _ = v.shape

    def body(v_ref, o_ref, recv_ref, send_sems, recv_sems):
        x, y, c = _mesh_position()
        me = 4 * x + 2 * y + c
        peers = []
        for d in range(1, N_DEV):
            px, py, pc = x ^ ((d >> 2) & 1), y ^ ((d >> 1) & 1), c ^ (d & 1)
            peers.append(((px, py, pc), 4 * px + 2 * py + pc))

        def remote(k, src_ref, dst_ref, to):
            return pltpu.make_async_remote_copy(src_ref=src_ref, dst_ref=dst_ref, send_sem=send_sems.at[k],
                                                recv_sem=recv_sems.at[k], device_id=to, device_id_type=MESH)

        scatter = [remote(d, v_ref.at[pid], recv_ref.at[me], to) for d, (to, pid) in enumerate(peers)]
        for cp in scatter:
            cp.start()
        recv_ref[pl.ds(me, 1)] = v_ref[pl.ds(me, 1)]
        for d, (to, pid) in enumerate(peers):
            remote(d, v_ref.at[pid], recv_ref.at[pid], to).wait_recv()
        total = recv_ref[0]
        for s in range(1, N_DEV):
            total = total + recv_ref[s]
        o_ref[pl.ds(me, 1)] = total[None]
        gather = [remote(N_DEV - 1 + d, o_ref.at[me], o_ref.at[me], to) for d, (to, pid) in enumerate(peers)]
        for cp in gather:
            cp.start()
        for d, (to, pid) in enumerate(peers):
            remote(N_DEV - 1 + d, o_ref.at[pid], o_ref.at[pid], to).wait_recv()
        for cp in scatter + gather:
            cp.wait_send()

    vm = pl.BlockSpec(memory_space=pltpu.VMEM)
    return pl.pallas_call(
        body, name=name,
        in_specs=[vm], out_specs=vm, out_shape=jax.ShapeDtypeStruct(v.shape, F32),
        scratch_shapes=[pltpu.VMEM(v.shape, F32), pltpu.SemaphoreType.DMA((2 * (N_DEV - 1),)),
                        pltpu.SemaphoreType.DMA((2 * (N_DEV - 1),))],
        compiler_params=_cparams(4 * _nbytes(v.shape, F32)),
    )(v)


SHARDED_MATS = ("ffn1_w_gu", "ffn1_w_down", "mix_w_in", "conv_w_proj", "rnn_w_proj", "mix_w_out", "ffn2_w_gu", "ffn2_w_down")
COL_SHARDED = ("ffn1_w_gu", "ffn2_w_gu", "conv_dw_w")
SHARDED_VECS = ("conv_dw_w", "rnn_conv_w")
WEIGHT_NAMES = ("ffn1_w_gu", "ffn1_w_down", "ln1_g", "ln1_b", "mix_w_in", "mix_b_in", "conv_dw_w", "conv_dw_b", "conv_gn_g",
                "conv_gn_b", "conv_w_proj", "rnn_conv_w", "rnn_conv_b", "rnn_w_a", "rnn_b_a", "rnn_w_x", "rnn_b_x",
                "rnn_lambda", "rnn_w_proj", "mix_w_out", "ln2_g", "ln2_b", "ffn2_w_gu", "ffn2_w_down", "ln3_g", "ln3_b")
SMALL_NAMES = tuple(n for n in WEIGHT_NAMES if n not in SHARDED_MATS)
SECTION_NAMES = ("cv", "cg", "rx", "rg", "gc", "gr")


def _unshard_cols(gathered):
    k4, K, n = gathered.shape
    return jnp.transpose(gathered, (1, 0, 2)).reshape(K, k4 * n)


def _row(v):
    return v.reshape(1, -1)


def _layer_forward(x0, p, alpha, l, hooks):
    t = f"l{l}_"
    sv = {"x0": x0}
    x1, sv["z1"], sv["hg1"], sv["hu1"] = _ffn_fwd(x0, p["wgu1"], p["wd1"], p["ln1_g"], p["ln1_b"], alpha, t + "ffn1_fwd")
    sv["x1"] = x1
    hooks.get("after_ffn1", lambda v: None)(x1)
    sec = dict(zip(SECTION_NAMES, _mix_in(x1, p["win"], p["bin"], p["sections"], t + "mix_in")))
    sv.update(sec)
    sv["c2"], c4 = _conv_branch_fwd(sec["cv"], sec["cg"], p["conv_dw_w"], p["conv_dw_b"], p["conv_gn_g"], p["conv_gn_b"], t + "conv_fwd")
    sv["c4"] = c4
    r1 = _short_conv_fwd(sec["rx"], p["rnn_conv_w"], p["rnn_conv_b"], t + "rconv_fwd")
    sv["r1"] = r1
    sv["ra"], sv["ri"], a, uu = _gates_fwd(r1, p["wa"], p["wx"], p["rnn_b_a"], p["rnn_b_x"], p["rnn_lambda"], t + "gates_fwd")
    sv["a"] = a
    h = _scan_fwd(a, uu, t + "scan_fwd")
    sv["h"] = h
    hooks.get("after_scan", lambda v: None)(h)
    sv["yc"], sv["yr"], sv["z2"], x2 = _mixer_out_fwd(c4, h, sec["rg"], sec["gc"], sec["gr"], x1, p["wc"], p["wr"], p["wo"],
                                                      p["ln2_g"], p["ln2_b"], alpha, t + "mixout_fwd")
    sv["x2"] = x2
    hooks.get("after_mixer", lambda v: None)(x2)
    x3, sv["z3"], sv["hg2"], sv["hu2"] = _ffn_fwd(x2, p["wgu2"], p["wd2"], p["ln3_g"], p["ln3_b"], alpha, t + "ffn2_fwd")
    hooks.get("after_layer", lambda v: None)(x3)
    return x3, sv


def _layer_backward(dy, p, sv, alpha, l, before_ffn1=None):
    t = f"l{l}_"
    g, gb = {}, {}
    dx2, df, a_act, dhg, dhu, g["ln3_g"], g["ln3_b"] = _ffn_bwd(dy, sv["z3"], sv["hg2"], sv["hu2"], p["wgu2"], p["wd2"],
                                                                 p["ln3_g"], alpha, t + "ffn2_bwd")
    g["ffn2_w_down"], gb["ffn2_w_down"] = _mm_tn(a_act, df, t + "dwd2")
    g["ffn2_w_gu"], gb["ffn2_w_gu"] = _mm_tn_pair(sv["x2"], dhg, dhu, t + "dwgu2")
    (dz2, dz2b, m_b, dyc, dyr, dgc, dgr, s_gc, s_gr, g["ln2_g"], g["ln2_b"]) = _mixer_out_bwd(
        dx2, sv["z2"], p["ln2_g"], p["wo"], sv["yc"], sv["yr"], sv["gc"], sv["gr"], t + "mixout_bwd")
    g["mix_w_out"], gb["mix_w_out"] = _mm_tn(m_b, dz2b, t + "dwo")
    dc4, dh, drg, q_b, s_rg = _branch_bwd(dyc, dyr, p["wc"], p["wr"], sv["h"], sv["rg"], t + "branch_bwd")
    g["conv_w_proj"], gb["conv_w_proj"] = _mm_tn(sv["c4"], dyc, t + "dwc")
    g["rnn_w_proj"], gb["rnn_w_proj"] = _mm_tn(q_b, dyr, t + "dwr")
    (dcv, dcg, g["conv_dw_w"], g["conv_dw_b"], g["conv_gn_g"], g["conv_gn_b"], s_cv, s_cg) = _conv_branch_bwd(
        dc4, sv["c2"], sv["cv"], sv["cg"], p["conv_dw_w"], p["conv_gn_g"], p["conv_gn_b"], t + "conv_bwd")
    guu, da = _scan_bwd(sv["a"], dh, sv["h"], t + "scan_bwd")
    dr1, dpa, dpx, g["rnn_b_a"], g["rnn_b_x"], g["rnn_lambda"] = _gates_bwd(
        guu, da, sv["ra"], sv["ri"], sv["r1"], p["wa"], p["wx"], p["rnn_lambda"], t + "gates_bwd")
    g["rnn_w_a"], g["rnn_w_x"] = _block_grads(sv["r1"], dpa, dpx, t + "dwax")
    drx, g["rnn_conv_w"], g["rnn_conv_b"], s_rx = _short_conv_bwd(dr1, sv["rx"], p["rnn_conv_w"], t + "rconv_bwd")
    du = {"cv": dcv, "cg": dcg, "rx": drx, "rg": drg, "gc": dgc, "gr": dgr}
    order = ("cv", "cg", "rx", "rg", "gc", "gr")
    pieces = [_mm_tn(du[s], sv["x1"], t + "dwin_" + s) for s in order]
    g["mix_w_in"] = jnp.concatenate([f for f, _ in pieces], axis=0)
    gb["mix_w_in"] = jnp.concatenate([h for _, h in pieces], axis=0)
    g["mix_b_in"] = jnp.concatenate([s_cv, s_cg, s_rx, s_rg, s_gc, s_gr], axis=1)
    dx1 = _mix_dx(dz2, [du[s] for s in order], p["win"], p["sections"], alpha, t + "mix_dx")
    ln1_g = p["ln1_g"] if before_ffn1 is None else p["ln1_g"] + before_ffn1(g, gb)[0:1, 0:1]
    dx0, df, a_act, dhg, dhu, g["ln1_g"], g["ln1_b"] = _ffn_bwd(dx1, sv["z1"], sv["hg1"], sv["hu1"], p["wgu1"], p["wd1"],
                                                                 ln1_g, alpha, t + "ffn1_bwd")
    g["ffn1_w_down"], gb["ffn1_w_down"] = _mm_tn(a_act, df, t + "dwd1")
    g["ffn1_w_gu"], gb["ffn1_w_gu"] = _mm_tn_pair(sv["x0"], dhg, dhu, t + "dwgu1")
    return dx0, g, gb


def _pack_small(arrays, piece_rows):
    flat = jnp.concatenate([a.reshape(-1) for a in arrays])
    total = N_DEV * piece_rows * LANES
    return jnp.pad(flat, (0, total - flat.shape[0])).reshape(N_DEV, piece_rows, LANES)


def _unpack_small(packed, shapes):
    flat = packed.reshape(-1)
    out, off = [], 0
    for shp in shapes:
        n = 1
        for s in shp:
            n *= s
        out.append(flat[off:off + n].reshape(shp))
        off += n
    return out


def kernel(x, ffn1_w_gu, ffn1_w_down, ln1_g, ln1_b, mix_w_in, mix_b_in, conv_dw_w, conv_dw_b, conv_gn_g, conv_gn_b, conv_w_proj, rnn_conv_w, rnn_conv_b, rnn_w_a, rnn_b_a, rnn_w_x, rnn_b_x, rnn_lambda, rnn_w_proj, mix_w_out, ln2_g, ln2_b, ffn2_w_gu, ffn2_w_down, ln3_g, ln3_b, loss_target, m_ffn1_w_gu, m_ffn1_w_down, m_ln1_g, m_ln1_b, m_mix_w_in, m_mix_b_in, m_conv_dw_w, m_conv_dw_b, m_conv_gn_g, m_conv_gn_b, m_conv_w_proj, m_rnn_conv_w, m_rnn_conv_b, m_rnn_w_a, m_rnn_b_a, m_rnn_w_x, m_rnn_b_x, m_rnn_lambda, m_rnn_w_proj, m_mix_w_out, m_ln2_g, m_ln2_b, m_ffn2_w_gu, m_ffn2_w_down, m_ln3_g, m_ln3_b, v_ffn1_w_gu, v_ffn1_w_down, v_ln1_g, v_ln1_b, v_mix_w_in, v_mix_b_in, v_conv_dw_w, v_conv_dw_b, v_conv_gn_g, v_conv_gn_b, v_conv_w_proj, v_rnn_conv_w, v_rnn_conv_b, v_rnn_w_a, v_rnn_b_a, v_rnn_w_x, v_rnn_b_x, v_rnn_lambda, v_rnn_w_proj, v_mix_w_out, v_ln2_g, v_ln2_b, v_ffn2_w_gu, v_ffn2_w_down, v_ln3_g, v_ln3_b):
    args = locals()
    W = {n: args[n] for n in WEIGHT_NAMES}
    M = {n: args["m_" + n] for n in WEIGHT_NAMES}
    V = {n: args["v_" + n] for n in WEIGHT_NAMES}
    depth = ln1_g.shape[0]
    assert depth == 2, "each core of a chip moves one layer's weights and gradients"
    alpha = float((2 * depth) ** 0.25)
    S, D = x.shape[1], x.shape[2]
    F = ffn1_w_down.shape[1] * N_CHIPS
    R = rnn_w_proj.shape[1] * N_CHIPS
    chip = 2 * lax.axis_index("x") + lax.axis_index("y")

    for d in (W, M, V):
        d["mix_w_in"] = jnp.transpose(d["mix_w_in"], (0, 2, 1))

    names = SHARDED_MATS + SHARDED_VECS
    unit_groups = [[(0, "ffn1_w_gu"), (0, "ffn1_w_down")], [(0, "mix_w_in"), (0, "conv_dw_w"), (0, "rnn_conv_w")],
                   [(0, "conv_w_proj"), (0, "rnn_w_proj"), (0, "mix_w_out")], [(0, "ffn2_w_gu"), (0, "ffn2_w_down")],
                   [(1, n) for n in names]]
    order = [u for g in unit_groups for u in g]
    index = {u: i for i, u in enumerate(order)}
    groups = [[index[u] for u in g] for g in unit_groups]
    srcs = [W[n][l].astype(BF16) if n in SHARDED_MATS else W[n][l] for l, n in order]
    by_cols = [n in COL_SHARDED for _, n in order]
    send_sems, recv_sems, src_thru, land_thru = _gather_start(srcs, by_cols, groups, "gather_start")

    sections = ((0, D, F32), (D, D, F32), (2 * D, R, F32), (2 * D + R, R, BF16), (2 * D + 2 * R, D, BF16),
                (3 * D + 2 * R, D, BF16))
    keys = {"ffn1_w_gu": "wgu1", "ffn1_w_down": "wd1", "ffn2_w_gu": "wgu2", "ffn2_w_down": "wd2", "conv_w_proj": "wc",
            "rnn_w_proj": "wr", "mix_w_out": "wo"}
    params = []
    for l in range(depth):
        p = {"wa": _embed_blocks(rnn_w_a[l], f"l{l}_embed_wa"), "wx": _embed_blocks(rnn_w_x[l], f"l{l}_embed_wx")}
        for n in ("ln1_g", "ln1_b", "ln2_g", "ln2_b", "ln3_g", "ln3_b", "conv_dw_b", "conv_gn_g", "conv_gn_b", "rnn_conv_b",
                  "rnn_b_a", "rnn_b_x", "rnn_lambda"):
            p[n] = _row(W[n][l])
        p["bin"] = _row(mix_b_in[l])
        p["sections"] = sections
        params.append(p)

    def wait_group(g, after):
        ids = groups[g]
        landed = _gather_wait([src_thru[i] for i in ids], [land_thru[i] for i in ids], [by_cols[i] for i in ids],
                              send_sems[g], recv_sems[g], after, f"gather_wait{g}")
        for i, full in zip(ids, landed):
            l, n = order[i]
            p = params[l]
            if n not in COL_SHARDED:
                full = full.reshape((N_CHIPS * full.shape[1],) + full.shape[2:])
            if n == "mix_w_in":
                p["win"] = full
            elif n == "rnn_conv_w":
                p[n] = _unshard_cols(landed[ids.index(i)])
            elif n == "conv_dw_w":
                p[n] = full
            else:
                p[keys[n]] = (full[None], 0)

    h = x[0]
    wait_group(0, h)
    saved = []
    hooks = [{"after_ffn1": lambda v: wait_group(1, v), "after_scan": lambda v: wait_group(2, v),
              "after_mixer": lambda v: wait_group(3, v), "after_layer": lambda v: wait_group(4, v)}, {}]
    for l in range(depth):
        h, sv = _layer_forward(h, params[l], alpha, l, hooks[l])
        saved.append(sv)
    loss_part, dy = _loss_head(h, loss_target[0], "loss_head")
    loss = lax.psum(loss_part[0, 0], ("x", "y", "c"))
    def pair_sums(names, g, gb, tag):
        cols = [n in COL_SHARDED for n in names]
        theirs = _sibling_exchange([_half_view(gb[n], bc) for n, bc in zip(names, cols)], cols, "pair_exchange" + tag)
        return cols, [_pair_sum_bf16(_half_view(g[n], bc), t, bc, f"pair_sum{tag}_{n}") for n, bc, t in zip(names, cols, theirs)]

    pending = []

    def start_scatter(layer, names, g, gb, tag):
        cols, sums = pair_sums(names, g, gb, tag)
        in_flight, token = _scatter_start(sums, cols, "scatter_start" + tag)
        pending.append((layer, names, cols, in_flight, tag))
        return token

    early = [n for n in SHARDED_MATS if not n.startswith("ffn1_")]
    late = [n for n in SHARDED_MATS if n.startswith("ffn1_")]
    grads, grads_bf16 = [None] * depth, [None] * depth
    dy, grads[1], grads_bf16[1] = _layer_backward(dy, params[1], saved[1], alpha, 1)
    token = start_scatter(1, SHARDED_MATS, grads[1], grads_bf16[1], "1")
    first = dict(params[0], ln3_g=params[0]["ln3_g"] + token[0:1, 0:1])
    dy, grads[0], grads_bf16[0] = _layer_backward(dy, first, saved[0], alpha, 0,
                                                  before_ffn1=lambda g, gb: start_scatter(0, early, g, gb, "0a"))
    grad_x = dy[None]
    late_cols, late_sums = pair_sums(late, grads[0], grads_bf16[0], "0b")
    late_rem, late_sib = _scatter_grads(late_sums, late_cols, "scatter_grads0b")
    partial = {(0, n): part for n, part in zip(late, zip(late_sums, late_rem, late_sib))}
    for layer, names, cols, in_flight, tag in pending:
        for n, part in zip(names, zip(*_scatter_wait(*in_flight, cols, late_rem[0], "scatter_wait" + tag))):
            partial[(layer, n)] = part

    results = {}
    for l in (1, 0):
        for n in SHARDED_MATS:
            cs, rm, sb = partial[(l, n)]
            if n in COL_SHARDED:
                width = cs.shape[1] // N_CHIPS
                own = lax.dynamic_slice_in_dim(cs, chip * width, width, axis=1)
            else:
                own = lax.dynamic_index_in_dim(cs, chip, axis=0, keepdims=False)
            results[n] = _adamw_sharded(W[n], M[n], V[n], own, sb, rm, l, results.get(n), f"adamw{l}_{n}")
    out_g, out_d, out_m, out_v = {}, {}, {}, {}
    for n in SHARDED_MATS:
        outs = results[n]
        if n == "mix_w_in":
            outs = [jnp.transpose(o, (0, 2, 1)) for o in outs]
        out_g[n], out_d[n], out_m[n], out_v[n] = outs

    small_grads = [jnp.stack([grads[l][n].reshape(W[n].shape[1:] if n not in SHARDED_VECS else
                                                   (W[n].shape[1], W[n].shape[2] * N_CHIPS)) for l in range(depth)])
                   for n in SMALL_NAMES]
    n_small = sum(int(a.size) for a in small_grads)
    piece_rows = -(-n_small // (N_DEV * LANES * SUBLANES)) * SUBLANES
    reduced = _unpack_small(_all_reduce_small(_pack_small(small_grads, piece_rows), "all_reduce_small"),
                            [a.shape for a in small_grads])
    local_g = []
    for n, gr in zip(SMALL_NAMES, reduced):
        if n in SHARDED_VECS:
            width = W[n].shape[2]
            gr = lax.dynamic_slice_in_dim(gr, chip * width, width, axis=2)
        local_g.append(gr)
    deltas, new_m, new_v = _adamw_small([W[n] for n in SMALL_NAMES], [M[n] for n in SMALL_NAMES],
                                        [V[n] for n in SMALL_NAMES], local_g, "adamw_small")
    for n, gr, d_, m_, v_ in zip(SMALL_NAMES, local_g, deltas, new_m, new_v):
        out_g[n], out_d[n], out_m[n], out_v[n] = gr, d_, m_, v_

    return (loss, grad_x, *[out_g[n] for n in WEIGHT_NAMES], *[out_d[n] for n in WEIGHT_NAMES],
            *[out_m[n] for n in WEIGHT_NAMES], *[out_v[n] for n in WEIGHT_NAMES])
```

```python
import functools

import jax
import jax.numpy as jnp
from jax import lax
from jax.experimental import pallas as pl
from jax.experimental.pallas import tpu as pltpu

F32 = jnp.float32
BF16 = jnp.bfloat16
MESH = pl.DeviceIdType.MESH

LN_EPS = 1e-5
CONV_GROUPS = 8
RNN_BLOCKS = 16
RG_LRU_C = 8.0
ADAM_LR = 0.001
ADAM_B1 = 0.9
ADAM_B2 = 0.999
ADAM_EPS = 1e-08
ADAM_WD = 0.01
ADAM_STEP = 10

LANES = 128
SUBLANES = 8
V7X_VMEM_BYTES = 64 << 20
VMEM_LIMIT_CAP = V7X_VMEM_BYTES - (6 << 20)
N_CHIPS = 4
N_DEV = 8
CONV_ROWS = 64
EW_ROWS = 1024
SCAN_SEGMENTS = 32
SCAN_UNROLL = 4


def _cparams(block_bytes):
    limit = min(VMEM_LIMIT_CAP, max(int(block_bytes) + (8 << 20), 24 << 20))
    return pltpu.CompilerParams(vmem_limit_bytes=limit)


def _nbytes(shape, dtype):
    n = 1
    for s in shape:
        n *= s
    return n * jnp.dtype(dtype).itemsize


def _divisor_tile(n, limit, quantum):
    if n <= limit:
        return n
    best = None
    for t in range(quantum, limit + 1, quantum):
        if n % t == 0:
            best = t
    assert best is not None, (n, limit, quantum)
    return best


def _bs(shape, imap, **kw):
    return pl.BlockSpec(shape, imap, **kw)


def _resident(shape):
    nd = len(shape)
    return pl.BlockSpec(shape, lambda *_: (0,) * nd, pipeline_mode=pl.Buffered(1))


def _streamed_call(body, **kw):
    call = pl.pallas_call(body, **kw)
    return lambda *operands: call(*[pltpu.with_memory_space_constraint(o, pltpu.HBM) for o in operands])


def _layer_block(w, block, imap, **kw):
    arr, layer = w
    return arr, pl.BlockSpec((None,) + block, lambda *ids: (layer,) + imap(*ids), **kw)


def _layer_resident(w):
    arr, _ = w
    return _layer_block(w, arr.shape[1:], lambda *_: (0, 0), pipeline_mode=pl.Buffered(1))


def _sigmoid(x):
    return jax.nn.sigmoid(x)


def _dot(a, b):
    return jnp.dot(a, b, preferred_element_type=F32)


def _dot_nt(a, b):
    return lax.dot_general(a, b, (((1,), (1,)), ((), ())), preferred_element_type=F32)


def _dot_tn(a, b):
    return lax.dot_general(a, b, (((0,), (0,)), ((), ())), preferred_element_type=F32)


def _row_mean(z):
    return jnp.mean(z, axis=-1, keepdims=True)


def _lane_mean(z):
    hi = z.astype(BF16)
    lo = (z - hi.astype(F32)).astype(BF16)
    ones = jnp.full((2 * LANES, LANES), 1.0 / LANES, BF16)
    return jnp.dot(jnp.concatenate([hi, lo], axis=-1), ones, preferred_element_type=F32)


def _norm_fwd(z, g, b, mean=_row_mean):
    mu = mean(z)
    xc = z - mu
    var = mean(xc * xc)
    return xc * lax.rsqrt(var + LN_EPS) * g + b


def _norm_bwd(z, g, dy, mean=_row_mean):
    mu = mean(z)
    xc = z - mu
    var = mean(xc * xc)
    rstd = lax.rsqrt(var + LN_EPS)
    xhat = xc * rstd
    dxh = dy * g
    m1 = mean(dxh)
    m2 = mean(dxh * xhat)
    return rstd * (dxh - m1 - xhat * m2), xhat


GELU_K = 0.7978845608028654
GELU_C = 0.044715


def _gelu(x):
    return 0.5 * x * (1.0 + jnp.tanh(GELU_K * (x + GELU_C * x * x * x)))


def _gelu_grad(x):
    t = jnp.tanh(GELU_K * (x + GELU_C * x * x * x))
    return 0.5 * (1.0 + t) + 0.5 * x * (1.0 - t * t) * GELU_K * (1.0 + 3.0 * GELU_C * x * x)


def _softplus(y):
    return jnp.maximum(y, 0.0) + jnp.log1p(jnp.exp(-jnp.abs(y)))


def _neg_expm1(y):
    series = -y * (1.0 + y * (0.5 + y * (1.0 / 6.0 + y * (1.0 / 24.0 + y * (1.0 / 120.0 + y * (1.0 / 720.0))))))
    return jnp.where(y > -0.25, series, 1.0 - jnp.exp(y))


def _colsum(x):
    return jnp.sum(x, axis=0, keepdims=True)


def _shifted_taps(src_ref, base, rows, taps):
    acc = None
    for o, coef in taps:
        term = coef() * src_ref[pl.ds(base + o, rows), :]
        acc = term if acc is None else acc + term
    return acc


def _shifted_corr(src_ref, base, rows, d, acc_ref, offs):
    for k, o in enumerate(offs):
        prod = d * src_ref[pl.ds(base + o, rows), :]
        part = jnp.sum(prod.reshape(rows // SUBLANES, SUBLANES, prod.shape[-1]), axis=0)
        acc_ref[SUBLANES * k:SUBLANES * (k + 1), :] += part


def _front_pad(ktaps):
    return SUBLANES * ((ktaps - 1 + SUBLANES - 1) // SUBLANES)


def _pad_rows(ktaps):
    return _front_pad(ktaps) + SUBLANES


def _ffn_tiles(S, F):
    tm = _divisor_tile(S, 1024, 16)
    tf = _divisor_tile(F, 256, LANES)
    return tm, tf


def _ffn_fwd(x, wgu, wd, g, b, alpha, name):
    S, D = x.shape
    F = wd[0].shape[1]
    tm, tf = _ffn_tiles(S, F)
    nf = F // tf
    wg_arr, wg_spec = _layer_block(wgu, (D, tf), lambda i, j: (0, j))
    wu_arr, wu_spec = _layer_block(wgu, (D, tf), lambda i, j: (0, nf + j))
    wd_arr, wd_spec = _layer_block(wd, (tf, D), lambda i, j: (j, 0))

    def body(x_ref, wg_ref, wu_ref, wd_ref, g_ref, b_ref, y_ref, z_ref, hg_ref, hu_ref, xb_ref, acc_ref):
        j = pl.program_id(1)

        @pl.when(j == 0)
        def _():
            xb_ref[...] = x_ref[...].astype(BF16)
            acc_ref[...] = jnp.zeros_like(acc_ref)

        xb = xb_ref[...]
        hg = _dot(xb, wg_ref[...])
        hu = _dot(xb, wu_ref[...])
        hg_ref[...] = hg
        hu_ref[...] = hu
        a = (hg * _sigmoid(hg) * hu).astype(BF16)
        acc_ref[...] += _dot(a, wd_ref[...])

        @pl.when(j == nf - 1)
        def _():
            z = alpha * x_ref[...] + 0.5 * acc_ref[...]
            z_ref[...] = z
            y_ref[...] = _norm_fwd(z, g_ref[...], b_ref[...])

    blk = 2 * (3 * tm * D * 4 + 2 * tm * tf * 4 + 3 * D * tf * 2) + tm * D * 6 + 3 * tm * tf * 4
    return _streamed_call(
        body, name=name, grid=(S // tm, nf),
        in_specs=[_bs((tm, D), lambda i, j: (i, 0)), wg_spec, wu_spec, wd_spec,
                  _bs((1, D), lambda i, j: (0, 0)), _bs((1, D), lambda i, j: (0, 0))],
        out_specs=[_bs((tm, D), lambda i, j: (i, 0)), _bs((tm, D), lambda i, j: (i, 0)),
                   _bs((tm, tf), lambda i, j: (i, j)), _bs((tm, tf), lambda i, j: (i, j)), _bs((tm, D), lambda i, j: (i, 0))],
        out_shape=[jax.ShapeDtypeStruct((S, D), F32), jax.ShapeDtypeStruct((S, D), F32),
                   jax.ShapeDtypeStruct((S, F), F32), jax.ShapeDtypeStruct((S, F), F32), jax.ShapeDtypeStruct((S, D), BF16)],
        scratch_shapes=[pltpu.VMEM((tm, D), F32)],
        compiler_params=_cparams(blk),
    )(x, wg_arr, wu_arr, wd_arr, g, b)


def _ffn_bwd(dy, z, hg, hu, wgu, wd, g, alpha, name):
    S, D = dy.shape
    F = wd[0].shape[1]
    tm, tf = _ffn_tiles(S, F)
    nf = F // tf
    wg_arr, wg_spec = _layer_block(wgu, (D, tf), lambda i, j: (0, j))
    wu_arr, wu_spec = _layer_block(wgu, (D, tf), lambda i, j: (0, nf + j))
    wd_arr, wd_spec = _layer_block(wd, (tf, D), lambda i, j: (j, 0))

    def body(dy_ref, z_ref, hg_ref, hu_ref, wg_ref, wu_ref, wd_ref, g_ref,
             dx_ref, df_ref, a_ref, dhg_ref, dhu_ref, dg_ref, db_ref, acc_ref):
        i = pl.program_id(0)
        j = pl.program_id(1)

        @pl.when((i == 0) & (j == 0))
        def _():
            dg_ref[...] = jnp.zeros_like(dg_ref)
            db_ref[...] = jnp.zeros_like(db_ref)

        @pl.when(j == 0)
        def _():
            dy_ = dy_ref[...]
            dz, xhat = _norm_bwd(z_ref[...], g_ref[...], dy_)
            dg_ref[...] += _colsum(dy_ * xhat)
            db_ref[...] += _colsum(dy_)
            acc_ref[...] = alpha * dz
            df_ref[...] = (0.5 * dz).astype(BF16)

        da = _dot_nt(df_ref[...], wd_ref[...])
        hg_ = hg_ref[...]
        hu_ = hu_ref[...]
        s = _sigmoid(hg_)
        sl = hg_ * s
        dgate = (da * hu_ * (s * (1.0 + hg_ * (1.0 - s)))).astype(BF16)
        dup = (da * sl).astype(BF16)
        a_ref[...] = (sl * hu_).astype(BF16)
        dhg_ref[...] = dgate
        dhu_ref[...] = dup
        acc_ref[...] += _dot_nt(dgate, wg_ref[...]) + _dot_nt(dup, wu_ref[...])

        @pl.when(j == nf - 1)
        def _():
            dx_ref[...] = acc_ref[...]

    blk = 2 * (2 * tm * D * 4 + tm * D * 2 + 2 * tm * tf * 4 + 3 * tm * tf * 2 + 3 * D * tf * 2) + 3 * tm * D * 4 + 8 * tm * tf * 4
    once = dict(pipeline_mode=pl.Buffered(1))
    return _streamed_call(
        body, name=name, grid=(S // tm, nf),
        in_specs=[_bs((tm, D), lambda i, j: (i, 0), **once), _bs((tm, D), lambda i, j: (i, 0), **once),
                  _bs((tm, tf), lambda i, j: (i, j)), _bs((tm, tf), lambda i, j: (i, j)),
                  wg_spec, wu_spec, wd_spec, _bs((1, D), lambda i, j: (0, 0))],
        out_specs=[_bs((tm, D), lambda i, j: (i, 0)), _bs((tm, D), lambda i, j: (i, 0)),
                   _bs((tm, tf), lambda i, j: (i, j)), _bs((tm, tf), lambda i, j: (i, j)), _bs((tm, tf), lambda i, j: (i, j)),
                   _bs((1, D), lambda i, j: (0, 0)), _bs((1, D), lambda i, j: (0, 0))],
        out_shape=[jax.ShapeDtypeStruct((S, D), F32), jax.ShapeDtypeStruct((S, D), BF16),
                   jax.ShapeDtypeStruct((S, F), BF16), jax.ShapeDtypeStruct((S, F), BF16), jax.ShapeDtypeStruct((S, F), BF16),
                   jax.ShapeDtypeStruct((1, D), F32), jax.ShapeDtypeStruct((1, D), F32)],
        scratch_shapes=[pltpu.VMEM((tm, D), F32)],
        compiler_params=_cparams(blk),
    )(dy, z, hg, hu, wg_arr, wu_arr, wd_arr, g)


def _mm_tn(a, b, name):
    S, M = a.shape
    N = b.shape[1]
    bm = _divisor_tile(M, 1408, LANES)
    bn = _divisor_tile(N, 1408, LANES)
    tk = _divisor_tile(S, 512, 16)
    nk = S // tk

    def body(a_ref, b_ref, o_ref, ob_ref):
        k = pl.program_id(2)

        @pl.when(k == 0)
        def _():
            o_ref[...] = jnp.zeros_like(o_ref)

        o_ref[...] += _dot_tn(a_ref[...].astype(BF16), b_ref[...].astype(BF16))

        @pl.when(k == nk - 1)
        def _():
            ob_ref[...] = o_ref[...].astype(BF16)

    blk = 2 * (tk * bm * a.dtype.itemsize + tk * bn * b.dtype.itemsize + bm * bn * 6) + tk * bm * 4 + bm * bn * 4
    tile = _bs((bm, bn), lambda i, j, k: (i, j))
    return _streamed_call(
        body, name=name, grid=(M // bm, N // bn, nk),
        in_specs=[_bs((tk, bm), lambda i, j, k: (k, i)), _bs((tk, bn), lambda i, j, k: (k, j))],
        out_specs=[tile, tile],
        out_shape=[jax.ShapeDtypeStruct((M, N), F32), jax.ShapeDtypeStruct((M, N), BF16)],
        compiler_params=_cparams(blk),
    )(a, b)


def _mm_tn_pair(a, b0, b1, name):
    S, M = a.shape
    N = b0.shape[1]
    assert b1.shape == b0.shape
    bm = _divisor_tile(M, 1408, LANES)
    bn = _divisor_tile(N, 1408, LANES)
    tk = _divisor_tile(S, 512, 16)
    nb = N // bn
    nk = S // tk

    def body(a_ref, b0_ref, b1_ref, o_ref, ob_ref):
        j = pl.program_id(1)
        k = pl.program_id(2)

        @pl.when(k == 0)
        def _():
            o_ref[...] = jnp.zeros_like(o_ref)

        ab = a_ref[...].astype(BF16)

        @pl.when(j < nb)
        def _():
            o_ref[...] += _dot_tn(ab, b0_ref[...])

        @pl.when(j >= nb)
        def _():
            o_ref[...] += _dot_tn(ab, b1_ref[...])

        @pl.when(k == nk - 1)
        def _():
            ob_ref[...] = o_ref[...].astype(BF16)

    b0_map = lambda i, j, k: (jnp.where(j < nb, k, nk - 1), jnp.minimum(j, nb - 1))
    b1_map = lambda i, j, k: (jnp.where(j >= nb, k, 0), jnp.maximum(j - nb, 0))
    blk = 2 * (tk * bm * a.dtype.itemsize + 2 * tk * bn * 2 + bm * bn * 6) + tk * bm * 4 + bm * bn * 4
    tile = _bs((bm, bn), lambda i, j, k: (i, j))
    return _streamed_call(
        body, name=name, grid=(M // bm, 2 * nb, nk),
        in_specs=[_bs((tk, bm), lambda i, j, k: (k, i)), _bs((tk, bn), b0_map), _bs((tk, bn), b1_map)],
        out_specs=[tile, tile],
        out_shape=[jax.ShapeDtypeStruct((M, 2 * N), F32), jax.ShapeDtypeStruct((M, 2 * N), BF16)],
        compiler_params=_cparams(blk),
    )(a, b0, b1)


def _mix_in(x, wt, bias, sections, name):
    S, D = x.shape
    tm = _divisor_tile(S, 256, 16)
    n = len(sections)

    def body(x_ref, w_ref, b_ref, *o_refs):
        xb = x_ref[...].astype(BF16)
        for (off, width, dtype), o_ref in zip(sections, o_refs):
            o_ref[...] = (_dot_nt(xb, w_ref[off:off + width, :]) + b_ref[:, off:off + width]).astype(dtype)

    total = wt.shape[0]
    blk = 2 * (tm * D * 4 + sum(tm * w * jnp.dtype(dt).itemsize for _, w, dt in sections)) + total * D * 2 + 3 * tm * D * 4
    return _streamed_call(
        body, name=name, grid=(S // tm,),
        in_specs=[_bs((tm, D), lambda i: (i, 0)), _resident((total, D)), _resident((1, total))],
        out_specs=[_bs((tm, w), lambda i: (i, 0)) for _, w, _ in sections],
        out_shape=[jax.ShapeDtypeStruct((S, w), dt) for _, w, dt in sections],
        compiler_params=_cparams(blk),
    )(x, wt, bias)


def _mix_dx(dz, parts, wt, sections, alpha, name):
    S, D = dz.shape
    tm = _divisor_tile(S, 256, 16)
    n = len(parts)

    def body(*refs):
        dz_ref = refs[0]
        p_refs = refs[1:1 + n]
        w_ref = refs[1 + n]
        o_ref = refs[2 + n]
        acc = alpha * dz_ref[...]
        for p_ref, (off, width, _) in zip(p_refs, sections):
            acc = acc + _dot(p_ref[...], w_ref[off:off + width, :])
        o_ref[...] = acc

    widths = [p.shape[1] for p in parts]
    total = wt.shape[0]
    blk = 2 * (2 * tm * D * 4 + sum(tm * w * 2 for w in widths)) + total * D * 2 + 2 * tm * D * 4
    return _streamed_call(
        body, name=name, grid=(S // tm,),
        in_specs=[_bs((tm, D), lambda i: (i, 0))] + [_bs((tm, w), lambda i: (i, 0)) for w in widths]
                 + [_resident((total, D))],
        out_specs=_bs((tm, D), lambda i: (i, 0)),
        out_shape=jax.ShapeDtypeStruct((S, D), F32),
        compiler_params=_cparams(blk),
    )(dz, *parts, wt)


def _conv_branch_fwd(cv, cg, w, b, gg, gb, name):
    S, C = cv.shape
    K = w.shape[0]
    assert C // CONV_GROUPS == LANES
    padf = _front_pad(K)
    R = min(CONV_ROWS, S)
    E = min(EW_ROWS, S)

    def body(cv_ref, cg_ref, w_ref, b_ref, gg_ref, gb_ref, c2_ref, c4_ref, pad_ref):
        pad_ref[0:padf, :] = jnp.zeros((padf, LANES), F32)
        pad_ref[S + padf:S + padf + SUBLANES, :] = jnp.zeros((SUBLANES, LANES), F32)

        def fill(i, carry):
            r = pl.multiple_of(i * E, E)
            pad_ref[pl.ds(r + padf, E), :] = cv_ref[pl.ds(r, E), :] * _sigmoid(cg_ref[pl.ds(r, E), :])
            return carry

        lax.fori_loop(0, S // E, fill, 0)
        taps = [(padf - (K - 1) + k, functools.partial(lambda k: w_ref[k:k + 1, :], k)) for k in range(K)]

        def conv(i, carry):
            r = pl.multiple_of(i * R, R)
            c2_ref[pl.ds(r, R), :] = _shifted_taps(pad_ref, r, R, taps) + b_ref[...]
            return carry

        lax.fori_loop(0, S // R, conv, 0)

        def norm(i, carry):
            r = pl.multiple_of(i * E, E)
            c3 = _norm_fwd(c2_ref[pl.ds(r, E), :], gg_ref[...], gb_ref[...], _lane_mean)
            c4_ref[pl.ds(r, E), :] = (c3 * _sigmoid(c3)).astype(BF16)
            return carry

        lax.fori_loop(0, S // E, norm, 0)

    col = lambda i: (0, i)
    blk = 2 * (3 * S * LANES * 4 + S * LANES * 2) + (S + _pad_rows(K)) * LANES * 4
    return _streamed_call(
        body, name=name, grid=(C // LANES,),
        in_specs=[_bs((S, LANES), col), _bs((S, LANES), col), _bs((K, LANES), col),
                  _bs((1, LANES), col), _bs((1, LANES), col), _bs((1, LANES), col)],
        out_specs=[_bs((S, LANES), col), _bs((S, LANES), col)],
        out_shape=[jax.ShapeDtypeStruct((S, C), F32), jax.ShapeDtypeStruct((S, C), BF16)],
        scratch_shapes=[pltpu.VMEM((S + _pad_rows(K), LANES), F32)],
        compiler_params=_cparams(blk),
    )(cv, cg, w, b, gg, gb)


def _conv_branch_bwd(dc4, c2, cv, cg, w, gg, gb, name):
    S, C = cv.shape
    K = w.shape[0]
    padf = _front_pad(K)
    R = min(CONV_ROWS, S)
    E = min(EW_ROWS, S)

    def body(dc4_ref, c2_ref, cv_ref, cg_ref, w_ref, gg_ref, gb_ref,
             dcv_ref, dcg_ref, dw_ref, dwb_ref, dgg_ref, dgb_ref, scv_ref, scg_ref,
             dpad_ref, cpad_ref, dwacc_ref):
        cpad_ref[0:padf, :] = jnp.zeros((padf, LANES), F32)
        cpad_ref[S + padf:S + padf + SUBLANES, :] = jnp.zeros((SUBLANES, LANES), F32)
        dpad_ref[S:S + padf + SUBLANES, :] = jnp.zeros((padf + SUBLANES, LANES), F32)
        dwacc_ref[...] = jnp.zeros_like(dwacc_ref)
        for ref in (dwb_ref, dgg_ref, dgb_ref, scv_ref, scg_ref):
            ref[...] = jnp.zeros_like(ref)

        def norm_pass(i, carry):
            r = pl.multiple_of(i * E, E)
            g_ = gg_ref[...]
            c2 = c2_ref[pl.ds(r, E), :]
            xc = c2 - _lane_mean(c2)
            rstd = lax.rsqrt(_lane_mean(xc * xc) + LN_EPS)
            xhat = xc * rstd
            c3 = xhat * g_ + gb_ref[...]
            s = _sigmoid(c3)
            dc3 = dc4_ref[pl.ds(r, E), :].astype(F32) * (s * (1.0 + c3 * (1.0 - s)))
            dgg_ref[...] += _colsum(dc3 * xhat)
            dgb_ref[...] += _colsum(dc3)
            dxh = dc3 * g_
            dc2 = rstd * (dxh - _lane_mean(dxh) - xhat * _lane_mean(dxh * xhat))
            dpad_ref[pl.ds(r, E), :] = dc2
            dwb_ref[...] += _colsum(dc2)
            cpad_ref[pl.ds(r + padf, E), :] = cv_ref[pl.ds(r, E), :] * _sigmoid(cg_ref[pl.ds(r, E), :])
            return carry

        lax.fori_loop(0, S // E, norm_pass, 0)
        taps = [(K - 1 - k, functools.partial(lambda k: w_ref[k:k + 1, :], k)) for k in range(K)]
        offs = [padf - (K - 1) + k for k in range(K)]

        def conv_pass(i, carry):
            r = pl.multiple_of(i * R, R)
            dc1 = _shifted_taps(dpad_ref, r, R, taps)
            sg = _sigmoid(cg_ref[pl.ds(r, R), :])
            cv_ = cv_ref[pl.ds(r, R), :]
            dcv = dc1 * sg
            dcg = dc1 * cv_ * sg * (1.0 - sg)
            dcv_ref[pl.ds(r, R), :] = dcv.astype(BF16)
            dcg_ref[pl.ds(r, R), :] = dcg.astype(BF16)
            scv_ref[...] += _colsum(dcv)
            scg_ref[...] += _colsum(dcg)
            _shifted_corr(cpad_ref, r, R, dpad_ref[pl.ds(r, R), :], dwacc_ref, offs)
            return carry

        lax.fori_loop(0, S // R, conv_pass, 0)
        for k in range(K):
            dw_ref[k:k + 1, :] = _colsum(dwacc_ref[SUBLANES * k:SUBLANES * (k + 1), :])

    col = lambda i: (0, i)
    row = jax.ShapeDtypeStruct((1, C), F32)
    blk = 2 * (4 * S * LANES * 4 + 2 * S * LANES * 2) + 2 * (S + _pad_rows(K)) * LANES * 4
    return _streamed_call(
        body, name=name, grid=(C // LANES,),
        in_specs=[_bs((S, LANES), col)] * 4 + [_bs((K, LANES), col), _bs((1, LANES), col), _bs((1, LANES), col)],
        out_specs=[_bs((S, LANES), col), _bs((S, LANES), col), _bs((K, LANES), col)] + [_bs((1, LANES), col)] * 5,
        out_shape=[jax.ShapeDtypeStruct((S, C), BF16), jax.ShapeDtypeStruct((S, C), BF16),
                   jax.ShapeDtypeStruct((K, C), F32), row, row, row, row, row],
        scratch_shapes=[pltpu.VMEM((S + _pad_rows(K), LANES), F32), pltpu.VMEM((S + _pad_rows(K), LANES), F32),
                        pltpu.VMEM((SUBLANES * K, LANES), F32)],
        compiler_params=_cparams(blk),
    )(dc4, c2, cv, cg, w, gg, gb)


def _short_conv_fwd(xin, w, b, name):
    S, C = xin.shape
    K = w.shape[0]
    padf = _front_pad(K)
    R = min(CONV_ROWS, S)
    E = min(EW_ROWS, S)

    def body(x_ref, w_ref, b_ref, o_ref, pad_ref):
        pad_ref[0:padf, :] = jnp.zeros((padf, LANES), F32)
        pad_ref[S + padf:S + padf + SUBLANES, :] = jnp.zeros((SUBLANES, LANES), F32)

        def fill(i, carry):
            r = pl.multiple_of(i * E, E)
            pad_ref[pl.ds(r + padf, E), :] = x_ref[pl.ds(r, E), :]
            return carry

        lax.fori_loop(0, S // E, fill, 0)
        taps = [(padf - (K - 1) + k, functools.partial(lambda k: w_ref[k:k + 1, :], k)) for k in range(K)]

        def conv(i, carry):
            r = pl.multiple_of(i * R, R)
            o_ref[pl.ds(r, R), :] = _shifted_taps(pad_ref, r, R, taps) + b_ref[...]
            return carry

        lax.fori_loop(0, S // R, conv, 0)

    col = lambda i: (0, i)
    blk = 2 * (2 * S * LANES * 4) + (S + _pad_rows(K)) * LANES * 4
    return _streamed_call(
        body, name=name, grid=(C // LANES,),
        in_specs=[_bs((S, LANES), col), _bs((K, LANES), col), _bs((1, LANES), col)],
        out_specs=_bs((S, LANES), col),
        out_shape=jax.ShapeDtypeStruct((S, C), F32),
        scratch_shapes=[pltpu.VMEM((S + _pad_rows(K), LANES), F32)],
        compiler_params=_cparams(blk),
    )(xin, w, b)


def _short_conv_bwd(dy, xin, w, name):
    S, C = xin.shape
    K = w.shape[0]
    padf = _front_pad(K)
    R = min(CONV_ROWS, S)
    E = min(EW_ROWS, S)

    def body(dy_ref, x_ref, w_ref, dx_ref, dw_ref, db_ref, sx_ref, dpad_ref, xpad_ref, dwacc_ref):
        xpad_ref[0:padf, :] = jnp.zeros((padf, LANES), F32)
        xpad_ref[S + padf:S + padf + SUBLANES, :] = jnp.zeros((SUBLANES, LANES), F32)
        dpad_ref[S:S + padf + SUBLANES, :] = jnp.zeros((padf + SUBLANES, LANES), F32)
        dwacc_ref[...] = jnp.zeros_like(dwacc_ref)
        db_ref[...] = jnp.zeros_like(db_ref)
        sx_ref[...] = jnp.zeros_like(sx_ref)

        def fill(i, carry):
            r = pl.multiple_of(i * E, E)
            d = dy_ref[pl.ds(r, E), :]
            dpad_ref[pl.ds(r, E), :] = d
            db_ref[...] += _colsum(d)
            xpad_ref[pl.ds(r + padf, E), :] = x_ref[pl.ds(r, E), :]
            return carry

        lax.fori_loop(0, S // E, fill, 0)
        taps = [(K - 1 - k, functools.partial(lambda k: w_ref[k:k + 1, :], k)) for k in range(K)]
        offs = [padf - (K - 1) + k for k in range(K)]

        def conv_pass(i, carry):
            r = pl.multiple_of(i * R, R)
            dx = _shifted_taps(dpad_ref, r, R, taps)
            dx_ref[pl.ds(r, R), :] = dx.astype(BF16)
            sx_ref[...] += _colsum(dx)
            _shifted_corr(xpad_ref, r, R, dpad_ref[pl.ds(r, R), :], dwacc_ref, offs)
            return carry

        lax.fori_loop(0, S // R, conv_pass, 0)
        for k in range(K):
            dw_ref[k:k + 1, :] = _colsum(dwacc_ref[SUBLANES * k:SUBLANES * (k + 1), :])

    col = lambda i: (0, i)
    row = jax.ShapeDtypeStruct((1, C), F32)
    blk = 2 * (2 * S * LANES * 4 + S * LANES * 2) + 2 * (S + _pad_rows(K)) * LANES * 4
    return _streamed_call(
        body, name=name, grid=(C // LANES,),
        in_specs=[_bs((S, LANES), col), _bs((S, LANES), col), _bs((K, LANES), col)],
        out_specs=[_bs((S, LANES), col), _bs((K, LANES), col), _bs((1, LANES), col), _bs((1, LANES), col)],
        out_shape=[jax.ShapeDtypeStruct((S, C), BF16), jax.ShapeDtypeStruct((K, C), F32), row, row],
        scratch_shapes=[pltpu.VMEM((S + _pad_rows(K), LANES), F32), pltpu.VMEM((S + _pad_rows(K), LANES), F32),
                        pltpu.VMEM((SUBLANES * K, LANES), F32)],
        compiler_params=_cparams(blk),
    )(dy, xin, w)


def _band_panels(width, block):
    assert width % LANES == 0 and block <= LANES
    panels = []
    for c0 in range(0, width, 2 * LANES):
        c1 = min(width, c0 + 2 * LANES)
        r0 = (c0 // block) * block // LANES * LANES
        r1 = min(width, -(-(-(-c1 // block) * block) // LANES) * LANES)
        panels.append((r0, r1, c0, c1))
    return panels


def _gates_fwd(r1, wa, wx, ba, bx, lam, name):
    S, R = r1.shape
    tm = _divisor_tile(S, 256, 16)
    panels = _band_panels(R, R // RNN_BLOCKS)

    def body(r1_ref, wa_ref, wx_ref, ba_ref, bx_ref, lam_ref, ra_ref, ri_ref, a_ref, uu_ref):
        for r0, r1e, c0, c1 in panels:
            rb = r1_ref[:, r0:r1e].astype(BF16)
            ra = _sigmoid(_dot(rb, wa_ref[r0:r1e, c0:c1]) + ba_ref[:, c0:c1])
            ri = _sigmoid(_dot(rb, wx_ref[r0:r1e, c0:c1]) + bx_ref[:, c0:c1])
            log_a = -RG_LRU_C * ra * _softplus(-lam_ref[:, c0:c1])
            ra_ref[:, c0:c1] = ra
            ri_ref[:, c0:c1] = ri
            a_ref[:, c0:c1] = jnp.exp(log_a)
            uu_ref[:, c0:c1] = jnp.sqrt(_neg_expm1(2.0 * log_a)) * (ri * r1_ref[:, c0:c1])

    blk = 2 * (5 * tm * R * 4) + 2 * R * R * 2 + 6 * tm * R * 4
    tile = _bs((tm, R), lambda i: (i, 0))
    return _streamed_call(
        body, name=name, grid=(S // tm,),
        in_specs=[tile, _resident((R, R)), _resident((R, R)), _resident((1, R)), _resident((1, R)), _resident((1, R))],
        out_specs=[tile] * 4,
        out_shape=[jax.ShapeDtypeStruct((S, R), F32)] * 4,
        compiler_params=_cparams(blk),
    )(r1, wa, wx, ba, bx, lam)


def _gates_bwd(guu, da, ra, ri, r1, wa, wx, lam, name):
    S, R = r1.shape
    tm = _divisor_tile(S, 256, 16)
    nsteps = S // tm
    panels = _band_panels(R, R // RNN_BLOCKS)

    def body(g_ref, da_ref, ra_ref, ri_ref, r1_ref, wa_ref, wx_ref, lam_ref,
             dr1_ref, dpa_ref, dpx_ref, dba_ref, dbx_ref, dlam_ref):
        i = pl.program_id(0)

        @pl.when(i == 0)
        def _():
            dba_ref[...] = jnp.zeros_like(dba_ref)
            dbx_ref[...] = jnp.zeros_like(dbx_ref)
            dlam_ref[...] = jnp.zeros_like(dlam_ref)

        g = g_ref[...]
        ra = ra_ref[...]
        ri = ri_ref[...]
        r1_ = r1_ref[...]
        sp = _softplus(-lam_ref[...])
        log_a = -RG_LRU_C * ra * sp
        a = jnp.exp(log_a)
        mult = jnp.sqrt(_neg_expm1(2.0 * log_a))
        d_ri = g * mult * r1_
        dr1 = g * mult * ri
        dmult = g * ri * r1_
        dlog_a = da_ref[...] * a - dmult * (a * a) / mult
        dra = dlog_a * (-RG_LRU_C * sp)
        dlam_ref[...] += _colsum(dlog_a * (-RG_LRU_C * ra))
        dpa = dra * ra * (1.0 - ra)
        dpx = d_ri * ri * (1.0 - ri)
        dba_ref[...] += _colsum(dpa)
        dbx_ref[...] += _colsum(dpx)
        dpa_b = dpa.astype(BF16)
        dpx_b = dpx.astype(BF16)
        dpa_ref[...] = dpa_b
        dpx_ref[...] = dpx_b
        dr1_ref[...] = dr1
        for k0, k1, c0, c1 in panels:
            dr1_ref[:, c0:c1] += (_dot_nt(dpa_ref[:, k0:k1], wa_ref[c0:c1, k0:k1])
                                  + _dot_nt(dpx_ref[:, k0:k1], wx_ref[c0:c1, k0:k1]))

        @pl.when(i == nsteps - 1)
        def _():
            dlam_ref[...] = dlam_ref[...] * (-_sigmoid(-lam_ref[...]))

    blk = 2 * (6 * tm * R * 4 + 2 * tm * R * 2) + 2 * R * R * 2 + 10 * tm * R * 4
    tile = _bs((tm, R), lambda i: (i, 0))
    rowspec = _bs((1, R), lambda i: (0, 0))
    row = jax.ShapeDtypeStruct((1, R), F32)
    return _streamed_call(
        body, name=name, grid=(nsteps,),
        in_specs=[tile] * 5 + [_resident((R, R)), _resident((R, R)), _resident((1, R))],
        out_specs=[tile, tile, tile, rowspec, rowspec, rowspec],
        out_shape=[jax.ShapeDtypeStruct((S, R), F32), jax.ShapeDtypeStruct((S, R), BF16), jax.ShapeDtypeStruct((S, R), BF16),
                   row, row, row],
        compiler_params=_cparams(blk),
    )(guu, da, ra, ri, r1, wa, wx, lam)


def _embed_blocks(w, name):
    H, bk, _ = w.shape

    def body(w_ref, o_ref):
        o_ref[...] = jnp.zeros_like(o_ref)
        for h in range(H):
            o_ref[bk * h:bk * (h + 1), bk * h:bk * (h + 1)] = w_ref[h].astype(BF16)

    return pl.pallas_call(body, name=name, out_shape=jax.ShapeDtypeStruct((H * bk, H * bk), BF16),
                          compiler_params=_cparams(3 * H * bk * H * bk * 2))(w)


def _block_grads(r1, dpa, dpx, name):
    S, R = r1.shape
    bk = R // RNN_BLOCKS
    tk = _divisor_tile(S, 512, 16)
    nsteps = S // tk
    panels = _band_panels(R, bk)

    def body(r1_ref, dpa_ref, dpx_ref, ga_ref, gx_ref, acca_ref, accx_ref):
        k = pl.program_id(0)

        @pl.when(k == 0)
        def _():
            acca_ref[...] = jnp.zeros_like(acca_ref)
            accx_ref[...] = jnp.zeros_like(accx_ref)

        for k0, k1, c0, c1 in panels:
            rb = r1_ref[:, k0:k1].astype(BF16)
            acca_ref[k0:k1, c0:c1] += _dot_tn(rb, dpa_ref[:, c0:c1])
            accx_ref[k0:k1, c0:c1] += _dot_tn(rb, dpx_ref[:, c0:c1])

        @pl.when(k == nsteps - 1)
        def _():
            for h in range(RNN_BLOCKS):
                ga_ref[h] = acca_ref[bk * h:bk * (h + 1), bk * h:bk * (h + 1)]
                gx_ref[h] = accx_ref[bk * h:bk * (h + 1), bk * h:bk * (h + 1)]

    tile = lambda: _bs((tk, R), lambda k: (k, 0))
    out = _bs((RNN_BLOCKS, bk, bk), lambda k: (0, 0, 0))
    sds = jax.ShapeDtypeStruct((RNN_BLOCKS, bk, bk), F32)
    return _streamed_call(
        body, name=name, grid=(nsteps,),
        in_specs=[tile(), tile(), tile()], out_specs=[out, out], out_shape=[sds, sds],
        scratch_shapes=[pltpu.VMEM((R, R), F32), pltpu.VMEM((R, R), F32)],
        compiler_params=_cparams(2 * (tk * R * 8) + 2 * R * R * 4 + 4 * tk * R * 4),
    )(r1, dpa, dpx)


def _scan_geometry(S):
    nseg = SCAN_SEGMENTS if S % (SCAN_SEGMENTS * SUBLANES) == 0 else SUBLANES
    return nseg, S // nseg


def _steps(n, step, init):
    u = SCAN_UNROLL

    def trip(t, carry):
        for k in range(u):
            carry = step(t * u + k, carry)
        return carry

    carry = lax.fori_loop(0, n // u, trip, init)
    for j in range(n - n % u, n):
        carry = step(j, carry)
    return carry


def _scan_fwd(a, u, name):
    S, C = a.shape
    nseg, L = _scan_geometry(S)
    T = min(SUBLANES, L)

    def body(a3, u3, h3, ta_ref, tu_ref, e_ref, p_ref, init_ref):

        def to_steps(i, carry):
            j0 = pl.multiple_of(i * T, T)
            ta_ref[pl.ds(j0, T)] = jnp.swapaxes(a3[:, pl.ds(j0, T), :], 0, 1)
            tu_ref[pl.ds(j0, T)] = jnp.swapaxes(u3[:, pl.ds(j0, T), :], 0, 1)
            return carry

        lax.fori_loop(0, L // T, to_steps, 0)

        def run1(j, carry):
            hs, ps = carry
            aj = ta_ref[j]
            return aj * hs + tu_ref[j], aj * ps

        e_ref[...], p_ref[...] = _steps(L, run1, (jnp.zeros((nseg, LANES), F32), jnp.ones((nseg, LANES), F32)))
        init_ref[0:1, :] = jnp.zeros((1, LANES), F32)
        for s in range(1, nseg):
            init_ref[s:s + 1, :] = e_ref[s - 1:s, :] + p_ref[s - 1:s, :] * init_ref[s - 1:s, :]

        def run2(j, hs):
            hs = ta_ref[j] * hs + tu_ref[j]
            tu_ref[j] = hs
            return hs

        _steps(L, run2, init_ref[...])

        def from_steps(i, carry):
            j0 = pl.multiple_of(i * T, T)
            h3[:, pl.ds(j0, T), :] = jnp.swapaxes(tu_ref[pl.ds(j0, T)], 0, 1)
            return carry

        lax.fori_loop(0, L // T, from_steps, 0)

    seg_block = _bs((nseg, L, LANES), lambda i: (0, 0, i))
    blk = 2 * (3 * S * LANES * 4) + 2 * S * LANES * 4
    return _streamed_call(
        body, name=name, grid=(C // LANES,),
        in_specs=[seg_block, seg_block],
        out_specs=seg_block,
        out_shape=jax.ShapeDtypeStruct((nseg, L, C), F32),
        scratch_shapes=[pltpu.VMEM((L, nseg, LANES), F32)] * 2 + [pltpu.VMEM((nseg, LANES), F32)] * 3,
        compiler_params=_cparams(blk),
    )(a.reshape(nseg, L, C), u.reshape(nseg, L, C)).reshape(S, C)


def _scan_bwd(a, dh, h, name):
    S, C = a.shape
    nseg, L = _scan_geometry(S)
    T = min(SUBLANES, L)
    assert L >= 2

    def body(a3, d3, h3, g3, da3, ta_ref, td_ref, th_ref, e_ref, p_ref, init_ref):

        def to_steps(i, carry):
            j0 = pl.multiple_of(i * T, T)
            for src, dst in ((a3, ta_ref), (d3, td_ref), (h3, th_ref)):
                dst[pl.ds(j0, T)] = jnp.swapaxes(src[:, pl.ds(j0, T), :], 0, 1)
            return carry

        lax.fori_loop(0, L // T, to_steps, 0)
        seg = lax.broadcasted_iota(jnp.int32, (nseg, LANES), 0)
        b_last = jnp.where(seg == nseg - 1, 0.0, pltpu.roll(ta_ref[0], nseg - 1, axis=0))
        h_first = jnp.where(seg == 0, 0.0, pltpu.roll(th_ref[L - 1], 1, axis=0))

        def run1(jj, carry):
            gs, ps = carry
            j = L - 2 - jj
            bj = ta_ref[j + 1]
            return bj * gs + td_ref[j], bj * ps

        e_ref[...], p_ref[...] = _steps(L - 1, run1, (td_ref[L - 1], b_last))
        init_ref[nseg - 1:nseg, :] = jnp.zeros((1, LANES), F32)
        for s in range(nseg - 2, -1, -1):
            init_ref[s:s + 1, :] = e_ref[s + 1:s + 2, :] + p_ref[s + 1:s + 2, :] * init_ref[s + 1:s + 2, :]

        gs = b_last * init_ref[...] + td_ref[L - 1]
        td_ref[L - 1] = gs
        th_ref[L - 1] = gs * th_ref[L - 2]

        def run2(jj, gs):
            j = L - 2 - jj
            gs = ta_ref[j + 1] * gs + td_ref[j]
            td_ref[j] = gs
            th_ref[j] = gs * th_ref[j - 1]
            return gs

        gs = _steps(L - 2, run2, gs)
        gs = ta_ref[1] * gs + td_ref[0]
        td_ref[0] = gs
        th_ref[0] = gs * h_first

        def from_steps(i, carry):
            j0 = pl.multiple_of(i * T, T)
            g3[:, pl.ds(j0, T), :] = jnp.swapaxes(td_ref[pl.ds(j0, T)], 0, 1)
            da3[:, pl.ds(j0, T), :] = jnp.swapaxes(th_ref[pl.ds(j0, T)], 0, 1)
            return carry

        lax.fori_loop(0, L // T, from_steps, 0)

    seg_block = _bs((nseg, L, LANES), lambda i: (0, 0, i))
    blk = 2 * (5 * S * LANES * 4) + 3 * S * LANES * 4
    g, da = _streamed_call(
        body, name=name, grid=(C // LANES,),
        in_specs=[seg_block] * 3,
        out_specs=[seg_block] * 2,
        out_shape=[jax.ShapeDtypeStruct((nseg, L, C), F32)] * 2,
        scratch_shapes=[pltpu.VMEM((L, nseg, LANES), F32)] * 3 + [pltpu.VMEM((nseg, LANES), F32)] * 3,
        compiler_params=_cparams(blk),
    )(a.reshape(nseg, L, C), dh.reshape(nseg, L, C), h.reshape(nseg, L, C))
    return g.reshape(S, C), da.reshape(S, C)


def _mixer_out_fwd(c4, h, rg, gc, gr, x1, wc, wr, wo, g, b, alpha, name):
    S, D = x1.shape
    R = h.shape[1]
    tm = _divisor_tile(S, 256, 16)

    def body(c4_ref, h_ref, rg_ref, gc_ref, gr_ref, x_ref, wc_ref, wr_ref, wo_ref, g_ref, b_ref,
             yc_ref, yr_ref, z_ref, y_ref):
        yc = _dot(c4_ref[...], wc_ref[...])
        q = (h_ref[...] * _gelu(rg_ref[...].astype(F32))).astype(BF16)
        yr = _dot(q, wr_ref[...])
        yc_ref[...] = yc.astype(BF16)
        yr_ref[...] = yr.astype(BF16)
        m = (_sigmoid(gc_ref[...].astype(F32)) * yc + _sigmoid(gr_ref[...].astype(F32)) * yr).astype(BF16)
        z = alpha * x_ref[...] + _dot(m, wo_ref[...])
        z_ref[...] = z
        y_ref[...] = _norm_fwd(z, g_ref[...], b_ref[...])

    blk = 2 * (tm * D * 2 + 2 * tm * R * 4 + 7 * tm * D * 4) + (2 * D * D + R * D) * 2 + 6 * tm * D * 4
    td = _bs((tm, D), lambda i: (i, 0))
    tr = _bs((tm, R), lambda i: (i, 0))
    return _streamed_call(
        body, name=name, grid=(S // tm,),
        in_specs=[td, tr, tr, td, td, td, _layer_resident(wc)[1], _layer_resident(wr)[1], _layer_resident(wo)[1],
                  _resident((1, D)), _resident((1, D))],
        out_specs=[td] * 4,
        out_shape=[jax.ShapeDtypeStruct((S, D), BF16)] * 2 + [jax.ShapeDtypeStruct((S, D), F32)] * 2,
        compiler_params=_cparams(blk),
    )(c4, h, rg, gc, gr, x1, wc[0], wr[0], wo[0], g, b)


def _mixer_out_bwd(dy, z, g, wo, yc, yr, gc, gr, name):
    S, D = dy.shape
    tm = _divisor_tile(S, 256, 16)

    def body(dy_ref, z_ref, g_ref, wo_ref, yc_ref, yr_ref, gc_ref, gr_ref,
             dz_ref, dzb_ref, m_ref, dyc_ref, dyr_ref, dgc_ref, dgr_ref, sgc_ref, sgr_ref, dg_ref, db_ref):
        @pl.when(pl.program_id(0) == 0)
        def _():
            for ref in (sgc_ref, sgr_ref, dg_ref, db_ref):
                ref[...] = jnp.zeros_like(ref)

        dy_ = dy_ref[...]
        dz, xhat = _norm_bwd(z_ref[...], g_ref[...], dy_)
        dg_ref[...] += _colsum(dy_ * xhat)
        db_ref[...] += _colsum(dy_)
        dz_ref[...] = dz
        dzb = dz.astype(BF16)
        dzb_ref[...] = dzb
        dm = _dot_nt(dzb, wo_ref[...])
        yc = yc_ref[...].astype(F32)
        yr = yr_ref[...].astype(F32)
        sc = _sigmoid(gc_ref[...].astype(F32))
        sr = _sigmoid(gr_ref[...].astype(F32))
        m_ref[...] = (sc * yc + sr * yr).astype(BF16)
        dyc_ref[...] = (dm * sc).astype(BF16)
        dyr_ref[...] = (dm * sr).astype(BF16)
        dgc = dm * yc * sc * (1.0 - sc)
        dgr = dm * yr * sr * (1.0 - sr)
        dgc_ref[...] = dgc.astype(BF16)
        dgr_ref[...] = dgr.astype(BF16)
        sgc_ref[...] += _colsum(dgc)
        sgr_ref[...] += _colsum(dgr)

    blk = 2 * (7 * tm * D * 4 + 6 * tm * D * 2) + D * D * 2 + 8 * tm * D * 4
    td = _bs((tm, D), lambda i: (i, 0))
    rowspec = _bs((1, D), lambda i: (0, 0))
    row = jax.ShapeDtypeStruct((1, D), F32)
    bfd = jax.ShapeDtypeStruct((S, D), BF16)
    return _streamed_call(
        body, name=name, grid=(S // tm,),
        in_specs=[td, td, _resident((1, D)), _layer_resident(wo)[1], td, td, td, td],
        out_specs=[td] * 7 + [rowspec] * 4,
        out_shape=[jax.ShapeDtypeStruct((S, D), F32), bfd, bfd, bfd, bfd, bfd, bfd, row, row, row, row],
        compiler_params=_cparams(blk),
    )(dy, z, g, wo[0], yc, yr, gc, gr)


def _branch_bwd(dyc, dyr, wc, wr, h, rg, name):
    S, D = dyc.shape
    R = h.shape[1]
    tm = _divisor_tile(S, 256, 16)

    def body(dyc_ref, dyr_ref, wc_ref, wr_ref, h_ref, rg_ref, dc4_ref, dh_ref, drg_ref, q_ref, srg_ref):
        @pl.when(pl.program_id(0) == 0)
        def _():
            srg_ref[...] = jnp.zeros_like(srg_ref)

        dc4_ref[...] = _dot_nt(dyc_ref[...], wc_ref[...]).astype(BF16)
        dq = _dot_nt(dyr_ref[...], wr_ref[...])
        h_ = h_ref[...]
        rg_ = rg_ref[...].astype(F32)
        ge = _gelu(rg_)
        dh_ref[...] = dq * ge
        drg = dq * h_ * _gelu_grad(rg_)
        drg_ref[...] = drg.astype(BF16)
        srg_ref[...] += _colsum(drg)
        q_ref[...] = (h_ * ge).astype(BF16)

    blk = 2 * (2 * tm * D * 2 + tm * D * 4 + 3 * tm * R * 4 + 2 * tm * R * 2) + (D * D + R * D) * 2 + 6 * tm * R * 4
    td = _bs((tm, D), lambda i: (i, 0))
    tr = _bs((tm, R), lambda i: (i, 0))
    return _streamed_call(
        body, name=name, grid=(S // tm,),
        in_specs=[td, td, _layer_resident(wc)[1], _layer_resident(wr)[1], tr, tr],
        out_specs=[td, tr, tr, tr, _bs((1, R), lambda i: (0, 0))],
        out_shape=[jax.ShapeDtypeStruct((S, D), BF16), jax.ShapeDtypeStruct((S, R), F32), jax.ShapeDtypeStruct((S, R), BF16),
                   jax.ShapeDtypeStruct((S, R), BF16), jax.ShapeDtypeStruct((1, R), F32)],
        compiler_params=_cparams(blk),
    )(dyc, dyr, wc[0], wr[0], h, rg)


def _loss_head(y, target, name):
    S, D = y.shape
    tm = _divisor_tile(S, 512, 16)
    nsteps = S // tm

    def body(y_ref, t_ref, loss_ref, dy_ref, acc_ref):
        i = pl.program_id(0)

        @pl.when(i == 0)
        def _():
            acc_ref[...] = jnp.zeros_like(acc_ref)

        err = y_ref[...] - t_ref[...]
        dy_ref[...] = err * (1.0 / D)
        acc_ref[...] += _colsum(err * err)

        @pl.when(i == nsteps - 1)
        def _():
            loss_ref[...] = jnp.sum(acc_ref[...], axis=-1, keepdims=True) * (0.5 / D)

    td = _bs((tm, D), lambda i: (i, 0))
    return _streamed_call(
        body, name=name, grid=(nsteps,),
        in_specs=[td, td],
        out_specs=[_bs((1, 1), lambda i: (0, 0)), td],
        out_shape=[jax.ShapeDtypeStruct((1, 1), F32), jax.ShapeDtypeStruct((S, D), F32)],
        scratch_shapes=[pltpu.VMEM((1, D), F32)],
        compiler_params=_cparams(2 * 3 * tm * D * 4),
    )(y, target)


def _adamw_math(w, g, m, v):
    m = ADAM_B1 * m + (1.0 - ADAM_B1) * g
    v = ADAM_B2 * v + (1.0 - ADAM_B2) * (g * g)
    m_hat = m / (1.0 - ADAM_B1 ** ADAM_STEP)
    v_hat = v / (1.0 - ADAM_B2 ** ADAM_STEP)
    delta = -ADAM_LR * (m_hat / (jnp.sqrt(v_hat) + ADAM_EPS) + ADAM_WD * w)
    return delta, m, v


def _adamw_sharded(w, m, v, own, sib, rem, layer, filled, name):
    layers, r, c = w.shape
    r2 = r // 2
    tr = _divisor_tile(r2, max(16, (1 << 20) // (4 * c) // 16 * 16), 16)
    n_out = 4

    def body(w_ref, m_ref, v_ref, own_ref, sib_ref, rem_ref, *rest):
        g_ref, d_ref, nm_ref, nv_ref = rest[-n_out:]
        mine = pl.program_id(0) == lax.axis_index("c")
        g = jnp.where(mine, own_ref[...], sib_ref[...]).astype(F32)
        for j in range(N_CHIPS - 1):
            g = g + rem_ref[j].astype(F32)
        delta, nm, nv = _adamw_math(w_ref[...], g, m_ref[...], v_ref[...])
        g_ref[...] = g
        d_ref[...] = delta
        nm_ref[...] = nm
        nv_ref[...] = nv

    halves = lambda a: a.reshape(layers, 2, r2, c)
    tile = _bs((None, None, tr, c), lambda h, i: (layer, h, i, 0))
    flat = _bs((tr, c), lambda h, i: (i, 0))
    sds = jax.ShapeDtypeStruct((layers, 2, r2, c), F32)
    passed = [] if filled is None else [halves(a) for a in filled]
    outs = _streamed_call(
        body, name=name, grid=(2, r2 // tr),
        in_specs=[tile, tile, tile, flat, flat, _bs((N_CHIPS - 1, None, tr, c), lambda h, i: (0, h, i, 0))] + [ANY] * len(passed),
        out_specs=[tile] * n_out,
        out_shape=[sds] * n_out,
        input_output_aliases={6 + k: k for k in range(len(passed))},
        compiler_params=_cparams(2 * (7 * tr * c * 4 + (N_CHIPS + 1) * tr * c * 2) + 6 * tr * c * 4),
    )(halves(w), halves(m), halves(v), own, sib, rem, *passed)
    return [o.reshape(layers, r, c) for o in outs]


def _adamw_small(ws, ms, vs, gs, name):
    n = len(ws)

    def body(*refs):
        w_refs, m_refs, v_refs, g_refs = (refs[k * n:(k + 1) * n] for k in range(4))
        d_refs, nm_refs, nv_refs = (refs[(4 + k) * n:(5 + k) * n] for k in range(3))
        for i in range(n):
            delta, nm, nv = _adamw_math(w_refs[i][...], g_refs[i][...], m_refs[i][...], v_refs[i][...])
            d_refs[i][...] = delta
            nm_refs[i][...] = nm
            nv_refs[i][...] = nv

    sds = [jax.ShapeDtypeStruct(w.shape, F32) for w in ws]
    total = sum(_nbytes(w.shape, F32) for w in ws)
    outs = pl.pallas_call(body, name=name, out_shape=sds * 3, compiler_params=_cparams(16 * total))(*ws, *ms, *vs, *gs)
    return outs[:n], outs[n:2 * n], outs[2 * n:]


def _half_view(g, by_cols):
    rows, cols = g.shape
    if by_cols:
        return g.reshape(2, rows // 2, cols)
    return g.reshape(N_CHIPS, 2, rows // (2 * N_CHIPS), cols)


def _pair_sum_bf16(view, theirs, by_cols, name):
    rows, c = theirs.shape[-2:]
    tr = _divisor_tile(rows, max(16, (1 << 20) // (4 * c) // 16 * 16), 16)

    def body(half_ref, g_ref, t_ref, o_ref):
        o_ref[...] = (g_ref[...] + t_ref[...].astype(F32)).astype(BF16)

    if by_cols:
        grid = (rows // tr,)
        mine = _bs((None, tr, c), lambda i, half: (half[0], i, 0))
        tile = _bs((tr, c), lambda i, half: (i, 0))
    else:
        grid = (N_CHIPS, rows // tr)
        mine = _bs((None, None, tr, c), lambda k, i, half: (k, half[0], i, 0))
        tile = _bs((None, tr, c), lambda k, i, half: (k, i, 0))
    return pl.pallas_call(
        body, name=name,
        grid_spec=pltpu.PrefetchScalarGridSpec(num_scalar_prefetch=1, grid=grid, in_specs=[mine, tile], out_specs=tile),
        out_shape=jax.ShapeDtypeStruct(theirs.shape, BF16),
        compiler_params=_cparams(2 * 3 * tr * c * 4),
    )(lax.axis_index("c").astype(jnp.int32).reshape(1), pltpu.with_memory_space_constraint(view, pltpu.HBM),
      pltpu.with_memory_space_constraint(theirs, pltpu.HBM))


ANY = pl.BlockSpec(memory_space=pl.ANY)


def _mesh_position():
    return lax.axis_index("x"), lax.axis_index("y"), lax.axis_index("c")


def _other_chips():
    x, y, c = _mesh_position()
    chips = [(1 - x, y), (x, 1 - y), (1 - x, 1 - y)]
    return 2 * x + y, (x, y, 1 - c), chips, [2 * cx + cy for cx, cy in chips]


def _chip_slab(ref, k, width, by_cols):
    if by_cols:
        start = k * width if isinstance(k, int) else pl.multiple_of(k * width, LANES)
        return ref.at[:, pl.ds(start, width)]
    return ref.at[k]


HBM = pl.BlockSpec(memory_space=pltpu.HBM)
SEM = pl.BlockSpec(memory_space=pltpu.SEMAPHORE)
DATAFLOW = pltpu.SideEffectType.DATAFLOW_SIDE_EFFECTING
N_GATHER_COPIES = 4


def _land_shape(src, by_cols):
    return src.shape[:-1] + (N_CHIPS * src.shape[-1],) if by_cols else (N_CHIPS,) + src.shape


def _gather_copy(src_ref, land_ref, by_cols, send_sems, recv_sems, pos, j, slab, to):
    width = src_ref.shape[-1]
    return pltpu.make_async_remote_copy(src_ref=src_ref, dst_ref=_chip_slab(land_ref, slab, width, by_cols),
                                        send_sem=send_sems.at[N_GATHER_COPIES * pos + j],
                                        recv_sem=recv_sems.at[N_GATHER_COPIES * pos + j],
                                        device_id=to, device_id_type=MESH)


def _gather_start(srcs, by_cols, groups, name):
    U = len(srcs)
    G = len(groups)
    lands = [lax.empty(_land_shape(s, bc), s.dtype) for s, bc in zip(srcs, by_cols)]

    def body(*refs):
        src = refs[:U]
        land = refs[U:2 * U]
        send_sems = refs[2 * U:2 * U + G]
        recv_sems = refs[2 * U + G:2 * U + 2 * G]
        token = refs[-1]
        c = lax.axis_index("c")
        me, sibling, chips, _ = _other_chips()
        targets = [(*chip, c) for chip in chips] + [sibling]
        for g, members in enumerate(groups):
            for pos, u in enumerate(members):
                for j, to in enumerate(targets):
                    _gather_copy(src[u], land[u], by_cols[u], send_sems[g], recv_sems[g], pos, j, me, to).start()
        token[...] = jnp.zeros_like(token)

    sem_shapes = [pltpu.SemaphoreType.DMA((len(m) * N_GATHER_COPIES,)) for m in groups]
    outs = pl.pallas_call(
        body, name=name,
        out_shape=tuple(sem_shapes + sem_shapes + [pltpu.HBM(s.shape, s.dtype) for s in srcs]
                        + [pltpu.HBM(v.shape, v.dtype) for v in lands] + [jax.ShapeDtypeStruct((SUBLANES, LANES), F32)]),
        in_specs=[HBM] * (2 * U),
        out_specs=tuple([SEM] * (2 * G) + [HBM] * (2 * U) + [pl.BlockSpec(memory_space=pltpu.VMEM)]),
        input_output_aliases={i: 2 * G + i for i in range(2 * U)},
        compiler_params=pltpu.CompilerParams(has_side_effects=DATAFLOW),
    )(*[pltpu.with_memory_space_constraint(a, pltpu.HBM) for a in list(srcs) + lands])
    return outs[:G], outs[G:2 * G], outs[2 * G:2 * G + U], outs[2 * G + U:2 * G + 2 * U]


def _gather_wait(srcs, lands, by_cols, send_sems, recv_sems, after, name):
    n = len(srcs)

    def body(*refs):
        src = refs[:n]
        land = refs[n:2 * n]
        send_ref, recv_ref = refs[2 * n:2 * n + 2]
        _, sibling, _, _ = _other_chips()
        for pos in range(n):
            for j in range(N_GATHER_COPIES):
                cp = _gather_copy(src[pos], land[pos], by_cols[pos], send_ref, recv_ref, pos, j, 0, sibling)
                cp.wait_send()
                cp.wait_recv()

    outs = pl.pallas_call(
        body, name=name,
        out_shape=tuple([pltpu.HBM(s.shape, s.dtype) for s in srcs] + [pltpu.HBM(v.shape, v.dtype) for v in lands]),
        in_specs=[HBM] * (2 * n) + [SEM, SEM, pl.BlockSpec(memory_space=pl.ANY)],
        out_specs=tuple([HBM] * (2 * n)),
        input_output_aliases={i: i for i in range(2 * n)},
        compiler_params=pltpu.CompilerParams(has_side_effects=DATAFLOW),
    )(*srcs, *lands, send_sems, recv_sems, after)
    return outs[n:]


def _scatter_grads(csums, by_cols, name):
    n = len(csums)
    shard = [(s.shape[0], s.shape[1] // N_CHIPS) if bc else s.shape[1:] for s, bc in zip(csums, by_cols)]

    def body(*refs):
        src = refs[:n]
        rem = refs[n:2 * n]
        sib = refs[2 * n:3 * n]
        send_sems, recv_sems = refs[3 * n:]
        c = lax.axis_index("c")
        me, sibling, chips, chip_ids = _other_chips()

        def remote(i, k, src_ref, dst_ref, to):
            return pltpu.make_async_remote_copy(src_ref=src_ref, dst_ref=dst_ref, send_sem=send_sems.at[i, k],
                                                recv_sem=recv_sems.at[i, k], device_id=to, device_id_type=MESH)

        def part(i, k):
            return _chip_slab(src[i], k, shard[i][-1], by_cols[i])

        started = []
        for i in range(n):
            for j in range(3):
                started.append(remote(i, j, part(i, chip_ids[j]), rem[i].at[j, c], (*chips[j], c)))
            started.append(remote(i, 6, part(i, me), sib[i], sibling))
        for cp in started:
            cp.start()
        for i in range(n):
            for j in range(3):
                slot = rem[i].at[j, c]
                remote(i, j, slot, slot, sibling).wait_recv()
                fwd = remote(i, 3 + j, slot, slot, sibling)
                fwd.start()
                started.append(fwd)
        for i in range(n):
            for j in range(3):
                slot = rem[i].at[j, 1 - c]
                remote(i, 3 + j, slot, slot, sibling).wait_recv()
            remote(i, 6, sib[i], sib[i], sibling).wait_recv()
        for cp in started:
            cp.wait_send()

    out_shape = ([jax.ShapeDtypeStruct((N_CHIPS - 1, 2) + tuple(sh), s.dtype) for s, sh in zip(csums, shard)]
                 + [jax.ShapeDtypeStruct(tuple(sh), s.dtype) for s, sh in zip(csums, shard)])
    outs = _streamed_call(
        body, name=name,
        in_specs=[ANY] * n, out_specs=[ANY] * (2 * n), out_shape=out_shape,
        scratch_shapes=[pltpu.SemaphoreType.DMA((n, 7)), pltpu.SemaphoreType.DMA((n, 7))],
    )(*csums)
    return outs[:n], outs[n:]


def _sibling_exchange(views, by_cols, name):
    n = len(views)

    def body(*refs):
        src = refs[:n]
        theirs = refs[n:2 * n]
        send_sems, recv_sems = refs[2 * n:]
        x, y, c = _mesh_position()
        copies = []
        for i in range(n):
            half = src[i].at[1 - c] if by_cols[i] else src[i].at[:, 1 - c]
            copies.append(pltpu.make_async_remote_copy(src_ref=half, dst_ref=theirs[i], send_sem=send_sems.at[i],
                                                       recv_sem=recv_sems.at[i], device_id=(x, y, 1 - c), device_id_type=MESH))
        for cp in copies:
            cp.start()
        for cp in copies:
            cp.wait()

    out_shape = [jax.ShapeDtypeStruct(v.shape[1:] if bc else v.shape[:1] + v.shape[2:], v.dtype) for v, bc in zip(views, by_cols)]
    return _streamed_call(
        body, name=name,
        in_specs=[ANY] * n, out_specs=[ANY] * n, out_shape=out_shape,
        scratch_shapes=[pltpu.SemaphoreType.DMA((n,)), pltpu.SemaphoreType.DMA((n,))],
    )(*views)


N_SCATTER_COPIES = 7


def _scatter_start(csums, by_cols, name):
    n = len(csums)
    shard = [(s.shape[0], s.shape[1] // N_CHIPS) if bc else s.shape[1:] for s, bc in zip(csums, by_cols)]
    rems = [lax.empty((N_CHIPS - 1, 2) + tuple(sh), s.dtype) for s, sh in zip(csums, shard)]
    sibs = [lax.empty(tuple(sh), s.dtype) for s, sh in zip(csums, shard)]

    def body(*refs):
        src = refs[:n]
        rem = refs[n:2 * n]
        sib = refs[2 * n:3 * n]
        send_sems, recv_sems = refs[3 * n:3 * n + 2]
        token = refs[-1]
        c = lax.axis_index("c")
        me, sibling, chips, chip_ids = _other_chips()
        for i in range(n):
            base = N_SCATTER_COPIES * i
            for j in range(3):
                part = _chip_slab(src[i], chip_ids[j], shard[i][-1], by_cols[i])
                for core in range(2):
                    pltpu.make_async_remote_copy(src_ref=part, dst_ref=rem[i].at[j, c], send_sem=send_sems.at[base + 2 * j + core],
                                                 recv_sem=recv_sems.at[base + 2 * j + c], device_id=(*chips[j], core),
                                                 device_id_type=MESH).start()
            pltpu.make_async_remote_copy(src_ref=_chip_slab(src[i], me, shard[i][-1], by_cols[i]), dst_ref=sib[i],
                                         send_sem=send_sems.at[base + 6], recv_sem=recv_sems.at[base + 6], device_id=sibling,
                                         device_id_type=MESH).start()
        token[...] = jnp.zeros_like(token)

    sems = pltpu.SemaphoreType.DMA((N_SCATTER_COPIES * n,))
    operands = list(csums) + rems + sibs
    outs = pl.pallas_call(
        body, name=name,
        out_shape=tuple([sems, sems] + [pltpu.HBM(a.shape, a.dtype) for a in operands] + [jax.ShapeDtypeStruct((SUBLANES, LANES), F32)]),
        in_specs=[HBM] * (3 * n),
        out_specs=tuple([SEM, SEM] + [HBM] * (3 * n) + [pl.BlockSpec(memory_space=pltpu.VMEM)]),
        input_output_aliases={i: 2 + i for i in range(3 * n)},
        compiler_params=pltpu.CompilerParams(has_side_effects=DATAFLOW),
    )(*[pltpu.with_memory_space_constraint(a, pltpu.HBM) for a in operands])
    return (outs[0], outs[1], outs[2:2 + n], outs[2 + n:2 + 2 * n], outs[2 + 2 * n:2 + 3 * n]), outs[-1]


def _scatter_wait(send_sems, recv_sems, srcs, rems, sibs, by_cols, after, name):
    n = len(srcs)

    def body(*refs):
        src = refs[:n]
        rem = refs[n:2 * n]
        sib = refs[2 * n:3 * n]
        send_ref, recv_ref = refs[3 * n:3 * n + 2]
        _, sibling, _, _ = _other_chips()
        for i in range(n):
            base = N_SCATTER_COPIES * i
            width = sib[i].shape[-1]
            for j in range(3):
                for core in range(2):
                    cp = pltpu.make_async_remote_copy(src_ref=_chip_slab(src[i], 0, width, by_cols[i]), dst_ref=rem[i].at[j, core],
                                                      send_sem=send_ref.at[base + 2 * j + core],
                                                      recv_sem=recv_ref.at[base + 2 * j + core], device_id=sibling,
                                                      device_id_type=MESH)
                    cp.wait_send()
                    cp.wait_recv()
            cp = pltpu.make_async_remote_copy(src_ref=_chip_slab(src[i], 0, width, by_cols[i]), dst_ref=sib[i],
                                              send_sem=send_ref.at[base + 6], recv_sem=recv_ref.at[base + 6], device_id=sibling,
                                              device_id_type=MESH)
            cp.wait_send()
            cp.wait_recv()

    operands = list(srcs) + list(rems) + list(sibs)
    outs = pl.pallas_call(
        body, name=name,
        out_shape=tuple(pltpu.HBM(a.shape, a.dtype) for a in operands),
        in_specs=[HBM] * (3 * n) + [SEM, SEM, pl.BlockSpec(memory_space=pl.ANY)],
        out_specs=tuple([HBM] * (3 * n)),
        input_output_aliases={i: i for i in range(3 * n)},
        compiler_params=pltpu.CompilerParams(has_side_effects=DATAFLOW),
    )(*operands, send_sems, recv_sems, after)
    return outs[:n], outs[n:2 * n], outs[2 * n:3 * n]


def _all_reduce_small(v, name):
    _, rows, _ = v.shape

    def body(v_ref, o_ref, recv_ref, send_sems, recv_sems):
        x, y, c = _mesh_position()
        me = 4 * x + 2 * y + c
        peers = []
        for d in range(1, N_DEV):
            px, py, pc = x ^ ((d >> 2) & 1), y ^ ((d >> 1) & 1), c ^ (d & 1)
            peers.append(((px, py, pc), 4 * px + 2 * py + pc))

        def remote(k, src_ref, dst_ref, to):
            return pltpu.make_async_remote_copy(src_ref=src_ref, dst_ref=dst_ref, send_sem=send_sems.at[k],
                                                recv_sem=recv_sems.at[k], device_id=to, device_id_type=MESH)

        scatter = [remote(d, v_ref.at[pid], recv_ref.at[me], to) for d, (to, pid) in enumerate(peers)]
        for cp in scatter:
            cp.start()
        recv_ref[pl.ds(me, 1)] = v_ref[pl.ds(me, 1)]
        for d, (to, pid) in enumerate(peers):
            remote(d, v_ref.at[pid], recv_ref.at[pid], to).wait_recv()
        total = recv_ref[0]
        for s in range(1, N_DEV):
            total = total + recv_ref[s]
        o_ref[pl.ds(me, 1)] = total[None]
        gather = [remote(N_DEV - 1 + d, o_ref.at[me], o_ref.at[me], to) for d, (to, pid) in enumerate(peers)]
        for cp in gather:
            cp.start()
        for d, (to, pid) in enumerate(peers):
            remote(N_DEV - 1 + d, o_ref.at[pid], o_ref.at[pid], to).wait_recv()
        for cp in scatter + gather:
            cp.wait_send()

    vm = pl.BlockSpec(memory_space=pltpu.VMEM)
    return pl.pallas_call(
        body, name=name,
        in_specs=[vm], out_specs=vm, out_shape=jax.ShapeDtypeStruct(v.shape, F32),
        scratch_shapes=[pltpu.VMEM(v.shape, F32), pltpu.SemaphoreType.DMA((2 * (N_DEV - 1),)),
                        pltpu.SemaphoreType.DMA((2 * (N_DEV - 1),))],
        compiler_params=_cparams(4 * _nbytes(v.shape, F32)),
    )(v)


SHARDED_MATS = ("ffn1_w_gu", "ffn1_w_down", "mix_w_in", "conv_w_proj", "rnn_w_proj", "mix_w_out", "ffn2_w_gu", "ffn2_w_down")
COL_SHARDED = ("ffn1_w_gu", "ffn2_w_gu", "conv_dw_w")
SHARDED_VECS = ("conv_dw_w", "rnn_conv_w")
WEIGHT_NAMES = ("ffn1_w_gu", "ffn1_w_down", "ln1_g", "ln1_b", "mix_w_in", "mix_b_in", "conv_dw_w", "conv_dw_b", "conv_gn_g",
                "conv_gn_b", "conv_w_proj", "rnn_conv_w", "rnn_conv_b", "rnn_w_a", "rnn_b_a", "rnn_w_x", "rnn_b_x",
                "rnn_lambda", "rnn_w_proj", "mix_w_out", "ln2_g", "ln2_b", "ffn2_w_gu", "ffn2_w_down", "ln3_g", "ln3_b")
SMALL_NAMES = tuple(n for n in WEIGHT_NAMES if n not in SHARDED_MATS)
SECTION_NAMES = ("cv", "cg", "rx", "rg", "gc", "gr")


def _unshard_cols(gathered):
    k4, K, n = gathered.shape
    return jnp.transpose(gathered, (1, 0, 2)).reshape(K, k4 * n)


def _row(v):
    return v.reshape(1, -1)


def _layer_forward(x0, p, alpha, l, hooks):
    t = f"l{l}_"
    sv = {"x0": x0}
    x1, sv["z1"], sv["hg1"], sv["hu1"], sv["x0b"] = _ffn_fwd(x0, p["wgu1"], p["wd1"], p["ln1_g"], p["ln1_b"], alpha, t + "ffn1_fwd")
    sv["x1"] = x1
    hooks.get("after_ffn1", lambda v: None)(x1)
    sec = dict(zip(SECTION_NAMES, _mix_in(x1, p["win"], p["bin"], p["sections"], t + "mix_in")))
    sv.update(sec)
    sv["c2"], c4 = _conv_branch_fwd(sec["cv"], sec["cg"], p["conv_dw_w"], p["conv_dw_b"], p["conv_gn_g"], p["conv_gn_b"], t + "conv_fwd")
    sv["c4"] = c4
    r1 = _short_conv_fwd(sec["rx"], p["rnn_conv_w"], p["rnn_conv_b"], t + "rconv_fwd")
    sv["r1"] = r1
    sv["ra"], sv["ri"], a, uu = _gates_fwd(r1, p["wa"], p["wx"], p["rnn_b_a"], p["rnn_b_x"], p["rnn_lambda"], t + "gates_fwd")
    sv["a"] = a
    h = _scan_fwd(a, uu, t + "scan_fwd")
    sv["h"] = h
    hooks.get("after_scan", lambda v: None)(h)
    sv["yc"], sv["yr"], sv["z2"], x2 = _mixer_out_fwd(c4, h, sec["rg"], sec["gc"], sec["gr"], x1, p["wc"], p["wr"], p["wo"],
                                                      p["ln2_g"], p["ln2_b"], alpha, t + "mixout_fwd")
    sv["x2"] = x2
    hooks.get("after_mixer", lambda v: None)(x2)
    x3, sv["z3"], sv["hg2"], sv["hu2"], sv["x2b"] = _ffn_fwd(x2, p["wgu2"], p["wd2"], p["ln3_g"], p["ln3_b"], alpha, t + "ffn2_fwd")
    hooks.get("after_layer", lambda v: None)(x3)
    return x3, sv


def _layer_backward(dy, p, sv, alpha, l, before_ffn1=None):
    t = f"l{l}_"
    g, gb = {}, {}
    dx2, df, a_act, dhg, dhu, g["ln3_g"], g["ln3_b"] = _ffn_bwd(dy, sv["z3"], sv["hg2"], sv["hu2"], p["wgu2"], p["wd2"],
                                                                 p["ln3_g"], alpha, t + "ffn2_bwd")
    g["ffn2_w_down"], gb["ffn2_w_down"] = _mm_tn(a_act, df, t + "dwd2")
    g["ffn2_w_gu"], gb["ffn2_w_gu"] = _mm_tn_pair(sv["x2b"], dhg, dhu, t + "dwgu2")
    (dz2, dz2b, m_b, dyc, dyr, dgc, dgr, s_gc, s_gr, g["ln2_g"], g["ln2_b"]) = _mixer_out_bwd(
        dx2, sv["z2"], p["ln2_g"], p["wo"], sv["yc"], sv["yr"], sv["gc"], sv["gr"], t + "mixout_bwd")
    g["mix_w_out"], gb["mix_w_out"] = _mm_tn(m_b, dz2b, t + "dwo")
    dc4, dh, drg, q_b, s_rg = _branch_bwd(dyc, dyr, p["wc"], p["wr"], sv["h"], sv["rg"], t + "branch_bwd")
    g["conv_w_proj"], gb["conv_w_proj"] = _mm_tn(sv["c4"], dyc, t + "dwc")
    g["rnn_w_proj"], gb["rnn_w_proj"] = _mm_tn(q_b, dyr, t + "dwr")
    (dcv, dcg, g["conv_dw_w"], g["conv_dw_b"], g["conv_gn_g"], g["conv_gn_b"], s_cv, s_cg) = _conv_branch_bwd(
        dc4, sv["c2"], sv["cv"], sv["cg"], p["conv_dw_w"], p["conv_gn_g"], p["conv_gn_b"], t + "conv_bwd")
    guu, da = _scan_bwd(sv["a"], dh, sv["h"], t + "scan_bwd")
    dr1, dpa, dpx, g["rnn_b_a"], g["rnn_b_x"], g["rnn_lambda"] = _gates_bwd(
        guu, da, sv["ra"], sv["ri"], sv["r1"], p["wa"], p["wx"], p["rnn_lambda"], t + "gates_bwd")
    g["rnn_w_a"], g["rnn_w_x"] = _block_grads(sv["r1"], dpa, dpx, t + "dwax")
    drx, g["rnn_conv_w"], g["rnn_conv_b"], s_rx = _short_conv_bwd(dr1, sv["rx"], p["rnn_conv_w"], t + "rconv_bwd")
    du = {"cv": dcv, "cg": dcg, "rx": drx, "rg": drg, "gc": dgc, "gr": dgr}
    order = ("cv", "cg", "rx", "rg", "gc", "gr")
    pieces = [_mm_tn(du[s], sv["x1"], t + "dwin_" + s) for s in order]
    g["mix_w_in"] = jnp.concatenate([f for f, _ in pieces], axis=0)
    gb["mix_w_in"] = jnp.concatenate([h for _, h in pieces], axis=0)
    g["mix_b_in"] = jnp.concatenate([s_cv, s_cg, s_rx, s_rg, s_gc, s_gr], axis=1)
    dx1 = _mix_dx(dz2, [du[s] for s in order], p["win"], p["sections"], alpha, t + "mix_dx")
    ln1_g = p["ln1_g"] if before_ffn1 is None else p["ln1_g"] + before_ffn1(g, gb)[0:1, 0:1]
    dx0, df, a_act, dhg, dhu, g["ln1_g"], g["ln1_b"] = _ffn_bwd(dx1, sv["z1"], sv["hg1"], sv["hu1"], p["wgu1"], p["wd1"],
                                                                 ln1_g, alpha, t + "ffn1_bwd")
    g["ffn1_w_down"], gb["ffn1_w_down"] = _mm_tn(a_act, df, t + "dwd1")
    g["ffn1_w_gu"], gb["ffn1_w_gu"] = _mm_tn_pair(sv["x0b"], dhg, dhu, t + "dwgu1")
    return dx0, g, gb


def _pack_small(arrays, piece_rows):
    flat = jnp.concatenate([a.reshape(-1) for a in arrays])
    total = N_DEV * piece_rows * LANES
    return jnp.pad(flat, (0, total - flat.shape[0])).reshape(N_DEV, piece_rows, LANES)


def _unpack_small(packed, shapes):
    flat = packed.reshape(-1)
    out, off = [], 0
    for shp in shapes:
        n = 1
        for s in shp:
            n *= s
        out.append(flat[off:off + n].reshape(shp))
        off += n
    return out


def kernel(x, ffn1_w_gu, ffn1_w_down, ln1_g, ln1_b, mix_w_in, mix_b_in, conv_dw_w, conv_dw_b, conv_gn_g, conv_gn_b, conv_w_proj, rnn_conv_w, rnn_conv_b, rnn_w_a, rnn_b_a, rnn_w_x, rnn_b_x, rnn_lambda, rnn_w_proj, mix_w_out, ln2_g, ln2_b, ffn2_w_gu, ffn2_w_down, ln3_g, ln3_b, loss_target, m_ffn1_w_gu, m_ffn1_w_down, m_ln1_g, m_ln1_b, m_mix_w_in, m_mix_b_in, m_conv_dw_w, m_conv_dw_b, m_conv_gn_g, m_conv_gn_b, m_conv_w_proj, m_rnn_conv_w, m_rnn_conv_b, m_rnn_w_a, m_rnn_b_a, m_rnn_w_x, m_rnn_b_x, m_rnn_lambda, m_rnn_w_proj, m_mix_w_out, m_ln2_g, m_ln2_b, m_ffn2_w_gu, m_ffn2_w_down, m_ln3_g, m_ln3_b, v_ffn1_w_gu, v_ffn1_w_down, v_ln1_g, v_ln1_b, v_mix_w_in, v_mix_b_in, v_conv_dw_w, v_conv_dw_b, v_conv_gn_g, v_conv_gn_b, v_conv_w_proj, v_rnn_conv_w, v_rnn_conv_b, v_rnn_w_a, v_rnn_b_a, v_rnn_w_x, v_rnn_b_x, v_rnn_lambda, v_rnn_w_proj, v_mix_w_out, v_ln2_g, v_ln2_b, v_ffn2_w_gu, v_ffn2_w_down, v_ln3_g, v_ln3_b):
    args = locals()
    W = {n: args[n] for n in WEIGHT_NAMES}
    M = {n: args["m_" + n] for n in WEIGHT_NAMES}
    V = {n: args["v_" + n] for n in WEIGHT_NAMES}
    depth = ln1_g.shape[0]
    assert depth == 2, "each core of a chip moves one layer's weights and gradients"
    alpha = float((2 * depth) ** 0.25)
    S, D = x.shape[1], x.shape[2]
    F = ffn1_w_down.shape[1] * N_CHIPS
    R = rnn_w_proj.shape[1] * N_CHIPS
    chip = 2 * lax.axis_index("x") + lax.axis_index("y")

    for d in (W, M, V):
        d["mix_w_in"] = jnp.transpose(d["mix_w_in"], (0, 2, 1))

    names = SHARDED_MATS + SHARDED_VECS
    unit_groups = [[(0, "ffn1_w_gu"), (0, "ffn1_w_down")], [(0, "mix_w_in"), (0, "conv_dw_w"), (0, "rnn_conv_w")],
                   [(0, "conv_w_proj"), (0, "rnn_w_proj"), (0, "mix_w_out")], [(0, "ffn2_w_gu"), (0, "ffn2_w_down")],
                   [(1, n) for n in names]]
    order = [u for g in unit_groups for u in g]
    index = {u: i for i, u in enumerate(order)}
    groups = [[index[u] for u in g] for g in unit_groups]
    srcs = [W[n][l].astype(BF16) if n in SHARDED_MATS else W[n][l] for l, n in order]
    by_cols = [n in COL_SHARDED for _, n in order]
    send_sems, recv_sems, src_thru, land_thru = _gather_start(srcs, by_cols, groups, "gather_start")

    sections = ((0, D, F32), (D, D, F32), (2 * D, R, F32), (2 * D + R, R, BF16), (2 * D + 2 * R, D, BF16),
                (3 * D + 2 * R, D, BF16))
    keys = {"ffn1_w_gu": "wgu1", "ffn1_w_down": "wd1", "ffn2_w_gu": "wgu2", "ffn2_w_down": "wd2", "conv_w_proj": "wc",
            "rnn_w_proj": "wr", "mix_w_out": "wo"}
    params = []
    for l in range(depth):
        p = {"wa": _embed_blocks(rnn_w_a[l], f"l{l}_embed_wa"), "wx": _embed_blocks(rnn_w_x[l], f"l{l}_embed_wx")}
        for n in ("ln1_g", "ln1_b", "ln2_g", "ln2_b", "ln3_g", "ln3_b", "conv_dw_b", "conv_gn_g", "conv_gn_b", "rnn_conv_b",
                  "rnn_b_a", "rnn_b_x", "rnn_lambda"):
            p[n] = _row(W[n][l])
        p["bin"] = _row(mix_b_in[l])
        p["sections"] = sections
        params.append(p)

    def wait_group(g, after):
        ids = groups[g]
        landed = _gather_wait([src_thru[i] for i in ids], [land_thru[i] for i in ids], [by_cols[i] for i in ids],
                              send_sems[g], recv_sems[g], after, f"gather_wait{g}")
        for i, full in zip(ids, landed):
            l, n = order[i]
            p = params[l]
            if n not in COL_SHARDED:
                full = full.reshape((N_CHIPS * full.shape[1],) + full.shape[2:])
            if n == "mix_w_in":
                p["win"] = full
            elif n == "rnn_conv_w":
                p[n] = _unshard_cols(landed[ids.index(i)])
            elif n == "conv_dw_w":
                p[n] = full
            else:
                p[keys[n]] = (full[None], 0)

    h = x[0]
    wait_group(0, h)
    saved = []
    hooks = [{"after_ffn1": lambda v: wait_group(1, v), "after_scan": lambda v: wait_group(2, v),
              "after_mixer": lambda v: wait_group(3, v), "after_layer": lambda v: wait_group(4, v)}, {}]
    for l in range(depth):
        h, sv = _layer_forward(h, params[l], alpha, l, hooks[l])
        saved.append(sv)
    loss_part, dy = _loss_head(h, loss_target[0], "loss_head")
    loss = lax.psum(loss_part[0, 0], ("x", "y", "c"))
    def pair_sums(names, g, gb, tag):
        cols = [n in COL_SHARDED for n in names]
        theirs = _sibling_exchange([_half_view(gb[n], bc) for n, bc in zip(names, cols)], cols, "pair_exchange" + tag)
        return cols, [_pair_sum_bf16(_half_view(g[n], bc), t, bc, f"pair_sum{tag}_{n}") for n, bc, t in zip(names, cols, theirs)]

    pending = []

    def start_scatter(layer, names, g, gb, tag):
        cols, sums = pair_sums(names, g, gb, tag)
        in_flight, token = _scatter_start(sums, cols, "scatter_start" + tag)
        pending.append((layer, names, cols, in_flight, tag))
        return token

    early = [n for n in SHARDED_MATS if not n.startswith("ffn1_")]
    late = [n for n in SHARDED_MATS if n.startswith("ffn1_")]
    grads, grads_bf16 = [None] * depth, [None] * depth
    dy, grads[1], grads_bf16[1] = _layer_backward(dy, params[1], saved[1], alpha, 1)
    token = start_scatter(1, SHARDED_MATS, grads[1], grads_bf16[1], "1")
    first = dict(params[0], ln3_g=params[0]["ln3_g"] + token[0:1, 0:1])
    dy, grads[0], grads_bf16[0] = _layer_backward(dy, first, saved[0], alpha, 0,
                                                  before_ffn1=lambda g, gb: start_scatter(0, early, g, gb, "0a"))
    grad_x = dy[None]
    late_cols, late_sums = pair_sums(late, grads[0], grads_bf16[0], "0b")
    late_rem, late_sib = _scatter_grads(late_sums, late_cols, "scatter_grads0b")
    partial = {(0, n): part for n, part in zip(late, zip(late_sums, late_rem, late_sib))}
    for layer, names, cols, in_flight, tag in pending:
        for n, part in zip(names, zip(*_scatter_wait(*in_flight, cols, late_rem[0], "scatter_wait" + tag))):
            partial[(layer, n)] = part

    results = {}
    for l in (1, 0):
        for n in SHARDED_MATS:
            cs, rm, sb = partial[(l, n)]
            if n in COL_SHARDED:
                width = cs.shape[1] // N_CHIPS
                own = lax.dynamic_slice_in_dim(cs, chip * width, width, axis=1)
            else:
                own = lax.dynamic_index_in_dim(cs, chip, axis=0, keepdims=False)
            results[n] = _adamw_sharded(W[n], M[n], V[n], own, sb, rm, l, results.get(n), f"adamw{l}_{n}")
    out_g, out_d, out_m, out_v = {}, {}, {}, {}
    for n in SHARDED_MATS:
        outs = results[n]
        if n == "mix_w_in":
            outs = [jnp.transpose(o, (0, 2, 1)) for o in outs]
        out_g[n], out_d[n], out_m[n], out_v[n] = outs

    small_grads = [jnp.stack([grads[l][n].reshape(W[n].shape[1:] if n not in SHARDED_VECS else
                                                   (W[n].shape[1], W[n].shape[2] * N_CHIPS)) for l in range(depth)])
                   for n in SMALL_NAMES]
    n_small = sum(int(a.size) for a in small_grads)
    piece_rows = -(-n_small // (N_DEV * LANES * SUBLANES)) * SUBLANES
    reduced = _unpack_small(_all_reduce_small(_pack_small(small_grads, piece_rows), "all_reduce_small"),
                            [a.shape for a in small_grads])
    local_g = []
    for n, gr in zip(SMALL_NAMES, reduced):
        if n in SHARDED_VECS:
            width = W[n].shape[2]
            gr = lax.dynamic_slice_in_dim(gr, chip * width, width, axis=2)
        local_g.append(gr)
    deltas, new_m, new_v = _adamw_small([W[n] for n in SMALL_NAMES], [M[n] for n in SMALL_NAMES],
                                        [V[n] for n in SMALL_NAMES], local_g, "adamw_small")
    for n, gr, d_, m_, v_ in zip(SMALL_NAMES, local_g, deltas, new_m, new_v):
        out_g[n], out_d[n], out_m[n], out_v[n] = gr, d_, m_, v_

    return (loss, grad_x, *[out_g[n] for n in WEIGHT_NAMES], *[out_d[n] for n in WEIGHT_NAMES],
            *[out_m[n] for n in WEIGHT_NAMES], *[out_v[n] for n in WEIGHT_NAMES])
```
